```python
import jax, jax.numpy as jnp
from jax import lax
import numpy as np

D_MODEL = 1024
BATCH = 8
SEQ = 16384
DEPTH = 4

N_MIXERS = 2
N_ATTN_LAYERS = (DEPTH + 1) // 2
N_GDN_LAYERS = DEPTH // 2
D_FF = 2816
NORM_EPS = 1e-6

ATTN_Q_HEADS = 16
ATTN_KV_HEADS = 4
ATTN_HEAD_DIM = 64
ATTN_GROUP = ATTN_Q_HEADS // ATTN_KV_HEADS
WINDOW = 128
ATTN_BLOCK = 128
ROPE_DIM = ATTN_HEAD_DIM // 4
ROPE_THETA = 500000.0
ATTN_Q_W = ATTN_Q_HEADS * ATTN_HEAD_DIM
ATTN_KV_W = ATTN_KV_HEADS * ATTN_HEAD_DIM
ATTN_IN = ATTN_Q_W + 2 * ATTN_KV_W

GDN_HEADS = 8
GDN_DK = 128
GDN_DV = 128
GDN_CONV = 4
GDN_CHUNK = 64
GDN_QK_W = GDN_HEADS * GDN_DK
GDN_V_W = GDN_HEADS * GDN_DV
GDN_CONV_W = 2 * GDN_QK_W + GDN_V_W
GDN_IN = GDN_CONV_W + GDN_V_W + 2 * GDN_HEADS

kernel_name = "hybrid_swa_sink_gdn_macaron"


def rms_norm(x, w):
    xf = x.astype(jnp.float32)
    y = xf * lax.rsqrt(jnp.mean(xf * xf, axis=-1, keepdims=True) + NORM_EPS)
    return (y * w.astype(jnp.float32)).astype(x.dtype)


def swiglu_ffn(h, w_gate_up, w_down):
    gate, up = jnp.split(h @ w_gate_up, 2, axis=-1)
    return (jax.nn.silu(gate) * up) @ w_down


def partial_rope(t, cos, sin):
    tf = t.astype(jnp.float32)
    half = ROPE_DIM // 2
    r1, r2, rest = tf[..., :half], tf[..., half:ROPE_DIM], tf[..., ROPE_DIM:]
    out = jnp.concatenate([r1 * cos - r2 * sin, r2 * cos + r1 * sin, rest], axis=-1)
    return out.astype(t.dtype)


def swa_sink_attention(h, cos, sin, w_in, b_in, sinks, w_out, b_out):
    B, S, _ = h.shape
    nb = S // ATTN_BLOCK
    qkv = h @ w_in + b_in
    q, k, v = jnp.split(qkv, [ATTN_Q_W, ATTN_Q_W + ATTN_KV_W], axis=-1)
    q = partial_rope(q.reshape(B, S, ATTN_Q_HEADS, ATTN_HEAD_DIM), cos, sin)
    k = partial_rope(k.reshape(B, S, ATTN_KV_HEADS, ATTN_HEAD_DIM), cos, sin)
    v = v.reshape(B, S, ATTN_KV_HEADS, ATTN_HEAD_DIM)
    qb = q.reshape(B, nb, ATTN_BLOCK, ATTN_KV_HEADS, ATTN_GROUP, ATTN_HEAD_DIM)

    def band(t):
        cur = t.reshape(B, nb, ATTN_BLOCK, ATTN_KV_HEADS, ATTN_HEAD_DIM)
        prev = jnp.pad(cur, ((0, 0), (1, 0), (0, 0), (0, 0), (0, 0)))[:, :-1]
        return jnp.concatenate([prev, cur], axis=2)

    kb, vb = band(k), band(v)
    scores = jnp.einsum('bnqhgd,bnkhd->bnhgqk', qb, kb,
                        preferred_element_type=jnp.float32) * (ATTN_HEAD_DIM ** -0.5)
    qi = jnp.arange(ATTN_BLOCK)[:, None]
    kj = jnp.arange(2 * ATTN_BLOCK)[None, :]
    rel = qi + ATTN_BLOCK - kj
    in_window = (rel >= 0) & (rel < WINDOW)
    blk = jnp.arange(nb)[:, None, None]
    valid = in_window[None] & ((blk - 1) * ATTN_BLOCK + kj[None] >= 0)
    scores = jnp.where(valid[None, :, None, None], scores, -jnp.inf)
    sink = sinks.astype(jnp.float32).reshape(ATTN_KV_HEADS, ATTN_GROUP)[None, None, :, :, None, None]
    m = jnp.maximum(jnp.max(scores, axis=-1, keepdims=True), sink)
    p = jnp.exp(scores - m)
    probs = (p / (jnp.sum(p, axis=-1, keepdims=True) + jnp.exp(sink - m))).astype(v.dtype)
    o = jnp.einsum('bnhgqk,bnkhd->bnqhgd', probs, vb).reshape(B, S, ATTN_Q_W)
    return o @ w_out + b_out


def l2_normalize(t):
    return t * lax.rsqrt(jnp.sum(t * t, axis=-1, keepdims=True) + NORM_EPS)


def chunk_gated_delta_rule(q, k, v, g, beta):
    B, S, H, DK = q.shape
    DV = v.shape[-1]
    C = GDN_CHUNK
    N = S // C

    def to_chunks(t):
        return jnp.moveaxis(t.reshape(B, N, C, H, *t.shape[3:]), 3, 1)

    q, k, v, g, beta = (to_chunks(t) for t in (q, k, v, g, beta))
    decay = jnp.cumsum(g, axis=-1)
    causal = jnp.tril(jnp.ones((C, C), dtype=bool))
    strict = jnp.tril(jnp.ones((C, C), dtype=bool), -1)
    L = jnp.exp(jnp.where(causal, decay[..., :, None] - decay[..., None, :], -jnp.inf))
    k_beta = k * beta[..., None]
    A = jnp.where(strict, jnp.einsum('bhncd,bhnsd->bhncs', k_beta, k) * L, 0.0)
    eye = jnp.eye(C, dtype=jnp.float32)
    rhs = jnp.concatenate([v * beta[..., None], k_beta * jnp.exp(decay)[..., None]], axis=-1)
    sol = lax.linalg.triangular_solve(eye + A, rhs, left_side=True, lower=True)
    U, W = sol[..., :DV], sol[..., DV:]
    Aqk = jnp.where(causal, jnp.einsum('bhncd,bhnsd->bhncs', q, k) * L, 0.0)
    decay_last = decay[..., -1:]
    q_dec = q * jnp.exp(decay)[..., None]
    k_dec = k * jnp.exp(decay_last - decay)[..., None]
    chunk_decay = jnp.exp(decay_last[..., 0])
    xs = tuple(jnp.moveaxis(t, 2, 0) for t in (q_dec, k_dec, U, W, Aqk, chunk_decay))

    def step(state, inp):
        qd, kd, u, w, aqk, cd = inp
        v_new = u - jnp.einsum('bhcd,bhde->bhce', w, state)
        o = jnp.einsum('bhcd,bhde->bhce', qd, state) + jnp.einsum('bhcs,bhse->bhce', aqk, v_new)
        state = state * cd[..., None, None] + jnp.einsum('bhcd,bhce->bhde', kd, v_new)
        return state, o

    _, o = lax.scan(step, jnp.zeros((B, H, DK, DV), jnp.float32), xs)
    return jnp.transpose(o, (1, 0, 3, 2, 4)).reshape(B, S, H, DV)


def gated_deltanet(h, w_in, conv_w, A_log, dt_bias, norm_w, w_out):
    B, S, _ = h.shape
    proj = h @ w_in
    qkv, z, b, a = jnp.split(proj, [GDN_CONV_W, GDN_CONV_W + GDN_V_W, GDN_CONV_W + GDN_V_W + GDN_HEADS], axis=-1)
    qkv = lax.conv_general_dilated(qkv, conv_w[:, None, :].astype(qkv.dtype), window_strides=(1,),
                                   padding=[(GDN_CONV - 1, 0)], dimension_numbers=('NWC', 'WIO', 'NWC'),
                                   feature_group_count=GDN_CONV_W)
    qkv = jax.nn.silu(qkv).astype(jnp.float32)
    q, k, v = jnp.split(qkv, [GDN_QK_W, 2 * GDN_QK_W], axis=-1)
    q = l2_normalize(q.reshape(B, S, GDN_HEADS, GDN_DK)) * (GDN_DK ** -0.5)
    k = l2_normalize(k.reshape(B, S, GDN_HEADS, GDN_DK))
    v = v.reshape(B, S, GDN_HEADS, GDN_DV)
    beta = jax.nn.sigmoid(b.astype(jnp.float32))
    g = -jnp.exp(A_log.astype(jnp.float32)) * jax.nn.softplus(a.astype(jnp.float32) + dt_bias.astype(jnp.float32))
    o = chunk_gated_delta_rule(q, k, v, g, beta)
    zf = z.reshape(B, S, GDN_HEADS, GDN_DV).astype(jnp.float32)
    o = o * lax.rsqrt(jnp.mean(o * o, axis=-1, keepdims=True) + NORM_EPS) * norm_w.astype(jnp.float32) * jax.nn.silu(zf)
    return o.astype(h.dtype).reshape(B, S, GDN_V_W) @ w_out


def _fwd_setup_inputs(seed: int = 0) -> dict:
    key = jax.random.key(seed)
    ks = jax.random.split(key, 24)
    f32 = jnp.float32

    def nrm(k, shape, fan_in):
        return jax.random.normal(k, shape, f32) * (fan_in ** -0.5)

    def gain(k, shape):
        return 1.0 + 0.05 * jax.random.normal(k, shape, f32)

    x = jax.random.normal(ks[0], (BATCH, SEQ, D_MODEL), f32)
    positions = (jnp.arange(SEQ, dtype=jnp.int32)[None, :]
                 + jax.random.randint(ks[1], (BATCH, 1), 0, 4096, dtype=jnp.int32))
    dt = jnp.exp(jax.random.uniform(ks[17], (N_GDN_LAYERS, GDN_HEADS), f32, np.log(1e-3), np.log(1e-1)))
    return {
        "x": x,
        "positions": positions,
        "ffn1_norm": gain(ks[2], (DEPTH, D_MODEL)),
        "ffn1_w_gate_up": nrm(ks[3], (DEPTH, D_MODEL, 2 * D_FF), D_MODEL),
        "ffn1_w_down": nrm(ks[4], (DEPTH, D_FF, D_MODEL), D_FF),
        "mix_norm": gain(ks[5], (DEPTH, D_MODEL)),
        "ffn2_norm": gain(ks[6], (DEPTH, D_MODEL)),
        "ffn2_w_gate_up": nrm(ks[7], (DEPTH, D_MODEL, 2 * D_FF), D_MODEL),
        "ffn2_w_down": nrm(ks[8], (DEPTH, D_FF, D_MODEL), D_FF),
        "attn_w_in": nrm(ks[9], (N_ATTN_LAYERS, D_MODEL, ATTN_IN), D_MODEL),
        "attn_b_in": 0.02 * jax.random.normal(ks[10], (N_ATTN_LAYERS, ATTN_IN), f32),
        "attn_sinks": jax.random.normal(ks[11], (N_ATTN_LAYERS, ATTN_Q_HEADS), f32),
        "attn_w_out": nrm(ks[12], (N_ATTN_LAYERS, ATTN_Q_W, D_MODEL), ATTN_Q_W),
        "attn_b_out": 0.02 * jax.random.normal(ks[13], (N_ATTN_LAYERS, D_MODEL), f32),
        "gdn_w_in": nrm(ks[14], (N_GDN_LAYERS, D_MODEL, GDN_IN), D_MODEL),
        "gdn_conv_w": nrm(ks[15], (N_GDN_LAYERS, GDN_CONV, GDN_CONV_W), GDN_CONV),
        "gdn_A_log": jnp.log(jax.random.uniform(ks[16], (N_GDN_LAYERS, GDN_HEADS), f32, 1.0, 16.0)),
        "gdn_dt_bias": dt + jnp.log(-jnp.expm1(-dt)),
        "gdn_norm_w": gain(ks[18], (N_GDN_LAYERS, GDN_DV)),
        "gdn_w_out": nrm(ks[19], (N_GDN_LAYERS, GDN_V_W, D_MODEL), GDN_V_W),
        "final_norm": gain(ks[20], (D_MODEL,)),
    }


def _fwd_reference(x, positions, ffn1_norm, ffn1_w_gate_up, ffn1_w_down, mix_norm, ffn2_norm,
              ffn2_w_gate_up, ffn2_w_down, attn_w_in, attn_b_in, attn_sinks, attn_w_out,
              attn_b_out, gdn_w_in, gdn_conv_w, gdn_A_log, gdn_dt_bias, gdn_norm_w, gdn_w_out,
              final_norm):
    inv_freq = ROPE_THETA ** (-jnp.arange(0, ROPE_DIM, 2, dtype=jnp.float32) / ROPE_DIM)
    ang = positions.astype(jnp.float32)[..., None] * inv_freq
    cos, sin = jnp.cos(ang)[:, :, None, :], jnp.sin(ang)[:, :, None, :]
    h = x
    for layer in range(DEPTH):
        h = h + 0.5 * swiglu_ffn(rms_norm(h, ffn1_norm[layer]), ffn1_w_gate_up[layer], ffn1_w_down[layer])
        hn = rms_norm(h, mix_norm[layer])
        j = layer // N_MIXERS
        if layer % N_MIXERS == 0:
            h = h + swa_sink_attention(hn, cos, sin, attn_w_in[j], attn_b_in[j], attn_sinks[j],
                                       attn_w_out[j], attn_b_out[j])
        else:
            h = h + gated_deltanet(hn, gdn_w_in[j], gdn_conv_w[j], gdn_A_log[j], gdn_dt_bias[j],
                                   gdn_norm_w[j], gdn_w_out[j])
        h = h + 0.5 * swiglu_ffn(rms_norm(h, ffn2_norm[layer]), ffn2_w_gate_up[layer], ffn2_w_down[layer])
    return rms_norm(h, final_norm)


import jax as _jax
import jax.numpy as _jnp

TWIN_FORMAT = 'train_step'
FWD_PARAMS = ['x', 'positions', 'ffn1_norm', 'ffn1_w_gate_up', 'ffn1_w_down', 'mix_norm', 'ffn2_norm', 'ffn2_w_gate_up', 'ffn2_w_down', 'attn_w_in', 'attn_b_in', 'attn_sinks', 'attn_w_out', 'attn_b_out', 'gdn_w_in', 'gdn_conv_w', 'gdn_A_log', 'gdn_dt_bias', 'gdn_norm_w', 'gdn_w_out', 'final_norm']
TWIN_WEIGHTS = ['ffn1_norm', 'ffn1_w_gate_up', 'ffn1_w_down', 'mix_norm', 'ffn2_norm', 'ffn2_w_gate_up', 'ffn2_w_down', 'attn_w_in', 'attn_b_in', 'attn_sinks', 'attn_w_out', 'attn_b_out', 'gdn_w_in', 'gdn_conv_w', 'gdn_A_log', 'gdn_dt_bias', 'gdn_norm_w', 'gdn_w_out', 'final_norm']
TWIN_DIFF_INPUT = 'x'
TWIN_INPUTS = ['x', 'positions', 'ffn1_norm', 'ffn1_w_gate_up', 'ffn1_w_down', 'mix_norm', 'ffn2_norm', 'ffn2_w_gate_up', 'ffn2_w_down', 'attn_w_in', 'attn_b_in', 'attn_sinks', 'attn_w_out', 'attn_b_out', 'gdn_w_in', 'gdn_conv_w', 'gdn_A_log', 'gdn_dt_bias', 'gdn_norm_w', 'gdn_w_out', 'final_norm', 'loss_target', 'm_ffn1_norm', 'm_ffn1_w_gate_up', 'm_ffn1_w_down', 'm_mix_norm', 'm_ffn2_norm', 'm_ffn2_w_gate_up', 'm_ffn2_w_down', 'm_attn_w_in', 'm_attn_b_in', 'm_attn_sinks', 'm_attn_w_out', 'm_attn_b_out', 'm_gdn_w_in', 'm_gdn_conv_w', 'm_gdn_A_log', 'm_gdn_dt_bias', 'm_gdn_norm_w', 'm_gdn_w_out', 'm_final_norm', 'v_ffn1_norm', 'v_ffn1_w_gate_up', 'v_ffn1_w_down', 'v_mix_norm', 'v_ffn2_norm', 'v_ffn2_w_gate_up', 'v_ffn2_w_down', 'v_attn_w_in', 'v_attn_b_in', 'v_attn_sinks', 'v_attn_w_out', 'v_attn_b_out', 'v_gdn_w_in', 'v_gdn_conv_w', 'v_gdn_A_log', 'v_gdn_dt_bias', 'v_gdn_norm_w', 'v_gdn_w_out', 'v_final_norm']
TWIN_OUTPUTS = ['loss', 'grad_x', 'grad_ffn1_norm', 'grad_ffn1_w_gate_up', 'grad_ffn1_w_down', 'grad_mix_norm', 'grad_ffn2_norm', 'grad_ffn2_w_gate_up', 'grad_ffn2_w_down', 'grad_attn_w_in', 'grad_attn_b_in', 'grad_attn_sinks', 'grad_attn_w_out', 'grad_attn_b_out', 'grad_gdn_w_in', 'grad_gdn_conv_w', 'grad_gdn_A_log', 'grad_gdn_dt_bias', 'grad_gdn_norm_w', 'grad_gdn_w_out', 'grad_final_norm', 'delta_ffn1_norm', 'delta_ffn1_w_gate_up', 'delta_ffn1_w_down', 'delta_mix_norm', 'delta_ffn2_norm', 'delta_ffn2_w_gate_up', 'delta_ffn2_w_down', 'delta_attn_w_in', 'delta_attn_b_in', 'delta_attn_sinks', 'delta_attn_w_out', 'delta_attn_b_out', 'delta_gdn_w_in', 'delta_gdn_conv_w', 'delta_gdn_A_log', 'delta_gdn_dt_bias', 'delta_gdn_norm_w', 'delta_gdn_w_out', 'delta_final_norm', 'new_m_ffn1_norm', 'new_m_ffn1_w_gate_up', 'new_m_ffn1_w_down', 'new_m_mix_norm', 'new_m_ffn2_norm', 'new_m_ffn2_w_gate_up', 'new_m_ffn2_w_down', 'new_m_attn_w_in', 'new_m_attn_b_in', 'new_m_attn_sinks', 'new_m_attn_w_out', 'new_m_attn_b_out', 'new_m_gdn_w_in', 'new_m_gdn_conv_w', 'new_m_gdn_A_log', 'new_m_gdn_dt_bias', 'new_m_gdn_norm_w', 'new_m_gdn_w_out', 'new_m_final_norm', 'new_v_ffn1_norm', 'new_v_ffn1_w_gate_up', 'new_v_ffn1_w_down', 'new_v_mix_norm', 'new_v_ffn2_norm', 'new_v_ffn2_w_gate_up', 'new_v_ffn2_w_down', 'new_v_attn_w_in', 'new_v_attn_b_in', 'new_v_attn_sinks', 'new_v_attn_w_out', 'new_v_attn_b_out', 'new_v_gdn_w_in', 'new_v_gdn_conv_w', 'new_v_gdn_A_log', 'new_v_gdn_dt_bias', 'new_v_gdn_norm_w', 'new_v_gdn_w_out', 'new_v_final_norm']
TWIN_LEAF_KINDS = {'loss': 'loss', 'grad_x': 'grad_x', 'grad_ffn1_norm': 'grad_w', 'grad_ffn1_w_gate_up': 'grad_w', 'grad_ffn1_w_down': 'grad_w', 'grad_mix_norm': 'grad_w', 'grad_ffn2_norm': 'grad_w', 'grad_ffn2_w_gate_up': 'grad_w', 'grad_ffn2_w_down': 'grad_w', 'grad_attn_w_in': 'grad_w', 'grad_attn_b_in': 'grad_w', 'grad_attn_sinks': 'grad_w', 'grad_attn_w_out': 'grad_w', 'grad_attn_b_out': 'grad_w', 'grad_gdn_w_in': 'grad_w', 'grad_gdn_conv_w': 'grad_w', 'grad_gdn_A_log': 'grad_w', 'grad_gdn_dt_bias': 'grad_w', 'grad_gdn_norm_w': 'grad_w', 'grad_gdn_w_out': 'grad_w', 'grad_final_norm': 'grad_w', 'delta_ffn1_norm': 'delta_w', 'delta_ffn1_w_gate_up': 'delta_w', 'delta_ffn1_w_down': 'delta_w', 'delta_mix_norm': 'delta_w', 'delta_ffn2_norm': 'delta_w', 'delta_ffn2_w_gate_up': 'delta_w', 'delta_ffn2_w_down': 'delta_w', 'delta_attn_w_in': 'delta_w', 'delta_attn_b_in': 'delta_w', 'delta_attn_sinks': 'delta_w', 'delta_attn_w_out': 'delta_w', 'delta_attn_b_out': 'delta_w', 'delta_gdn_w_in': 'delta_w', 'delta_gdn_conv_w': 'delta_w', 'delta_gdn_A_log': 'delta_w', 'delta_gdn_dt_bias': 'delta_w', 'delta_gdn_norm_w': 'delta_w', 'delta_gdn_w_out': 'delta_w', 'delta_final_norm': 'delta_w', 'new_m_ffn1_norm': 'new_m', 'new_m_ffn1_w_gate_up': 'new_m', 'new_m_ffn1_w_down': 'new_m', 'new_m_mix_norm': 'new_m', 'new_m_ffn2_norm': 'new_m', 'new_m_ffn2_w_gate_up': 'new_m', 'new_m_ffn2_w_down': 'new_m', 'new_m_attn_w_in': 'new_m', 'new_m_attn_b_in': 'new_m', 'new_m_attn_sinks': 'new_m', 'new_m_attn_w_out': 'new_m', 'new_m_attn_b_out': 'new_m', 'new_m_gdn_w_in': 'new_m', 'new_m_gdn_conv_w': 'new_m', 'new_m_gdn_A_log': 'new_m', 'new_m_gdn_dt_bias': 'new_m', 'new_m_gdn_norm_w': 'new_m', 'new_m_gdn_w_out': 'new_m', 'new_m_final_norm': 'new_m', 'new_v_ffn1_norm': 'new_v', 'new_v_ffn1_w_gate_up': 'new_v', 'new_v_ffn1_w_down': 'new_v', 'new_v_mix_norm': 'new_v', 'new_v_ffn2_norm': 'new_v', 'new_v_ffn2_w_gate_up': 'new_v', 'new_v_ffn2_w_down': 'new_v', 'new_v_attn_w_in': 'new_v', 'new_v_attn_b_in': 'new_v', 'new_v_attn_sinks': 'new_v', 'new_v_attn_w_out': 'new_v', 'new_v_attn_b_out': 'new_v', 'new_v_gdn_w_in': 'new_v', 'new_v_gdn_conv_w': 'new_v', 'new_v_gdn_A_log': 'new_v', 'new_v_gdn_dt_bias': 'new_v', 'new_v_gdn_norm_w': 'new_v', 'new_v_gdn_w_out': 'new_v', 'new_v_final_norm': 'new_v'}


def _forward(args):
    return _fwd_reference(*[args[k] for k in FWD_PARAMS])


def _output_shape():
    def fwd():
        inp = _fwd_setup_inputs(0)
        return _fwd_reference(*[inp[k] for k in FWD_PARAMS])
    out = _jax.eval_shape(fwd)
    return out.shape, out.dtype

N_MICROBATCH = 1
ADAM_LR = 0.001
ADAM_B1 = 0.9
ADAM_B2 = 0.999
ADAM_EPS = 1e-08
ADAM_WD = 0.01
ADAM_STEP = 10
PER_EXAMPLE_BATCH_AXIS = {'x': 0, 'positions': 0, 'loss_target': 0}
SHARED_INPUTS = []
_WEIGHT_DTYPES = {'ffn1_norm': _jnp.float32, 'ffn1_w_gate_up': _jnp.float32, 'ffn1_w_down': _jnp.float32, 'mix_norm': _jnp.float32, 'ffn2_norm': _jnp.float32, 'ffn2_w_gate_up': _jnp.float32, 'ffn2_w_down': _jnp.float32, 'attn_w_in': _jnp.float32, 'attn_b_in': _jnp.float32, 'attn_sinks': _jnp.float32, 'attn_w_out': _jnp.float32, 'attn_b_out': _jnp.float32, 'gdn_w_in': _jnp.float32, 'gdn_conv_w': _jnp.float32, 'gdn_A_log': _jnp.float32, 'gdn_dt_bias': _jnp.float32, 'gdn_norm_w': _jnp.float32, 'gdn_w_out': _jnp.float32, 'final_norm': _jnp.float32}
MOMENT_SCALE = {'ffn1_norm': 1.853037e-01, 'ffn1_w_gate_up': 7.867997e-02, 'ffn1_w_down': 1.286935e-01, 'mix_norm': 2.357823e-01, 'ffn2_norm': 1.545039e-01, 'ffn2_w_gate_up': 6.692569e-02, 'ffn2_w_down': 1.096750e-01, 'attn_w_in': 1.241596e-01, 'attn_b_in': 6.293406e-01, 'attn_sinks': 9.067581e-02, 'attn_w_out': 1.029578e-01, 'attn_b_out': 7.876096e-01, 'gdn_w_in': 1.508567e-01, 'gdn_conv_w': 1.419955e-01, 'gdn_A_log': 1.746771e+00, 'gdn_dt_bias': 1.618050e+00, 'gdn_norm_w': 6.005086e-01, 'gdn_w_out': 1.950868e-01, 'final_norm': 1.284647e+02}


def _to_microbatches(a, axis):
    t = _jnp.moveaxis(a, axis, 0)
    t = t.reshape((N_MICROBATCH, t.shape[0] // N_MICROBATCH) + t.shape[1:])
    return _jnp.moveaxis(t, 1, axis + 1)


def setup_inputs(seed: int = 0) -> dict:
    inp = _fwd_setup_inputs(seed)
    key = _jax.random.fold_in(_jax.random.key(seed), 7919)
    shape, _ = _output_shape()
    out = dict(inp)
    out["loss_target"] = _jax.random.normal(_jax.random.fold_in(key, 0), shape, _jnp.float32)
    for i, name in enumerate(TWIN_WEIGHTS):
        w = inp[name].astype(_jnp.float32)
        if MOMENT_SCALE is None:
            s = _jnp.sqrt(_jnp.mean(_jnp.square(w)) + 1e-30)
        else:
            s = MOMENT_SCALE[name]
        km, kv = _jax.random.split(_jax.random.fold_in(key, i + 1))
        out[name] = w
        out["m_" + name] = s * _jax.random.normal(km, w.shape, _jnp.float32)
        out["v_" + name] = (s * s) * _jax.random.uniform(kv, w.shape, _jnp.float32, 0.5, 1.5)
    if N_MICROBATCH > 1:
        for name, axis in PER_EXAMPLE_BATCH_AXIS.items():
            out[name] = _to_microbatches(out[name], axis)
    return {'x': out['x'], 'positions': out['positions'], 'ffn1_norm': out['ffn1_norm'], 'ffn1_w_gate_up': out['ffn1_w_gate_up'], 'ffn1_w_down': out['ffn1_w_down'], 'mix_norm': out['mix_norm'], 'ffn2_norm': out['ffn2_norm'], 'ffn2_w_gate_up': out['ffn2_w_gate_up'], 'ffn2_w_down': out['ffn2_w_down'], 'attn_w_in': out['attn_w_in'], 'attn_b_in': out['attn_b_in'], 'attn_sinks': out['attn_sinks'], 'attn_w_out': out['attn_w_out'], 'attn_b_out': out['attn_b_out'], 'gdn_w_in': out['gdn_w_in'], 'gdn_conv_w': out['gdn_conv_w'], 'gdn_A_log': out['gdn_A_log'], 'gdn_dt_bias': out['gdn_dt_bias'], 'gdn_norm_w': out['gdn_norm_w'], 'gdn_w_out': out['gdn_w_out'], 'final_norm': out['final_norm'], 'loss_target': out['loss_target'], 'm_ffn1_norm': out['m_ffn1_norm'], 'm_ffn1_w_gate_up': out['m_ffn1_w_gate_up'], 'm_ffn1_w_down': out['m_ffn1_w_down'], 'm_mix_norm': out['m_mix_norm'], 'm_ffn2_norm': out['m_ffn2_norm'], 'm_ffn2_w_gate_up': out['m_ffn2_w_gate_up'], 'm_ffn2_w_down': out['m_ffn2_w_down'], 'm_attn_w_in': out['m_attn_w_in'], 'm_attn_b_in': out['m_attn_b_in'], 'm_attn_sinks': out['m_attn_sinks'], 'm_attn_w_out': out['m_attn_w_out'], 'm_attn_b_out': out['m_attn_b_out'], 'm_gdn_w_in': out['m_gdn_w_in'], 'm_gdn_conv_w': out['m_gdn_conv_w'], 'm_gdn_A_log': out['m_gdn_A_log'], 'm_gdn_dt_bias': out['m_gdn_dt_bias'], 'm_gdn_norm_w': out['m_gdn_norm_w'], 'm_gdn_w_out': out['m_gdn_w_out'], 'm_final_norm': out['m_final_norm'], 'v_ffn1_norm': out['v_ffn1_norm'], 'v_ffn1_w_gate_up': out['v_ffn1_w_gate_up'], 'v_ffn1_w_down': out['v_ffn1_w_down'], 'v_mix_norm': out['v_mix_norm'], 'v_ffn2_norm': out['v_ffn2_norm'], 'v_ffn2_w_gate_up': out['v_ffn2_w_gate_up'], 'v_ffn2_w_down': out['v_ffn2_w_down'], 'v_attn_w_in': out['v_attn_w_in'], 'v_attn_b_in': out['v_attn_b_in'], 'v_attn_sinks': out['v_attn_sinks'], 'v_attn_w_out': out['v_attn_w_out'], 'v_attn_b_out': out['v_attn_b_out'], 'v_gdn_w_in': out['v_gdn_w_in'], 'v_gdn_conv_w': out['v_gdn_conv_w'], 'v_gdn_A_log': out['v_gdn_A_log'], 'v_gdn_dt_bias': out['v_gdn_dt_bias'], 'v_gdn_norm_w': out['v_gdn_norm_w'], 'v_gdn_w_out': out['v_gdn_w_out'], 'v_final_norm': out['v_final_norm']}


def _loss(weights, diff, rest, loss_target):
    with _jax.named_scope("forward"):
        args = {**rest, TWIN_DIFF_INPUT: diff, **{k: w.astype(_WEIGHT_DTYPES[k]) for k, w in weights.items()}}
        y = _forward(args)
    with _jax.named_scope("loss_head"):
        err = _jnp.square(y.astype(_jnp.float32) - loss_target)
        return 0.5 * _jnp.sum(_jnp.mean(err, axis=-1)) if err.ndim else 0.5 * err


def _adamw(w, g, m, v):
    m = ADAM_B1 * m + (1.0 - ADAM_B1) * g
    v = ADAM_B2 * v + (1.0 - ADAM_B2) * _jnp.square(g)
    m_hat = m / (1.0 - ADAM_B1 ** ADAM_STEP)
    v_hat = v / (1.0 - ADAM_B2 ** ADAM_STEP)
    delta = -ADAM_LR * (m_hat / (_jnp.sqrt(v_hat) + ADAM_EPS) + ADAM_WD * w)
    return delta, m, v


def reference(x, positions, ffn1_norm, ffn1_w_gate_up, ffn1_w_down, mix_norm, ffn2_norm, ffn2_w_gate_up, ffn2_w_down, attn_w_in, attn_b_in, attn_sinks, attn_w_out, attn_b_out, gdn_w_in, gdn_conv_w, gdn_A_log, gdn_dt_bias, gdn_norm_w, gdn_w_out, final_norm, loss_target, m_ffn1_norm, m_ffn1_w_gate_up, m_ffn1_w_down, m_mix_norm, m_ffn2_norm, m_ffn2_w_gate_up, m_ffn2_w_down, m_attn_w_in, m_attn_b_in, m_attn_sinks, m_attn_w_out, m_attn_b_out, m_gdn_w_in, m_gdn_conv_w, m_gdn_A_log, m_gdn_dt_bias, m_gdn_norm_w, m_gdn_w_out, m_final_norm, v_ffn1_norm, v_ffn1_w_gate_up, v_ffn1_w_down, v_mix_norm, v_ffn2_norm, v_ffn2_w_gate_up, v_ffn2_w_down, v_attn_w_in, v_attn_b_in, v_attn_sinks, v_attn_w_out, v_attn_b_out, v_gdn_w_in, v_gdn_conv_w, v_gdn_A_log, v_gdn_dt_bias, v_gdn_norm_w, v_gdn_w_out, v_final_norm):
    given = dict(x=x, positions=positions, ffn1_norm=ffn1_norm, ffn1_w_gate_up=ffn1_w_gate_up, ffn1_w_down=ffn1_w_down, mix_norm=mix_norm, ffn2_norm=ffn2_norm, ffn2_w_gate_up=ffn2_w_gate_up, ffn2_w_down=ffn2_w_down, attn_w_in=attn_w_in, attn_b_in=attn_b_in, attn_sinks=attn_sinks, attn_w_out=attn_w_out, attn_b_out=attn_b_out, gdn_w_in=gdn_w_in, gdn_conv_w=gdn_conv_w, gdn_A_log=gdn_A_log, gdn_dt_bias=gdn_dt_bias, gdn_norm_w=gdn_norm_w, gdn_w_out=gdn_w_out, final_norm=final_norm, loss_target=loss_target, m_ffn1_norm=m_ffn1_norm, m_ffn1_w_gate_up=m_ffn1_w_gate_up, m_ffn1_w_down=m_ffn1_w_down, m_mix_norm=m_mix_norm, m_ffn2_norm=m_ffn2_norm, m_ffn2_w_gate_up=m_ffn2_w_gate_up, m_ffn2_w_down=m_ffn2_w_down, m_attn_w_in=m_attn_w_in, m_attn_b_in=m_attn_b_in, m_attn_sinks=m_attn_sinks, m_attn_w_out=m_attn_w_out, m_attn_b_out=m_attn_b_out, m_gdn_w_in=m_gdn_w_in, m_gdn_conv_w=m_gdn_conv_w, m_gdn_A_log=m_gdn_A_log, m_gdn_dt_bias=m_gdn_dt_bias, m_gdn_norm_w=m_gdn_norm_w, m_gdn_w_out=m_gdn_w_out, m_final_norm=m_final_norm, v_ffn1_norm=v_ffn1_norm, v_ffn1_w_gate_up=v_ffn1_w_gate_up, v_ffn1_w_down=v_ffn1_w_down, v_mix_norm=v_mix_norm, v_ffn2_norm=v_ffn2_norm, v_ffn2_w_gate_up=v_ffn2_w_gate_up, v_ffn2_w_down=v_ffn2_w_down, v_attn_w_in=v_attn_w_in, v_attn_b_in=v_attn_b_in, v_attn_sinks=v_attn_sinks, v_attn_w_out=v_attn_w_out, v_attn_b_out=v_attn_b_out, v_gdn_w_in=v_gdn_w_in, v_gdn_conv_w=v_gdn_conv_w, v_gdn_A_log=v_gdn_A_log, v_gdn_dt_bias=v_gdn_dt_bias, v_gdn_norm_w=v_gdn_norm_w, v_gdn_w_out=v_gdn_w_out, v_final_norm=v_final_norm)
    weights = {n: given[n] for n in TWIN_WEIGHTS}
    shared = {n: given[n] for n in SHARED_INPUTS}
    per_example = {n: given[n] for n in ['x', 'positions']}
    grad_fn = _jax.value_and_grad(_loss, argnums=(0, 1))

    def one_microbatch(ex, loss_target):
        ex = dict(ex)
        diff = ex.pop(TWIN_DIFF_INPUT)
        return grad_fn(weights, diff, {**shared, **ex}, loss_target)

    if N_MICROBATCH == 1:
        loss, (grad_w, grad_x) = one_microbatch(per_example, given["loss_target"])
    else:
        def body(carry, xs):
            loss_sum, grad_sum = carry
            l_k, (gw_k, gx_k) = one_microbatch(xs[0], xs[1])
            with _jax.named_scope("update"):
                return (loss_sum + l_k, _jax.tree.map(_jnp.add, grad_sum, gw_k)), gx_k

        init = (_jnp.zeros((), _jnp.float32), _jax.tree.map(_jnp.zeros_like, weights))
        (loss, grad_w), grad_x = _jax.lax.scan(body, init, (per_example, given["loss_target"]))
    with _jax.named_scope("update"):
        delta_w, new_m, new_v = {}, {}, {}
        for n in TWIN_WEIGHTS:
            delta_w[n], new_m[n], new_v[n] = _adamw(weights[n], grad_w[n], given["m_" + n], given["v_" + n])
    return (loss, grad_x, *[grad_w[n] for n in TWIN_WEIGHTS], *[delta_w[n] for n in TWIN_WEIGHTS],
            *[new_m[n] for n in TWIN_WEIGHTS], *[new_v[n] for n in TWIN_WEIGHTS])
```

```python
import functools

import jax
import jax.numpy as jnp
from jax import lax
from jax.experimental import pallas as pl
from jax.experimental.pallas import tpu as pltpu

F32 = jnp.float32
BF16 = jnp.bfloat16
HI = lax.Precision.HIGHEST
MESH = pl.DeviceIdType.MESH

D_MODEL = 1024
D_FF = 2816
DEPTH = 4
NORM_EPS = 1e-6
LANE = 128

ATTN_Q_HEADS = 16
ATTN_KV_HEADS = 4
ATTN_HEAD_DIM = 64
ATTN_GROUP = 4
ATTN_BLOCK = 128
ROPE_DIM = 16
ROPE_THETA = 500000.0
ATTN_Q_W = 1024
ATTN_KV_W = 256

GDN_HEADS = 8
GDN_DK = 128
GDN_CONV = 4
GDN_CHUNK = 64
GDN_QKV_W = 3072

ADAM_LR = 0.001
ADAM_B1 = 0.9
ADAM_B2 = 0.999
ADAM_EPS = 1e-08
ADAM_WD = 0.01
ADAM_STEP = 10

NEG = -1e30


def _dot(a, b, prec=None):
    return lax.dot_general(a, b, (((1,), (0,)), ((), ())), precision=prec, preferred_element_type=F32)


def _dot_nt(a, b, prec=None):
    return lax.dot_general(a, b, (((1,), (1,)), ((), ())), precision=prec, preferred_element_type=F32)


def _dot_tn(a, b, prec=None):
    return lax.dot_general(a, b, (((0,), (0,)), ((), ())), precision=prec, preferred_element_type=F32)


def _bdot(a, b):
    return _dot(a.astype(BF16), b.astype(BF16))


def _bdot_nt(a, b):
    return _dot_nt(a.astype(BF16), b.astype(BF16))


def _bdot_tn(a, b):
    return _dot_tn(a.astype(BF16), b.astype(BF16))


def _sigmoid(x):
    return 1.0 / (1.0 + jnp.exp(-x))


def _silu(x):
    return x * _sigmoid(x)


def _silu_grad(x):
    s = _sigmoid(x)
    return s * (1.0 + x * (1.0 - s))


def _rms(x, w):
    r = lax.rsqrt(jnp.mean(x * x, axis=-1, keepdims=True) + NORM_EPS)
    xhat = x * r
    return xhat * w, xhat, r


def _rms_bwd(dy, w, xhat, r):
    dxhat = dy * w
    dx = r * (dxhat - xhat * jnp.mean(dxhat * xhat, axis=-1, keepdims=True))
    dw = jnp.sum(dy * xhat, axis=0, keepdims=True)
    return dx, dw


def _arb(n):
    return pltpu.CompilerParams(dimension_semantics=("arbitrary",) * n)


def _tile(n, want):
    t = min(n, want)
    assert n % t == 0, (n, want)
    return t


def _iota2(shape, dim):
    return lax.broadcasted_iota(jnp.int32, shape, dim)


def _inv_chain(a):
    c = a.shape[0]
    eye = (_iota2((c, c), 0) == _iota2((c, c), 1)).astype(F32)
    n = -a
    m = eye + n
    p = n
    k = 1
    while 2 * k < c:
        p = _dot(p, p, HI)
        m = m + _dot(m, p, HI)
        k *= 2
    return m


@jax.custom_vjp
def _unit_lower_inv(a):
    return _inv_chain(a)


def _unit_lower_inv_fwd(a):
    m = _inv_chain(a)
    return m, m


def _unit_lower_inv_bwd(m, dm):
    return (-_dot_nt(_dot_tn(m, dm, HI), m, HI),)


_unit_lower_inv.defvjp(_unit_lower_inv_fwd, _unit_lower_inv_bwd)


def _gdn_local(q, k, v, gb128, gb64, bb128):
    c = q.shape[0]
    r, s = _iota2((c, c), 0), _iota2((c, c), 1)
    causal, strict = r >= s, r > s
    tri = causal.astype(F32)
    tri_t = (r <= s).astype(F32)
    dc128 = _dot(tri, gb128, HI)
    dc64 = _dot(tri, gb64, HI)
    dr64 = _dot_tn(gb64, tri_t, HI)
    decay_l = jnp.exp(jnp.where(causal, dc64 - dr64, NEG))
    kb = k * bb128
    kbf = k.astype(BF16)
    a = jnp.where(strict, _dot_nt(kb.astype(BF16), kbf) * decay_l, 0.0)
    m = _unit_lower_inv(a)
    edc = jnp.exp(dc128)
    u = _dot(m, v * bb128, HI)
    w = _dot(m, kb * edc, HI)
    aqk = jnp.where(causal, _dot_nt(q.astype(BF16), kbf) * decay_l, 0.0)
    dl128 = _dot(jnp.ones((c, c), F32), gb128, HI)
    q_dec = q * edc
    k_dec = k * jnp.exp(dl128 - dc128)
    cd = jnp.exp(_dot(jnp.ones((GDN_DK, c), F32), gb128, HI))
    return u, w, aqk, q_dec, k_dec, cd


def _gated_norm(o, z, nw):
    r = lax.rsqrt(jnp.mean(o * o, axis=-1, keepdims=True) + NORM_EPS)
    return o * r * nw * _silu(z)


def _gated_norm_bwd(dy, o, z, nw):
    r = lax.rsqrt(jnp.mean(o * o, axis=-1, keepdims=True) + NORM_EPS)
    xhat = o * r
    sz = _silu(z)
    dxhat = dy * nw * sz
    do = r * (dxhat - xhat * jnp.mean(dxhat * xhat, axis=-1, keepdims=True))
    dz = dy * xhat * nw * _silu_grad(z)
    dnw = jnp.sum(dy * xhat * sz, axis=0, keepdims=True)
    return do, dz, dnw


def _gate_cols(gc, h, rows):
    lane = _iota2((rows, LANE), 1)
    beta = jnp.sum(jnp.where(lane == h, gc, 0.0), axis=-1, keepdims=True)
    g = jnp.sum(jnp.where(lane == GDN_HEADS + h, gc, 0.0), axis=-1, keepdims=True)
    return beta, g


def gdn_chunk_fwd(qkv, gc, proj, norm_w):
    t = qkv.shape[0]
    c = GDN_CHUNK
    tc = _tile(t, 256)
    nsub = tc // c

    def body(q_ref, k_ref, v_ref, gc_ref, z_ref, nw_ref, on_ref, st_ref, s_ref):
        h = pl.program_id(0)

        @pl.when(pl.program_id(1) == 0)
        def _():
            s_ref[...] = jnp.zeros_like(s_ref)

        beta, g = _gate_cols(gc_ref[...], h, tc)
        for j in range(nsub):
            sl = slice(j * c, (j + 1) * c)
            gj, bj = g[sl], beta[sl]
            u, w, aqk, q_dec, k_dec, cd = _gdn_local(
                q_ref[sl, :], k_ref[sl, :], v_ref[sl, :],
                jnp.broadcast_to(gj, (c, LANE)), jnp.broadcast_to(gj, (c, c)), jnp.broadcast_to(bj, (c, LANE)))
            s = s_ref[...]
            st_ref[0, j] = s
            v_new = u - _bdot(w, s)
            o = _bdot(q_dec, s) + _bdot(aqk, v_new)
            s_ref[...] = s * cd + _bdot_tn(k_dec, v_new)
            on_ref[sl, :] = _gated_norm(o, z_ref[sl, :], nw_ref[...]).astype(BF16)

    col = lambda off: pl.BlockSpec((tc, LANE), lambda h, i: (i, off + h))
    return pl.pallas_call(
        body, name="gdn_chunk_fwd",
        grid=(GDN_HEADS, t // tc),
        in_specs=[col(0), col(GDN_HEADS), col(2 * GDN_HEADS),
                  pl.BlockSpec((tc, LANE), lambda h, i: (i, 0)),
                  col(GDN_QKV_W // LANE),
                  pl.BlockSpec((1, LANE), lambda h, i: (0, 0))],
        out_specs=[pl.BlockSpec((tc, LANE), lambda h, i: (i, h)),
                   pl.BlockSpec((1, nsub, GDN_DK, GDN_DK), lambda h, i: (h, i, 0, 0))],
        out_shape=[jax.ShapeDtypeStruct((t, GDN_HEADS * GDN_DK), BF16),
                   jax.ShapeDtypeStruct((GDN_HEADS, t // c, GDN_DK, GDN_DK), F32)],
        scratch_shapes=[pltpu.VMEM((GDN_DK, GDN_DK), F32)],
        compiler_params=_arb(2),
    )(qkv, qkv, qkv, gc, proj, norm_w)


def gdn_chunk_bwd(qkv, gc, proj, norm_w, states, d_on):
    t = qkv.shape[0]
    c = GDN_CHUNK
    tc = _tile(t, 256)
    nsub = tc // c
    nblk = t // tc

    def body(q_ref, k_ref, v_ref, gc_ref, z_ref, nw_ref, st_ref, don_ref,
             dq_ref, dk_ref, dv_ref, dz_ref, dgc_ref, dnw_ref, ds_ref):
        h = pl.program_id(0)

        @pl.when(pl.program_id(1) == 0)
        def _():
            ds_ref[...] = jnp.zeros_like(ds_ref)

        @pl.when((pl.program_id(1) == 0) & (h == 0))
        def _():
            dnw_ref[...] = jnp.zeros_like(dnw_ref)

        beta, g = _gate_cols(gc_ref[...], h, tc)
        lane = _iota2((c, LANE), 1)
        nw = nw_ref[...]
        dnw = jnp.zeros((1, LANE), F32)
        for j in reversed(range(nsub)):
            sl = slice(j * c, (j + 1) * c)
            gj, bj = g[sl], beta[sl]
            (u, w, aqk, q_dec, k_dec, cd), vjp = jax.vjp(
                _gdn_local, q_ref[sl, :], k_ref[sl, :], v_ref[sl, :],
                jnp.broadcast_to(gj, (c, LANE)), jnp.broadcast_to(gj, (c, c)), jnp.broadcast_to(bj, (c, LANE)))
            s = st_ref[0, j]
            v_new = u - _bdot(w, s)
            o = _bdot(q_dec, s) + _bdot(aqk, v_new)
            do, dz, dnw_j = _gated_norm_bwd(don_ref[sl, :], o, z_ref[sl, :], nw)
            dnw = dnw + dnw_j
            dz_ref[sl, :] = dz.astype(BF16)
            ds_next = ds_ref[...]
            dv_new = _bdot_tn(aqk, do) + _bdot(k_dec, ds_next)
            d_qdec = _bdot_nt(do, s)
            d_aqk = _bdot_nt(do, v_new)
            d_kdec = _bdot_nt(v_new, ds_next)
            d_cd = s * ds_next
            d_w = -_bdot_nt(dv_new, s)
            ds_ref[...] = _bdot_tn(q_dec, do) + cd * ds_next - _bdot_tn(w, dv_new)
            dq, dk, dv, dgb128, dgb64, dbb = vjp((dv_new, d_w, d_aqk, d_qdec, d_kdec, d_cd))
            dq_ref[sl, :] = dq
            dk_ref[sl, :] = dk
            dv_ref[sl, :] = dv
            dg = jnp.sum(dgb128, axis=-1, keepdims=True) + jnp.sum(dgb64, axis=-1, keepdims=True)
            dbeta = jnp.sum(dbb, axis=-1, keepdims=True)
            dgc_ref[0, sl, :] = jnp.where(lane == h, dbeta, jnp.where(lane == GDN_HEADS + h, dg, 0.0))
        dnw_ref[...] += dnw

    rev = lambda i: nblk - 1 - i
    col = lambda off: pl.BlockSpec((tc, LANE), lambda h, i: (rev(i), off + h))
    return pl.pallas_call(
        body, name="gdn_chunk_bwd",
        grid=(GDN_HEADS, nblk),
        in_specs=[col(0), col(GDN_HEADS), col(2 * GDN_HEADS),
                  pl.BlockSpec((tc, LANE), lambda h, i: (rev(i), 0)),
                  col(GDN_QKV_W // LANE),
                  pl.BlockSpec((1, LANE), lambda h, i: (0, 0)),
                  pl.BlockSpec((1, nsub, GDN_DK, GDN_DK), lambda h, i: (h, rev(i), 0, 0)),
                  col(0)],
        out_specs=[col(0), col(0), col(0), col(0),
                   pl.BlockSpec((1, tc, LANE), lambda h, i: (h, rev(i), 0)),
                   pl.BlockSpec((1, LANE), lambda h, i: (0, 0))],
        out_shape=[
            jax.ShapeDtypeStruct((t, GDN_HEADS * GDN_DK), F32),
            jax.ShapeDtypeStruct((t, GDN_HEADS * GDN_DK), F32),
            jax.ShapeDtypeStruct((t, GDN_HEADS * GDN_DK), F32),
            jax.ShapeDtypeStruct((t, GDN_HEADS * GDN_DK), BF16),
            jax.ShapeDtypeStruct((GDN_HEADS, t, LANE), F32),
            jax.ShapeDtypeStruct((1, LANE), F32)],
        scratch_shapes=[pltpu.VMEM((GDN_DK, GDN_DK), F32)],
        compiler_params=_arb(2),
    )(qkv, qkv, qkv, gc, proj, norm_w, states, d_on)


GDN_PROJ_W = GDN_QKV_W + 1024 + LANE
GDN_Q_SCALE = GDN_DK ** -0.5


def _shift_rows(ext, shift, lo, n):
    if shift == 0:
        return ext[lo:lo + n]
    return pltpu.roll(ext, shift, 0)[lo:lo + n]


def _conv_fwd(x, halo, w):
    n = x.shape[0]
    ext = jnp.concatenate([halo, x], axis=0)
    y = w[GDN_CONV - 1:GDN_CONV] * x
    for j in range(GDN_CONV - 1):
        y = y + w[j:j + 1] * _shift_rows(ext, GDN_CONV - 1 - j, 8, n)
    return y


def _softplus(x):
    return jnp.maximum(x, 0.0) + jnp.log(1.0 + jnp.exp(-jnp.abs(x)))


def _l2n(t):
    rs = lax.rsqrt(jnp.sum(t * t, axis=-1, keepdims=True) + NORM_EPS)
    return t * rs, rs


def gdn_pre_fwd(proj, conv_w, gate_par):
    t = proj.shape[0]
    tm = _tile(t, 256)

    def body(x_ref, halo_ref, ba_ref, w_ref, gp_ref, qkv_ref, gc_ref):
        i = pl.program_id(0)
        halo = jnp.where(i > 0, halo_ref[...], 0.0)
        y = _silu(_conv_fwd(x_ref[...], halo, w_ref[...]))
        for hh in range(2 * GDN_HEADS):
            sl = slice(hh * LANE, (hh + 1) * LANE)
            tn, _ = _l2n(y[:, sl])
            qkv_ref[:, sl] = tn * GDN_Q_SCALE if hh < GDN_HEADS else tn
        qkv_ref[:, 2 * GDN_HEADS * LANE:] = y[:, 2 * GDN_HEADS * LANE:]
        ba = ba_ref[...]
        lane = _iota2(ba.shape, 1)
        gp = gp_ref[...]
        g = -jnp.exp(gp[0:1]) * _softplus(ba + gp[1:2])
        gc_ref[...] = jnp.where(lane < GDN_HEADS, _sigmoid(ba), jnp.where(lane < 2 * GDN_HEADS, g, 0.0))

    return pl.pallas_call(
        body, name="gdn_pre_fwd",
        grid=(t // tm,),
        in_specs=[pl.BlockSpec((tm, GDN_QKV_W), lambda i: (i, 0)),
                  pl.BlockSpec((8, GDN_QKV_W), lambda i: (jnp.maximum(i * (tm // 8) - 1, 0), 0)),
                  pl.BlockSpec((tm, LANE), lambda i: (i, (GDN_QKV_W + 1024) // LANE)),
                  pl.BlockSpec((8, GDN_QKV_W), lambda i: (0, 0)),
                  pl.BlockSpec((8, LANE), lambda i: (0, 0))],
        out_specs=[pl.BlockSpec((tm, GDN_QKV_W), lambda i: (i, 0)),
                   pl.BlockSpec((tm, LANE), lambda i: (i, 0))],
        out_shape=[jax.ShapeDtypeStruct((t, GDN_QKV_W), F32), jax.ShapeDtypeStruct((t, LANE), F32)],
        compiler_params=_arb(1),
    )(proj, proj, proj, conv_w, gate_par)


def gdn_pre_bwd(proj, conv_w, gate_par, dq, dk, dv, dgc_heads):
    t = proj.shape[0]
    tm = _tile(t, 256)

    def body(x_ref, halo_ref, ba_ref, w_ref, gp_ref, dq_ref, dk_ref, dv_ref, dgc_ref, dy_ref, dba_ref, dgp_ref):
        i = pl.program_id(0)

        @pl.when(i == 0)
        def _():
            dgp_ref[...] = jnp.zeros_like(dgp_ref)

        halo = jnp.where(i > 0, halo_ref[...], 0.0)
        y = _conv_fwd(x_ref[...], halo, w_ref[...])
        for hh in range(3 * GDN_HEADS):
            sl = slice(hh * LANE, (hh + 1) * LANE)
            hsl = slice((hh % GDN_HEADS) * LANE, (hh % GDN_HEADS + 1) * LANE)
            yy = y[:, sl]
            if hh < 2 * GDN_HEADS:
                tn, rs = _l2n(_silu(yy))
                dtn = dq_ref[:, hsl] * GDN_Q_SCALE if hh < GDN_HEADS else dk_ref[:, hsl]
                dsil = rs * (dtn - tn * jnp.sum(dtn * tn, axis=-1, keepdims=True))
            else:
                dsil = dv_ref[:, hsl]
            dy_ref[:, sl] = dsil * _silu_grad(yy)
        dgc = dgc_ref[0]
        for hh in range(1, GDN_HEADS):
            dgc = dgc + dgc_ref[hh]
        ba = ba_ref[...]
        lane = _iota2(ba.shape, 1)
        gp = gp_ref[...]
        xg = ba + gp[1:2]
        ea = jnp.exp(gp[0:1])
        sp = _softplus(xg)
        sb = _sigmoid(ba)
        is_b = lane < GDN_HEADS
        is_a = (lane >= GDN_HEADS) & (lane < 2 * GDN_HEADS)
        d_pre = jnp.where(is_a, dgc * (-ea) * _sigmoid(xg), 0.0)
        dba_ref[...] = jnp.where(is_b, dgc * sb * (1.0 - sb), d_pre).astype(BF16)
        d_alog = jnp.sum(jnp.where(is_a, dgc * (-ea) * sp, 0.0), axis=0, keepdims=True)
        d_dtb = jnp.sum(d_pre, axis=0, keepdims=True)
        row = _iota2((8, LANE), 0)
        dgp_ref[...] += jnp.where(row == 0, d_alog, jnp.where(row == 1, d_dtb, 0.0))

    hspec = pl.BlockSpec((tm, GDN_HEADS * LANE), lambda i: (i, 0))
    return pl.pallas_call(
        body, name="gdn_pre_bwd",
        grid=(t // tm,),
        in_specs=[pl.BlockSpec((tm, GDN_QKV_W), lambda i: (i, 0)),
                  pl.BlockSpec((8, GDN_QKV_W), lambda i: (jnp.maximum(i * (tm // 8) - 1, 0), 0)),
                  pl.BlockSpec((tm, LANE), lambda i: (i, (GDN_QKV_W + 1024) // LANE)),
                  pl.BlockSpec((8, GDN_QKV_W), lambda i: (0, 0)),
                  pl.BlockSpec((8, LANE), lambda i: (0, 0)),
                  hspec, hspec, hspec,
                  pl.BlockSpec((GDN_HEADS, tm, LANE), lambda i: (0, i, 0))],
        out_specs=[pl.BlockSpec((tm, GDN_QKV_W), lambda i: (i, 0)),
                   pl.BlockSpec((tm, LANE), lambda i: (i, 0)),
                   pl.BlockSpec((8, LANE), lambda i: (0, 0))],
        out_shape=[jax.ShapeDtypeStruct((t, GDN_QKV_W), F32), jax.ShapeDtypeStruct((t, LANE), BF16),
                   jax.ShapeDtypeStruct((8, LANE), F32)],
        compiler_params=_arb(1),
    )(proj, proj, proj, conv_w, gate_par, dq, dk, dv, dgc_heads)


def gdn_conv_bwd(proj, conv_w, dy):
    t = proj.shape[0]
    tm = _tile(t, 256)
    nblk = t // tm

    def body(x_ref, halo_ref, w_ref, dy_ref, dyn_ref, dx_ref, dw_ref):
        i = pl.program_id(0)

        @pl.when(i == 0)
        def _():
            dw_ref[...] = jnp.zeros_like(dw_ref)

        w = w_ref[...]
        dy = dy_ref[...]
        ext_dy = jnp.concatenate([dy, jnp.where(i < nblk - 1, dyn_ref[...], 0.0)], axis=0)
        ext_x = jnp.concatenate([jnp.where(i > 0, halo_ref[...], 0.0), x_ref[...]], axis=0)
        dx = w[GDN_CONV - 1:GDN_CONV] * dy
        rows = [jnp.sum(dy * x_ref[...], axis=0, keepdims=True)]
        for j in range(GDN_CONV - 1):
            sh = GDN_CONV - 1 - j
            dx = dx + w[j:j + 1] * _shift_rows(ext_dy, tm + 8 - sh, 0, tm)
            rows.insert(j, jnp.sum(dy * _shift_rows(ext_x, sh, 8, tm), axis=0, keepdims=True))
        dx_ref[...] = dx.astype(BF16)
        row = _iota2((8, GDN_QKV_W), 0)
        acc = jnp.zeros((8, GDN_QKV_W), F32)
        for j in range(GDN_CONV):
            acc = acc + jnp.where(row == j, rows[j], 0.0)
        dw_ref[...] += acc

    return pl.pallas_call(
        body, name="gdn_conv_bwd",
        grid=(nblk,),
        in_specs=[pl.BlockSpec((tm, GDN_QKV_W), lambda i: (i, 0)),
                  pl.BlockSpec((8, GDN_QKV_W), lambda i: (jnp.maximum(i * (tm // 8) - 1, 0), 0)),
                  pl.BlockSpec((8, GDN_QKV_W), lambda i: (0, 0)),
                  pl.BlockSpec((tm, GDN_QKV_W), lambda i: (i, 0)),
                  pl.BlockSpec((8, GDN_QKV_W), lambda i: (jnp.minimum((i + 1) * (tm // 8), t // 8 - 1), 0))],
        out_specs=[pl.BlockSpec((tm, GDN_QKV_W), lambda i: (i, 0)),
                   pl.BlockSpec((8, GDN_QKV_W), lambda i: (0, 0))],
        out_shape=[jax.ShapeDtypeStruct((t, GDN_QKV_W), BF16), jax.ShapeDtypeStruct((8, GDN_QKV_W), F32)],
        compiler_params=_arb(1),
    )(proj, proj, conv_w, dy, dy)


def _resident(w_hbm, w_vmem, sem):
    @pl.when(pl.program_id(0) == 0)
    def _():
        cp = pltpu.make_async_copy(w_hbm, w_vmem, sem)
        cp.start()
        cp.wait()


ANY = pl.BlockSpec(memory_space=pl.ANY)


def norm_proj(h, nw, w, bias):
    t, d = h.shape
    n = w.shape[1]
    tm = _tile(t, 256)
    nc = _tile(n, 1536) if n % 1536 == 0 else _tile(n, 1408)

    def body(h_ref, nw_ref, w_hbm, b_ref, o_ref, w_ref, sem):
        _resident(w_hbm, w_ref, sem)
        hn = _rms(h_ref[...], nw_ref[...])[0].astype(BF16)
        for c0 in range(0, n, nc):
            o_ref[:, c0:c0 + nc] = _dot(hn, w_ref[:, c0:c0 + nc]) + b_ref[:, c0:c0 + nc]

    return pl.pallas_call(
        body, name="norm_proj",
        grid=(t // tm,),
        in_specs=[pl.BlockSpec((tm, d), lambda i: (i, 0)), pl.BlockSpec((1, d), lambda i: (0, 0)), ANY,
                  pl.BlockSpec((1, n), lambda i: (0, 0))],
        out_specs=pl.BlockSpec((tm, n), lambda i: (i, 0)),
        out_shape=jax.ShapeDtypeStruct((t, n), F32),
        scratch_shapes=[pltpu.VMEM((d, n), BF16), pltpu.SemaphoreType.DMA],
        compiler_params=_arb(1),
    )(h, nw, w, bias)


def linear_residual(h, x, w, bias):
    t, d = h.shape
    k = x.shape[1]
    tm = _tile(t, 512)

    def body(h_ref, x_ref, w_hbm, b_ref, o_ref, w_ref, sem):
        _resident(w_hbm, w_ref, sem)
        o_ref[...] = h_ref[...] + _dot(x_ref[...], w_ref[...]) + b_ref[...]

    return pl.pallas_call(
        body, name="linear_residual",
        grid=(t // tm,),
        in_specs=[pl.BlockSpec((tm, d), lambda i: (i, 0)), pl.BlockSpec((tm, k), lambda i: (i, 0)), ANY,
                  pl.BlockSpec((1, d), lambda i: (0, 0))],
        out_specs=pl.BlockSpec((tm, d), lambda i: (i, 0)),
        out_shape=jax.ShapeDtypeStruct((t, d), F32),
        scratch_shapes=[pltpu.VMEM((k, d), BF16), pltpu.SemaphoreType.DMA],
        compiler_params=_arb(1),
    )(h, x, w, bias)


def matmul_nt(dy, w):
    t, d = dy.shape
    k = w.shape[0]
    tm = _tile(t, 512)

    def body(dy_ref, w_hbm, o_ref, cs_ref, w_ref, sem):
        _resident(w_hbm, w_ref, sem)

        @pl.when(pl.program_id(0) == 0)
        def _():
            cs_ref[...] = jnp.zeros_like(cs_ref)

        dy = dy_ref[...]
        cs_ref[...] += jnp.sum(dy, axis=0, keepdims=True)
        o_ref[...] = _dot_nt(dy.astype(BF16), w_ref[...])

    return pl.pallas_call(
        body, name="matmul_nt",
        grid=(t // tm,),
        in_specs=[pl.BlockSpec((tm, d), lambda i: (i, 0)), ANY],
        out_specs=[pl.BlockSpec((tm, k), lambda i: (i, 0)), pl.BlockSpec((1, d), lambda i: (0, 0))],
        out_shape=[jax.ShapeDtypeStruct((t, k), F32), jax.ShapeDtypeStruct((1, d), F32)],
        scratch_shapes=[pltpu.VMEM((k, d), BF16), pltpu.SemaphoreType.DMA],
        compiler_params=_arb(1),
    )(dy, w)


def norm_proj_bwd(h, nw, dh, dps, ws):
    t, d = h.shape
    np_ = len(dps)
    ns = [w.shape[1] for w in ws]
    tm = _tile(t, 256)

    def body(*refs):
        h_ref, nw_ref, dh_ref = refs[:3]
        dp_refs = refs[3:3 + np_]
        w_hbms = refs[3 + np_:3 + 2 * np_]
        o_ref, hn_ref, dnw_ref = refs[3 + 2 * np_:6 + 2 * np_]
        cs_refs = refs[6 + 2 * np_:6 + 3 * np_]
        w_refs = refs[6 + 3 * np_:6 + 4 * np_]
        sem = refs[6 + 4 * np_]
        for a, b in zip(w_hbms, w_refs):
            _resident(a, b, sem)

        @pl.when(pl.program_id(0) == 0)
        def _():
            dnw_ref[...] = jnp.zeros_like(dnw_ref)
            for c in cs_refs:
                c[...] = jnp.zeros_like(c)

        nw = nw_ref[...]
        hn, xhat, r = _rms(h_ref[...], nw)
        hn_ref[...] = hn.astype(BF16)
        dhn = jnp.zeros((tm, d), F32)
        for dp_ref, w_ref, cs_ref in zip(dp_refs, w_refs, cs_refs):
            dp = dp_ref[...]
            cs_ref[...] += jnp.sum(dp.astype(F32), axis=0, keepdims=True)
            dhn = dhn + _dot_nt(dp, w_ref[...])
        dx, dnw = _rms_bwd(dhn, nw, xhat, r)
        dnw_ref[...] += dnw
        o_ref[...] = dh_ref[...] + dx

    row = pl.BlockSpec((tm, d), lambda i: (i, 0))
    vec = pl.BlockSpec((1, d), lambda i: (0, 0))
    return pl.pallas_call(
        body, name="norm_proj_bwd",
        grid=(t // tm,),
        in_specs=[row, vec, row] + [pl.BlockSpec((tm, n), lambda i: (i, 0)) for n in ns] + [ANY] * np_,
        out_specs=[row, row, vec] + [pl.BlockSpec((1, n), lambda i: (0, 0)) for n in ns],
        out_shape=[jax.ShapeDtypeStruct((t, d), F32), jax.ShapeDtypeStruct((t, d), BF16),
                   jax.ShapeDtypeStruct((1, d), F32)] + [jax.ShapeDtypeStruct((1, n), F32) for n in ns],
        scratch_shapes=[pltpu.VMEM((d, n), BF16) for n in ns] + [pltpu.SemaphoreType.DMA],
        compiler_params=_arb(1),
    )(h, nw, dh, *dps, *ws)


def matmul_tn(x, y, scale=1.0):
    t, k = x.shape
    n = y.shape[1]
    tk = _tile(k, 1024) if k % 1024 == 0 else _tile(k, 1408)
    tn = n if n <= 1536 else (1024 if n % 1024 == 0 else 1408)
    assert n % tn == 0
    tt = _tile(t, 1024)
    nt = t // tt

    def body(x_ref, y_ref, o_ref):
        @pl.when(pl.program_id(2) == 0)
        def _():
            o_ref[...] = jnp.zeros_like(o_ref)

        yv = y_ref[...]
        if scale != 1.0:
            yv = yv * scale
        o_ref[...] += _dot_tn(x_ref[...].astype(BF16), yv.astype(BF16))

    return pl.pallas_call(
        body, name="matmul_tn",
        grid=(k // tk, n // tn, nt),
        in_specs=[pl.BlockSpec((tt, tk), lambda i, j, s: (s, i)), pl.BlockSpec((tt, tn), lambda i, j, s: (s, j))],
        out_specs=pl.BlockSpec((tk, tn), lambda i, j, s: (i, j)),
        out_shape=jax.ShapeDtypeStruct((k, n), F32),
        compiler_params=_arb(3),
    )(x, y)


FFN_CHUNKS = 2


def ffn_fwd(h, nw, wgu, wd):
    t, d = h.shape
    f = wd.shape[0]
    fc = f // FFN_CHUNKS
    tm = _tile(t, 512)

    def body(h_ref, nw_ref, wgu_hbm, wd_hbm, o_ref, wgu_ref, wd_ref, sem):
        _resident(wgu_hbm, wgu_ref, sem)
        _resident(wd_hbm, wd_ref, sem)
        x = h_ref[...]
        hn = _rms(x, nw_ref[...])[0].astype(BF16)
        acc = jnp.zeros((tm, d), F32)
        for c in range(FFN_CHUNKS):
            g = _dot(hn, wgu_ref[:, c * fc:(c + 1) * fc])
            u = _dot(hn, wgu_ref[:, f + c * fc:f + (c + 1) * fc])
            acc = acc + _dot((_silu(g) * u).astype(BF16), wd_ref[c * fc:(c + 1) * fc, :])
        o_ref[...] = x + 0.5 * acc

    return pl.pallas_call(
        body, name="ffn_fwd",
        grid=(t // tm,),
        in_specs=[pl.BlockSpec((tm, d), lambda i: (i, 0)), pl.BlockSpec((1, d), lambda i: (0, 0)), ANY, ANY],
        out_specs=pl.BlockSpec((tm, d), lambda i: (i, 0)),
        out_shape=jax.ShapeDtypeStruct((t, d), F32),
        scratch_shapes=[pltpu.VMEM((d, 2 * f), BF16), pltpu.VMEM((f, d), BF16), pltpu.SemaphoreType.DMA],
        compiler_params=_arb(1),
    )(h, nw, wgu, wd)


def ffn_bwd(h, nw, wgu, wd, dh):
    t, d = h.shape
    f = wd.shape[0]
    fc = f // FFN_CHUNKS
    tm = _tile(t, 256)

    def body(h_ref, nw_ref, wgu_hbm, wd_hbm, dh_ref, o_ref, hn_ref, a_ref, dgu_ref, dnw_ref, wgu_ref, wd_ref, sem):
        _resident(wgu_hbm, wgu_ref, sem)
        _resident(wd_hbm, wd_ref, sem)

        @pl.when(pl.program_id(0) == 0)
        def _():
            dnw_ref[...] = jnp.zeros_like(dnw_ref)

        nw = nw_ref[...]
        hn32, xhat, r = _rms(h_ref[...], nw)
        hn = hn32.astype(BF16)
        hn_ref[...] = hn
        dh = dh_ref[...]
        dout = (0.5 * dh).astype(BF16)
        dhn = jnp.zeros((tm, d), F32)
        for c in range(FFN_CHUNKS):
            gs, us = slice(c * fc, (c + 1) * fc), slice(f + c * fc, f + (c + 1) * fc)
            g = _dot(hn, wgu_ref[:, gs])
            u = _dot(hn, wgu_ref[:, us])
            sg = _sigmoid(g)
            sil = g * sg
            a_ref[:, gs] = (sil * u).astype(BF16)
            da = _dot_nt(dout, wd_ref[gs, :])
            dg = (da * u * (sg * (1.0 + g * (1.0 - sg)))).astype(BF16)
            du = (da * sil).astype(BF16)
            dgu_ref[:, gs] = dg
            dgu_ref[:, us] = du
            dhn = dhn + _dot_nt(dg, wgu_ref[:, gs]) + _dot_nt(du, wgu_ref[:, us])
        dx, dnw = _rms_bwd(dhn, nw, xhat, r)
        dnw_ref[...] += dnw
        o_ref[...] = dh + dx

    row = pl.BlockSpec((tm, d), lambda i: (i, 0))
    vec = pl.BlockSpec((1, d), lambda i: (0, 0))
    return pl.pallas_call(
        body, name="ffn_bwd",
        grid=(t // tm,),
        in_specs=[row, vec, ANY, ANY, row],
        out_specs=[row, row, pl.BlockSpec((tm, f), lambda i: (i, 0)), pl.BlockSpec((tm, 2 * f), lambda i: (i, 0)), vec],
        out_shape=[jax.ShapeDtypeStruct((t, d), F32), jax.ShapeDtypeStruct((t, d), BF16),
                   jax.ShapeDtypeStruct((t, f), BF16), jax.ShapeDtypeStruct((t, 2 * f), BF16),
                   jax.ShapeDtypeStruct((1, d), F32)],
        scratch_shapes=[pltpu.VMEM((d, 2 * f), BF16), pltpu.VMEM((f, d), BF16), pltpu.SemaphoreType.DMA],
        compiler_params=_arb(1),
    )(h, nw, wgu, wd, dh)


def loss_head(h, nw, target):
    t, d = h.shape
    tm = _tile(t, 512)

    def body(h_ref, nw_ref, tg_ref, dh_ref, loss_ref, dnw_ref):
        @pl.when(pl.program_id(0) == 0)
        def _():
            loss_ref[...] = jnp.zeros_like(loss_ref)
            dnw_ref[...] = jnp.zeros_like(dnw_ref)

        nw = nw_ref[...]
        y, xhat, r = _rms(h_ref[...], nw)
        e = y - tg_ref[...]
        loss_ref[...] += 0.5 * jnp.sum(jnp.mean(e * e, axis=-1, keepdims=True), axis=0, keepdims=True)
        dx, dnw = _rms_bwd(e * (1.0 / d), nw, xhat, r)
        dnw_ref[...] += dnw
        dh_ref[...] = dx

    row = pl.BlockSpec((tm, d), lambda i: (i, 0))
    vec = pl.BlockSpec((1, d), lambda i: (0, 0))
    return pl.pallas_call(
        body, name="loss_head",
        grid=(t // tm,),
        in_specs=[row, vec, row],
        out_specs=[row, pl.BlockSpec((8, LANE), lambda i: (0, 0)), vec],
        out_shape=[jax.ShapeDtypeStruct((t, d), F32), jax.ShapeDtypeStruct((8, LANE), F32),
                   jax.ShapeDtypeStruct((1, d), F32)],
        compiler_params=_arb(1),
    )(h, nw, target)


def adamw(w, g, m, v):
    r, c = w.shape
    tr = r
    while tr * c * 4 > (1 << 20) and tr % 16 == 0:
        tr //= 2

    def body(w_ref, g_ref, m_ref, v_ref, d_ref, nm_ref, nv_ref):
        g = g_ref[...]
        m = ADAM_B1 * m_ref[...] + (1.0 - ADAM_B1) * g
        v = ADAM_B2 * v_ref[...] + (1.0 - ADAM_B2) * (g * g)
        m_hat = m / (1.0 - ADAM_B1 ** ADAM_STEP)
        v_hat = v / (1.0 - ADAM_B2 ** ADAM_STEP)
        d_ref[...] = -ADAM_LR * (m_hat / (jnp.sqrt(v_hat) + ADAM_EPS) + ADAM_WD * w_ref[...])
        nm_ref[...] = m
        nv_ref[...] = v

    blk = pl.BlockSpec((tr, c), lambda i: (i, 0))
    return pl.pallas_call(
        body, name="adamw",
        grid=(r // tr,),
        in_specs=[blk] * 4, out_specs=[blk] * 3,
        out_shape=[jax.ShapeDtypeStruct((r, c), F32)] * 3,
        compiler_params=_arb(1),
    )(w, g, m, v)


ATTN_P_W = ATTN_Q_W + 2 * ATTN_KV_W
ATTN_SCALE = ATTN_HEAD_DIM ** -0.5


def _rope_group(t, tab, sign):
    return (t * tab[:, 0:LANE] + sign * pltpu.roll(t, 8, 1) * tab[:, LANE:2 * LANE]
            + sign * pltpu.roll(t, LANE - 8, 1) * tab[:, 2 * LANE:3 * LANE])


def _rope(t, tab, sign=1.0):
    return jnp.concatenate([_rope_group(t[:, s:s + LANE], tab, sign) for s in range(0, t.shape[1], LANE)], axis=1)


def _attn_heads(qs, ks, vs, sinks, first):
    b = ATTN_BLOCK
    rows = ATTN_GROUP * b
    qi = _iota2((rows, 2 * b), 0) % b
    kj = _iota2((rows, 2 * b), 1)
    rel = qi + b - kj
    valid = (rel >= 0) & (rel < b) & ((kj >= b) | jnp.logical_not(first))
    outs = []
    for q, k, v, sink in zip(qs, ks, vs, sinks):
        s = jnp.where(valid, _dot_nt(q, k) * ATTN_SCALE, NEG)
        m = lax.stop_gradient(jnp.maximum(jnp.max(s, axis=-1, keepdims=True), sink))
        p = jnp.exp(s - m)
        den = jnp.sum(p, axis=-1, keepdims=True) + jnp.exp(sink - m)
        outs.append(_dot((p / den).astype(BF16), v))
    return outs


def _attn_prepare(p_ref, kvp_ref, tab_ref, tabp_ref, sink_ref):
    b = ATTN_BLOCK
    hd = ATTN_HEAD_DIM
    tab, tabp = tab_ref[...], tabp_ref[...]
    q = _rope(p_ref[:, 0:ATTN_Q_W], tab).astype(BF16)
    kc = _rope(p_ref[:, ATTN_Q_W:ATTN_Q_W + ATTN_KV_W], tab).astype(BF16)
    kp = _rope(kvp_ref[:, 0:ATTN_KV_W], tabp).astype(BF16)
    vc = p_ref[:, ATTN_Q_W + ATTN_KV_W:ATTN_P_W].astype(BF16)
    vp = kvp_ref[:, ATTN_KV_W:2 * ATTN_KV_W].astype(BF16)
    sk = sink_ref[...]
    qs, ks, vs, sinks = [], [], [], []
    for h in range(ATTN_KV_HEADS):
        heads = [ATTN_GROUP * h + g for g in range(ATTN_GROUP)]
        qs.append(jnp.concatenate([q[:, i * hd:(i + 1) * hd] for i in heads], axis=0))
        ks.append(jnp.concatenate([kp[:, h * hd:(h + 1) * hd], kc[:, h * hd:(h + 1) * hd]], axis=0))
        vs.append(jnp.concatenate([vp[:, h * hd:(h + 1) * hd], vc[:, h * hd:(h + 1) * hd]], axis=0))
        sinks.append(jnp.concatenate([jnp.broadcast_to(sk[:, i:i + 1], (b, 1)) for i in heads], axis=0))
    return qs, ks, vs, sinks


def _unstack_heads(xs):
    b = ATTN_BLOCK
    return jnp.concatenate([x[g * b:(g + 1) * b] for x in xs for g in range(ATTN_GROUP)], axis=1)


def _attn_specs(nb):
    b = ATTN_BLOCK
    prev = lambda n: jnp.maximum(n - 1, 0)
    return [pl.BlockSpec((b, ATTN_P_W), lambda n: (n, 0)),
            pl.BlockSpec((b, 2 * ATTN_KV_W), lambda n: (prev(n), ATTN_Q_W // (2 * ATTN_KV_W))),
            pl.BlockSpec((b, 3 * LANE), lambda n: (n, 0)),
            pl.BlockSpec((b, 3 * LANE), lambda n: (prev(n), 0)),
            pl.BlockSpec((1, ATTN_Q_HEADS), lambda n: (0, 0))]


def attn_fwd(p, tab, sinks):
    t = p.shape[0]
    nb = t // ATTN_BLOCK

    def body(p_ref, kvp_ref, tab_ref, tabp_ref, sink_ref, o_ref):
        qs, ks, vs, sk = _attn_prepare(p_ref, kvp_ref, tab_ref, tabp_ref, sink_ref)
        os_ = _attn_heads(qs, ks, vs, sk, pl.program_id(0) == 0)
        o_ref[...] = _unstack_heads(os_).astype(BF16)

    return pl.pallas_call(
        body, name="attn_fwd",
        grid=(nb,),
        in_specs=_attn_specs(nb),
        out_specs=pl.BlockSpec((ATTN_BLOCK, ATTN_Q_W), lambda n: (n, 0)),
        out_shape=jax.ShapeDtypeStruct((t, ATTN_Q_W), BF16),
        compiler_params=_arb(1),
    )(p, p, tab, tab, sinks)


def attn_bwd(p, tab, sinks, do):
    t = p.shape[0]
    b = ATTN_BLOCK
    hd = ATTN_HEAD_DIM
    nb = t // b

    def body(p_ref, kvp_ref, tab_ref, tabp_ref, sink_ref, do_ref, dq_ref, dkvc_ref, dkvp_ref, dsink_ref):
        n = pl.program_id(0)

        @pl.when(n == 0)
        def _():
            dsink_ref[...] = jnp.zeros_like(dsink_ref)

        qs, ks, vs, sk = _attn_prepare(p_ref, kvp_ref, tab_ref, tabp_ref, sink_ref)
        first = n == 0
        _, vjp = jax.vjp(lambda a, bb, c, d: _attn_heads(a, bb, c, d, first), qs, ks, vs, sk)
        do = do_ref[...]
        dos = [jnp.concatenate([do[:, i * hd:(i + 1) * hd] for i in range(ATTN_GROUP * h, ATTN_GROUP * (h + 1))], axis=0)
               for h in range(ATTN_KV_HEADS)]
        dqs, dks, dvs, dsk = vjp(dos)
        tab, tabp = tab_ref[...], tabp_ref[...]
        dq_ref[...] = _rope(_unstack_heads([x.astype(F32) for x in dqs]), tab, -1.0).astype(BF16)
        dkc = jnp.concatenate([x.astype(F32)[b:] for x in dks], axis=1)
        dkp = jnp.concatenate([x.astype(F32)[:b] for x in dks], axis=1)
        dkvc_ref[:, 0:ATTN_KV_W] = _rope(dkc, tab, -1.0)
        dkvp_ref[:, 0:ATTN_KV_W] = _rope(dkp, tabp, -1.0)
        dkvc_ref[:, ATTN_KV_W:] = jnp.concatenate([x.astype(F32)[b:] for x in dvs], axis=1)
        dkvp_ref[:, ATTN_KV_W:] = jnp.concatenate([x.astype(F32)[:b] for x in dvs], axis=1)
        parts = [jnp.sum(d[g * b:(g + 1) * b], axis=0, keepdims=True) for d in dsk for g in range(ATTN_GROUP)]
        dsink_ref[...] += jnp.concatenate(parts, axis=1)

    blk = lambda w: pl.BlockSpec((b, w), lambda n: (n, 0))
    return pl.pallas_call(
        body, name="attn_bwd",
        grid=(nb,),
        in_specs=_attn_specs(nb) + [blk(ATTN_Q_W)],
        out_specs=[blk(ATTN_Q_W), blk(2 * ATTN_KV_W), blk(2 * ATTN_KV_W),
                   pl.BlockSpec((1, ATTN_Q_HEADS), lambda n: (0, 0))],
        out_shape=[jax.ShapeDtypeStruct((t, ATTN_Q_W), BF16), jax.ShapeDtypeStruct((t, 2 * ATTN_KV_W), F32),
                   jax.ShapeDtypeStruct((t, 2 * ATTN_KV_W), F32), jax.ShapeDtypeStruct((1, ATTN_Q_HEADS), F32)],
        compiler_params=_arb(1),
    )(p, p, tab, tab, sinks, do)


def kv_combine(dkvc, dkvp):
    t, w = dkvc.shape
    b = ATTN_BLOCK
    nb = t // b

    def body(c_ref, p_ref, o_ref):
        nxt = jnp.where(pl.program_id(0) < nb - 1, p_ref[...], 0.0)
        o_ref[...] = (c_ref[...] + nxt).astype(BF16)

    return pl.pallas_call(
        body, name="kv_combine",
        grid=(nb,),
        in_specs=[pl.BlockSpec((b, w), lambda n: (n, 0)),
                  pl.BlockSpec((b, w), lambda n: (jnp.minimum(n + 1, nb - 1), 0))],
        out_specs=pl.BlockSpec((b, w), lambda n: (n, 0)),
        out_shape=jax.ShapeDtypeStruct((t, w), BF16),
        compiler_params=_arb(1),
    )(dkvc, dkvp)


def _me():
    return lax.axis_index("x"), lax.axis_index("y"), lax.axis_index("c")


def _flip(v, bit):
    return 1 - v if bit else v


def allgather_chips(x):
    r, c = x.shape

    def body(x_ref, o_ref, send_sems, recv_sems, local_sem):
        mx, my, mc = _me()
        me = 2 * mx + my
        local = pltpu.make_async_copy(x_ref, o_ref.at[me], local_sem)
        local.start()
        copies = []
        for k in (1, 2, 3):
            peer = (_flip(mx, k >> 1), _flip(my, k & 1), mc)
            cp = pltpu.make_async_remote_copy(src_ref=x_ref, dst_ref=o_ref.at[me], send_sem=send_sems.at[k - 1],
                                              recv_sem=recv_sems.at[k - 1], device_id=peer, device_id_type=MESH)
            cp.start()
            copies.append(cp)
        for cp in copies:
            cp.wait()
        local.wait()

    return pl.pallas_call(
        body, name="allgather_chips",
        in_specs=[ANY], out_specs=ANY,
        out_shape=jax.ShapeDtypeStruct((4, r, c), x.dtype),
        scratch_shapes=[pltpu.SemaphoreType.DMA((3,)), pltpu.SemaphoreType.DMA((3,)), pltpu.SemaphoreType.DMA],
    )(x)


def pair_exchange(g):
    _, _, r, c = g.shape

    def body(g_ref, mine_ref, got_ref, send_sems, recv_sems, local_sems):
        mx, my, mc = _me()
        sib = (mx, my, 1 - mc)
        todo = []
        for j in range(4):
            lc = pltpu.make_async_copy(g_ref.at[j, mc], mine_ref.at[j], local_sems.at[j])
            lc.start()
            cp = pltpu.make_async_remote_copy(src_ref=g_ref.at[j, 1 - mc], dst_ref=got_ref.at[j], send_sem=send_sems.at[j],
                                              recv_sem=recv_sems.at[j], device_id=sib, device_id_type=MESH)
            cp.start()
            todo += [lc, cp]
        for cp in todo:
            cp.wait()

    return pl.pallas_call(
        body, name="pair_exchange",
        in_specs=[ANY], out_specs=[ANY, ANY],
        out_shape=[jax.ShapeDtypeStruct((4, r, c), g.dtype)] * 2,
        scratch_shapes=[pltpu.SemaphoreType.DMA((4,))] * 3,
    )(g)


def scatter_chips(p):
    _, r, c = p.shape

    def body(p_ref, o_ref, send_sems, recv_sems, local_sem):
        mx, my, mc = _me()
        me = 2 * mx + my
        local = pltpu.make_async_copy(p_ref.at[me], o_ref.at[me], local_sem)
        local.start()
        copies = []
        for k in (1, 2, 3):
            px, py = _flip(mx, k >> 1), _flip(my, k & 1)
            cp = pltpu.make_async_remote_copy(src_ref=p_ref.at[2 * px + py], dst_ref=o_ref.at[me],
                                              send_sem=send_sems.at[k - 1], recv_sem=recv_sems.at[k - 1],
                                              device_id=(px, py, mc), device_id_type=MESH)
            cp.start()
            copies.append(cp)
        for cp in copies:
            cp.wait()
        local.wait()

    return pl.pallas_call(
        body, name="scatter_chips",
        in_specs=[ANY], out_specs=ANY,
        out_shape=jax.ShapeDtypeStruct((4, r, c), p.dtype),
        scratch_shapes=[pltpu.SemaphoreType.DMA((3,)), pltpu.SemaphoreType.DMA((3,)), pltpu.SemaphoreType.DMA],
    )(p)


def pair_share(o):
    r, c = o.shape

    def body(o_ref, out_ref, send_sem, recv_sem, local_sem):
        mx, my, mc = _me()
        local = pltpu.make_async_copy(o_ref, out_ref.at[mc], local_sem)
        local.start()
        cp = pltpu.make_async_remote_copy(src_ref=o_ref, dst_ref=out_ref.at[mc], send_sem=send_sem, recv_sem=recv_sem,
                                          device_id=(mx, my, 1 - mc), device_id_type=MESH)
        cp.start()
        cp.wait()
        local.wait()

    return pl.pallas_call(
        body, name="pair_share",
        in_specs=[ANY], out_specs=ANY,
        out_shape=jax.ShapeDtypeStruct((2, r, c), o.dtype),
        scratch_shapes=[pltpu.SemaphoreType.DMA, pltpu.SemaphoreType.DMA, pltpu.SemaphoreType.DMA],
    )(o)


def _row_tile(r, c):
    tr = 8
    while r % (2 * tr) == 0 and 2 * tr * c * 4 <= (2 << 20):
        tr *= 2
    return tr


def add_n(xs):
    r, c = xs[0].shape
    tr = _row_tile(r, c)

    def body(*refs):
        acc = refs[0][...]
        for x_ref in refs[1:-1]:
            acc = acc + x_ref[...]
        refs[-1][...] = acc

    blk = pl.BlockSpec((tr, c), lambda i: (i, 0))
    return pl.pallas_call(
        body, name="add_n",
        grid=(r // tr,),
        in_specs=[blk] * len(xs), out_specs=blk,
        out_shape=jax.ShapeDtypeStruct((r, c), F32),
        compiler_params=_arb(1),
    )(*xs)


def add_slots(q):
    n, r, c = q.shape
    tr = _row_tile(r, c)

    def body(*refs):
        acc = refs[0][...]
        for x_ref in refs[1:-1]:
            acc = acc + x_ref[...]
        refs[-1][...] = acc

    return pl.pallas_call(
        body, name="add_slots",
        grid=(r // tr,),
        in_specs=[pl.BlockSpec((None, tr, c), functools.partial(lambda k, i: (k, i, 0), k)) for k in range(n)],
        out_specs=pl.BlockSpec((tr, c), lambda i: (i, 0)),
        out_shape=jax.ShapeDtypeStruct((r, c), F32),
        compiler_params=_arb(1),
    )(*([q] * n))


def allreduce_small(x):
    r, c = x.shape

    def body(x_ref, o_ref, buf, send_sems, recv_sems):
        mx, my, mc = _me()
        me = 4 * mx + 2 * my + mc
        buf[pl.ds(me, 1)] = x_ref[...][None]
        copies = []
        for k in range(1, 8):
            peer = (_flip(mx, k >> 2), _flip(my, (k >> 1) & 1), _flip(mc, k & 1))
            cp = pltpu.make_async_remote_copy(src_ref=x_ref, dst_ref=buf.at[me], send_sem=send_sems.at[k - 1],
                                              recv_sem=recv_sems.at[k - 1], device_id=peer, device_id_type=MESH)
            cp.start()
            copies.append(cp)
        for cp in copies:
            cp.wait()
        acc = buf[0]
        for d in range(1, 8):
            acc = acc + buf[d]
        o_ref[...] = acc

    return pl.pallas_call(
        body, name="allreduce_small",
        out_shape=jax.ShapeDtypeStruct((r, c), F32),
        scratch_shapes=[pltpu.VMEM((8, r, c), F32), pltpu.SemaphoreType.DMA((7,)), pltpu.SemaphoreType.DMA((7,))],
    )(x)


def reduce_scatter(g):
    _, r, c = g.shape
    rh = r // 2
    mine, got = pair_exchange(g.reshape(4, 2, rh, c))
    p = add_n([mine.reshape(4 * rh, c), got.reshape(4 * rh, c)]).reshape(4, rh, c)
    q = scatter_chips(p)
    o = add_slots(q)
    return pair_share(o).reshape(r, c)


PACK_W = 1024
PACK_ROWS = 1024

SHARDED = {"ffn1_w_gate_up": 1, "ffn1_w_down": 0, "ffn2_w_gate_up": 1, "ffn2_w_down": 0, "attn_w_in": 1,
           "attn_w_out": 0, "gdn_w_in": 1, "gdn_w_out": 0, "gdn_conv_w": 1}
REPLICATED = ["ffn1_norm", "mix_norm", "ffn2_norm", "attn_b_in", "attn_sinks", "attn_b_out", "gdn_A_log",
              "gdn_dt_bias", "gdn_norm_w", "final_norm"]
WEIGHTS = ["ffn1_norm", "ffn1_w_gate_up", "ffn1_w_down", "mix_norm", "ffn2_norm", "ffn2_w_gate_up", "ffn2_w_down",
           "attn_w_in", "attn_b_in", "attn_sinks", "attn_w_out", "attn_b_out", "gdn_w_in", "gdn_conv_w", "gdn_A_log",
           "gdn_dt_bias", "gdn_norm_w", "gdn_w_out", "final_norm"]


def _pack_rows(flats, dtype, width, row_multiple):
    flat = jnp.concatenate([f.astype(dtype).reshape(-1) for f in flats])
    per = width * row_multiple
    pad = (-flat.shape[0]) % per
    if pad:
        flat = jnp.concatenate([flat, jnp.zeros((pad,), dtype)])
    return flat.reshape(-1, width)


def _gather_weights(shards, dtype):
    names = list(shards)
    packed = _pack_rows([shards[n] for n in names], dtype, PACK_W, 16)
    got = allgather_chips(packed).reshape(4, -1)
    full, off = {}, 0
    for n in names:
        shape = shards[n].shape
        size = shards[n].size
        full[n] = jnp.concatenate([got[j, off:off + size].reshape(shape) for j in range(4)], axis=1 + SHARDED[n])
        off += size
    return full


def _scatter_grads(grads, shard_shapes):
    names = list(grads)
    slots = []
    for j in range(4):
        parts = []
        for n in names:
            ax = 1 + SHARDED[n]
            w = shard_shapes[n][ax]
            parts.append(lax.slice_in_dim(grads[n], j * w, (j + 1) * w, axis=ax))
        slots.append(_pack_rows(parts, F32, PACK_W, PACK_ROWS))
    red = reduce_scatter(jnp.stack(slots)).reshape(-1)
    out, off = {}, 0
    for n in names:
        size = 1
        for s in shard_shapes[n]:
            size *= s
        out[n] = red[off:off + size].reshape(shard_shapes[n])
        off += size
    return out


def _small_pack(items):
    rows = []
    for a in items:
        f = a.astype(F32).reshape(-1)
        pad = (-f.shape[0]) % LANE
        rows.append(jnp.concatenate([f, jnp.zeros((pad,), F32)]) if pad else f)
    return _pack_rows(rows, F32, LANE, 8)


def _small_unpack(buf, shapes):
    flat = buf.reshape(-1)
    out, off = [], 0
    for shape in shapes:
        size = 1
        for s in shape:
            size *= s
        out.append(flat[off:off + size].reshape(shape))
        off += size + (-size) % LANE
    return out


def _rope_table(positions):
    t = positions.shape[0]
    inv_freq = ROPE_THETA ** (-jnp.arange(0, ROPE_DIM, 2, dtype=F32) / ROPE_DIM)
    ang = positions.astype(F32)[:, None] * inv_freq
    cos, sin = jnp.cos(ang), jnp.sin(ang)
    rest = ATTN_HEAD_DIM - ROPE_DIM
    zeros = lambda n: jnp.zeros((t, n), F32)
    c64 = jnp.concatenate([cos, cos, jnp.ones((t, rest), F32)], axis=1)
    s_up = jnp.concatenate([zeros(ROPE_DIM // 2), sin, zeros(rest)], axis=1)
    s_dn = jnp.concatenate([-sin, zeros(ROPE_DIM // 2 + rest)], axis=1)
    return jnp.concatenate([c64, c64, s_up, s_up, s_dn, s_dn], axis=1)


def _as2d(a):
    return a.reshape(-1, a.shape[-1]) if a.ndim > 1 else a.reshape(1, -1)


def kernel(x, positions, ffn1_norm, ffn1_w_gate_up, ffn1_w_down, mix_norm, ffn2_norm, ffn2_w_gate_up, ffn2_w_down, attn_w_in, attn_b_in, attn_sinks, attn_w_out, attn_b_out, gdn_w_in, gdn_conv_w, gdn_A_log, gdn_dt_bias, gdn_norm_w, gdn_w_out, final_norm, loss_target, m_ffn1_norm, m_ffn1_w_gate_up, m_ffn1_w_down, m_mix_norm, m_ffn2_norm, m_ffn2_w_gate_up, m_ffn2_w_down, m_attn_w_in, m_attn_b_in, m_attn_sinks, m_attn_w_out, m_attn_b_out, m_gdn_w_in, m_gdn_conv_w, m_gdn_A_log, m_gdn_dt_bias, m_gdn_norm_w, m_gdn_w_out, m_final_norm, v_ffn1_norm, v_ffn1_w_gate_up, v_ffn1_w_down, v_mix_norm, v_ffn2_norm, v_ffn2_w_gate_up, v_ffn2_w_down, v_attn_w_in, v_attn_b_in, v_attn_sinks, v_attn_w_out, v_attn_b_out, v_gdn_w_in, v_gdn_conv_w, v_gdn_A_log, v_gdn_dt_bias, v_gdn_norm_w, v_gdn_w_out, v_final_norm):
    given = dict(locals())
    w = {n: given[n] for n in WEIGHTS}
    d = D_MODEL
    h = x[0]
    target = loss_target[0]
    depth = ffn1_norm.shape[0]

    big = [n for n in SHARDED if n != "gdn_conv_w"]
    full = _gather_weights({n: w[n] for n in big}, BF16)
    conv_full = _gather_weights({"gdn_conv_w": gdn_conv_w}, F32)["gdn_conv_w"]
    tab = _rope_table(positions[0])
    zero_d = jnp.zeros((1, d), F32)

    def gdn_params(j):
        w_in = full["gdn_w_in"][j]
        w_cat = jnp.concatenate([w_in, jnp.zeros((d, GDN_PROJ_W - w_in.shape[1]), BF16)], axis=1)
        conv = jnp.concatenate([conv_full[j], jnp.zeros((8 - GDN_CONV, GDN_QKV_W), F32)], axis=0)
        par = jnp.zeros((8, LANE), F32)
        par = par.at[0, GDN_HEADS:2 * GDN_HEADS].set(gdn_A_log[j]).at[1, GDN_HEADS:2 * GDN_HEADS].set(gdn_dt_bias[j])
        return w_cat, conv, par

    saved = []
    for l in range(depth):
        j = l // 2
        rec = {"h1": h}
        h = ffn_fwd(h, ffn1_norm[l][None], full["ffn1_w_gate_up"][l], full["ffn1_w_down"][l])
        rec["h2"] = h
        if l % 2 == 0:
            p = norm_proj(h, mix_norm[l][None], full["attn_w_in"][j], attn_b_in[j][None])
            o = attn_fwd(p, tab, attn_sinks[j][None])
            h = linear_residual(h, o, full["attn_w_out"][j], attn_b_out[j][None])
            rec.update(p=p, o=o)
        else:
            w_cat, conv, par = gdn_params(j)
            proj = norm_proj(h, mix_norm[l][None], w_cat, jnp.zeros((1, GDN_PROJ_W), F32))
            qkv, gc = gdn_pre_fwd(proj, conv, par)
            on, st = gdn_chunk_fwd(qkv, gc, proj, gdn_norm_w[j][None])
            h = linear_residual(h, on, full["gdn_w_out"][j], zero_d)
            rec.update(proj=proj, qkv=qkv, gc=gc, on=on, st=st, w_cat=w_cat, conv=conv, par=par)
        rec["h3"] = h
        h = ffn_fwd(h, ffn2_norm[l][None], full["ffn2_w_gate_up"][l], full["ffn2_w_down"][l])
        saved.append(rec)

    dh, loss_tile, d_final = loss_head(h, final_norm[None], target)

    g = {n: [None] * w[n].shape[0] for n in WEIGHTS if n != "final_norm"}
    for l in reversed(range(depth)):
        j = l // 2
        rec = saved[l]

        def ffn_back(tag, h_in, dh):
            wgu, wd = full[tag + "_w_gate_up"][l], full[tag + "_w_down"][l]
            dh_new, hn, a, dgu, dn = ffn_bwd(h_in, w[tag + "_norm"][l][None], wgu, wd, dh)
            g[tag + "_w_gate_up"][l] = matmul_tn(hn, dgu)
            g[tag + "_w_down"][l] = matmul_tn(a, dh, 0.5)
            g[tag + "_norm"][l] = dn[0]
            return dh_new

        dh = ffn_back("ffn2", rec["h3"], dh)
        if l % 2 == 0:
            w_in, w_out = full["attn_w_in"][j], full["attn_w_out"][j]
            do, db_out = matmul_nt(dh, w_out)
            g["attn_w_out"][j] = matmul_tn(rec["o"], dh)
            g["attn_b_out"][j] = db_out[0]
            dq, dkvc, dkvp, dsink = attn_bwd(rec["p"], tab, attn_sinks[j][None], do)
            dkv = kv_combine(dkvc, dkvp)
            dh, hn, dn, cs_q, cs_kv = norm_proj_bwd(rec["h2"], mix_norm[l][None], dh, [dq, dkv],
                                                    [w_in[:, :ATTN_Q_W], w_in[:, ATTN_Q_W:]])
            g["attn_w_in"][j] = jnp.concatenate([matmul_tn(hn, dq), matmul_tn(hn, dkv)], axis=1)
            g["attn_b_in"][j] = jnp.concatenate([cs_q[0], cs_kv[0]])
            g["attn_sinks"][j] = dsink[0]
        else:
            w_cat, conv, par = rec["w_cat"], rec["conv"], rec["par"]
            d_on, _ = matmul_nt(dh, full["gdn_w_out"][j])
            g["gdn_w_out"][j] = matmul_tn(rec["on"], dh)
            dq, dk, dv, dz, dgc_heads, dnw = gdn_chunk_bwd(rec["qkv"], rec["gc"], rec["proj"], gdn_norm_w[j][None],
                                                           rec["st"], d_on)
            dy, dba, dpar = gdn_pre_bwd(rec["proj"], conv, par, dq, dk, dv, dgc_heads)
            dx, dconv = gdn_conv_bwd(rec["proj"], conv, dy)
            nz = GDN_QKV_W + GDN_HEADS * GDN_DK
            dh, hn, dn, _, _, _ = norm_proj_bwd(rec["h2"], mix_norm[l][None], dh, [dx, dz, dba],
                                                [w_cat[:, :GDN_QKV_W], w_cat[:, GDN_QKV_W:nz], w_cat[:, nz:]])
            g["gdn_w_in"][j] = jnp.concatenate(
                [matmul_tn(hn, dx), matmul_tn(hn, dz), matmul_tn(hn, dba)[:, :2 * GDN_HEADS]], axis=1)
            g["gdn_conv_w"][j] = dconv[:GDN_CONV]
            g["gdn_A_log"][j] = dpar[0, GDN_HEADS:2 * GDN_HEADS]
            g["gdn_dt_bias"][j] = dpar[1, GDN_HEADS:2 * GDN_HEADS]
            g["gdn_norm_w"][j] = dnw[0]
        g["mix_norm"][l] = dn[0]
        dh = ffn_back("ffn1", rec["h1"], dh)
    grad_x = dh[None]

    local = {n: jnp.stack(v) for n, v in g.items()}
    local["final_norm"] = d_final[0]
    small_shapes = [(1,)] + [w[n].shape for n in REPLICATED]
    small = allreduce_small(_small_pack([loss_tile[0, 0:1]] + [local[n] for n in REPLICATED]))
    small = _small_unpack(small, small_shapes)
    loss = small[0][0]
    grads = dict(zip(REPLICATED, small[1:]))
    grads.update(_scatter_grads({n: local[n] for n in SHARDED}, {n: w[n].shape for n in SHARDED}))

    delta, new_m, new_v = {}, {}, {}
    for n in SHARDED:
        dl, nm, nv = adamw(_as2d(w[n]), _as2d(grads[n]), _as2d(given["m_" + n]), _as2d(given["v_" + n]))
        delta[n], new_m[n], new_v[n] = dl.reshape(w[n].shape), nm.reshape(w[n].shape), nv.reshape(w[n].shape)
    shapes = [w[n].shape for n in REPLICATED]
    packed = [_small_pack([src[n] for n in REPLICATED]) for src in
              (w, grads, {n: given["m_" + n] for n in REPLICATED}, {n: given["v_" + n] for n in REPLICATED})]
    for dst, buf in zip((delta, new_m, new_v), adamw(*packed)):
        dst.update(zip(REPLICATED, _small_unpack(buf, shapes)))

    return (loss, grad_x, *[grads[n] for n in WEIGHTS], *[delta[n] for n in WEIGHTS],
            *[new_m[n] for n in WEIGHTS], *[new_v[n] for n in WEIGHTS])
```

```python
import functools

import jax
import jax.numpy as jnp
from jax import lax
from jax.experimental import pallas as pl
from jax.experimental.pallas import tpu as pltpu

F32 = jnp.float32
BF16 = jnp.bfloat16
HI = lax.Precision.HIGHEST
MESH = pl.DeviceIdType.MESH

D_MODEL = 1024
D_FF = 2816
DEPTH = 4
NORM_EPS = 1e-6
LANE = 128

ATTN_Q_HEADS = 16
ATTN_KV_HEADS = 4
ATTN_HEAD_DIM = 64
ATTN_GROUP = 4
ATTN_BLOCK = 128
ROPE_DIM = 16
ROPE_THETA = 500000.0
ATTN_Q_W = 1024
ATTN_KV_W = 256

GDN_HEADS = 8
GDN_DK = 128
GDN_CONV = 4
GDN_CHUNK = 64
GDN_QKV_W = 3072

ADAM_LR = 0.001
ADAM_B1 = 0.9
ADAM_B2 = 0.999
ADAM_EPS = 1e-08
ADAM_WD = 0.01
ADAM_STEP = 10

NEG = -1e30


def _dot(a, b, prec=None):
    return lax.dot_general(a, b, (((1,), (0,)), ((), ())), precision=prec, preferred_element_type=F32)


def _dot_nt(a, b, prec=None):
    return lax.dot_general(a, b, (((1,), (1,)), ((), ())), precision=prec, preferred_element_type=F32)


def _dot_tn(a, b, prec=None):
    return lax.dot_general(a, b, (((0,), (0,)), ((), ())), precision=prec, preferred_element_type=F32)


def _bdot(a, b):
    return _dot(a.astype(BF16), b.astype(BF16))


def _bdot_nt(a, b):
    return _dot_nt(a.astype(BF16), b.astype(BF16))


def _bdot_tn(a, b):
    return _dot_tn(a.astype(BF16), b.astype(BF16))


def _sigmoid(x):
    return 1.0 / (1.0 + jnp.exp(-x))


def _silu(x):
    return x * _sigmoid(x)


def _silu_grad(x):
    s = _sigmoid(x)
    return s * (1.0 + x * (1.0 - s))


def _rms(x, w):
    r = lax.rsqrt(jnp.mean(x * x, axis=-1, keepdims=True) + NORM_EPS)
    xhat = x * r
    return xhat * w, xhat, r


def _rms_bwd(dy, w, xhat, r):
    dxhat = dy * w
    dx = r * (dxhat - xhat * jnp.mean(dxhat * xhat, axis=-1, keepdims=True))
    dw = jnp.sum(dy * xhat, axis=0, keepdims=True)
    return dx, dw


def _arb(n):
    return pltpu.CompilerParams(dimension_semantics=("arbitrary",) * n)


def _tile(n, want):
    t = min(n, want)
    assert n % t == 0, (n, want)
    return t


def _iota2(shape, dim):
    return lax.broadcasted_iota(jnp.int32, shape, dim)


_NN = (((1,), (0,)), ((), ()))
_NT = (((1,), (1,)), ((), ()))
_TN = (((0,), (0,)), ((), ()))


def _raw1(a, b, dn):
    return lax.dot_general(a.astype(BF16), b.astype(BF16), dn, preferred_element_type=F32)


def _raw3(a, b, dn):
    ah, bh = a.astype(BF16), b.astype(BF16)
    al, bl = (a - ah.astype(F32)).astype(BF16), (b - bh.astype(F32)).astype(BF16)
    f = lambda x, y: lax.dot_general(x, y, dn, preferred_element_type=F32)
    return f(ah, bh) + f(ah, bl) + f(al, bh)


def _make_mm(raw):
    @jax.custom_vjp
    def mm(a, b):
        return raw(a, b, _NN)

    mm.defvjp(lambda a, b: (raw(a, b, _NN), (a, b)),
              lambda res, ct: (raw(ct, res[1], _NT), raw(res[0], ct, _TN)))

    @jax.custom_vjp
    def mm_nt(a, b):
        return raw(a, b, _NT)

    mm_nt.defvjp(lambda a, b: (raw(a, b, _NT), (a, b)),
                 lambda res, ct: (raw(ct, res[1], _NN), raw(ct, res[0], _TN)))
    return mm, mm_nt


_mm1, _mm1_nt = _make_mm(_raw1)
_mm3, _mm3_nt = _make_mm(_raw3)


def _eye(n):
    return (_iota2((n, n), 0) == _iota2((n, n), 1)).astype(F32)


def _inv_newton(a):
    eye = _eye(a.shape[0])
    n = -a
    m = eye + n
    p = n
    k = 1
    while 2 * k < GDN_CHUNK:
        p = _raw1(p, p, _NN)
        m = m + _raw1(m, p, _NN)
        k *= 2
    r = eye - m - _raw3(a, m, _NN)
    return m + _raw1(m, r, _NN)


@jax.custom_vjp
def _unit_lower_inv(a):
    return _inv_newton(a)


def _unit_lower_inv_fwd(a):
    m = _inv_newton(a)
    return m, m


def _unit_lower_inv_bwd(m, dm):
    return (-_raw3(_raw3(m, dm, _TN), m, _NT),)


_unit_lower_inv.defvjp(_unit_lower_inv_fwd, _unit_lower_inv_bwd)


def _gdn_local(q, k, v, dcb, dcb128, dlb128, bb128):
    n = q.shape[0]
    r, s = _iota2((n, n), 0), _iota2((n, n), 1)
    same = (r // GDN_CHUNK) == (s // GDN_CHUNK)
    causal, strict = same & (r >= s), same & (r > s)
    decay_l = jnp.exp(jnp.where(causal, dcb - dcb.T, NEG))
    kb = k * bb128
    a = jnp.where(strict, _mm1_nt(kb, k) * decay_l, 0.0)
    m_off = _unit_lower_inv(a) - _eye(n)
    edc = jnp.exp(dcb128)
    rhs = jnp.concatenate([v * bb128, kb * edc], axis=1)
    sol = rhs + _mm3(m_off, rhs)
    aqk = jnp.where(causal, _mm1_nt(q, k) * decay_l, 0.0)
    return sol, aqk, q * edc, k * jnp.exp(dlb128 - dcb128)


def _gated_norm(o, z, nw):
    r = lax.rsqrt(jnp.mean(o * o, axis=-1, keepdims=True) + NORM_EPS)
    return o * r * nw * _silu(z)


def _gated_norm_bwd(dy, o, z, nw):
    r = lax.rsqrt(jnp.mean(o * o, axis=-1, keepdims=True) + NORM_EPS)
    xhat = o * r
    sz = _silu(z)
    dxhat = dy * nw * sz
    do = r * (dxhat - xhat * jnp.mean(dxhat * xhat, axis=-1, keepdims=True))
    dz = dy * xhat * nw * _silu_grad(z)
    dnw = jnp.sum(dy * xhat * sz, axis=0, keepdims=True)
    return do, dz, dnw


GDN_HEAD_GROUP = 2
GC_BETA, GC_G, GC_DECAY, GC_LAST =0, GDN_HEADS, 2 * GDN_HEADS, 3 * GDN_HEADS


def _gate_cols(gc, h, rows):
    lane = _iota2((rows, LANE), 1)
    col = lambda off: jnp.sum(jnp.where(lane == off + h, gc, 0.0), axis=-1, keepdims=True)
    return col(GC_BETA), col(GC_DECAY), col(GC_LAST)


def _gdn_local_args(q_ref, k_ref, v_ref, gc_ref, h, n):
    beta, dc, dl = _gate_cols(gc_ref[...], h, n)
    args = (q_ref[...], k_ref[...], v_ref[...], jnp.broadcast_to(dc, (n, n)), jnp.broadcast_to(dc, (n, LANE)),
            jnp.broadcast_to(dl, (n, LANE)), jnp.broadcast_to(beta, (n, LANE)))
    return args, dl


def gdn_chunk_fwd(qkv, gc, proj, norm_w):
    t = qkv.shape[0]
    c = GDN_CHUNK
    tc = _tile(t, 256)
    nsub = tc // c

    hp = GDN_HEAD_GROUP

    def body(q_ref, k_ref, v_ref, gc_ref, z_ref, nw_ref, on_ref, st_ref, s_ref):
        @pl.when(pl.program_id(1) == 0)
        def _():
            s_ref[...] = jnp.zeros_like(s_ref)

        for hh in range(hp):
            h = pl.program_id(0) * hp + hh
            ls = slice(hh * LANE, (hh + 1) * LANE)
            args, dl = _gdn_local_args(q_ref.at[:, ls], k_ref.at[:, ls], v_ref.at[:, ls], gc_ref, h, tc)
            sol, aqk, q_dec, k_dec = _gdn_local(*args)
            u, w = sol[:, :GDN_DK], sol[:, GDN_DK:]
            s = s_ref[hh]
            v_new, o_state = [], []
            for j in range(nsub):
                sl = slice(j * c, (j + 1) * c)
                st_ref[hh, j] = s
                v_new.append(u[sl] - _bdot(w[sl], s))
                o_state.append(_bdot(q_dec[sl], s))
                s = s * jnp.exp(dl[j * c:j * c + 1]) + _bdot_tn(k_dec[sl], v_new[j])
            s_ref[hh] = s
            o = jnp.concatenate(o_state, axis=0) + _bdot(aqk, jnp.concatenate(v_new, axis=0))
            on_ref[:, ls] = _gated_norm(o, z_ref[:, ls], nw_ref[...]).astype(BF16)

    col = lambda off: pl.BlockSpec((tc, hp * LANE), lambda g, i: (i, off // hp + g))
    return pl.pallas_call(
        body, name="gdn_chunk_fwd",
        grid=(GDN_HEADS // hp, t // tc),
        in_specs=[col(0), col(GDN_HEADS), col(2 * GDN_HEADS),
                  pl.BlockSpec((tc, LANE), lambda g, i: (i, 0)),
                  col(GDN_QKV_W // LANE),
                  pl.BlockSpec((1, LANE), lambda g, i: (0, 0))],
        out_specs=[col(0),
                   pl.BlockSpec((hp, nsub, GDN_DK, GDN_DK), lambda g, i: (g, i, 0, 0))],
        out_shape=[jax.ShapeDtypeStruct((t, GDN_HEADS * GDN_DK), BF16),
                   jax.ShapeDtypeStruct((GDN_HEADS, t // c, GDN_DK, GDN_DK), F32)],
        scratch_shapes=[pltpu.VMEM((hp, GDN_DK, GDN_DK), F32)],
        compiler_params=_arb(2),
    )(qkv, qkv, qkv, gc, proj, norm_w)


def gdn_chunk_bwd(qkv, gc, proj, norm_w, states, d_on):
    t = qkv.shape[0]
    c = GDN_CHUNK
    tc = _tile(t, 256)
    nsub = tc // c
    nblk = t // tc
    hp = GDN_HEAD_GROUP

    def body(q_ref, k_ref, v_ref, gc_ref, z_ref, nw_ref, st_ref, don_ref,
             dq_ref, dk_ref, dv_ref, dz_ref, dgc_ref, dnw_ref, ds_ref):
        @pl.when(pl.program_id(1) == 0)
        def _():
            ds_ref[...] = jnp.zeros_like(ds_ref)

        @pl.when((pl.program_id(1) == 0) & (pl.program_id(0) == 0))
        def _():
            dnw_ref[...] = jnp.zeros_like(dnw_ref)

        rows = [slice(j * c, (j + 1) * c) for j in range(nsub)]
        cat = lambda xs: jnp.concatenate(xs, axis=0)
        lsum = lambda x: jnp.sum(x, axis=-1, keepdims=True)
        lane = _iota2((tc, LANE), 1)
        row = _iota2((tc, 1), 0)
        dgc = jnp.zeros((tc, LANE), F32)
        for hh in range(hp):
            h = pl.program_id(0) * hp + hh
            ls = slice(hh * LANE, (hh + 1) * LANE)
            args, dl = _gdn_local_args(q_ref.at[:, ls], k_ref.at[:, ls], v_ref.at[:, ls], gc_ref, h, tc)
            (sol, aqk, q_dec, k_dec), vjp = jax.vjp(_gdn_local, *args)
            u, w = sol[:, :GDN_DK], sol[:, GDN_DK:]
            states = [st_ref[hh, j] for j in range(nsub)]
            v_new = [u[sl] - _bdot(w[sl], s) for sl, s in zip(rows, states)]
            v_all = cat(v_new)
            o = cat([_bdot(q_dec[sl], s) for sl, s in zip(rows, states)]) + _bdot(aqk, v_all)
            do, dz, dnw = _gated_norm_bwd(don_ref[:, ls], o, z_ref[:, ls], nw_ref[...])
            dnw_ref[...] += dnw
            dz_ref[:, ls] = dz.astype(BF16)
            d_aqk = _bdot_nt(do, v_all)
            dv_o = _bdot_tn(aqk, do)
            ds = ds_ref[hh]
            d_u, d_w, d_qdec, d_kdec, d_last = ([None] * nsub for _ in range(5))
            for j in reversed(range(nsub)):
                sl, s = rows[j], states[j]
                cd = jnp.exp(dl[j * c:j * c + 1])
                d_u[j] = dv_o[sl] + _bdot(k_dec[sl], ds)
                d_qdec[j] = _bdot_nt(do[sl], s)
                d_kdec[j] = _bdot_nt(v_new[j], ds)
                d_last[j] = jnp.sum(lsum(s * ds), axis=0, keepdims=True) * cd
                d_w[j] = -_bdot_nt(d_u[j], s)
                ds = _bdot_tn(q_dec[sl], do[sl]) + cd * ds - _bdot_tn(w[sl], d_u[j])
            ds_ref[hh] = ds
            d_sol = jnp.concatenate([cat(d_u), cat(d_w)], axis=1)
            dq, dk, dv, d_dcb, d_dcb128, d_dlb128, d_bb = vjp((d_sol, d_aqk, cat(d_qdec), cat(d_kdec)))
            dq_ref[:, ls] = dq
            dk_ref[:, ls] = dk
            dv_ref[:, ls] = dv
            d_dl = lsum(d_dlb128)
            for j in range(nsub):
                d_dl = d_dl + jnp.where(row == j * c, d_last[j], 0.0)
            dgc = dgc + jnp.where(lane == GC_BETA + h, lsum(d_bb),
                                  jnp.where(lane == GC_DECAY + h, lsum(d_dcb) + lsum(d_dcb128),
                                            jnp.where(lane == GC_LAST + h, d_dl, 0.0)))
        dgc_ref[0] = dgc

    rev = lambda i: nblk - 1 - i
    col = lambda off: pl.BlockSpec((tc, hp * LANE), lambda g, i: (rev(i), off // hp + g))
    return pl.pallas_call(
        body, name="gdn_chunk_bwd",
        grid=(GDN_HEADS // hp, nblk),
        in_specs=[col(0), col(GDN_HEADS), col(2 * GDN_HEADS),
                  pl.BlockSpec((tc, LANE), lambda g, i: (rev(i), 0)),
                  col(GDN_QKV_W // LANE),
                  pl.BlockSpec((1, LANE), lambda g, i: (0, 0)),
                  pl.BlockSpec((hp, nsub, GDN_DK, GDN_DK), lambda g, i: (g, rev(i), 0, 0)),
                  col(0)],
        out_specs=[col(0), col(0), col(0), col(0),
                   pl.BlockSpec((1, tc, LANE), lambda g, i: (g, rev(i), 0)),
                   pl.BlockSpec((1, LANE), lambda g, i: (0, 0))],
        out_shape=[
            jax.ShapeDtypeStruct((t, GDN_HEADS * GDN_DK), F32),
            jax.ShapeDtypeStruct((t, GDN_HEADS * GDN_DK), F32),
            jax.ShapeDtypeStruct((t, GDN_HEADS * GDN_DK), F32),
            jax.ShapeDtypeStruct((t, GDN_HEADS * GDN_DK), BF16),
            jax.ShapeDtypeStruct((GDN_HEADS // hp, t, LANE), F32),
            jax.ShapeDtypeStruct((1, LANE), F32)],
        scratch_shapes=[pltpu.VMEM((hp, GDN_DK, GDN_DK), F32)],
        compiler_params=_arb(2),
    )(qkv, qkv, qkv, gc, proj, norm_w, states, d_on)


GDN_PROJ_W = GDN_QKV_W + 1024 + LANE
GDN_Q_SCALE = GDN_DK ** -0.5


def _shift_rows(ext, shift, lo, n):
    if shift == 0:
        return ext[lo:lo + n]
    return pltpu.roll(ext, shift, 0)[lo:lo + n]


def _conv_fwd(x, halo, w):
    n = x.shape[0]
    ext = jnp.concatenate([halo, x], axis=0)
    y = w[GDN_CONV - 1:GDN_CONV] * x
    for j in range(GDN_CONV - 1):
        y = y + w[j:j + 1] * _shift_rows(ext, GDN_CONV - 1 - j, 8, n)
    return y


def _chunk_masks(n):
    r, s = _iota2((n, n), 0), _iota2((n, n), 1)
    same = (r // GDN_CHUNK) == (s // GDN_CHUNK)
    return (same & (r >= s)).astype(F32), (same & (r <= s)).astype(F32), same.astype(F32)


def _softplus(x):
    return jnp.maximum(x, 0.0) + jnp.log(1.0 + jnp.exp(-jnp.abs(x)))


def _l2n(t):
    rs = lax.rsqrt(jnp.sum(t * t, axis=-1, keepdims=True) + NORM_EPS)
    return t * rs, rs


def gdn_pre_fwd(proj, conv_w, gate_par):
    t = proj.shape[0]
    tm = _tile(t, 256)

    def body(x_ref, halo_ref, ba_ref, w_ref, gp_ref, qkv_ref, gc_ref):
        i = pl.program_id(0)
        halo = jnp.where(i > 0, halo_ref[...], 0.0)
        y = _silu(_conv_fwd(x_ref[...], halo, w_ref[...]))
        for hh in range(2 * GDN_HEADS):
            sl = slice(hh * LANE, (hh + 1) * LANE)
            tn, _ = _l2n(y[:, sl])
            qkv_ref[:, sl] = tn * GDN_Q_SCALE if hh < GDN_HEADS else tn
        qkv_ref[:, 2 * GDN_HEADS * LANE:] = y[:, 2 * GDN_HEADS * LANE:]
        ba = ba_ref[...]
        lane = _iota2(ba.shape, 1)
        gp = gp_ref[...]
        is_a = (lane >= GC_G) & (lane < GC_G + GDN_HEADS)
        g = jnp.where(is_a, -jnp.exp(gp[0:1]) * _softplus(ba + gp[1:2]), 0.0)
        tri, _, same = _chunk_masks(tm)
        decay = pltpu.roll(_dot(tri, g, HI), GC_DECAY - GC_G, 1)
        last = pltpu.roll(_dot(same, g, HI), GC_LAST - GC_G, 1)
        gc_ref[...] = jnp.where(lane < GDN_HEADS, _sigmoid(ba), g) + decay + last

    return pl.pallas_call(
        body, name="gdn_pre_fwd",
        grid=(t // tm,),
        in_specs=[pl.BlockSpec((tm, GDN_QKV_W), lambda i: (i, 0)),
                  pl.BlockSpec((8, GDN_QKV_W), lambda i: (jnp.maximum(i * (tm // 8) - 1, 0), 0)),
                  pl.BlockSpec((tm, LANE), lambda i: (i, (GDN_QKV_W + 1024) // LANE)),
                  pl.BlockSpec((8, GDN_QKV_W), lambda i: (0, 0)),
                  pl.BlockSpec((8, LANE), lambda i: (0, 0))],
        out_specs=[pl.BlockSpec((tm, GDN_QKV_W), lambda i: (i, 0)),
                   pl.BlockSpec((tm, LANE), lambda i: (i, 0))],
        out_shape=[jax.ShapeDtypeStruct((t, GDN_QKV_W), F32), jax.ShapeDtypeStruct((t, LANE), F32)],
        compiler_params=_arb(1),
    )(proj, proj, proj, conv_w, gate_par)


def gdn_pre_bwd(proj, conv_w, gate_par, dq, dk, dv, dgc_heads):
    t = proj.shape[0]
    tm = _tile(t, 256)

    def body(x_ref, halo_ref, ba_ref, w_ref, gp_ref, dq_ref, dk_ref, dv_ref, dgc_ref, dy_ref, dba_ref, dgp_ref):
        i = pl.program_id(0)

        @pl.when(i == 0)
        def _():
            dgp_ref[...] = jnp.zeros_like(dgp_ref)

        halo = jnp.where(i > 0, halo_ref[...], 0.0)
        y = _conv_fwd(x_ref[...], halo, w_ref[...])
        for hh in range(3 * GDN_HEADS):
            sl = slice(hh * LANE, (hh + 1) * LANE)
            hsl = slice((hh % GDN_HEADS) * LANE, (hh % GDN_HEADS + 1) * LANE)
            yy = y[:, sl]
            if hh < 2 * GDN_HEADS:
                tn, rs = _l2n(_silu(yy))
                dtn = dq_ref[:, hsl] * GDN_Q_SCALE if hh < GDN_HEADS else dk_ref[:, hsl]
                dsil = rs * (dtn - tn * jnp.sum(dtn * tn, axis=-1, keepdims=True))
            else:
                dsil = dv_ref[:, hsl]
            dy_ref[:, sl] = dsil * _silu_grad(yy)
        dgc = dgc_ref[0]
        for hh in range(1, dgc_heads.shape[0]):
            dgc = dgc + dgc_ref[hh]
        ba = ba_ref[...]
        lane = _iota2(ba.shape, 1)
        _, tri_t, same = _chunk_masks(tm)
        d_decay = jnp.where((lane >= GC_DECAY) & (lane < GC_DECAY + GDN_HEADS), dgc, 0.0)
        d_last = jnp.where((lane >= GC_LAST) & (lane < GC_LAST + GDN_HEADS), dgc, 0.0)
        dgc = (jnp.where(lane < GDN_HEADS, dgc, 0.0) + pltpu.roll(_dot(tri_t, d_decay, HI), LANE - (GC_DECAY - GC_G), 1)
               + pltpu.roll(_dot(same, d_last, HI), LANE - (GC_LAST - GC_G), 1))
        gp = gp_ref[...]
        xg = ba + gp[1:2]
        ea = jnp.exp(gp[0:1])
        sp = _softplus(xg)
        sb = _sigmoid(ba)
        is_b = lane < GDN_HEADS
        is_a = (lane >= GDN_HEADS) & (lane < 2 * GDN_HEADS)
        d_pre = jnp.where(is_a, dgc * (-ea) * _sigmoid(xg), 0.0)
        dba_ref[...] = jnp.where(is_b, dgc * sb * (1.0 - sb), d_pre).astype(BF16)
        d_alog = jnp.sum(jnp.where(is_a, dgc * (-ea) * sp, 0.0), axis=0, keepdims=True)
        d_dtb = jnp.sum(d_pre, axis=0, keepdims=True)
        row = _iota2((8, LANE), 0)
        dgp_ref[...] += jnp.where(row == 0, d_alog, jnp.where(row == 1, d_dtb, 0.0))

    hspec = pl.BlockSpec((tm, GDN_HEADS * LANE), lambda i: (i, 0))
    return pl.pallas_call(
        body, name="gdn_pre_bwd",
        grid=(t // tm,),
        in_specs=[pl.BlockSpec((tm, GDN_QKV_W), lambda i: (i, 0)),
                  pl.BlockSpec((8, GDN_QKV_W), lambda i: (jnp.maximum(i * (tm // 8) - 1, 0), 0)),
                  pl.BlockSpec((tm, LANE), lambda i: (i, (GDN_QKV_W + 1024) // LANE)),
                  pl.BlockSpec((8, GDN_QKV_W), lambda i: (0, 0)),
                  pl.BlockSpec((8, LANE), lambda i: (0, 0)),
                  hspec, hspec, hspec,
                  pl.BlockSpec((dgc_heads.shape[0], tm, LANE), lambda i: (0, i, 0))],
        out_specs=[pl.BlockSpec((tm, GDN_QKV_W), lambda i: (i, 0)),
                   pl.BlockSpec((tm, LANE), lambda i: (i, 0)),
                   pl.BlockSpec((8, LANE), lambda i: (0, 0))],
        out_shape=[jax.ShapeDtypeStruct((t, GDN_QKV_W), F32), jax.ShapeDtypeStruct((t, LANE), BF16),
                   jax.ShapeDtypeStruct((8, LANE), F32)],
        compiler_params=_arb(1),
    )(proj, proj, proj, conv_w, gate_par, dq, dk, dv, dgc_heads)


def gdn_conv_bwd(proj, conv_w, dy):
    t = proj.shape[0]
    tm = _tile(t, 256)
    nblk = t // tm

    def body(x_ref, halo_ref, w_ref, dy_ref, dyn_ref, dx_ref, dw_ref):
        i = pl.program_id(0)

        @pl.when(i == 0)
        def _():
            dw_ref[...] = jnp.zeros_like(dw_ref)

        w = w_ref[...]
        dy = dy_ref[...]
        ext_dy = jnp.concatenate([dy, jnp.where(i < nblk - 1, dyn_ref[...], 0.0)], axis=0)
        ext_x = jnp.concatenate([jnp.where(i > 0, halo_ref[...], 0.0), x_ref[...]], axis=0)
        dx = w[GDN_CONV - 1:GDN_CONV] * dy
        rows = [jnp.sum(dy * x_ref[...], axis=0, keepdims=True)]
        for j in range(GDN_CONV - 1):
            sh = GDN_CONV - 1 - j
            dx = dx + w[j:j + 1] * _shift_rows(ext_dy, tm + 8 - sh, 0, tm)
            rows.insert(j, jnp.sum(dy * _shift_rows(ext_x, sh, 8, tm), axis=0, keepdims=True))
        dx_ref[...] = dx.astype(BF16)
        row = _iota2((8, GDN_QKV_W), 0)
        acc = jnp.zeros((8, GDN_QKV_W), F32)
        for j in range(GDN_CONV):
            acc = acc + jnp.where(row == j, rows[j], 0.0)
        dw_ref[...] += acc

    return pl.pallas_call(
        body, name="gdn_conv_bwd",
        grid=(nblk,),
        in_specs=[pl.BlockSpec((tm, GDN_QKV_W), lambda i: (i, 0)),
                  pl.BlockSpec((8, GDN_QKV_W), lambda i: (jnp.maximum(i * (tm // 8) - 1, 0), 0)),
                  pl.BlockSpec((8, GDN_QKV_W), lambda i: (0, 0)),
                  pl.BlockSpec((tm, GDN_QKV_W), lambda i: (i, 0)),
                  pl.BlockSpec((8, GDN_QKV_W), lambda i: (jnp.minimum((i + 1) * (tm // 8), t // 8 - 1), 0))],
        out_specs=[pl.BlockSpec((tm, GDN_QKV_W), lambda i: (i, 0)),
                   pl.BlockSpec((8, GDN_QKV_W), lambda i: (0, 0))],
        out_shape=[jax.ShapeDtypeStruct((t, GDN_QKV_W), BF16), jax.ShapeDtypeStruct((8, GDN_QKV_W), F32)],
        compiler_params=_arb(1),
    )(proj, proj, conv_w, dy, dy)


def _resident(w_hbm, w_vmem, sem):
    @pl.when(pl.program_id(0) == 0)
    def _():
        cp = pltpu.make_async_copy(w_hbm, w_vmem, sem)
        cp.start()
        cp.wait()


ANY = pl.BlockSpec(memory_space=pl.ANY)


def norm_proj(h, nw, w, bias):
    t, d = h.shape
    n = w.shape[1]
    tm = _tile(t, 256)
    nc = _tile(n, 1536) if n % 1536 == 0 else _tile(n, 1408)

    def body(h_ref, nw_ref, w_hbm, b_ref, o_ref, w_ref, sem):
        _resident(w_hbm, w_ref, sem)
        hn = _rms(h_ref[...], nw_ref[...])[0].astype(BF16)
        for c0 in range(0, n, nc):
            o_ref[:, c0:c0 + nc] = _dot(hn, w_ref[:, c0:c0 + nc]) + b_ref[:, c0:c0 + nc]

    return pl.pallas_call(
        body, name="norm_proj",
        grid=(t // tm,),
        in_specs=[pl.BlockSpec((tm, d), lambda i: (i, 0)), pl.BlockSpec((1, d), lambda i: (0, 0)), ANY,
                  pl.BlockSpec((1, n), lambda i: (0, 0))],
        out_specs=pl.BlockSpec((tm, n), lambda i: (i, 0)),
        out_shape=jax.ShapeDtypeStruct((t, n), F32),
        scratch_shapes=[pltpu.VMEM((d, n), BF16), pltpu.SemaphoreType.DMA],
        compiler_params=_arb(1),
    )(h, nw, w, bias)


def linear_residual(h, x, w, bias):
    t, d = h.shape
    k = x.shape[1]
    tm = _tile(t, 512)

    def body(h_ref, x_ref, w_hbm, b_ref, o_ref, w_ref, sem):
        _resident(w_hbm, w_ref, sem)
        o_ref[...] = h_ref[...] + _dot(x_ref[...], w_ref[...]) + b_ref[...]

    return pl.pallas_call(
        body, name="linear_residual",
        grid=(t // tm,),
        in_specs=[pl.BlockSpec((tm, d), lambda i: (i, 0)), pl.BlockSpec((tm, k), lambda i: (i, 0)), ANY,
                  pl.BlockSpec((1, d), lambda i: (0, 0))],
        out_specs=pl.BlockSpec((tm, d), lambda i: (i, 0)),
        out_shape=jax.ShapeDtypeStruct((t, d), F32),
        scratch_shapes=[pltpu.VMEM((k, d), BF16), pltpu.SemaphoreType.DMA],
        compiler_params=_arb(1),
    )(h, x, w, bias)


def matmul_nt(dy, w):
    t, d = dy.shape
    k = w.shape[0]
    tm = _tile(t, 512)

    def body(dy_ref, w_hbm, o_ref, cs_ref, w_ref, sem):
        _resident(w_hbm, w_ref, sem)

        @pl.when(pl.program_id(0) == 0)
        def _():
            cs_ref[...] = jnp.zeros_like(cs_ref)

        dy = dy_ref[...]
        cs_ref[...] += jnp.sum(dy, axis=0, keepdims=True)
        o_ref[...] = _dot_nt(dy.astype(BF16), w_ref[...])

    return pl.pallas_call(
        body, name="matmul_nt",
        grid=(t // tm,),
        in_specs=[pl.BlockSpec((tm, d), lambda i: (i, 0)), ANY],
        out_specs=[pl.BlockSpec((tm, k), lambda i: (i, 0)), pl.BlockSpec((1, d), lambda i: (0, 0))],
        out_shape=[jax.ShapeDtypeStruct((t, k), F32), jax.ShapeDtypeStruct((1, d), F32)],
        scratch_shapes=[pltpu.VMEM((k, d), BF16), pltpu.SemaphoreType.DMA],
        compiler_params=_arb(1),
    )(dy, w)


def norm_proj_bwd(h, nw, dh, dps, ws):
    t, d = h.shape
    np_ = len(dps)
    ns = [w.shape[1] for w in ws]
    tm = _tile(t, 256)

    def body(*refs):
        h_ref, nw_ref, dh_ref = refs[:3]
        dp_refs = refs[3:3 + np_]
        w_hbms = refs[3 + np_:3 + 2 * np_]
        o_ref, hn_ref, dnw_ref = refs[3 + 2 * np_:6 + 2 * np_]
        cs_refs = refs[6 + 2 * np_:6 + 3 * np_]
        w_refs = refs[6 + 3 * np_:6 + 4 * np_]
        sem = refs[6 + 4 * np_]
        for a, b in zip(w_hbms, w_refs):
            _resident(a, b, sem)

        @pl.when(pl.program_id(0) == 0)
        def _():
            dnw_ref[...] = jnp.zeros_like(dnw_ref)
            for c in cs_refs:
                c[...] = jnp.zeros_like(c)

        nw = nw_ref[...]
        hn, xhat, r = _rms(h_ref[...], nw)
        hn_ref[...] = hn.astype(BF16)
        dhn = jnp.zeros((tm, d), F32)
        for dp_ref, w_ref, cs_ref in zip(dp_refs, w_refs, cs_refs):
            dp = dp_ref[...]
            cs_ref[...] += jnp.sum(dp.astype(F32), axis=0, keepdims=True)
            dhn = dhn + _dot_nt(dp, w_ref[...])
        dx, dnw = _rms_bwd(dhn, nw, xhat, r)
        dnw_ref[...] += dnw
        o_ref[...] = dh_ref[...] + dx

    row = pl.BlockSpec((tm, d), lambda i: (i, 0))
    vec = pl.BlockSpec((1, d), lambda i: (0, 0))
    return pl.pallas_call(
        body, name="norm_proj_bwd",
        grid=(t // tm,),
        in_specs=[row, vec, row] + [pl.BlockSpec((tm, n), lambda i: (i, 0)) for n in ns] + [ANY] * np_,
        out_specs=[row, row, vec] + [pl.BlockSpec((1, n), lambda i: (0, 0)) for n in ns],
        out_shape=[jax.ShapeDtypeStruct((t, d), F32), jax.ShapeDtypeStruct((t, d), BF16),
                   jax.ShapeDtypeStruct((1, d), F32)] + [jax.ShapeDtypeStruct((1, n), F32) for n in ns],
        scratch_shapes=[pltpu.VMEM((d, n), BF16) for n in ns] + [pltpu.SemaphoreType.DMA],
        compiler_params=_arb(1),
    )(h, nw, dh, *dps, *ws)


def matmul_tn(x, y, scale=1.0):
    t, k = x.shape
    n = y.shape[1]
    tk = _tile(k, 1024) if k % 1024 == 0 else _tile(k, 1408)
    tn = n if n <= 1536 else (1024 if n % 1024 == 0 else 1408)
    assert n % tn == 0
    tt = _tile(t, 1024)
    nt = t // tt

    def body(x_ref, y_ref, o_ref):
        @pl.when(pl.program_id(2) == 0)
        def _():
            o_ref[...] = jnp.zeros_like(o_ref)

        yv = y_ref[...]
        if scale != 1.0:
            yv = yv * scale
        o_ref[...] += _dot_tn(x_ref[...].astype(BF16), yv.astype(BF16))

    return pl.pallas_call(
        body, name="matmul_tn",
        grid=(k // tk, n // tn, nt),
        in_specs=[pl.BlockSpec((tt, tk), lambda i, j, s: (s, i)), pl.BlockSpec((tt, tn), lambda i, j, s: (s, j))],
        out_specs=pl.BlockSpec((tk, tn), lambda i, j, s: (i, j)),
        out_shape=jax.ShapeDtypeStruct((k, n), F32),
        compiler_params=_arb(3),
    )(x, y)


FFN_CHUNKS = 2


def ffn_fwd(h, nw, wgu, wd):
    t, d = h.shape
    f = wd.shape[0]
    fc = f // FFN_CHUNKS
    tm = _tile(t, 512)

    def body(h_ref, nw_ref, wgu_hbm, wd_hbm, o_ref, wgu_ref, wd_ref, sem):
        _resident(wgu_hbm, wgu_ref, sem)
        _resident(wd_hbm, wd_ref, sem)
        x = h_ref[...]
        hn = _rms(x, nw_ref[...])[0].astype(BF16)
        acc = jnp.zeros((tm, d), F32)
        for c in range(FFN_CHUNKS):
            g = _dot(hn, wgu_ref[:, c * fc:(c + 1) * fc])
            u = _dot(hn, wgu_ref[:, f + c * fc:f + (c + 1) * fc])
            acc = acc + _dot((_silu(g) * u).astype(BF16), wd_ref[c * fc:(c + 1) * fc, :])
        o_ref[...] = x + 0.5 * acc

    return pl.pallas_call(
        body, name="ffn_fwd",
        grid=(t // tm,),
        in_specs=[pl.BlockSpec((tm, d), lambda i: (i, 0)), pl.BlockSpec((1, d), lambda i: (0, 0)), ANY, ANY],
        out_specs=pl.BlockSpec((tm, d), lambda i: (i, 0)),
        out_shape=jax.ShapeDtypeStruct((t, d), F32),
        scratch_shapes=[pltpu.VMEM((d, 2 * f), BF16), pltpu.VMEM((f, d), BF16), pltpu.SemaphoreType.DMA],
        compiler_params=_arb(1),
    )(h, nw, wgu, wd)


def ffn_bwd(h, nw, wgu, wd, dh):
    t, d = h.shape
    f = wd.shape[0]
    fc = f // FFN_CHUNKS
    tm = _tile(t, 256)

    def body(h_ref, nw_ref, wgu_hbm, wd_hbm, dh_ref, o_ref, hn_ref, a_ref, dgu_ref, dnw_ref, wgu_ref, wd_ref, sem):
        _resident(wgu_hbm, wgu_ref, sem)
        _resident(wd_hbm, wd_ref, sem)

        @pl.when(pl.program_id(0) == 0)
        def _():
            dnw_ref[...] = jnp.zeros_like(dnw_ref)

        nw = nw_ref[...]
        hn32, xhat, r = _rms(h_ref[...], nw)
        hn = hn32.astype(BF16)
        hn_ref[...] = hn
        dh = dh_ref[...]
        dout = (0.5 * dh).astype(BF16)
        dhn = jnp.zeros((tm, d), F32)
        for c in range(FFN_CHUNKS):
            gs, us = slice(c * fc, (c + 1) * fc), slice(f + c * fc, f + (c + 1) * fc)
            g = _dot(hn, wgu_ref[:, gs])
            u = _dot(hn, wgu_ref[:, us])
            sg = _sigmoid(g)
            sil = g * sg
            a_ref[:, gs] = (sil * u).astype(BF16)
            da = _dot_nt(dout, wd_ref[gs, :])
            dg = (da * u * (sg * (1.0 + g * (1.0 - sg)))).astype(BF16)
            du = (da * sil).astype(BF16)
            dgu_ref[:, gs] = dg
            dgu_ref[:, us] = du
            dhn = dhn + _dot_nt(dg, wgu_ref[:, gs]) + _dot_nt(du, wgu_ref[:, us])
        dx, dnw = _rms_bwd(dhn, nw, xhat, r)
        dnw_ref[...] += dnw
        o_ref[...] = dh + dx

    row = pl.BlockSpec((tm, d), lambda i: (i, 0))
    vec = pl.BlockSpec((1, d), lambda i: (0, 0))
    return pl.pallas_call(
        body, name="ffn_bwd",
        grid=(t // tm,),
        in_specs=[row, vec, ANY, ANY, row],
        out_specs=[row, row, pl.BlockSpec((tm, f), lambda i: (i, 0)), pl.BlockSpec((tm, 2 * f), lambda i: (i, 0)), vec],
        out_shape=[jax.ShapeDtypeStruct((t, d), F32), jax.ShapeDtypeStruct((t, d), BF16),
                   jax.ShapeDtypeStruct((t, f), BF16), jax.ShapeDtypeStruct((t, 2 * f), BF16),
                   jax.ShapeDtypeStruct((1, d), F32)],
        scratch_shapes=[pltpu.VMEM((d, 2 * f), BF16), pltpu.VMEM((f, d), BF16), pltpu.SemaphoreType.DMA],
        compiler_params=_arb(1),
    )(h, nw, wgu, wd, dh)


def loss_head(h, nw, target):
    t, d = h.shape
    tm = _tile(t, 512)

    def body(h_ref, nw_ref, tg_ref, dh_ref, loss_ref, dnw_ref):
        @pl.when(pl.program_id(0) == 0)
        def _():
            loss_ref[...] = jnp.zeros_like(loss_ref)
            dnw_ref[...] = jnp.zeros_like(dnw_ref)

        nw = nw_ref[...]
        y, xhat, r = _rms(h_ref[...], nw)
        e = y - tg_ref[...]
        loss_ref[...] += 0.5 * jnp.sum(jnp.mean(e * e, axis=-1, keepdims=True), axis=0, keepdims=True)
        dx, dnw = _rms_bwd(e * (1.0 / d), nw, xhat, r)
        dnw_ref[...] += dnw
        dh_ref[...] = dx

    row = pl.BlockSpec((tm, d), lambda i: (i, 0))
    vec = pl.BlockSpec((1, d), lambda i: (0, 0))
    return pl.pallas_call(
        body, name="loss_head",
        grid=(t // tm,),
        in_specs=[row, vec, row],
        out_specs=[row, pl.BlockSpec((8, LANE), lambda i: (0, 0)), vec],
        out_shape=[jax.ShapeDtypeStruct((t, d), F32), jax.ShapeDtypeStruct((8, LANE), F32),
                   jax.ShapeDtypeStruct((1, d), F32)],
        compiler_params=_arb(1),
    )(h, nw, target)


def adamw(w, g, m, v):
    r, c = w.shape
    tr = r
    while tr * c * 4 > (1 << 20) and tr % 16 == 0:
        tr //= 2

    def body(w_ref, g_ref, m_ref, v_ref, d_ref, nm_ref, nv_ref):
        g = g_ref[...]
        m = ADAM_B1 * m_ref[...] + (1.0 - ADAM_B1) * g
        v = ADAM_B2 * v_ref[...] + (1.0 - ADAM_B2) * (g * g)
        m_hat = m / (1.0 - ADAM_B1 ** ADAM_STEP)
        v_hat = v / (1.0 - ADAM_B2 ** ADAM_STEP)
        d_ref[...] = -ADAM_LR * (m_hat / (jnp.sqrt(v_hat) + ADAM_EPS) + ADAM_WD * w_ref[...])
        nm_ref[...] = m
        nv_ref[...] = v

    blk = pl.BlockSpec((tr, c), lambda i: (i, 0))
    return pl.pallas_call(
        body, name="adamw",
        grid=(r // tr,),
        in_specs=[blk] * 4, out_specs=[blk] * 3,
        out_shape=[jax.ShapeDtypeStruct((r, c), F32)] * 3,
        compiler_params=_arb(1),
    )(w, g, m, v)


ATTN_P_W = ATTN_Q_W + 2 * ATTN_KV_W
ATTN_SCALE = ATTN_HEAD_DIM ** -0.5


def _rope_group(t, tab, sign):
    return (t * tab[:, 0:LANE] + sign * pltpu.roll(t, 8, 1) * tab[:, LANE:2 * LANE]
            + sign * pltpu.roll(t, LANE - 8, 1) * tab[:, 2 * LANE:3 * LANE])


def _rope(t, tab, sign=1.0):
    return jnp.concatenate([_rope_group(t[:, s:s + LANE], tab, sign) for s in range(0, t.shape[1], LANE)], axis=1)


def _attn_heads(qs, ks, vs, sinks, first):
    b = ATTN_BLOCK
    rows = ATTN_GROUP * b
    qi = _iota2((rows, 2 * b), 0) % b
    kj = _iota2((rows, 2 * b), 1)
    rel = qi + b - kj
    valid = (rel >= 0) & (rel < b) & ((kj >= b) | jnp.logical_not(first))
    outs = []
    for q, k, v, sink in zip(qs, ks, vs, sinks):
        s = jnp.where(valid, _mm1_nt(q, k) * ATTN_SCALE, NEG)
        m = lax.stop_gradient(jnp.maximum(jnp.max(s, axis=-1, keepdims=True), sink))
        p = jnp.exp(s - m)
        den = jnp.sum(p, axis=-1, keepdims=True) + jnp.exp(sink - m)
        outs.append(_mm1(p / den, v))
    return outs


def _attn_prepare(p_ref, kvp_ref, tab_ref, tabp_ref, sink_ref):
    b = ATTN_BLOCK
    hd = ATTN_HEAD_DIM
    tab, tabp = tab_ref[...], tabp_ref[...]
    q = _rope(p_ref[:, 0:ATTN_Q_W], tab)
    kc = _rope(p_ref[:, ATTN_Q_W:ATTN_Q_W + ATTN_KV_W], tab)
    kp = _rope(kvp_ref[:, 0:ATTN_KV_W], tabp)
    vc = p_ref[:, ATTN_Q_W + ATTN_KV_W:ATTN_P_W]
    vp = kvp_ref[:, ATTN_KV_W:2 * ATTN_KV_W]
    sk = sink_ref[...]
    qs, ks, vs, sinks = [], [], [], []
    for h in range(ATTN_KV_HEADS):
        heads = [ATTN_GROUP * h + g for g in range(ATTN_GROUP)]
        qs.append(jnp.concatenate([q[:, i * hd:(i + 1) * hd] for i in heads], axis=0))
        ks.append(jnp.concatenate([kp[:, h * hd:(h + 1) * hd], kc[:, h * hd:(h + 1) * hd]], axis=0))
        vs.append(jnp.concatenate([vp[:, h * hd:(h + 1) * hd], vc[:, h * hd:(h + 1) * hd]], axis=0))
        sinks.append(jnp.concatenate([jnp.broadcast_to(sk[:, i:i + 1], (b, 1)) for i in heads], axis=0))
    return qs, ks, vs, sinks


def _unstack_heads(xs):
    b = ATTN_BLOCK
    return jnp.concatenate([x[g * b:(g + 1) * b] for x in xs for g in range(ATTN_GROUP)], axis=1)


def _attn_specs(nb):
    b = ATTN_BLOCK
    prev = lambda n: jnp.maximum(n - 1, 0)
    return [pl.BlockSpec((b, ATTN_P_W), lambda n: (n, 0)),
            pl.BlockSpec((b, 2 * ATTN_KV_W), lambda n: (prev(n), ATTN_Q_W // (2 * ATTN_KV_W))),
            pl.BlockSpec((b, 3 * LANE), lambda n: (n, 0)),
            pl.BlockSpec((b, 3 * LANE), lambda n: (prev(n), 0)),
            pl.BlockSpec((1, ATTN_Q_HEADS), lambda n: (0, 0))]


def attn_fwd(p, tab, sinks):
    t = p.shape[0]
    nb = t // ATTN_BLOCK

    def body(p_ref, kvp_ref, tab_ref, tabp_ref, sink_ref, o_ref):
        qs, ks, vs, sk = _attn_prepare(p_ref, kvp_ref, tab_ref, tabp_ref, sink_ref)
        os_ = _attn_heads(qs, ks, vs, sk, pl.program_id(0) == 0)
        o_ref[...] = _unstack_heads(os_).astype(BF16)

    return pl.pallas_call(
        body, name="attn_fwd",
        grid=(nb,),
        in_specs=_attn_specs(nb),
        out_specs=pl.BlockSpec((ATTN_BLOCK, ATTN_Q_W), lambda n: (n, 0)),
        out_shape=jax.ShapeDtypeStruct((t, ATTN_Q_W), BF16),
        compiler_params=_arb(1),
    )(p, p, tab, tab, sinks)


def attn_bwd(p, tab, sinks, do):
    t = p.shape[0]
    b = ATTN_BLOCK
    hd = ATTN_HEAD_DIM
    nb = t // b

    def body(p_ref, kvp_ref, tab_ref, tabp_ref, sink_ref, do_ref, dq_ref, dkvc_ref, dkvp_ref, dsink_ref):
        n = pl.program_id(0)

        @pl.when(n == 0)
        def _():
            dsink_ref[...] = jnp.zeros_like(dsink_ref)

        qs, ks, vs, sk = _attn_prepare(p_ref, kvp_ref, tab_ref, tabp_ref, sink_ref)
        first = n == 0
        _, vjp = jax.vjp(lambda a, bb, c, d: _attn_heads(a, bb, c, d, first), qs, ks, vs, sk)
        do = do_ref[...]
        dos = [jnp.concatenate([do[:, i * hd:(i + 1) * hd] for i in range(ATTN_GROUP * h, ATTN_GROUP * (h + 1))], axis=0)
               for h in range(ATTN_KV_HEADS)]
        dqs, dks, dvs, dsk = vjp(dos)
        tab, tabp = tab_ref[...], tabp_ref[...]
        dq_ref[...] = _rope(_unstack_heads([x.astype(F32) for x in dqs]), tab, -1.0).astype(BF16)
        dkc = jnp.concatenate([x.astype(F32)[b:] for x in dks], axis=1)
        dkp = jnp.concatenate([x.astype(F32)[:b] for x in dks], axis=1)
        dkvc_ref[:, 0:ATTN_KV_W] = _rope(dkc, tab, -1.0)
        dkvp_ref[:, 0:ATTN_KV_W] = _rope(dkp, tabp, -1.0)
        dkvc_ref[:, ATTN_KV_W:] = jnp.concatenate([x.astype(F32)[b:] for x in dvs], axis=1)
        dkvp_ref[:, ATTN_KV_W:] = jnp.concatenate([x.astype(F32)[:b] for x in dvs], axis=1)
        parts = [jnp.sum(d[g * b:(g + 1) * b], axis=0, keepdims=True) for d in dsk for g in range(ATTN_GROUP)]
        dsink_ref[...] += jnp.concatenate(parts, axis=1)

    blk = lambda w: pl.BlockSpec((b, w), lambda n: (n, 0))
    return pl.pallas_call(
        body, name="attn_bwd",
        grid=(nb,),
        in_specs=_attn_specs(nb) + [blk(ATTN_Q_W)],
        out_specs=[blk(ATTN_Q_W), blk(2 * ATTN_KV_W), blk(2 * ATTN_KV_W),
                   pl.BlockSpec((1, ATTN_Q_HEADS), lambda n: (0, 0))],
        out_shape=[jax.ShapeDtypeStruct((t, ATTN_Q_W), BF16), jax.ShapeDtypeStruct((t, 2 * ATTN_KV_W), F32),
                   jax.ShapeDtypeStruct((t, 2 * ATTN_KV_W), F32), jax.ShapeDtypeStruct((1, ATTN_Q_HEADS), F32)],
        compiler_params=_arb(1),
    )(p, p, tab, tab, sinks, do)


def kv_combine(dkvc, dkvp):
    t, w = dkvc.shape
    b = ATTN_BLOCK
    nb = t // b

    def body(c_ref, p_ref, o_ref):
        nxt = jnp.where(pl.program_id(0) < nb - 1, p_ref[...], 0.0)
        o_ref[...] = (c_ref[...] + nxt).astype(BF16)

    return pl.pallas_call(
        body, name="kv_combine",
        grid=(nb,),
        in_specs=[pl.BlockSpec((b, w), lambda n: (n, 0)),
                  pl.BlockSpec((b, w), lambda n: (jnp.minimum(n + 1, nb - 1), 0))],
        out_specs=pl.BlockSpec((b, w), lambda n: (n, 0)),
        out_shape=jax.ShapeDtypeStruct((t, w), BF16),
        compiler_params=_arb(1),
    )(dkvc, dkvp)


def _me():
    return lax.axis_index("x"), lax.axis_index("y"), lax.axis_index("c")


def _flip(v, bit):
    return 1 - v if bit else v


def allgather_chips(x):
    r, c = x.shape

    def body(x_ref, o_ref, send_sems, recv_sems, local_sem):
        mx, my, mc = _me()
        me = 2 * mx + my
        local = pltpu.make_async_copy(x_ref, o_ref.at[me], local_sem)
        local.start()
        copies = []
        for k in (1, 2, 3):
            peer = (_flip(mx, k >> 1), _flip(my, k & 1), mc)
            cp = pltpu.make_async_remote_copy(src_ref=x_ref, dst_ref=o_ref.at[me], send_sem=send_sems.at[k - 1],
                                              recv_sem=recv_sems.at[k - 1], device_id=peer, device_id_type=MESH)
            cp.start()
            copies.append(cp)
        for cp in copies:
            cp.wait()
        local.wait()

    return pl.pallas_call(
        body, name="allgather_chips",
        in_specs=[ANY], out_specs=ANY,
        out_shape=jax.ShapeDtypeStruct((4, r, c), x.dtype),
        scratch_shapes=[pltpu.SemaphoreType.DMA((3,)), pltpu.SemaphoreType.DMA((3,)), pltpu.SemaphoreType.DMA],
    )(x)


def pair_exchange(g):
    _, _, r, c = g.shape

    def body(g_ref, mine_ref, got_ref, send_sems, recv_sems, local_sems):
        mx, my, mc = _me()
        sib = (mx, my, 1 - mc)
        todo = []
        for j in range(4):
            lc = pltpu.make_async_copy(g_ref.at[j, mc], mine_ref.at[j], local_sems.at[j])
            lc.start()
            cp = pltpu.make_async_remote_copy(src_ref=g_ref.at[j, 1 - mc], dst_ref=got_ref.at[j], send_sem=send_sems.at[j],
                                              recv_sem=recv_sems.at[j], device_id=sib, device_id_type=MESH)
            cp.start()
            todo += [lc, cp]
        for cp in todo:
            cp.wait()

    return pl.pallas_call(
        body, name="pair_exchange",
        in_specs=[ANY], out_specs=[ANY, ANY],
        out_shape=[jax.ShapeDtypeStruct((4, r, c), g.dtype)] * 2,
        scratch_shapes=[pltpu.SemaphoreType.DMA((4,))] * 3,
    )(g)


def scatter_chips(p):
    _, r, c = p.shape

    def body(p_ref, o_ref, send_sems, recv_sems, local_sem):
        mx, my, mc = _me()
        me = 2 * mx + my
        local = pltpu.make_async_copy(p_ref.at[me], o_ref.at[me], local_sem)
        local.start()
        copies = []
        for k in (1, 2, 3):
            px, py = _flip(mx, k >> 1), _flip(my, k & 1)
            cp = pltpu.make_async_remote_copy(src_ref=p_ref.at[2 * px + py], dst_ref=o_ref.at[me],
                                              send_sem=send_sems.at[k - 1], recv_sem=recv_sems.at[k - 1],
                                              device_id=(px, py, mc), device_id_type=MESH)
            cp.start()
            copies.append(cp)
        for cp in copies:
            cp.wait()
        local.wait()

    return pl.pallas_call(
        body, name="scatter_chips",
        in_specs=[ANY], out_specs=ANY,
        out_shape=jax.ShapeDtypeStruct((4, r, c), p.dtype),
        scratch_shapes=[pltpu.SemaphoreType.DMA((3,)), pltpu.SemaphoreType.DMA((3,)), pltpu.SemaphoreType.DMA],
    )(p)


def pair_share(o):
    r, c = o.shape

    def body(o_ref, out_ref, send_sem, recv_sem, local_sem):
        mx, my, mc = _me()
        local = pltpu.make_async_copy(o_ref, out_ref.at[mc], local_sem)
        local.start()
        cp = pltpu.make_async_remote_copy(src_ref=o_ref, dst_ref=out_ref.at[mc], send_sem=send_sem, recv_sem=recv_sem,
                                          device_id=(mx, my, 1 - mc), device_id_type=MESH)
        cp.start()
        cp.wait()
        local.wait()

    return pl.pallas_call(
        body, name="pair_share",
        in_specs=[ANY], out_specs=ANY,
        out_shape=jax.ShapeDtypeStruct((2, r, c), o.dtype),
        scratch_shapes=[pltpu.SemaphoreType.DMA, pltpu.SemaphoreType.DMA, pltpu.SemaphoreType.DMA],
    )(o)


def _row_tile(r, c):
    tr = 8
    while r % (2 * tr) == 0 and 2 * tr * c * 4 <= (2 << 20):
        tr *= 2
    return tr


def add_n(xs):
    r, c = xs[0].shape
    tr = _row_tile(r, c)

    def body(*refs):
        acc = refs[0][...]
        for x_ref in refs[1:-1]:
            acc = acc + x_ref[...]
        refs[-1][...] = acc

    blk = pl.BlockSpec((tr, c), lambda i: (i, 0))
    return pl.pallas_call(
        body, name="add_n",
        grid=(r // tr,),
        in_specs=[blk] * len(xs), out_specs=blk,
        out_shape=jax.ShapeDtypeStruct((r, c), F32),
        compiler_params=_arb(1),
    )(*xs)


def add_slots(q):
    n, r, c = q.shape
    tr = _row_tile(r, c)

    def body(*refs):
        acc = refs[0][...]
        for x_ref in refs[1:-1]:
            acc = acc + x_ref[...]
        refs[-1][...] = acc

    return pl.pallas_call(
        body, name="add_slots",
        grid=(r // tr,),
        in_specs=[pl.BlockSpec((None, tr, c), functools.partial(lambda k, i: (k, i, 0), k)) for k in range(n)],
        out_specs=pl.BlockSpec((tr, c), lambda i: (i, 0)),
        out_shape=jax.ShapeDtypeStruct((r, c), F32),
        compiler_params=_arb(1),
    )(*([q] * n))


def allreduce_small(x):
    r, c = x.shape

    def body(x_ref, o_ref, buf, send_sems, recv_sems):
        mx, my, mc = _me()
        me = 4 * mx + 2 * my + mc
        buf[pl.ds(me, 1)] = x_ref[...][None]
        copies = []
        for k in range(1, 8):
            peer = (_flip(mx, k >> 2), _flip(my, (k >> 1) & 1), _flip(mc, k & 1))
            cp = pltpu.make_async_remote_copy(src_ref=x_ref, dst_ref=buf.at[me], send_sem=send_sems.at[k - 1],
                                              recv_sem=recv_sems.at[k - 1], device_id=peer, device_id_type=MESH)
            cp.start()
            copies.append(cp)
        for cp in copies:
            cp.wait()
        acc = buf[0]
        for d in range(1, 8):
            acc = acc + buf[d]
        o_ref[...] = acc

    return pl.pallas_call(
        body, name="allreduce_small",
        out_shape=jax.ShapeDtypeStruct((r, c), F32),
        scratch_shapes=[pltpu.VMEM((8, r, c), F32), pltpu.SemaphoreType.DMA((7,)), pltpu.SemaphoreType.DMA((7,))],
    )(x)


def reduce_scatter(g):
    _, r, c = g.shape
    rh = r // 2
    mine, got = pair_exchange(g.reshape(4, 2, rh, c))
    p = add_n([mine.reshape(4 * rh, c), got.reshape(4 * rh, c)]).reshape(4, rh, c)
    q = scatter_chips(p)
    o = add_slots(q)
    return pair_share(o).reshape(r, c)


PACK_W = 1024
PACK_ROWS = 1024

SHARDED = {"ffn1_w_gate_up": 1, "ffn1_w_down": 0, "ffn2_w_gate_up": 1, "ffn2_w_down": 0, "attn_w_in": 1,
           "attn_w_out": 0, "gdn_w_in": 1, "gdn_w_out": 0, "gdn_conv_w": 1}
REPLICATED = ["ffn1_norm", "mix_norm", "ffn2_norm", "attn_b_in", "attn_sinks", "attn_b_out", "gdn_A_log",
              "gdn_dt_bias", "gdn_norm_w", "final_norm"]
WEIGHTS = ["ffn1_norm", "ffn1_w_gate_up", "ffn1_w_down", "mix_norm", "ffn2_norm", "ffn2_w_gate_up", "ffn2_w_down",
           "attn_w_in", "attn_b_in", "attn_sinks", "attn_w_out", "attn_b_out", "gdn_w_in", "gdn_conv_w", "gdn_A_log",
           "gdn_dt_bias", "gdn_norm_w", "gdn_w_out", "final_norm"]


def _pack_rows(flats, dtype, width, row_multiple):
    flat = jnp.concatenate([f.astype(dtype).reshape(-1) for f in flats])
    per = width * row_multiple
    pad = (-flat.shape[0]) % per
    if pad:
        flat = jnp.concatenate([flat, jnp.zeros((pad,), dtype)])
    return flat.reshape(-1, width)


def _gather_weights(shards, dtype):
    names = list(shards)
    packed = _pack_rows([shards[n] for n in names], dtype, PACK_W, 16)
    got = allgather_chips(packed).reshape(4, -1)
    full, off = {}, 0
    for n in names:
        shape = shards[n].shape
        size = shards[n].size
        full[n] = jnp.concatenate([got[j, off:off + size].reshape(shape) for j in range(4)], axis=1 + SHARDED[n])
        off += size
    return full


def _scatter_grads(grads, shard_shapes):
    names = list(grads)
    slots = []
    for j in range(4):
        parts = []
        for n in names:
            ax = 1 + SHARDED[n]
            w = shard_shapes[n][ax]
            parts.append(lax.slice_in_dim(grads[n], j * w, (j + 1) * w, axis=ax))
        slots.append(_pack_rows(parts, F32, PACK_W, PACK_ROWS))
    red = reduce_scatter(jnp.stack(slots)).reshape(-1)
    out, off = {}, 0
    for n in names:
        size = 1
        for s in shard_shapes[n]:
            size *= s
        out[n] = red[off:off + size].reshape(shard_shapes[n])
        off += size
    return out


def _small_pack(items):
    rows = []
    for a in items:
        f = a.astype(F32).reshape(-1)
        pad = (-f.shape[0]) % LANE
        rows.append(jnp.concatenate([f, jnp.zeros((pad,), F32)]) if pad else f)
    return _pack_rows(rows, F32, LANE, 8)


def _small_unpack(buf, shapes):
    flat = buf.reshape(-1)
    out, off = [], 0
    for shape in shapes:
        size = 1
        for s in shape:
            size *= s
        out.append(flat[off:off + size].reshape(shape))
        off += size + (-size) % LANE
    return out


def _rope_table(positions):
    t = positions.shape[0]
    inv_freq = ROPE_THETA ** (-jnp.arange(0, ROPE_DIM, 2, dtype=F32) / ROPE_DIM)
    ang = positions.astype(F32)[:, None] * inv_freq
    cos, sin = jnp.cos(ang), jnp.sin(ang)
    rest = ATTN_HEAD_DIM - ROPE_DIM
    zeros = lambda n: jnp.zeros((t, n), F32)
    c64 = jnp.concatenate([cos, cos, jnp.ones((t, rest), F32)], axis=1)
    s_up = jnp.concatenate([zeros(ROPE_DIM // 2), sin, zeros(rest)], axis=1)
    s_dn = jnp.concatenate([-sin, zeros(ROPE_DIM // 2 + rest)], axis=1)
    return jnp.concatenate([c64, c64, s_up, s_up, s_dn, s_dn], axis=1)


def _as2d(a):
    return a.reshape(-1, a.shape[-1]) if a.ndim > 1 else a.reshape(1, -1)


def kernel(x, positions, ffn1_norm, ffn1_w_gate_up, ffn1_w_down, mix_norm, ffn2_norm, ffn2_w_gate_up, ffn2_w_down, attn_w_in, attn_b_in, attn_sinks, attn_w_out, attn_b_out, gdn_w_in, gdn_conv_w, gdn_A_log, gdn_dt_bias, gdn_norm_w, gdn_w_out, final_norm, loss_target, m_ffn1_norm, m_ffn1_w_gate_up, m_ffn1_w_down, m_mix_norm, m_ffn2_norm, m_ffn2_w_gate_up, m_ffn2_w_down, m_attn_w_in, m_attn_b_in, m_attn_sinks, m_attn_w_out, m_attn_b_out, m_gdn_w_in, m_gdn_conv_w, m_gdn_A_log, m_gdn_dt_bias, m_gdn_norm_w, m_gdn_w_out, m_final_norm, v_ffn1_norm, v_ffn1_w_gate_up, v_ffn1_w_down, v_mix_norm, v_ffn2_norm, v_ffn2_w_gate_up, v_ffn2_w_down, v_attn_w_in, v_attn_b_in, v_attn_sinks, v_attn_w_out, v_attn_b_out, v_gdn_w_in, v_gdn_conv_w, v_gdn_A_log, v_gdn_dt_bias, v_gdn_norm_w, v_gdn_w_out, v_final_norm):
    given = dict(locals())
    w = {n: given[n] for n in WEIGHTS}
    d = D_MODEL
    h = x[0]
    target = loss_target[0]
    depth = ffn1_norm.shape[0]

    big = [n for n in SHARDED if n != "gdn_conv_w"]
    full = _gather_weights({n: w[n] for n in big}, BF16)
    conv_full = _gather_weights({"gdn_conv_w": gdn_conv_w}, F32)["gdn_conv_w"]
    tab = _rope_table(positions[0])
    zero_d = jnp.zeros((1, d), F32)

    def gdn_params(j):
        w_in = full["gdn_w_in"][j]
        w_cat = jnp.concatenate([w_in, jnp.zeros((d, GDN_PROJ_W - w_in.shape[1]), BF16)], axis=1)
        conv = jnp.concatenate([conv_full[j], jnp.zeros((8 - GDN_CONV, GDN_QKV_W), F32)], axis=0)
        lanes = lambda vec: jnp.concatenate([jnp.zeros((GC_G,), F32), vec, jnp.zeros((LANE - GC_G - GDN_HEADS,), F32)])
        par = jnp.concatenate([lanes(gdn_A_log[j])[None], lanes(gdn_dt_bias[j])[None], jnp.zeros((6, LANE), F32)], axis=0)
        return w_cat, conv, par

    saved = []
    for l in range(depth):
        j = l // 2
        rec = {"h1": h}
        h = ffn_fwd(h, ffn1_norm[l][None], full["ffn1_w_gate_up"][l], full["ffn1_w_down"][l])
        rec["h2"] = h
        if l % 2 == 0:
            p = norm_proj(h, mix_norm[l][None], full["attn_w_in"][j], attn_b_in[j][None])
            o = attn_fwd(p, tab, attn_sinks[j][None])
            h = linear_residual(h, o, full["attn_w_out"][j], attn_b_out[j][None])
            rec.update(p=p, o=o)
        else:
            w_cat, conv, par = gdn_params(j)
            proj = norm_proj(h, mix_norm[l][None], w_cat, jnp.zeros((1, GDN_PROJ_W), F32))
            qkv, gc = gdn_pre_fwd(proj, conv, par)
            on, st = gdn_chunk_fwd(qkv, gc, proj, gdn_norm_w[j][None])
            h = linear_residual(h, on, full["gdn_w_out"][j], zero_d)
            rec.update(proj=proj, qkv=qkv, gc=gc, on=on, st=st, w_cat=w_cat, conv=conv, par=par)
        rec["h3"] = h
        h = ffn_fwd(h, ffn2_norm[l][None], full["ffn2_w_gate_up"][l], full["ffn2_w_down"][l])
        saved.append(rec)

    dh, loss_tile, d_final = loss_head(h, final_norm[None], target)

    g = {n: [None] * w[n].shape[0] for n in WEIGHTS if n != "final_norm"}
    for l in reversed(range(depth)):
        j = l // 2
        rec = saved[l]

        def ffn_back(tag, h_in, dh):
            wgu, wd = full[tag + "_w_gate_up"][l], full[tag + "_w_down"][l]
            dh_new, hn, a, dgu, dn = ffn_bwd(h_in, w[tag + "_norm"][l][None], wgu, wd, dh)
            g[tag + "_w_gate_up"][l] = matmul_tn(hn, dgu)
            g[tag + "_w_down"][l] = matmul_tn(a, dh, 0.5)
            g[tag + "_norm"][l] = dn[0]
            return dh_new

        dh = ffn_back("ffn2", rec["h3"], dh)
        if l % 2 == 0:
            w_in, w_out = full["attn_w_in"][j], full["attn_w_out"][j]
            do, db_out = matmul_nt(dh, w_out)
            g["attn_w_out"][j] = matmul_tn(rec["o"], dh)
            g["attn_b_out"][j] = db_out[0]
            dq, dkvc, dkvp, dsink = attn_bwd(rec["p"], tab, attn_sinks[j][None], do)
            dkv = kv_combine(dkvc, dkvp)
            dh, hn, dn, cs_q, cs_kv = norm_proj_bwd(rec["h2"], mix_norm[l][None], dh, [dq, dkv],
                                                    [w_in[:, :ATTN_Q_W], w_in[:, ATTN_Q_W:]])
            g["attn_w_in"][j] = jnp.concatenate([matmul_tn(hn, dq), matmul_tn(hn, dkv)], axis=1)
            g["attn_b_in"][j] = jnp.concatenate([cs_q[0], cs_kv[0]])
            g["attn_sinks"][j] = dsink[0]
        else:
            w_cat, conv, par = rec["w_cat"], rec["conv"], rec["par"]
            d_on, _ = matmul_nt(dh, full["gdn_w_out"][j])
            g["gdn_w_out"][j] = matmul_tn(rec["on"], dh)
            dq, dk, dv, dz, dgc_heads, dnw = gdn_chunk_bwd(rec["qkv"], rec["gc"], rec["proj"], gdn_norm_w[j][None],
                                                           rec["st"], d_on)
            dy, dba, dpar = gdn_pre_bwd(rec["proj"], conv, par, dq, dk, dv, dgc_heads)
            dx, dconv = gdn_conv_bwd(rec["proj"], conv, dy)
            nz = GDN_QKV_W + GDN_HEADS * GDN_DK
            dh, hn, dn, _, _, _ = norm_proj_bwd(rec["h2"], mix_norm[l][None], dh, [dx, dz, dba],
                                                [w_cat[:, :GDN_QKV_W], w_cat[:, GDN_QKV_W:nz], w_cat[:, nz:]])
            g["gdn_w_in"][j] = jnp.concatenate(
                [matmul_tn(hn, dx), matmul_tn(hn, dz), matmul_tn(hn, dba)[:, :2 * GDN_HEADS]], axis=1)
            g["gdn_conv_w"][j] = dconv[:GDN_CONV]
            g["gdn_A_log"][j] = dpar[0, GDN_HEADS:2 * GDN_HEADS]
            g["gdn_dt_bias"][j] = dpar[1, GDN_HEADS:2 * GDN_HEADS]
            g["gdn_norm_w"][j] = dnw[0]
        g["mix_norm"][l] = dn[0]
        dh = ffn_back("ffn1", rec["h1"], dh)
    grad_x = dh[None]

    local = {n: jnp.stack(v) for n, v in g.items()}
    local["final_norm"] = d_final[0]
    small_shapes = [(1,)] + [w[n].shape for n in REPLICATED]
    small = allreduce_small(_small_pack([loss_tile[0, 0:1]] + [local[n] for n in REPLICATED]))
    small = _small_unpack(small, small_shapes)
    loss = small[0][0]
    grads = dict(zip(REPLICATED, small[1:]))
    grads.update(_scatter_grads({n: local[n] for n in SHARDED}, {n: w[n].shape for n in SHARDED}))

    delta, new_m, new_v = {}, {}, {}
    for n in SHARDED:
        dl, nm, nv = adamw(_as2d(w[n]), _as2d(grads[n]), _as2d(given["m_" + n]), _as2d(given["v_" + n]))
        delta[n], new_m[n], new_v[n] = dl.reshape(w[n].shape), nm.reshape(w[n].shape), nv.reshape(w[n].shape)
    shapes = [w[n].shape for n in REPLICATED]
    packed = [_small_pack([src[n] for n in REPLICATED]) for src in
              (w, grads, {n: given["m_" + n] for n in REPLICATED}, {n: given["v_" + n] for n in REPLICATED})]
    for dst, buf in zip((delta, new_m, new_v), adamw(*packed)):
        dst.update(zip(REPLICATED, _small_unpack(buf, shapes)))

    return (loss, grad_x, *[grads[n] for n in WEIGHTS], *[delta[n] for n in WEIGHTS],
            *[new_m[n] for n in WEIGHTS], *[new_v[n] for n in WEIGHTS])
```

```python
import functools

import jax
import jax.numpy as jnp
from jax import lax
from jax.experimental import pallas as pl
from jax.experimental.pallas import tpu as pltpu

F32 = jnp.float32
BF16 = jnp.bfloat16
HI = lax.Precision.HIGHEST
MESH = pl.DeviceIdType.MESH

D_MODEL = 1024
D_FF = 2816
DEPTH = 4
NORM_EPS = 1e-6
LANE = 128

ATTN_Q_HEADS = 16
ATTN_KV_HEADS = 4
ATTN_HEAD_DIM = 64
ATTN_GROUP = 4
ATTN_BLOCK = 128
ROPE_DIM = 16
ROPE_THETA = 500000.0
ATTN_Q_W = 1024
ATTN_KV_W = 256

GDN_HEADS = 8
GDN_DK = 128
GDN_CONV = 4
GDN_CHUNK = 64
GDN_QKV_W = 3072

ADAM_LR = 0.001
ADAM_B1 = 0.9
ADAM_B2 = 0.999
ADAM_EPS = 1e-08
ADAM_WD = 0.01
ADAM_STEP = 10

NEG = -1e30


def _dot(a, b, prec=None):
    return lax.dot_general(a, b, (((1,), (0,)), ((), ())), precision=prec, preferred_element_type=F32)


def _dot_nt(a, b, prec=None):
    return lax.dot_general(a, b, (((1,), (1,)), ((), ())), precision=prec, preferred_element_type=F32)


def _dot_tn(a, b, prec=None):
    return lax.dot_general(a, b, (((0,), (0,)), ((), ())), precision=prec, preferred_element_type=F32)


def _bdot(a, b):
    return _dot(a.astype(BF16), b.astype(BF16))


def _bdot_nt(a, b):
    return _dot_nt(a.astype(BF16), b.astype(BF16))


def _bdot_tn(a, b):
    return _dot_tn(a.astype(BF16), b.astype(BF16))


def _sigmoid(x):
    return 1.0 / (1.0 + jnp.exp(-x))


def _silu(x):
    return x * _sigmoid(x)


def _silu_grad(x):
    s = _sigmoid(x)
    return s * (1.0 + x * (1.0 - s))


def _rms(x, w):
    r = lax.rsqrt(jnp.mean(x * x, axis=-1, keepdims=True) + NORM_EPS)
    xhat = x * r
    return xhat * w, xhat, r


def _rms_bwd(dy, w, xhat, r):
    dxhat = dy * w
    dx = r * (dxhat - xhat * jnp.mean(dxhat * xhat, axis=-1, keepdims=True))
    dw = jnp.sum(dy * xhat, axis=0, keepdims=True)
    return dx, dw


def _arb(n):
    return pltpu.CompilerParams(dimension_semantics=("arbitrary",) * n)


def _tile(n, want):
    t = min(n, want)
    assert n % t == 0, (n, want)
    return t


def _iota2(shape, dim):
    return lax.broadcasted_iota(jnp.int32, shape, dim)


_NN = (((1,), (0,)), ((), ()))
_NT = (((1,), (1,)), ((), ()))
_TN = (((0,), (0,)), ((), ()))


def _raw1(a, b, dn):
    return lax.dot_general(a.astype(BF16), b.astype(BF16), dn, preferred_element_type=F32)


def _raw3(a, b, dn):
    ah, bh = a.astype(BF16), b.astype(BF16)
    al, bl = (a - ah.astype(F32)).astype(BF16), (b - bh.astype(F32)).astype(BF16)
    f = lambda x, y: lax.dot_general(x, y, dn, preferred_element_type=F32)
    return f(ah, bh) + f(ah, bl) + f(al, bh)


def _make_mm(raw):
    @jax.custom_vjp
    def mm(a, b):
        return raw(a, b, _NN)

    mm.defvjp(lambda a, b: (raw(a, b, _NN), (a, b)),
              lambda res, ct: (raw(ct, res[1], _NT), raw(res[0], ct, _TN)))

    @jax.custom_vjp
    def mm_nt(a, b):
        return raw(a, b, _NT)

    mm_nt.defvjp(lambda a, b: (raw(a, b, _NT), (a, b)),
                 lambda res, ct: (raw(ct, res[1], _NN), raw(ct, res[0], _TN)))
    return mm, mm_nt


_mm1, _mm1_nt = _make_mm(_raw1)
_mm3, _mm3_nt = _make_mm(_raw3)


def _eye(n):
    return (_iota2((n, n), 0) == _iota2((n, n), 1)).astype(F32)


def _inv_newton(a):
    eye = _eye(a.shape[0])
    n = -a
    m = eye + n
    p = n
    k = 1
    while 2 * k < GDN_CHUNK:
        p = _raw1(p, p, _NN)
        m = m + _raw1(m, p, _NN)
        k *= 2
    r = eye - m - _raw3(a, m, _NN)
    return m + _raw1(m, r, _NN)


@jax.custom_vjp
def _unit_lower_inv(a):
    return _inv_newton(a)


def _unit_lower_inv_fwd(a):
    m = _inv_newton(a)
    return m, m


def _unit_lower_inv_bwd(m, dm):
    return (-_raw3(_raw3(m, dm, _TN), m, _NT),)


_unit_lower_inv.defvjp(_unit_lower_inv_fwd, _unit_lower_inv_bwd)


def _gdn_local(q, k, v, dcb, dcb128, dlb128, bb128):
    n = q.shape[0]
    r, s = _iota2((n, n), 0), _iota2((n, n), 1)
    same = (r // GDN_CHUNK) == (s // GDN_CHUNK)
    causal, strict = same & (r >= s), same & (r > s)
    decay_l = jnp.exp(jnp.where(causal, dcb - dcb.T, NEG))
    kb = k * bb128
    a = jnp.where(strict, _mm1_nt(kb, k) * decay_l, 0.0)
    m_off = _unit_lower_inv(a) - _eye(n)
    edc = jnp.exp(dcb128)
    rhs = jnp.concatenate([v * bb128, kb * edc], axis=1)
    sol = rhs + _mm3(m_off, rhs)
    aqk = jnp.where(causal, _mm1_nt(q, k) * decay_l, 0.0)
    return sol, aqk, q * edc, k * jnp.exp(dlb128 - dcb128)


def _gated_norm(o, z, nw):
    r = lax.rsqrt(jnp.mean(o * o, axis=-1, keepdims=True) + NORM_EPS)
    return o * r * nw * _silu(z)


def _gated_norm_bwd(dy, o, z, nw):
    r = lax.rsqrt(jnp.mean(o * o, axis=-1, keepdims=True) + NORM_EPS)
    xhat = o * r
    sz = _silu(z)
    dxhat = dy * nw * sz
    do = r * (dxhat - xhat * jnp.mean(dxhat * xhat, axis=-1, keepdims=True))
    dz = dy * xhat * nw * _silu_grad(z)
    dnw = jnp.sum(dy * xhat * sz, axis=0, keepdims=True)
    return do, dz, dnw


GDN_HEAD_GROUP = 2
GC_BETA, GC_G, GC_DECAY, GC_LAST =0, GDN_HEADS, 2 * GDN_HEADS, 3 * GDN_HEADS


def _gate_cols(gc, h, rows):
    lane = _iota2((rows, LANE), 1)
    col = lambda off: jnp.sum(jnp.where(lane == off + h, gc, 0.0), axis=-1, keepdims=True)
    return col(GC_BETA), col(GC_DECAY), col(GC_LAST)


def _gdn_local_args(q_ref, k_ref, v_ref, gc_ref, h, n):
    beta, dc, dl = _gate_cols(gc_ref[...], h, n)
    args = (q_ref[...], k_ref[...], v_ref[...], jnp.broadcast_to(dc, (n, n)), jnp.broadcast_to(dc, (n, LANE)),
            jnp.broadcast_to(dl, (n, LANE)), jnp.broadcast_to(beta, (n, LANE)))
    return args, dl


def gdn_chunk_fwd(qkv, gc, proj, norm_w):
    t = qkv.shape[0]
    c = GDN_CHUNK
    tc = _tile(t, 256)
    nsub = tc // c

    hp = GDN_HEAD_GROUP

    def body(q_ref, k_ref, v_ref, gc_ref, z_ref, nw_ref, on_ref, st_ref, s_ref):
        @pl.when(pl.program_id(1) == 0)
        def _():
            s_ref[...] = jnp.zeros_like(s_ref)

        for hh in range(hp):
            h = pl.program_id(0) * hp + hh
            ls = slice(hh * LANE, (hh + 1) * LANE)
            args, dl = _gdn_local_args(q_ref.at[:, ls], k_ref.at[:, ls], v_ref.at[:, ls], gc_ref, h, tc)
            sol, aqk, q_dec, k_dec = _gdn_local(*args)
            u, w = sol[:, :GDN_DK], sol[:, GDN_DK:]
            s = s_ref[hh]
            v_new, o_state = [], []
            for j in range(nsub):
                sl = slice(j * c, (j + 1) * c)
                st_ref[hh, j] = s
                v_new.append(u[sl] - _bdot(w[sl], s))
                o_state.append(_bdot(q_dec[sl], s))
                s = s * jnp.exp(dl[j * c:j * c + 1]) + _bdot_tn(k_dec[sl], v_new[j])
            s_ref[hh] = s
            o = jnp.concatenate(o_state, axis=0) + _bdot(aqk, jnp.concatenate(v_new, axis=0))
            on_ref[:, ls] = _gated_norm(o, z_ref[:, ls], nw_ref[...]).astype(BF16)

    col = lambda off: pl.BlockSpec((tc, hp * LANE), lambda g, i: (i, off // hp + g))
    return pl.pallas_call(
        body, name="gdn_chunk_fwd",
        grid=(GDN_HEADS // hp, t // tc),
        in_specs=[col(0), col(GDN_HEADS), col(2 * GDN_HEADS),
                  pl.BlockSpec((tc, LANE), lambda g, i: (i, 0)),
                  col(GDN_QKV_W // LANE),
                  pl.BlockSpec((1, LANE), lambda g, i: (0, 0))],
        out_specs=[col(0),
                   pl.BlockSpec((hp, nsub, GDN_DK, GDN_DK), lambda g, i: (g, i, 0, 0))],
        out_shape=[jax.ShapeDtypeStruct((t, GDN_HEADS * GDN_DK), BF16),
                   jax.ShapeDtypeStruct((GDN_HEADS, t // c, GDN_DK, GDN_DK), F32)],
        scratch_shapes=[pltpu.VMEM((hp, GDN_DK, GDN_DK), F32)],
        compiler_params=_arb(2),
    )(qkv, qkv, qkv, gc, proj, norm_w)


def gdn_chunk_bwd(qkv, gc, proj, norm_w, states, d_on):
    t = qkv.shape[0]
    c = GDN_CHUNK
    tc = _tile(t, 256)
    nsub = tc // c
    nblk = t // tc
    hp = GDN_HEAD_GROUP

    def body(q_ref, k_ref, v_ref, gc_ref, z_ref, nw_ref, st_ref, don_ref,
             dq_ref, dk_ref, dv_ref, dz_ref, dgc_ref, dnw_ref, ds_ref):
        @pl.when(pl.program_id(1) == 0)
        def _():
            ds_ref[...] = jnp.zeros_like(ds_ref)

        @pl.when((pl.program_id(1) == 0) & (pl.program_id(0) == 0))
        def _():
            dnw_ref[...] = jnp.zeros_like(dnw_ref)

        rows = [slice(j * c, (j + 1) * c) for j in range(nsub)]
        cat = lambda xs: jnp.concatenate(xs, axis=0)
        lsum = lambda x: jnp.sum(x, axis=-1, keepdims=True)
        lane = _iota2((tc, LANE), 1)
        row = _iota2((tc, 1), 0)
        dgc = jnp.zeros((tc, LANE), F32)
        for hh in range(hp):
            h = pl.program_id(0) * hp + hh
            ls = slice(hh * LANE, (hh + 1) * LANE)
            args, dl = _gdn_local_args(q_ref.at[:, ls], k_ref.at[:, ls], v_ref.at[:, ls], gc_ref, h, tc)
            (sol, aqk, q_dec, k_dec), vjp = jax.vjp(_gdn_local, *args)
            u, w = sol[:, :GDN_DK], sol[:, GDN_DK:]
            states = [st_ref[hh, j] for j in range(nsub)]
            v_new = [u[sl] - _bdot(w[sl], s) for sl, s in zip(rows, states)]
            v_all = cat(v_new)
            o = cat([_bdot(q_dec[sl], s) for sl, s in zip(rows, states)]) + _bdot(aqk, v_all)
            do, dz, dnw = _gated_norm_bwd(don_ref[:, ls], o, z_ref[:, ls], nw_ref[...])
            dnw_ref[...] += dnw
            dz_ref[:, ls] = dz.astype(BF16)
            d_aqk = _bdot_nt(do, v_all)
            dv_o = _bdot_tn(aqk, do)
            ds = ds_ref[hh]
            d_u, d_w, d_qdec, d_kdec, d_last = ([None] * nsub for _ in range(5))
            for j in reversed(range(nsub)):
                sl, s = rows[j], states[j]
                cd = jnp.exp(dl[j * c:j * c + 1])
                d_u[j] = dv_o[sl] + _bdot(k_dec[sl], ds)
                d_qdec[j] = _bdot_nt(do[sl], s)
                d_kdec[j] = _bdot_nt(v_new[j], ds)
                d_last[j] = jnp.sum(lsum(s * ds), axis=0, keepdims=True) * cd
                d_w[j] = -_bdot_nt(d_u[j], s)
                ds = _bdot_tn(q_dec[sl], do[sl]) + cd * ds - _bdot_tn(w[sl], d_u[j])
            ds_ref[hh] = ds
            d_sol = jnp.concatenate([cat(d_u), cat(d_w)], axis=1)
            dq, dk, dv, d_dcb, d_dcb128, d_dlb128, d_bb = vjp((d_sol, d_aqk, cat(d_qdec), cat(d_kdec)))
            dq_ref[:, ls] = dq
            dk_ref[:, ls] = dk
            dv_ref[:, ls] = dv
            d_dl = lsum(d_dlb128)
            for j in range(nsub):
                d_dl = d_dl + jnp.where(row == j * c, d_last[j], 0.0)
            dgc = dgc + jnp.where(lane == GC_BETA + h, lsum(d_bb),
                                  jnp.where(lane == GC_DECAY + h, lsum(d_dcb) + lsum(d_dcb128),
                                            jnp.where(lane == GC_LAST + h, d_dl, 0.0)))
        dgc_ref[0] = dgc

    rev = lambda i: nblk - 1 - i
    col = lambda off: pl.BlockSpec((tc, hp * LANE), lambda g, i: (rev(i), off // hp + g))
    return pl.pallas_call(
        body, name="gdn_chunk_bwd",
        grid=(GDN_HEADS // hp, nblk),
        in_specs=[col(0), col(GDN_HEADS), col(2 * GDN_HEADS),
                  pl.BlockSpec((tc, LANE), lambda g, i: (rev(i), 0)),
                  col(GDN_QKV_W // LANE),
                  pl.BlockSpec((1, LANE), lambda g, i: (0, 0)),
                  pl.BlockSpec((hp, nsub, GDN_DK, GDN_DK), lambda g, i: (g, rev(i), 0, 0)),
                  col(0)],
        out_specs=[col(0), col(0), col(0), col(0),
                   pl.BlockSpec((1, tc, LANE), lambda g, i: (g, rev(i), 0)),
                   pl.BlockSpec((1, LANE), lambda g, i: (0, 0))],
        out_shape=[
            jax.ShapeDtypeStruct((t, GDN_HEADS * GDN_DK), F32),
            jax.ShapeDtypeStruct((t, GDN_HEADS * GDN_DK), F32),
            jax.ShapeDtypeStruct((t, GDN_HEADS * GDN_DK), F32),
            jax.ShapeDtypeStruct((t, GDN_HEADS * GDN_DK), BF16),
            jax.ShapeDtypeStruct((GDN_HEADS // hp, t, LANE), F32),
            jax.ShapeDtypeStruct((1, LANE), F32)],
        scratch_shapes=[pltpu.VMEM((hp, GDN_DK, GDN_DK), F32)],
        compiler_params=_arb(2),
    )(qkv, qkv, qkv, gc, proj, norm_w, states, d_on)


GDN_PROJ_W = GDN_QKV_W + 1024 + LANE
GDN_Q_SCALE = GDN_DK ** -0.5


def _shift_rows(ext, shift, lo, n):
    if shift == 0:
        return ext[lo:lo + n]
    return pltpu.roll(ext, shift, 0)[lo:lo + n]


def _conv_fwd(x, halo, w):
    n = x.shape[0]
    ext = jnp.concatenate([halo, x], axis=0)
    y = w[GDN_CONV - 1:GDN_CONV] * x
    for j in range(GDN_CONV - 1):
        y = y + w[j:j + 1] * _shift_rows(ext, GDN_CONV - 1 - j, 8, n)
    return y


def _chunk_masks(n):
    r, s = _iota2((n, n), 0), _iota2((n, n), 1)
    same = (r // GDN_CHUNK) == (s // GDN_CHUNK)
    return (same & (r >= s)).astype(F32), (same & (r <= s)).astype(F32), same.astype(F32)


def _softplus(x):
    return jnp.maximum(x, 0.0) + jnp.log(1.0 + jnp.exp(-jnp.abs(x)))


def _l2n(t):
    rs = lax.rsqrt(jnp.sum(t * t, axis=-1, keepdims=True) + NORM_EPS)
    return t * rs, rs


def gdn_pre_fwd(proj, conv_w, gate_par):
    t = proj.shape[0]
    tm = _tile(t, 256)

    def body(x_ref, halo_ref, ba_ref, w_ref, gp_ref, qkv_ref, gc_ref):
        i = pl.program_id(0)
        halo = jnp.where(i > 0, halo_ref[...], 0.0)
        y = _silu(_conv_fwd(x_ref[...], halo, w_ref[...]))
        for hh in range(2 * GDN_HEADS):
            sl = slice(hh * LANE, (hh + 1) * LANE)
            tn, _ = _l2n(y[:, sl])
            qkv_ref[:, sl] = tn * GDN_Q_SCALE if hh < GDN_HEADS else tn
        qkv_ref[:, 2 * GDN_HEADS * LANE:] = y[:, 2 * GDN_HEADS * LANE:]
        ba = ba_ref[...]
        lane = _iota2(ba.shape, 1)
        gp = gp_ref[...]
        is_a = (lane >= GC_G) & (lane < GC_G + GDN_HEADS)
        g = jnp.where(is_a, -jnp.exp(gp[0:1]) * _softplus(ba + gp[1:2]), 0.0)
        tri, _, same = _chunk_masks(tm)
        decay = pltpu.roll(_dot(tri, g, HI), GC_DECAY - GC_G, 1)
        last = pltpu.roll(_dot(same, g, HI), GC_LAST - GC_G, 1)
        gc_ref[...] = jnp.where(lane < GDN_HEADS, _sigmoid(ba), g) + decay + last

    return pl.pallas_call(
        body, name="gdn_pre_fwd",
        grid=(t // tm,),
        in_specs=[pl.BlockSpec((tm, GDN_QKV_W), lambda i: (i, 0)),
                  pl.BlockSpec((8, GDN_QKV_W), lambda i: (jnp.maximum(i * (tm // 8) - 1, 0), 0)),
                  pl.BlockSpec((tm, LANE), lambda i: (i, (GDN_QKV_W + 1024) // LANE)),
                  pl.BlockSpec((8, GDN_QKV_W), lambda i: (0, 0)),
                  pl.BlockSpec((8, LANE), lambda i: (0, 0))],
        out_specs=[pl.BlockSpec((tm, GDN_QKV_W), lambda i: (i, 0)),
                   pl.BlockSpec((tm, LANE), lambda i: (i, 0))],
        out_shape=[jax.ShapeDtypeStruct((t, GDN_QKV_W), F32), jax.ShapeDtypeStruct((t, LANE), F32)],
        compiler_params=_arb(1),
    )(proj, proj, proj, conv_w, gate_par)


def gdn_pre_bwd(proj, conv_w, gate_par, dq, dk, dv, dgc_heads):
    t = proj.shape[0]
    tm = _tile(t, 256)

    def body(x_ref, halo_ref, ba_ref, w_ref, gp_ref, dq_ref, dk_ref, dv_ref, dgc_ref, dy_ref, dba_ref, dgp_ref):
        i = pl.program_id(0)

        @pl.when(i == 0)
        def _():
            dgp_ref[...] = jnp.zeros_like(dgp_ref)

        halo = jnp.where(i > 0, halo_ref[...], 0.0)
        y = _conv_fwd(x_ref[...], halo, w_ref[...])
        for hh in range(3 * GDN_HEADS):
            sl = slice(hh * LANE, (hh + 1) * LANE)
            hsl = slice((hh % GDN_HEADS) * LANE, (hh % GDN_HEADS + 1) * LANE)
            yy = y[:, sl]
            if hh < 2 * GDN_HEADS:
                tn, rs = _l2n(_silu(yy))
                dtn = dq_ref[:, hsl] * GDN_Q_SCALE if hh < GDN_HEADS else dk_ref[:, hsl]
                dsil = rs * (dtn - tn * jnp.sum(dtn * tn, axis=-1, keepdims=True))
            else:
                dsil = dv_ref[:, hsl]
            dy_ref[:, sl] = dsil * _silu_grad(yy)
        dgc = dgc_ref[0]
        for hh in range(1, dgc_heads.shape[0]):
            dgc = dgc + dgc_ref[hh]
        ba = ba_ref[...]
        lane = _iota2(ba.shape, 1)
        _, tri_t, same = _chunk_masks(tm)
        d_decay = jnp.where((lane >= GC_DECAY) & (lane < GC_DECAY + GDN_HEADS), dgc, 0.0)
        d_last = jnp.where((lane >= GC_LAST) & (lane < GC_LAST + GDN_HEADS), dgc, 0.0)
        dgc = (jnp.where(lane < GDN_HEADS, dgc, 0.0) + pltpu.roll(_dot(tri_t, d_decay, HI), LANE - (GC_DECAY - GC_G), 1)
               + pltpu.roll(_dot(same, d_last, HI), LANE - (GC_LAST - GC_G), 1))
        gp = gp_ref[...]
        xg = ba + gp[1:2]
        ea = jnp.exp(gp[0:1])
        sp = _softplus(xg)
        sb = _sigmoid(ba)
        is_b = lane < GDN_HEADS
        is_a = (lane >= GDN_HEADS) & (lane < 2 * GDN_HEADS)
        d_pre = jnp.where(is_a, dgc * (-ea) * _sigmoid(xg), 0.0)
        dba_ref[...] = jnp.where(is_b, dgc * sb * (1.0 - sb), d_pre).astype(BF16)
        d_alog = jnp.sum(jnp.where(is_a, dgc * (-ea) * sp, 0.0), axis=0, keepdims=True)
        d_dtb = jnp.sum(d_pre, axis=0, keepdims=True)
        row = _iota2((8, LANE), 0)
        dgp_ref[...] += jnp.where(row == 0, d_alog, jnp.where(row == 1, d_dtb, 0.0))

    hspec = pl.BlockSpec((tm, GDN_HEADS * LANE), lambda i: (i, 0))
    return pl.pallas_call(
        body, name="gdn_pre_bwd",
        grid=(t // tm,),
        in_specs=[pl.BlockSpec((tm, GDN_QKV_W), lambda i: (i, 0)),
                  pl.BlockSpec((8, GDN_QKV_W), lambda i: (jnp.maximum(i * (tm // 8) - 1, 0), 0)),
                  pl.BlockSpec((tm, LANE), lambda i: (i, (GDN_QKV_W + 1024) // LANE)),
                  pl.BlockSpec((8, GDN_QKV_W), lambda i: (0, 0)),
                  pl.BlockSpec((8, LANE), lambda i: (0, 0)),
                  hspec, hspec, hspec,
                  pl.BlockSpec((dgc_heads.shape[0], tm, LANE), lambda i: (0, i, 0))],
        out_specs=[pl.BlockSpec((tm, GDN_QKV_W), lambda i: (i, 0)),
                   pl.BlockSpec((tm, LANE), lambda i: (i, 0)),
                   pl.BlockSpec((8, LANE), lambda i: (0, 0))],
        out_shape=[jax.ShapeDtypeStruct((t, GDN_QKV_W), F32), jax.ShapeDtypeStruct((t, LANE), BF16),
                   jax.ShapeDtypeStruct((8, LANE), F32)],
        compiler_params=_arb(1),
    )(proj, proj, proj, conv_w, gate_par, dq, dk, dv, dgc_heads)


def gdn_conv_bwd(proj, conv_w, dy):
    t = proj.shape[0]
    tm = _tile(t, 256)
    nblk = t // tm

    def body(x_ref, halo_ref, w_ref, dy_ref, dyn_ref, dx_ref, dw_ref):
        i = pl.program_id(0)

        @pl.when(i == 0)
        def _():
            dw_ref[...] = jnp.zeros_like(dw_ref)

        w = w_ref[...]
        dy = dy_ref[...]
        ext_dy = jnp.concatenate([dy, jnp.where(i < nblk - 1, dyn_ref[...], 0.0)], axis=0)
        ext_x = jnp.concatenate([jnp.where(i > 0, halo_ref[...], 0.0), x_ref[...]], axis=0)
        dx = w[GDN_CONV - 1:GDN_CONV] * dy
        rows = [jnp.sum(dy * x_ref[...], axis=0, keepdims=True)]
        for j in range(GDN_CONV - 1):
            sh = GDN_CONV - 1 - j
            dx = dx + w[j:j + 1] * _shift_rows(ext_dy, tm + 8 - sh, 0, tm)
            rows.insert(j, jnp.sum(dy * _shift_rows(ext_x, sh, 8, tm), axis=0, keepdims=True))
        dx_ref[...] = dx.astype(BF16)
        row = _iota2((8, GDN_QKV_W), 0)
        acc = jnp.zeros((8, GDN_QKV_W), F32)
        for j in range(GDN_CONV):
            acc = acc + jnp.where(row == j, rows[j], 0.0)
        dw_ref[...] += acc

    return pl.pallas_call(
        body, name="gdn_conv_bwd",
        grid=(nblk,),
        in_specs=[pl.BlockSpec((tm, GDN_QKV_W), lambda i: (i, 0)),
                  pl.BlockSpec((8, GDN_QKV_W), lambda i: (jnp.maximum(i * (tm // 8) - 1, 0), 0)),
                  pl.BlockSpec((8, GDN_QKV_W), lambda i: (0, 0)),
                  pl.BlockSpec((tm, GDN_QKV_W), lambda i: (i, 0)),
                  pl.BlockSpec((8, GDN_QKV_W), lambda i: (jnp.minimum((i + 1) * (tm // 8), t // 8 - 1), 0))],
        out_specs=[pl.BlockSpec((tm, GDN_QKV_W), lambda i: (i, 0)),
                   pl.BlockSpec((8, GDN_QKV_W), lambda i: (0, 0))],
        out_shape=[jax.ShapeDtypeStruct((t, GDN_QKV_W), BF16), jax.ShapeDtypeStruct((8, GDN_QKV_W), F32)],
        compiler_params=_arb(1),
    )(proj, proj, conv_w, dy, dy)


def _resident(w_hbm, w_vmem, sem):
    @pl.when(pl.program_id(0) == 0)
    def _():
        cp = pltpu.make_async_copy(w_hbm, w_vmem, sem)
        cp.start()
        cp.wait()


ANY = pl.BlockSpec(memory_space=pl.ANY)


def norm_proj(h, nw, w, bias):
    t, d = h.shape
    n = w.shape[1]
    tm = _tile(t, 256)
    nc = _tile(n, 1536) if n % 1536 == 0 else _tile(n, 1408)

    def body(h_ref, nw_ref, w_hbm, b_ref, o_ref, w_ref, sem):
        _resident(w_hbm, w_ref, sem)
        hn = _rms(h_ref[...], nw_ref[...])[0].astype(BF16)
        for c0 in range(0, n, nc):
            o_ref[:, c0:c0 + nc] = _dot(hn, w_ref[:, c0:c0 + nc]) + b_ref[:, c0:c0 + nc]

    return pl.pallas_call(
        body, name="norm_proj",
        grid=(t // tm,),
        in_specs=[pl.BlockSpec((tm, d), lambda i: (i, 0)), pl.BlockSpec((1, d), lambda i: (0, 0)), ANY,
                  pl.BlockSpec((1, n), lambda i: (0, 0))],
        out_specs=pl.BlockSpec((tm, n), lambda i: (i, 0)),
        out_shape=jax.ShapeDtypeStruct((t, n), F32),
        scratch_shapes=[pltpu.VMEM((d, n), BF16), pltpu.SemaphoreType.DMA],
        compiler_params=_arb(1),
    )(h, nw, w, bias)


def linear_residual(h, x, w, bias):
    t, d = h.shape
    k = x.shape[1]
    tm = _tile(t, 512)

    def body(h_ref, x_ref, w_hbm, b_ref, o_ref, w_ref, sem):
        _resident(w_hbm, w_ref, sem)
        o_ref[...] = h_ref[...] + _dot(x_ref[...], w_ref[...]) + b_ref[...]

    return pl.pallas_call(
        body, name="linear_residual",
        grid=(t // tm,),
        in_specs=[pl.BlockSpec((tm, d), lambda i: (i, 0)), pl.BlockSpec((tm, k), lambda i: (i, 0)), ANY,
                  pl.BlockSpec((1, d), lambda i: (0, 0))],
        out_specs=pl.BlockSpec((tm, d), lambda i: (i, 0)),
        out_shape=jax.ShapeDtypeStruct((t, d), F32),
        scratch_shapes=[pltpu.VMEM((k, d), BF16), pltpu.SemaphoreType.DMA],
        compiler_params=_arb(1),
    )(h, x, w, bias)


def matmul_nt(dy, w):
    t, d = dy.shape
    k = w.shape[0]
    tm = _tile(t, 512)

    def body(dy_ref, w_hbm, o_ref, cs_ref, w_ref, sem):
        _resident(w_hbm, w_ref, sem)

        @pl.when(pl.program_id(0) == 0)
        def _():
            cs_ref[...] = jnp.zeros_like(cs_ref)

        dy = dy_ref[...]
        cs_ref[...] += jnp.sum(dy, axis=0, keepdims=True)
        o_ref[...] = _dot_nt(dy.astype(BF16), w_ref[...])

    return pl.pallas_call(
        body, name="matmul_nt",
        grid=(t // tm,),
        in_specs=[pl.BlockSpec((tm, d), lambda i: (i, 0)), ANY],
        out_specs=[pl.BlockSpec((tm, k), lambda i: (i, 0)), pl.BlockSpec((1, d), lambda i: (0, 0))],
        out_shape=[jax.ShapeDtypeStruct((t, k), F32), jax.ShapeDtypeStruct((1, d), F32)],
        scratch_shapes=[pltpu.VMEM((k, d), BF16), pltpu.SemaphoreType.DMA],
        compiler_params=_arb(1),
    )(dy, w)


def norm_proj_bwd(h, nw, dh, dps, ws):
    t, d = h.shape
    np_ = len(dps)
    ns = [w.shape[1] for w in ws]
    tm = _tile(t, 256)

    def body(*refs):
        h_ref, nw_ref, dh_ref = refs[:3]
        dp_refs = refs[3:3 + np_]
        w_hbms = refs[3 + np_:3 + 2 * np_]
        o_ref, hn_ref, dnw_ref = refs[3 + 2 * np_:6 + 2 * np_]
        cs_refs = refs[6 + 2 * np_:6 + 3 * np_]
        w_refs = refs[6 + 3 * np_:6 + 4 * np_]
        sem = refs[6 + 4 * np_]
        for a, b in zip(w_hbms, w_refs):
            _resident(a, b, sem)

        @pl.when(pl.program_id(0) == 0)
        def _():
            dnw_ref[...] = jnp.zeros_like(dnw_ref)
            for c in cs_refs:
                c[...] = jnp.zeros_like(c)

        nw = nw_ref[...]
        hn, xhat, r = _rms(h_ref[...], nw)
        hn_ref[...] = hn.astype(BF16)
        dhn = jnp.zeros((tm, d), F32)
        for dp_ref, w_ref, cs_ref in zip(dp_refs, w_refs, cs_refs):
            dp = dp_ref[...]
            cs_ref[...] += jnp.sum(dp.astype(F32), axis=0, keepdims=True)
            dhn = dhn + _dot_nt(dp, w_ref[...])
        dx, dnw = _rms_bwd(dhn, nw, xhat, r)
        dnw_ref[...] += dnw
        o_ref[...] = dh_ref[...] + dx

    row = pl.BlockSpec((tm, d), lambda i: (i, 0))
    vec = pl.BlockSpec((1, d), lambda i: (0, 0))
    return pl.pallas_call(
        body, name="norm_proj_bwd",
        grid=(t // tm,),
        in_specs=[row, vec, row] + [pl.BlockSpec((tm, n), lambda i: (i, 0)) for n in ns] + [ANY] * np_,
        out_specs=[row, row, vec] + [pl.BlockSpec((1, n), lambda i: (0, 0)) for n in ns],
        out_shape=[jax.ShapeDtypeStruct((t, d), F32), jax.ShapeDtypeStruct((t, d), BF16),
                   jax.ShapeDtypeStruct((1, d), F32)] + [jax.ShapeDtypeStruct((1, n), F32) for n in ns],
        scratch_shapes=[pltpu.VMEM((d, n), BF16) for n in ns] + [pltpu.SemaphoreType.DMA],
        compiler_params=_arb(1),
    )(h, nw, dh, *dps, *ws)


def matmul_tn(x, y, scale=1.0):
    t, k = x.shape
    n = y.shape[1]
    tk = _tile(k, 1024) if k % 1024 == 0 else _tile(k, 1408)
    tn = n if n <= 1536 else (1024 if n % 1024 == 0 else 1408)
    assert n % tn == 0
    tt = _tile(t, 1024)
    nt = t // tt

    def body(x_ref, y_ref, o_ref):
        @pl.when(pl.program_id(2) == 0)
        def _():
            o_ref[...] = jnp.zeros_like(o_ref)

        yv = y_ref[...]
        if scale != 1.0:
            yv = yv * scale
        o_ref[...] += _dot_tn(x_ref[...].astype(BF16), yv.astype(BF16))

    return pl.pallas_call(
        body, name="matmul_tn",
        grid=(k // tk, n // tn, nt),
        in_specs=[pl.BlockSpec((tt, tk), lambda i, j, s: (s, i)), pl.BlockSpec((tt, tn), lambda i, j, s: (s, j))],
        out_specs=pl.BlockSpec((tk, tn), lambda i, j, s: (i, j)),
        out_shape=jax.ShapeDtypeStruct((k, n), F32),
        compiler_params=_arb(3),
    )(x, y)


FFN_CHUNKS = 2


def ffn_fwd(h, nw, wgu, wd):
    t, d = h.shape
    f = wd.shape[0]
    fc = f // FFN_CHUNKS
    tm = _tile(t, 512)

    def body(h_ref, nw_ref, wgu_hbm, wd_hbm, o_ref, wgu_ref, wd_ref, sem):
        _resident(wgu_hbm, wgu_ref, sem)
        _resident(wd_hbm, wd_ref, sem)
        x = h_ref[...]
        hn = _rms(x, nw_ref[...])[0].astype(BF16)
        acc = jnp.zeros((tm, d), F32)
        for c in range(FFN_CHUNKS):
            g = _dot(hn, wgu_ref[:, c * fc:(c + 1) * fc])
            u = _dot(hn, wgu_ref[:, f + c * fc:f + (c + 1) * fc])
            acc = acc + _dot((_silu(g) * u).astype(BF16), wd_ref[c * fc:(c + 1) * fc, :])
        o_ref[...] = x + 0.5 * acc

    return pl.pallas_call(
        body, name="ffn_fwd",
        grid=(t // tm,),
        in_specs=[pl.BlockSpec((tm, d), lambda i: (i, 0)), pl.BlockSpec((1, d), lambda i: (0, 0)), ANY, ANY],
        out_specs=pl.BlockSpec((tm, d), lambda i: (i, 0)),
        out_shape=jax.ShapeDtypeStruct((t, d), F32),
        scratch_shapes=[pltpu.VMEM((d, 2 * f), BF16), pltpu.VMEM((f, d), BF16), pltpu.SemaphoreType.DMA],
        compiler_params=_arb(1),
    )(h, nw, wgu, wd)


def ffn_bwd(h, nw, wgu, wd, dh):
    t, d = h.shape
    f = wd.shape[0]
    fc = f // FFN_CHUNKS
    tm = _tile(t, 256)

    def body(h_ref, nw_ref, wgu_hbm, wd_hbm, dh_ref, o_ref, hn_ref, a_ref, dgu_ref, dnw_ref, wgu_ref, wd_ref, sem):
        _resident(wgu_hbm, wgu_ref, sem)
        _resident(wd_hbm, wd_ref, sem)

        @pl.when(pl.program_id(0) == 0)
        def _():
            dnw_ref[...] = jnp.zeros_like(dnw_ref)

        nw = nw_ref[...]
        hn32, xhat, r = _rms(h_ref[...], nw)
        hn = hn32.astype(BF16)
        hn_ref[...] = hn
        dh = dh_ref[...]
        dout = (0.5 * dh).astype(BF16)
        dhn = jnp.zeros((tm, d), F32)
        for c in range(FFN_CHUNKS):
            gs, us = slice(c * fc, (c + 1) * fc), slice(f + c * fc, f + (c + 1) * fc)
            g = _dot(hn, wgu_ref[:, gs])
            u = _dot(hn, wgu_ref[:, us])
            sg = _sigmoid(g)
            sil = g * sg
            a_ref[:, gs] = (sil * u).astype(BF16)
            da = _dot_nt(dout, wd_ref[gs, :])
            dg = (da * u * (sg * (1.0 + g * (1.0 - sg)))).astype(BF16)
            du = (da * sil).astype(BF16)
            dgu_ref[:, gs] = dg
            dgu_ref[:, us] = du
            dhn = dhn + _dot_nt(dg, wgu_ref[:, gs]) + _dot_nt(du, wgu_ref[:, us])
        dx, dnw = _rms_bwd(dhn, nw, xhat, r)
        dnw_ref[...] += dnw
        o_ref[...] = dh + dx

    row = pl.BlockSpec((tm, d), lambda i: (i, 0))
    vec = pl.BlockSpec((1, d), lambda i: (0, 0))
    return pl.pallas_call(
        body, name="ffn_bwd",
        grid=(t // tm,),
        in_specs=[row, vec, ANY, ANY, row],
        out_specs=[row, row, pl.BlockSpec((tm, f), lambda i: (i, 0)), pl.BlockSpec((tm, 2 * f), lambda i: (i, 0)), vec],
        out_shape=[jax.ShapeDtypeStruct((t, d), F32), jax.ShapeDtypeStruct((t, d), BF16),
                   jax.ShapeDtypeStruct((t, f), BF16), jax.ShapeDtypeStruct((t, 2 * f), BF16),
                   jax.ShapeDtypeStruct((1, d), F32)],
        scratch_shapes=[pltpu.VMEM((d, 2 * f), BF16), pltpu.VMEM((f, d), BF16), pltpu.SemaphoreType.DMA],
        compiler_params=_arb(1),
    )(h, nw, wgu, wd, dh)


def loss_head(h, nw, target):
    t, d = h.shape
    tm = _tile(t, 512)

    def body(h_ref, nw_ref, tg_ref, dh_ref, loss_ref, dnw_ref):
        @pl.when(pl.program_id(0) == 0)
        def _():
            loss_ref[...] = jnp.zeros_like(loss_ref)
            dnw_ref[...] = jnp.zeros_like(dnw_ref)

        nw = nw_ref[...]
        y, xhat, r = _rms(h_ref[...], nw)
        e = y - tg_ref[...]
        loss_ref[...] += 0.5 * jnp.sum(jnp.mean(e * e, axis=-1, keepdims=True), axis=0, keepdims=True)
        dx, dnw = _rms_bwd(e * (1.0 / d), nw, xhat, r)
        dnw_ref[...] += dnw
        dh_ref[...] = dx

    row = pl.BlockSpec((tm, d), lambda i: (i, 0))
    vec = pl.BlockSpec((1, d), lambda i: (0, 0))
    return pl.pallas_call(
        body, name="loss_head",
        grid=(t // tm,),
        in_specs=[row, vec, row],
        out_specs=[row, pl.BlockSpec((8, LANE), lambda i: (0, 0)), vec],
        out_shape=[jax.ShapeDtypeStruct((t, d), F32), jax.ShapeDtypeStruct((8, LANE), F32),
                   jax.ShapeDtypeStruct((1, d), F32)],
        compiler_params=_arb(1),
    )(h, nw, target)


def adamw(w, g, m, v):
    r, c = w.shape
    tr = r
    while tr * c * 4 > (1 << 20) and tr % 16 == 0:
        tr //= 2

    def body(w_ref, g_ref, m_ref, v_ref, d_ref, nm_ref, nv_ref):
        g = g_ref[...]
        m = ADAM_B1 * m_ref[...] + (1.0 - ADAM_B1) * g
        v = ADAM_B2 * v_ref[...] + (1.0 - ADAM_B2) * (g * g)
        m_hat = m / (1.0 - ADAM_B1 ** ADAM_STEP)
        v_hat = v / (1.0 - ADAM_B2 ** ADAM_STEP)
        d_ref[...] = -ADAM_LR * (m_hat / (jnp.sqrt(v_hat) + ADAM_EPS) + ADAM_WD * w_ref[...])
        nm_ref[...] = m
        nv_ref[...] = v

    blk = pl.BlockSpec((tr, c), lambda i: (i, 0))
    return pl.pallas_call(
        body, name="adamw",
        grid=(r // tr,),
        in_specs=[blk] * 4, out_specs=[blk] * 3,
        out_shape=[jax.ShapeDtypeStruct((r, c), F32)] * 3,
        compiler_params=_arb(1),
    )(w, g, m, v)


ATTN_P_W = ATTN_Q_W + 2 * ATTN_KV_W
ATTN_SCALE = ATTN_HEAD_DIM ** -0.5


def _rope_group(t, tab, sign):
    return (t * tab[:, 0:LANE] + sign * pltpu.roll(t, 8, 1) * tab[:, LANE:2 * LANE]
            + sign * pltpu.roll(t, LANE - 8, 1) * tab[:, 2 * LANE:3 * LANE])


def _rope(t, tab, sign=1.0):
    return jnp.concatenate([_rope_group(t[:, s:s + LANE], tab, sign) for s in range(0, t.shape[1], LANE)], axis=1)


def _attn_heads(qs, ks, vs, sinks, first):
    b = ATTN_BLOCK
    rows = ATTN_GROUP * b
    qi = _iota2((rows, 2 * b), 0) % b
    kj = _iota2((rows, 2 * b), 1)
    rel = qi + b - kj
    valid = (rel >= 0) & (rel < b) & ((kj >= b) | jnp.logical_not(first))
    outs = []
    for q, k, v, sink in zip(qs, ks, vs, sinks):
        s = jnp.where(valid, _mm1_nt(q, k) * ATTN_SCALE, NEG)
        m = lax.stop_gradient(jnp.maximum(jnp.max(s, axis=-1, keepdims=True), sink))
        p = jnp.exp(s - m)
        den = jnp.sum(p, axis=-1, keepdims=True) + jnp.exp(sink - m)
        outs.append(_mm1(p / den, v))
    return outs


def _attn_prepare(p_ref, kvp_ref, tab_ref, tabp_ref, sink_ref):
    b = ATTN_BLOCK
    hd = ATTN_HEAD_DIM
    tab, tabp = tab_ref[...], tabp_ref[...]
    q = _rope(p_ref[:, 0:ATTN_Q_W], tab)
    kc = _rope(p_ref[:, ATTN_Q_W:ATTN_Q_W + ATTN_KV_W], tab)
    kp = _rope(kvp_ref[:, 0:ATTN_KV_W], tabp)
    vc = p_ref[:, ATTN_Q_W + ATTN_KV_W:ATTN_P_W]
    vp = kvp_ref[:, ATTN_KV_W:2 * ATTN_KV_W]
    sk = sink_ref[...]
    qs, ks, vs, sinks = [], [], [], []
    for h in range(ATTN_KV_HEADS):
        heads = [ATTN_GROUP * h + g for g in range(ATTN_GROUP)]
        qs.append(jnp.concatenate([q[:, i * hd:(i + 1) * hd] for i in heads], axis=0))
        ks.append(jnp.concatenate([kp[:, h * hd:(h + 1) * hd], kc[:, h * hd:(h + 1) * hd]], axis=0))
        vs.append(jnp.concatenate([vp[:, h * hd:(h + 1) * hd], vc[:, h * hd:(h + 1) * hd]], axis=0))
        sinks.append(jnp.concatenate([jnp.broadcast_to(sk[:, i:i + 1], (b, 1)) for i in heads], axis=0))
    return qs, ks, vs, sinks


def _unstack_heads(xs):
    b = ATTN_BLOCK
    return jnp.concatenate([x[g * b:(g + 1) * b] for x in xs for g in range(ATTN_GROUP)], axis=1)


def _attn_specs(nb):
    b = ATTN_BLOCK
    prev = lambda n: jnp.maximum(n - 1, 0)
    return [pl.BlockSpec((b, ATTN_P_W), lambda n: (n, 0)),
            pl.BlockSpec((b, 2 * ATTN_KV_W), lambda n: (prev(n), ATTN_Q_W // (2 * ATTN_KV_W))),
            pl.BlockSpec((b, 3 * LANE), lambda n: (n, 0)),
            pl.BlockSpec((b, 3 * LANE), lambda n: (prev(n), 0)),
            pl.BlockSpec((1, ATTN_Q_HEADS), lambda n: (0, 0))]


def attn_fwd(p, tab, sinks):
    t = p.shape[0]
    nb = t // ATTN_BLOCK

    def body(p_ref, kvp_ref, tab_ref, tabp_ref, sink_ref, o_ref):
        qs, ks, vs, sk = _attn_prepare(p_ref, kvp_ref, tab_ref, tabp_ref, sink_ref)
        os_ = _attn_heads(qs, ks, vs, sk, pl.program_id(0) == 0)
        o_ref[...] = _unstack_heads(os_).astype(BF16)

    return pl.pallas_call(
        body, name="attn_fwd",
        grid=(nb,),
        in_specs=_attn_specs(nb),
        out_specs=pl.BlockSpec((ATTN_BLOCK, ATTN_Q_W), lambda n: (n, 0)),
        out_shape=jax.ShapeDtypeStruct((t, ATTN_Q_W), BF16),
        compiler_params=_arb(1),
    )(p, p, tab, tab, sinks)


def attn_bwd(p, tab, sinks, do):
    t = p.shape[0]
    b = ATTN_BLOCK
    hd = ATTN_HEAD_DIM
    nb = t // b

    def body(p_ref, kvp_ref, tab_ref, tabp_ref, sink_ref, do_ref, dq_ref, dkvc_ref, dkvp_ref, dsink_ref):
        n = pl.program_id(0)

        @pl.when(n == 0)
        def _():
            dsink_ref[...] = jnp.zeros_like(dsink_ref)

        qs, ks, vs, sk = _attn_prepare(p_ref, kvp_ref, tab_ref, tabp_ref, sink_ref)
        first = n == 0
        _, vjp = jax.vjp(lambda a, bb, c, d: _attn_heads(a, bb, c, d, first), qs, ks, vs, sk)
        do = do_ref[...]
        dos = [jnp.concatenate([do[:, i * hd:(i + 1) * hd] for i in range(ATTN_GROUP * h, ATTN_GROUP * (h + 1))], axis=0)
               for h in range(ATTN_KV_HEADS)]
        dqs, dks, dvs, dsk = vjp(dos)
        tab, tabp = tab_ref[...], tabp_ref[...]
        dq_ref[...] = _rope(_unstack_heads([x.astype(F32) for x in dqs]), tab, -1.0).astype(BF16)
        dkc = jnp.concatenate([x.astype(F32)[b:] for x in dks], axis=1)
        dkp = jnp.concatenate([x.astype(F32)[:b] for x in dks], axis=1)
        dkvc_ref[:, 0:ATTN_KV_W] = _rope(dkc, tab, -1.0)
        dkvp_ref[:, 0:ATTN_KV_W] = _rope(dkp, tabp, -1.0)
        dkvc_ref[:, ATTN_KV_W:] = jnp.concatenate([x.astype(F32)[b:] for x in dvs], axis=1)
        dkvp_ref[:, ATTN_KV_W:] = jnp.concatenate([x.astype(F32)[:b] for x in dvs], axis=1)
        parts = [jnp.sum(d[g * b:(g + 1) * b], axis=0, keepdims=True) for d in dsk for g in range(ATTN_GROUP)]
        dsink_ref[...] += jnp.concatenate(parts, axis=1)

    blk = lambda w: pl.BlockSpec((b, w), lambda n: (n, 0))
    return pl.pallas_call(
        body, name="attn_bwd",
        grid=(nb,),
        in_specs=_attn_specs(nb) + [blk(ATTN_Q_W)],
        out_specs=[blk(ATTN_Q_W), blk(2 * ATTN_KV_W), blk(2 * ATTN_KV_W),
                   pl.BlockSpec((1, ATTN_Q_HEADS), lambda n: (0, 0))],
        out_shape=[jax.ShapeDtypeStruct((t, ATTN_Q_W), BF16), jax.ShapeDtypeStruct((t, 2 * ATTN_KV_W), F32),
                   jax.ShapeDtypeStruct((t, 2 * ATTN_KV_W), F32), jax.ShapeDtypeStruct((1, ATTN_Q_HEADS), F32)],
        compiler_params=_arb(1),
    )(p, p, tab, tab, sinks, do)


def kv_combine(dkvc, dkvp):
    t, w = dkvc.shape
    b = ATTN_BLOCK
    nb = t // b

    def body(c_ref, p_ref, o_ref):
        nxt = jnp.where(pl.program_id(0) < nb - 1, p_ref[...], 0.0)
        o_ref[...] = (c_ref[...] + nxt).astype(BF16)

    return pl.pallas_call(
        body, name="kv_combine",
        grid=(nb,),
        in_specs=[pl.BlockSpec((b, w), lambda n: (n, 0)),
                  pl.BlockSpec((b, w), lambda n: (jnp.minimum(n + 1, nb - 1), 0))],
        out_specs=pl.BlockSpec((b, w), lambda n: (n, 0)),
        out_shape=jax.ShapeDtypeStruct((t, w), BF16),
        compiler_params=_arb(1),
    )(dkvc, dkvp)


def _me():
    return lax.axis_index("x"), lax.axis_index("y"), lax.axis_index("c")


def _flip(v, bit):
    return 1 - v if bit else v


def _chip_index():
    return 2 * lax.axis_index("x") + lax.axis_index("y")


def allgather_chips(x):
    r, c = x.shape

    def body(x_ref, o_ref, send_sems, recv_sems):
        mx, my, mc = _me()
        me = 2 * mx + my
        copies = []
        for k in (1, 2, 3):
            peer = (_flip(mx, k >> 1), _flip(my, k & 1), mc)
            cp = pltpu.make_async_remote_copy(src_ref=x_ref, dst_ref=o_ref.at[me], send_sem=send_sems.at[k - 1],
                                              recv_sem=recv_sems.at[k - 1], device_id=peer, device_id_type=MESH)
            cp.start()
            copies.append(cp)
        for cp in copies:
            cp.wait()

    got = pl.pallas_call(
        body, name="allgather_chips",
        in_specs=[ANY], out_specs=ANY,
        out_shape=jax.ShapeDtypeStruct((4, r, c), x.dtype),
        scratch_shapes=[pltpu.SemaphoreType.DMA((3,)), pltpu.SemaphoreType.DMA((3,))],
    )(x)
    return lax.dynamic_update_slice(got, x[None], (_chip_index(), 0, 0))


def pair_exchange(g):
    _, _, r, c = g.shape

    def body(g_ref, got_ref, send_sems, recv_sems):
        mx, my, mc = _me()
        sib = (mx, my, 1 - mc)
        copies = []
        for j in range(4):
            cp = pltpu.make_async_remote_copy(src_ref=g_ref.at[j, 1 - mc], dst_ref=got_ref.at[j], send_sem=send_sems.at[j],
                                              recv_sem=recv_sems.at[j], device_id=sib, device_id_type=MESH)
            cp.start()
            copies.append(cp)
        for cp in copies:
            cp.wait()

    return pl.pallas_call(
        body, name="pair_exchange",
        in_specs=[ANY], out_specs=ANY,
        out_shape=jax.ShapeDtypeStruct((4, r, c), g.dtype),
        scratch_shapes=[pltpu.SemaphoreType.DMA((4,))] * 2,
    )(g)


def scatter_chips(p):
    _, r, c = p.shape

    def body(p_ref, o_ref, send_sems, recv_sems):
        mx, my, mc = _me()
        me = 2 * mx + my
        copies = []
        for k in (1, 2, 3):
            px, py = _flip(mx, k >> 1), _flip(my, k & 1)
            cp = pltpu.make_async_remote_copy(src_ref=p_ref.at[2 * px + py], dst_ref=o_ref.at[me],
                                              send_sem=send_sems.at[k - 1], recv_sem=recv_sems.at[k - 1],
                                              device_id=(px, py, mc), device_id_type=MESH)
            cp.start()
            copies.append(cp)
        for cp in copies:
            cp.wait()

    return pl.pallas_call(
        body, name="scatter_chips",
        in_specs=[ANY], out_specs=ANY,
        out_shape=jax.ShapeDtypeStruct((4, r, c), p.dtype),
        scratch_shapes=[pltpu.SemaphoreType.DMA((3,)), pltpu.SemaphoreType.DMA((3,))],
    )(p)


def pair_share(o):
    r, c = o.shape

    def body(o_ref, out_ref, send_sem, recv_sem):
        mx, my, mc = _me()
        cp = pltpu.make_async_remote_copy(src_ref=o_ref, dst_ref=out_ref, send_sem=send_sem, recv_sem=recv_sem,
                                          device_id=(mx, my, 1 - mc), device_id_type=MESH)
        cp.start()
        cp.wait()

    return pl.pallas_call(
        body, name="pair_share",
        in_specs=[ANY], out_specs=ANY,
        out_shape=jax.ShapeDtypeStruct((r, c), o.dtype),
        scratch_shapes=[pltpu.SemaphoreType.DMA, pltpu.SemaphoreType.DMA],
    )(o)


def _row_tile(r, c):
    tr = 8
    while r % (2 * tr) == 0 and 2 * tr * c * 4 <= (2 << 20):
        tr *= 2
    return tr


def add_pair(g, got, core):
    _, _, r, c = g.shape
    tr = _row_tile(r, c)

    def body(core_ref, g_ref, got_ref, p32_ref, p16_ref):
        s = g_ref[...] + got_ref[...]
        p32_ref[...] = s
        p16_ref[...] = s.astype(BF16)

    blk = pl.BlockSpec((None, tr, c), lambda j, i, core_ref: (j, i, 0))
    return pl.pallas_call(
        body, name="add_pair",
        grid_spec=pltpu.PrefetchScalarGridSpec(
            num_scalar_prefetch=1, grid=(4, r // tr),
            in_specs=[pl.BlockSpec((None, None, tr, c), lambda j, i, core_ref: (j, core_ref[0], i, 0)), blk],
            out_specs=[blk, blk]),
        out_shape=[jax.ShapeDtypeStruct((4, r, c), F32), jax.ShapeDtypeStruct((4, r, c), BF16)],
        compiler_params=_arb(2),
    )(core, g, got)


def sum_slots(p32, q16, order):
    _, r, c = p32.shape
    tr = _row_tile(r, c)

    def body(order_ref, own_ref, a_ref, b_ref, c_ref, o_ref):
        o_ref[...] = ((own_ref[...] + a_ref[...].astype(F32)) + b_ref[...].astype(F32)) + c_ref[...].astype(F32)

    slot = lambda k: pl.BlockSpec((None, tr, c), functools.partial(lambda k, i, order_ref: (order_ref[k], i, 0), k))
    return pl.pallas_call(
        body, name="sum_slots",
        grid_spec=pltpu.PrefetchScalarGridSpec(
            num_scalar_prefetch=1, grid=(r // tr,),
            in_specs=[slot(0), slot(1), slot(2), slot(3)],
            out_specs=pl.BlockSpec((tr, c), lambda i, order_ref: (i, 0))),
        out_shape=jax.ShapeDtypeStruct((r, c), F32),
        compiler_params=_arb(1),
    )(order, p32, q16, q16, q16)


def allreduce_small(x):
    r, c = x.shape

    def body(x_ref, o_ref, buf, send_sems, recv_sems):
        mx, my, mc = _me()
        me = 4 * mx + 2 * my + mc
        buf[pl.ds(me, 1)] = x_ref[...][None]
        copies = []
        for k in range(1, 8):
            peer = (_flip(mx, k >> 2), _flip(my, (k >> 1) & 1), _flip(mc, k & 1))
            cp = pltpu.make_async_remote_copy(src_ref=x_ref, dst_ref=buf.at[me], send_sem=send_sems.at[k - 1],
                                              recv_sem=recv_sems.at[k - 1], device_id=peer, device_id_type=MESH)
            cp.start()
            copies.append(cp)
        for cp in copies:
            cp.wait()
        acc = buf[0]
        for d in range(1, 8):
            acc = acc + buf[d]
        o_ref[...] = acc

    return pl.pallas_call(
        body, name="allreduce_small",
        out_shape=jax.ShapeDtypeStruct((r, c), F32),
        scratch_shapes=[pltpu.VMEM((8, r, c), F32), pltpu.SemaphoreType.DMA((7,)), pltpu.SemaphoreType.DMA((7,))],
    )(x)


def reduce_scatter(g):
    _, r, c = g.shape
    rh = r // 2
    core = lax.axis_index("c")
    me = _chip_index()
    g = g.reshape(4, 2, rh, c)
    p32, p16 = add_pair(g, pair_exchange(g), core.astype(jnp.int32)[None])
    q16 = scatter_chips(p16)
    order = jnp.stack([me, me ^ 1, me ^ 2, me ^ 3]).astype(jnp.int32)
    mine = sum_slots(p32, q16, order)
    theirs = pair_share(mine)
    return jnp.where(core == 0, jnp.concatenate([mine, theirs]), jnp.concatenate([theirs, mine]))


PACK_W = 1024
PACK_ROWS = 1024

SHARDED = {"ffn1_w_gate_up": 1, "ffn1_w_down": 0, "ffn2_w_gate_up": 1, "ffn2_w_down": 0, "attn_w_in": 1,
           "attn_w_out": 0, "gdn_w_in": 1, "gdn_w_out": 0, "gdn_conv_w": 1}
REPLICATED = ["ffn1_norm", "mix_norm", "ffn2_norm", "attn_b_in", "attn_sinks", "attn_b_out", "gdn_A_log",
              "gdn_dt_bias", "gdn_norm_w", "final_norm"]
WEIGHTS = ["ffn1_norm", "ffn1_w_gate_up", "ffn1_w_down", "mix_norm", "ffn2_norm", "ffn2_w_gate_up", "ffn2_w_down",
           "attn_w_in", "attn_b_in", "attn_sinks", "attn_w_out", "attn_b_out", "gdn_w_in", "gdn_conv_w", "gdn_A_log",
           "gdn_dt_bias", "gdn_norm_w", "gdn_w_out", "final_norm"]


def _pack_rows(flats, dtype, width, row_multiple):
    flat = jnp.concatenate([f.astype(dtype).reshape(-1) for f in flats])
    per = width * row_multiple
    pad = (-flat.shape[0]) % per
    if pad:
        flat = jnp.concatenate([flat, jnp.zeros((pad,), dtype)])
    return flat.reshape(-1, width)


def _gather_weights(shards, dtype):
    names = list(shards)
    packed = _pack_rows([shards[n] for n in names], dtype, PACK_W, 16)
    got = allgather_chips(packed).reshape(4, -1)
    full, off = {}, 0
    for n in names:
        shape = shards[n].shape
        size = shards[n].size
        full[n] = jnp.concatenate([got[j, off:off + size].reshape(shape) for j in range(4)], axis=1 + SHARDED[n])
        off += size
    return full


def _scatter_grads(grads, shard_shapes):
    names = list(grads)
    slots = []
    for j in range(4):
        parts = []
        for n in names:
            ax = 1 + SHARDED[n]
            w = shard_shapes[n][ax]
            parts.append(lax.slice_in_dim(grads[n], j * w, (j + 1) * w, axis=ax))
        slots.append(_pack_rows(parts, F32, PACK_W, PACK_ROWS))
    red = reduce_scatter(jnp.stack(slots)).reshape(-1)
    out, off = {}, 0
    for n in names:
        size = 1
        for s in shard_shapes[n]:
            size *= s
        out[n] = red[off:off + size].reshape(shard_shapes[n])
        off += size
    return out


def _small_pack(items):
    rows = []
    for a in items:
        f = a.astype(F32).reshape(-1)
        pad = (-f.shape[0]) % LANE
        rows.append(jnp.concatenate([f, jnp.zeros((pad,), F32)]) if pad else f)
    return _pack_rows(rows, F32, LANE, 8)


def _small_unpack(buf, shapes):
    flat = buf.reshape(-1)
    out, off = [], 0
    for shape in shapes:
        size = 1
        for s in shape:
            size *= s
        out.append(flat[off:off + size].reshape(shape))
        off += size + (-size) % LANE
    return out


def _rope_table(positions):
    t = positions.shape[0]
    inv_freq = ROPE_THETA ** (-jnp.arange(0, ROPE_DIM, 2, dtype=F32) / ROPE_DIM)
    ang = positions.astype(F32)[:, None] * inv_freq
    cos, sin = jnp.cos(ang), jnp.sin(ang)
    rest = ATTN_HEAD_DIM - ROPE_DIM
    zeros = lambda n: jnp.zeros((t, n), F32)
    c64 = jnp.concatenate([cos, cos, jnp.ones((t, rest), F32)], axis=1)
    s_up = jnp.concatenate([zeros(ROPE_DIM // 2), sin, zeros(rest)], axis=1)
    s_dn = jnp.concatenate([-sin, zeros(ROPE_DIM // 2 + rest)], axis=1)
    return jnp.concatenate([c64, c64, s_up, s_up, s_dn, s_dn], axis=1)


def _as2d(a):
    return a.reshape(-1, a.shape[-1]) if a.ndim > 1 else a.reshape(1, -1)


def kernel(x, positions, ffn1_norm, ffn1_w_gate_up, ffn1_w_down, mix_norm, ffn2_norm, ffn2_w_gate_up, ffn2_w_down, attn_w_in, attn_b_in, attn_sinks, attn_w_out, attn_b_out, gdn_w_in, gdn_conv_w, gdn_A_log, gdn_dt_bias, gdn_norm_w, gdn_w_out, final_norm, loss_target, m_ffn1_norm, m_ffn1_w_gate_up, m_ffn1_w_down, m_mix_norm, m_ffn2_norm, m_ffn2_w_gate_up, m_ffn2_w_down, m_attn_w_in, m_attn_b_in, m_attn_sinks, m_attn_w_out, m_attn_b_out, m_gdn_w_in, m_gdn_conv_w, m_gdn_A_log, m_gdn_dt_bias, m_gdn_norm_w, m_gdn_w_out, m_final_norm, v_ffn1_norm, v_ffn1_w_gate_up, v_ffn1_w_down, v_mix_norm, v_ffn2_norm, v_ffn2_w_gate_up, v_ffn2_w_down, v_attn_w_in, v_attn_b_in, v_attn_sinks, v_attn_w_out, v_attn_b_out, v_gdn_w_in, v_gdn_conv_w, v_gdn_A_log, v_gdn_dt_bias, v_gdn_norm_w, v_gdn_w_out, v_final_norm):
    given = dict(locals())
    w = {n: given[n] for n in WEIGHTS}
    d = D_MODEL
    h = x[0]
    target = loss_target[0]
    depth = ffn1_norm.shape[0]

    big = [n for n in SHARDED if n != "gdn_conv_w"]
    full = _gather_weights({n: w[n] for n in big}, BF16)
    conv_full = _gather_weights({"gdn_conv_w": gdn_conv_w}, F32)["gdn_conv_w"]
    tab = _rope_table(positions[0])
    zero_d = jnp.zeros((1, d), F32)

    def gdn_params(j):
        w_in = full["gdn_w_in"][j]
        w_cat = jnp.concatenate([w_in, jnp.zeros((d, GDN_PROJ_W - w_in.shape[1]), BF16)], axis=1)
        conv = jnp.concatenate([conv_full[j], jnp.zeros((8 - GDN_CONV, GDN_QKV_W), F32)], axis=0)
        lanes = lambda vec: jnp.concatenate([jnp.zeros((GC_G,), F32), vec, jnp.zeros((LANE - GC_G - GDN_HEADS,), F32)])
        par = jnp.concatenate([lanes(gdn_A_log[j])[None], lanes(gdn_dt_bias[j])[None], jnp.zeros((6, LANE), F32)], axis=0)
        return w_cat, conv, par

    saved = []
    for l in range(depth):
        j = l // 2
        rec = {"h1": h}
        h = ffn_fwd(h, ffn1_norm[l][None], full["ffn1_w_gate_up"][l], full["ffn1_w_down"][l])
        rec["h2"] = h
        if l % 2 == 0:
            p = norm_proj(h, mix_norm[l][None], full["attn_w_in"][j], attn_b_in[j][None])
            o = attn_fwd(p, tab, attn_sinks[j][None])
            h = linear_residual(h, o, full["attn_w_out"][j], attn_b_out[j][None])
            rec.update(p=p, o=o)
        else:
            w_cat, conv, par = gdn_params(j)
            proj = norm_proj(h, mix_norm[l][None], w_cat, jnp.zeros((1, GDN_PROJ_W), F32))
            qkv, gc = gdn_pre_fwd(proj, conv, par)
            on, st = gdn_chunk_fwd(qkv, gc, proj, gdn_norm_w[j][None])
            h = linear_residual(h, on, full["gdn_w_out"][j], zero_d)
            rec.update(proj=proj, qkv=qkv, gc=gc, on=on, st=st, w_cat=w_cat, conv=conv, par=par)
        rec["h3"] = h
        h = ffn_fwd(h, ffn2_norm[l][None], full["ffn2_w_gate_up"][l], full["ffn2_w_down"][l])
        saved.append(rec)

    dh, loss_tile, d_final = loss_head(h, final_norm[None], target)

    g = {n: [None] * w[n].shape[0] for n in WEIGHTS if n != "final_norm"}
    for l in reversed(range(depth)):
        j = l // 2
        rec = saved[l]

        def ffn_back(tag, h_in, dh):
            wgu, wd = full[tag + "_w_gate_up"][l], full[tag + "_w_down"][l]
            dh_new, hn, a, dgu, dn = ffn_bwd(h_in, w[tag + "_norm"][l][None], wgu, wd, dh)
            g[tag + "_w_gate_up"][l] = matmul_tn(hn, dgu)
            g[tag + "_w_down"][l] = matmul_tn(a, dh, 0.5)
            g[tag + "_norm"][l] = dn[0]
            return dh_new

        dh = ffn_back("ffn2", rec["h3"], dh)
        if l % 2 == 0:
            w_in, w_out = full["attn_w_in"][j], full["attn_w_out"][j]
            do, db_out = matmul_nt(dh, w_out)
            g["attn_w_out"][j] = matmul_tn(rec["o"], dh)
            g["attn_b_out"][j] = db_out[0]
            dq, dkvc, dkvp, dsink = attn_bwd(rec["p"], tab, attn_sinks[j][None], do)
            dkv = kv_combine(dkvc, dkvp)
            dh, hn, dn, cs_q, cs_kv = norm_proj_bwd(rec["h2"], mix_norm[l][None], dh, [dq, dkv],
                                                    [w_in[:, :ATTN_Q_W], w_in[:, ATTN_Q_W:]])
            g["attn_w_in"][j] = jnp.concatenate([matmul_tn(hn, dq), matmul_tn(hn, dkv)], axis=1)
            g["attn_b_in"][j] = jnp.concatenate([cs_q[0], cs_kv[0]])
            g["attn_sinks"][j] = dsink[0]
        else:
            w_cat, conv, par = rec["w_cat"], rec["conv"], rec["par"]
            d_on, _ = matmul_nt(dh, full["gdn_w_out"][j])
            g["gdn_w_out"][j] = matmul_tn(rec["on"], dh)
            dq, dk, dv, dz, dgc_heads, dnw = gdn_chunk_bwd(rec["qkv"], rec["gc"], rec["proj"], gdn_norm_w[j][None],
                                                           rec["st"], d_on)
            dy, dba, dpar = gdn_pre_bwd(rec["proj"], conv, par, dq, dk, dv, dgc_heads)
            dx, dconv = gdn_conv_bwd(rec["proj"], conv, dy)
            nz = GDN_QKV_W + GDN_HEADS * GDN_DK
            dh, hn, dn, _, _, _ = norm_proj_bwd(rec["h2"], mix_norm[l][None], dh, [dx, dz, dba],
                                                [w_cat[:, :GDN_QKV_W], w_cat[:, GDN_QKV_W:nz], w_cat[:, nz:]])
            g["gdn_w_in"][j] = jnp.concatenate(
                [matmul_tn(hn, dx), matmul_tn(hn, dz), matmul_tn(hn, dba)[:, :2 * GDN_HEADS]], axis=1)
            g["gdn_conv_w"][j] = dconv[:GDN_CONV]
            g["gdn_A_log"][j] = dpar[0, GDN_HEADS:2 * GDN_HEADS]
            g["gdn_dt_bias"][j] = dpar[1, GDN_HEADS:2 * GDN_HEADS]
            g["gdn_norm_w"][j] = dnw[0]
        g["mix_norm"][l] = dn[0]
        dh = ffn_back("ffn1", rec["h1"], dh)
    grad_x = dh[None]

    local = {n: jnp.stack(v) for n, v in g.items()}
    local["final_norm"] = d_final[0]
    small_shapes = [(1,)] + [w[n].shape for n in REPLICATED]
    small = allreduce_small(_small_pack([loss_tile[0, 0:1]] + [local[n] for n in REPLICATED]))
    small = _small_unpack(small, small_shapes)
    loss = small[0][0]
    grads = dict(zip(REPLICATED, small[1:]))
    grads.update(_scatter_grads({n: local[n] for n in SHARDED}, {n: w[n].shape for n in SHARDED}))

    delta, new_m, new_v = {}, {}, {}
    for n in SHARDED:
        dl, nm, nv = adamw(_as2d(w[n]), _as2d(grads[n]), _as2d(given["m_" + n]), _as2d(given["v_" + n]))
        delta[n], new_m[n], new_v[n] = dl.reshape(w[n].shape), nm.reshape(w[n].shape), nv.reshape(w[n].shape)
    shapes = [w[n].shape for n in REPLICATED]
    packed = [_small_pack([src[n] for n in REPLICATED]) for src in
              (w, grads, {n: given["m_" + n] for n in REPLICATED}, {n: given["v_" + n] for n in REPLICATED})]
    for dst, buf in zip((delta, new_m, new_v), adamw(*packed)):
        dst.update(zip(REPLICATED, _small_unpack(buf, shapes)))

    return (loss, grad_x, *[grads[n] for n in WEIGHTS], *[delta[n] for n in WEIGHTS],
            *[new_m[n] for n in WEIGHTS], *[new_v[n] for n in WEIGHTS])
```

```python
import functools

import jax
import jax.numpy as jnp
from jax import lax
from jax.experimental import pallas as pl
from jax.experimental.pallas import tpu as pltpu

F32 = jnp.float32
BF16 = jnp.bfloat16
HI = lax.Precision.HIGHEST
MESH = pl.DeviceIdType.MESH

D_MODEL = 1024
D_FF = 2816
DEPTH = 4
NORM_EPS = 1e-6
LANE = 128

ATTN_Q_HEADS = 16
ATTN_KV_HEADS = 4
ATTN_HEAD_DIM = 64
ATTN_GROUP = 4
ATTN_BLOCK = 128
ROPE_DIM = 16
ROPE_THETA = 500000.0
ATTN_Q_W = 1024
ATTN_KV_W = 256

GDN_HEADS = 8
GDN_DK = 128
GDN_CONV = 4
GDN_CHUNK = 64
GDN_QKV_W = 3072

ADAM_LR = 0.001
ADAM_B1 = 0.9
ADAM_B2 = 0.999
ADAM_EPS = 1e-08
ADAM_WD = 0.01
ADAM_STEP = 10

NEG = -1e30


def _dot(a, b, prec=None):
    return lax.dot_general(a, b, (((1,), (0,)), ((), ())), precision=prec, preferred_element_type=F32)


def _dot_nt(a, b, prec=None):
    return lax.dot_general(a, b, (((1,), (1,)), ((), ())), precision=prec, preferred_element_type=F32)


def _dot_tn(a, b, prec=None):
    return lax.dot_general(a, b, (((0,), (0,)), ((), ())), precision=prec, preferred_element_type=F32)


def _bdot(a, b):
    return _dot(a.astype(BF16), b.astype(BF16))


def _bdot_nt(a, b):
    return _dot_nt(a.astype(BF16), b.astype(BF16))


def _bdot_tn(a, b):
    return _dot_tn(a.astype(BF16), b.astype(BF16))


def _sigmoid(x):
    return 1.0 / (1.0 + jnp.exp(-x))


def _silu(x):
    return x * _sigmoid(x)


def _silu_grad(x):
    s = _sigmoid(x)
    return s * (1.0 + x * (1.0 - s))


def _rms(x, w):
    r = lax.rsqrt(jnp.mean(x * x, axis=-1, keepdims=True) + NORM_EPS)
    xhat = x * r
    return xhat * w, xhat, r


def _rms_bwd(dy, w, xhat, r):
    dxhat = dy * w
    dx = r * (dxhat - xhat * jnp.mean(dxhat * xhat, axis=-1, keepdims=True))
    dw = jnp.sum(dy * xhat, axis=0, keepdims=True)
    return dx, dw


def _arb(n):
    return pltpu.CompilerParams(dimension_semantics=("arbitrary",) * n)


def _tile(n, want):
    t = min(n, want)
    assert n % t == 0, (n, want)
    return t


def _iota2(shape, dim):
    return lax.broadcasted_iota(jnp.int32, shape, dim)


_NN = (((1,), (0,)), ((), ()))
_NT = (((1,), (1,)), ((), ()))
_TN = (((0,), (0,)), ((), ()))


def _raw1(a, b, dn):
    return lax.dot_general(a.astype(BF16), b.astype(BF16), dn, preferred_element_type=F32)


def _raw3(a, b, dn):
    ah, bh = a.astype(BF16), b.astype(BF16)
    al, bl = (a - ah.astype(F32)).astype(BF16), (b - bh.astype(F32)).astype(BF16)
    f = lambda x, y: lax.dot_general(x, y, dn, preferred_element_type=F32)
    return f(ah, bh) + f(ah, bl) + f(al, bh)


def _make_mm(raw):
    @jax.custom_vjp
    def mm(a, b):
        return raw(a, b, _NN)

    mm.defvjp(lambda a, b: (raw(a, b, _NN), (a, b)),
              lambda res, ct: (raw(ct, res[1], _NT), raw(res[0], ct, _TN)))

    @jax.custom_vjp
    def mm_nt(a, b):
        return raw(a, b, _NT)

    mm_nt.defvjp(lambda a, b: (raw(a, b, _NT), (a, b)),
                 lambda res, ct: (raw(ct, res[1], _NN), raw(ct, res[0], _TN)))
    return mm, mm_nt


_mm1, _mm1_nt = _make_mm(_raw1)
_mm3, _mm3_nt = _make_mm(_raw3)


def _eye(n):
    return (_iota2((n, n), 0) == _iota2((n, n), 1)).astype(F32)


def _inv_newton(a):
    eye = _eye(a.shape[0])
    n = -a
    m = eye + n
    p = n
    k = 1
    while 2 * k < GDN_CHUNK:
        p = _raw1(p, p, _NN)
        m = m + _raw1(m, p, _NN)
        k *= 2
    r = eye - m - _raw3(a, m, _NN)
    return m + _raw1(m, r, _NN)


@jax.custom_vjp
def _unit_lower_inv(a):
    return _inv_newton(a)


def _unit_lower_inv_fwd(a):
    m = _inv_newton(a)
    return m, m


def _unit_lower_inv_bwd(m, dm):
    return (-_raw3(_raw3(m, dm, _TN), m, _NT),)


_unit_lower_inv.defvjp(_unit_lower_inv_fwd, _unit_lower_inv_bwd)


def _gdn_local(q, k, v, dcb, dcb128, dlb128, bb128):
    n = q.shape[0]
    r, s = _iota2((n, n), 0), _iota2((n, n), 1)
    same = (r // GDN_CHUNK) == (s // GDN_CHUNK)
    causal, strict = same & (r >= s), same & (r > s)
    decay_l = jnp.exp(jnp.where(causal, dcb - dcb.T, NEG))
    kb = k * bb128
    a = jnp.where(strict, _mm1_nt(kb, k) * decay_l, 0.0)
    m_off = _unit_lower_inv(a) - _eye(n)
    edc = jnp.exp(dcb128)
    rhs = jnp.concatenate([v * bb128, kb * edc], axis=1)
    sol = rhs + _mm3(m_off, rhs)
    aqk = jnp.where(causal, _mm1_nt(q, k) * decay_l, 0.0)
    return sol, aqk, q * edc, k * jnp.exp(dlb128 - dcb128)


def _gated_norm(o, z, nw):
    r = lax.rsqrt(jnp.mean(o * o, axis=-1, keepdims=True) + NORM_EPS)
    return o * r * nw * _silu(z)


def _gated_norm_bwd(dy, o, z, nw):
    r = lax.rsqrt(jnp.mean(o * o, axis=-1, keepdims=True) + NORM_EPS)
    xhat = o * r
    sz = _silu(z)
    dxhat = dy * nw * sz
    do = r * (dxhat - xhat * jnp.mean(dxhat * xhat, axis=-1, keepdims=True))
    dz = dy * xhat * nw * _silu_grad(z)
    dnw = jnp.sum(dy * xhat * sz, axis=0, keepdims=True)
    return do, dz, dnw


GDN_HEAD_GROUP = 2
GC_BETA, GC_G, GC_DECAY, GC_LAST =0, GDN_HEADS, 2 * GDN_HEADS, 3 * GDN_HEADS


def _gate_cols(gc, h, rows):
    lane = _iota2((rows, LANE), 1)
    col = lambda off: jnp.sum(jnp.where(lane == off + h, gc, 0.0), axis=-1, keepdims=True)
    return col(GC_BETA), col(GC_DECAY), col(GC_LAST)


def _gdn_local_args(q_ref, k_ref, v_ref, gc_ref, h, n):
    beta, dc, dl = _gate_cols(gc_ref[...], h, n)
    args = (q_ref[...], k_ref[...], v_ref[...], jnp.broadcast_to(dc, (n, n)), jnp.broadcast_to(dc, (n, LANE)),
            jnp.broadcast_to(dl, (n, LANE)), jnp.broadcast_to(beta, (n, LANE)))
    return args, dl


def gdn_chunk_fwd(qkv, gc, proj, norm_w):
    t = qkv.shape[0]
    c = GDN_CHUNK
    tc = _tile(t, 256)
    nsub = tc // c

    hp = GDN_HEAD_GROUP

    def body(q_ref, k_ref, v_ref, gc_ref, z_ref, nw_ref, on_ref, st_ref, s_ref):
        @pl.when(pl.program_id(1) == 0)
        def _():
            s_ref[...] = jnp.zeros_like(s_ref)

        for hh in range(hp):
            h = pl.program_id(0) * hp + hh
            ls = slice(hh * LANE, (hh + 1) * LANE)
            args, dl = _gdn_local_args(q_ref.at[:, ls], k_ref.at[:, ls], v_ref.at[:, ls], gc_ref, h, tc)
            sol, aqk, q_dec, k_dec = _gdn_local(*args)
            u, w = sol[:, :GDN_DK], sol[:, GDN_DK:]
            s = s_ref[hh]
            v_new, o_state = [], []
            for j in range(nsub):
                sl = slice(j * c, (j + 1) * c)
                st_ref[hh, j] = s
                v_new.append(u[sl] - _bdot(w[sl], s))
                o_state.append(_bdot(q_dec[sl], s))
                s = s * jnp.exp(dl[j * c:j * c + 1]) + _bdot_tn(k_dec[sl], v_new[j])
            s_ref[hh] = s
            o = jnp.concatenate(o_state, axis=0) + _bdot(aqk, jnp.concatenate(v_new, axis=0))
            on_ref[:, ls] = _gated_norm(o, z_ref[:, ls], nw_ref[...]).astype(BF16)

    col = lambda off: pl.BlockSpec((tc, hp * LANE), lambda g, i: (i, off // hp + g))
    return pl.pallas_call(
        body, name="gdn_chunk_fwd",
        grid=(GDN_HEADS // hp, t // tc),
        in_specs=[col(0), col(GDN_HEADS), col(2 * GDN_HEADS),
                  pl.BlockSpec((tc, LANE), lambda g, i: (i, 0)),
                  col(GDN_QKV_W // LANE),
                  pl.BlockSpec((1, LANE), lambda g, i: (0, 0))],
        out_specs=[col(0),
                   pl.BlockSpec((hp, nsub, GDN_DK, GDN_DK), lambda g, i: (g, i, 0, 0))],
        out_shape=[jax.ShapeDtypeStruct((t, GDN_HEADS * GDN_DK), BF16),
                   jax.ShapeDtypeStruct((GDN_HEADS, t // c, GDN_DK, GDN_DK), F32)],
        scratch_shapes=[pltpu.VMEM((hp, GDN_DK, GDN_DK), F32)],
        compiler_params=_arb(2),
    )(qkv, qkv, qkv, gc, proj, norm_w)


def gdn_chunk_bwd(qkv, gc, proj, norm_w, states, d_on):
    t = qkv.shape[0]
    c = GDN_CHUNK
    tc = _tile(t, 256)
    nsub = tc // c
    nblk = t // tc
    hp = GDN_HEAD_GROUP

    def body(q_ref, k_ref, v_ref, gc_ref, z_ref, nw_ref, st_ref, don_ref,
             dq_ref, dk_ref, dv_ref, dz_ref, dgc_ref, dnw_ref, ds_ref):
        @pl.when(pl.program_id(1) == 0)
        def _():
            ds_ref[...] = jnp.zeros_like(ds_ref)

        @pl.when((pl.program_id(1) == 0) & (pl.program_id(0) == 0))
        def _():
            dnw_ref[...] = jnp.zeros_like(dnw_ref)

        rows = [slice(j * c, (j + 1) * c) for j in range(nsub)]
        cat = lambda xs: jnp.concatenate(xs, axis=0)
        lsum = lambda x: jnp.sum(x, axis=-1, keepdims=True)
        lane = _iota2((tc, LANE), 1)
        row = _iota2((tc, 1), 0)
        dgc = jnp.zeros((tc, LANE), F32)
        for hh in range(hp):
            h = pl.program_id(0) * hp + hh
            ls = slice(hh * LANE, (hh + 1) * LANE)
            args, dl = _gdn_local_args(q_ref.at[:, ls], k_ref.at[:, ls], v_ref.at[:, ls], gc_ref, h, tc)
            (sol, aqk, q_dec, k_dec), vjp = jax.vjp(_gdn_local, *args)
            u, w = sol[:, :GDN_DK], sol[:, GDN_DK:]
            states = [st_ref[hh, j] for j in range(nsub)]
            v_new = [u[sl] - _bdot(w[sl], s) for sl, s in zip(rows, states)]
            v_all = cat(v_new)
            o = cat([_bdot(q_dec[sl], s) for sl, s in zip(rows, states)]) + _bdot(aqk, v_all)
            do, dz, dnw = _gated_norm_bwd(don_ref[:, ls], o, z_ref[:, ls], nw_ref[...])
            dnw_ref[...] += dnw
            dz_ref[:, ls] = dz.astype(BF16)
            d_aqk = _bdot_nt(do, v_all)
            dv_o = _bdot_tn(aqk, do)
            ds = ds_ref[hh]
            d_u, d_w, d_qdec, d_kdec, d_last = ([None] * nsub for _ in range(5))
            for j in reversed(range(nsub)):
                sl, s = rows[j], states[j]
                cd = jnp.exp(dl[j * c:j * c + 1])
                d_u[j] = dv_o[sl] + _bdot(k_dec[sl], ds)
                d_qdec[j] = _bdot_nt(do[sl], s)
                d_kdec[j] = _bdot_nt(v_new[j], ds)
                d_last[j] = jnp.sum(lsum(s * ds), axis=0, keepdims=True) * cd
                d_w[j] = -_bdot_nt(d_u[j], s)
                ds = _bdot_tn(q_dec[sl], do[sl]) + cd * ds - _bdot_tn(w[sl], d_u[j])
            ds_ref[hh] = ds
            d_sol = jnp.concatenate([cat(d_u), cat(d_w)], axis=1)
            dq, dk, dv, d_dcb, d_dcb128, d_dlb128, d_bb = vjp((d_sol, d_aqk, cat(d_qdec), cat(d_kdec)))
            dq_ref[:, ls] = dq
            dk_ref[:, ls] = dk
            dv_ref[:, ls] = dv
            d_dl = lsum(d_dlb128)
            for j in range(nsub):
                d_dl = d_dl + jnp.where(row == j * c, d_last[j], 0.0)
            dgc = dgc + jnp.where(lane == GC_BETA + h, lsum(d_bb),
                                  jnp.where(lane == GC_DECAY + h, lsum(d_dcb) + lsum(d_dcb128),
                                            jnp.where(lane == GC_LAST + h, d_dl, 0.0)))
        dgc_ref[0] = dgc

    rev = lambda i: nblk - 1 - i
    col = lambda off: pl.BlockSpec((tc, hp * LANE), lambda g, i: (rev(i), off // hp + g))
    return pl.pallas_call(
        body, name="gdn_chunk_bwd",
        grid=(GDN_HEADS // hp, nblk),
        in_specs=[col(0), col(GDN_HEADS), col(2 * GDN_HEADS),
                  pl.BlockSpec((tc, LANE), lambda g, i: (rev(i), 0)),
                  col(GDN_QKV_W // LANE),
                  pl.BlockSpec((1, LANE), lambda g, i: (0, 0)),
                  pl.BlockSpec((hp, nsub, GDN_DK, GDN_DK), lambda g, i: (g, rev(i), 0, 0)),
                  col(0)],
        out_specs=[col(0), col(0), col(0), col(0),
                   pl.BlockSpec((1, tc, LANE), lambda g, i: (g, rev(i), 0)),
                   pl.BlockSpec((1, LANE), lambda g, i: (0, 0))],
        out_shape=[
            jax.ShapeDtypeStruct((t, GDN_HEADS * GDN_DK), F32),
            jax.ShapeDtypeStruct((t, GDN_HEADS * GDN_DK), F32),
            jax.ShapeDtypeStruct((t, GDN_HEADS * GDN_DK), F32),
            jax.ShapeDtypeStruct((t, GDN_HEADS * GDN_DK), BF16),
            jax.ShapeDtypeStruct((GDN_HEADS // hp, t, LANE), F32),
            jax.ShapeDtypeStruct((1, LANE), F32)],
        scratch_shapes=[pltpu.VMEM((hp, GDN_DK, GDN_DK), F32)],
        compiler_params=_arb(2),
    )(qkv, qkv, qkv, gc, proj, norm_w, states, d_on)


GDN_PROJ_W = GDN_QKV_W + 1024 + LANE
GDN_Q_SCALE = GDN_DK ** -0.5


def _shift_rows(ext, shift, lo, n):
    if shift == 0:
        return ext[lo:lo + n]
    return pltpu.roll(ext, shift, 0)[lo:lo + n]


def _conv_fwd(x, halo, w):
    n = x.shape[0]
    ext = jnp.concatenate([halo, x], axis=0)
    y = w[GDN_CONV - 1:GDN_CONV] * x
    for j in range(GDN_CONV - 1):
        y = y + w[j:j + 1] * _shift_rows(ext, GDN_CONV - 1 - j, 8, n)
    return y


def _chunk_masks(n):
    r, s = _iota2((n, n), 0), _iota2((n, n), 1)
    same = (r // GDN_CHUNK) == (s // GDN_CHUNK)
    return (same & (r >= s)).astype(F32), (same & (r <= s)).astype(F32), same.astype(F32)


def _softplus(x):
    return jnp.maximum(x, 0.0) + jnp.log(1.0 + jnp.exp(-jnp.abs(x)))


def _l2n(t):
    rs = lax.rsqrt(jnp.sum(t * t, axis=-1, keepdims=True) + NORM_EPS)
    return t * rs, rs


def gdn_pre_fwd(proj, conv_w, gate_par):
    t = proj.shape[0]
    tm = _tile(t, 256)

    def body(x_ref, halo_ref, ba_ref, w_ref, gp_ref, qkv_ref, gc_ref):
        i = pl.program_id(0)
        halo = jnp.where(i > 0, halo_ref[...], 0.0)
        y = _silu(_conv_fwd(x_ref[...], halo, w_ref[...]))
        for hh in range(2 * GDN_HEADS):
            sl = slice(hh * LANE, (hh + 1) * LANE)
            tn, _ = _l2n(y[:, sl])
            qkv_ref[:, sl] = tn * GDN_Q_SCALE if hh < GDN_HEADS else tn
        qkv_ref[:, 2 * GDN_HEADS * LANE:] = y[:, 2 * GDN_HEADS * LANE:]
        ba = ba_ref[...]
        lane = _iota2(ba.shape, 1)
        gp = gp_ref[...]
        is_a = (lane >= GC_G) & (lane < GC_G + GDN_HEADS)
        g = jnp.where(is_a, -jnp.exp(gp[0:1]) * _softplus(ba + gp[1:2]), 0.0)
        tri, _, same = _chunk_masks(tm)
        decay = pltpu.roll(_dot(tri, g, HI), GC_DECAY - GC_G, 1)
        last = pltpu.roll(_dot(same, g, HI), GC_LAST - GC_G, 1)
        gc_ref[...] = jnp.where(lane < GDN_HEADS, _sigmoid(ba), g) + decay + last

    return pl.pallas_call(
        body, name="gdn_pre_fwd",
        grid=(t // tm,),
        in_specs=[pl.BlockSpec((tm, GDN_QKV_W), lambda i: (i, 0)),
                  pl.BlockSpec((8, GDN_QKV_W), lambda i: (jnp.maximum(i * (tm // 8) - 1, 0), 0)),
                  pl.BlockSpec((tm, LANE), lambda i: (i, (GDN_QKV_W + 1024) // LANE)),
                  pl.BlockSpec((8, GDN_QKV_W), lambda i: (0, 0)),
                  pl.BlockSpec((8, LANE), lambda i: (0, 0))],
        out_specs=[pl.BlockSpec((tm, GDN_QKV_W), lambda i: (i, 0)),
                   pl.BlockSpec((tm, LANE), lambda i: (i, 0))],
        out_shape=[jax.ShapeDtypeStruct((t, GDN_QKV_W), F32), jax.ShapeDtypeStruct((t, LANE), F32)],
        compiler_params=_arb(1),
    )(proj, proj, proj, conv_w, gate_par)


def gdn_pre_bwd(proj, conv_w, gate_par, dq, dk, dv, dgc_heads):
    t = proj.shape[0]
    tm = _tile(t, 256)

    def body(x_ref, halo_ref, ba_ref, w_ref, gp_ref, dq_ref, dk_ref, dv_ref, dgc_ref, dy_ref, dba_ref, dgp_ref):
        i = pl.program_id(0)

        @pl.when(i == 0)
        def _():
            dgp_ref[...] = jnp.zeros_like(dgp_ref)

        halo = jnp.where(i > 0, halo_ref[...], 0.0)
        y = _conv_fwd(x_ref[...], halo, w_ref[...])
        for hh in range(3 * GDN_HEADS):
            sl = slice(hh * LANE, (hh + 1) * LANE)
            hsl = slice((hh % GDN_HEADS) * LANE, (hh % GDN_HEADS + 1) * LANE)
            yy = y[:, sl]
            if hh < 2 * GDN_HEADS:
                tn, rs = _l2n(_silu(yy))
                dtn = dq_ref[:, hsl] * GDN_Q_SCALE if hh < GDN_HEADS else dk_ref[:, hsl]
                dsil = rs * (dtn - tn * jnp.sum(dtn * tn, axis=-1, keepdims=True))
            else:
                dsil = dv_ref[:, hsl]
            dy_ref[:, sl] = dsil * _silu_grad(yy)
        dgc = dgc_ref[0]
        for hh in range(1, dgc_heads.shape[0]):
            dgc = dgc + dgc_ref[hh]
        ba = ba_ref[...]
        lane = _iota2(ba.shape, 1)
        _, tri_t, same = _chunk_masks(tm)
        d_decay = jnp.where((lane >= GC_DECAY) & (lane < GC_DECAY + GDN_HEADS), dgc, 0.0)
        d_last = jnp.where((lane >= GC_LAST) & (lane < GC_LAST + GDN_HEADS), dgc, 0.0)
        dgc = (jnp.where(lane < GDN_HEADS, dgc, 0.0) + pltpu.roll(_dot(tri_t, d_decay, HI), LANE - (GC_DECAY - GC_G), 1)
               + pltpu.roll(_dot(same, d_last, HI), LANE - (GC_LAST - GC_G), 1))
        gp = gp_ref[...]
        xg = ba + gp[1:2]
        ea = jnp.exp(gp[0:1])
        sp = _softplus(xg)
        sb = _sigmoid(ba)
        is_b = lane < GDN_HEADS
        is_a = (lane >= GDN_HEADS) & (lane < 2 * GDN_HEADS)
        d_pre = jnp.where(is_a, dgc * (-ea) * _sigmoid(xg), 0.0)
        dba_ref[...] = jnp.where(is_b, dgc * sb * (1.0 - sb), d_pre).astype(BF16)
        d_alog = jnp.sum(jnp.where(is_a, dgc * (-ea) * sp, 0.0), axis=0, keepdims=True)
        d_dtb = jnp.sum(d_pre, axis=0, keepdims=True)
        row = _iota2((8, LANE), 0)
        dgp_ref[...] += jnp.where(row == 0, d_alog, jnp.where(row == 1, d_dtb, 0.0))

    hspec = pl.BlockSpec((tm, GDN_HEADS * LANE), lambda i: (i, 0))
    return pl.pallas_call(
        body, name="gdn_pre_bwd",
        grid=(t // tm,),
        in_specs=[pl.BlockSpec((tm, GDN_QKV_W), lambda i: (i, 0)),
                  pl.BlockSpec((8, GDN_QKV_W), lambda i: (jnp.maximum(i * (tm // 8) - 1, 0), 0)),
                  pl.BlockSpec((tm, LANE), lambda i: (i, (GDN_QKV_W + 1024) // LANE)),
                  pl.BlockSpec((8, GDN_QKV_W), lambda i: (0, 0)),
                  pl.BlockSpec((8, LANE), lambda i: (0, 0)),
                  hspec, hspec, hspec,
                  pl.BlockSpec((dgc_heads.shape[0], tm, LANE), lambda i: (0, i, 0))],
        out_specs=[pl.BlockSpec((tm, GDN_QKV_W), lambda i: (i, 0)),
                   pl.BlockSpec((tm, LANE), lambda i: (i, 0)),
                   pl.BlockSpec((8, LANE), lambda i: (0, 0))],
        out_shape=[jax.ShapeDtypeStruct((t, GDN_QKV_W), F32), jax.ShapeDtypeStruct((t, LANE), BF16),
                   jax.ShapeDtypeStruct((8, LANE), F32)],
        compiler_params=_arb(1),
    )(proj, proj, proj, conv_w, gate_par, dq, dk, dv, dgc_heads)


def gdn_conv_bwd(proj, conv_w, dy):
    t = proj.shape[0]
    tm = _tile(t, 256)
    nblk = t // tm

    def body(x_ref, halo_ref, w_ref, dy_ref, dyn_ref, dx_ref, dw_ref):
        i = pl.program_id(0)

        @pl.when(i == 0)
        def _():
            dw_ref[...] = jnp.zeros_like(dw_ref)

        w = w_ref[...]
        dy = dy_ref[...]
        ext_dy = jnp.concatenate([dy, jnp.where(i < nblk - 1, dyn_ref[...], 0.0)], axis=0)
        ext_x = jnp.concatenate([jnp.where(i > 0, halo_ref[...], 0.0), x_ref[...]], axis=0)
        dx = w[GDN_CONV - 1:GDN_CONV] * dy
        rows = [jnp.sum(dy * x_ref[...], axis=0, keepdims=True)]
        for j in range(GDN_CONV - 1):
            sh = GDN_CONV - 1 - j
            dx = dx + w[j:j + 1] * _shift_rows(ext_dy, tm + 8 - sh, 0, tm)
            rows.insert(j, jnp.sum(dy * _shift_rows(ext_x, sh, 8, tm), axis=0, keepdims=True))
        dx_ref[...] = dx.astype(BF16)
        row = _iota2((8, GDN_QKV_W), 0)
        acc = jnp.zeros((8, GDN_QKV_W), F32)
        for j in range(GDN_CONV):
            acc = acc + jnp.where(row == j, rows[j], 0.0)
        dw_ref[...] += acc

    return pl.pallas_call(
        body, name="gdn_conv_bwd",
        grid=(nblk,),
        in_specs=[pl.BlockSpec((tm, GDN_QKV_W), lambda i: (i, 0)),
                  pl.BlockSpec((8, GDN_QKV_W), lambda i: (jnp.maximum(i * (tm // 8) - 1, 0), 0)),
                  pl.BlockSpec((8, GDN_QKV_W), lambda i: (0, 0)),
                  pl.BlockSpec((tm, GDN_QKV_W), lambda i: (i, 0)),
                  pl.BlockSpec((8, GDN_QKV_W), lambda i: (jnp.minimum((i + 1) * (tm // 8), t // 8 - 1), 0))],
        out_specs=[pl.BlockSpec((tm, GDN_QKV_W), lambda i: (i, 0)),
                   pl.BlockSpec((8, GDN_QKV_W), lambda i: (0, 0))],
        out_shape=[jax.ShapeDtypeStruct((t, GDN_QKV_W), BF16), jax.ShapeDtypeStruct((8, GDN_QKV_W), F32)],
        compiler_params=_arb(1),
    )(proj, proj, conv_w, dy, dy)


def _resident(w_hbm, w_vmem, sem):
    @pl.when(pl.program_id(0) == 0)
    def _():
        cp = pltpu.make_async_copy(w_hbm, w_vmem, sem)
        cp.start()
        cp.wait()


ANY = pl.BlockSpec(memory_space=pl.ANY)


def norm_proj(h, nw, w, bias):
    t, d = h.shape
    n = w.shape[1]
    tm = _tile(t, 256)
    nc = _tile(n, 1536) if n % 1536 == 0 else _tile(n, 1408)

    def body(h_ref, nw_ref, w_hbm, b_ref, o_ref, w_ref, sem):
        _resident(w_hbm, w_ref, sem)
        hn = _rms(h_ref[...], nw_ref[...])[0].astype(BF16)
        for c0 in range(0, n, nc):
            o_ref[:, c0:c0 + nc] = _dot(hn, w_ref[:, c0:c0 + nc]) + b_ref[:, c0:c0 + nc]

    return pl.pallas_call(
        body, name="norm_proj",
        grid=(t // tm,),
        in_specs=[pl.BlockSpec((tm, d), lambda i: (i, 0)), pl.BlockSpec((1, d), lambda i: (0, 0)), ANY,
                  pl.BlockSpec((1, n), lambda i: (0, 0))],
        out_specs=pl.BlockSpec((tm, n), lambda i: (i, 0)),
        out_shape=jax.ShapeDtypeStruct((t, n), F32),
        scratch_shapes=[pltpu.VMEM((d, n), BF16), pltpu.SemaphoreType.DMA],
        compiler_params=_arb(1),
    )(h, nw, w, bias)


def linear_residual(h, x, w, bias):
    t, d = h.shape
    k = x.shape[1]
    tm = _tile(t, 512)

    def body(h_ref, x_ref, w_hbm, b_ref, o_ref, w_ref, sem):
        _resident(w_hbm, w_ref, sem)
        o_ref[...] = h_ref[...] + _dot(x_ref[...], w_ref[...]) + b_ref[...]

    return pl.pallas_call(
        body, name="linear_residual",
        grid=(t // tm,),
        in_specs=[pl.BlockSpec((tm, d), lambda i: (i, 0)), pl.BlockSpec((tm, k), lambda i: (i, 0)), ANY,
                  pl.BlockSpec((1, d), lambda i: (0, 0))],
        out_specs=pl.BlockSpec((tm, d), lambda i: (i, 0)),
        out_shape=jax.ShapeDtypeStruct((t, d), F32),
        scratch_shapes=[pltpu.VMEM((k, d), BF16), pltpu.SemaphoreType.DMA],
        compiler_params=_arb(1),
    )(h, x, w, bias)


def matmul_nt(dy, w):
    t, d = dy.shape
    k = w.shape[0]
    tm = _tile(t, 512)

    def body(dy_ref, w_hbm, o_ref, cs_ref, w_ref, sem):
        _resident(w_hbm, w_ref, sem)

        @pl.when(pl.program_id(0) == 0)
        def _():
            cs_ref[...] = jnp.zeros_like(cs_ref)

        dy = dy_ref[...]
        cs_ref[...] += jnp.sum(dy, axis=0, keepdims=True)
        o_ref[...] = _dot_nt(dy.astype(BF16), w_ref[...])

    return pl.pallas_call(
        body, name="matmul_nt",
        grid=(t // tm,),
        in_specs=[pl.BlockSpec((tm, d), lambda i: (i, 0)), ANY],
        out_specs=[pl.BlockSpec((tm, k), lambda i: (i, 0)), pl.BlockSpec((1, d), lambda i: (0, 0))],
        out_shape=[jax.ShapeDtypeStruct((t, k), F32), jax.ShapeDtypeStruct((1, d), F32)],
        scratch_shapes=[pltpu.VMEM((k, d), BF16), pltpu.SemaphoreType.DMA],
        compiler_params=_arb(1),
    )(dy, w)


def norm_proj_bwd(h, nw, dh, dps, ws):
    t, d = h.shape
    np_ = len(dps)
    ns = [w.shape[1] for w in ws]
    tm = _tile(t, 256)

    def body(*refs):
        h_ref, nw_ref, dh_ref = refs[:3]
        dp_refs = refs[3:3 + np_]
        w_hbms = refs[3 + np_:3 + 2 * np_]
        o_ref, hn_ref, dnw_ref = refs[3 + 2 * np_:6 + 2 * np_]
        cs_refs = refs[6 + 2 * np_:6 + 3 * np_]
        w_refs = refs[6 + 3 * np_:6 + 4 * np_]
        sem = refs[6 + 4 * np_]
        for a, b in zip(w_hbms, w_refs):
            _resident(a, b, sem)

        @pl.when(pl.program_id(0) == 0)
        def _():
            dnw_ref[...] = jnp.zeros_like(dnw_ref)
            for c in cs_refs:
                c[...] = jnp.zeros_like(c)

        nw = nw_ref[...]
        hn, xhat, r = _rms(h_ref[...], nw)
        hn_ref[...] = hn.astype(BF16)
        dhn = jnp.zeros((tm, d), F32)
        for dp_ref, w_ref, cs_ref in zip(dp_refs, w_refs, cs_refs):
            dp = dp_ref[...]
            cs_ref[...] += jnp.sum(dp.astype(F32), axis=0, keepdims=True)
            dhn = dhn + _dot_nt(dp, w_ref[...])
        dx, dnw = _rms_bwd(dhn, nw, xhat, r)
        dnw_ref[...] += dnw
        o_ref[...] = dh_ref[...] + dx

    row = pl.BlockSpec((tm, d), lambda i: (i, 0))
    vec = pl.BlockSpec((1, d), lambda i: (0, 0))
    return pl.pallas_call(
        body, name="norm_proj_bwd",
        grid=(t // tm,),
        in_specs=[row, vec, row] + [pl.BlockSpec((tm, n), lambda i: (i, 0)) for n in ns] + [ANY] * np_,
        out_specs=[row, row, vec] + [pl.BlockSpec((1, n), lambda i: (0, 0)) for n in ns],
        out_shape=[jax.ShapeDtypeStruct((t, d), F32), jax.ShapeDtypeStruct((t, d), BF16),
                   jax.ShapeDtypeStruct((1, d), F32)] + [jax.ShapeDtypeStruct((1, n), F32) for n in ns],
        scratch_shapes=[pltpu.VMEM((d, n), BF16) for n in ns] + [pltpu.SemaphoreType.DMA],
        compiler_params=_arb(1),
    )(h, nw, dh, *dps, *ws)


def matmul_tn(x, y, scale=1.0):
    t, k = x.shape
    n = y.shape[1]
    tk = _tile(k, 1024) if k % 1024 == 0 else _tile(k, 1408)
    tn = n if n <= 1536 else (1024 if n % 1024 == 0 else 1408)
    assert n % tn == 0
    tt = _tile(t, 1024)
    nt = t // tt

    def body(x_ref, y_ref, o_ref):
        @pl.when(pl.program_id(2) == 0)
        def _():
            o_ref[...] = jnp.zeros_like(o_ref)

        yv = y_ref[...]
        if scale != 1.0:
            yv = yv * scale
        o_ref[...] += _dot_tn(x_ref[...].astype(BF16), yv.astype(BF16))

    return pl.pallas_call(
        body, name="matmul_tn",
        grid=(k // tk, n // tn, nt),
        in_specs=[pl.BlockSpec((tt, tk), lambda i, j, s: (s, i)), pl.BlockSpec((tt, tn), lambda i, j, s: (s, j))],
        out_specs=pl.BlockSpec((tk, tn), lambda i, j, s: (i, j)),
        out_shape=jax.ShapeDtypeStruct((k, n), F32),
        compiler_params=_arb(3),
    )(x, y)


FFN_CHUNKS = 2


def ffn_fwd(h, nw, wgu, wd):
    t, d = h.shape
    f = wd.shape[0]
    fc = f // FFN_CHUNKS
    tm = _tile(t, 512)

    def body(h_ref, nw_ref, wgu_hbm, wd_hbm, o_ref, wgu_ref, wd_ref, sem):
        _resident(wgu_hbm, wgu_ref, sem)
        _resident(wd_hbm, wd_ref, sem)
        x = h_ref[...]
        hn = _rms(x, nw_ref[...])[0].astype(BF16)
        acc = jnp.zeros((tm, d), F32)
        for c in range(FFN_CHUNKS):
            g = _dot(hn, wgu_ref[:, c * fc:(c + 1) * fc])
            u = _dot(hn, wgu_ref[:, f + c * fc:f + (c + 1) * fc])
            acc = acc + _dot((_silu(g) * u).astype(BF16), wd_ref[c * fc:(c + 1) * fc, :])
        o_ref[...] = x + 0.5 * acc

    return pl.pallas_call(
        body, name="ffn_fwd",
        grid=(t // tm,),
        in_specs=[pl.BlockSpec((tm, d), lambda i: (i, 0)), pl.BlockSpec((1, d), lambda i: (0, 0)), ANY, ANY],
        out_specs=pl.BlockSpec((tm, d), lambda i: (i, 0)),
        out_shape=jax.ShapeDtypeStruct((t, d), F32),
        scratch_shapes=[pltpu.VMEM((d, 2 * f), BF16), pltpu.VMEM((f, d), BF16), pltpu.SemaphoreType.DMA],
        compiler_params=_arb(1),
    )(h, nw, wgu, wd)


def ffn_bwd(h, nw, wgu, wd, dh):
    t, d = h.shape
    f = wd.shape[0]
    fc = f // FFN_CHUNKS
    tm = _tile(t, 256)

    def body(h_ref, nw_ref, wgu_hbm, wd_hbm, dh_ref, o_ref, hn_ref, a_ref, dgu_ref, dnw_ref, wgu_ref, wd_ref, sem):
        _resident(wgu_hbm, wgu_ref, sem)
        _resident(wd_hbm, wd_ref, sem)

        @pl.when(pl.program_id(0) == 0)
        def _():
            dnw_ref[...] = jnp.zeros_like(dnw_ref)

        nw = nw_ref[...]
        hn32, xhat, r = _rms(h_ref[...], nw)
        hn = hn32.astype(BF16)
        hn_ref[...] = hn
        dh = dh_ref[...]
        dout = (0.5 * dh).astype(BF16)
        dhn = jnp.zeros((tm, d), F32)
        for c in range(FFN_CHUNKS):
            gs, us = slice(c * fc, (c + 1) * fc), slice(f + c * fc, f + (c + 1) * fc)
            g = _dot(hn, wgu_ref[:, gs])
            u = _dot(hn, wgu_ref[:, us])
            sg = _sigmoid(g)
            sil = g * sg
            a_ref[:, gs] = (sil * u).astype(BF16)
            da = _dot_nt(dout, wd_ref[gs, :])
            dg = (da * u * (sg * (1.0 + g * (1.0 - sg)))).astype(BF16)
            du = (da * sil).astype(BF16)
            dgu_ref[:, gs] = dg
            dgu_ref[:, us] = du
            dhn = dhn + _dot_nt(dg, wgu_ref[:, gs]) + _dot_nt(du, wgu_ref[:, us])
        dx, dnw = _rms_bwd(dhn, nw, xhat, r)
        dnw_ref[...] += dnw
        o_ref[...] = dh + dx

    row = pl.BlockSpec((tm, d), lambda i: (i, 0))
    vec = pl.BlockSpec((1, d), lambda i: (0, 0))
    return pl.pallas_call(
        body, name="ffn_bwd",
        grid=(t // tm,),
        in_specs=[row, vec, ANY, ANY, row],
        out_specs=[row, row, pl.BlockSpec((tm, f), lambda i: (i, 0)), pl.BlockSpec((tm, 2 * f), lambda i: (i, 0)), vec],
        out_shape=[jax.ShapeDtypeStruct((t, d), F32), jax.ShapeDtypeStruct((t, d), BF16),
                   jax.ShapeDtypeStruct((t, f), BF16), jax.ShapeDtypeStruct((t, 2 * f), BF16),
                   jax.ShapeDtypeStruct((1, d), F32)],
        scratch_shapes=[pltpu.VMEM((d, 2 * f), BF16), pltpu.VMEM((f, d), BF16), pltpu.SemaphoreType.DMA],
        compiler_params=_arb(1),
    )(h, nw, wgu, wd, dh)


def loss_head(h, nw, target):
    t, d = h.shape
    tm = _tile(t, 512)

    def body(h_ref, nw_ref, tg_ref, dh_ref, loss_ref, dnw_ref):
        @pl.when(pl.program_id(0) == 0)
        def _():
            loss_ref[...] = jnp.zeros_like(loss_ref)
            dnw_ref[...] = jnp.zeros_like(dnw_ref)

        nw = nw_ref[...]
        y, xhat, r = _rms(h_ref[...], nw)
        e = y - tg_ref[...]
        loss_ref[...] += 0.5 * jnp.sum(jnp.mean(e * e, axis=-1, keepdims=True), axis=0, keepdims=True)
        dx, dnw = _rms_bwd(e * (1.0 / d), nw, xhat, r)
        dnw_ref[...] += dnw
        dh_ref[...] = dx

    row = pl.BlockSpec((tm, d), lambda i: (i, 0))
    vec = pl.BlockSpec((1, d), lambda i: (0, 0))
    return pl.pallas_call(
        body, name="loss_head",
        grid=(t // tm,),
        in_specs=[row, vec, row],
        out_specs=[row, pl.BlockSpec((8, LANE), lambda i: (0, 0)), vec],
        out_shape=[jax.ShapeDtypeStruct((t, d), F32), jax.ShapeDtypeStruct((8, LANE), F32),
                   jax.ShapeDtypeStruct((1, d), F32)],
        compiler_params=_arb(1),
    )(h, nw, target)


def adamw(w, g, m, v):
    r, c = w.shape
    tr = r
    while tr * c * 4 > (1 << 20) and tr % 16 == 0:
        tr //= 2

    def body(w_ref, g_ref, m_ref, v_ref, d_ref, nm_ref, nv_ref):
        g = g_ref[...]
        m = ADAM_B1 * m_ref[...] + (1.0 - ADAM_B1) * g
        v = ADAM_B2 * v_ref[...] + (1.0 - ADAM_B2) * (g * g)
        m_hat = m / (1.0 - ADAM_B1 ** ADAM_STEP)
        v_hat = v / (1.0 - ADAM_B2 ** ADAM_STEP)
        d_ref[...] = -ADAM_LR * (m_hat / (jnp.sqrt(v_hat) + ADAM_EPS) + ADAM_WD * w_ref[...])
        nm_ref[...] = m
        nv_ref[...] = v

    blk = pl.BlockSpec((tr, c), lambda i: (i, 0))
    return pl.pallas_call(
        body, name="adamw",
        grid=(r // tr,),
        in_specs=[blk] * 4, out_specs=[blk] * 3,
        out_shape=[jax.ShapeDtypeStruct((r, c), F32)] * 3,
        compiler_params=_arb(1),
    )(w, g, m, v)


ATTN_P_W = ATTN_Q_W + 2 * ATTN_KV_W
ATTN_SCALE = ATTN_HEAD_DIM ** -0.5


def _rope_group(t, tab, sign):
    return (t * tab[:, 0:LANE] + sign * pltpu.roll(t, 8, 1) * tab[:, LANE:2 * LANE]
            + sign * pltpu.roll(t, LANE - 8, 1) * tab[:, 2 * LANE:3 * LANE])


def _rope(t, tab, sign=1.0):
    return jnp.concatenate([_rope_group(t[:, s:s + LANE], tab, sign) for s in range(0, t.shape[1], LANE)], axis=1)


def _attn_heads(qs, ks, vs, sinks, first):
    b = ATTN_BLOCK
    rows = ATTN_GROUP * b
    qi = _iota2((rows, 2 * b), 0) % b
    kj = _iota2((rows, 2 * b), 1)
    rel = qi + b - kj
    valid = (rel >= 0) & (rel < b) & ((kj >= b) | jnp.logical_not(first))
    outs = []
    for q, k, v, sink in zip(qs, ks, vs, sinks):
        s = jnp.where(valid, _mm1_nt(q, k) * ATTN_SCALE, NEG)
        m = lax.stop_gradient(jnp.maximum(jnp.max(s, axis=-1, keepdims=True), sink))
        p = jnp.exp(s - m)
        den = jnp.sum(p, axis=-1, keepdims=True) + jnp.exp(sink - m)
        outs.append(_mm1(p / den, v))
    return outs


def _attn_prepare(p_ref, kvp_ref, tab_ref, tabp_ref, sink_ref):
    b = ATTN_BLOCK
    hd = ATTN_HEAD_DIM
    tab, tabp = tab_ref[...], tabp_ref[...]
    q = _rope(p_ref[:, 0:ATTN_Q_W], tab)
    kc = _rope(p_ref[:, ATTN_Q_W:ATTN_Q_W + ATTN_KV_W], tab)
    kp = _rope(kvp_ref[:, 0:ATTN_KV_W], tabp)
    vc = p_ref[:, ATTN_Q_W + ATTN_KV_W:ATTN_P_W]
    vp = kvp_ref[:, ATTN_KV_W:2 * ATTN_KV_W]
    sk = sink_ref[...]
    qs, ks, vs, sinks = [], [], [], []
    for h in range(ATTN_KV_HEADS):
        heads = [ATTN_GROUP * h + g for g in range(ATTN_GROUP)]
        qs.append(jnp.concatenate([q[:, i * hd:(i + 1) * hd] for i in heads], axis=0))
        ks.append(jnp.concatenate([kp[:, h * hd:(h + 1) * hd], kc[:, h * hd:(h + 1) * hd]], axis=0))
        vs.append(jnp.concatenate([vp[:, h * hd:(h + 1) * hd], vc[:, h * hd:(h + 1) * hd]], axis=0))
        sinks.append(jnp.concatenate([jnp.broadcast_to(sk[:, i:i + 1], (b, 1)) for i in heads], axis=0))
    return qs, ks, vs, sinks


def _unstack_heads(xs):
    b = ATTN_BLOCK
    return jnp.concatenate([x[g * b:(g + 1) * b] for x in xs for g in range(ATTN_GROUP)], axis=1)


def _attn_specs(nb):
    b = ATTN_BLOCK
    prev = lambda n: jnp.maximum(n - 1, 0)
    return [pl.BlockSpec((b, ATTN_P_W), lambda n: (n, 0)),
            pl.BlockSpec((b, 2 * ATTN_KV_W), lambda n: (prev(n), ATTN_Q_W // (2 * ATTN_KV_W))),
            pl.BlockSpec((b, 3 * LANE), lambda n: (n, 0)),
            pl.BlockSpec((b, 3 * LANE), lambda n: (prev(n), 0)),
            pl.BlockSpec((1, ATTN_Q_HEADS), lambda n: (0, 0))]


def attn_fwd(p, tab, sinks):
    t = p.shape[0]
    nb = t // ATTN_BLOCK

    def body(p_ref, kvp_ref, tab_ref, tabp_ref, sink_ref, o_ref):
        qs, ks, vs, sk = _attn_prepare(p_ref, kvp_ref, tab_ref, tabp_ref, sink_ref)
        os_ = _attn_heads(qs, ks, vs, sk, pl.program_id(0) == 0)
        o_ref[...] = _unstack_heads(os_).astype(BF16)

    return pl.pallas_call(
        body, name="attn_fwd",
        grid=(nb,),
        in_specs=_attn_specs(nb),
        out_specs=pl.BlockSpec((ATTN_BLOCK, ATTN_Q_W), lambda n: (n, 0)),
        out_shape=jax.ShapeDtypeStruct((t, ATTN_Q_W), BF16),
        compiler_params=_arb(1),
    )(p, p, tab, tab, sinks)


def attn_bwd(p, tab, sinks, do):
    t = p.shape[0]
    b = ATTN_BLOCK
    hd = ATTN_HEAD_DIM
    nb = t // b

    def body(p_ref, kvp_ref, tab_ref, tabp_ref, sink_ref, do_ref, dq_ref, dkvc_ref, dkvp_ref, dsink_ref):
        n = pl.program_id(0)

        @pl.when(n == 0)
        def _():
            dsink_ref[...] = jnp.zeros_like(dsink_ref)

        qs, ks, vs, sk = _attn_prepare(p_ref, kvp_ref, tab_ref, tabp_ref, sink_ref)
        first = n == 0
        _, vjp = jax.vjp(lambda a, bb, c, d: _attn_heads(a, bb, c, d, first), qs, ks, vs, sk)
        do = do_ref[...]
        dos = [jnp.concatenate([do[:, i * hd:(i + 1) * hd] for i in range(ATTN_GROUP * h, ATTN_GROUP * (h + 1))], axis=0)
               for h in range(ATTN_KV_HEADS)]
        dqs, dks, dvs, dsk = vjp(dos)
        tab, tabp = tab_ref[...], tabp_ref[...]
        dq_ref[...] = _rope(_unstack_heads([x.astype(F32) for x in dqs]), tab, -1.0).astype(BF16)
        dkc = jnp.concatenate([x.astype(F32)[b:] for x in dks], axis=1)
        dkp = jnp.concatenate([x.astype(F32)[:b] for x in dks], axis=1)
        dkvc_ref[:, 0:ATTN_KV_W] = _rope(dkc, tab, -1.0)
        dkvp_ref[:, 0:ATTN_KV_W] = _rope(dkp, tabp, -1.0)
        dkvc_ref[:, ATTN_KV_W:] = jnp.concatenate([x.astype(F32)[b:] for x in dvs], axis=1)
        dkvp_ref[:, ATTN_KV_W:] = jnp.concatenate([x.astype(F32)[:b] for x in dvs], axis=1)
        parts = [jnp.sum(d[g * b:(g + 1) * b], axis=0, keepdims=True) for d in dsk for g in range(ATTN_GROUP)]
        dsink_ref[...] += jnp.concatenate(parts, axis=1)

    blk = lambda w: pl.BlockSpec((b, w), lambda n: (n, 0))
    return pl.pallas_call(
        body, name="attn_bwd",
        grid=(nb,),
        in_specs=_attn_specs(nb) + [blk(ATTN_Q_W)],
        out_specs=[blk(ATTN_Q_W), blk(2 * ATTN_KV_W), blk(2 * ATTN_KV_W),
                   pl.BlockSpec((1, ATTN_Q_HEADS), lambda n: (0, 0))],
        out_shape=[jax.ShapeDtypeStruct((t, ATTN_Q_W), BF16), jax.ShapeDtypeStruct((t, 2 * ATTN_KV_W), F32),
                   jax.ShapeDtypeStruct((t, 2 * ATTN_KV_W), F32), jax.ShapeDtypeStruct((1, ATTN_Q_HEADS), F32)],
        compiler_params=_arb(1),
    )(p, p, tab, tab, sinks, do)


def kv_combine(dkvc, dkvp):
    t, w = dkvc.shape
    b = ATTN_BLOCK
    nb = t // b

    def body(c_ref, p_ref, o_ref):
        nxt = jnp.where(pl.program_id(0) < nb - 1, p_ref[...], 0.0)
        o_ref[...] = (c_ref[...] + nxt).astype(BF16)

    return pl.pallas_call(
        body, name="kv_combine",
        grid=(nb,),
        in_specs=[pl.BlockSpec((b, w), lambda n: (n, 0)),
                  pl.BlockSpec((b, w), lambda n: (jnp.minimum(n + 1, nb - 1), 0))],
        out_specs=pl.BlockSpec((b, w), lambda n: (n, 0)),
        out_shape=jax.ShapeDtypeStruct((t, w), BF16),
        compiler_params=_arb(1),
    )(dkvc, dkvp)


def _me():
    return lax.axis_index("x"), lax.axis_index("y"), lax.axis_index("c")


def _flip(v, bit):
    return 1 - v if bit else v


def _chip_index():
    return 2 * lax.axis_index("x") + lax.axis_index("y")


def allgather_chips(x, split):
    r, c = x.shape
    rh = r // 2 if split else r

    def body(x_ref, o_ref, send_sems, recv_sems):
        mx, my, mc = _me()
        me = 2 * mx + my
        rows = pl.ds(pl.multiple_of(mc * rh, 16), rh) if split else pl.ds(0, r)
        fetched, passed = [], []
        for k in (1, 2, 3):
            peer = (_flip(mx, k >> 1), _flip(my, k & 1), mc)
            cp = pltpu.make_async_remote_copy(src_ref=x_ref.at[rows], dst_ref=o_ref.at[me, rows],
                                              send_sem=send_sems.at[k - 1], recv_sem=recv_sems.at[k - 1],
                                              device_id=peer, device_id_type=MESH)
            cp.start()
            fetched.append(cp)
        if split:
            for k in (1, 2, 3):
                fetched[k - 1].wait_recv()
                theirs = o_ref.at[2 * _flip(mx, k >> 1) + _flip(my, k & 1), rows]
                cp = pltpu.make_async_remote_copy(src_ref=theirs, dst_ref=theirs, send_sem=send_sems.at[2 + k],
                                                  recv_sem=recv_sems.at[2 + k], device_id=(mx, my, 1 - mc),
                                                  device_id_type=MESH)
                cp.start()
                passed.append(cp)
            for cp in fetched:
                cp.wait_send()
            for cp in passed:
                cp.wait()
        else:
            for cp in fetched:
                cp.wait()

    got = pl.pallas_call(
        body, name="allgather_chips",
        in_specs=[ANY], out_specs=ANY,
        out_shape=jax.ShapeDtypeStruct((4, r, c), x.dtype),
        scratch_shapes=[pltpu.SemaphoreType.DMA((6,)), pltpu.SemaphoreType.DMA((6,))],
    )(x)
    return lax.dynamic_update_slice(got, x[None], (_chip_index(), 0, 0))


def pair_exchange(g):
    _, _, r, c = g.shape

    def body(g_ref, got_ref, send_sems, recv_sems):
        mx, my, mc = _me()
        sib = (mx, my, 1 - mc)
        copies = []
        for j in range(4):
            cp = pltpu.make_async_remote_copy(src_ref=g_ref.at[j, 1 - mc], dst_ref=got_ref.at[j], send_sem=send_sems.at[j],
                                              recv_sem=recv_sems.at[j], device_id=sib, device_id_type=MESH)
            cp.start()
            copies.append(cp)
        for cp in copies:
            cp.wait()

    return pl.pallas_call(
        body, name="pair_exchange",
        in_specs=[ANY], out_specs=ANY,
        out_shape=jax.ShapeDtypeStruct((4, r, c), g.dtype),
        scratch_shapes=[pltpu.SemaphoreType.DMA((4,))] * 2,
    )(g)


def scatter_chips(p):
    _, r, c = p.shape

    def body(p_ref, o_ref, send_sems, recv_sems):
        mx, my, mc = _me()
        me = 2 * mx + my
        copies = []
        for k in (1, 2, 3):
            px, py = _flip(mx, k >> 1), _flip(my, k & 1)
            cp = pltpu.make_async_remote_copy(src_ref=p_ref.at[2 * px + py], dst_ref=o_ref.at[me],
                                              send_sem=send_sems.at[k - 1], recv_sem=recv_sems.at[k - 1],
                                              device_id=(px, py, mc), device_id_type=MESH)
            cp.start()
            copies.append(cp)
        for cp in copies:
            cp.wait()

    return pl.pallas_call(
        body, name="scatter_chips",
        in_specs=[ANY], out_specs=ANY,
        out_shape=jax.ShapeDtypeStruct((4, r, c), p.dtype),
        scratch_shapes=[pltpu.SemaphoreType.DMA((3,)), pltpu.SemaphoreType.DMA((3,))],
    )(p)


def pair_share(o):
    r, c = o.shape

    def body(o_ref, out_ref, send_sem, recv_sem):
        mx, my, mc = _me()
        cp = pltpu.make_async_remote_copy(src_ref=o_ref, dst_ref=out_ref, send_sem=send_sem, recv_sem=recv_sem,
                                          device_id=(mx, my, 1 - mc), device_id_type=MESH)
        cp.start()
        cp.wait()

    return pl.pallas_call(
        body, name="pair_share",
        in_specs=[ANY], out_specs=ANY,
        out_shape=jax.ShapeDtypeStruct((r, c), o.dtype),
        scratch_shapes=[pltpu.SemaphoreType.DMA, pltpu.SemaphoreType.DMA],
    )(o)


def _row_tile(r, c):
    tr = 8
    while r % (2 * tr) == 0 and 2 * tr * c * 4 <= (2 << 20):
        tr *= 2
    return tr


def add_pair(g, got, core):
    _, _, r, c = g.shape
    tr = _row_tile(r, c)

    def body(core_ref, g_ref, got_ref, p32_ref, p16_ref):
        s = g_ref[...] + got_ref[...]
        p32_ref[...] = s
        p16_ref[...] = s.astype(BF16)

    blk = pl.BlockSpec((None, tr, c), lambda j, i, core_ref: (j, i, 0))
    return pl.pallas_call(
        body, name="add_pair",
        grid_spec=pltpu.PrefetchScalarGridSpec(
            num_scalar_prefetch=1, grid=(4, r // tr),
            in_specs=[pl.BlockSpec((None, None, tr, c), lambda j, i, core_ref: (j, core_ref[0], i, 0)), blk],
            out_specs=[blk, blk]),
        out_shape=[jax.ShapeDtypeStruct((4, r, c), F32), jax.ShapeDtypeStruct((4, r, c), BF16)],
        compiler_params=_arb(2),
    )(core, g, got)


def sum_slots(p32, q16, order):
    _, r, c = p32.shape
    tr = _row_tile(r, c)

    def body(order_ref, own_ref, a_ref, b_ref, c_ref, o_ref):
        o_ref[...] = ((own_ref[...] + a_ref[...].astype(F32)) + b_ref[...].astype(F32)) + c_ref[...].astype(F32)

    slot = lambda k: pl.BlockSpec((None, tr, c), functools.partial(lambda k, i, order_ref: (order_ref[k], i, 0), k))
    return pl.pallas_call(
        body, name="sum_slots",
        grid_spec=pltpu.PrefetchScalarGridSpec(
            num_scalar_prefetch=1, grid=(r // tr,),
            in_specs=[slot(0), slot(1), slot(2), slot(3)],
            out_specs=pl.BlockSpec((tr, c), lambda i, order_ref: (i, 0))),
        out_shape=jax.ShapeDtypeStruct((r, c), F32),
        compiler_params=_arb(1),
    )(order, p32, q16, q16, q16)


def allreduce_small(x):
    r, c = x.shape

    def body(x_ref, o_ref, buf, send_sems, recv_sems):
        mx, my, mc = _me()
        me = 4 * mx + 2 * my + mc
        buf[pl.ds(me, 1)] = x_ref[...][None]
        copies = []
        for k in range(1, 8):
            peer = (_flip(mx, k >> 2), _flip(my, (k >> 1) & 1), _flip(mc, k & 1))
            cp = pltpu.make_async_remote_copy(src_ref=x_ref, dst_ref=buf.at[me], send_sem=send_sems.at[k - 1],
                                              recv_sem=recv_sems.at[k - 1], device_id=peer, device_id_type=MESH)
            cp.start()
            copies.append(cp)
        for cp in copies:
            cp.wait()
        acc = buf[0]
        for d in range(1, 8):
            acc = acc + buf[d]
        o_ref[...] = acc

    return pl.pallas_call(
        body, name="allreduce_small",
        out_shape=jax.ShapeDtypeStruct((r, c), F32),
        scratch_shapes=[pltpu.VMEM((8, r, c), F32), pltpu.SemaphoreType.DMA((7,)), pltpu.SemaphoreType.DMA((7,))],
    )(x)


def reduce_scatter(g):
    _, r, c = g.shape
    rh = r // 2
    core = lax.axis_index("c")
    me = _chip_index()
    g = g.reshape(4, 2, rh, c)
    p32, p16 = add_pair(g, pair_exchange(g), core.astype(jnp.int32)[None])
    q16 = scatter_chips(p16)
    order = jnp.stack([me, me ^ 1, me ^ 2, me ^ 3]).astype(jnp.int32)
    mine = sum_slots(p32, q16, order)
    theirs = pair_share(mine)
    return jnp.where(core == 0, jnp.concatenate([mine, theirs]), jnp.concatenate([theirs, mine]))


PACK_W = 1024
PACK_ROWS = 1024

SHARDED = {"ffn1_w_gate_up": 1, "ffn1_w_down": 0, "ffn2_w_gate_up": 1, "ffn2_w_down": 0, "attn_w_in": 1,
           "attn_w_out": 0, "gdn_w_in": 1, "gdn_w_out": 0, "gdn_conv_w": 1}
REPLICATED = ["ffn1_norm", "mix_norm", "ffn2_norm", "attn_b_in", "attn_sinks", "attn_b_out", "gdn_A_log",
              "gdn_dt_bias", "gdn_norm_w", "final_norm"]
WEIGHTS = ["ffn1_norm", "ffn1_w_gate_up", "ffn1_w_down", "mix_norm", "ffn2_norm", "ffn2_w_gate_up", "ffn2_w_down",
           "attn_w_in", "attn_b_in", "attn_sinks", "attn_w_out", "attn_b_out", "gdn_w_in", "gdn_conv_w", "gdn_A_log",
           "gdn_dt_bias", "gdn_norm_w", "gdn_w_out", "final_norm"]


def _pack_rows(flats, dtype, width, row_multiple):
    flat = jnp.concatenate([f.astype(dtype).reshape(-1) for f in flats])
    per = width * row_multiple
    pad = (-flat.shape[0]) % per
    if pad:
        flat = jnp.concatenate([flat, jnp.zeros((pad,), dtype)])
    return flat.reshape(-1, width)


def _by_width(shapes):
    groups = {}
    for n, shape in shapes.items():
        groups.setdefault(shape[-1], []).append(n)
    return groups


def _gather_weights(shards, dtype, split=True):
    full = {}
    for width, names in _by_width({n: s.shape for n, s in shards.items()}).items():
        packed = jnp.concatenate([shards[n].astype(dtype).reshape(-1, width) for n in names], axis=0)
        got = allgather_chips(packed, split)
        off = 0
        for n in names:
            layers, a, _ = shards[n].shape
            full[n] = [jnp.concatenate([got[j, off + l * a:off + (l + 1) * a] for j in range(4)], axis=SHARDED[n])
                       for l in range(layers)]
            off += layers * a
    return full


def _scatter_grads(grads, shard_shapes):
    out = {}
    for _, names in _by_width(shard_shapes).items():
        slots = []
        for j in range(4):
            parts = []
            for n in names:
                w = shard_shapes[n][1 + SHARDED[n]]
                parts += [lax.slice_in_dim(gl, j * w, (j + 1) * w, axis=SHARDED[n]) for gl in grads[n]]
            slots.append(jnp.concatenate(parts, axis=0))
        red = reduce_scatter(jnp.stack(slots))
        off = 0
        for n in names:
            layers, a, _ = shard_shapes[n]
            out[n] = red[off:off + layers * a].reshape(shard_shapes[n])
            off += layers * a
    return out


def _small_pack(items):
    rows = []
    for a in items:
        f = a.astype(F32).reshape(-1)
        pad = (-f.shape[0]) % LANE
        rows.append(jnp.concatenate([f, jnp.zeros((pad,), F32)]) if pad else f)
    return _pack_rows(rows, F32, LANE, 8)


def _small_unpack(buf, shapes):
    flat = buf.reshape(-1)
    out, off = [], 0
    for shape in shapes:
        size = 1
        for s in shape:
            size *= s
        out.append(flat[off:off + size].reshape(shape))
        off += size + (-size) % LANE
    return out


def _rope_table(positions):
    t = positions.shape[0]
    inv_freq = ROPE_THETA ** (-jnp.arange(0, ROPE_DIM, 2, dtype=F32) / ROPE_DIM)
    ang = positions.astype(F32)[:, None] * inv_freq
    cos, sin = jnp.cos(ang), jnp.sin(ang)
    rest = ATTN_HEAD_DIM - ROPE_DIM
    zeros = lambda n: jnp.zeros((t, n), F32)
    c64 = jnp.concatenate([cos, cos, jnp.ones((t, rest), F32)], axis=1)
    s_up = jnp.concatenate([zeros(ROPE_DIM // 2), sin, zeros(rest)], axis=1)
    s_dn = jnp.concatenate([-sin, zeros(ROPE_DIM // 2 + rest)], axis=1)
    return jnp.concatenate([c64, c64, s_up, s_up, s_dn, s_dn], axis=1)


def _as2d(a):
    return a.reshape(-1, a.shape[-1]) if a.ndim > 1 else a.reshape(1, -1)


def kernel(x, positions, ffn1_norm, ffn1_w_gate_up, ffn1_w_down, mix_norm, ffn2_norm, ffn2_w_gate_up, ffn2_w_down, attn_w_in, attn_b_in, attn_sinks, attn_w_out, attn_b_out, gdn_w_in, gdn_conv_w, gdn_A_log, gdn_dt_bias, gdn_norm_w, gdn_w_out, final_norm, loss_target, m_ffn1_norm, m_ffn1_w_gate_up, m_ffn1_w_down, m_mix_norm, m_ffn2_norm, m_ffn2_w_gate_up, m_ffn2_w_down, m_attn_w_in, m_attn_b_in, m_attn_sinks, m_attn_w_out, m_attn_b_out, m_gdn_w_in, m_gdn_conv_w, m_gdn_A_log, m_gdn_dt_bias, m_gdn_norm_w, m_gdn_w_out, m_final_norm, v_ffn1_norm, v_ffn1_w_gate_up, v_ffn1_w_down, v_mix_norm, v_ffn2_norm, v_ffn2_w_gate_up, v_ffn2_w_down, v_attn_w_in, v_attn_b_in, v_attn_sinks, v_attn_w_out, v_attn_b_out, v_gdn_w_in, v_gdn_conv_w, v_gdn_A_log, v_gdn_dt_bias, v_gdn_norm_w, v_gdn_w_out, v_final_norm):
    given = dict(locals())
    w = {n: given[n] for n in WEIGHTS}
    d = D_MODEL
    h = x[0]
    target = loss_target[0]
    depth = ffn1_norm.shape[0]

    big = [n for n in SHARDED if n != "gdn_conv_w"]
    full = _gather_weights({n: w[n] for n in big}, BF16)
    conv_full = _gather_weights({"gdn_conv_w": gdn_conv_w}, F32, split=False)["gdn_conv_w"]
    tab = _rope_table(positions[0])
    zero_d = jnp.zeros((1, d), F32)

    def gdn_params(j):
        w_in = full["gdn_w_in"][j]
        w_cat = jnp.concatenate([w_in, jnp.zeros((d, GDN_PROJ_W - w_in.shape[1]), BF16)], axis=1)
        conv = jnp.concatenate([conv_full[j], jnp.zeros((8 - GDN_CONV, GDN_QKV_W), F32)], axis=0)
        lanes = lambda vec: jnp.concatenate([jnp.zeros((GC_G,), F32), vec, jnp.zeros((LANE - GC_G - GDN_HEADS,), F32)])
        par = jnp.concatenate([lanes(gdn_A_log[j])[None], lanes(gdn_dt_bias[j])[None], jnp.zeros((6, LANE), F32)], axis=0)
        return w_cat, conv, par

    saved = []
    for l in range(depth):
        j = l // 2
        rec = {"h1": h}
        h = ffn_fwd(h, ffn1_norm[l][None], full["ffn1_w_gate_up"][l], full["ffn1_w_down"][l])
        rec["h2"] = h
        if l % 2 == 0:
            p = norm_proj(h, mix_norm[l][None], full["attn_w_in"][j], attn_b_in[j][None])
            o = attn_fwd(p, tab, attn_sinks[j][None])
            h = linear_residual(h, o, full["attn_w_out"][j], attn_b_out[j][None])
            rec.update(p=p, o=o)
        else:
            w_cat, conv, par = gdn_params(j)
            proj = norm_proj(h, mix_norm[l][None], w_cat, jnp.zeros((1, GDN_PROJ_W), F32))
            qkv, gc = gdn_pre_fwd(proj, conv, par)
            on, st = gdn_chunk_fwd(qkv, gc, proj, gdn_norm_w[j][None])
            h = linear_residual(h, on, full["gdn_w_out"][j], zero_d)
            rec.update(proj=proj, qkv=qkv, gc=gc, on=on, st=st, w_cat=w_cat, conv=conv, par=par)
        rec["h3"] = h
        h = ffn_fwd(h, ffn2_norm[l][None], full["ffn2_w_gate_up"][l], full["ffn2_w_down"][l])
        saved.append(rec)

    dh, loss_tile, d_final = loss_head(h, final_norm[None], target)

    g = {n: [None] * w[n].shape[0] for n in WEIGHTS if n != "final_norm"}
    for l in reversed(range(depth)):
        j = l // 2
        rec = saved[l]

        def ffn_back(tag, h_in, dh):
            wgu, wd = full[tag + "_w_gate_up"][l], full[tag + "_w_down"][l]
            dh_new, hn, a, dgu, dn = ffn_bwd(h_in, w[tag + "_norm"][l][None], wgu, wd, dh)
            g[tag + "_w_gate_up"][l] = matmul_tn(hn, dgu)
            g[tag + "_w_down"][l] = matmul_tn(a, dh, 0.5)
            g[tag + "_norm"][l] = dn[0]
            return dh_new

        dh = ffn_back("ffn2", rec["h3"], dh)
        if l % 2 == 0:
            w_in, w_out = full["attn_w_in"][j], full["attn_w_out"][j]
            do, db_out = matmul_nt(dh, w_out)
            g["attn_w_out"][j] = matmul_tn(rec["o"], dh)
            g["attn_b_out"][j] = db_out[0]
            dq, dkvc, dkvp, dsink = attn_bwd(rec["p"], tab, attn_sinks[j][None], do)
            dkv = kv_combine(dkvc, dkvp)
            dh, hn, dn, cs_q, cs_kv = norm_proj_bwd(rec["h2"], mix_norm[l][None], dh, [dq, dkv],
                                                    [w_in[:, :ATTN_Q_W], w_in[:, ATTN_Q_W:]])
            g["attn_w_in"][j] = jnp.concatenate([matmul_tn(hn, dq), matmul_tn(hn, dkv)], axis=1)
            g["attn_b_in"][j] = jnp.concatenate([cs_q[0], cs_kv[0]])
            g["attn_sinks"][j] = dsink[0]
        else:
            w_cat, conv, par = rec["w_cat"], rec["conv"], rec["par"]
            d_on, _ = matmul_nt(dh, full["gdn_w_out"][j])
            g["gdn_w_out"][j] = matmul_tn(rec["on"], dh)
            dq, dk, dv, dz, dgc_heads, dnw = gdn_chunk_bwd(rec["qkv"], rec["gc"], rec["proj"], gdn_norm_w[j][None],
                                                           rec["st"], d_on)
            dy, dba, dpar = gdn_pre_bwd(rec["proj"], conv, par, dq, dk, dv, dgc_heads)
            dx, dconv = gdn_conv_bwd(rec["proj"], conv, dy)
            nz = GDN_QKV_W + GDN_HEADS * GDN_DK
            dh, hn, dn, _, _, _ = norm_proj_bwd(rec["h2"], mix_norm[l][None], dh, [dx, dz, dba],
                                                [w_cat[:, :GDN_QKV_W], w_cat[:, GDN_QKV_W:nz], w_cat[:, nz:]])
            g["gdn_w_in"][j] = jnp.concatenate(
                [matmul_tn(hn, dx), matmul_tn(hn, dz), matmul_tn(hn, dba)[:, :2 * GDN_HEADS]], axis=1)
            g["gdn_conv_w"][j] = dconv[:GDN_CONV]
            g["gdn_A_log"][j] = dpar[0, GDN_HEADS:2 * GDN_HEADS]
            g["gdn_dt_bias"][j] = dpar[1, GDN_HEADS:2 * GDN_HEADS]
            g["gdn_norm_w"][j] = dnw[0]
        g["mix_norm"][l] = dn[0]
        dh = ffn_back("ffn1", rec["h1"], dh)
    grad_x = dh[None]

    small_names = REPLICATED + ["gdn_conv_w"]
    local = {n: jnp.stack(g[n]) for n in small_names if n != "final_norm"}
    local["final_norm"] = d_final[0]
    small_shapes = [(1,)] + [local[n].shape for n in small_names]
    small = allreduce_small(_small_pack([loss_tile[0, 0:1]] + [local[n] for n in small_names]))
    small = _small_unpack(small, small_shapes)
    loss = small[0][0]
    grads = dict(zip(small_names, small[1:]))
    conv_cols = gdn_conv_w.shape[2]
    grads["gdn_conv_w"] = lax.dynamic_slice_in_dim(grads["gdn_conv_w"], _chip_index() * conv_cols, conv_cols, axis=2)
    grads.update(_scatter_grads({n: g[n] for n in big}, {n: w[n].shape for n in big}))

    delta, new_m, new_v = {}, {}, {}
    for n in SHARDED:
        dl, nm, nv = adamw(_as2d(w[n]), _as2d(grads[n]), _as2d(given["m_" + n]), _as2d(given["v_" + n]))
        delta[n], new_m[n], new_v[n] = dl.reshape(w[n].shape), nm.reshape(w[n].shape), nv.reshape(w[n].shape)
    shapes = [w[n].shape for n in REPLICATED]
    packed = [_small_pack([src[n] for n in REPLICATED]) for src in
              (w, grads, {n: given["m_" + n] for n in REPLICATED}, {n: given["v_" + n] for n in REPLICATED})]
    for dst, buf in zip((delta, new_m, new_v), adamw(*packed)):
        dst.update(zip(REPLICATED, _small_unpack(buf, shapes)))

    return (loss, grad_x, *[grads[n] for n in WEIGHTS], *[delta[n] for n in WEIGHTS],
            *[new_m[n] for n in WEIGHTS], *[new_v[n] for n in WEIGHTS])
```

```python
import functools

import jax
import jax.numpy as jnp
from jax import lax
from jax.experimental import pallas as pl
from jax.experimental.pallas import tpu as pltpu

F32 = jnp.float32
BF16 = jnp.bfloat16
HI = lax.Precision.HIGHEST
MESH = pl.DeviceIdType.MESH

D_MODEL = 1024
D_FF = 2816
DEPTH = 4
NORM_EPS = 1e-6
LANE = 128

ATTN_Q_HEADS = 16
ATTN_KV_HEADS = 4
ATTN_HEAD_DIM = 64
ATTN_GROUP = 4
ATTN_BLOCK = 128
ROPE_DIM = 16
ROPE_THETA = 500000.0
ATTN_Q_W = 1024
ATTN_KV_W = 256

GDN_HEADS = 8
GDN_DK = 128
GDN_CONV = 4
GDN_CHUNK = 64
GDN_QKV_W = 3072

ADAM_LR = 0.001
ADAM_B1 = 0.9
ADAM_B2 = 0.999
ADAM_EPS = 1e-08
ADAM_WD = 0.01
ADAM_STEP = 10

NEG = -1e30


def _dot(a, b, prec=None):
    return lax.dot_general(a, b, (((1,), (0,)), ((), ())), precision=prec, preferred_element_type=F32)


def _dot_nt(a, b, prec=None):
    return lax.dot_general(a, b, (((1,), (1,)), ((), ())), precision=prec, preferred_element_type=F32)


def _dot_tn(a, b, prec=None):
    return lax.dot_general(a, b, (((0,), (0,)), ((), ())), precision=prec, preferred_element_type=F32)


def _bdot(a, b):
    return _dot(a.astype(BF16), b.astype(BF16))


def _bdot_nt(a, b):
    return _dot_nt(a.astype(BF16), b.astype(BF16))


def _bdot_tn(a, b):
    return _dot_tn(a.astype(BF16), b.astype(BF16))


def _sigmoid(x):
    return 1.0 / (1.0 + jnp.exp(-x))


def _silu(x):
    return x * _sigmoid(x)


def _silu_grad(x):
    s = _sigmoid(x)
    return s * (1.0 + x * (1.0 - s))


def _rms(x, w):
    r = lax.rsqrt(jnp.mean(x * x, axis=-1, keepdims=True) + NORM_EPS)
    xhat = x * r
    return xhat * w, xhat, r


def _rms_bwd(dy, w, xhat, r):
    dxhat = dy * w
    dx = r * (dxhat - xhat * jnp.mean(dxhat * xhat, axis=-1, keepdims=True))
    dw = jnp.sum(dy * xhat, axis=0, keepdims=True)
    return dx, dw


def _arb(n):
    return pltpu.CompilerParams(dimension_semantics=("arbitrary",) * n)


def _tile(n, want):
    t = min(n, want)
    assert n % t == 0, (n, want)
    return t


def _iota2(shape, dim):
    return lax.broadcasted_iota(jnp.int32, shape, dim)


_NN = (((1,), (0,)), ((), ()))
_NT = (((1,), (1,)), ((), ()))
_TN = (((0,), (0,)), ((), ()))


def _raw1(a, b, dn):
    return lax.dot_general(a.astype(BF16), b.astype(BF16), dn, preferred_element_type=F32)


def _raw3(a, b, dn):
    ah, bh = a.astype(BF16), b.astype(BF16)
    al, bl = (a - ah.astype(F32)).astype(BF16), (b - bh.astype(F32)).astype(BF16)
    f = lambda x, y: lax.dot_general(x, y, dn, preferred_element_type=F32)
    return f(ah, bh) + f(ah, bl) + f(al, bh)


def _make_mm(raw):
    @jax.custom_vjp
    def mm(a, b):
        return raw(a, b, _NN)

    mm.defvjp(lambda a, b: (raw(a, b, _NN), (a, b)),
              lambda res, ct: (raw(ct, res[1], _NT), raw(res[0], ct, _TN)))

    @jax.custom_vjp
    def mm_nt(a, b):
        return raw(a, b, _NT)

    mm_nt.defvjp(lambda a, b: (raw(a, b, _NT), (a, b)),
                 lambda res, ct: (raw(ct, res[1], _NN), raw(ct, res[0], _TN)))
    return mm, mm_nt


_mm1, _mm1_nt = _make_mm(_raw1)
_mm3, _mm3_nt = _make_mm(_raw3)


def _eye(n):
    return (_iota2((n, n), 0) == _iota2((n, n), 1)).astype(F32)


def _each(f, *lists):
    return [f(*t) for t in zip(*lists)]


def _inv_newton(mats):
    eye = _eye(mats[0].shape[0])
    ps = [-a for a in mats]
    ms = [eye + p for p in ps]
    k = 1
    while 2 * k < GDN_CHUNK:
        ps = [_raw1(p, p, _NN) for p in ps]
        ms = _each(lambda m, p: m + _raw1(m, p, _NN), ms, ps)
        k *= 2
    rs = _each(lambda a, m: eye - m - _raw3(a, m, _NN), mats, ms)
    return _each(lambda m, r: m + _raw1(m, r, _NN), ms, rs)


@jax.custom_vjp
def _unit_lower_inv(mats):
    return _inv_newton(mats)


def _unit_lower_inv_fwd(mats):
    ms = _inv_newton(mats)
    return ms, ms


def _unit_lower_inv_bwd(ms, dms):
    ts = _each(lambda m, dm: _raw3(m, dm, _TN), ms, dms)
    return (_each(lambda t, m: -_raw3(t, m, _NT), ts, ms),)


_unit_lower_inv.defvjp(_unit_lower_inv_fwd, _unit_lower_inv_bwd)


def _gdn_local(q, k, v, dcb, dcb128, dlb128, bb128):
    n = q[0].shape[0]
    r, s = _iota2((n, n), 0), _iota2((n, n), 1)
    same = (r // GDN_CHUNK) == (s // GDN_CHUNK)
    causal, strict = same & (r >= s), same & (r > s)
    eye = _eye(n)
    decay_l = [jnp.exp(jnp.where(causal, d - d.T, NEG)) for d in dcb]
    kb = _each(lambda a, b: a * b, k, bb128)
    a = _each(lambda x, y, dl: jnp.where(strict, _mm1_nt(x, y) * dl, 0.0), kb, k, decay_l)
    m_off = [m - eye for m in _unit_lower_inv(a)]
    edc = [jnp.exp(d) for d in dcb128]
    rhs = _each(lambda vv, bb, kk, e: jnp.concatenate([vv * bb, kk * e], axis=1), v, bb128, kb, edc)
    sol = _each(lambda x, m: x + _mm3(m, x), rhs, m_off)
    aqk = _each(lambda x, y, dl: jnp.where(causal, _mm1_nt(x, y) * dl, 0.0), q, k, decay_l)
    q_dec = _each(lambda x, e: x * e, q, edc)
    k_dec = _each(lambda x, dl, dc: x * jnp.exp(dl - dc), k, dlb128, dcb128)
    return sol, aqk, q_dec, k_dec


def _gated_norm(o, z, nw):
    r = lax.rsqrt(jnp.mean(o * o, axis=-1, keepdims=True) + NORM_EPS)
    return o * r * nw * _silu(z)


def _gated_norm_bwd(dy, o, z, nw):
    r = lax.rsqrt(jnp.mean(o * o, axis=-1, keepdims=True) + NORM_EPS)
    xhat = o * r
    sz = _silu(z)
    dxhat = dy * nw * sz
    do = r * (dxhat - xhat * jnp.mean(dxhat * xhat, axis=-1, keepdims=True))
    dz = dy * xhat * nw * _silu_grad(z)
    dnw = jnp.sum(dy * xhat * sz, axis=0, keepdims=True)
    return do, dz, dnw


GDN_HEAD_GROUP = 8
GC_BETA, GC_G, GC_DECAY, GC_LAST =0, GDN_HEADS, 2 * GDN_HEADS, 3 * GDN_HEADS


def _gate_cols(gc, h, rows):
    lane = _iota2((rows, LANE), 1)
    col = lambda off: jnp.sum(jnp.where(lane == off + h, gc, 0.0), axis=-1, keepdims=True)
    return col(GC_BETA), col(GC_DECAY), col(GC_LAST)


def _gdn_local_args(q_ref, k_ref, v_ref, gc_ref, hp, n):
    gc = gc_ref[...]
    lanes = [slice(hh * LANE, (hh + 1) * LANE) for hh in range(hp)]
    cols = [_gate_cols(gc, pl.program_id(0) * hp + hh, n) for hh in range(hp)]
    bcast = lambda i, w: [jnp.broadcast_to(c[i], (n, w)) for c in cols]
    args = ([q_ref[:, ls] for ls in lanes], [k_ref[:, ls] for ls in lanes], [v_ref[:, ls] for ls in lanes],
            bcast(1, n), bcast(1, LANE), bcast(2, LANE), bcast(0, LANE))
    return args, [c[2] for c in cols], lanes


def gdn_chunk_fwd(qkv, gc, proj, norm_w):
    t = qkv.shape[0]
    c = GDN_CHUNK
    tc = _tile(t, 256)
    nsub = tc // c

    hp = GDN_HEAD_GROUP

    def body(q_ref, k_ref, v_ref, gc_ref, z_ref, nw_ref, on_ref, st_ref, s_ref):
        @pl.when(pl.program_id(1) == 0)
        def _():
            s_ref[...] = jnp.zeros_like(s_ref)

        args, dl, lanes = _gdn_local_args(q_ref, k_ref, v_ref, gc_ref, hp, tc)
        sol, aqk, q_dec, k_dec = _gdn_local(*args)
        u = [x[:, :GDN_DK] for x in sol]
        w = [x[:, GDN_DK:] for x in sol]
        s = [s_ref[hh] for hh in range(hp)]
        v_new, o_state = [[] for _ in range(hp)], [[] for _ in range(hp)]
        for j in range(nsub):
            sl = slice(j * c, (j + 1) * c)
            for hh in range(hp):
                st_ref[hh, j] = s[hh]
            vn = _each(lambda uu, ww, ss: uu[sl] - _bdot(ww[sl], ss), u, w, s)
            os_ = _each(lambda qq, ss: _bdot(qq[sl], ss), q_dec, s)
            s = _each(lambda ss, d, kk, vv: ss * jnp.exp(d[j * c:j * c + 1]) + _bdot_tn(kk[sl], vv), s, dl, k_dec, vn)
            for hh in range(hp):
                v_new[hh].append(vn[hh])
                o_state[hh].append(os_[hh])
        for hh in range(hp):
            s_ref[hh] = s[hh]
        cat = lambda xs: jnp.concatenate(xs, axis=0)
        o = _each(lambda os_, aa, vv: cat(os_) + _bdot(aa, cat(vv)), o_state, aqk, v_new)
        for hh, ls in enumerate(lanes):
            on_ref[:, ls] = _gated_norm(o[hh], z_ref[:, ls], nw_ref[...]).astype(BF16)

    col = lambda off: pl.BlockSpec((tc, hp * LANE), lambda g, i: (i, off // hp + g))
    return pl.pallas_call(
        body, name="gdn_chunk_fwd",
        grid=(GDN_HEADS // hp, t // tc),
        in_specs=[col(0), col(GDN_HEADS), col(2 * GDN_HEADS),
                  pl.BlockSpec((tc, LANE), lambda g, i: (i, 0)),
                  col(GDN_QKV_W // LANE),
                  pl.BlockSpec((1, LANE), lambda g, i: (0, 0))],
        out_specs=[col(0),
                   pl.BlockSpec((hp, nsub, GDN_DK, GDN_DK), lambda g, i: (g, i, 0, 0))],
        out_shape=[jax.ShapeDtypeStruct((t, GDN_HEADS * GDN_DK), BF16),
                   jax.ShapeDtypeStruct((GDN_HEADS, t // c, GDN_DK, GDN_DK), F32)],
        scratch_shapes=[pltpu.VMEM((hp, GDN_DK, GDN_DK), F32)],
        compiler_params=_arb(2),
    )(qkv, qkv, qkv, gc, proj, norm_w)


def gdn_chunk_bwd(qkv, gc, proj, norm_w, states, d_on):
    t = qkv.shape[0]
    c = GDN_CHUNK
    tc = _tile(t, 256)
    nsub = tc // c
    nblk = t // tc
    hp = GDN_HEAD_GROUP

    def body(q_ref, k_ref, v_ref, gc_ref, z_ref, nw_ref, st_ref, don_ref,
             dq_ref, dk_ref, dv_ref, dz_ref, dgc_ref, dnw_ref, ds_ref):
        @pl.when(pl.program_id(1) == 0)
        def _():
            ds_ref[...] = jnp.zeros_like(ds_ref)

        @pl.when((pl.program_id(1) == 0) & (pl.program_id(0) == 0))
        def _():
            dnw_ref[...] = jnp.zeros_like(dnw_ref)

        rows = [slice(j * c, (j + 1) * c) for j in range(nsub)]
        cat = lambda xs: jnp.concatenate(xs, axis=0)
        lsum = lambda x: jnp.sum(x, axis=-1, keepdims=True)
        lane = _iota2((tc, LANE), 1)
        row = _iota2((tc, 1), 0)
        heads = range(hp)
        args, dl, lanes = _gdn_local_args(q_ref, k_ref, v_ref, gc_ref, hp, tc)
        (sol, aqk, q_dec, k_dec), vjp = jax.vjp(_gdn_local, *args)
        u = [x[:, :GDN_DK] for x in sol]
        w = [x[:, GDN_DK:] for x in sol]
        states = [[st_ref[hh, j] for j in range(nsub)] for hh in heads]
        v_new = _each(lambda uu, ww, st: [uu[sl] - _bdot(ww[sl], s) for sl, s in zip(rows, st)], u, w, states)
        v_all = [cat(v) for v in v_new]
        o = _each(lambda qq, st, aa, vv: cat([_bdot(qq[sl], s) for sl, s in zip(rows, st)]) + _bdot(aa, vv),
                  q_dec, states, aqk, v_all)
        nw = nw_ref[...]
        gn = [_gated_norm_bwd(don_ref[:, ls], o[hh], z_ref[:, ls], nw) for hh, ls in enumerate(lanes)]
        do = [x[0] for x in gn]
        for hh, ls in enumerate(lanes):
            dnw_ref[...] += gn[hh][2]
            dz_ref[:, ls] = gn[hh][1].astype(BF16)
        d_aqk = _each(_bdot_nt, do, v_all)
        dv_o = _each(_bdot_tn, aqk, do)
        ds = [ds_ref[hh] for hh in heads]
        d_u, d_w, d_qdec, d_kdec, d_last = ([[None] * nsub for _ in heads] for _ in range(5))
        for j in reversed(range(nsub)):
            sl = rows[j]
            sj = [states[hh][j] for hh in heads]
            cd = [jnp.exp(d[j * c:j * c + 1]) for d in dl]
            du = _each(lambda dvo, kk, dd: dvo[sl] + _bdot(kk[sl], dd), dv_o, k_dec, ds)
            dqd = _each(lambda dd, s: _bdot_nt(dd[sl], s), do, sj)
            dkd = _each(lambda vv, dd: _bdot_nt(vv[j], dd), v_new, ds)
            dla = _each(lambda s, dd, cc: jnp.sum(lsum(s * dd), axis=0, keepdims=True) * cc, sj, ds, cd)
            dw = _each(lambda x, s: -_bdot_nt(x, s), du, sj)
            ds = _each(lambda qq, dd, cc, dsn, ww, x: _bdot_tn(qq[sl], dd[sl]) + cc * dsn - _bdot_tn(ww[sl], x),
                       q_dec, do, cd, ds, w, du)
            for hh in heads:
                d_u[hh][j], d_w[hh][j], d_qdec[hh][j], d_kdec[hh][j], d_last[hh][j] = du[hh], dw[hh], dqd[hh], dkd[hh], dla[hh]
        for hh in heads:
            ds_ref[hh] = ds[hh]
        d_sol = _each(lambda a, b: jnp.concatenate([cat(a), cat(b)], axis=1), d_u, d_w)
        dq, dk, dv, d_dcb, d_dcb128, d_dlb128, d_bb = vjp((d_sol, d_aqk, [cat(x) for x in d_qdec], [cat(x) for x in d_kdec]))
        dgc = jnp.zeros((tc, LANE), F32)
        for hh, ls in enumerate(lanes):
            h = pl.program_id(0) * hp + hh
            dq_ref[:, ls] = dq[hh]
            dk_ref[:, ls] = dk[hh]
            dv_ref[:, ls] = dv[hh]
            d_dl = lsum(d_dlb128[hh])
            for j in range(nsub):
                d_dl = d_dl + jnp.where(row == j * c, d_last[hh][j], 0.0)
            dgc = dgc + jnp.where(lane == GC_BETA + h, lsum(d_bb[hh]),
                                  jnp.where(lane == GC_DECAY + h, lsum(d_dcb[hh]) + lsum(d_dcb128[hh]),
                                            jnp.where(lane == GC_LAST + h, d_dl, 0.0)))
        dgc_ref[0] = dgc

    rev = lambda i: nblk - 1 - i
    col = lambda off: pl.BlockSpec((tc, hp * LANE), lambda g, i: (rev(i), off // hp + g))
    return pl.pallas_call(
        body, name="gdn_chunk_bwd",
        grid=(GDN_HEADS // hp, nblk),
        in_specs=[col(0), col(GDN_HEADS), col(2 * GDN_HEADS),
                  pl.BlockSpec((tc, LANE), lambda g, i: (rev(i), 0)),
                  col(GDN_QKV_W // LANE),
                  pl.BlockSpec((1, LANE), lambda g, i: (0, 0)),
                  pl.BlockSpec((hp, nsub, GDN_DK, GDN_DK), lambda g, i: (g, rev(i), 0, 0)),
                  col(0)],
        out_specs=[col(0), col(0), col(0), col(0),
                   pl.BlockSpec((1, tc, LANE), lambda g, i: (g, rev(i), 0)),
                   pl.BlockSpec((1, LANE), lambda g, i: (0, 0))],
        out_shape=[
            jax.ShapeDtypeStruct((t, GDN_HEADS * GDN_DK), F32),
            jax.ShapeDtypeStruct((t, GDN_HEADS * GDN_DK), F32),
            jax.ShapeDtypeStruct((t, GDN_HEADS * GDN_DK), F32),
            jax.ShapeDtypeStruct((t, GDN_HEADS * GDN_DK), BF16),
            jax.ShapeDtypeStruct((GDN_HEADS // hp, t, LANE), F32),
            jax.ShapeDtypeStruct((1, LANE), F32)],
        scratch_shapes=[pltpu.VMEM((hp, GDN_DK, GDN_DK), F32)],
        compiler_params=_arb(2),
    )(qkv, qkv, qkv, gc, proj, norm_w, states, d_on)


GDN_PROJ_W = GDN_QKV_W + 1024 + LANE
GDN_Q_SCALE = GDN_DK ** -0.5


def _shift_rows(ext, shift, lo, n):
    if shift == 0:
        return ext[lo:lo + n]
    return pltpu.roll(ext, shift, 0)[lo:lo + n]


def _conv_fwd(x, halo, w):
    n = x.shape[0]
    ext = jnp.concatenate([halo, x], axis=0)
    y = w[GDN_CONV - 1:GDN_CONV] * x
    for j in range(GDN_CONV - 1):
        y = y + w[j:j + 1] * _shift_rows(ext, GDN_CONV - 1 - j, 8, n)
    return y


def _chunk_masks(n):
    r, s = _iota2((n, n), 0), _iota2((n, n), 1)
    same = (r // GDN_CHUNK) == (s // GDN_CHUNK)
    return (same & (r >= s)).astype(F32), (same & (r <= s)).astype(F32), same.astype(F32)


def _softplus(x):
    return jnp.maximum(x, 0.0) + jnp.log(1.0 + jnp.exp(-jnp.abs(x)))


def _l2n(t):
    rs = lax.rsqrt(jnp.sum(t * t, axis=-1, keepdims=True) + NORM_EPS)
    return t * rs, rs


def gdn_pre_fwd(proj, conv_w, gate_par):
    t = proj.shape[0]
    tm = _tile(t, 256)

    def body(x_ref, halo_ref, ba_ref, w_ref, gp_ref, qkv_ref, gc_ref):
        i = pl.program_id(0)
        halo = jnp.where(i > 0, halo_ref[...], 0.0)
        y = _silu(_conv_fwd(x_ref[...], halo, w_ref[...]))
        for hh in range(2 * GDN_HEADS):
            sl = slice(hh * LANE, (hh + 1) * LANE)
            tn, _ = _l2n(y[:, sl])
            qkv_ref[:, sl] = tn * GDN_Q_SCALE if hh < GDN_HEADS else tn
        qkv_ref[:, 2 * GDN_HEADS * LANE:] = y[:, 2 * GDN_HEADS * LANE:]
        ba = ba_ref[...]
        lane = _iota2(ba.shape, 1)
        gp = gp_ref[...]
        is_a = (lane >= GC_G) & (lane < GC_G + GDN_HEADS)
        g = jnp.where(is_a, -jnp.exp(gp[0:1]) * _softplus(ba + gp[1:2]), 0.0)
        tri, _, same = _chunk_masks(tm)
        decay = pltpu.roll(_dot(tri, g, HI), GC_DECAY - GC_G, 1)
        last = pltpu.roll(_dot(same, g, HI), GC_LAST - GC_G, 1)
        gc_ref[...] = jnp.where(lane < GDN_HEADS, _sigmoid(ba), g) + decay + last

    return pl.pallas_call(
        body, name="gdn_pre_fwd",
        grid=(t // tm,),
        in_specs=[pl.BlockSpec((tm, GDN_QKV_W), lambda i: (i, 0)),
                  pl.BlockSpec((8, GDN_QKV_W), lambda i: (jnp.maximum(i * (tm // 8) - 1, 0), 0)),
                  pl.BlockSpec((tm, LANE), lambda i: (i, (GDN_QKV_W + 1024) // LANE)),
                  pl.BlockSpec((8, GDN_QKV_W), lambda i: (0, 0)),
                  pl.BlockSpec((8, LANE), lambda i: (0, 0))],
        out_specs=[pl.BlockSpec((tm, GDN_QKV_W), lambda i: (i, 0)),
                   pl.BlockSpec((tm, LANE), lambda i: (i, 0))],
        out_shape=[jax.ShapeDtypeStruct((t, GDN_QKV_W), F32), jax.ShapeDtypeStruct((t, LANE), F32)],
        compiler_params=_arb(1),
    )(proj, proj, proj, conv_w, gate_par)


def gdn_pre_bwd(proj, conv_w, gate_par, dq, dk, dv, dgc_heads):
    t = proj.shape[0]
    tm = _tile(t, 256)

    def body(x_ref, halo_ref, ba_ref, w_ref, gp_ref, dq_ref, dk_ref, dv_ref, dgc_ref, dy_ref, dba_ref, dgp_ref):
        i = pl.program_id(0)

        @pl.when(i == 0)
        def _():
            dgp_ref[...] = jnp.zeros_like(dgp_ref)

        halo = jnp.where(i > 0, halo_ref[...], 0.0)
        y = _conv_fwd(x_ref[...], halo, w_ref[...])
        for hh in range(3 * GDN_HEADS):
            sl = slice(hh * LANE, (hh + 1) * LANE)
            hsl = slice((hh % GDN_HEADS) * LANE, (hh % GDN_HEADS + 1) * LANE)
            yy = y[:, sl]
            if hh < 2 * GDN_HEADS:
                tn, rs = _l2n(_silu(yy))
                dtn = dq_ref[:, hsl] * GDN_Q_SCALE if hh < GDN_HEADS else dk_ref[:, hsl]
                dsil = rs * (dtn - tn * jnp.sum(dtn * tn, axis=-1, keepdims=True))
            else:
                dsil = dv_ref[:, hsl]
            dy_ref[:, sl] = dsil * _silu_grad(yy)
        dgc = dgc_ref[0]
        for hh in range(1, dgc_heads.shape[0]):
            dgc = dgc + dgc_ref[hh]
        ba = ba_ref[...]
        lane = _iota2(ba.shape, 1)
        _, tri_t, same = _chunk_masks(tm)
        d_decay = jnp.where((lane >= GC_DECAY) & (lane < GC_DECAY + GDN_HEADS), dgc, 0.0)
        d_last = jnp.where((lane >= GC_LAST) & (lane < GC_LAST + GDN_HEADS), dgc, 0.0)
        dgc = (jnp.where(lane < GDN_HEADS, dgc, 0.0) + pltpu.roll(_dot(tri_t, d_decay, HI), LANE - (GC_DECAY - GC_G), 1)
               + pltpu.roll(_dot(same, d_last, HI), LANE - (GC_LAST - GC_G), 1))
        gp = gp_ref[...]
        xg = ba + gp[1:2]
        ea = jnp.exp(gp[0:1])
        sp = _softplus(xg)
        sb = _sigmoid(ba)
        is_b = lane < GDN_HEADS
        is_a = (lane >= GDN_HEADS) & (lane < 2 * GDN_HEADS)
        d_pre = jnp.where(is_a, dgc * (-ea) * _sigmoid(xg), 0.0)
        dba_ref[...] = jnp.where(is_b, dgc * sb * (1.0 - sb), d_pre).astype(BF16)
        d_alog = jnp.sum(jnp.where(is_a, dgc * (-ea) * sp, 0.0), axis=0, keepdims=True)
        d_dtb = jnp.sum(d_pre, axis=0, keepdims=True)
        row = _iota2((8, LANE), 0)
        dgp_ref[...] += jnp.where(row == 0, d_alog, jnp.where(row == 1, d_dtb, 0.0))

    hspec = pl.BlockSpec((tm, GDN_HEADS * LANE), lambda i: (i, 0))
    return pl.pallas_call(
        body, name="gdn_pre_bwd",
        grid=(t // tm,),
        in_specs=[pl.BlockSpec((tm, GDN_QKV_W), lambda i: (i, 0)),
                  pl.BlockSpec((8, GDN_QKV_W), lambda i: (jnp.maximum(i * (tm // 8) - 1, 0), 0)),
                  pl.BlockSpec((tm, LANE), lambda i: (i, (GDN_QKV_W + 1024) // LANE)),
                  pl.BlockSpec((8, GDN_QKV_W), lambda i: (0, 0)),
                  pl.BlockSpec((8, LANE), lambda i: (0, 0)),
                  hspec, hspec, hspec,
                  pl.BlockSpec((dgc_heads.shape[0], tm, LANE), lambda i: (0, i, 0))],
        out_specs=[pl.BlockSpec((tm, GDN_QKV_W), lambda i: (i, 0)),
                   pl.BlockSpec((tm, LANE), lambda i: (i, 0)),
                   pl.BlockSpec((8, LANE), lambda i: (0, 0))],
        out_shape=[jax.ShapeDtypeStruct((t, GDN_QKV_W), F32), jax.ShapeDtypeStruct((t, LANE), BF16),
                   jax.ShapeDtypeStruct((8, LANE), F32)],
        compiler_params=_arb(1),
    )(proj, proj, proj, conv_w, gate_par, dq, dk, dv, dgc_heads)


def gdn_conv_bwd(proj, conv_w, dy):
    t = proj.shape[0]
    tm = _tile(t, 256)
    nblk = t // tm

    def body(x_ref, halo_ref, w_ref, dy_ref, dyn_ref, dx_ref, dw_ref):
        i = pl.program_id(0)

        @pl.when(i == 0)
        def _():
            dw_ref[...] = jnp.zeros_like(dw_ref)

        w = w_ref[...]
        dy = dy_ref[...]
        ext_dy = jnp.concatenate([dy, jnp.where(i < nblk - 1, dyn_ref[...], 0.0)], axis=0)
        ext_x = jnp.concatenate([jnp.where(i > 0, halo_ref[...], 0.0), x_ref[...]], axis=0)
        dx = w[GDN_CONV - 1:GDN_CONV] * dy
        rows = [jnp.sum(dy * x_ref[...], axis=0, keepdims=True)]
        for j in range(GDN_CONV - 1):
            sh = GDN_CONV - 1 - j
            dx = dx + w[j:j + 1] * _shift_rows(ext_dy, tm + 8 - sh, 0, tm)
            rows.insert(j, jnp.sum(dy * _shift_rows(ext_x, sh, 8, tm), axis=0, keepdims=True))
        dx_ref[...] = dx.astype(BF16)
        row = _iota2((8, GDN_QKV_W), 0)
        acc = jnp.zeros((8, GDN_QKV_W), F32)
        for j in range(GDN_CONV):
            acc = acc + jnp.where(row == j, rows[j], 0.0)
        dw_ref[...] += acc

    return pl.pallas_call(
        body, name="gdn_conv_bwd",
        grid=(nblk,),
        in_specs=[pl.BlockSpec((tm, GDN_QKV_W), lambda i: (i, 0)),
                  pl.BlockSpec((8, GDN_QKV_W), lambda i: (jnp.maximum(i * (tm // 8) - 1, 0), 0)),
                  pl.BlockSpec((8, GDN_QKV_W), lambda i: (0, 0)),
                  pl.BlockSpec((tm, GDN_QKV_W), lambda i: (i, 0)),
                  pl.BlockSpec((8, GDN_QKV_W), lambda i: (jnp.minimum((i + 1) * (tm // 8), t // 8 - 1), 0))],
        out_specs=[pl.BlockSpec((tm, GDN_QKV_W), lambda i: (i, 0)),
                   pl.BlockSpec((8, GDN_QKV_W), lambda i: (0, 0))],
        out_shape=[jax.ShapeDtypeStruct((t, GDN_QKV_W), BF16), jax.ShapeDtypeStruct((8, GDN_QKV_W), F32)],
        compiler_params=_arb(1),
    )(proj, proj, conv_w, dy, dy)


def _resident(w_hbm, w_vmem, sem):
    @pl.when(pl.program_id(0) == 0)
    def _():
        cp = pltpu.make_async_copy(w_hbm, w_vmem, sem)
        cp.start()
        cp.wait()


ANY = pl.BlockSpec(memory_space=pl.ANY)


def norm_proj(h, nw, w, bias):
    t, d = h.shape
    n = w.shape[1]
    tm = _tile(t, 256)
    nc = _tile(n, 1536) if n % 1536 == 0 else _tile(n, 1408)

    def body(h_ref, nw_ref, w_hbm, b_ref, o_ref, w_ref, sem):
        _resident(w_hbm, w_ref, sem)
        hn = _rms(h_ref[...], nw_ref[...])[0].astype(BF16)
        for c0 in range(0, n, nc):
            o_ref[:, c0:c0 + nc] = _dot(hn, w_ref[:, c0:c0 + nc]) + b_ref[:, c0:c0 + nc]

    return pl.pallas_call(
        body, name="norm_proj",
        grid=(t // tm,),
        in_specs=[pl.BlockSpec((tm, d), lambda i: (i, 0)), pl.BlockSpec((1, d), lambda i: (0, 0)), ANY,
                  pl.BlockSpec((1, n), lambda i: (0, 0))],
        out_specs=pl.BlockSpec((tm, n), lambda i: (i, 0)),
        out_shape=jax.ShapeDtypeStruct((t, n), F32),
        scratch_shapes=[pltpu.VMEM((d, n), BF16), pltpu.SemaphoreType.DMA],
        compiler_params=_arb(1),
    )(h, nw, w, bias)


def linear_residual(h, x, w, bias):
    t, d = h.shape
    k = x.shape[1]
    tm = _tile(t, 512)

    def body(h_ref, x_ref, w_hbm, b_ref, o_ref, w_ref, sem):
        _resident(w_hbm, w_ref, sem)
        o_ref[...] = h_ref[...] + _dot(x_ref[...], w_ref[...]) + b_ref[...]

    return pl.pallas_call(
        body, name="linear_residual",
        grid=(t // tm,),
        in_specs=[pl.BlockSpec((tm, d), lambda i: (i, 0)), pl.BlockSpec((tm, k), lambda i: (i, 0)), ANY,
                  pl.BlockSpec((1, d), lambda i: (0, 0))],
        out_specs=pl.BlockSpec((tm, d), lambda i: (i, 0)),
        out_shape=jax.ShapeDtypeStruct((t, d), F32),
        scratch_shapes=[pltpu.VMEM((k, d), BF16), pltpu.SemaphoreType.DMA],
        compiler_params=_arb(1),
    )(h, x, w, bias)


def matmul_nt(dy, w):
    t, d = dy.shape
    k = w.shape[0]
    tm = _tile(t, 512)

    def body(dy_ref, w_hbm, o_ref, cs_ref, w_ref, sem):
        _resident(w_hbm, w_ref, sem)

        @pl.when(pl.program_id(0) == 0)
        def _():
            cs_ref[...] = jnp.zeros_like(cs_ref)

        dy = dy_ref[...]
        cs_ref[...] += jnp.sum(dy, axis=0, keepdims=True)
        o_ref[...] = _dot_nt(dy.astype(BF16), w_ref[...])

    return pl.pallas_call(
        body, name="matmul_nt",
        grid=(t // tm,),
        in_specs=[pl.BlockSpec((tm, d), lambda i: (i, 0)), ANY],
        out_specs=[pl.BlockSpec((tm, k), lambda i: (i, 0)), pl.BlockSpec((1, d), lambda i: (0, 0))],
        out_shape=[jax.ShapeDtypeStruct((t, k), F32), jax.ShapeDtypeStruct((1, d), F32)],
        scratch_shapes=[pltpu.VMEM((k, d), BF16), pltpu.SemaphoreType.DMA],
        compiler_params=_arb(1),
    )(dy, w)


def norm_proj_bwd(h, nw, dh, dps, ws):
    t, d = h.shape
    np_ = len(dps)
    ns = [w.shape[1] for w in ws]
    tm = _tile(t, 256)

    def body(*refs):
        h_ref, nw_ref, dh_ref = refs[:3]
        dp_refs = refs[3:3 + np_]
        w_hbms = refs[3 + np_:3 + 2 * np_]
        o_ref, hn_ref, dnw_ref = refs[3 + 2 * np_:6 + 2 * np_]
        cs_refs = refs[6 + 2 * np_:6 + 3 * np_]
        w_refs = refs[6 + 3 * np_:6 + 4 * np_]
        sem = refs[6 + 4 * np_]
        for a, b in zip(w_hbms, w_refs):
            _resident(a, b, sem)

        @pl.when(pl.program_id(0) == 0)
        def _():
            dnw_ref[...] = jnp.zeros_like(dnw_ref)
            for c in cs_refs:
                c[...] = jnp.zeros_like(c)

        nw = nw_ref[...]
        hn, xhat, r = _rms(h_ref[...], nw)
        hn_ref[...] = hn.astype(BF16)
        dhn = jnp.zeros((tm, d), F32)
        for dp_ref, w_ref, cs_ref in zip(dp_refs, w_refs, cs_refs):
            dp = dp_ref[...]
            cs_ref[...] += jnp.sum(dp.astype(F32), axis=0, keepdims=True)
            dhn = dhn + _dot_nt(dp, w_ref[...])
        dx, dnw = _rms_bwd(dhn, nw, xhat, r)
        dnw_ref[...] += dnw
        o_ref[...] = dh_ref[...] + dx

    row = pl.BlockSpec((tm, d), lambda i: (i, 0))
    vec = pl.BlockSpec((1, d), lambda i: (0, 0))
    return pl.pallas_call(
        body, name="norm_proj_bwd",
        grid=(t // tm,),
        in_specs=[row, vec, row] + [pl.BlockSpec((tm, n), lambda i: (i, 0)) for n in ns] + [ANY] * np_,
        out_specs=[row, row, vec] + [pl.BlockSpec((1, n), lambda i: (0, 0)) for n in ns],
        out_shape=[jax.ShapeDtypeStruct((t, d), F32), jax.ShapeDtypeStruct((t, d), BF16),
                   jax.ShapeDtypeStruct((1, d), F32)] + [jax.ShapeDtypeStruct((1, n), F32) for n in ns],
        scratch_shapes=[pltpu.VMEM((d, n), BF16) for n in ns] + [pltpu.SemaphoreType.DMA],
        compiler_params=_arb(1),
    )(h, nw, dh, *dps, *ws)


def matmul_tn(x, y, scale=1.0):
    t, k = x.shape
    n = y.shape[1]
    tk = _tile(k, 1024) if k % 1024 == 0 else _tile(k, 1408)
    tn = n if n <= 1536 else (1024 if n % 1024 == 0 else 1408)
    assert n % tn == 0
    tt = _tile(t, 1024)
    nt = t // tt

    def body(x_ref, y_ref, o_ref):
        @pl.when(pl.program_id(2) == 0)
        def _():
            o_ref[...] = jnp.zeros_like(o_ref)

        yv = y_ref[...]
        if scale != 1.0:
            yv = yv * scale
        o_ref[...] += _dot_tn(x_ref[...].astype(BF16), yv.astype(BF16))

    return pl.pallas_call(
        body, name="matmul_tn",
        grid=(k // tk, n // tn, nt),
        in_specs=[pl.BlockSpec((tt, tk), lambda i, j, s: (s, i)), pl.BlockSpec((tt, tn), lambda i, j, s: (s, j))],
        out_specs=pl.BlockSpec((tk, tn), lambda i, j, s: (i, j)),
        out_shape=jax.ShapeDtypeStruct((k, n), F32),
        compiler_params=_arb(3),
    )(x, y)


FFN_CHUNKS = 2


def ffn_fwd(h, nw, wgu, wd):
    t, d = h.shape
    f = wd.shape[0]
    fc = f // FFN_CHUNKS
    tm = _tile(t, 512)

    def body(h_ref, nw_ref, wgu_hbm, wd_hbm, o_ref, wgu_ref, wd_ref, sem):
        _resident(wgu_hbm, wgu_ref, sem)
        _resident(wd_hbm, wd_ref, sem)
        x = h_ref[...]
        hn = _rms(x, nw_ref[...])[0].astype(BF16)
        acc = jnp.zeros((tm, d), F32)
        for c in range(FFN_CHUNKS):
            g = _dot(hn, wgu_ref[:, c * fc:(c + 1) * fc])
            u = _dot(hn, wgu_ref[:, f + c * fc:f + (c + 1) * fc])
            acc = acc + _dot((_silu(g) * u).astype(BF16), wd_ref[c * fc:(c + 1) * fc, :])
        o_ref[...] = x + 0.5 * acc

    return pl.pallas_call(
        body, name="ffn_fwd",
        grid=(t // tm,),
        in_specs=[pl.BlockSpec((tm, d), lambda i: (i, 0)), pl.BlockSpec((1, d), lambda i: (0, 0)), ANY, ANY],
        out_specs=pl.BlockSpec((tm, d), lambda i: (i, 0)),
        out_shape=jax.ShapeDtypeStruct((t, d), F32),
        scratch_shapes=[pltpu.VMEM((d, 2 * f), BF16), pltpu.VMEM((f, d), BF16), pltpu.SemaphoreType.DMA],
        compiler_params=_arb(1),
    )(h, nw, wgu, wd)


def ffn_bwd(h, nw, wgu, wd, dh):
    t, d = h.shape
    f = wd.shape[0]
    fc = f // FFN_CHUNKS
    tm = _tile(t, 256)

    def body(h_ref, nw_ref, wgu_hbm, wd_hbm, dh_ref, o_ref, hn_ref, a_ref, dgu_ref, dnw_ref, wgu_ref, wd_ref, sem):
        _resident(wgu_hbm, wgu_ref, sem)
        _resident(wd_hbm, wd_ref, sem)

        @pl.when(pl.program_id(0) == 0)
        def _():
            dnw_ref[...] = jnp.zeros_like(dnw_ref)

        nw = nw_ref[...]
        hn32, xhat, r = _rms(h_ref[...], nw)
        hn = hn32.astype(BF16)
        hn_ref[...] = hn
        dh = dh_ref[...]
        dout = (0.5 * dh).astype(BF16)
        dhn = jnp.zeros((tm, d), F32)
        for c in range(FFN_CHUNKS):
            gs, us = slice(c * fc, (c + 1) * fc), slice(f + c * fc, f + (c + 1) * fc)
            g = _dot(hn, wgu_ref[:, gs])
            u = _dot(hn, wgu_ref[:, us])
            sg = _sigmoid(g)
            sil = g * sg
            a_ref[:, gs] = (sil * u).astype(BF16)
            da = _dot_nt(dout, wd_ref[gs, :])
            dg = (da * u * (sg * (1.0 + g * (1.0 - sg)))).astype(BF16)
            du = (da * sil).astype(BF16)
            dgu_ref[:, gs] = dg
            dgu_ref[:, us] = du
            dhn = dhn + _dot_nt(dg, wgu_ref[:, gs]) + _dot_nt(du, wgu_ref[:, us])
        dx, dnw = _rms_bwd(dhn, nw, xhat, r)
        dnw_ref[...] += dnw
        o_ref[...] = dh + dx

    row = pl.BlockSpec((tm, d), lambda i: (i, 0))
    vec = pl.BlockSpec((1, d), lambda i: (0, 0))
    return pl.pallas_call(
        body, name="ffn_bwd",
        grid=(t // tm,),
        in_specs=[row, vec, ANY, ANY, row],
        out_specs=[row, row, pl.BlockSpec((tm, f), lambda i: (i, 0)), pl.BlockSpec((tm, 2 * f), lambda i: (i, 0)), vec],
        out_shape=[jax.ShapeDtypeStruct((t, d), F32), jax.ShapeDtypeStruct((t, d), BF16),
                   jax.ShapeDtypeStruct((t, f), BF16), jax.ShapeDtypeStruct((t, 2 * f), BF16),
                   jax.ShapeDtypeStruct((1, d), F32)],
        scratch_shapes=[pltpu.VMEM((d, 2 * f), BF16), pltpu.VMEM((f, d), BF16), pltpu.SemaphoreType.DMA],
        compiler_params=_arb(1),
    )(h, nw, wgu, wd, dh)


def loss_head(h, nw, target):
    t, d = h.shape
    tm = _tile(t, 512)

    def body(h_ref, nw_ref, tg_ref, dh_ref, loss_ref, dnw_ref):
        @pl.when(pl.program_id(0) == 0)
        def _():
            loss_ref[...] = jnp.zeros_like(loss_ref)
            dnw_ref[...] = jnp.zeros_like(dnw_ref)

        nw = nw_ref[...]
        y, xhat, r = _rms(h_ref[...], nw)
        e = y - tg_ref[...]
        loss_ref[...] += 0.5 * jnp.sum(jnp.mean(e * e, axis=-1, keepdims=True), axis=0, keepdims=True)
        dx, dnw = _rms_bwd(e * (1.0 / d), nw, xhat, r)
        dnw_ref[...] += dnw
        dh_ref[...] = dx

    row = pl.BlockSpec((tm, d), lambda i: (i, 0))
    vec = pl.BlockSpec((1, d), lambda i: (0, 0))
    return pl.pallas_call(
        body, name="loss_head",
        grid=(t // tm,),
        in_specs=[row, vec, row],
        out_specs=[row, pl.BlockSpec((8, LANE), lambda i: (0, 0)), vec],
        out_shape=[jax.ShapeDtypeStruct((t, d), F32), jax.ShapeDtypeStruct((8, LANE), F32),
                   jax.ShapeDtypeStruct((1, d), F32)],
        compiler_params=_arb(1),
    )(h, nw, target)


def adamw(w, g, m, v):
    r, c = w.shape
    tr = r
    while tr * c * 4 > (1 << 20) and tr % 16 == 0:
        tr //= 2

    def body(w_ref, g_ref, m_ref, v_ref, d_ref, nm_ref, nv_ref):
        g = g_ref[...]
        m = ADAM_B1 * m_ref[...] + (1.0 - ADAM_B1) * g
        v = ADAM_B2 * v_ref[...] + (1.0 - ADAM_B2) * (g * g)
        m_hat = m / (1.0 - ADAM_B1 ** ADAM_STEP)
        v_hat = v / (1.0 - ADAM_B2 ** ADAM_STEP)
        d_ref[...] = -ADAM_LR * (m_hat / (jnp.sqrt(v_hat) + ADAM_EPS) + ADAM_WD * w_ref[...])
        nm_ref[...] = m
        nv_ref[...] = v

    blk = pl.BlockSpec((tr, c), lambda i: (i, 0))
    return pl.pallas_call(
        body, name="adamw",
        grid=(r // tr,),
        in_specs=[blk] * 4, out_specs=[blk] * 3,
        out_shape=[jax.ShapeDtypeStruct((r, c), F32)] * 3,
        compiler_params=_arb(1),
    )(w, g, m, v)


ATTN_P_W = ATTN_Q_W + 2 * ATTN_KV_W
ATTN_SCALE = ATTN_HEAD_DIM ** -0.5


def _rope_group(t, tab, sign):
    return (t * tab[:, 0:LANE] + sign * pltpu.roll(t, 8, 1) * tab[:, LANE:2 * LANE]
            + sign * pltpu.roll(t, LANE - 8, 1) * tab[:, 2 * LANE:3 * LANE])


def _rope(t, tab, sign=1.0):
    return jnp.concatenate([_rope_group(t[:, s:s + LANE], tab, sign) for s in range(0, t.shape[1], LANE)], axis=1)


def _attn_heads(qs, ks, vs, sinks, first):
    b = ATTN_BLOCK
    rows = ATTN_GROUP * b
    qi = _iota2((rows, 2 * b), 0) % b
    kj = _iota2((rows, 2 * b), 1)
    rel = qi + b - kj
    valid = (rel >= 0) & (rel < b) & ((kj >= b) | jnp.logical_not(first))
    ss = _each(lambda q, k: jnp.where(valid, _mm1_nt(q, k) * ATTN_SCALE, NEG), qs, ks)
    ms = _each(lambda s, sink: lax.stop_gradient(jnp.maximum(jnp.max(s, axis=-1, keepdims=True), sink)), ss, sinks)
    ps = _each(lambda s, m: jnp.exp(s - m), ss, ms)
    dens = _each(lambda p, sink, m: jnp.sum(p, axis=-1, keepdims=True) + jnp.exp(sink - m), ps, sinks, ms)
    return _each(lambda p, den, v: _mm1(p / den, v), ps, dens, vs)


def _attn_prepare(p_ref, kvp_ref, tab_ref, tabp_ref, sink_ref):
    b = ATTN_BLOCK
    hd = ATTN_HEAD_DIM
    tab, tabp = tab_ref[...], tabp_ref[...]
    q = _rope(p_ref[:, 0:ATTN_Q_W], tab)
    kc = _rope(p_ref[:, ATTN_Q_W:ATTN_Q_W + ATTN_KV_W], tab)
    kp = _rope(kvp_ref[:, 0:ATTN_KV_W], tabp)
    vc = p_ref[:, ATTN_Q_W + ATTN_KV_W:ATTN_P_W]
    vp = kvp_ref[:, ATTN_KV_W:2 * ATTN_KV_W]
    sk = sink_ref[...]
    qs, ks, vs, sinks = [], [], [], []
    for h in range(ATTN_KV_HEADS):
        heads = [ATTN_GROUP * h + g for g in range(ATTN_GROUP)]
        qs.append(jnp.concatenate([q[:, i * hd:(i + 1) * hd] for i in heads], axis=0))
        ks.append(jnp.concatenate([kp[:, h * hd:(h + 1) * hd], kc[:, h * hd:(h + 1) * hd]], axis=0))
        vs.append(jnp.concatenate([vp[:, h * hd:(h + 1) * hd], vc[:, h * hd:(h + 1) * hd]], axis=0))
        sinks.append(jnp.concatenate([jnp.broadcast_to(sk[:, i:i + 1], (b, 1)) for i in heads], axis=0))
    return qs, ks, vs, sinks


def _unstack_heads(xs):
    b = ATTN_BLOCK
    return jnp.concatenate([x[g * b:(g + 1) * b] for x in xs for g in range(ATTN_GROUP)], axis=1)


def _attn_specs(nb):
    b = ATTN_BLOCK
    prev = lambda n: jnp.maximum(n - 1, 0)
    return [pl.BlockSpec((b, ATTN_P_W), lambda n: (n, 0)),
            pl.BlockSpec((b, 2 * ATTN_KV_W), lambda n: (prev(n), ATTN_Q_W // (2 * ATTN_KV_W))),
            pl.BlockSpec((b, 3 * LANE), lambda n: (n, 0)),
            pl.BlockSpec((b, 3 * LANE), lambda n: (prev(n), 0)),
            pl.BlockSpec((1, ATTN_Q_HEADS), lambda n: (0, 0))]


def attn_fwd(p, tab, sinks):
    t = p.shape[0]
    nb = t // ATTN_BLOCK

    def body(p_ref, kvp_ref, tab_ref, tabp_ref, sink_ref, o_ref):
        qs, ks, vs, sk = _attn_prepare(p_ref, kvp_ref, tab_ref, tabp_ref, sink_ref)
        os_ = _attn_heads(qs, ks, vs, sk, pl.program_id(0) == 0)
        o_ref[...] = _unstack_heads(os_).astype(BF16)

    return pl.pallas_call(
        body, name="attn_fwd",
        grid=(nb,),
        in_specs=_attn_specs(nb),
        out_specs=pl.BlockSpec((ATTN_BLOCK, ATTN_Q_W), lambda n: (n, 0)),
        out_shape=jax.ShapeDtypeStruct((t, ATTN_Q_W), BF16),
        compiler_params=_arb(1),
    )(p, p, tab, tab, sinks)


def attn_bwd(p, tab, sinks, do):
    t = p.shape[0]
    b = ATTN_BLOCK
    hd = ATTN_HEAD_DIM
    nb = t // b

    def body(p_ref, kvp_ref, tab_ref, tabp_ref, sink_ref, do_ref, dq_ref, dkvc_ref, dkvp_ref, dsink_ref):
        n = pl.program_id(0)

        @pl.when(n == 0)
        def _():
            dsink_ref[...] = jnp.zeros_like(dsink_ref)

        qs, ks, vs, sk = _attn_prepare(p_ref, kvp_ref, tab_ref, tabp_ref, sink_ref)
        first = n == 0
        _, vjp = jax.vjp(lambda a, bb, c, d: _attn_heads(a, bb, c, d, first), qs, ks, vs, sk)
        do = do_ref[...]
        dos = [jnp.concatenate([do[:, i * hd:(i + 1) * hd] for i in range(ATTN_GROUP * h, ATTN_GROUP * (h + 1))], axis=0)
               for h in range(ATTN_KV_HEADS)]
        dqs, dks, dvs, dsk = vjp(dos)
        tab, tabp = tab_ref[...], tabp_ref[...]
        dq_ref[...] = _rope(_unstack_heads([x.astype(F32) for x in dqs]), tab, -1.0).astype(BF16)
        dkc = jnp.concatenate([x.astype(F32)[b:] for x in dks], axis=1)
        dkp = jnp.concatenate([x.astype(F32)[:b] for x in dks], axis=1)
        dkvc_ref[:, 0:ATTN_KV_W] = _rope(dkc, tab, -1.0)
        dkvp_ref[:, 0:ATTN_KV_W] = _rope(dkp, tabp, -1.0)
        dkvc_ref[:, ATTN_KV_W:] = jnp.concatenate([x.astype(F32)[b:] for x in dvs], axis=1)
        dkvp_ref[:, ATTN_KV_W:] = jnp.concatenate([x.astype(F32)[:b] for x in dvs], axis=1)
        parts = [jnp.sum(d[g * b:(g + 1) * b], axis=0, keepdims=True) for d in dsk for g in range(ATTN_GROUP)]
        dsink_ref[...] += jnp.concatenate(parts, axis=1)

    blk = lambda w: pl.BlockSpec((b, w), lambda n: (n, 0))
    return pl.pallas_call(
        body, name="attn_bwd",
        grid=(nb,),
        in_specs=_attn_specs(nb) + [blk(ATTN_Q_W)],
        out_specs=[blk(ATTN_Q_W), blk(2 * ATTN_KV_W), blk(2 * ATTN_KV_W),
                   pl.BlockSpec((1, ATTN_Q_HEADS), lambda n: (0, 0))],
        out_shape=[jax.ShapeDtypeStruct((t, ATTN_Q_W), BF16), jax.ShapeDtypeStruct((t, 2 * ATTN_KV_W), F32),
                   jax.ShapeDtypeStruct((t, 2 * ATTN_KV_W), F32), jax.ShapeDtypeStruct((1, ATTN_Q_HEADS), F32)],
        compiler_params=_arb(1),
    )(p, p, tab, tab, sinks, do)


def kv_combine(dkvc, dkvp):
    t, w = dkvc.shape
    b = ATTN_BLOCK
    nb = t // b

    def body(c_ref, p_ref, o_ref):
        nxt = jnp.where(pl.program_id(0) < nb - 1, p_ref[...], 0.0)
        o_ref[...] = (c_ref[...] + nxt).astype(BF16)

    return pl.pallas_call(
        body, name="kv_combine",
        grid=(nb,),
        in_specs=[pl.BlockSpec((b, w), lambda n: (n, 0)),
                  pl.BlockSpec((b, w), lambda n: (jnp.minimum(n + 1, nb - 1), 0))],
        out_specs=pl.BlockSpec((b, w), lambda n: (n, 0)),
        out_shape=jax.ShapeDtypeStruct((t, w), BF16),
        compiler_params=_arb(1),
    )(dkvc, dkvp)


def _me():
    return lax.axis_index("x"), lax.axis_index("y"), lax.axis_index("c")


def _flip(v, bit):
    return 1 - v if bit else v


def _chip_index():
    return 2 * lax.axis_index("x") + lax.axis_index("y")


def allgather_chips(x, split):
    r, c = x.shape
    rh = r // 2 if split else r

    def body(x_ref, o_ref, send_sems, recv_sems):
        mx, my, mc = _me()
        me = 2 * mx + my
        rows = pl.ds(pl.multiple_of(mc * rh, 16), rh) if split else pl.ds(0, r)
        fetched, passed = [], []
        for k in (1, 2, 3):
            peer = (_flip(mx, k >> 1), _flip(my, k & 1), mc)
            cp = pltpu.make_async_remote_copy(src_ref=x_ref.at[rows], dst_ref=o_ref.at[me, rows],
                                              send_sem=send_sems.at[k - 1], recv_sem=recv_sems.at[k - 1],
                                              device_id=peer, device_id_type=MESH)
            cp.start()
            fetched.append(cp)
        if split:
            for k in (1, 2, 3):
                fetched[k - 1].wait_recv()
                theirs = o_ref.at[2 * _flip(mx, k >> 1) + _flip(my, k & 1), rows]
                cp = pltpu.make_async_remote_copy(src_ref=theirs, dst_ref=theirs, send_sem=send_sems.at[2 + k],
                                                  recv_sem=recv_sems.at[2 + k], device_id=(mx, my, 1 - mc),
                                                  device_id_type=MESH)
                cp.start()
                passed.append(cp)
            for cp in fetched:
                cp.wait_send()
            for cp in passed:
                cp.wait()
        else:
            for cp in fetched:
                cp.wait()

    got = pl.pallas_call(
        body, name="allgather_chips",
        in_specs=[ANY], out_specs=ANY,
        out_shape=jax.ShapeDtypeStruct((4, r, c), x.dtype),
        scratch_shapes=[pltpu.SemaphoreType.DMA((6,)), pltpu.SemaphoreType.DMA((6,))],
    )(x)
    return lax.dynamic_update_slice(got, x[None], (_chip_index(), 0, 0))


def pair_exchange(g):
    _, _, r, c = g.shape

    def body(g_ref, got_ref, send_sems, recv_sems):
        mx, my, mc = _me()
        sib = (mx, my, 1 - mc)
        copies = []
        for j in range(4):
            cp = pltpu.make_async_remote_copy(src_ref=g_ref.at[j, 1 - mc], dst_ref=got_ref.at[j], send_sem=send_sems.at[j],
                                              recv_sem=recv_sems.at[j], device_id=sib, device_id_type=MESH)
            cp.start()
            copies.append(cp)
        for cp in copies:
            cp.wait()

    return pl.pallas_call(
        body, name="pair_exchange",
        in_specs=[ANY], out_specs=ANY,
        out_shape=jax.ShapeDtypeStruct((4, r, c), g.dtype),
        scratch_shapes=[pltpu.SemaphoreType.DMA((4,))] * 2,
    )(g)


def scatter_chips(p):
    _, r, c = p.shape

    def body(p_ref, o_ref, send_sems, recv_sems):
        mx, my, mc = _me()
        me = 2 * mx + my
        copies = []
        for k in (1, 2, 3):
            px, py = _flip(mx, k >> 1), _flip(my, k & 1)
            cp = pltpu.make_async_remote_copy(src_ref=p_ref.at[2 * px + py], dst_ref=o_ref.at[me],
                                              send_sem=send_sems.at[k - 1], recv_sem=recv_sems.at[k - 1],
                                              device_id=(px, py, mc), device_id_type=MESH)
            cp.start()
            copies.append(cp)
        for cp in copies:
            cp.wait()

    return pl.pallas_call(
        body, name="scatter_chips",
        in_specs=[ANY], out_specs=ANY,
        out_shape=jax.ShapeDtypeStruct((4, r, c), p.dtype),
        scratch_shapes=[pltpu.SemaphoreType.DMA((3,)), pltpu.SemaphoreType.DMA((3,))],
    )(p)


def pair_share(o):
    r, c = o.shape

    def body(o_ref, out_ref, send_sem, recv_sem):
        mx, my, mc = _me()
        cp = pltpu.make_async_remote_copy(src_ref=o_ref, dst_ref=out_ref, send_sem=send_sem, recv_sem=recv_sem,
                                          device_id=(mx, my, 1 - mc), device_id_type=MESH)
        cp.start()
        cp.wait()

    return pl.pallas_call(
        body, name="pair_share",
        in_specs=[ANY], out_specs=ANY,
        out_shape=jax.ShapeDtypeStruct((r, c), o.dtype),
        scratch_shapes=[pltpu.SemaphoreType.DMA, pltpu.SemaphoreType.DMA],
    )(o)


def _row_tile(r, c):
    tr = 8
    while r % (2 * tr) == 0 and 2 * tr * c * 4 <= (2 << 20):
        tr *= 2
    return tr


def add_pair(g, got, core):
    _, _, r, c = g.shape
    tr = _row_tile(r, c)

    def body(core_ref, g_ref, got_ref, p32_ref, p16_ref):
        s = g_ref[...] + got_ref[...]
        p32_ref[...] = s
        p16_ref[...] = s.astype(BF16)

    blk = pl.BlockSpec((None, tr, c), lambda j, i, core_ref: (j, i, 0))
    return pl.pallas_call(
        body, name="add_pair",
        grid_spec=pltpu.PrefetchScalarGridSpec(
            num_scalar_prefetch=1, grid=(4, r // tr),
            in_specs=[pl.BlockSpec((None, None, tr, c), lambda j, i, core_ref: (j, core_ref[0], i, 0)), blk],
            out_specs=[blk, blk]),
        out_shape=[jax.ShapeDtypeStruct((4, r, c), F32), jax.ShapeDtypeStruct((4, r, c), BF16)],
        compiler_params=_arb(2),
    )(core, g, got)


def sum_slots(p32, q16, order):
    _, r, c = p32.shape
    tr = _row_tile(r, c)

    def body(order_ref, own_ref, a_ref, b_ref, c_ref, o_ref):
        o_ref[...] = ((own_ref[...] + a_ref[...].astype(F32)) + b_ref[...].astype(F32)) + c_ref[...].astype(F32)

    slot = lambda k: pl.BlockSpec((None, tr, c), functools.partial(lambda k, i, order_ref: (order_ref[k], i, 0), k))
    return pl.pallas_call(
        body, name="sum_slots",
        grid_spec=pltpu.PrefetchScalarGridSpec(
            num_scalar_prefetch=1, grid=(r // tr,),
            in_specs=[slot(0), slot(1), slot(2), slot(3)],
            out_specs=pl.BlockSpec((tr, c), lambda i, order_ref: (i, 0))),
        out_shape=jax.ShapeDtypeStruct((r, c), F32),
        compiler_params=_arb(1),
    )(order, p32, q16, q16, q16)


def allreduce_small(x):
    r, c = x.shape

    def body(x_ref, o_ref, buf, send_sems, recv_sems):
        mx, my, mc = _me()
        me = 4 * mx + 2 * my + mc
        buf[pl.ds(me, 1)] = x_ref[...][None]
        copies = []
        for k in range(1, 8):
            peer = (_flip(mx, k >> 2), _flip(my, (k >> 1) & 1), _flip(mc, k & 1))
            cp = pltpu.make_async_remote_copy(src_ref=x_ref, dst_ref=buf.at[me], send_sem=send_sems.at[k - 1],
                                              recv_sem=recv_sems.at[k - 1], device_id=peer, device_id_type=MESH)
            cp.start()
            copies.append(cp)
        for cp in copies:
            cp.wait()
        acc = buf[0]
        for d in range(1, 8):
            acc = acc + buf[d]
        o_ref[...] = acc

    return pl.pallas_call(
        body, name="allreduce_small",
        out_shape=jax.ShapeDtypeStruct((r, c), F32),
        scratch_shapes=[pltpu.VMEM((8, r, c), F32), pltpu.SemaphoreType.DMA((7,)), pltpu.SemaphoreType.DMA((7,))],
    )(x)


def reduce_scatter(g):
    _, r, c = g.shape
    rh = r // 2
    core = lax.axis_index("c")
    me = _chip_index()
    g = g.reshape(4, 2, rh, c)
    p32, p16 = add_pair(g, pair_exchange(g), core.astype(jnp.int32)[None])
    q16 = scatter_chips(p16)
    order = jnp.stack([me, me ^ 1, me ^ 2, me ^ 3]).astype(jnp.int32)
    mine = sum_slots(p32, q16, order)
    theirs = pair_share(mine)
    return jnp.where(core == 0, jnp.concatenate([mine, theirs]), jnp.concatenate([theirs, mine]))


PACK_W = 1024
PACK_ROWS = 1024

SHARDED = {"ffn1_w_gate_up": 1, "ffn1_w_down": 0, "ffn2_w_gate_up": 1, "ffn2_w_down": 0, "attn_w_in": 1,
           "attn_w_out": 0, "gdn_w_in": 1, "gdn_w_out": 0, "gdn_conv_w": 1}
REPLICATED = ["ffn1_norm", "mix_norm", "ffn2_norm", "attn_b_in", "attn_sinks", "attn_b_out", "gdn_A_log",
              "gdn_dt_bias", "gdn_norm_w", "final_norm"]
WEIGHTS = ["ffn1_norm", "ffn1_w_gate_up", "ffn1_w_down", "mix_norm", "ffn2_norm", "ffn2_w_gate_up", "ffn2_w_down",
           "attn_w_in", "attn_b_in", "attn_sinks", "attn_w_out", "attn_b_out", "gdn_w_in", "gdn_conv_w", "gdn_A_log",
           "gdn_dt_bias", "gdn_norm_w", "gdn_w_out", "final_norm"]


def _pack_rows(flats, dtype, width, row_multiple):
    flat = jnp.concatenate([f.astype(dtype).reshape(-1) for f in flats])
    per = width * row_multiple
    pad = (-flat.shape[0]) % per
    if pad:
        flat = jnp.concatenate([flat, jnp.zeros((pad,), dtype)])
    return flat.reshape(-1, width)


def _by_width(shapes):
    groups = {}
    for n, shape in shapes.items():
        groups.setdefault(shape[-1], []).append(n)
    return groups


def _gather_weights(shards, dtype, split=True):
    full = {}
    for width, names in _by_width({n: s.shape for n, s in shards.items()}).items():
        packed = jnp.concatenate([shards[n].astype(dtype).reshape(-1, width) for n in names], axis=0)
        got = allgather_chips(packed, split)
        off = 0
        for n in names:
            layers, a, _ = shards[n].shape
            full[n] = [jnp.concatenate([got[j, off + l * a:off + (l + 1) * a] for j in range(4)], axis=SHARDED[n])
                       for l in range(layers)]
            off += layers * a
    return full


def _scatter_grads(grads, shard_shapes):
    out = {}
    for _, names in _by_width(shard_shapes).items():
        slots = []
        for j in range(4):
            parts = []
            for n in names:
                w = shard_shapes[n][1 + SHARDED[n]]
                parts += [lax.slice_in_dim(gl, j * w, (j + 1) * w, axis=SHARDED[n]) for gl in grads[n]]
            slots.append(jnp.concatenate(parts, axis=0))
        red = reduce_scatter(jnp.stack(slots))
        off = 0
        for n in names:
            layers, a, _ = shard_shapes[n]
            out[n] = red[off:off + layers * a].reshape(shard_shapes[n])
            off += layers * a
    return out


def _small_pack(items):
    rows = []
    for a in items:
        f = a.astype(F32).reshape(-1)
        pad = (-f.shape[0]) % LANE
        rows.append(jnp.concatenate([f, jnp.zeros((pad,), F32)]) if pad else f)
    return _pack_rows(rows, F32, LANE, 8)


def _small_unpack(buf, shapes):
    flat = buf.reshape(-1)
    out, off = [], 0
    for shape in shapes:
        size = 1
        for s in shape:
            size *= s
        out.append(flat[off:off + size].reshape(shape))
        off += size + (-size) % LANE
    return out


def _rope_table(positions):
    t = positions.shape[0]
    inv_freq = ROPE_THETA ** (-jnp.arange(0, ROPE_DIM, 2, dtype=F32) / ROPE_DIM)
    ang = positions.astype(F32)[:, None] * inv_freq
    cos, sin = jnp.cos(ang), jnp.sin(ang)
    rest = ATTN_HEAD_DIM - ROPE_DIM
    zeros = lambda n: jnp.zeros((t, n), F32)
    c64 = jnp.concatenate([cos, cos, jnp.ones((t, rest), F32)], axis=1)
    s_up = jnp.concatenate([zeros(ROPE_DIM // 2), sin, zeros(rest)], axis=1)
    s_dn = jnp.concatenate([-sin, zeros(ROPE_DIM // 2 + rest)], axis=1)
    return jnp.concatenate([c64, c64, s_up, s_up, s_dn, s_dn], axis=1)


def _as2d(a):
    return a.reshape(-1, a.shape[-1]) if a.ndim > 1 else a.reshape(1, -1)


def kernel(x, positions, ffn1_norm, ffn1_w_gate_up, ffn1_w_down, mix_norm, ffn2_norm, ffn2_w_gate_up, ffn2_w_down, attn_w_in, attn_b_in, attn_sinks, attn_w_out, attn_b_out, gdn_w_in, gdn_conv_w, gdn_A_log, gdn_dt_bias, gdn_norm_w, gdn_w_out, final_norm, loss_target, m_ffn1_norm, m_ffn1_w_gate_up, m_ffn1_w_down, m_mix_norm, m_ffn2_norm, m_ffn2_w_gate_up, m_ffn2_w_down, m_attn_w_in, m_attn_b_in, m_attn_sinks, m_attn_w_out, m_attn_b_out, m_gdn_w_in, m_gdn_conv_w, m_gdn_A_log, m_gdn_dt_bias, m_gdn_norm_w, m_gdn_w_out, m_final_norm, v_ffn1_norm, v_ffn1_w_gate_up, v_ffn1_w_down, v_mix_norm, v_ffn2_norm, v_ffn2_w_gate_up, v_ffn2_w_down, v_attn_w_in, v_attn_b_in, v_attn_sinks, v_attn_w_out, v_attn_b_out, v_gdn_w_in, v_gdn_conv_w, v_gdn_A_log, v_gdn_dt_bias, v_gdn_norm_w, v_gdn_w_out, v_final_norm):
    given = dict(locals())
    w = {n: given[n] for n in WEIGHTS}
    d = D_MODEL
    h = x[0]
    target = loss_target[0]
    depth = ffn1_norm.shape[0]

    big = [n for n in SHARDED if n != "gdn_conv_w"]
    full = _gather_weights({n: w[n] for n in big}, BF16)
    conv_full = _gather_weights({"gdn_conv_w": gdn_conv_w}, F32, split=False)["gdn_conv_w"]
    tab = _rope_table(positions[0])
    zero_d = jnp.zeros((1, d), F32)

    def gdn_params(j):
        w_in = full["gdn_w_in"][j]
        w_cat = jnp.concatenate([w_in, jnp.zeros((d, GDN_PROJ_W - w_in.shape[1]), BF16)], axis=1)
        conv = jnp.concatenate([conv_full[j], jnp.zeros((8 - GDN_CONV, GDN_QKV_W), F32)], axis=0)
        lanes = lambda vec: jnp.concatenate([jnp.zeros((GC_G,), F32), vec, jnp.zeros((LANE - GC_G - GDN_HEADS,), F32)])
        par = jnp.concatenate([lanes(gdn_A_log[j])[None], lanes(gdn_dt_bias[j])[None], jnp.zeros((6, LANE), F32)], axis=0)
        return w_cat, conv, par

    saved = []
    for l in range(depth):
        j = l // 2
        rec = {"h1": h}
        h = ffn_fwd(h, ffn1_norm[l][None], full["ffn1_w_gate_up"][l], full["ffn1_w_down"][l])
        rec["h2"] = h
        if l % 2 == 0:
            p = norm_proj(h, mix_norm[l][None], full["attn_w_in"][j], attn_b_in[j][None])
            o = attn_fwd(p, tab, attn_sinks[j][None])
            h = linear_residual(h, o, full["attn_w_out"][j], attn_b_out[j][None])
            rec.update(p=p, o=o)
        else:
            w_cat, conv, par = gdn_params(j)
            proj = norm_proj(h, mix_norm[l][None], w_cat, jnp.zeros((1, GDN_PROJ_W), F32))
            qkv, gc = gdn_pre_fwd(proj, conv, par)
            on, st = gdn_chunk_fwd(qkv, gc, proj, gdn_norm_w[j][None])
            h = linear_residual(h, on, full["gdn_w_out"][j], zero_d)
            rec.update(proj=proj, qkv=qkv, gc=gc, on=on, st=st, w_cat=w_cat, conv=conv, par=par)
        rec["h3"] = h
        h = ffn_fwd(h, ffn2_norm[l][None], full["ffn2_w_gate_up"][l], full["ffn2_w_down"][l])
        saved.append(rec)

    dh, loss_tile, d_final = loss_head(h, final_norm[None], target)

    g = {n: [None] * w[n].shape[0] for n in WEIGHTS if n != "final_norm"}
    for l in reversed(range(depth)):
        j = l // 2
        rec = saved[l]

        def ffn_back(tag, h_in, dh):
            wgu, wd = full[tag + "_w_gate_up"][l], full[tag + "_w_down"][l]
            dh_new, hn, a, dgu, dn = ffn_bwd(h_in, w[tag + "_norm"][l][None], wgu, wd, dh)
            g[tag + "_w_gate_up"][l] = matmul_tn(hn, dgu)
            g[tag + "_w_down"][l] = matmul_tn(a, dh, 0.5)
            g[tag + "_norm"][l] = dn[0]
            return dh_new

        dh = ffn_back("ffn2", rec["h3"], dh)
        if l % 2 == 0:
            w_in, w_out = full["attn_w_in"][j], full["attn_w_out"][j]
            do, db_out = matmul_nt(dh, w_out)
            g["attn_w_out"][j] = matmul_tn(rec["o"], dh)
            g["attn_b_out"][j] = db_out[0]
            dq, dkvc, dkvp, dsink = attn_bwd(rec["p"], tab, attn_sinks[j][None], do)
            dkv = kv_combine(dkvc, dkvp)
            dh, hn, dn, cs_q, cs_kv = norm_proj_bwd(rec["h2"], mix_norm[l][None], dh, [dq, dkv],
                                                    [w_in[:, :ATTN_Q_W], w_in[:, ATTN_Q_W:]])
            g["attn_w_in"][j] = jnp.concatenate([matmul_tn(hn, dq), matmul_tn(hn, dkv)], axis=1)
            g["attn_b_in"][j] = jnp.concatenate([cs_q[0], cs_kv[0]])
            g["attn_sinks"][j] = dsink[0]
        else:
            w_cat, conv, par = rec["w_cat"], rec["conv"], rec["par"]
            d_on, _ = matmul_nt(dh, full["gdn_w_out"][j])
            g["gdn_w_out"][j] = matmul_tn(rec["on"], dh)
            dq, dk, dv, dz, dgc_heads, dnw = gdn_chunk_bwd(rec["qkv"], rec["gc"], rec["proj"], gdn_norm_w[j][None],
                                                           rec["st"], d_on)
            dy, dba, dpar = gdn_pre_bwd(rec["proj"], conv, par, dq, dk, dv, dgc_heads)
            dx, dconv = gdn_conv_bwd(rec["proj"], conv, dy)
            nz = GDN_QKV_W + GDN_HEADS * GDN_DK
            dh, hn, dn, _, _, _ = norm_proj_bwd(rec["h2"], mix_norm[l][None], dh, [dx, dz, dba],
                                                [w_cat[:, :GDN_QKV_W], w_cat[:, GDN_QKV_W:nz], w_cat[:, nz:]])
            g["gdn_w_in"][j] = jnp.concatenate(
                [matmul_tn(hn, dx), matmul_tn(hn, dz), matmul_tn(hn, dba)[:, :2 * GDN_HEADS]], axis=1)
            g["gdn_conv_w"][j] = dconv[:GDN_CONV]
            g["gdn_A_log"][j] = dpar[0, GDN_HEADS:2 * GDN_HEADS]
            g["gdn_dt_bias"][j] = dpar[1, GDN_HEADS:2 * GDN_HEADS]
            g["gdn_norm_w"][j] = dnw[0]
        g["mix_norm"][l] = dn[0]
        dh = ffn_back("ffn1", rec["h1"], dh)
    grad_x = dh[None]

    small_names = REPLICATED + ["gdn_conv_w"]
    local = {n: jnp.stack(g[n]) for n in small_names if n != "final_norm"}
    local["final_norm"] = d_final[0]
    small_shapes = [(1,)] + [local[n].shape for n in small_names]
    small = allreduce_small(_small_pack([loss_tile[0, 0:1]] + [local[n] for n in small_names]))
    small = _small_unpack(small, small_shapes)
    loss = small[0][0]
    grads = dict(zip(small_names, small[1:]))
    conv_cols = gdn_conv_w.shape[2]
    grads["gdn_conv_w"] = lax.dynamic_slice_in_dim(grads["gdn_conv_w"], _chip_index() * conv_cols, conv_cols, axis=2)
    grads.update(_scatter_grads({n: g[n] for n in big}, {n: w[n].shape for n in big}))

    delta, new_m, new_v = {}, {}, {}
    for n in SHARDED:
        dl, nm, nv = adamw(_as2d(w[n]), _as2d(grads[n]), _as2d(given["m_" + n]), _as2d(given["v_" + n]))
        delta[n], new_m[n], new_v[n] = dl.reshape(w[n].shape), nm.reshape(w[n].shape), nv.reshape(w[n].shape)
    shapes = [w[n].shape for n in REPLICATED]
    packed = [_small_pack([src[n] for n in REPLICATED]) for src in
              (w, grads, {n: given["m_" + n] for n in REPLICATED}, {n: given["v_" + n] for n in REPLICATED})]
    for dst, buf in zip((delta, new_m, new_v), adamw(*packed)):
        dst.update(zip(REPLICATED, _small_unpack(buf, shapes)))

    return (loss, grad_x, *[grads[n] for n in WEIGHTS], *[delta[n] for n in WEIGHTS],
            *[new_m[n] for n in WEIGHTS], *[new_v[n] for n in WEIGHTS])
```

```python
import functools

import jax
import jax.numpy as jnp
from jax import lax
from jax.experimental import pallas as pl
from jax.experimental.pallas import tpu as pltpu

F32 = jnp.float32
BF16 = jnp.bfloat16
HI = lax.Precision.HIGHEST
MESH = pl.DeviceIdType.MESH

D_MODEL = 1024
D_FF = 2816
DEPTH = 4
NORM_EPS = 1e-6
LANE = 128

ATTN_Q_HEADS = 16
ATTN_KV_HEADS = 4
ATTN_HEAD_DIM = 64
ATTN_GROUP = 4
ATTN_BLOCK = 128
ROPE_DIM = 16
ROPE_THETA = 500000.0
ATTN_Q_W = 1024
ATTN_KV_W = 256

GDN_HEADS = 8
GDN_DK = 128
GDN_CONV = 4
GDN_CHUNK = 64
GDN_QKV_W = 3072

ADAM_LR = 0.001
ADAM_B1 = 0.9
ADAM_B2 = 0.999
ADAM_EPS = 1e-08
ADAM_WD = 0.01
ADAM_STEP = 10

NEG = -1e30


def _dot(a, b, prec=None):
    return lax.dot_general(a, b, (((1,), (0,)), ((), ())), precision=prec, preferred_element_type=F32)


def _dot_nt(a, b, prec=None):
    return lax.dot_general(a, b, (((1,), (1,)), ((), ())), precision=prec, preferred_element_type=F32)


def _dot_tn(a, b, prec=None):
    return lax.dot_general(a, b, (((0,), (0,)), ((), ())), precision=prec, preferred_element_type=F32)


def _bdot(a, b):
    return _dot(a.astype(BF16), b.astype(BF16))


def _bdot_nt(a, b):
    return _dot_nt(a.astype(BF16), b.astype(BF16))


def _bdot_tn(a, b):
    return _dot_tn(a.astype(BF16), b.astype(BF16))


def _sigmoid(x):
    return 1.0 / (1.0 + jnp.exp(-x))


def _silu(x):
    return x * _sigmoid(x)


def _silu_grad(x):
    s = _sigmoid(x)
    return s * (1.0 + x * (1.0 - s))


def _rms(x, w):
    r = lax.rsqrt(jnp.mean(x * x, axis=-1, keepdims=True) + NORM_EPS)
    xhat = x * r
    return xhat * w, xhat, r


def _rms_bwd(dy, w, xhat, r):
    dxhat = dy * w
    dx = r * (dxhat - xhat * jnp.mean(dxhat * xhat, axis=-1, keepdims=True))
    dw = jnp.sum(dy * xhat, axis=0, keepdims=True)
    return dx, dw


def _arb(n):
    return pltpu.CompilerParams(dimension_semantics=("arbitrary",) * n)


def _tile(n, want):
    t = min(n, want)
    assert n % t == 0, (n, want)
    return t


def _iota2(shape, dim):
    return lax.broadcasted_iota(jnp.int32, shape, dim)


_NN = (((1,), (0,)), ((), ()))
_NT = (((1,), (1,)), ((), ()))
_TN = (((0,), (0,)), ((), ()))


def _raw1(a, b, dn):
    return lax.dot_general(a.astype(BF16), b.astype(BF16), dn, preferred_element_type=F32)


def _raw3(a, b, dn):
    ah, bh = a.astype(BF16), b.astype(BF16)
    al, bl = (a - ah.astype(F32)).astype(BF16), (b - bh.astype(F32)).astype(BF16)
    f = lambda x, y: lax.dot_general(x, y, dn, preferred_element_type=F32)
    return f(ah, bh) + f(ah, bl) + f(al, bh)


def _make_mm(raw):
    @jax.custom_vjp
    def mm(a, b):
        return raw(a, b, _NN)

    mm.defvjp(lambda a, b: (raw(a, b, _NN), (a, b)),
              lambda res, ct: (raw(ct, res[1], _NT), raw(res[0], ct, _TN)))

    @jax.custom_vjp
    def mm_nt(a, b):
        return raw(a, b, _NT)

    mm_nt.defvjp(lambda a, b: (raw(a, b, _NT), (a, b)),
                 lambda res, ct: (raw(ct, res[1], _NN), raw(ct, res[0], _TN)))
    return mm, mm_nt


_mm1, _mm1_nt = _make_mm(_raw1)
_mm3, _mm3_nt = _make_mm(_raw3)


def _eye(n):
    return (_iota2((n, n), 0) == _iota2((n, n), 1)).astype(F32)


def _each(f, *lists):
    return [f(*t) for t in zip(*lists)]


def _inv_newton(mats):
    eye = _eye(mats[0].shape[0])
    ps = [-a for a in mats]
    ms = [eye + p for p in ps]
    k = 1
    while 2 * k < GDN_CHUNK:
        ps = [_raw1(p, p, _NN) for p in ps]
        ms = _each(lambda m, p: m + _raw1(m, p, _NN), ms, ps)
        k *= 2
    rs = _each(lambda a, m: eye - m - _raw3(a, m, _NN), mats, ms)
    return _each(lambda m, r: m + _raw1(m, r, _NN), ms, rs)


@jax.custom_vjp
def _unit_lower_inv(mats):
    return _inv_newton(mats)


def _unit_lower_inv_fwd(mats):
    ms = _inv_newton(mats)
    return ms, ms


def _unit_lower_inv_bwd(ms, dms):
    ts = _each(lambda m, dm: _raw3(m, dm, _TN), ms, dms)
    return (_each(lambda t, m: -_raw3(t, m, _NT), ts, ms),)


_unit_lower_inv.defvjp(_unit_lower_inv_fwd, _unit_lower_inv_bwd)


def _gdn_local(q, k, v, dcb, dcb128, dlb128, bb128):
    n = q[0].shape[0]
    r, s = _iota2((n, n), 0), _iota2((n, n), 1)
    same = (r // GDN_CHUNK) == (s // GDN_CHUNK)
    causal, strict = same & (r >= s), same & (r > s)
    eye = _eye(n)
    decay_l = [jnp.exp(jnp.where(causal, d - d.T, NEG)) for d in dcb]
    kb = _each(lambda a, b: a * b, k, bb128)
    a = _each(lambda x, y, dl: jnp.where(strict, _mm1_nt(x, y) * dl, 0.0), kb, k, decay_l)
    m_off = [m - eye for m in _unit_lower_inv(a)]
    edc = [jnp.exp(d) for d in dcb128]
    rhs = _each(lambda vv, bb, kk, e: jnp.concatenate([vv * bb, kk * e], axis=1), v, bb128, kb, edc)
    sol = _each(lambda x, m: x + _mm3(m, x), rhs, m_off)
    aqk = _each(lambda x, y, dl: jnp.where(causal, _mm1_nt(x, y) * dl, 0.0), q, k, decay_l)
    q_dec = _each(lambda x, e: x * e, q, edc)
    k_dec = _each(lambda x, dl, dc: x * jnp.exp(dl - dc), k, dlb128, dcb128)
    return sol, aqk, q_dec, k_dec


def _gated_norm(o, z, nw):
    r = lax.rsqrt(jnp.mean(o * o, axis=-1, keepdims=True) + NORM_EPS)
    return o * r * nw * _silu(z)


def _gated_norm_bwd(dy, o, z, nw):
    r = lax.rsqrt(jnp.mean(o * o, axis=-1, keepdims=True) + NORM_EPS)
    xhat = o * r
    sz = _silu(z)
    dxhat = dy * nw * sz
    do = r * (dxhat - xhat * jnp.mean(dxhat * xhat, axis=-1, keepdims=True))
    dz = dy * xhat * nw * _silu_grad(z)
    dnw = jnp.sum(dy * xhat * sz, axis=0, keepdims=True)
    return do, dz, dnw


GDN_HEAD_GROUP = 8
GC_BETA, GC_G, GC_DECAY, GC_LAST =0, GDN_HEADS, 2 * GDN_HEADS, 3 * GDN_HEADS


def _gate_cols(gc, h, rows):
    lane = _iota2((rows, LANE), 1)
    col = lambda off: jnp.sum(jnp.where(lane == off + h, gc, 0.0), axis=-1, keepdims=True)
    return col(GC_BETA), col(GC_DECAY), col(GC_LAST)


def _gdn_local_args(q_ref, k_ref, v_ref, gc_ref, hp, n):
    gc = gc_ref[...]
    lanes = [slice(hh * LANE, (hh + 1) * LANE) for hh in range(hp)]
    cols = [_gate_cols(gc, pl.program_id(0) * hp + hh, n) for hh in range(hp)]
    bcast = lambda i, w: [jnp.broadcast_to(c[i], (n, w)) for c in cols]
    args = ([q_ref[:, ls] for ls in lanes], [k_ref[:, ls] for ls in lanes], [v_ref[:, ls] for ls in lanes],
            bcast(1, n), bcast(1, LANE), bcast(2, LANE), bcast(0, LANE))
    return args, [c[2] for c in cols], lanes


def gdn_chunk_fwd(qkv, gc, proj, norm_w):
    t = qkv.shape[0]
    c = GDN_CHUNK
    tc = _tile(t, 256)
    nsub = tc // c

    hp = GDN_HEAD_GROUP

    def body(q_ref, k_ref, v_ref, gc_ref, z_ref, nw_ref, on_ref, st_ref, s_ref):
        @pl.when(pl.program_id(1) == 0)
        def _():
            s_ref[...] = jnp.zeros_like(s_ref)

        args, dl, lanes = _gdn_local_args(q_ref, k_ref, v_ref, gc_ref, hp, tc)
        sol, aqk, q_dec, k_dec = _gdn_local(*args)
        u = [x[:, :GDN_DK] for x in sol]
        w = [x[:, GDN_DK:] for x in sol]
        s = [s_ref[hh] for hh in range(hp)]
        v_new, o_state = [[] for _ in range(hp)], [[] for _ in range(hp)]
        for j in range(nsub):
            sl = slice(j * c, (j + 1) * c)
            for hh in range(hp):
                st_ref[hh, j] = s[hh]
            vn = _each(lambda uu, ww, ss: uu[sl] - _bdot(ww[sl], ss), u, w, s)
            os_ = _each(lambda qq, ss: _bdot(qq[sl], ss), q_dec, s)
            s = _each(lambda ss, d, kk, vv: ss * jnp.exp(d[j * c:j * c + 1]) + _bdot_tn(kk[sl], vv), s, dl, k_dec, vn)
            for hh in range(hp):
                v_new[hh].append(vn[hh])
                o_state[hh].append(os_[hh])
        for hh in range(hp):
            s_ref[hh] = s[hh]
        cat = lambda xs: jnp.concatenate(xs, axis=0)
        o = _each(lambda os_, aa, vv: cat(os_) + _bdot(aa, cat(vv)), o_state, aqk, v_new)
        for hh, ls in enumerate(lanes):
            on_ref[:, ls] = _gated_norm(o[hh], z_ref[:, ls], nw_ref[...]).astype(BF16)

    col = lambda off: pl.BlockSpec((tc, hp * LANE), lambda g, i: (i, off // hp + g))
    return pl.pallas_call(
        body, name="gdn_chunk_fwd",
        grid=(GDN_HEADS // hp, t // tc),
        in_specs=[col(0), col(GDN_HEADS), col(2 * GDN_HEADS),
                  pl.BlockSpec((tc, LANE), lambda g, i: (i, 0)),
                  col(GDN_QKV_W // LANE),
                  pl.BlockSpec((1, LANE), lambda g, i: (0, 0))],
        out_specs=[col(0),
                   pl.BlockSpec((hp, nsub, GDN_DK, GDN_DK), lambda g, i: (g, i, 0, 0))],
        out_shape=[jax.ShapeDtypeStruct((t, GDN_HEADS * GDN_DK), BF16),
                   jax.ShapeDtypeStruct((GDN_HEADS, t // c, GDN_DK, GDN_DK), F32)],
        scratch_shapes=[pltpu.VMEM((hp, GDN_DK, GDN_DK), F32)],
        compiler_params=_arb(2),
    )(qkv, qkv, qkv, gc, proj, norm_w)


def gdn_chunk_bwd(qkv, gc, proj, norm_w, states, d_on):
    t = qkv.shape[0]
    c = GDN_CHUNK
    tc = _tile(t, 256)
    nsub = tc // c
    nblk = t // tc
    hp = GDN_HEAD_GROUP

    def body(q_ref, k_ref, v_ref, gc_ref, z_ref, nw_ref, st_ref, don_ref,
             dq_ref, dk_ref, dv_ref, dz_ref, dgc_ref, dnw_ref, ds_ref):
        @pl.when(pl.program_id(1) == 0)
        def _():
            ds_ref[...] = jnp.zeros_like(ds_ref)

        @pl.when((pl.program_id(1) == 0) & (pl.program_id(0) == 0))
        def _():
            dnw_ref[...] = jnp.zeros_like(dnw_ref)

        rows = [slice(j * c, (j + 1) * c) for j in range(nsub)]
        cat = lambda xs: jnp.concatenate(xs, axis=0)
        lsum = lambda x: jnp.sum(x, axis=-1, keepdims=True)
        lane = _iota2((tc, LANE), 1)
        row = _iota2((tc, 1), 0)
        heads = range(hp)
        args, dl, lanes = _gdn_local_args(q_ref, k_ref, v_ref, gc_ref, hp, tc)
        (sol, aqk, q_dec, k_dec), vjp = jax.vjp(_gdn_local, *args)
        u = [x[:, :GDN_DK] for x in sol]
        w = [x[:, GDN_DK:] for x in sol]
        states = [[st_ref[hh, j] for j in range(nsub)] for hh in heads]
        v_new = _each(lambda uu, ww, st: [uu[sl] - _bdot(ww[sl], s) for sl, s in zip(rows, st)], u, w, states)
        v_all = [cat(v) for v in v_new]
        o = _each(lambda qq, st, aa, vv: cat([_bdot(qq[sl], s) for sl, s in zip(rows, st)]) + _bdot(aa, vv),
                  q_dec, states, aqk, v_all)
        nw = nw_ref[...]
        gn = [_gated_norm_bwd(don_ref[:, ls], o[hh], z_ref[:, ls], nw) for hh, ls in enumerate(lanes)]
        do = [x[0] for x in gn]
        for hh, ls in enumerate(lanes):
            dnw_ref[...] += gn[hh][2]
            dz_ref[:, ls] = gn[hh][1].astype(BF16)
        d_aqk = _each(_bdot_nt, do, v_all)
        dv_o = _each(_bdot_tn, aqk, do)
        ds = [ds_ref[hh] for hh in heads]
        d_u, d_w, d_qdec, d_kdec, d_last = ([[None] * nsub for _ in heads] for _ in range(5))
        for j in reversed(range(nsub)):
            sl = rows[j]
            sj = [states[hh][j] for hh in heads]
            cd = [jnp.exp(d[j * c:j * c + 1]) for d in dl]
            du = _each(lambda dvo, kk, dd: dvo[sl] + _bdot(kk[sl], dd), dv_o, k_dec, ds)
            dqd = _each(lambda dd, s: _bdot_nt(dd[sl], s), do, sj)
            dkd = _each(lambda vv, dd: _bdot_nt(vv[j], dd), v_new, ds)
            dla = _each(lambda s, dd, cc: jnp.sum(lsum(s * dd), axis=0, keepdims=True) * cc, sj, ds, cd)
            dw = _each(lambda x, s: -_bdot_nt(x, s), du, sj)
            ds = _each(lambda qq, dd, cc, dsn, ww, x: _bdot_tn(qq[sl], dd[sl]) + cc * dsn - _bdot_tn(ww[sl], x),
                       q_dec, do, cd, ds, w, du)
            for hh in heads:
                d_u[hh][j], d_w[hh][j], d_qdec[hh][j], d_kdec[hh][j], d_last[hh][j] = du[hh], dw[hh], dqd[hh], dkd[hh], dla[hh]
        for hh in heads:
            ds_ref[hh] = ds[hh]
        d_sol = _each(lambda a, b: jnp.concatenate([cat(a), cat(b)], axis=1), d_u, d_w)
        dq, dk, dv, d_dcb, d_dcb128, d_dlb128, d_bb = vjp((d_sol, d_aqk, [cat(x) for x in d_qdec], [cat(x) for x in d_kdec]))
        dgc = jnp.zeros((tc, LANE), F32)
        for hh, ls in enumerate(lanes):
            h = pl.program_id(0) * hp + hh
            dq_ref[:, ls] = dq[hh]
            dk_ref[:, ls] = dk[hh]
            dv_ref[:, ls] = dv[hh]
            d_dl = lsum(d_dlb128[hh])
            for j in range(nsub):
                d_dl = d_dl + jnp.where(row == j * c, d_last[hh][j], 0.0)
            dgc = dgc + jnp.where(lane == GC_BETA + h, lsum(d_bb[hh]),
                                  jnp.where(lane == GC_DECAY + h, lsum(d_dcb[hh]) + lsum(d_dcb128[hh]),
                                            jnp.where(lane == GC_LAST + h, d_dl, 0.0)))
        dgc_ref[0] = dgc

    rev = lambda i: nblk - 1 - i
    col = lambda off: pl.BlockSpec((tc, hp * LANE), lambda g, i: (rev(i), off // hp + g))
    return pl.pallas_call(
        body, name="gdn_chunk_bwd",
        grid=(GDN_HEADS // hp, nblk),
        in_specs=[col(0), col(GDN_HEADS), col(2 * GDN_HEADS),
                  pl.BlockSpec((tc, LANE), lambda g, i: (rev(i), 0)),
                  col(GDN_QKV_W // LANE),
                  pl.BlockSpec((1, LANE), lambda g, i: (0, 0)),
                  pl.BlockSpec((hp, nsub, GDN_DK, GDN_DK), lambda g, i: (g, rev(i), 0, 0)),
                  col(0)],
        out_specs=[col(0), col(0), col(0), col(0),
                   pl.BlockSpec((1, tc, LANE), lambda g, i: (g, rev(i), 0)),
                   pl.BlockSpec((1, LANE), lambda g, i: (0, 0))],
        out_shape=[
            jax.ShapeDtypeStruct((t, GDN_HEADS * GDN_DK), F32),
            jax.ShapeDtypeStruct((t, GDN_HEADS * GDN_DK), F32),
            jax.ShapeDtypeStruct((t, GDN_HEADS * GDN_DK), F32),
            jax.ShapeDtypeStruct((t, GDN_HEADS * GDN_DK), BF16),
            jax.ShapeDtypeStruct((GDN_HEADS // hp, t, LANE), F32),
            jax.ShapeDtypeStruct((1, LANE), F32)],
        scratch_shapes=[pltpu.VMEM((hp, GDN_DK, GDN_DK), F32)],
        compiler_params=_arb(2),
    )(qkv, qkv, qkv, gc, proj, norm_w, states, d_on)


GDN_PROJ_W = GDN_QKV_W + 1024 + LANE
GDN_Q_SCALE = GDN_DK ** -0.5


def _shift_rows(ext, shift, lo, n):
    if shift == 0:
        return ext[lo:lo + n]
    return pltpu.roll(ext, shift, 0)[lo:lo + n]


def _conv_fwd(x, halo, w):
    n = x.shape[0]
    ext = jnp.concatenate([halo, x], axis=0)
    y = w[GDN_CONV - 1:GDN_CONV] * x
    for j in range(GDN_CONV - 1):
        y = y + w[j:j + 1] * _shift_rows(ext, GDN_CONV - 1 - j, 8, n)
    return y


def _chunk_masks(n):
    r, s = _iota2((n, n), 0), _iota2((n, n), 1)
    same = (r // GDN_CHUNK) == (s // GDN_CHUNK)
    return (same & (r >= s)).astype(F32), (same & (r <= s)).astype(F32), same.astype(F32)


def _softplus(x):
    return jnp.maximum(x, 0.0) + jnp.log(1.0 + jnp.exp(-jnp.abs(x)))


def _l2n(t):
    rs = lax.rsqrt(jnp.sum(t * t, axis=-1, keepdims=True) + NORM_EPS)
    return t * rs, rs


def gdn_pre_fwd(proj, conv_w, gate_par):
    t = proj.shape[0]
    tm = _tile(t, 256)

    def body(x_ref, halo_ref, ba_ref, w_ref, gp_ref, qkv_ref, gc_ref):
        i = pl.program_id(0)
        halo = jnp.where(i > 0, halo_ref[...], 0.0)
        y = _silu(_conv_fwd(x_ref[...], halo, w_ref[...]))
        for hh in range(2 * GDN_HEADS):
            sl = slice(hh * LANE, (hh + 1) * LANE)
            tn, _ = _l2n(y[:, sl])
            qkv_ref[:, sl] = tn * GDN_Q_SCALE if hh < GDN_HEADS else tn
        qkv_ref[:, 2 * GDN_HEADS * LANE:] = y[:, 2 * GDN_HEADS * LANE:]
        ba = ba_ref[...]
        lane = _iota2(ba.shape, 1)
        gp = gp_ref[...]
        is_a = (lane >= GC_G) & (lane < GC_G + GDN_HEADS)
        g = jnp.where(is_a, -jnp.exp(gp[0:1]) * _softplus(ba + gp[1:2]), 0.0)
        tri, _, same = _chunk_masks(tm)
        decay = pltpu.roll(_dot(tri, g, HI), GC_DECAY - GC_G, 1)
        last = pltpu.roll(_dot(same, g, HI), GC_LAST - GC_G, 1)
        gc_ref[...] = jnp.where(lane < GDN_HEADS, _sigmoid(ba), g) + decay + last

    return pl.pallas_call(
        body, name="gdn_pre_fwd",
        grid=(t // tm,),
        in_specs=[pl.BlockSpec((tm, GDN_QKV_W), lambda i: (i, 0)),
                  pl.BlockSpec((8, GDN_QKV_W), lambda i: (jnp.maximum(i * (tm // 8) - 1, 0), 0)),
                  pl.BlockSpec((tm, LANE), lambda i: (i, (GDN_QKV_W + 1024) // LANE)),
                  pl.BlockSpec((8, GDN_QKV_W), lambda i: (0, 0)),
                  pl.BlockSpec((8, LANE), lambda i: (0, 0))],
        out_specs=[pl.BlockSpec((tm, GDN_QKV_W), lambda i: (i, 0)),
                   pl.BlockSpec((tm, LANE), lambda i: (i, 0))],
        out_shape=[jax.ShapeDtypeStruct((t, GDN_QKV_W), F32), jax.ShapeDtypeStruct((t, LANE), F32)],
        compiler_params=_arb(1),
    )(proj, proj, proj, conv_w, gate_par)


def gdn_pre_bwd(proj, conv_w, gate_par, dq, dk, dv, dgc_heads):
    t = proj.shape[0]
    tm = _tile(t, 256)

    def body(x_ref, halo_ref, ba_ref, w_ref, gp_ref, dq_ref, dk_ref, dv_ref, dgc_ref, dy_ref, dba_ref, dgp_ref):
        i = pl.program_id(0)

        @pl.when(i == 0)
        def _():
            dgp_ref[...] = jnp.zeros_like(dgp_ref)

        halo = jnp.where(i > 0, halo_ref[...], 0.0)
        y = _conv_fwd(x_ref[...], halo, w_ref[...])
        for hh in range(3 * GDN_HEADS):
            sl = slice(hh * LANE, (hh + 1) * LANE)
            hsl = slice((hh % GDN_HEADS) * LANE, (hh % GDN_HEADS + 1) * LANE)
            yy = y[:, sl]
            if hh < 2 * GDN_HEADS:
                tn, rs = _l2n(_silu(yy))
                dtn = dq_ref[:, hsl] * GDN_Q_SCALE if hh < GDN_HEADS else dk_ref[:, hsl]
                dsil = rs * (dtn - tn * jnp.sum(dtn * tn, axis=-1, keepdims=True))
            else:
                dsil = dv_ref[:, hsl]
            dy_ref[:, sl] = dsil * _silu_grad(yy)
        dgc = dgc_ref[0]
        for hh in range(1, dgc_heads.shape[0]):
            dgc = dgc + dgc_ref[hh]
        ba = ba_ref[...]
        lane = _iota2(ba.shape, 1)
        _, tri_t, same = _chunk_masks(tm)
        d_decay = jnp.where((lane >= GC_DECAY) & (lane < GC_DECAY + GDN_HEADS), dgc, 0.0)
        d_last = jnp.where((lane >= GC_LAST) & (lane < GC_LAST + GDN_HEADS), dgc, 0.0)
        dgc = (jnp.where(lane < GDN_HEADS, dgc, 0.0) + pltpu.roll(_dot(tri_t, d_decay, HI), LANE - (GC_DECAY - GC_G), 1)
               + pltpu.roll(_dot(same, d_last, HI), LANE - (GC_LAST - GC_G), 1))
        gp = gp_ref[...]
        xg = ba + gp[1:2]
        ea = jnp.exp(gp[0:1])
        sp = _softplus(xg)
        sb = _sigmoid(ba)
        is_b = lane < GDN_HEADS
        is_a = (lane >= GDN_HEADS) & (lane < 2 * GDN_HEADS)
        d_pre = jnp.where(is_a, dgc * (-ea) * _sigmoid(xg), 0.0)
        dba_ref[...] = jnp.where(is_b, dgc * sb * (1.0 - sb), d_pre).astype(BF16)
        d_alog = jnp.sum(jnp.where(is_a, dgc * (-ea) * sp, 0.0), axis=0, keepdims=True)
        d_dtb = jnp.sum(d_pre, axis=0, keepdims=True)
        row = _iota2((8, LANE), 0)
        dgp_ref[...] += jnp.where(row == 0, d_alog, jnp.where(row == 1, d_dtb, 0.0))

    hspec = pl.BlockSpec((tm, GDN_HEADS * LANE), lambda i: (i, 0))
    return pl.pallas_call(
        body, name="gdn_pre_bwd",
        grid=(t // tm,),
        in_specs=[pl.BlockSpec((tm, GDN_QKV_W), lambda i: (i, 0)),
                  pl.BlockSpec((8, GDN_QKV_W), lambda i: (jnp.maximum(i * (tm // 8) - 1, 0), 0)),
                  pl.BlockSpec((tm, LANE), lambda i: (i, (GDN_QKV_W + 1024) // LANE)),
                  pl.BlockSpec((8, GDN_QKV_W), lambda i: (0, 0)),
                  pl.BlockSpec((8, LANE), lambda i: (0, 0)),
                  hspec, hspec, hspec,
                  pl.BlockSpec((dgc_heads.shape[0], tm, LANE), lambda i: (0, i, 0))],
        out_specs=[pl.BlockSpec((tm, GDN_QKV_W), lambda i: (i, 0)),
                   pl.BlockSpec((tm, LANE), lambda i: (i, 0)),
                   pl.BlockSpec((8, LANE), lambda i: (0, 0))],
        out_shape=[jax.ShapeDtypeStruct((t, GDN_QKV_W), F32), jax.ShapeDtypeStruct((t, LANE), BF16),
                   jax.ShapeDtypeStruct((8, LANE), F32)],
        compiler_params=_arb(1),
    )(proj, proj, proj, conv_w, gate_par, dq, dk, dv, dgc_heads)


def gdn_conv_bwd(proj, conv_w, dy):
    t = proj.shape[0]
    tm = _tile(t, 256)
    nblk = t // tm

    def body(x_ref, halo_ref, w_ref, dy_ref, dyn_ref, dx_ref, dw_ref):
        i = pl.program_id(0)

        @pl.when(i == 0)
        def _():
            dw_ref[...] = jnp.zeros_like(dw_ref)

        w = w_ref[...]
        dy = dy_ref[...]
        ext_dy = jnp.concatenate([dy, jnp.where(i < nblk - 1, dyn_ref[...], 0.0)], axis=0)
        ext_x = jnp.concatenate([jnp.where(i > 0, halo_ref[...], 0.0), x_ref[...]], axis=0)
        dx = w[GDN_CONV - 1:GDN_CONV] * dy
        rows = [jnp.sum(dy * x_ref[...], axis=0, keepdims=True)]
        for j in range(GDN_CONV - 1):
            sh = GDN_CONV - 1 - j
            dx = dx + w[j:j + 1] * _shift_rows(ext_dy, tm + 8 - sh, 0, tm)
            rows.insert(j, jnp.sum(dy * _shift_rows(ext_x, sh, 8, tm), axis=0, keepdims=True))
        dx_ref[...] = dx.astype(BF16)
        row = _iota2((8, GDN_QKV_W), 0)
        acc = jnp.zeros((8, GDN_QKV_W), F32)
        for j in range(GDN_CONV):
            acc = acc + jnp.where(row == j, rows[j], 0.0)
        dw_ref[...] += acc

    return pl.pallas_call(
        body, name="gdn_conv_bwd",
        grid=(nblk,),
        in_specs=[pl.BlockSpec((tm, GDN_QKV_W), lambda i: (i, 0)),
                  pl.BlockSpec((8, GDN_QKV_W), lambda i: (jnp.maximum(i * (tm // 8) - 1, 0), 0)),
                  pl.BlockSpec((8, GDN_QKV_W), lambda i: (0, 0)),
                  pl.BlockSpec((tm, GDN_QKV_W), lambda i: (i, 0)),
                  pl.BlockSpec((8, GDN_QKV_W), lambda i: (jnp.minimum((i + 1) * (tm // 8), t // 8 - 1), 0))],
        out_specs=[pl.BlockSpec((tm, GDN_QKV_W), lambda i: (i, 0)),
                   pl.BlockSpec((8, GDN_QKV_W), lambda i: (0, 0))],
        out_shape=[jax.ShapeDtypeStruct((t, GDN_QKV_W), BF16), jax.ShapeDtypeStruct((8, GDN_QKV_W), F32)],
        compiler_params=_arb(1),
    )(proj, proj, conv_w, dy, dy)


def _resident(w_hbm, w_vmem, sem):
    @pl.when(pl.program_id(0) == 0)
    def _():
        cp = pltpu.make_async_copy(w_hbm, w_vmem, sem)
        cp.start()
        cp.wait()


ANY = pl.BlockSpec(memory_space=pl.ANY)


def norm_proj(h, nw, w, bias):
    t, d = h.shape
    n = w.shape[1]
    tm = _tile(t, 256)
    nc = _tile(n, 1536) if n % 1536 == 0 else _tile(n, 1408)

    def body(h_ref, nw_ref, w_hbm, b_ref, o_ref, w_ref, sem):
        _resident(w_hbm, w_ref, sem)
        hn = _rms(h_ref[...], nw_ref[...])[0].astype(BF16)
        for c0 in range(0, n, nc):
            o_ref[:, c0:c0 + nc] = _dot(hn, w_ref[:, c0:c0 + nc]) + b_ref[:, c0:c0 + nc]

    return pl.pallas_call(
        body, name="norm_proj",
        grid=(t // tm,),
        in_specs=[pl.BlockSpec((tm, d), lambda i: (i, 0)), pl.BlockSpec((1, d), lambda i: (0, 0)), ANY,
                  pl.BlockSpec((1, n), lambda i: (0, 0))],
        out_specs=pl.BlockSpec((tm, n), lambda i: (i, 0)),
        out_shape=jax.ShapeDtypeStruct((t, n), F32),
        scratch_shapes=[pltpu.VMEM((d, n), BF16), pltpu.SemaphoreType.DMA],
        compiler_params=_arb(1),
    )(h, nw, w, bias)


def linear_residual(h, x, w, bias):
    t, d = h.shape
    k = x.shape[1]
    tm = _tile(t, 512)

    def body(h_ref, x_ref, w_hbm, b_ref, o_ref, w_ref, sem):
        _resident(w_hbm, w_ref, sem)
        o_ref[...] = h_ref[...] + _dot(x_ref[...], w_ref[...]) + b_ref[...]

    return pl.pallas_call(
        body, name="linear_residual",
        grid=(t // tm,),
        in_specs=[pl.BlockSpec((tm, d), lambda i: (i, 0)), pl.BlockSpec((tm, k), lambda i: (i, 0)), ANY,
                  pl.BlockSpec((1, d), lambda i: (0, 0))],
        out_specs=pl.BlockSpec((tm, d), lambda i: (i, 0)),
        out_shape=jax.ShapeDtypeStruct((t, d), F32),
        scratch_shapes=[pltpu.VMEM((k, d), BF16), pltpu.SemaphoreType.DMA],
        compiler_params=_arb(1),
    )(h, x, w, bias)


def matmul_nt(dy, w):
    t, d = dy.shape
    k = w.shape[0]
    tm = _tile(t, 512)

    def body(dy_ref, w_hbm, o_ref, cs_ref, w_ref, sem):
        _resident(w_hbm, w_ref, sem)

        @pl.when(pl.program_id(0) == 0)
        def _():
            cs_ref[...] = jnp.zeros_like(cs_ref)

        dy = dy_ref[...]
        cs_ref[...] += jnp.sum(dy, axis=0, keepdims=True)
        o_ref[...] = _dot_nt(dy.astype(BF16), w_ref[...])

    return pl.pallas_call(
        body, name="matmul_nt",
        grid=(t // tm,),
        in_specs=[pl.BlockSpec((tm, d), lambda i: (i, 0)), ANY],
        out_specs=[pl.BlockSpec((tm, k), lambda i: (i, 0)), pl.BlockSpec((1, d), lambda i: (0, 0))],
        out_shape=[jax.ShapeDtypeStruct((t, k), F32), jax.ShapeDtypeStruct((1, d), F32)],
        scratch_shapes=[pltpu.VMEM((k, d), BF16), pltpu.SemaphoreType.DMA],
        compiler_params=_arb(1),
    )(dy, w)


def norm_proj_bwd(h, nw, dh, dps, ws):
    t, d = h.shape
    np_ = len(dps)
    ns = [w.shape[1] for w in ws]
    tm = _tile(t, 256)

    def body(*refs):
        h_ref, nw_ref, dh_ref = refs[:3]
        dp_refs = refs[3:3 + np_]
        w_hbms = refs[3 + np_:3 + 2 * np_]
        o_ref, hn_ref, dnw_ref = refs[3 + 2 * np_:6 + 2 * np_]
        cs_refs = refs[6 + 2 * np_:6 + 3 * np_]
        w_refs = refs[6 + 3 * np_:6 + 4 * np_]
        sem = refs[6 + 4 * np_]
        for a, b in zip(w_hbms, w_refs):
            _resident(a, b, sem)

        @pl.when(pl.program_id(0) == 0)
        def _():
            dnw_ref[...] = jnp.zeros_like(dnw_ref)
            for c in cs_refs:
                c[...] = jnp.zeros_like(c)

        nw = nw_ref[...]
        hn, xhat, r = _rms(h_ref[...], nw)
        hn_ref[...] = hn.astype(BF16)
        dhn = jnp.zeros((tm, d), F32)
        for dp_ref, w_ref, cs_ref in zip(dp_refs, w_refs, cs_refs):
            dp = dp_ref[...]
            cs_ref[...] += jnp.sum(dp.astype(F32), axis=0, keepdims=True)
            dhn = dhn + _dot_nt(dp, w_ref[...])
        dx, dnw = _rms_bwd(dhn, nw, xhat, r)
        dnw_ref[...] += dnw
        o_ref[...] = dh_ref[...] + dx

    row = pl.BlockSpec((tm, d), lambda i: (i, 0))
    vec = pl.BlockSpec((1, d), lambda i: (0, 0))
    return pl.pallas_call(
        body, name="norm_proj_bwd",
        grid=(t // tm,),
        in_specs=[row, vec, row] + [pl.BlockSpec((tm, n), lambda i: (i, 0)) for n in ns] + [ANY] * np_,
        out_specs=[row, row, vec] + [pl.BlockSpec((1, n), lambda i: (0, 0)) for n in ns],
        out_shape=[jax.ShapeDtypeStruct((t, d), F32), jax.ShapeDtypeStruct((t, d), BF16),
                   jax.ShapeDtypeStruct((1, d), F32)] + [jax.ShapeDtypeStruct((1, n), F32) for n in ns],
        scratch_shapes=[pltpu.VMEM((d, n), BF16) for n in ns] + [pltpu.SemaphoreType.DMA],
        compiler_params=_arb(1),
    )(h, nw, dh, *dps, *ws)


def matmul_tn(x, y, scale=1.0):
    t, k = x.shape
    n = y.shape[1]
    tk = _tile(k, 1024) if k % 1024 == 0 else _tile(k, 1408)
    tn = n if n <= 1536 else (1024 if n % 1024 == 0 else 1408)
    assert n % tn == 0
    tt = _tile(t, 1024)
    nt = t // tt

    def body(x_ref, y_ref, o_ref):
        @pl.when(pl.program_id(2) == 0)
        def _():
            o_ref[...] = jnp.zeros_like(o_ref)

        yv = y_ref[...]
        if scale != 1.0:
            yv = yv * scale
        o_ref[...] += _dot_tn(x_ref[...].astype(BF16), yv.astype(BF16))

    return pl.pallas_call(
        body, name="matmul_tn",
        grid=(k // tk, n // tn, nt),
        in_specs=[pl.BlockSpec((tt, tk), lambda i, j, s: (s, i)), pl.BlockSpec((tt, tn), lambda i, j, s: (s, j))],
        out_specs=pl.BlockSpec((tk, tn), lambda i, j, s: (i, j)),
        out_shape=jax.ShapeDtypeStruct((k, n), F32),
        compiler_params=_arb(3),
    )(x, y)


FFN_CHUNKS = 2


def ffn_fwd(h, nw, wgu, wd):
    t, d = h.shape
    f = wd.shape[0]
    fc = f // FFN_CHUNKS
    tm = _tile(t, 512)

    def body(h_ref, nw_ref, wgu_hbm, wd_hbm, o_ref, gu_ref, a_ref, wgu_ref, wd_ref, sem):
        _resident(wgu_hbm, wgu_ref, sem)
        _resident(wd_hbm, wd_ref, sem)
        x = h_ref[...]
        hn = _rms(x, nw_ref[...])[0].astype(BF16)
        acc = jnp.zeros((tm, d), F32)
        for c in range(FFN_CHUNKS):
            gs, us = slice(c * fc, (c + 1) * fc), slice(f + c * fc, f + (c + 1) * fc)
            g = _dot(hn, wgu_ref[:, gs])
            u = _dot(hn, wgu_ref[:, us])
            a = (_silu(g) * u).astype(BF16)
            gu_ref[:, gs] = g.astype(BF16)
            gu_ref[:, us] = u.astype(BF16)
            a_ref[:, gs] = a
            acc = acc + _dot(a, wd_ref[gs, :])
        o_ref[...] = x + 0.5 * acc

    row = lambda w: pl.BlockSpec((tm, w), lambda i: (i, 0))
    return pl.pallas_call(
        body, name="ffn_fwd",
        grid=(t // tm,),
        in_specs=[row(d), pl.BlockSpec((1, d), lambda i: (0, 0)), ANY, ANY],
        out_specs=[row(d), row(2 * f), row(f)],
        out_shape=[jax.ShapeDtypeStruct((t, d), F32), jax.ShapeDtypeStruct((t, 2 * f), BF16),
                   jax.ShapeDtypeStruct((t, f), BF16)],
        scratch_shapes=[pltpu.VMEM((d, 2 * f), BF16), pltpu.VMEM((f, d), BF16), pltpu.SemaphoreType.DMA],
        compiler_params=_arb(1),
    )(h, nw, wgu, wd)


def ffn_bwd(h, nw, wgu, wd, gu, dh):
    t, d = h.shape
    f = wd.shape[0]
    fc = f // FFN_CHUNKS
    tm = _tile(t, 256)

    def body(h_ref, nw_ref, wgu_hbm, wd_hbm, gu_ref, dh_ref, o_ref, hn_ref, dgu_ref, dnw_ref, wgu_ref, wd_ref, sem):
        _resident(wgu_hbm, wgu_ref, sem)
        _resident(wd_hbm, wd_ref, sem)

        @pl.when(pl.program_id(0) == 0)
        def _():
            dnw_ref[...] = jnp.zeros_like(dnw_ref)

        nw = nw_ref[...]
        hn32, xhat, r = _rms(h_ref[...], nw)
        hn_ref[...] = hn32.astype(BF16)
        dh = dh_ref[...]
        dout = (0.5 * dh).astype(BF16)
        dhn = jnp.zeros((tm, d), F32)
        for c in range(FFN_CHUNKS):
            gs, us = slice(c * fc, (c + 1) * fc), slice(f + c * fc, f + (c + 1) * fc)
            g = gu_ref[:, gs].astype(F32)
            u = gu_ref[:, us].astype(F32)
            sg = _sigmoid(g)
            sil = g * sg
            da = _dot_nt(dout, wd_ref[gs, :])
            dg = (da * u * (sg * (1.0 + g * (1.0 - sg)))).astype(BF16)
            du = (da * sil).astype(BF16)
            dgu_ref[:, gs] = dg
            dgu_ref[:, us] = du
            dhn = dhn + _dot_nt(dg, wgu_ref[:, gs]) + _dot_nt(du, wgu_ref[:, us])
        dx, dnw = _rms_bwd(dhn, nw, xhat, r)
        dnw_ref[...] += dnw
        o_ref[...] = dh + dx

    row = pl.BlockSpec((tm, d), lambda i: (i, 0))
    vec = pl.BlockSpec((1, d), lambda i: (0, 0))
    return pl.pallas_call(
        body, name="ffn_bwd",
        grid=(t // tm,),
        in_specs=[row, vec, ANY, ANY, pl.BlockSpec((tm, 2 * f), lambda i: (i, 0)), row],
        out_specs=[row, row, pl.BlockSpec((tm, 2 * f), lambda i: (i, 0)), vec],
        out_shape=[jax.ShapeDtypeStruct((t, d), F32), jax.ShapeDtypeStruct((t, d), BF16),
                   jax.ShapeDtypeStruct((t, 2 * f), BF16), jax.ShapeDtypeStruct((1, d), F32)],
        scratch_shapes=[pltpu.VMEM((d, 2 * f), BF16), pltpu.VMEM((f, d), BF16), pltpu.SemaphoreType.DMA],
        compiler_params=_arb(1),
    )(h, nw, wgu, wd, gu, dh)


def loss_head(h, nw, target):
    t, d = h.shape
    tm = _tile(t, 512)

    def body(h_ref, nw_ref, tg_ref, dh_ref, loss_ref, dnw_ref):
        @pl.when(pl.program_id(0) == 0)
        def _():
            loss_ref[...] = jnp.zeros_like(loss_ref)
            dnw_ref[...] = jnp.zeros_like(dnw_ref)

        nw = nw_ref[...]
        y, xhat, r = _rms(h_ref[...], nw)
        e = y - tg_ref[...]
        loss_ref[...] += 0.5 * jnp.sum(jnp.mean(e * e, axis=-1, keepdims=True), axis=0, keepdims=True)
        dx, dnw = _rms_bwd(e * (1.0 / d), nw, xhat, r)
        dnw_ref[...] += dnw
        dh_ref[...] = dx

    row = pl.BlockSpec((tm, d), lambda i: (i, 0))
    vec = pl.BlockSpec((1, d), lambda i: (0, 0))
    return pl.pallas_call(
        body, name="loss_head",
        grid=(t // tm,),
        in_specs=[row, vec, row],
        out_specs=[row, pl.BlockSpec((8, LANE), lambda i: (0, 0)), vec],
        out_shape=[jax.ShapeDtypeStruct((t, d), F32), jax.ShapeDtypeStruct((8, LANE), F32),
                   jax.ShapeDtypeStruct((1, d), F32)],
        compiler_params=_arb(1),
    )(h, nw, target)


def adamw(w, g, m, v):
    r, c = w.shape
    tr = r
    while tr * c * 4 > (1 << 20) and tr % 16 == 0:
        tr //= 2

    def body(w_ref, g_ref, m_ref, v_ref, d_ref, nm_ref, nv_ref):
        g = g_ref[...]
        m = ADAM_B1 * m_ref[...] + (1.0 - ADAM_B1) * g
        v = ADAM_B2 * v_ref[...] + (1.0 - ADAM_B2) * (g * g)
        m_hat = m / (1.0 - ADAM_B1 ** ADAM_STEP)
        v_hat = v / (1.0 - ADAM_B2 ** ADAM_STEP)
        d_ref[...] = -ADAM_LR * (m_hat / (jnp.sqrt(v_hat) + ADAM_EPS) + ADAM_WD * w_ref[...])
        nm_ref[...] = m
        nv_ref[...] = v

    blk = pl.BlockSpec((tr, c), lambda i: (i, 0))
    return pl.pallas_call(
        body, name="adamw",
        grid=(r // tr,),
        in_specs=[blk] * 4, out_specs=[blk] * 3,
        out_shape=[jax.ShapeDtypeStruct((r, c), F32)] * 3,
        compiler_params=_arb(1),
    )(w, g, m, v)


ATTN_P_W = ATTN_Q_W + 2 * ATTN_KV_W
ATTN_SCALE = ATTN_HEAD_DIM ** -0.5


def _rope_group(t, tab, sign):
    return (t * tab[:, 0:LANE] + sign * pltpu.roll(t, 8, 1) * tab[:, LANE:2 * LANE]
            + sign * pltpu.roll(t, LANE - 8, 1) * tab[:, 2 * LANE:3 * LANE])


def _rope(t, tab, sign=1.0):
    return jnp.concatenate([_rope_group(t[:, s:s + LANE], tab, sign) for s in range(0, t.shape[1], LANE)], axis=1)


def _attn_heads(qs, ks, vs, sinks, first):
    b = ATTN_BLOCK
    rows = ATTN_GROUP * b
    qi = _iota2((rows, 2 * b), 0) % b
    kj = _iota2((rows, 2 * b), 1)
    rel = qi + b - kj
    valid = (rel >= 0) & (rel < b) & ((kj >= b) | jnp.logical_not(first))
    ss = _each(lambda q, k: jnp.where(valid, _mm1_nt(q, k) * ATTN_SCALE, NEG), qs, ks)
    ms = _each(lambda s, sink: lax.stop_gradient(jnp.maximum(jnp.max(s, axis=-1, keepdims=True), sink)), ss, sinks)
    ps = _each(lambda s, m: jnp.exp(s - m), ss, ms)
    dens = _each(lambda p, sink, m: jnp.sum(p, axis=-1, keepdims=True) + jnp.exp(sink - m), ps, sinks, ms)
    return _each(lambda p, den, v: _mm1(p / den, v), ps, dens, vs)


def _attn_prepare(p_ref, kvp_ref, tab_ref, tabp_ref, sink_ref):
    b = ATTN_BLOCK
    hd = ATTN_HEAD_DIM
    tab, tabp = tab_ref[...], tabp_ref[...]
    q = _rope(p_ref[:, 0:ATTN_Q_W], tab)
    kc = _rope(p_ref[:, ATTN_Q_W:ATTN_Q_W + ATTN_KV_W], tab)
    kp = _rope(kvp_ref[:, 0:ATTN_KV_W], tabp)
    vc = p_ref[:, ATTN_Q_W + ATTN_KV_W:ATTN_P_W]
    vp = kvp_ref[:, ATTN_KV_W:2 * ATTN_KV_W]
    sk = sink_ref[...]
    qs, ks, vs, sinks = [], [], [], []
    for h in range(ATTN_KV_HEADS):
        heads = [ATTN_GROUP * h + g for g in range(ATTN_GROUP)]
        qs.append(jnp.concatenate([q[:, i * hd:(i + 1) * hd] for i in heads], axis=0))
        ks.append(jnp.concatenate([kp[:, h * hd:(h + 1) * hd], kc[:, h * hd:(h + 1) * hd]], axis=0))
        vs.append(jnp.concatenate([vp[:, h * hd:(h + 1) * hd], vc[:, h * hd:(h + 1) * hd]], axis=0))
        sinks.append(jnp.concatenate([jnp.broadcast_to(sk[:, i:i + 1], (b, 1)) for i in heads], axis=0))
    return qs, ks, vs, sinks


def _unstack_heads(xs):
    b = ATTN_BLOCK
    return jnp.concatenate([x[g * b:(g + 1) * b] for x in xs for g in range(ATTN_GROUP)], axis=1)


def _attn_specs(nb):
    b = ATTN_BLOCK
    prev = lambda n: jnp.maximum(n - 1, 0)
    return [pl.BlockSpec((b, ATTN_P_W), lambda n: (n, 0)),
            pl.BlockSpec((b, 2 * ATTN_KV_W), lambda n: (prev(n), ATTN_Q_W // (2 * ATTN_KV_W))),
            pl.BlockSpec((b, 3 * LANE), lambda n: (n, 0)),
            pl.BlockSpec((b, 3 * LANE), lambda n: (prev(n), 0)),
            pl.BlockSpec((1, ATTN_Q_HEADS), lambda n: (0, 0))]


def attn_fwd(p, tab, sinks):
    t = p.shape[0]
    nb = t // ATTN_BLOCK

    def body(p_ref, kvp_ref, tab_ref, tabp_ref, sink_ref, o_ref):
        qs, ks, vs, sk = _attn_prepare(p_ref, kvp_ref, tab_ref, tabp_ref, sink_ref)
        os_ = _attn_heads(qs, ks, vs, sk, pl.program_id(0) == 0)
        o_ref[...] = _unstack_heads(os_).astype(BF16)

    return pl.pallas_call(
        body, name="attn_fwd",
        grid=(nb,),
        in_specs=_attn_specs(nb),
        out_specs=pl.BlockSpec((ATTN_BLOCK, ATTN_Q_W), lambda n: (n, 0)),
        out_shape=jax.ShapeDtypeStruct((t, ATTN_Q_W), BF16),
        compiler_params=_arb(1),
    )(p, p, tab, tab, sinks)


def attn_bwd(p, tab, sinks, do):
    t = p.shape[0]
    b = ATTN_BLOCK
    hd = ATTN_HEAD_DIM
    nb = t // b

    def body(p_ref, kvp_ref, tab_ref, tabp_ref, sink_ref, do_ref, dq_ref, dkvc_ref, dkvp_ref, dsink_ref):
        n = pl.program_id(0)

        @pl.when(n == 0)
        def _():
            dsink_ref[...] = jnp.zeros_like(dsink_ref)

        qs, ks, vs, sk = _attn_prepare(p_ref, kvp_ref, tab_ref, tabp_ref, sink_ref)
        first = n == 0
        _, vjp = jax.vjp(lambda a, bb, c, d: _attn_heads(a, bb, c, d, first), qs, ks, vs, sk)
        do = do_ref[...]
        dos = [jnp.concatenate([do[:, i * hd:(i + 1) * hd] for i in range(ATTN_GROUP * h, ATTN_GROUP * (h + 1))], axis=0)
               for h in range(ATTN_KV_HEADS)]
        dqs, dks, dvs, dsk = vjp(dos)
        tab, tabp = tab_ref[...], tabp_ref[...]
        dq_ref[...] = _rope(_unstack_heads([x.astype(F32) for x in dqs]), tab, -1.0).astype(BF16)
        dkc = jnp.concatenate([x.astype(F32)[b:] for x in dks], axis=1)
        dkp = jnp.concatenate([x.astype(F32)[:b] for x in dks], axis=1)
        dkvc_ref[:, 0:ATTN_KV_W] = _rope(dkc, tab, -1.0)
        dkvp_ref[:, 0:ATTN_KV_W] = _rope(dkp, tabp, -1.0)
        dkvc_ref[:, ATTN_KV_W:] = jnp.concatenate([x.astype(F32)[b:] for x in dvs], axis=1)
        dkvp_ref[:, ATTN_KV_W:] = jnp.concatenate([x.astype(F32)[:b] for x in dvs], axis=1)
        parts = [jnp.sum(d[g * b:(g + 1) * b], axis=0, keepdims=True) for d in dsk for g in range(ATTN_GROUP)]
        dsink_ref[...] += jnp.concatenate(parts, axis=1)

    blk = lambda w: pl.BlockSpec((b, w), lambda n: (n, 0))
    return pl.pallas_call(
        body, name="attn_bwd",
        grid=(nb,),
        in_specs=_attn_specs(nb) + [blk(ATTN_Q_W)],
        out_specs=[blk(ATTN_Q_W), blk(2 * ATTN_KV_W), blk(2 * ATTN_KV_W),
                   pl.BlockSpec((1, ATTN_Q_HEADS), lambda n: (0, 0))],
        out_shape=[jax.ShapeDtypeStruct((t, ATTN_Q_W), BF16), jax.ShapeDtypeStruct((t, 2 * ATTN_KV_W), F32),
                   jax.ShapeDtypeStruct((t, 2 * ATTN_KV_W), F32), jax.ShapeDtypeStruct((1, ATTN_Q_HEADS), F32)],
        compiler_params=_arb(1),
    )(p, p, tab, tab, sinks, do)


def kv_combine(dkvc, dkvp):
    t, w = dkvc.shape
    b = ATTN_BLOCK
    nb = t // b

    def body(c_ref, p_ref, o_ref):
        nxt = jnp.where(pl.program_id(0) < nb - 1, p_ref[...], 0.0)
        o_ref[...] = (c_ref[...] + nxt).astype(BF16)

    return pl.pallas_call(
        body, name="kv_combine",
        grid=(nb,),
        in_specs=[pl.BlockSpec((b, w), lambda n: (n, 0)),
                  pl.BlockSpec((b, w), lambda n: (jnp.minimum(n + 1, nb - 1), 0))],
        out_specs=pl.BlockSpec((b, w), lambda n: (n, 0)),
        out_shape=jax.ShapeDtypeStruct((t, w), BF16),
        compiler_params=_arb(1),
    )(dkvc, dkvp)


def _me():
    return lax.axis_index("x"), lax.axis_index("y"), lax.axis_index("c")


def _flip(v, bit):
    return 1 - v if bit else v


def _chip_index():
    return 2 * lax.axis_index("x") + lax.axis_index("y")


def allgather_chips(x, split):
    r, c = x.shape
    rh = r // 2 if split else r

    def body(x_ref, o_ref, send_sems, recv_sems):
        mx, my, mc = _me()
        me = 2 * mx + my
        rows = pl.ds(pl.multiple_of(mc * rh, 16), rh) if split else pl.ds(0, r)
        fetched, passed = [], []
        for k in (1, 2, 3):
            peer = (_flip(mx, k >> 1), _flip(my, k & 1), mc)
            cp = pltpu.make_async_remote_copy(src_ref=x_ref.at[rows], dst_ref=o_ref.at[me, rows],
                                              send_sem=send_sems.at[k - 1], recv_sem=recv_sems.at[k - 1],
                                              device_id=peer, device_id_type=MESH)
            cp.start()
            fetched.append(cp)
        if split:
            for k in (1, 2, 3):
                fetched[k - 1].wait_recv()
                theirs = o_ref.at[2 * _flip(mx, k >> 1) + _flip(my, k & 1), rows]
                cp = pltpu.make_async_remote_copy(src_ref=theirs, dst_ref=theirs, send_sem=send_sems.at[2 + k],
                                                  recv_sem=recv_sems.at[2 + k], device_id=(mx, my, 1 - mc),
                                                  device_id_type=MESH)
                cp.start()
                passed.append(cp)
            for cp in fetched:
                cp.wait_send()
            for cp in passed:
                cp.wait()
        else:
            for cp in fetched:
                cp.wait()

    got = pl.pallas_call(
        body, name="allgather_chips",
        in_specs=[ANY], out_specs=ANY,
        out_shape=jax.ShapeDtypeStruct((4, r, c), x.dtype),
        scratch_shapes=[pltpu.SemaphoreType.DMA((6,)), pltpu.SemaphoreType.DMA((6,))],
    )(x)
    return lax.dynamic_update_slice(got, x[None], (_chip_index(), 0, 0))


def pair_exchange(g):
    _, _, r, c = g.shape

    def body(g_ref, got_ref, send_sems, recv_sems):
        mx, my, mc = _me()
        sib = (mx, my, 1 - mc)
        copies = []
        for j in range(4):
            cp = pltpu.make_async_remote_copy(src_ref=g_ref.at[j, 1 - mc], dst_ref=got_ref.at[j], send_sem=send_sems.at[j],
                                              recv_sem=recv_sems.at[j], device_id=sib, device_id_type=MESH)
            cp.start()
            copies.append(cp)
        for cp in copies:
            cp.wait()

    return pl.pallas_call(
        body, name="pair_exchange",
        in_specs=[ANY], out_specs=ANY,
        out_shape=jax.ShapeDtypeStruct((4, r, c), g.dtype),
        scratch_shapes=[pltpu.SemaphoreType.DMA((4,))] * 2,
    )(g)


def scatter_chips(p):
    _, r, c = p.shape

    def body(p_ref, o_ref, send_sems, recv_sems):
        mx, my, mc = _me()
        me = 2 * mx + my
        copies = []
        for k in (1, 2, 3):
            px, py = _flip(mx, k >> 1), _flip(my, k & 1)
            cp = pltpu.make_async_remote_copy(src_ref=p_ref.at[2 * px + py], dst_ref=o_ref.at[me],
                                              send_sem=send_sems.at[k - 1], recv_sem=recv_sems.at[k - 1],
                                              device_id=(px, py, mc), device_id_type=MESH)
            cp.start()
            copies.append(cp)
        for cp in copies:
            cp.wait()

    return pl.pallas_call(
        body, name="scatter_chips",
        in_specs=[ANY], out_specs=ANY,
        out_shape=jax.ShapeDtypeStruct((4, r, c), p.dtype),
        scratch_shapes=[pltpu.SemaphoreType.DMA((3,)), pltpu.SemaphoreType.DMA((3,))],
    )(p)


def pair_share(o):
    r, c = o.shape

    def body(o_ref, out_ref, send_sem, recv_sem):
        mx, my, mc = _me()
        cp = pltpu.make_async_remote_copy(src_ref=o_ref, dst_ref=out_ref, send_sem=send_sem, recv_sem=recv_sem,
                                          device_id=(mx, my, 1 - mc), device_id_type=MESH)
        cp.start()
        cp.wait()

    return pl.pallas_call(
        body, name="pair_share",
        in_specs=[ANY], out_specs=ANY,
        out_shape=jax.ShapeDtypeStruct((r, c), o.dtype),
        scratch_shapes=[pltpu.SemaphoreType.DMA, pltpu.SemaphoreType.DMA],
    )(o)


def _row_tile(r, c):
    tr = 8
    while r % (2 * tr) == 0 and 2 * tr * c * 4 <= (2 << 20):
        tr *= 2
    return tr


def add_pair(g, got, core):
    _, _, r, c = g.shape
    tr = _row_tile(r, c)

    def body(core_ref, g_ref, got_ref, p32_ref, p16_ref):
        s = g_ref[...] + got_ref[...]
        p32_ref[...] = s
        p16_ref[...] = s.astype(BF16)

    blk = pl.BlockSpec((None, tr, c), lambda j, i, core_ref: (j, i, 0))
    return pl.pallas_call(
        body, name="add_pair",
        grid_spec=pltpu.PrefetchScalarGridSpec(
            num_scalar_prefetch=1, grid=(4, r // tr),
            in_specs=[pl.BlockSpec((None, None, tr, c), lambda j, i, core_ref: (j, core_ref[0], i, 0)), blk],
            out_specs=[blk, blk]),
        out_shape=[jax.ShapeDtypeStruct((4, r, c), F32), jax.ShapeDtypeStruct((4, r, c), BF16)],
        compiler_params=_arb(2),
    )(core, g, got)


def sum_slots(p32, q16, order):
    _, r, c = p32.shape
    tr = _row_tile(r, c)

    def body(order_ref, own_ref, a_ref, b_ref, c_ref, o_ref):
        o_ref[...] = ((own_ref[...] + a_ref[...].astype(F32)) + b_ref[...].astype(F32)) + c_ref[...].astype(F32)

    slot = lambda k: pl.BlockSpec((None, tr, c), functools.partial(lambda k, i, order_ref: (order_ref[k], i, 0), k))
    return pl.pallas_call(
        body, name="sum_slots",
        grid_spec=pltpu.PrefetchScalarGridSpec(
            num_scalar_prefetch=1, grid=(r // tr,),
            in_specs=[slot(0), slot(1), slot(2), slot(3)],
            out_specs=pl.BlockSpec((tr, c), lambda i, order_ref: (i, 0))),
        out_shape=jax.ShapeDtypeStruct((r, c), F32),
        compiler_params=_arb(1),
    )(order, p32, q16, q16, q16)


def allreduce_small(x):
    r, c = x.shape

    def body(x_ref, o_ref, buf, send_sems, recv_sems):
        mx, my, mc = _me()
        me = 4 * mx + 2 * my + mc
        buf[pl.ds(me, 1)] = x_ref[...][None]
        copies = []
        for k in range(1, 8):
            peer = (_flip(mx, k >> 2), _flip(my, (k >> 1) & 1), _flip(mc, k & 1))
            cp = pltpu.make_async_remote_copy(src_ref=x_ref, dst_ref=buf.at[me], send_sem=send_sems.at[k - 1],
                                              recv_sem=recv_sems.at[k - 1], device_id=peer, device_id_type=MESH)
            cp.start()
            copies.append(cp)
        for cp in copies:
            cp.wait()
        acc = buf[0]
        for d in range(1, 8):
            acc = acc + buf[d]
        o_ref[...] = acc

    return pl.pallas_call(
        body, name="allreduce_small",
        out_shape=jax.ShapeDtypeStruct((r, c), F32),
        scratch_shapes=[pltpu.VMEM((8, r, c), F32), pltpu.SemaphoreType.DMA((7,)), pltpu.SemaphoreType.DMA((7,))],
    )(x)


def reduce_scatter(g):
    _, r, c = g.shape
    rh = r // 2
    core = lax.axis_index("c")
    me = _chip_index()
    g = g.reshape(4, 2, rh, c)
    p32, p16 = add_pair(g, pair_exchange(g), core.astype(jnp.int32)[None])
    q16 = scatter_chips(p16)
    order = jnp.stack([me, me ^ 1, me ^ 2, me ^ 3]).astype(jnp.int32)
    mine = sum_slots(p32, q16, order)
    theirs = pair_share(mine)
    return jnp.where(core == 0, jnp.concatenate([mine, theirs]), jnp.concatenate([theirs, mine]))


PACK_W = 1024
PACK_ROWS = 1024

SHARDED = {"ffn1_w_gate_up": 1, "ffn1_w_down": 0, "ffn2_w_gate_up": 1, "ffn2_w_down": 0, "attn_w_in": 1,
           "attn_w_out": 0, "gdn_w_in": 1, "gdn_w_out": 0, "gdn_conv_w": 1}
REPLICATED = ["ffn1_norm", "mix_norm", "ffn2_norm", "attn_b_in", "attn_sinks", "attn_b_out", "gdn_A_log",
              "gdn_dt_bias", "gdn_norm_w", "final_norm"]
WEIGHTS = ["ffn1_norm", "ffn1_w_gate_up", "ffn1_w_down", "mix_norm", "ffn2_norm", "ffn2_w_gate_up", "ffn2_w_down",
           "attn_w_in", "attn_b_in", "attn_sinks", "attn_w_out", "attn_b_out", "gdn_w_in", "gdn_conv_w", "gdn_A_log",
           "gdn_dt_bias", "gdn_norm_w", "gdn_w_out", "final_norm"]


def _pack_rows(flats, dtype, width, row_multiple):
    flat = jnp.concatenate([f.astype(dtype).reshape(-1) for f in flats])
    per = width * row_multiple
    pad = (-flat.shape[0]) % per
    if pad:
        flat = jnp.concatenate([flat, jnp.zeros((pad,), dtype)])
    return flat.reshape(-1, width)


def _by_width(shapes):
    groups = {}
    for n, shape in shapes.items():
        groups.setdefault(shape[-1], []).append(n)
    return groups


def _gather_weights(shards, dtype, split=True):
    full = {}
    for width, names in _by_width({n: s.shape for n, s in shards.items()}).items():
        packed = jnp.concatenate([shards[n].astype(dtype).reshape(-1, width) for n in names], axis=0)
        got = allgather_chips(packed, split)
        off = 0
        for n in names:
            layers, a, _ = shards[n].shape
            full[n] = [jnp.concatenate([got[j, off + l * a:off + (l + 1) * a] for j in range(4)], axis=SHARDED[n])
                       for l in range(layers)]
            off += layers * a
    return full


def _scatter_grads(grads, shard_shapes):
    out = {}
    for _, names in _by_width(shard_shapes).items():
        slots = []
        for j in range(4):
            parts = []
            for n in names:
                w = shard_shapes[n][1 + SHARDED[n]]
                parts += [lax.slice_in_dim(gl, j * w, (j + 1) * w, axis=SHARDED[n]) for gl in grads[n]]
            slots.append(jnp.concatenate(parts, axis=0))
        red = reduce_scatter(jnp.stack(slots))
        off = 0
        for n in names:
            layers, a, _ = shard_shapes[n]
            out[n] = red[off:off + layers * a].reshape(shard_shapes[n])
            off += layers * a
    return out


def _small_pack(items):
    rows = []
    for a in items:
        f = a.astype(F32).reshape(-1)
        pad = (-f.shape[0]) % LANE
        rows.append(jnp.concatenate([f, jnp.zeros((pad,), F32)]) if pad else f)
    return _pack_rows(rows, F32, LANE, 8)


def _small_unpack(buf, shapes):
    flat = buf.reshape(-1)
    out, off = [], 0
    for shape in shapes:
        size = 1
        for s in shape:
            size *= s
        out.append(flat[off:off + size].reshape(shape))
        off += size + (-size) % LANE
    return out


def _rope_table(positions):
    t = positions.shape[0]
    inv_freq = ROPE_THETA ** (-jnp.arange(0, ROPE_DIM, 2, dtype=F32) / ROPE_DIM)
    ang = positions.astype(F32)[:, None] * inv_freq
    cos, sin = jnp.cos(ang), jnp.sin(ang)
    rest = ATTN_HEAD_DIM - ROPE_DIM
    zeros = lambda n: jnp.zeros((t, n), F32)
    c64 = jnp.concatenate([cos, cos, jnp.ones((t, rest), F32)], axis=1)
    s_up = jnp.concatenate([zeros(ROPE_DIM // 2), sin, zeros(rest)], axis=1)
    s_dn = jnp.concatenate([-sin, zeros(ROPE_DIM // 2 + rest)], axis=1)
    return jnp.concatenate([c64, c64, s_up, s_up, s_dn, s_dn], axis=1)


def _as2d(a):
    return a.reshape(-1, a.shape[-1]) if a.ndim > 1 else a.reshape(1, -1)


def kernel(x, positions, ffn1_norm, ffn1_w_gate_up, ffn1_w_down, mix_norm, ffn2_norm, ffn2_w_gate_up, ffn2_w_down, attn_w_in, attn_b_in, attn_sinks, attn_w_out, attn_b_out, gdn_w_in, gdn_conv_w, gdn_A_log, gdn_dt_bias, gdn_norm_w, gdn_w_out, final_norm, loss_target, m_ffn1_norm, m_ffn1_w_gate_up, m_ffn1_w_down, m_mix_norm, m_ffn2_norm, m_ffn2_w_gate_up, m_ffn2_w_down, m_attn_w_in, m_attn_b_in, m_attn_sinks, m_attn_w_out, m_attn_b_out, m_gdn_w_in, m_gdn_conv_w, m_gdn_A_log, m_gdn_dt_bias, m_gdn_norm_w, m_gdn_w_out, m_final_norm, v_ffn1_norm, v_ffn1_w_gate_up, v_ffn1_w_down, v_mix_norm, v_ffn2_norm, v_ffn2_w_gate_up, v_ffn2_w_down, v_attn_w_in, v_attn_b_in, v_attn_sinks, v_attn_w_out, v_attn_b_out, v_gdn_w_in, v_gdn_conv_w, v_gdn_A_log, v_gdn_dt_bias, v_gdn_norm_w, v_gdn_w_out, v_final_norm):
    given = dict(locals())
    w = {n: given[n] for n in WEIGHTS}
    d = D_MODEL
    h = x[0]
    target = loss_target[0]
    depth = ffn1_norm.shape[0]

    big = [n for n in SHARDED if n != "gdn_conv_w"]
    full = _gather_weights({n: w[n] for n in big}, BF16)
    conv_full = _gather_weights({"gdn_conv_w": gdn_conv_w}, F32, split=False)["gdn_conv_w"]
    tab = _rope_table(positions[0])
    zero_d = jnp.zeros((1, d), F32)

    def gdn_params(j):
        w_in = full["gdn_w_in"][j]
        w_cat = jnp.concatenate([w_in, jnp.zeros((d, GDN_PROJ_W - w_in.shape[1]), BF16)], axis=1)
        conv = jnp.concatenate([conv_full[j], jnp.zeros((8 - GDN_CONV, GDN_QKV_W), F32)], axis=0)
        lanes = lambda vec: jnp.concatenate([jnp.zeros((GC_G,), F32), vec, jnp.zeros((LANE - GC_G - GDN_HEADS,), F32)])
        par = jnp.concatenate([lanes(gdn_A_log[j])[None], lanes(gdn_dt_bias[j])[None], jnp.zeros((6, LANE), F32)], axis=0)
        return w_cat, conv, par

    saved = []
    for l in range(depth):
        j = l // 2
        rec = {"h1": h}
        h, rec["ffn1_gu"], rec["ffn1_a"] = ffn_fwd(h, ffn1_norm[l][None], full["ffn1_w_gate_up"][l],
                                                   full["ffn1_w_down"][l])
        rec["h2"] = h
        if l % 2 == 0:
            p = norm_proj(h, mix_norm[l][None], full["attn_w_in"][j], attn_b_in[j][None])
            o = attn_fwd(p, tab, attn_sinks[j][None])
            h = linear_residual(h, o, full["attn_w_out"][j], attn_b_out[j][None])
            rec.update(p=p, o=o)
        else:
            w_cat, conv, par = gdn_params(j)
            proj = norm_proj(h, mix_norm[l][None], w_cat, jnp.zeros((1, GDN_PROJ_W), F32))
            qkv, gc = gdn_pre_fwd(proj, conv, par)
            on, st = gdn_chunk_fwd(qkv, gc, proj, gdn_norm_w[j][None])
            h = linear_residual(h, on, full["gdn_w_out"][j], zero_d)
            rec.update(proj=proj, qkv=qkv, gc=gc, on=on, st=st, w_cat=w_cat, conv=conv, par=par)
        rec["h3"] = h
        h, rec["ffn2_gu"], rec["ffn2_a"] = ffn_fwd(h, ffn2_norm[l][None], full["ffn2_w_gate_up"][l],
                                                   full["ffn2_w_down"][l])
        saved.append(rec)

    dh, loss_tile, d_final = loss_head(h, final_norm[None], target)

    g = {n: [None] * w[n].shape[0] for n in WEIGHTS if n != "final_norm"}
    for l in reversed(range(depth)):
        j = l // 2
        rec = saved[l]

        def ffn_back(tag, h_in, dh):
            wgu, wd = full[tag + "_w_gate_up"][l], full[tag + "_w_down"][l]
            dh_new, hn, dgu, dn = ffn_bwd(h_in, w[tag + "_norm"][l][None], wgu, wd, rec[tag + "_gu"], dh)
            g[tag + "_w_gate_up"][l] = matmul_tn(hn, dgu)
            g[tag + "_w_down"][l] = matmul_tn(rec[tag + "_a"], dh, 0.5)
            g[tag + "_norm"][l] = dn[0]
            return dh_new

        dh = ffn_back("ffn2", rec["h3"], dh)
        if l % 2 == 0:
            w_in, w_out = full["attn_w_in"][j], full["attn_w_out"][j]
            do, db_out = matmul_nt(dh, w_out)
            g["attn_w_out"][j] = matmul_tn(rec["o"], dh)
            g["attn_b_out"][j] = db_out[0]
            dq, dkvc, dkvp, dsink = attn_bwd(rec["p"], tab, attn_sinks[j][None], do)
            dkv = kv_combine(dkvc, dkvp)
            dh, hn, dn, cs_q, cs_kv = norm_proj_bwd(rec["h2"], mix_norm[l][None], dh, [dq, dkv],
                                                    [w_in[:, :ATTN_Q_W], w_in[:, ATTN_Q_W:]])
            g["attn_w_in"][j] = jnp.concatenate([matmul_tn(hn, dq), matmul_tn(hn, dkv)], axis=1)
            g["attn_b_in"][j] = jnp.concatenate([cs_q[0], cs_kv[0]])
            g["attn_sinks"][j] = dsink[0]
        else:
            w_cat, conv, par = rec["w_cat"], rec["conv"], rec["par"]
            d_on, _ = matmul_nt(dh, full["gdn_w_out"][j])
            g["gdn_w_out"][j] = matmul_tn(rec["on"], dh)
            dq, dk, dv, dz, dgc_heads, dnw = gdn_chunk_bwd(rec["qkv"], rec["gc"], rec["proj"], gdn_norm_w[j][None],
                                                           rec["st"], d_on)
            dy, dba, dpar = gdn_pre_bwd(rec["proj"], conv, par, dq, dk, dv, dgc_heads)
            dx, dconv = gdn_conv_bwd(rec["proj"], conv, dy)
            nz = GDN_QKV_W + GDN_HEADS * GDN_DK
            dh, hn, dn, _, _, _ = norm_proj_bwd(rec["h2"], mix_norm[l][None], dh, [dx, dz, dba],
                                                [w_cat[:, :GDN_QKV_W], w_cat[:, GDN_QKV_W:nz], w_cat[:, nz:]])
            g["gdn_w_in"][j] = jnp.concatenate(
                [matmul_tn(hn, dx), matmul_tn(hn, dz), matmul_tn(hn, dba)[:, :2 * GDN_HEADS]], axis=1)
            g["gdn_conv_w"][j] = dconv[:GDN_CONV]
            g["gdn_A_log"][j] = dpar[0, GDN_HEADS:2 * GDN_HEADS]
            g["gdn_dt_bias"][j] = dpar[1, GDN_HEADS:2 * GDN_HEADS]
            g["gdn_norm_w"][j] = dnw[0]
        g["mix_norm"][l] = dn[0]
        dh = ffn_back("ffn1", rec["h1"], dh)
    grad_x = dh[None]

    small_names = REPLICATED + ["gdn_conv_w"]
    local = {n: jnp.stack(g[n]) for n in small_names if n != "final_norm"}
    local["final_norm"] = d_final[0]
    small_shapes = [(1,)] + [local[n].shape for n in small_names]
    small = allreduce_small(_small_pack([loss_tile[0, 0:1]] + [local[n] for n in small_names]))
    small = _small_unpack(small, small_shapes)
    loss = small[0][0]
    grads = dict(zip(small_names, small[1:]))
    conv_cols = gdn_conv_w.shape[2]
    grads["gdn_conv_w"] = lax.dynamic_slice_in_dim(grads["gdn_conv_w"], _chip_index() * conv_cols, conv_cols, axis=2)
    grads.update(_scatter_grads({n: g[n] for n in big}, {n: w[n].shape for n in big}))

    delta, new_m, new_v = {}, {}, {}
    for n in SHARDED:
        dl, nm, nv = adamw(_as2d(w[n]), _as2d(grads[n]), _as2d(given["m_" + n]), _as2d(given["v_" + n]))
        delta[n], new_m[n], new_v[n] = dl.reshape(w[n].shape), nm.reshape(w[n].shape), nv.reshape(w[n].shape)
    shapes = [w[n].shape for n in REPLICATED]
    packed = [_small_pack([src[n] for n in REPLICATED]) for src in
              (w, grads, {n: given["m_" + n] for n in REPLICATED}, {n: given["v_" + n] for n in REPLICATED})]
    for dst, buf in zip((delta, new_m, new_v), adamw(*packed)):
        dst.update(zip(REPLICATED, _small_unpack(buf, shapes)))

    return (loss, grad_x, *[grads[n] for n in WEIGHTS], *[delta[n] for n in WEIGHTS],
            *[new_m[n] for n in WEIGHTS], *[new_v[n] for n in WEIGHTS])
```

```python
import functools

import jax
import jax.numpy as jnp
from jax import lax
from jax.experimental import pallas as pl
from jax.experimental.pallas import tpu as pltpu

F32 = jnp.float32
BF16 = jnp.bfloat16
HI = lax.Precision.HIGHEST
MESH = pl.DeviceIdType.MESH

D_MODEL = 1024
D_FF = 2816
DEPTH = 4
NORM_EPS = 1e-6
LANE = 128

ATTN_Q_HEADS = 16
ATTN_KV_HEADS = 4
ATTN_HEAD_DIM = 64
ATTN_GROUP = 4
ATTN_BLOCK = 128
ROPE_DIM = 16
ROPE_THETA = 500000.0
ATTN_Q_W = 1024
ATTN_KV_W = 256

GDN_HEADS = 8
GDN_DK = 128
GDN_CONV = 4
GDN_CHUNK = 64
GDN_QKV_W = 3072

ADAM_LR = 0.001
ADAM_B1 = 0.9
ADAM_B2 = 0.999
ADAM_EPS = 1e-08
ADAM_WD = 0.01
ADAM_STEP = 10

NEG = -1e30


def _dot(a, b, prec=None):
    return lax.dot_general(a, b, (((1,), (0,)), ((), ())), precision=prec, preferred_element_type=F32)


def _dot_nt(a, b, prec=None):
    return lax.dot_general(a, b, (((1,), (1,)), ((), ())), precision=prec, preferred_element_type=F32)


def _dot_tn(a, b, prec=None):
    return lax.dot_general(a, b, (((0,), (0,)), ((), ())), precision=prec, preferred_element_type=F32)


def _bdot(a, b):
    return _dot(a.astype(BF16), b.astype(BF16))


def _bdot_nt(a, b):
    return _dot_nt(a.astype(BF16), b.astype(BF16))


def _bdot_tn(a, b):
    return _dot_tn(a.astype(BF16), b.astype(BF16))


def _sigmoid(x):
    return 1.0 / (1.0 + jnp.exp(-x))


def _silu(x):
    return x * _sigmoid(x)


def _silu_grad(x):
    s = _sigmoid(x)
    return s * (1.0 + x * (1.0 - s))


def _rms(x, w):
    r = lax.rsqrt(jnp.mean(x * x, axis=-1, keepdims=True) + NORM_EPS)
    xhat = x * r
    return xhat * w, xhat, r


def _rms_bwd(dy, w, xhat, r):
    dxhat = dy * w
    dx = r * (dxhat - xhat * jnp.mean(dxhat * xhat, axis=-1, keepdims=True))
    dw = jnp.sum(dy * xhat, axis=0, keepdims=True)
    return dx, dw


def _arb(n):
    return pltpu.CompilerParams(dimension_semantics=("arbitrary",) * n)


def _tile(n, want):
    t = min(n, want)
    assert n % t == 0, (n, want)
    return t


def _iota2(shape, dim):
    return lax.broadcasted_iota(jnp.int32, shape, dim)


_NN = (((1,), (0,)), ((), ()))
_NT = (((1,), (1,)), ((), ()))
_TN = (((0,), (0,)), ((), ()))


def _raw1(a, b, dn):
    return lax.dot_general(a.astype(BF16), b.astype(BF16), dn, preferred_element_type=F32)


def _raw3(a, b, dn):
    ah, bh = a.astype(BF16), b.astype(BF16)
    al, bl = (a - ah.astype(F32)).astype(BF16), (b - bh.astype(F32)).astype(BF16)
    f = lambda x, y: lax.dot_general(x, y, dn, preferred_element_type=F32)
    return f(ah, bh) + f(ah, bl) + f(al, bh)


def _make_mm(raw):
    @jax.custom_vjp
    def mm(a, b):
        return raw(a, b, _NN)

    mm.defvjp(lambda a, b: (raw(a, b, _NN), (a, b)),
              lambda res, ct: (raw(ct, res[1], _NT), raw(res[0], ct, _TN)))

    @jax.custom_vjp
    def mm_nt(a, b):
        return raw(a, b, _NT)

    mm_nt.defvjp(lambda a, b: (raw(a, b, _NT), (a, b)),
                 lambda res, ct: (raw(ct, res[1], _NN), raw(ct, res[0], _TN)))
    return mm, mm_nt


_mm1, _mm1_nt = _make_mm(_raw1)
_mm3, _mm3_nt = _make_mm(_raw3)


def _eye(n):
    return (_iota2((n, n), 0) == _iota2((n, n), 1)).astype(F32)


def _each(f, *lists):
    return [f(*t) for t in zip(*lists)]


def _inv_newton(mats):
    eye = _eye(mats[0].shape[0])
    ps = [-a for a in mats]
    ms = [eye + p for p in ps]
    k = 1
    while 2 * k < GDN_CHUNK:
        ps = [_raw1(p, p, _NN) for p in ps]
        ms = _each(lambda m, p: m + _raw1(m, p, _NN), ms, ps)
        k *= 2
    rs = _each(lambda a, m: eye - m - _raw3(a, m, _NN), mats, ms)
    return _each(lambda m, r: m + _raw1(m, r, _NN), ms, rs)


@jax.custom_vjp
def _unit_lower_inv(mats):
    return _inv_newton(mats)


def _unit_lower_inv_fwd(mats):
    ms = _inv_newton(mats)
    return ms, ms


def _unit_lower_inv_bwd(ms, dms):
    ts = _each(lambda m, dm: _raw3(m, dm, _TN), ms, dms)
    return (_each(lambda t, m: -_raw3(t, m, _NT), ts, ms),)


_unit_lower_inv.defvjp(_unit_lower_inv_fwd, _unit_lower_inv_bwd)


def _gdn_local(q, k, v, dcb, dcb128, dlb128, bb128):
    n = q[0].shape[0]
    r, s = _iota2((n, n), 0), _iota2((n, n), 1)
    same = (r // GDN_CHUNK) == (s // GDN_CHUNK)
    causal, strict = same & (r >= s), same & (r > s)
    eye = _eye(n)
    decay_l = [jnp.exp(jnp.where(causal, d - d.T, NEG)) for d in dcb]
    kb = _each(lambda a, b: a * b, k, bb128)
    a = _each(lambda x, y, dl: jnp.where(strict, _mm1_nt(x, y) * dl, 0.0), kb, k, decay_l)
    m_off = [m - eye for m in _unit_lower_inv(a)]
    edc = [jnp.exp(d) for d in dcb128]
    rhs = _each(lambda vv, bb, kk, e: jnp.concatenate([vv * bb, kk * e], axis=1), v, bb128, kb, edc)
    sol = _each(lambda x, m: x + _mm3(m, x), rhs, m_off)
    aqk = _each(lambda x, y, dl: jnp.where(causal, _mm1_nt(x, y) * dl, 0.0), q, k, decay_l)
    q_dec = _each(lambda x, e: x * e, q, edc)
    k_dec = _each(lambda x, dl, dc: x * jnp.exp(dl - dc), k, dlb128, dcb128)
    return sol, aqk, q_dec, k_dec


def _gated_norm(o, z, nw):
    r = lax.rsqrt(jnp.mean(o * o, axis=-1, keepdims=True) + NORM_EPS)
    return o * r * nw * _silu(z)


def _gated_norm_bwd(dy, o, z, nw):
    r = lax.rsqrt(jnp.mean(o * o, axis=-1, keepdims=True) + NORM_EPS)
    xhat = o * r
    sz = _silu(z)
    dxhat = dy * nw * sz
    do = r * (dxhat - xhat * jnp.mean(dxhat * xhat, axis=-1, keepdims=True))
    dz = dy * xhat * nw * _silu_grad(z)
    dnw = jnp.sum(dy * xhat * sz, axis=0, keepdims=True)
    return do, dz, dnw


GDN_HEAD_GROUP = 8
GC_BETA, GC_G, GC_DECAY, GC_LAST =0, GDN_HEADS, 2 * GDN_HEADS, 3 * GDN_HEADS


def _gate_cols(gc, h, rows):
    lane = _iota2((rows, LANE), 1)
    col = lambda off: jnp.sum(jnp.where(lane == off + h, gc, 0.0), axis=-1, keepdims=True)
    return col(GC_BETA), col(GC_DECAY), col(GC_LAST)


def _gdn_local_args(q_ref, k_ref, v_ref, gc_ref, hp, n):
    gc = gc_ref[...]
    lanes = [slice(hh * LANE, (hh + 1) * LANE) for hh in range(hp)]
    cols = [_gate_cols(gc, pl.program_id(0) * hp + hh, n) for hh in range(hp)]
    bcast = lambda i, w: [jnp.broadcast_to(c[i], (n, w)) for c in cols]
    args = ([q_ref[:, ls] for ls in lanes], [k_ref[:, ls] for ls in lanes], [v_ref[:, ls] for ls in lanes],
            bcast(1, n), bcast(1, LANE), bcast(2, LANE), bcast(0, LANE))
    return args, [c[2] for c in cols], lanes


def gdn_chunk_fwd(qkv, gc, proj, norm_w):
    t = qkv.shape[0]
    c = GDN_CHUNK
    tc = _tile(t, 256)
    nsub = tc // c

    hp = GDN_HEAD_GROUP

    def body(q_ref, k_ref, v_ref, gc_ref, z_ref, nw_ref, on_ref, st_ref, s_ref):
        @pl.when(pl.program_id(1) == 0)
        def _():
            s_ref[...] = jnp.zeros_like(s_ref)

        args, dl, lanes = _gdn_local_args(q_ref, k_ref, v_ref, gc_ref, hp, tc)
        sol, aqk, q_dec, k_dec = _gdn_local(*args)
        u = [x[:, :GDN_DK] for x in sol]
        w = [x[:, GDN_DK:] for x in sol]
        s = [s_ref[hh] for hh in range(hp)]
        v_new, o_state = [[] for _ in range(hp)], [[] for _ in range(hp)]
        for j in range(nsub):
            sl = slice(j * c, (j + 1) * c)
            for hh in range(hp):
                st_ref[hh, j] = s[hh]
            vn = _each(lambda uu, ww, ss: uu[sl] - _bdot(ww[sl], ss), u, w, s)
            os_ = _each(lambda qq, ss: _bdot(qq[sl], ss), q_dec, s)
            s = _each(lambda ss, d, kk, vv: ss * jnp.exp(d[j * c:j * c + 1]) + _bdot_tn(kk[sl], vv), s, dl, k_dec, vn)
            for hh in range(hp):
                v_new[hh].append(vn[hh])
                o_state[hh].append(os_[hh])
        for hh in range(hp):
            s_ref[hh] = s[hh]
        cat = lambda xs: jnp.concatenate(xs, axis=0)
        o = _each(lambda os_, aa, vv: cat(os_) + _bdot(aa, cat(vv)), o_state, aqk, v_new)
        for hh, ls in enumerate(lanes):
            on_ref[:, ls] = _gated_norm(o[hh], z_ref[:, ls], nw_ref[...]).astype(BF16)

    col = lambda off: pl.BlockSpec((tc, hp * LANE), lambda g, i: (i, off // hp + g))
    return pl.pallas_call(
        body, name="gdn_chunk_fwd",
        grid=(GDN_HEADS // hp, t // tc),
        in_specs=[col(0), col(GDN_HEADS), col(2 * GDN_HEADS),
                  pl.BlockSpec((tc, LANE), lambda g, i: (i, 0)),
                  col(GDN_QKV_W // LANE),
                  pl.BlockSpec((1, LANE), lambda g, i: (0, 0))],
        out_specs=[col(0),
                   pl.BlockSpec((hp, nsub, GDN_DK, GDN_DK), lambda g, i: (g, i, 0, 0))],
        out_shape=[jax.ShapeDtypeStruct((t, GDN_HEADS * GDN_DK), BF16),
                   jax.ShapeDtypeStruct((GDN_HEADS, t // c, GDN_DK, GDN_DK), F32)],
        scratch_shapes=[pltpu.VMEM((hp, GDN_DK, GDN_DK), F32)],
        compiler_params=_arb(2),
    )(qkv, qkv, qkv, gc, proj, norm_w)


def gdn_chunk_bwd(qkv, gc, proj, norm_w, states, d_on):
    t = qkv.shape[0]
    c = GDN_CHUNK
    tc = _tile(t, 256)
    nsub = tc // c
    nblk = t // tc
    hp = GDN_HEAD_GROUP

    def body(q_ref, k_ref, v_ref, gc_ref, z_ref, nw_ref, st_ref, don_ref,
             dq_ref, dk_ref, dv_ref, dz_ref, dgc_ref, dnw_ref, ds_ref):
        @pl.when(pl.program_id(1) == 0)
        def _():
            ds_ref[...] = jnp.zeros_like(ds_ref)

        @pl.when((pl.program_id(1) == 0) & (pl.program_id(0) == 0))
        def _():
            dnw_ref[...] = jnp.zeros_like(dnw_ref)

        rows = [slice(j * c, (j + 1) * c) for j in range(nsub)]
        cat = lambda xs: jnp.concatenate(xs, axis=0)
        lsum = lambda x: jnp.sum(x, axis=-1, keepdims=True)
        lane = _iota2((tc, LANE), 1)
        row = _iota2((tc, 1), 0)
        heads = range(hp)
        args, dl, lanes = _gdn_local_args(q_ref, k_ref, v_ref, gc_ref, hp, tc)
        (sol, aqk, q_dec, k_dec), vjp = jax.vjp(_gdn_local, *args)
        u = [x[:, :GDN_DK] for x in sol]
        w = [x[:, GDN_DK:] for x in sol]
        states = [[st_ref[hh, j] for j in range(nsub)] for hh in heads]
        v_new = _each(lambda uu, ww, st: [uu[sl] - _bdot(ww[sl], s) for sl, s in zip(rows, st)], u, w, states)
        v_all = [cat(v) for v in v_new]
        o = _each(lambda qq, st, aa, vv: cat([_bdot(qq[sl], s) for sl, s in zip(rows, st)]) + _bdot(aa, vv),
                  q_dec, states, aqk, v_all)
        nw = nw_ref[...]
        gn = [_gated_norm_bwd(don_ref[:, ls], o[hh], z_ref[:, ls], nw) for hh, ls in enumerate(lanes)]
        do = [x[0] for x in gn]
        for hh, ls in enumerate(lanes):
            dnw_ref[...] += gn[hh][2]
            dz_ref[:, ls] = gn[hh][1].astype(BF16)
        d_aqk = _each(_bdot_nt, do, v_all)
        dv_o = _each(_bdot_tn, aqk, do)
        ds = [ds_ref[hh] for hh in heads]
        d_u, d_w, d_qdec, d_kdec, d_last = ([[None] * nsub for _ in heads] for _ in range(5))
        for j in reversed(range(nsub)):
            sl = rows[j]
            sj = [states[hh][j] for hh in heads]
            cd = [jnp.exp(d[j * c:j * c + 1]) for d in dl]
            du = _each(lambda dvo, kk, dd: dvo[sl] + _bdot(kk[sl], dd), dv_o, k_dec, ds)
            dqd = _each(lambda dd, s: _bdot_nt(dd[sl], s), do, sj)
            dkd = _each(lambda vv, dd: _bdot_nt(vv[j], dd), v_new, ds)
            dla = _each(lambda s, dd, cc: jnp.sum(lsum(s * dd), axis=0, keepdims=True) * cc, sj, ds, cd)
            dw = _each(lambda x, s: -_bdot_nt(x, s), du, sj)
            ds = _each(lambda qq, dd, cc, dsn, ww, x: _bdot_tn(qq[sl], dd[sl]) + cc * dsn - _bdot_tn(ww[sl], x),
                       q_dec, do, cd, ds, w, du)
            for hh in heads:
                d_u[hh][j], d_w[hh][j], d_qdec[hh][j], d_kdec[hh][j], d_last[hh][j] = du[hh], dw[hh], dqd[hh], dkd[hh], dla[hh]
        for hh in heads:
            ds_ref[hh] = ds[hh]
        d_sol = _each(lambda a, b: jnp.concatenate([cat(a), cat(b)], axis=1), d_u, d_w)
        dq, dk, dv, d_dcb, d_dcb128, d_dlb128, d_bb = vjp((d_sol, d_aqk, [cat(x) for x in d_qdec], [cat(x) for x in d_kdec]))
        dgc = jnp.zeros((tc, LANE), F32)
        for hh, ls in enumerate(lanes):
            h = pl.program_id(0) * hp + hh
            dq_ref[:, ls] = dq[hh]
            dk_ref[:, ls] = dk[hh]
            dv_ref[:, ls] = dv[hh]
            d_dl = lsum(d_dlb128[hh])
            for j in range(nsub):
                d_dl = d_dl + jnp.where(row == j * c, d_last[hh][j], 0.0)
            dgc = dgc + jnp.where(lane == GC_BETA + h, lsum(d_bb[hh]),
                                  jnp.where(lane == GC_DECAY + h, lsum(d_dcb[hh]) + lsum(d_dcb128[hh]),
                                            jnp.where(lane == GC_LAST + h, d_dl, 0.0)))
        dgc_ref[0] = dgc

    rev = lambda i: nblk - 1 - i
    col = lambda off: pl.BlockSpec((tc, hp * LANE), lambda g, i: (rev(i), off // hp + g))
    return pl.pallas_call(
        body, name="gdn_chunk_bwd",
        grid=(GDN_HEADS // hp, nblk),
        in_specs=[col(0), col(GDN_HEADS), col(2 * GDN_HEADS),
                  pl.BlockSpec((tc, LANE), lambda g, i: (rev(i), 0)),
                  col(GDN_QKV_W // LANE),
                  pl.BlockSpec((1, LANE), lambda g, i: (0, 0)),
                  pl.BlockSpec((hp, nsub, GDN_DK, GDN_DK), lambda g, i: (g, rev(i), 0, 0)),
                  col(0)],
        out_specs=[col(0), col(0), col(0), col(0),
                   pl.BlockSpec((1, tc, LANE), lambda g, i: (g, rev(i), 0)),
                   pl.BlockSpec((1, LANE), lambda g, i: (0, 0))],
        out_shape=[
            jax.ShapeDtypeStruct((t, GDN_HEADS * GDN_DK), F32),
            jax.ShapeDtypeStruct((t, GDN_HEADS * GDN_DK), F32),
            jax.ShapeDtypeStruct((t, GDN_HEADS * GDN_DK), F32),
            jax.ShapeDtypeStruct((t, GDN_HEADS * GDN_DK), BF16),
            jax.ShapeDtypeStruct((GDN_HEADS // hp, t, LANE), F32),
            jax.ShapeDtypeStruct((1, LANE), F32)],
        scratch_shapes=[pltpu.VMEM((hp, GDN_DK, GDN_DK), F32)],
        compiler_params=_arb(2),
    )(qkv, qkv, qkv, gc, proj, norm_w, states, d_on)


GDN_PROJ_W = GDN_QKV_W + 1024 + LANE
GDN_Q_SCALE = GDN_DK ** -0.5


def _shift_rows(ext, shift, lo, n):
    if shift == 0:
        return ext[lo:lo + n]
    return pltpu.roll(ext, shift, 0)[lo:lo + n]


def _conv_fwd(x, halo, w):
    n = x.shape[0]
    ext = jnp.concatenate([halo, x], axis=0)
    y = w[GDN_CONV - 1:GDN_CONV] * x
    for j in range(GDN_CONV - 1):
        y = y + w[j:j + 1] * _shift_rows(ext, GDN_CONV - 1 - j, 8, n)
    return y


def _chunk_masks(n):
    r, s = _iota2((n, n), 0), _iota2((n, n), 1)
    same = (r // GDN_CHUNK) == (s // GDN_CHUNK)
    return (same & (r >= s)).astype(F32), (same & (r <= s)).astype(F32), same.astype(F32)


def _softplus(x):
    return jnp.maximum(x, 0.0) + jnp.log(1.0 + jnp.exp(-jnp.abs(x)))


def _l2n(t):
    rs = lax.rsqrt(jnp.sum(t * t, axis=-1, keepdims=True) + NORM_EPS)
    return t * rs, rs


def gdn_pre_fwd(proj, conv_w, gate_par):
    t = proj.shape[0]
    tm = _tile(t, 256)

    def body(x_ref, halo_ref, ba_ref, w_ref, gp_ref, qkv_ref, gc_ref):
        i = pl.program_id(0)
        halo = jnp.where(i > 0, halo_ref[...], 0.0)
        y = _silu(_conv_fwd(x_ref[...], halo, w_ref[...]))
        for hh in range(2 * GDN_HEADS):
            sl = slice(hh * LANE, (hh + 1) * LANE)
            tn, _ = _l2n(y[:, sl])
            qkv_ref[:, sl] = tn * GDN_Q_SCALE if hh < GDN_HEADS else tn
        qkv_ref[:, 2 * GDN_HEADS * LANE:] = y[:, 2 * GDN_HEADS * LANE:]
        ba = ba_ref[...]
        lane = _iota2(ba.shape, 1)
        gp = gp_ref[...]
        is_a = (lane >= GC_G) & (lane < GC_G + GDN_HEADS)
        g = jnp.where(is_a, -jnp.exp(gp[0:1]) * _softplus(ba + gp[1:2]), 0.0)
        tri, _, same = _chunk_masks(tm)
        decay = pltpu.roll(_dot(tri, g, HI), GC_DECAY - GC_G, 1)
        last = pltpu.roll(_dot(same, g, HI), GC_LAST - GC_G, 1)
        gc_ref[...] = jnp.where(lane < GDN_HEADS, _sigmoid(ba), g) + decay + last

    return pl.pallas_call(
        body, name="gdn_pre_fwd",
        grid=(t // tm,),
        in_specs=[pl.BlockSpec((tm, GDN_QKV_W), lambda i: (i, 0)),
                  pl.BlockSpec((8, GDN_QKV_W), lambda i: (jnp.maximum(i * (tm // 8) - 1, 0), 0)),
                  pl.BlockSpec((tm, LANE), lambda i: (i, (GDN_QKV_W + 1024) // LANE)),
                  pl.BlockSpec((8, GDN_QKV_W), lambda i: (0, 0)),
                  pl.BlockSpec((8, LANE), lambda i: (0, 0))],
        out_specs=[pl.BlockSpec((tm, GDN_QKV_W), lambda i: (i, 0)),
                   pl.BlockSpec((tm, LANE), lambda i: (i, 0))],
        out_shape=[jax.ShapeDtypeStruct((t, GDN_QKV_W), F32), jax.ShapeDtypeStruct((t, LANE), F32)],
        compiler_params=_arb(1),
    )(proj, proj, proj, conv_w, gate_par)


def gdn_pre_bwd(proj, conv_w, gate_par, dq, dk, dv, dgc_heads):
    t = proj.shape[0]
    tm = _tile(t, 256)

    def body(x_ref, halo_ref, ba_ref, w_ref, gp_ref, dq_ref, dk_ref, dv_ref, dgc_ref, dy_ref, dba_ref, dgp_ref):
        i = pl.program_id(0)

        @pl.when(i == 0)
        def _():
            dgp_ref[...] = jnp.zeros_like(dgp_ref)

        halo = jnp.where(i > 0, halo_ref[...], 0.0)
        y = _conv_fwd(x_ref[...], halo, w_ref[...])
        for hh in range(3 * GDN_HEADS):
            sl = slice(hh * LANE, (hh + 1) * LANE)
            hsl = slice((hh % GDN_HEADS) * LANE, (hh % GDN_HEADS + 1) * LANE)
            yy = y[:, sl]
            if hh < 2 * GDN_HEADS:
                tn, rs = _l2n(_silu(yy))
                dtn = dq_ref[:, hsl] * GDN_Q_SCALE if hh < GDN_HEADS else dk_ref[:, hsl]
                dsil = rs * (dtn - tn * jnp.sum(dtn * tn, axis=-1, keepdims=True))
            else:
                dsil = dv_ref[:, hsl]
            dy_ref[:, sl] = dsil * _silu_grad(yy)
        dgc = dgc_ref[0]
        for hh in range(1, dgc_heads.shape[0]):
            dgc = dgc + dgc_ref[hh]
        ba = ba_ref[...]
        lane = _iota2(ba.shape, 1)
        _, tri_t, same = _chunk_masks(tm)
        d_decay = jnp.where((lane >= GC_DECAY) & (lane < GC_DECAY + GDN_HEADS), dgc, 0.0)
        d_last = jnp.where((lane >= GC_LAST) & (lane < GC_LAST + GDN_HEADS), dgc, 0.0)
        dgc = (jnp.where(lane < GDN_HEADS, dgc, 0.0) + pltpu.roll(_dot(tri_t, d_decay, HI), LANE - (GC_DECAY - GC_G), 1)
               + pltpu.roll(_dot(same, d_last, HI), LANE - (GC_LAST - GC_G), 1))
        gp = gp_ref[...]
        xg = ba + gp[1:2]
        ea = jnp.exp(gp[0:1])
        sp = _softplus(xg)
        sb = _sigmoid(ba)
        is_b = lane < GDN_HEADS
        is_a = (lane >= GDN_HEADS) & (lane < 2 * GDN_HEADS)
        d_pre = jnp.where(is_a, dgc * (-ea) * _sigmoid(xg), 0.0)
        dba_ref[...] = jnp.where(is_b, dgc * sb * (1.0 - sb), d_pre).astype(BF16)
        d_alog = jnp.sum(jnp.where(is_a, dgc * (-ea) * sp, 0.0), axis=0, keepdims=True)
        d_dtb = jnp.sum(d_pre, axis=0, keepdims=True)
        row = _iota2((8, LANE), 0)
        dgp_ref[...] += jnp.where(row == 0, d_alog, jnp.where(row == 1, d_dtb, 0.0))

    hspec = pl.BlockSpec((tm, GDN_HEADS * LANE), lambda i: (i, 0))
    return pl.pallas_call(
        body, name="gdn_pre_bwd",
        grid=(t // tm,),
        in_specs=[pl.BlockSpec((tm, GDN_QKV_W), lambda i: (i, 0)),
                  pl.BlockSpec((8, GDN_QKV_W), lambda i: (jnp.maximum(i * (tm // 8) - 1, 0), 0)),
                  pl.BlockSpec((tm, LANE), lambda i: (i, (GDN_QKV_W + 1024) // LANE)),
                  pl.BlockSpec((8, GDN_QKV_W), lambda i: (0, 0)),
                  pl.BlockSpec((8, LANE), lambda i: (0, 0)),
                  hspec, hspec, hspec,
                  pl.BlockSpec((dgc_heads.shape[0], tm, LANE), lambda i: (0, i, 0))],
        out_specs=[pl.BlockSpec((tm, GDN_QKV_W), lambda i: (i, 0)),
                   pl.BlockSpec((tm, LANE), lambda i: (i, 0)),
                   pl.BlockSpec((8, LANE), lambda i: (0, 0))],
        out_shape=[jax.ShapeDtypeStruct((t, GDN_QKV_W), F32), jax.ShapeDtypeStruct((t, LANE), BF16),
                   jax.ShapeDtypeStruct((8, LANE), F32)],
        compiler_params=_arb(1),
    )(proj, proj, proj, conv_w, gate_par, dq, dk, dv, dgc_heads)


def gdn_conv_bwd(proj, conv_w, dy):
    t = proj.shape[0]
    tm = _tile(t, 256)
    nblk = t // tm

    def body(x_ref, halo_ref, w_ref, dy_ref, dyn_ref, dx_ref, dw_ref):
        i = pl.program_id(0)

        @pl.when(i == 0)
        def _():
            dw_ref[...] = jnp.zeros_like(dw_ref)

        w = w_ref[...]
        dy = dy_ref[...]
        ext_dy = jnp.concatenate([dy, jnp.where(i < nblk - 1, dyn_ref[...], 0.0)], axis=0)
        ext_x = jnp.concatenate([jnp.where(i > 0, halo_ref[...], 0.0), x_ref[...]], axis=0)
        dx = w[GDN_CONV - 1:GDN_CONV] * dy
        rows = [jnp.sum(dy * x_ref[...], axis=0, keepdims=True)]
        for j in range(GDN_CONV - 1):
            sh = GDN_CONV - 1 - j
            dx = dx + w[j:j + 1] * _shift_rows(ext_dy, tm + 8 - sh, 0, tm)
            rows.insert(j, jnp.sum(dy * _shift_rows(ext_x, sh, 8, tm), axis=0, keepdims=True))
        dx_ref[...] = dx.astype(BF16)
        row = _iota2((8, GDN_QKV_W), 0)
        acc = jnp.zeros((8, GDN_QKV_W), F32)
        for j in range(GDN_CONV):
            acc = acc + jnp.where(row == j, rows[j], 0.0)
        dw_ref[...] += acc

    return pl.pallas_call(
        body, name="gdn_conv_bwd",
        grid=(nblk,),
        in_specs=[pl.BlockSpec((tm, GDN_QKV_W), lambda i: (i, 0)),
                  pl.BlockSpec((8, GDN_QKV_W), lambda i: (jnp.maximum(i * (tm // 8) - 1, 0), 0)),
                  pl.BlockSpec((8, GDN_QKV_W), lambda i: (0, 0)),
                  pl.BlockSpec((tm, GDN_QKV_W), lambda i: (i, 0)),
                  pl.BlockSpec((8, GDN_QKV_W), lambda i: (jnp.minimum((i + 1) * (tm // 8), t // 8 - 1), 0))],
        out_specs=[pl.BlockSpec((tm, GDN_QKV_W), lambda i: (i, 0)),
                   pl.BlockSpec((8, GDN_QKV_W), lambda i: (0, 0))],
        out_shape=[jax.ShapeDtypeStruct((t, GDN_QKV_W), BF16), jax.ShapeDtypeStruct((8, GDN_QKV_W), F32)],
        compiler_params=_arb(1),
    )(proj, proj, conv_w, dy, dy)


def _resident(w_hbm, w_vmem, sem):
    @pl.when(pl.program_id(0) == 0)
    def _():
        cp = pltpu.make_async_copy(w_hbm, w_vmem, sem)
        cp.start()
        cp.wait()


ANY = pl.BlockSpec(memory_space=pl.ANY)


def norm_proj(h, nw, w, bias):
    t, d = h.shape
    n = w.shape[1]
    tm = _tile(t, 256)
    nc = _tile(n, 1536) if n % 1536 == 0 else _tile(n, 1408)

    def body(h_ref, nw_ref, w_hbm, b_ref, o_ref, w_ref, sem):
        _resident(w_hbm, w_ref, sem)
        hn = _rms(h_ref[...], nw_ref[...])[0].astype(BF16)
        for c0 in range(0, n, nc):
            o_ref[:, c0:c0 + nc] = _dot(hn, w_ref[:, c0:c0 + nc]) + b_ref[:, c0:c0 + nc]

    return pl.pallas_call(
        body, name="norm_proj",
        grid=(t // tm,),
        in_specs=[pl.BlockSpec((tm, d), lambda i: (i, 0)), pl.BlockSpec((1, d), lambda i: (0, 0)), ANY,
                  pl.BlockSpec((1, n), lambda i: (0, 0))],
        out_specs=pl.BlockSpec((tm, n), lambda i: (i, 0)),
        out_shape=jax.ShapeDtypeStruct((t, n), F32),
        scratch_shapes=[pltpu.VMEM((d, n), BF16), pltpu.SemaphoreType.DMA],
        compiler_params=_arb(1),
    )(h, nw, w, bias)


def linear_residual(h, x, w, bias):
    t, d = h.shape
    k = x.shape[1]
    tm = _tile(t, 512)

    def body(h_ref, x_ref, w_hbm, b_ref, o_ref, w_ref, sem):
        _resident(w_hbm, w_ref, sem)
        o_ref[...] = h_ref[...] + _dot(x_ref[...], w_ref[...]) + b_ref[...]

    return pl.pallas_call(
        body, name="linear_residual",
        grid=(t // tm,),
        in_specs=[pl.BlockSpec((tm, d), lambda i: (i, 0)), pl.BlockSpec((tm, k), lambda i: (i, 0)), ANY,
                  pl.BlockSpec((1, d), lambda i: (0, 0))],
        out_specs=pl.BlockSpec((tm, d), lambda i: (i, 0)),
        out_shape=jax.ShapeDtypeStruct((t, d), F32),
        scratch_shapes=[pltpu.VMEM((k, d), BF16), pltpu.SemaphoreType.DMA],
        compiler_params=_arb(1),
    )(h, x, w, bias)


def matmul_nt(dy, w):
    t, d = dy.shape
    k = w.shape[0]
    tm = _tile(t, 512)

    def body(dy_ref, w_hbm, o_ref, cs_ref, w_ref, sem):
        _resident(w_hbm, w_ref, sem)

        @pl.when(pl.program_id(0) == 0)
        def _():
            cs_ref[...] = jnp.zeros_like(cs_ref)

        dy = dy_ref[...]
        cs_ref[...] += jnp.sum(dy, axis=0, keepdims=True)
        o_ref[...] = _dot_nt(dy.astype(BF16), w_ref[...])

    return pl.pallas_call(
        body, name="matmul_nt",
        grid=(t // tm,),
        in_specs=[pl.BlockSpec((tm, d), lambda i: (i, 0)), ANY],
        out_specs=[pl.BlockSpec((tm, k), lambda i: (i, 0)), pl.BlockSpec((1, d), lambda i: (0, 0))],
        out_shape=[jax.ShapeDtypeStruct((t, k), F32), jax.ShapeDtypeStruct((1, d), F32)],
        scratch_shapes=[pltpu.VMEM((k, d), BF16), pltpu.SemaphoreType.DMA],
        compiler_params=_arb(1),
    )(dy, w)


def norm_proj_bwd(h, nw, dh, dps, ws):
    t, d = h.shape
    np_ = len(dps)
    ns = [w.shape[1] for w in ws]
    tm = _tile(t, 256)

    def body(*refs):
        h_ref, nw_ref, dh_ref = refs[:3]
        dp_refs = refs[3:3 + np_]
        w_hbms = refs[3 + np_:3 + 2 * np_]
        o_ref, hn_ref, dnw_ref = refs[3 + 2 * np_:6 + 2 * np_]
        cs_refs = refs[6 + 2 * np_:6 + 3 * np_]
        w_refs = refs[6 + 3 * np_:6 + 4 * np_]
        sem = refs[6 + 4 * np_]
        for a, b in zip(w_hbms, w_refs):
            _resident(a, b, sem)

        @pl.when(pl.program_id(0) == 0)
        def _():
            dnw_ref[...] = jnp.zeros_like(dnw_ref)
            for c in cs_refs:
                c[...] = jnp.zeros_like(c)

        nw = nw_ref[...]
        hn, xhat, r = _rms(h_ref[...], nw)
        hn_ref[...] = hn.astype(BF16)
        dhn = jnp.zeros((tm, d), F32)
        for dp_ref, w_ref, cs_ref in zip(dp_refs, w_refs, cs_refs):
            dp = dp_ref[...]
            cs_ref[...] += jnp.sum(dp.astype(F32), axis=0, keepdims=True)
            dhn = dhn + _dot_nt(dp, w_ref[...])
        dx, dnw = _rms_bwd(dhn, nw, xhat, r)
        dnw_ref[...] += dnw
        o_ref[...] = dh_ref[...] + dx

    row = pl.BlockSpec((tm, d), lambda i: (i, 0))
    vec = pl.BlockSpec((1, d), lambda i: (0, 0))
    return pl.pallas_call(
        body, name="norm_proj_bwd",
        grid=(t // tm,),
        in_specs=[row, vec, row] + [pl.BlockSpec((tm, n), lambda i: (i, 0)) for n in ns] + [ANY] * np_,
        out_specs=[row, row, vec] + [pl.BlockSpec((1, n), lambda i: (0, 0)) for n in ns],
        out_shape=[jax.ShapeDtypeStruct((t, d), F32), jax.ShapeDtypeStruct((t, d), BF16),
                   jax.ShapeDtypeStruct((1, d), F32)] + [jax.ShapeDtypeStruct((1, n), F32) for n in ns],
        scratch_shapes=[pltpu.VMEM((d, n), BF16) for n in ns] + [pltpu.SemaphoreType.DMA],
        compiler_params=_arb(1),
    )(h, nw, dh, *dps, *ws)


def matmul_tn(x, y, scale=1.0):
    t, k = x.shape
    n = y.shape[1]
    tk = _tile(k, 1024) if k % 1024 == 0 else _tile(k, 1408)
    tn = n if n <= 1536 else (1024 if n % 1024 == 0 else 1408)
    assert n % tn == 0
    tt = _tile(t, 1024)
    nt = t // tt

    def body(x_ref, y_ref, o_ref):
        @pl.when(pl.program_id(2) == 0)
        def _():
            o_ref[...] = jnp.zeros_like(o_ref)

        yv = y_ref[...]
        if scale != 1.0:
            yv = yv * scale
        o_ref[...] += _dot_tn(x_ref[...].astype(BF16), yv.astype(BF16))

    return pl.pallas_call(
        body, name="matmul_tn",
        grid=(k // tk, n // tn, nt),
        in_specs=[pl.BlockSpec((tt, tk), lambda i, j, s: (s, i)), pl.BlockSpec((tt, tn), lambda i, j, s: (s, j))],
        out_specs=pl.BlockSpec((tk, tn), lambda i, j, s: (i, j)),
        out_shape=jax.ShapeDtypeStruct((k, n), F32),
        compiler_params=_arb(3),
    )(x, y)


FFN_CHUNKS = 2


class _FfnGather:
    def __init__(self, gu_sh, d_sh, gu_full, d_full, send_sems, recv_sems, local_sems):
        self.sh, self.full = (gu_sh, d_sh), (gu_full, d_full)
        self.send_sems, self.recv_sems, self.local_sems = send_sems, recv_sems, local_sems
        self.mx, self.my, self.mc = _me()
        self.me = 2 * self.mx + self.my
        self.w4, self.f4 = gu_sh.shape[1], d_sh.shape[0]
        self.hg, self.hd = gu_sh.shape[0] // 2, d_sh.shape[0] // 2

    def _window(self, a, chip, core=None):
        gu_full, d_full = self.full
        if a == 0:
            rows = pl.ds(0, 2 * self.hg) if core is None else pl.ds(pl.multiple_of(core * self.hg, 16), self.hg)
            return gu_full.at[rows, pl.ds(pl.multiple_of(chip * self.w4, LANE), self.w4)]
        if core is None:
            return d_full.at[pl.ds(pl.multiple_of(chip * self.f4, 16), self.f4), :]
        return d_full.at[pl.ds(pl.multiple_of(chip * self.f4 + core * self.hd, 16), self.hd), :]

    def _half(self, a):
        n = (self.hg, self.hd)[a]
        return self.sh[a].at[pl.ds(pl.multiple_of(self.mc * n, 16), n), :]

    def _peer(self, k):
        return _flip(self.mx, k >> 1), _flip(self.my, k & 1)

    def _fetch(self, k, a):
        px, py = self._peer(k)
        i = 2 * (k - 1) + a
        return pltpu.make_async_remote_copy(src_ref=self._half(a), dst_ref=self._window(a, self.me, self.mc),
                                            send_sem=self.send_sems.at[i], recv_sem=self.recv_sems.at[i],
                                            device_id=(px, py, self.mc), device_id_type=MESH)

    def _relay(self, k, a):
        px, py = self._peer(k)
        got = self._window(a, 2 * px + py, self.mc)
        i = 6 + 2 * (k - 1) + a
        return pltpu.make_async_remote_copy(src_ref=got, dst_ref=got, send_sem=self.send_sems.at[i],
                                            recv_sem=self.recv_sems.at[i], device_id=(self.mx, self.my, 1 - self.mc),
                                            device_id_type=MESH)

    def _own(self, a):
        return pltpu.make_async_copy(self.sh[a], self._window(a, self.me), self.local_sems.at[a])

    def start(self):
        for a in (0, 1):
            self._own(a).start()
            for k in (1, 2, 3):
                self._fetch(k, a).start()

    def relay(self):
        for k in (1, 2, 3):
            for a in (0, 1):
                self._fetch(k, a).wait_recv()
                self._relay(k, a).start()

    def finish(self):
        for k in (1, 2, 3):
            for a in (0, 1):
                self._fetch(k, a).wait_send()
                self._relay(k, a).wait()
        for a in (0, 1):
            self._own(a).wait()


def ffn_fwd(h, nw, wgu, wd, nxt=None):
    t, d = h.shape
    f = wd.shape[0]
    fc = f // FFN_CHUNKS
    tm = _tile(t, 512)
    nsteps = t // tm

    def body(h_ref, nw_ref, wgu_hbm, wd_hbm, *rest):
        if nxt is None:
            o_ref, gu_ref, a_ref, wgu_ref, wd_ref, sem = rest
            gather = None
        else:
            (gu_sh, d_sh, o_ref, gu_ref, a_ref, gu_full, d_full, wgu_ref, wd_ref, sem,
             send_sems, recv_sems, local_sems) = rest
            gather = _FfnGather(gu_sh, d_sh, gu_full, d_full, send_sems, recv_sems, local_sems)
            pl.when(pl.program_id(0) == 0)(gather.start)
        _resident(wgu_hbm, wgu_ref, sem)
        _resident(wd_hbm, wd_ref, sem)
        if gather is not None:
            pl.when(pl.program_id(0) == nsteps // 2)(gather.relay)
        x = h_ref[...]
        hn = _rms(x, nw_ref[...])[0].astype(BF16)
        acc = jnp.zeros((tm, d), F32)
        for c in range(FFN_CHUNKS):
            gs, us = slice(c * fc, (c + 1) * fc), slice(f + c * fc, f + (c + 1) * fc)
            g = _dot(hn, wgu_ref[:, gs])
            u = _dot(hn, wgu_ref[:, us])
            a = (_silu(g) * u).astype(BF16)
            gu_ref[:, gs] = g.astype(BF16)
            gu_ref[:, us] = u.astype(BF16)
            a_ref[:, gs] = a
            acc = acc + _dot(a, wd_ref[gs, :])
        o_ref[...] = x + 0.5 * acc
        if gather is not None:
            pl.when(pl.program_id(0) == nsteps - 1)(gather.finish)

    row = lambda w: pl.BlockSpec((tm, w), lambda i: (i, 0))
    in_specs = [row(d), pl.BlockSpec((1, d), lambda i: (0, 0)), ANY, ANY]
    out_specs = [row(d), row(2 * f), row(f)]
    out_shape = [jax.ShapeDtypeStruct((t, d), F32), jax.ShapeDtypeStruct((t, 2 * f), BF16),
                 jax.ShapeDtypeStruct((t, f), BF16)]
    scratch = [pltpu.VMEM((d, 2 * f), BF16), pltpu.VMEM((f, d), BF16), pltpu.SemaphoreType.DMA]
    args = (h, nw, wgu, wd)
    if nxt is not None:
        in_specs += [ANY, ANY]
        out_specs += [ANY, ANY]
        out_shape += [jax.ShapeDtypeStruct((d, 2 * f), BF16), jax.ShapeDtypeStruct((f, d), BF16)]
        scratch += [pltpu.SemaphoreType.DMA((12,)), pltpu.SemaphoreType.DMA((12,)), pltpu.SemaphoreType.DMA((2,))]
        args += tuple(nxt)
    return pl.pallas_call(
        body, name="ffn_fwd_gather" if nxt is not None else "ffn_fwd",
        grid=(nsteps,),
        in_specs=in_specs, out_specs=out_specs, out_shape=out_shape, scratch_shapes=scratch,
        compiler_params=_arb(1),
    )(*args)


def ffn_bwd(h, nw, wgu, wd, gu, dh):
    t, d = h.shape
    f = wd.shape[0]
    fc = f // FFN_CHUNKS
    tm = _tile(t, 256)

    def body(h_ref, nw_ref, wgu_hbm, wd_hbm, gu_ref, dh_ref, o_ref, hn_ref, dgu_ref, dnw_ref, wgu_ref, wd_ref, sem):
        _resident(wgu_hbm, wgu_ref, sem)
        _resident(wd_hbm, wd_ref, sem)

        @pl.when(pl.program_id(0) == 0)
        def _():
            dnw_ref[...] = jnp.zeros_like(dnw_ref)

        nw = nw_ref[...]
        hn32, xhat, r = _rms(h_ref[...], nw)
        hn_ref[...] = hn32.astype(BF16)
        dh = dh_ref[...]
        dout = (0.5 * dh).astype(BF16)
        dhn = jnp.zeros((tm, d), F32)
        for c in range(FFN_CHUNKS):
            gs, us = slice(c * fc, (c + 1) * fc), slice(f + c * fc, f + (c + 1) * fc)
            g = gu_ref[:, gs].astype(F32)
            u = gu_ref[:, us].astype(F32)
            sg = _sigmoid(g)
            sil = g * sg
            da = _dot_nt(dout, wd_ref[gs, :])
            dg = (da * u * (sg * (1.0 + g * (1.0 - sg)))).astype(BF16)
            du = (da * sil).astype(BF16)
            dgu_ref[:, gs] = dg
            dgu_ref[:, us] = du
            dhn = dhn + _dot_nt(dg, wgu_ref[:, gs]) + _dot_nt(du, wgu_ref[:, us])
        dx, dnw = _rms_bwd(dhn, nw, xhat, r)
        dnw_ref[...] += dnw
        o_ref[...] = dh + dx

    row = pl.BlockSpec((tm, d), lambda i: (i, 0))
    vec = pl.BlockSpec((1, d), lambda i: (0, 0))
    return pl.pallas_call(
        body, name="ffn_bwd",
        grid=(t // tm,),
        in_specs=[row, vec, ANY, ANY, pl.BlockSpec((tm, 2 * f), lambda i: (i, 0)), row],
        out_specs=[row, row, pl.BlockSpec((tm, 2 * f), lambda i: (i, 0)), vec],
        out_shape=[jax.ShapeDtypeStruct((t, d), F32), jax.ShapeDtypeStruct((t, d), BF16),
                   jax.ShapeDtypeStruct((t, 2 * f), BF16), jax.ShapeDtypeStruct((1, d), F32)],
        scratch_shapes=[pltpu.VMEM((d, 2 * f), BF16), pltpu.VMEM((f, d), BF16), pltpu.SemaphoreType.DMA],
        compiler_params=_arb(1),
    )(h, nw, wgu, wd, gu, dh)


def loss_head(h, nw, target):
    t, d = h.shape
    tm = _tile(t, 512)

    def body(h_ref, nw_ref, tg_ref, dh_ref, loss_ref, dnw_ref):
        @pl.when(pl.program_id(0) == 0)
        def _():
            loss_ref[...] = jnp.zeros_like(loss_ref)
            dnw_ref[...] = jnp.zeros_like(dnw_ref)

        nw = nw_ref[...]
        y, xhat, r = _rms(h_ref[...], nw)
        e = y - tg_ref[...]
        loss_ref[...] += 0.5 * jnp.sum(jnp.mean(e * e, axis=-1, keepdims=True), axis=0, keepdims=True)
        dx, dnw = _rms_bwd(e * (1.0 / d), nw, xhat, r)
        dnw_ref[...] += dnw
        dh_ref[...] = dx

    row = pl.BlockSpec((tm, d), lambda i: (i, 0))
    vec = pl.BlockSpec((1, d), lambda i: (0, 0))
    return pl.pallas_call(
        body, name="loss_head",
        grid=(t // tm,),
        in_specs=[row, vec, row],
        out_specs=[row, pl.BlockSpec((8, LANE), lambda i: (0, 0)), vec],
        out_shape=[jax.ShapeDtypeStruct((t, d), F32), jax.ShapeDtypeStruct((8, LANE), F32),
                   jax.ShapeDtypeStruct((1, d), F32)],
        compiler_params=_arb(1),
    )(h, nw, target)


def adamw(w, g, m, v):
    r, c = w.shape
    tr = r
    while tr * c * 4 > (1 << 20) and tr % 16 == 0:
        tr //= 2

    def body(w_ref, g_ref, m_ref, v_ref, d_ref, nm_ref, nv_ref):
        g = g_ref[...]
        m = ADAM_B1 * m_ref[...] + (1.0 - ADAM_B1) * g
        v = ADAM_B2 * v_ref[...] + (1.0 - ADAM_B2) * (g * g)
        m_hat = m / (1.0 - ADAM_B1 ** ADAM_STEP)
        v_hat = v / (1.0 - ADAM_B2 ** ADAM_STEP)
        d_ref[...] = -ADAM_LR * (m_hat / (jnp.sqrt(v_hat) + ADAM_EPS) + ADAM_WD * w_ref[...])
        nm_ref[...] = m
        nv_ref[...] = v

    blk = pl.BlockSpec((tr, c), lambda i: (i, 0))
    return pl.pallas_call(
        body, name="adamw",
        grid=(r // tr,),
        in_specs=[blk] * 4, out_specs=[blk] * 3,
        out_shape=[jax.ShapeDtypeStruct((r, c), F32)] * 3,
        compiler_params=_arb(1),
    )(w, g, m, v)


ATTN_P_W = ATTN_Q_W + 2 * ATTN_KV_W
ATTN_SCALE = ATTN_HEAD_DIM ** -0.5


def _rope_group(t, tab, sign):
    return (t * tab[:, 0:LANE] + sign * pltpu.roll(t, 8, 1) * tab[:, LANE:2 * LANE]
            + sign * pltpu.roll(t, LANE - 8, 1) * tab[:, 2 * LANE:3 * LANE])


def _rope(t, tab, sign=1.0):
    return jnp.concatenate([_rope_group(t[:, s:s + LANE], tab, sign) for s in range(0, t.shape[1], LANE)], axis=1)


def _attn_heads(qs, ks, vs, sinks, first):
    b = ATTN_BLOCK
    rows = ATTN_GROUP * b
    qi = _iota2((rows, 2 * b), 0) % b
    kj = _iota2((rows, 2 * b), 1)
    rel = qi + b - kj
    valid = (rel >= 0) & (rel < b) & ((kj >= b) | jnp.logical_not(first))
    ss = _each(lambda q, k: jnp.where(valid, _mm1_nt(q, k) * ATTN_SCALE, NEG), qs, ks)
    ms = _each(lambda s, sink: lax.stop_gradient(jnp.maximum(jnp.max(s, axis=-1, keepdims=True), sink)), ss, sinks)
    ps = _each(lambda s, m: jnp.exp(s - m), ss, ms)
    dens = _each(lambda p, sink, m: jnp.sum(p, axis=-1, keepdims=True) + jnp.exp(sink - m), ps, sinks, ms)
    return _each(lambda p, den, v: _mm1(p / den, v), ps, dens, vs)


def _attn_prepare(p_ref, kvp_ref, tab_ref, tabp_ref, sink_ref):
    b = ATTN_BLOCK
    hd = ATTN_HEAD_DIM
    tab, tabp = tab_ref[...], tabp_ref[...]
    q = _rope(p_ref[:, 0:ATTN_Q_W], tab)
    kc = _rope(p_ref[:, ATTN_Q_W:ATTN_Q_W + ATTN_KV_W], tab)
    kp = _rope(kvp_ref[:, 0:ATTN_KV_W], tabp)
    vc = p_ref[:, ATTN_Q_W + ATTN_KV_W:ATTN_P_W]
    vp = kvp_ref[:, ATTN_KV_W:2 * ATTN_KV_W]
    sk = sink_ref[...]
    qs, ks, vs, sinks = [], [], [], []
    for h in range(ATTN_KV_HEADS):
        heads = [ATTN_GROUP * h + g for g in range(ATTN_GROUP)]
        qs.append(jnp.concatenate([q[:, i * hd:(i + 1) * hd] for i in heads], axis=0))
        ks.append(jnp.concatenate([kp[:, h * hd:(h + 1) * hd], kc[:, h * hd:(h + 1) * hd]], axis=0))
        vs.append(jnp.concatenate([vp[:, h * hd:(h + 1) * hd], vc[:, h * hd:(h + 1) * hd]], axis=0))
        sinks.append(jnp.concatenate([jnp.broadcast_to(sk[:, i:i + 1], (b, 1)) for i in heads], axis=0))
    return qs, ks, vs, sinks


def _unstack_heads(xs):
    b = ATTN_BLOCK
    return jnp.concatenate([x[g * b:(g + 1) * b] for x in xs for g in range(ATTN_GROUP)], axis=1)


def _attn_specs(nb):
    b = ATTN_BLOCK
    prev = lambda n: jnp.maximum(n - 1, 0)
    return [pl.BlockSpec((b, ATTN_P_W), lambda n: (n, 0)),
            pl.BlockSpec((b, 2 * ATTN_KV_W), lambda n: (prev(n), ATTN_Q_W // (2 * ATTN_KV_W))),
            pl.BlockSpec((b, 3 * LANE), lambda n: (n, 0)),
            pl.BlockSpec((b, 3 * LANE), lambda n: (prev(n), 0)),
            pl.BlockSpec((1, ATTN_Q_HEADS), lambda n: (0, 0))]


def attn_fwd(p, tab, sinks):
    t = p.shape[0]
    nb = t // ATTN_BLOCK

    def body(p_ref, kvp_ref, tab_ref, tabp_ref, sink_ref, o_ref):
        qs, ks, vs, sk = _attn_prepare(p_ref, kvp_ref, tab_ref, tabp_ref, sink_ref)
        os_ = _attn_heads(qs, ks, vs, sk, pl.program_id(0) == 0)
        o_ref[...] = _unstack_heads(os_).astype(BF16)

    return pl.pallas_call(
        body, name="attn_fwd",
        grid=(nb,),
        in_specs=_attn_specs(nb),
        out_specs=pl.BlockSpec((ATTN_BLOCK, ATTN_Q_W), lambda n: (n, 0)),
        out_shape=jax.ShapeDtypeStruct((t, ATTN_Q_W), BF16),
        compiler_params=_arb(1),
    )(p, p, tab, tab, sinks)


def attn_bwd(p, tab, sinks, do):
    t = p.shape[0]
    b = ATTN_BLOCK
    hd = ATTN_HEAD_DIM
    nb = t // b

    def body(p_ref, kvp_ref, tab_ref, tabp_ref, sink_ref, do_ref, dq_ref, dkvc_ref, dkvp_ref, dsink_ref):
        n = pl.program_id(0)

        @pl.when(n == 0)
        def _():
            dsink_ref[...] = jnp.zeros_like(dsink_ref)

        qs, ks, vs, sk = _attn_prepare(p_ref, kvp_ref, tab_ref, tabp_ref, sink_ref)
        first = n == 0
        _, vjp = jax.vjp(lambda a, bb, c, d: _attn_heads(a, bb, c, d, first), qs, ks, vs, sk)
        do = do_ref[...]
        dos = [jnp.concatenate([do[:, i * hd:(i + 1) * hd] for i in range(ATTN_GROUP * h, ATTN_GROUP * (h + 1))], axis=0)
               for h in range(ATTN_KV_HEADS)]
        dqs, dks, dvs, dsk = vjp(dos)
        tab, tabp = tab_ref[...], tabp_ref[...]
        dq_ref[...] = _rope(_unstack_heads([x.astype(F32) for x in dqs]), tab, -1.0).astype(BF16)
        dkc = jnp.concatenate([x.astype(F32)[b:] for x in dks], axis=1)
        dkp = jnp.concatenate([x.astype(F32)[:b] for x in dks], axis=1)
        dkvc_ref[:, 0:ATTN_KV_W] = _rope(dkc, tab, -1.0)
        dkvp_ref[:, 0:ATTN_KV_W] = _rope(dkp, tabp, -1.0)
        dkvc_ref[:, ATTN_KV_W:] = jnp.concatenate([x.astype(F32)[b:] for x in dvs], axis=1)
        dkvp_ref[:, ATTN_KV_W:] = jnp.concatenate([x.astype(F32)[:b] for x in dvs], axis=1)
        parts = [jnp.sum(d[g * b:(g + 1) * b], axis=0, keepdims=True) for d in dsk for g in range(ATTN_GROUP)]
        dsink_ref[...] += jnp.concatenate(parts, axis=1)

    blk = lambda w: pl.BlockSpec((b, w), lambda n: (n, 0))
    return pl.pallas_call(
        body, name="attn_bwd",
        grid=(nb,),
        in_specs=_attn_specs(nb) + [blk(ATTN_Q_W)],
        out_specs=[blk(ATTN_Q_W), blk(2 * ATTN_KV_W), blk(2 * ATTN_KV_W),
                   pl.BlockSpec((1, ATTN_Q_HEADS), lambda n: (0, 0))],
        out_shape=[jax.ShapeDtypeStruct((t, ATTN_Q_W), BF16), jax.ShapeDtypeStruct((t, 2 * ATTN_KV_W), F32),
                   jax.ShapeDtypeStruct((t, 2 * ATTN_KV_W), F32), jax.ShapeDtypeStruct((1, ATTN_Q_HEADS), F32)],
        compiler_params=_arb(1),
    )(p, p, tab, tab, sinks, do)


def kv_combine(dkvc, dkvp):
    t, w = dkvc.shape
    b = ATTN_BLOCK
    nb = t // b

    def body(c_ref, p_ref, o_ref):
        nxt = jnp.where(pl.program_id(0) < nb - 1, p_ref[...], 0.0)
        o_ref[...] = (c_ref[...] + nxt).astype(BF16)

    return pl.pallas_call(
        body, name="kv_combine",
        grid=(nb,),
        in_specs=[pl.BlockSpec((b, w), lambda n: (n, 0)),
                  pl.BlockSpec((b, w), lambda n: (jnp.minimum(n + 1, nb - 1), 0))],
        out_specs=pl.BlockSpec((b, w), lambda n: (n, 0)),
        out_shape=jax.ShapeDtypeStruct((t, w), BF16),
        compiler_params=_arb(1),
    )(dkvc, dkvp)


def _me():
    return lax.axis_index("x"), lax.axis_index("y"), lax.axis_index("c")


def _flip(v, bit):
    return 1 - v if bit else v


def _chip_index():
    return 2 * lax.axis_index("x") + lax.axis_index("y")


def allgather_chips(x, split):
    r, c = x.shape
    rh = r // 2 if split else r

    def body(x_ref, o_ref, send_sems, recv_sems):
        mx, my, mc = _me()
        me = 2 * mx + my
        rows = pl.ds(pl.multiple_of(mc * rh, 16), rh) if split else pl.ds(0, r)
        fetched, passed = [], []
        for k in (1, 2, 3):
            peer = (_flip(mx, k >> 1), _flip(my, k & 1), mc)
            cp = pltpu.make_async_remote_copy(src_ref=x_ref.at[rows], dst_ref=o_ref.at[me, rows],
                                              send_sem=send_sems.at[k - 1], recv_sem=recv_sems.at[k - 1],
                                              device_id=peer, device_id_type=MESH)
            cp.start()
            fetched.append(cp)
        if split:
            for k in (1, 2, 3):
                fetched[k - 1].wait_recv()
                theirs = o_ref.at[2 * _flip(mx, k >> 1) + _flip(my, k & 1), rows]
                cp = pltpu.make_async_remote_copy(src_ref=theirs, dst_ref=theirs, send_sem=send_sems.at[2 + k],
                                                  recv_sem=recv_sems.at[2 + k], device_id=(mx, my, 1 - mc),
                                                  device_id_type=MESH)
                cp.start()
                passed.append(cp)
            for cp in fetched:
                cp.wait_send()
            for cp in passed:
                cp.wait()
        else:
            for cp in fetched:
                cp.wait()

    got = pl.pallas_call(
        body, name="allgather_chips",
        in_specs=[ANY], out_specs=ANY,
        out_shape=jax.ShapeDtypeStruct((4, r, c), x.dtype),
        scratch_shapes=[pltpu.SemaphoreType.DMA((6,)), pltpu.SemaphoreType.DMA((6,))],
    )(x)
    return lax.dynamic_update_slice(got, x[None], (_chip_index(), 0, 0))


def pair_exchange(g):
    _, _, r, c = g.shape

    def body(g_ref, got_ref, send_sems, recv_sems):
        mx, my, mc = _me()
        sib = (mx, my, 1 - mc)
        copies = []
        for j in range(4):
            cp = pltpu.make_async_remote_copy(src_ref=g_ref.at[j, 1 - mc], dst_ref=got_ref.at[j], send_sem=send_sems.at[j],
                                              recv_sem=recv_sems.at[j], device_id=sib, device_id_type=MESH)
            cp.start()
            copies.append(cp)
        for cp in copies:
            cp.wait()

    return pl.pallas_call(
        body, name="pair_exchange",
        in_specs=[ANY], out_specs=ANY,
        out_shape=jax.ShapeDtypeStruct((4, r, c), g.dtype),
        scratch_shapes=[pltpu.SemaphoreType.DMA((4,))] * 2,
    )(g)


def scatter_chips(p):
    _, r, c = p.shape

    def body(p_ref, o_ref, send_sems, recv_sems):
        mx, my, mc = _me()
        me = 2 * mx + my
        copies = []
        for k in (1, 2, 3):
            px, py = _flip(mx, k >> 1), _flip(my, k & 1)
            cp = pltpu.make_async_remote_copy(src_ref=p_ref.at[2 * px + py], dst_ref=o_ref.at[me],
                                              send_sem=send_sems.at[k - 1], recv_sem=recv_sems.at[k - 1],
                                              device_id=(px, py, mc), device_id_type=MESH)
            cp.start()
            copies.append(cp)
        for cp in copies:
            cp.wait()

    return pl.pallas_call(
        body, name="scatter_chips",
        in_specs=[ANY], out_specs=ANY,
        out_shape=jax.ShapeDtypeStruct((4, r, c), p.dtype),
        scratch_shapes=[pltpu.SemaphoreType.DMA((3,)), pltpu.SemaphoreType.DMA((3,))],
    )(p)


def pair_share(o):
    r, c = o.shape

    def body(o_ref, out_ref, send_sem, recv_sem):
        mx, my, mc = _me()
        cp = pltpu.make_async_remote_copy(src_ref=o_ref, dst_ref=out_ref, send_sem=send_sem, recv_sem=recv_sem,
                                          device_id=(mx, my, 1 - mc), device_id_type=MESH)
        cp.start()
        cp.wait()

    return pl.pallas_call(
        body, name="pair_share",
        in_specs=[ANY], out_specs=ANY,
        out_shape=jax.ShapeDtypeStruct((r, c), o.dtype),
        scratch_shapes=[pltpu.SemaphoreType.DMA, pltpu.SemaphoreType.DMA],
    )(o)


def _row_tile(r, c):
    tr = 8
    while r % (2 * tr) == 0 and 2 * tr * c * 4 <= (2 << 20):
        tr *= 2
    return tr


def add_pair(g, got, core):
    _, _, r, c = g.shape
    tr = _row_tile(r, c)

    def body(core_ref, g_ref, got_ref, p32_ref, p16_ref):
        s = g_ref[...] + got_ref[...]
        p32_ref[...] = s
        p16_ref[...] = s.astype(BF16)

    blk = pl.BlockSpec((None, tr, c), lambda j, i, core_ref: (j, i, 0))
    return pl.pallas_call(
        body, name="add_pair",
        grid_spec=pltpu.PrefetchScalarGridSpec(
            num_scalar_prefetch=1, grid=(4, r // tr),
            in_specs=[pl.BlockSpec((None, None, tr, c), lambda j, i, core_ref: (j, core_ref[0], i, 0)), blk],
            out_specs=[blk, blk]),
        out_shape=[jax.ShapeDtypeStruct((4, r, c), F32), jax.ShapeDtypeStruct((4, r, c), BF16)],
        compiler_params=_arb(2),
    )(core, g, got)


def sum_slots(p32, q16, order):
    _, r, c = p32.shape
    tr = _row_tile(r, c)

    def body(order_ref, own_ref, a_ref, b_ref, c_ref, o_ref):
        o_ref[...] = ((own_ref[...] + a_ref[...].astype(F32)) + b_ref[...].astype(F32)) + c_ref[...].astype(F32)

    slot = lambda k: pl.BlockSpec((None, tr, c), functools.partial(lambda k, i, order_ref: (order_ref[k], i, 0), k))
    return pl.pallas_call(
        body, name="sum_slots",
        grid_spec=pltpu.PrefetchScalarGridSpec(
            num_scalar_prefetch=1, grid=(r // tr,),
            in_specs=[slot(0), slot(1), slot(2), slot(3)],
            out_specs=pl.BlockSpec((tr, c), lambda i, order_ref: (i, 0))),
        out_shape=jax.ShapeDtypeStruct((r, c), F32),
        compiler_params=_arb(1),
    )(order, p32, q16, q16, q16)


def allreduce_small(x):
    r, c = x.shape

    def body(x_ref, o_ref, buf, send_sems, recv_sems):
        mx, my, mc = _me()
        me = 4 * mx + 2 * my + mc
        buf[pl.ds(me, 1)] = x_ref[...][None]
        copies = []
        for k in range(1, 8):
            peer = (_flip(mx, k >> 2), _flip(my, (k >> 1) & 1), _flip(mc, k & 1))
            cp = pltpu.make_async_remote_copy(src_ref=x_ref, dst_ref=buf.at[me], send_sem=send_sems.at[k - 1],
                                              recv_sem=recv_sems.at[k - 1], device_id=peer, device_id_type=MESH)
            cp.start()
            copies.append(cp)
        for cp in copies:
            cp.wait()
        acc = buf[0]
        for d in range(1, 8):
            acc = acc + buf[d]
        o_ref[...] = acc

    return pl.pallas_call(
        body, name="allreduce_small",
        out_shape=jax.ShapeDtypeStruct((r, c), F32),
        scratch_shapes=[pltpu.VMEM((8, r, c), F32), pltpu.SemaphoreType.DMA((7,)), pltpu.SemaphoreType.DMA((7,))],
    )(x)


def reduce_scatter(g):
    _, r, c = g.shape
    rh = r // 2
    core = lax.axis_index("c")
    me = _chip_index()
    g = g.reshape(4, 2, rh, c)
    p32, p16 = add_pair(g, pair_exchange(g), core.astype(jnp.int32)[None])
    q16 = scatter_chips(p16)
    order = jnp.stack([me, me ^ 1, me ^ 2, me ^ 3]).astype(jnp.int32)
    mine = sum_slots(p32, q16, order)
    theirs = pair_share(mine)
    return jnp.where(core == 0, jnp.concatenate([mine, theirs]), jnp.concatenate([theirs, mine]))


PACK_W = 1024
PACK_ROWS = 1024

SHARDED = {"ffn1_w_gate_up": 1, "ffn1_w_down": 0, "ffn2_w_gate_up": 1, "ffn2_w_down": 0, "attn_w_in": 1,
           "attn_w_out": 0, "gdn_w_in": 1, "gdn_w_out": 0, "gdn_conv_w": 1}
REPLICATED = ["ffn1_norm", "mix_norm", "ffn2_norm", "attn_b_in", "attn_sinks", "attn_b_out", "gdn_A_log",
              "gdn_dt_bias", "gdn_norm_w", "final_norm"]
WEIGHTS = ["ffn1_norm", "ffn1_w_gate_up", "ffn1_w_down", "mix_norm", "ffn2_norm", "ffn2_w_gate_up", "ffn2_w_down",
           "attn_w_in", "attn_b_in", "attn_sinks", "attn_w_out", "attn_b_out", "gdn_w_in", "gdn_conv_w", "gdn_A_log",
           "gdn_dt_bias", "gdn_norm_w", "gdn_w_out", "final_norm"]


def _pack_rows(flats, dtype, width, row_multiple):
    flat = jnp.concatenate([f.astype(dtype).reshape(-1) for f in flats])
    per = width * row_multiple
    pad = (-flat.shape[0]) % per
    if pad:
        flat = jnp.concatenate([flat, jnp.zeros((pad,), dtype)])
    return flat.reshape(-1, width)


def _by_width(shapes):
    groups = {}
    for n, shape in shapes.items():
        groups.setdefault(shape[-1], []).append(n)
    return groups


def _gather_weights(shards, dtype, split=True):
    full = {}
    for width, names in _by_width({n: s.shape for n, s in shards.items()}).items():
        packed = jnp.concatenate([shards[n].astype(dtype).reshape(-1, width) for n in names], axis=0)
        got = allgather_chips(packed, split)
        off = 0
        for n in names:
            layers, a, _ = shards[n].shape
            full[n] = [jnp.concatenate([got[j, off + l * a:off + (l + 1) * a] for j in range(4)], axis=SHARDED[n])
                       for l in range(layers)]
            off += layers * a
    return full


def _scatter_grads(grads, shard_shapes):
    out = {}
    for _, names in _by_width(shard_shapes).items():
        slots = []
        for j in range(4):
            parts = []
            for n in names:
                w = shard_shapes[n][1 + SHARDED[n]]
                parts += [lax.slice_in_dim(gl, j * w, (j + 1) * w, axis=SHARDED[n]) for gl in grads[n]]
            slots.append(jnp.concatenate(parts, axis=0))
        red = reduce_scatter(jnp.stack(slots))
        off = 0
        for n in names:
            layers, a, _ = shard_shapes[n]
            out[n] = red[off:off + layers * a].reshape(shard_shapes[n])
            off += layers * a
    return out


def _small_pack(items):
    rows = []
    for a in items:
        f = a.astype(F32).reshape(-1)
        pad = (-f.shape[0]) % LANE
        rows.append(jnp.concatenate([f, jnp.zeros((pad,), F32)]) if pad else f)
    return _pack_rows(rows, F32, LANE, 8)


def _small_unpack(buf, shapes):
    flat = buf.reshape(-1)
    out, off = [], 0
    for shape in shapes:
        size = 1
        for s in shape:
            size *= s
        out.append(flat[off:off + size].reshape(shape))
        off += size + (-size) % LANE
    return out


def _rope_table(positions):
    t = positions.shape[0]
    inv_freq = ROPE_THETA ** (-jnp.arange(0, ROPE_DIM, 2, dtype=F32) / ROPE_DIM)
    ang = positions.astype(F32)[:, None] * inv_freq
    cos, sin = jnp.cos(ang), jnp.sin(ang)
    rest = ATTN_HEAD_DIM - ROPE_DIM
    zeros = lambda n: jnp.zeros((t, n), F32)
    c64 = jnp.concatenate([cos, cos, jnp.ones((t, rest), F32)], axis=1)
    s_up = jnp.concatenate([zeros(ROPE_DIM // 2), sin, zeros(rest)], axis=1)
    s_dn = jnp.concatenate([-sin, zeros(ROPE_DIM // 2 + rest)], axis=1)
    return jnp.concatenate([c64, c64, s_up, s_up, s_dn, s_dn], axis=1)


def _as2d(a):
    return a.reshape(-1, a.shape[-1]) if a.ndim > 1 else a.reshape(1, -1)


def kernel(x, positions, ffn1_norm, ffn1_w_gate_up, ffn1_w_down, mix_norm, ffn2_norm, ffn2_w_gate_up, ffn2_w_down, attn_w_in, attn_b_in, attn_sinks, attn_w_out, attn_b_out, gdn_w_in, gdn_conv_w, gdn_A_log, gdn_dt_bias, gdn_norm_w, gdn_w_out, final_norm, loss_target, m_ffn1_norm, m_ffn1_w_gate_up, m_ffn1_w_down, m_mix_norm, m_ffn2_norm, m_ffn2_w_gate_up, m_ffn2_w_down, m_attn_w_in, m_attn_b_in, m_attn_sinks, m_attn_w_out, m_attn_b_out, m_gdn_w_in, m_gdn_conv_w, m_gdn_A_log, m_gdn_dt_bias, m_gdn_norm_w, m_gdn_w_out, m_final_norm, v_ffn1_norm, v_ffn1_w_gate_up, v_ffn1_w_down, v_mix_norm, v_ffn2_norm, v_ffn2_w_gate_up, v_ffn2_w_down, v_attn_w_in, v_attn_b_in, v_attn_sinks, v_attn_w_out, v_attn_b_out, v_gdn_w_in, v_gdn_conv_w, v_gdn_A_log, v_gdn_dt_bias, v_gdn_norm_w, v_gdn_w_out, v_final_norm):
    given = dict(locals())
    w = {n: given[n] for n in WEIGHTS}
    d = D_MODEL
    h = x[0]
    target = loss_target[0]
    depth = ffn1_norm.shape[0]

    big = [n for n in SHARDED if n != "gdn_conv_w"]
    up_front = {n: (w[n][0:1] if n.startswith("ffn1") else w[n]) for n in big if not n.startswith("ffn2")}
    full = _gather_weights(up_front, BF16)
    ffn_order = [(tag, l) for l in range(depth) for tag in ("ffn1", "ffn2")]
    ffn_w = {ffn_order[0]: (full["ffn1_w_gate_up"][0], full["ffn1_w_down"][0])}

    def run_ffn(tag, l, h):
        i = ffn_order.index((tag, l))
        wgu, wd = ffn_w[(tag, l)]
        nw = w[tag + "_norm"][l][None]
        if i + 1 == len(ffn_order):
            return ffn_fwd(h, nw, wgu, wd)
        ntag, nl = ffn_order[i + 1]
        shards = (w[ntag + "_w_gate_up"][nl].astype(BF16), w[ntag + "_w_down"][nl].astype(BF16))
        h, gu, a, ngu, nd = ffn_fwd(h, nw, wgu, wd, shards)
        ffn_w[(ntag, nl)] = (ngu, nd)
        return h, gu, a

    conv_full = _gather_weights({"gdn_conv_w": gdn_conv_w}, F32, split=False)["gdn_conv_w"]
    tab = _rope_table(positions[0])
    zero_d = jnp.zeros((1, d), F32)

    def gdn_params(j):
        w_in = full["gdn_w_in"][j]
        w_cat = jnp.concatenate([w_in, jnp.zeros((d, GDN_PROJ_W - w_in.shape[1]), BF16)], axis=1)
        conv = jnp.concatenate([conv_full[j], jnp.zeros((8 - GDN_CONV, GDN_QKV_W), F32)], axis=0)
        lanes = lambda vec: jnp.concatenate([jnp.zeros((GC_G,), F32), vec, jnp.zeros((LANE - GC_G - GDN_HEADS,), F32)])
        par = jnp.concatenate([lanes(gdn_A_log[j])[None], lanes(gdn_dt_bias[j])[None], jnp.zeros((6, LANE), F32)], axis=0)
        return w_cat, conv, par

    saved = []
    for l in range(depth):
        j = l // 2
        rec = {"h1": h}
        h, rec["ffn1_gu"], rec["ffn1_a"] = run_ffn("ffn1", l, h)
        rec["h2"] = h
        if l % 2 == 0:
            p = norm_proj(h, mix_norm[l][None], full["attn_w_in"][j], attn_b_in[j][None])
            o = attn_fwd(p, tab, attn_sinks[j][None])
            h = linear_residual(h, o, full["attn_w_out"][j], attn_b_out[j][None])
            rec.update(p=p, o=o)
        else:
            w_cat, conv, par = gdn_params(j)
            proj = norm_proj(h, mix_norm[l][None], w_cat, jnp.zeros((1, GDN_PROJ_W), F32))
            qkv, gc = gdn_pre_fwd(proj, conv, par)
            on, st = gdn_chunk_fwd(qkv, gc, proj, gdn_norm_w[j][None])
            h = linear_residual(h, on, full["gdn_w_out"][j], zero_d)
            rec.update(proj=proj, qkv=qkv, gc=gc, on=on, st=st, w_cat=w_cat, conv=conv, par=par)
        rec["h3"] = h
        h, rec["ffn2_gu"], rec["ffn2_a"] = run_ffn("ffn2", l, h)
        saved.append(rec)

    dh, loss_tile, d_final = loss_head(h, final_norm[None], target)

    g = {n: [None] * w[n].shape[0] for n in WEIGHTS if n != "final_norm"}
    for l in reversed(range(depth)):
        j = l // 2
        rec = saved[l]

        def ffn_back(tag, h_in, dh):
            wgu, wd = ffn_w[(tag, l)]
            dh_new, hn, dgu, dn = ffn_bwd(h_in, w[tag + "_norm"][l][None], wgu, wd, rec[tag + "_gu"], dh)
            g[tag + "_w_gate_up"][l] = matmul_tn(hn, dgu)
            g[tag + "_w_down"][l] = matmul_tn(rec[tag + "_a"], dh, 0.5)
            g[tag + "_norm"][l] = dn[0]
            return dh_new

        dh = ffn_back("ffn2", rec["h3"], dh)
        if l % 2 == 0:
            w_in, w_out = full["attn_w_in"][j], full["attn_w_out"][j]
            do, db_out = matmul_nt(dh, w_out)
            g["attn_w_out"][j] = matmul_tn(rec["o"], dh)
            g["attn_b_out"][j] = db_out[0]
            dq, dkvc, dkvp, dsink = attn_bwd(rec["p"], tab, attn_sinks[j][None], do)
            dkv = kv_combine(dkvc, dkvp)
            dh, hn, dn, cs_q, cs_kv = norm_proj_bwd(rec["h2"], mix_norm[l][None], dh, [dq, dkv],
                                                    [w_in[:, :ATTN_Q_W], w_in[:, ATTN_Q_W:]])
            g["attn_w_in"][j] = jnp.concatenate([matmul_tn(hn, dq), matmul_tn(hn, dkv)], axis=1)
            g["attn_b_in"][j] = jnp.concatenate([cs_q[0], cs_kv[0]])
            g["attn_sinks"][j] = dsink[0]
        else:
            w_cat, conv, par = rec["w_cat"], rec["conv"], rec["par"]
            d_on, _ = matmul_nt(dh, full["gdn_w_out"][j])
            g["gdn_w_out"][j] = matmul_tn(rec["on"], dh)
            dq, dk, dv, dz, dgc_heads, dnw = gdn_chunk_bwd(rec["qkv"], rec["gc"], rec["proj"], gdn_norm_w[j][None],
                                                           rec["st"], d_on)
            dy, dba, dpar = gdn_pre_bwd(rec["proj"], conv, par, dq, dk, dv, dgc_heads)
            dx, dconv = gdn_conv_bwd(rec["proj"], conv, dy)
            nz = GDN_QKV_W + GDN_HEADS * GDN_DK
            dh, hn, dn, _, _, _ = norm_proj_bwd(rec["h2"], mix_norm[l][None], dh, [dx, dz, dba],
                                                [w_cat[:, :GDN_QKV_W], w_cat[:, GDN_QKV_W:nz], w_cat[:, nz:]])
            g["gdn_w_in"][j] = jnp.concatenate(
                [matmul_tn(hn, dx), matmul_tn(hn, dz), matmul_tn(hn, dba)[:, :2 * GDN_HEADS]], axis=1)
            g["gdn_conv_w"][j] = dconv[:GDN_CONV]
            g["gdn_A_log"][j] = dpar[0, GDN_HEADS:2 * GDN_HEADS]
            g["gdn_dt_bias"][j] = dpar[1, GDN_HEADS:2 * GDN_HEADS]
            g["gdn_norm_w"][j] = dnw[0]
        g["mix_norm"][l] = dn[0]
        dh = ffn_back("ffn1", rec["h1"], dh)
    grad_x = dh[None]

    small_names = REPLICATED + ["gdn_conv_w"]
    local = {n: jnp.stack(g[n]) for n in small_names if n != "final_norm"}
    local["final_norm"] = d_final[0]
    small_shapes = [(1,)] + [local[n].shape for n in small_names]
    small = allreduce_small(_small_pack([loss_tile[0, 0:1]] + [local[n] for n in small_names]))
    small = _small_unpack(small, small_shapes)
    loss = small[0][0]
    grads = dict(zip(small_names, small[1:]))
    conv_cols = gdn_conv_w.shape[2]
    grads["gdn_conv_w"] = lax.dynamic_slice_in_dim(grads["gdn_conv_w"], _chip_index() * conv_cols, conv_cols, axis=2)
    grads.update(_scatter_grads({n: g[n] for n in big}, {n: w[n].shape for n in big}))

    delta, new_m, new_v = {}, {}, {}
    for n in SHARDED:
        dl, nm, nv = adamw(_as2d(w[n]), _as2d(grads[n]), _as2d(given["m_" + n]), _as2d(given["v_" + n]))
        delta[n], new_m[n], new_v[n] = dl.reshape(w[n].shape), nm.reshape(w[n].shape), nv.reshape(w[n].shape)
    shapes = [w[n].shape for n in REPLICATED]
    packed = [_small_pack([src[n] for n in REPLICATED]) for src in
              (w, grads, {n: given["m_" + n] for n in REPLICATED}, {n: given["v_" + n] for n in REPLICATED})]
    for dst, buf in zip((delta, new_m, new_v), adamw(*packed)):
        dst.update(zip(REPLICATED, _small_unpack(buf, shapes)))

    return (loss, grad_x, *[grads[n] for n in WEIGHTS], *[delta[n] for n in WEIGHTS],
            *[new_m[n] for n in WEIGHTS], *[new_v[n] for n in WEIGHTS])
```

```python
import functools

import jax
import jax.numpy as jnp
from jax import lax
from jax.experimental import pallas as pl
from jax.experimental.pallas import tpu as pltpu

F32 = jnp.float32
BF16 = jnp.bfloat16
HI = lax.Precision.HIGHEST
MESH = pl.DeviceIdType.MESH

D_MODEL = 1024
D_FF = 2816
DEPTH = 4
NORM_EPS = 1e-6
LANE = 128

ATTN_Q_HEADS = 16
ATTN_KV_HEADS = 4
ATTN_HEAD_DIM = 64
ATTN_GROUP = 4
ATTN_BLOCK = 128
ROPE_DIM = 16
ROPE_THETA = 500000.0
ATTN_Q_W = 1024
ATTN_KV_W = 256

GDN_HEADS = 8
GDN_DK = 128
GDN_CONV = 4
GDN_CHUNK = 64
GDN_QKV_W = 3072

ADAM_LR = 0.001
ADAM_B1 = 0.9
ADAM_B2 = 0.999
ADAM_EPS = 1e-08
ADAM_WD = 0.01
ADAM_STEP = 10

NEG = -1e30


def _dot(a, b, prec=None):
    return lax.dot_general(a, b, (((1,), (0,)), ((), ())), precision=prec, preferred_element_type=F32)


def _dot_nt(a, b, prec=None):
    return lax.dot_general(a, b, (((1,), (1,)), ((), ())), precision=prec, preferred_element_type=F32)


def _dot_tn(a, b, prec=None):
    return lax.dot_general(a, b, (((0,), (0,)), ((), ())), precision=prec, preferred_element_type=F32)


def _bdot(a, b):
    return _dot(a.astype(BF16), b.astype(BF16))


def _bdot_nt(a, b):
    return _dot_nt(a.astype(BF16), b.astype(BF16))


def _bdot_tn(a, b):
    return _dot_tn(a.astype(BF16), b.astype(BF16))


def _sigmoid(x):
    return 1.0 / (1.0 + jnp.exp(-x))


def _silu(x):
    return x * _sigmoid(x)


def _silu_grad(x):
    s = _sigmoid(x)
    return s * (1.0 + x * (1.0 - s))


def _rms(x, w):
    r = lax.rsqrt(jnp.mean(x * x, axis=-1, keepdims=True) + NORM_EPS)
    xhat = x * r
    return xhat * w, xhat, r


def _rms_bwd(dy, w, xhat, r):
    dxhat = dy * w
    dx = r * (dxhat - xhat * jnp.mean(dxhat * xhat, axis=-1, keepdims=True))
    dw = jnp.sum(dy * xhat, axis=0, keepdims=True)
    return dx, dw


def _arb(n):
    return pltpu.CompilerParams(dimension_semantics=("arbitrary",) * n)


def _tile(n, want):
    t = min(n, want)
    assert n % t == 0, (n, want)
    return t


def _iota2(shape, dim):
    return lax.broadcasted_iota(jnp.int32, shape, dim)


_NN = (((1,), (0,)), ((), ()))
_NT = (((1,), (1,)), ((), ()))
_TN = (((0,), (0,)), ((), ()))


def _raw1(a, b, dn):
    return lax.dot_general(a.astype(BF16), b.astype(BF16), dn, preferred_element_type=F32)


def _raw3(a, b, dn):
    ah, bh = a.astype(BF16), b.astype(BF16)
    al, bl = (a - ah.astype(F32)).astype(BF16), (b - bh.astype(F32)).astype(BF16)
    f = lambda x, y: lax.dot_general(x, y, dn, preferred_element_type=F32)
    return f(ah, bh) + f(ah, bl) + f(al, bh)


def _make_mm(raw):
    @jax.custom_vjp
    def mm(a, b):
        return raw(a, b, _NN)

    mm.defvjp(lambda a, b: (raw(a, b, _NN), (a, b)),
              lambda res, ct: (raw(ct, res[1], _NT), raw(res[0], ct, _TN)))

    @jax.custom_vjp
    def mm_nt(a, b):
        return raw(a, b, _NT)

    mm_nt.defvjp(lambda a, b: (raw(a, b, _NT), (a, b)),
                 lambda res, ct: (raw(ct, res[1], _NN), raw(ct, res[0], _TN)))
    return mm, mm_nt


_mm1, _mm1_nt = _make_mm(_raw1)
_mm3, _mm3_nt = _make_mm(_raw3)


def _eye(n):
    return (_iota2((n, n), 0) == _iota2((n, n), 1)).astype(F32)


def _each(f, *lists):
    return [f(*t) for t in zip(*lists)]


def _inv_newton(mats):
    eye = _eye(mats[0].shape[0])
    ps = [-a for a in mats]
    ms = [eye + p for p in ps]
    k = 1
    while 2 * k < GDN_CHUNK:
        ps = [_raw1(p, p, _NN) for p in ps]
        ms = _each(lambda m, p: m + _raw1(m, p, _NN), ms, ps)
        k *= 2
    rs = _each(lambda a, m: eye - m - _raw3(a, m, _NN), mats, ms)
    return _each(lambda m, r: m + _raw1(m, r, _NN), ms, rs)


@jax.custom_vjp
def _unit_lower_inv(mats):
    return _inv_newton(mats)


def _unit_lower_inv_fwd(mats):
    ms = _inv_newton(mats)
    return ms, ms


def _unit_lower_inv_bwd(ms, dms):
    ts = _each(lambda m, dm: _raw3(m, dm, _TN), ms, dms)
    return (_each(lambda t, m: -_raw3(t, m, _NT), ts, ms),)


_unit_lower_inv.defvjp(_unit_lower_inv_fwd, _unit_lower_inv_bwd)


def _gdn_local(q, k, v, dcb, dcb128, dlb128, bb128):
    n = q[0].shape[0]
    r, s = _iota2((n, n), 0), _iota2((n, n), 1)
    same = (r // GDN_CHUNK) == (s // GDN_CHUNK)
    causal, strict = same & (r >= s), same & (r > s)
    eye = _eye(n)
    decay_l = [jnp.exp(jnp.where(causal, d - d.T, NEG)) for d in dcb]
    kb = _each(lambda a, b: a * b, k, bb128)
    a = _each(lambda x, y, dl: jnp.where(strict, _mm1_nt(x, y) * dl, 0.0), kb, k, decay_l)
    m_off = [m - eye for m in _unit_lower_inv(a)]
    edc = [jnp.exp(d) for d in dcb128]
    rhs = _each(lambda vv, bb, kk, e: jnp.concatenate([vv * bb, kk * e], axis=1), v, bb128, kb, edc)
    sol = _each(lambda x, m: x + _mm3(m, x), rhs, m_off)
    aqk = _each(lambda x, y, dl: jnp.where(causal, _mm1_nt(x, y) * dl, 0.0), q, k, decay_l)
    q_dec = _each(lambda x, e: x * e, q, edc)
    k_dec = _each(lambda x, dl, dc: x * jnp.exp(dl - dc), k, dlb128, dcb128)
    return sol, aqk, q_dec, k_dec


def _gated_norm(o, z, nw):
    r = lax.rsqrt(jnp.mean(o * o, axis=-1, keepdims=True) + NORM_EPS)
    return o * r * nw * _silu(z)


def _gated_norm_bwd(dy, o, z, nw):
    r = lax.rsqrt(jnp.mean(o * o, axis=-1, keepdims=True) + NORM_EPS)
    xhat = o * r
    sz = _silu(z)
    dxhat = dy * nw * sz
    do = r * (dxhat - xhat * jnp.mean(dxhat * xhat, axis=-1, keepdims=True))
    dz = dy * xhat * nw * _silu_grad(z)
    dnw = jnp.sum(dy * xhat * sz, axis=0, keepdims=True)
    return do, dz, dnw


GDN_HEAD_GROUP = 8
GC_BETA, GC_G, GC_DECAY, GC_LAST =0, GDN_HEADS, 2 * GDN_HEADS, 3 * GDN_HEADS


def _gate_cols(gc, h, rows):
    lane = _iota2((rows, LANE), 1)
    col = lambda off: jnp.sum(jnp.where(lane == off + h, gc, 0.0), axis=-1, keepdims=True)
    return col(GC_BETA), col(GC_DECAY), col(GC_LAST)


def _gdn_local_args(q_ref, k_ref, v_ref, gc_ref, hp, n):
    gc = gc_ref[...]
    lanes = [slice(hh * LANE, (hh + 1) * LANE) for hh in range(hp)]
    cols = [_gate_cols(gc, pl.program_id(0) * hp + hh, n) for hh in range(hp)]
    bcast = lambda i, w: [jnp.broadcast_to(c[i], (n, w)) for c in cols]
    args = ([q_ref[:, ls] for ls in lanes], [k_ref[:, ls] for ls in lanes], [v_ref[:, ls] for ls in lanes],
            bcast(1, n), bcast(1, LANE), bcast(2, LANE), bcast(0, LANE))
    return args, [c[2] for c in cols], lanes


def gdn_chunk_fwd(qkv, gc, proj, norm_w):
    t = qkv.shape[0]
    c = GDN_CHUNK
    tc = _tile(t, 256)
    nsub = tc // c

    hp = GDN_HEAD_GROUP

    def body(q_ref, k_ref, v_ref, gc_ref, z_ref, nw_ref, on_ref, st_ref, s_ref):
        @pl.when(pl.program_id(1) == 0)
        def _():
            s_ref[...] = jnp.zeros_like(s_ref)

        args, dl, lanes = _gdn_local_args(q_ref, k_ref, v_ref, gc_ref, hp, tc)
        sol, aqk, q_dec, k_dec = _gdn_local(*args)
        u = [x[:, :GDN_DK] for x in sol]
        w = [x[:, GDN_DK:] for x in sol]
        s = [s_ref[hh] for hh in range(hp)]
        v_new, o_state = [[] for _ in range(hp)], [[] for _ in range(hp)]
        for j in range(nsub):
            sl = slice(j * c, (j + 1) * c)
            for hh in range(hp):
                st_ref[hh, j] = s[hh]
            vn = _each(lambda uu, ww, ss: uu[sl] - _bdot(ww[sl], ss), u, w, s)
            os_ = _each(lambda qq, ss: _bdot(qq[sl], ss), q_dec, s)
            s = _each(lambda ss, d, kk, vv: ss * jnp.exp(d[j * c:j * c + 1]) + _bdot_tn(kk[sl], vv), s, dl, k_dec, vn)
            for hh in range(hp):
                v_new[hh].append(vn[hh])
                o_state[hh].append(os_[hh])
        for hh in range(hp):
            s_ref[hh] = s[hh]
        cat = lambda xs: jnp.concatenate(xs, axis=0)
        o = _each(lambda os_, aa, vv: cat(os_) + _bdot(aa, cat(vv)), o_state, aqk, v_new)
        for hh, ls in enumerate(lanes):
            on_ref[:, ls] = _gated_norm(o[hh], z_ref[:, ls], nw_ref[...]).astype(BF16)

    col = lambda off: pl.BlockSpec((tc, hp * LANE), lambda g, i: (i, off // hp + g))
    return pl.pallas_call(
        body, name="gdn_chunk_fwd",
        grid=(GDN_HEADS // hp, t // tc),
        in_specs=[col(0), col(GDN_HEADS), col(2 * GDN_HEADS),
                  pl.BlockSpec((tc, LANE), lambda g, i: (i, 0)),
                  col(GDN_QKV_W // LANE),
                  pl.BlockSpec((1, LANE), lambda g, i: (0, 0))],
        out_specs=[col(0),
                   pl.BlockSpec((hp, nsub, GDN_DK, GDN_DK), lambda g, i: (g, i, 0, 0))],
        out_shape=[jax.ShapeDtypeStruct((t, GDN_HEADS * GDN_DK), BF16),
                   jax.ShapeDtypeStruct((GDN_HEADS, t // c, GDN_DK, GDN_DK), F32)],
        scratch_shapes=[pltpu.VMEM((hp, GDN_DK, GDN_DK), F32)],
        compiler_params=_arb(2),
    )(qkv, qkv, qkv, gc, proj, norm_w)


def gdn_chunk_bwd(qkv, gc, proj, norm_w, states, d_on):
    t = qkv.shape[0]
    c = GDN_CHUNK
    tc = _tile(t, 256)
    nsub = tc // c
    nblk = t // tc
    hp = GDN_HEAD_GROUP

    def body(q_ref, k_ref, v_ref, gc_ref, z_ref, nw_ref, st_ref, don_ref,
             dq_ref, dk_ref, dv_ref, dz_ref, dgc_ref, dnw_ref, ds_ref):
        @pl.when(pl.program_id(1) == 0)
        def _():
            ds_ref[...] = jnp.zeros_like(ds_ref)

        @pl.when((pl.program_id(1) == 0) & (pl.program_id(0) == 0))
        def _():
            dnw_ref[...] = jnp.zeros_like(dnw_ref)

        rows = [slice(j * c, (j + 1) * c) for j in range(nsub)]
        cat = lambda xs: jnp.concatenate(xs, axis=0)
        lsum = lambda x: jnp.sum(x, axis=-1, keepdims=True)
        lane = _iota2((tc, LANE), 1)
        row = _iota2((tc, 1), 0)
        heads = range(hp)
        args, dl, lanes = _gdn_local_args(q_ref, k_ref, v_ref, gc_ref, hp, tc)
        (sol, aqk, q_dec, k_dec), vjp = jax.vjp(_gdn_local, *args)
        u = [x[:, :GDN_DK] for x in sol]
        w = [x[:, GDN_DK:] for x in sol]
        states = [[st_ref[hh, j] for j in range(nsub)] for hh in heads]
        v_new = _each(lambda uu, ww, st: [uu[sl] - _bdot(ww[sl], s) for sl, s in zip(rows, st)], u, w, states)
        v_all = [cat(v) for v in v_new]
        o = _each(lambda qq, st, aa, vv: cat([_bdot(qq[sl], s) for sl, s in zip(rows, st)]) + _bdot(aa, vv),
                  q_dec, states, aqk, v_all)
        nw = nw_ref[...]
        gn = [_gated_norm_bwd(don_ref[:, ls], o[hh], z_ref[:, ls], nw) for hh, ls in enumerate(lanes)]
        do = [x[0] for x in gn]
        for hh, ls in enumerate(lanes):
            dnw_ref[...] += gn[hh][2]
            dz_ref[:, ls] = gn[hh][1].astype(BF16)
        d_aqk = _each(_bdot_nt, do, v_all)
        dv_o = _each(_bdot_tn, aqk, do)
        ds = [ds_ref[hh] for hh in heads]
        d_u, d_w, d_qdec, d_kdec, d_last = ([[None] * nsub for _ in heads] for _ in range(5))
        for j in reversed(range(nsub)):
            sl = rows[j]
            sj = [states[hh][j] for hh in heads]
            cd = [jnp.exp(d[j * c:j * c + 1]) for d in dl]
            du = _each(lambda dvo, kk, dd: dvo[sl] + _bdot(kk[sl], dd), dv_o, k_dec, ds)
            dqd = _each(lambda dd, s: _bdot_nt(dd[sl], s), do, sj)
            dkd = _each(lambda vv, dd: _bdot_nt(vv[j], dd), v_new, ds)
            dla = _each(lambda s, dd, cc: jnp.sum(lsum(s * dd), axis=0, keepdims=True) * cc, sj, ds, cd)
            dw = _each(lambda x, s: -_bdot_nt(x, s), du, sj)
            ds = _each(lambda qq, dd, cc, dsn, ww, x: _bdot_tn(qq[sl], dd[sl]) + cc * dsn - _bdot_tn(ww[sl], x),
                       q_dec, do, cd, ds, w, du)
            for hh in heads:
                d_u[hh][j], d_w[hh][j], d_qdec[hh][j], d_kdec[hh][j], d_last[hh][j] = du[hh], dw[hh], dqd[hh], dkd[hh], dla[hh]
        for hh in heads:
            ds_ref[hh] = ds[hh]
        d_sol = _each(lambda a, b: jnp.concatenate([cat(a), cat(b)], axis=1), d_u, d_w)
        dq, dk, dv, d_dcb, d_dcb128, d_dlb128, d_bb = vjp((d_sol, d_aqk, [cat(x) for x in d_qdec], [cat(x) for x in d_kdec]))
        dgc = jnp.zeros((tc, LANE), F32)
        for hh, ls in enumerate(lanes):
            h = pl.program_id(0) * hp + hh
            dq_ref[:, ls] = dq[hh]
            dk_ref[:, ls] = dk[hh]
            dv_ref[:, ls] = dv[hh]
            d_dl = lsum(d_dlb128[hh])
            for j in range(nsub):
                d_dl = d_dl + jnp.where(row == j * c, d_last[hh][j], 0.0)
            dgc = dgc + jnp.where(lane == GC_BETA + h, lsum(d_bb[hh]),
                                  jnp.where(lane == GC_DECAY + h, lsum(d_dcb[hh]) + lsum(d_dcb128[hh]),
                                            jnp.where(lane == GC_LAST + h, d_dl, 0.0)))
        dgc_ref[0] = dgc

    rev = lambda i: nblk - 1 - i
    col = lambda off: pl.BlockSpec((tc, hp * LANE), lambda g, i: (rev(i), off // hp + g))
    return pl.pallas_call(
        body, name="gdn_chunk_bwd",
        grid=(GDN_HEADS // hp, nblk),
        in_specs=[col(0), col(GDN_HEADS), col(2 * GDN_HEADS),
                  pl.BlockSpec((tc, LANE), lambda g, i: (rev(i), 0)),
                  col(GDN_QKV_W // LANE),
                  pl.BlockSpec((1, LANE), lambda g, i: (0, 0)),
                  pl.BlockSpec((hp, nsub, GDN_DK, GDN_DK), lambda g, i: (g, rev(i), 0, 0)),
                  col(0)],
        out_specs=[col(0), col(0), col(0), col(0),
                   pl.BlockSpec((1, tc, LANE), lambda g, i: (g, rev(i), 0)),
                   pl.BlockSpec((1, LANE), lambda g, i: (0, 0))],
        out_shape=[
            jax.ShapeDtypeStruct((t, GDN_HEADS * GDN_DK), F32),
            jax.ShapeDtypeStruct((t, GDN_HEADS * GDN_DK), F32),
            jax.ShapeDtypeStruct((t, GDN_HEADS * GDN_DK), F32),
            jax.ShapeDtypeStruct((t, GDN_HEADS * GDN_DK), BF16),
            jax.ShapeDtypeStruct((GDN_HEADS // hp, t, LANE), F32),
            jax.ShapeDtypeStruct((1, LANE), F32)],
        scratch_shapes=[pltpu.VMEM((hp, GDN_DK, GDN_DK), F32)],
        compiler_params=_arb(2),
    )(qkv, qkv, qkv, gc, proj, norm_w, states, d_on)


GDN_PROJ_W = GDN_QKV_W + 1024 + LANE
GDN_Q_SCALE = GDN_DK ** -0.5


def _shift_rows(ext, shift, lo, n):
    if shift == 0:
        return ext[lo:lo + n]
    return pltpu.roll(ext, shift, 0)[lo:lo + n]


def _conv_fwd(x, halo, w):
    n = x.shape[0]
    ext = jnp.concatenate([halo, x], axis=0)
    y = w[GDN_CONV - 1:GDN_CONV] * x
    for j in range(GDN_CONV - 1):
        y = y + w[j:j + 1] * _shift_rows(ext, GDN_CONV - 1 - j, 8, n)
    return y


def _chunk_masks(n):
    r, s = _iota2((n, n), 0), _iota2((n, n), 1)
    same = (r // GDN_CHUNK) == (s // GDN_CHUNK)
    return (same & (r >= s)).astype(F32), (same & (r <= s)).astype(F32), same.astype(F32)


def _softplus(x):
    return jnp.maximum(x, 0.0) + jnp.log(1.0 + jnp.exp(-jnp.abs(x)))


def _l2n(t):
    rs = lax.rsqrt(jnp.sum(t * t, axis=-1, keepdims=True) + NORM_EPS)
    return t * rs, rs


def gdn_pre_fwd(proj, conv_w, gate_par):
    t = proj.shape[0]
    tm = _tile(t, 256)

    def body(x_ref, halo_ref, ba_ref, w_ref, gp_ref, qkv_ref, gc_ref):
        i = pl.program_id(0)
        halo = jnp.where(i > 0, halo_ref[...], 0.0)
        y = _silu(_conv_fwd(x_ref[...], halo, w_ref[...]))
        for hh in range(2 * GDN_HEADS):
            sl = slice(hh * LANE, (hh + 1) * LANE)
            tn, _ = _l2n(y[:, sl])
            qkv_ref[:, sl] = tn * GDN_Q_SCALE if hh < GDN_HEADS else tn
        qkv_ref[:, 2 * GDN_HEADS * LANE:] = y[:, 2 * GDN_HEADS * LANE:]
        ba = ba_ref[...]
        lane = _iota2(ba.shape, 1)
        gp = gp_ref[...]
        is_a = (lane >= GC_G) & (lane < GC_G + GDN_HEADS)
        g = jnp.where(is_a, -jnp.exp(gp[0:1]) * _softplus(ba + gp[1:2]), 0.0)
        tri, _, same = _chunk_masks(tm)
        decay = pltpu.roll(_dot(tri, g, HI), GC_DECAY - GC_G, 1)
        last = pltpu.roll(_dot(same, g, HI), GC_LAST - GC_G, 1)
        gc_ref[...] = jnp.where(lane < GDN_HEADS, _sigmoid(ba), g) + decay + last

    return pl.pallas_call(
        body, name="gdn_pre_fwd",
        grid=(t // tm,),
        in_specs=[pl.BlockSpec((tm, GDN_QKV_W), lambda i: (i, 0)),
                  pl.BlockSpec((8, GDN_QKV_W), lambda i: (jnp.maximum(i * (tm // 8) - 1, 0), 0)),
                  pl.BlockSpec((tm, LANE), lambda i: (i, (GDN_QKV_W + 1024) // LANE)),
                  pl.BlockSpec((8, GDN_QKV_W), lambda i: (0, 0)),
                  pl.BlockSpec((8, LANE), lambda i: (0, 0))],
        out_specs=[pl.BlockSpec((tm, GDN_QKV_W), lambda i: (i, 0)),
                   pl.BlockSpec((tm, LANE), lambda i: (i, 0))],
        out_shape=[jax.ShapeDtypeStruct((t, GDN_QKV_W), F32), jax.ShapeDtypeStruct((t, LANE), F32)],
        compiler_params=_arb(1),
    )(proj, proj, proj, conv_w, gate_par)


def gdn_pre_bwd(proj, conv_w, gate_par, dq, dk, dv, dgc_heads):
    t = proj.shape[0]
    tm = _tile(t, 256)

    def body(x_ref, halo_ref, ba_ref, w_ref, gp_ref, dq_ref, dk_ref, dv_ref, dgc_ref, dy_ref, dba_ref, dgp_ref):
        i = pl.program_id(0)

        @pl.when(i == 0)
        def _():
            dgp_ref[...] = jnp.zeros_like(dgp_ref)

        halo = jnp.where(i > 0, halo_ref[...], 0.0)
        y = _conv_fwd(x_ref[...], halo, w_ref[...])
        for hh in range(3 * GDN_HEADS):
            sl = slice(hh * LANE, (hh + 1) * LANE)
            hsl = slice((hh % GDN_HEADS) * LANE, (hh % GDN_HEADS + 1) * LANE)
            yy = y[:, sl]
            if hh < 2 * GDN_HEADS:
                tn, rs = _l2n(_silu(yy))
                dtn = dq_ref[:, hsl] * GDN_Q_SCALE if hh < GDN_HEADS else dk_ref[:, hsl]
                dsil = rs * (dtn - tn * jnp.sum(dtn * tn, axis=-1, keepdims=True))
            else:
                dsil = dv_ref[:, hsl]
            dy_ref[:, sl] = dsil * _silu_grad(yy)
        dgc = dgc_ref[0]
        for hh in range(1, dgc_heads.shape[0]):
            dgc = dgc + dgc_ref[hh]
        ba = ba_ref[...]
        lane = _iota2(ba.shape, 1)
        _, tri_t, same = _chunk_masks(tm)
        d_decay = jnp.where((lane >= GC_DECAY) & (lane < GC_DECAY + GDN_HEADS), dgc, 0.0)
        d_last = jnp.where((lane >= GC_LAST) & (lane < GC_LAST + GDN_HEADS), dgc, 0.0)
        dgc = (jnp.where(lane < GDN_HEADS, dgc, 0.0) + pltpu.roll(_dot(tri_t, d_decay, HI), LANE - (GC_DECAY - GC_G), 1)
               + pltpu.roll(_dot(same, d_last, HI), LANE - (GC_LAST - GC_G), 1))
        gp = gp_ref[...]
        xg = ba + gp[1:2]
        ea = jnp.exp(gp[0:1])
        sp = _softplus(xg)
        sb = _sigmoid(ba)
        is_b = lane < GDN_HEADS
        is_a = (lane >= GDN_HEADS) & (lane < 2 * GDN_HEADS)
        d_pre = jnp.where(is_a, dgc * (-ea) * _sigmoid(xg), 0.0)
        dba_ref[...] = jnp.where(is_b, dgc * sb * (1.0 - sb), d_pre).astype(BF16)
        d_alog = jnp.sum(jnp.where(is_a, dgc * (-ea) * sp, 0.0), axis=0, keepdims=True)
        d_dtb = jnp.sum(d_pre, axis=0, keepdims=True)
        row = _iota2((8, LANE), 0)
        dgp_ref[...] += jnp.where(row == 0, d_alog, jnp.where(row == 1, d_dtb, 0.0))

    hspec = pl.BlockSpec((tm, GDN_HEADS * LANE), lambda i: (i, 0))
    return pl.pallas_call(
        body, name="gdn_pre_bwd",
        grid=(t // tm,),
        in_specs=[pl.BlockSpec((tm, GDN_QKV_W), lambda i: (i, 0)),
                  pl.BlockSpec((8, GDN_QKV_W), lambda i: (jnp.maximum(i * (tm // 8) - 1, 0), 0)),
                  pl.BlockSpec((tm, LANE), lambda i: (i, (GDN_QKV_W + 1024) // LANE)),
                  pl.BlockSpec((8, GDN_QKV_W), lambda i: (0, 0)),
                  pl.BlockSpec((8, LANE), lambda i: (0, 0)),
                  hspec, hspec, hspec,
                  pl.BlockSpec((dgc_heads.shape[0], tm, LANE), lambda i: (0, i, 0))],
        out_specs=[pl.BlockSpec((tm, GDN_QKV_W), lambda i: (i, 0)),
                   pl.BlockSpec((tm, LANE), lambda i: (i, 0)),
                   pl.BlockSpec((8, LANE), lambda i: (0, 0))],
        out_shape=[jax.ShapeDtypeStruct((t, GDN_QKV_W), F32), jax.ShapeDtypeStruct((t, LANE), BF16),
                   jax.ShapeDtypeStruct((8, LANE), F32)],
        compiler_params=_arb(1),
    )(proj, proj, proj, conv_w, gate_par, dq, dk, dv, dgc_heads)


def gdn_conv_bwd(proj, conv_w, dy):
    t = proj.shape[0]
    tm = _tile(t, 256)
    nblk = t // tm

    def body(x_ref, halo_ref, w_ref, dy_ref, dyn_ref, dx_ref, dw_ref):
        i = pl.program_id(0)

        @pl.when(i == 0)
        def _():
            dw_ref[...] = jnp.zeros_like(dw_ref)

        w = w_ref[...]
        dy = dy_ref[...]
        ext_dy = jnp.concatenate([dy, jnp.where(i < nblk - 1, dyn_ref[...], 0.0)], axis=0)
        ext_x = jnp.concatenate([jnp.where(i > 0, halo_ref[...], 0.0), x_ref[...]], axis=0)
        dx = w[GDN_CONV - 1:GDN_CONV] * dy
        rows = [jnp.sum(dy * x_ref[...], axis=0, keepdims=True)]
        for j in range(GDN_CONV - 1):
            sh = GDN_CONV - 1 - j
            dx = dx + w[j:j + 1] * _shift_rows(ext_dy, tm + 8 - sh, 0, tm)
            rows.insert(j, jnp.sum(dy * _shift_rows(ext_x, sh, 8, tm), axis=0, keepdims=True))
        dx_ref[...] = dx.astype(BF16)
        row = _iota2((8, GDN_QKV_W), 0)
        acc = jnp.zeros((8, GDN_QKV_W), F32)
        for j in range(GDN_CONV):
            acc = acc + jnp.where(row == j, rows[j], 0.0)
        dw_ref[...] += acc

    return pl.pallas_call(
        body, name="gdn_conv_bwd",
        grid=(nblk,),
        in_specs=[pl.BlockSpec((tm, GDN_QKV_W), lambda i: (i, 0)),
                  pl.BlockSpec((8, GDN_QKV_W), lambda i: (jnp.maximum(i * (tm // 8) - 1, 0), 0)),
                  pl.BlockSpec((8, GDN_QKV_W), lambda i: (0, 0)),
                  pl.BlockSpec((tm, GDN_QKV_W), lambda i: (i, 0)),
                  pl.BlockSpec((8, GDN_QKV_W), lambda i: (jnp.minimum((i + 1) * (tm // 8), t // 8 - 1), 0))],
        out_specs=[pl.BlockSpec((tm, GDN_QKV_W), lambda i: (i, 0)),
                   pl.BlockSpec((8, GDN_QKV_W), lambda i: (0, 0))],
        out_shape=[jax.ShapeDtypeStruct((t, GDN_QKV_W), BF16), jax.ShapeDtypeStruct((8, GDN_QKV_W), F32)],
        compiler_params=_arb(1),
    )(proj, proj, conv_w, dy, dy)


def _resident(w_hbm, w_vmem, sem):
    @pl.when(pl.program_id(0) == 0)
    def _():
        cp = pltpu.make_async_copy(w_hbm, w_vmem, sem)
        cp.start()
        cp.wait()


ANY = pl.BlockSpec(memory_space=pl.ANY)


def norm_proj(h, nw, w, bias):
    t, d = h.shape
    n = w.shape[1]
    tm = _tile(t, 256)
    nc = _tile(n, 1536) if n % 1536 == 0 else _tile(n, 1408)

    def body(h_ref, nw_ref, w_hbm, b_ref, o_ref, w_ref, sem):
        _resident(w_hbm, w_ref, sem)
        hn = _rms(h_ref[...], nw_ref[...])[0].astype(BF16)
        for c0 in range(0, n, nc):
            o_ref[:, c0:c0 + nc] = _dot(hn, w_ref[:, c0:c0 + nc]) + b_ref[:, c0:c0 + nc]

    return pl.pallas_call(
        body, name="norm_proj",
        grid=(t // tm,),
        in_specs=[pl.BlockSpec((tm, d), lambda i: (i, 0)), pl.BlockSpec((1, d), lambda i: (0, 0)), ANY,
                  pl.BlockSpec((1, n), lambda i: (0, 0))],
        out_specs=pl.BlockSpec((tm, n), lambda i: (i, 0)),
        out_shape=jax.ShapeDtypeStruct((t, n), F32),
        scratch_shapes=[pltpu.VMEM((d, n), BF16), pltpu.SemaphoreType.DMA],
        compiler_params=_arb(1),
    )(h, nw, w, bias)


def linear_residual(h, x, w, bias):
    t, d = h.shape
    k = x.shape[1]
    tm = _tile(t, 512)

    def body(h_ref, x_ref, w_hbm, b_ref, o_ref, w_ref, sem):
        _resident(w_hbm, w_ref, sem)
        o_ref[...] = h_ref[...] + _dot(x_ref[...], w_ref[...]) + b_ref[...]

    return pl.pallas_call(
        body, name="linear_residual",
        grid=(t // tm,),
        in_specs=[pl.BlockSpec((tm, d), lambda i: (i, 0)), pl.BlockSpec((tm, k), lambda i: (i, 0)), ANY,
                  pl.BlockSpec((1, d), lambda i: (0, 0))],
        out_specs=pl.BlockSpec((tm, d), lambda i: (i, 0)),
        out_shape=jax.ShapeDtypeStruct((t, d), F32),
        scratch_shapes=[pltpu.VMEM((k, d), BF16), pltpu.SemaphoreType.DMA],
        compiler_params=_arb(1),
    )(h, x, w, bias)


def matmul_nt(dy, w):
    t, d = dy.shape
    k = w.shape[0]
    tm = _tile(t, 512)

    def body(dy_ref, w_hbm, o_ref, cs_ref, w_ref, sem):
        _resident(w_hbm, w_ref, sem)

        @pl.when(pl.program_id(0) == 0)
        def _():
            cs_ref[...] = jnp.zeros_like(cs_ref)

        dy = dy_ref[...]
        cs_ref[...] += jnp.sum(dy, axis=0, keepdims=True)
        o_ref[...] = _dot_nt(dy.astype(BF16), w_ref[...])

    return pl.pallas_call(
        body, name="matmul_nt",
        grid=(t // tm,),
        in_specs=[pl.BlockSpec((tm, d), lambda i: (i, 0)), ANY],
        out_specs=[pl.BlockSpec((tm, k), lambda i: (i, 0)), pl.BlockSpec((1, d), lambda i: (0, 0))],
        out_shape=[jax.ShapeDtypeStruct((t, k), F32), jax.ShapeDtypeStruct((1, d), F32)],
        scratch_shapes=[pltpu.VMEM((k, d), BF16), pltpu.SemaphoreType.DMA],
        compiler_params=_arb(1),
    )(dy, w)


def norm_proj_bwd(h, nw, dh, dps, ws):
    t, d = h.shape
    np_ = len(dps)
    ns = [w.shape[1] for w in ws]
    tm = _tile(t, 256)

    def body(*refs):
        h_ref, nw_ref, dh_ref = refs[:3]
        dp_refs = refs[3:3 + np_]
        w_hbms = refs[3 + np_:3 + 2 * np_]
        o_ref, hn_ref, dnw_ref = refs[3 + 2 * np_:6 + 2 * np_]
        cs_refs = refs[6 + 2 * np_:6 + 3 * np_]
        w_refs = refs[6 + 3 * np_:6 + 4 * np_]
        sem = refs[6 + 4 * np_]
        for a, b in zip(w_hbms, w_refs):
            _resident(a, b, sem)

        @pl.when(pl.program_id(0) == 0)
        def _():
            dnw_ref[...] = jnp.zeros_like(dnw_ref)
            for c in cs_refs:
                c[...] = jnp.zeros_like(c)

        nw = nw_ref[...]
        hn, xhat, r = _rms(h_ref[...], nw)
        hn_ref[...] = hn.astype(BF16)
        dhn = jnp.zeros((tm, d), F32)
        for dp_ref, w_ref, cs_ref in zip(dp_refs, w_refs, cs_refs):
            dp = dp_ref[...]
            cs_ref[...] += jnp.sum(dp.astype(F32), axis=0, keepdims=True)
            dhn = dhn + _dot_nt(dp, w_ref[...])
        dx, dnw = _rms_bwd(dhn, nw, xhat, r)
        dnw_ref[...] += dnw
        o_ref[...] = dh_ref[...] + dx

    row = pl.BlockSpec((tm, d), lambda i: (i, 0))
    vec = pl.BlockSpec((1, d), lambda i: (0, 0))
    return pl.pallas_call(
        body, name="norm_proj_bwd",
        grid=(t // tm,),
        in_specs=[row, vec, row] + [pl.BlockSpec((tm, n), lambda i: (i, 0)) for n in ns] + [ANY] * np_,
        out_specs=[row, row, vec] + [pl.BlockSpec((1, n), lambda i: (0, 0)) for n in ns],
        out_shape=[jax.ShapeDtypeStruct((t, d), F32), jax.ShapeDtypeStruct((t, d), BF16),
                   jax.ShapeDtypeStruct((1, d), F32)] + [jax.ShapeDtypeStruct((1, n), F32) for n in ns],
        scratch_shapes=[pltpu.VMEM((d, n), BF16) for n in ns] + [pltpu.SemaphoreType.DMA],
        compiler_params=_arb(1),
    )(h, nw, dh, *dps, *ws)


def _pair_copies(g_refs, got_refs, send_sems, recv_sems):
    mx, my, mc = _me()
    copies = []
    for g_ref, got_ref in zip(g_refs, got_refs):
        for j in range(4):
            i = len(copies)
            copies.append(pltpu.make_async_remote_copy(
                src_ref=g_ref.at[j, 1 - mc], dst_ref=got_ref.at[j], send_sem=send_sems.at[i], recv_sem=recv_sems.at[i],
                device_id=(mx, my, 1 - mc), device_id_type=MESH))
    return copies


def _scatter_copies(p_refs, q_refs, send_sems, recv_sems):
    mx, my, mc = _me()
    copies = []
    for p_ref, q_ref in zip(p_refs, q_refs):
        for k in (1, 2, 3):
            px, py = _flip(mx, k >> 1), _flip(my, k & 1)
            i = len(copies)
            copies.append(pltpu.make_async_remote_copy(
                src_ref=p_ref.at[2 * px + py], dst_ref=q_ref.at[2 * mx + my], send_sem=send_sems.at[i],
                recv_sem=recv_sems.at[i], device_id=(px, py, mc), device_id_type=MESH))
    return copies


def _share_copies(o_refs, out_refs, send_sems, recv_sems):
    mx, my, mc = _me()
    return [pltpu.make_async_remote_copy(src_ref=o_ref, dst_ref=out_ref, send_sem=send_sems.at[i], recv_sem=recv_sems.at[i],
                                         device_id=(mx, my, 1 - mc), device_id_type=MESH)
            for i, (o_ref, out_ref) in enumerate(zip(o_refs, out_refs))]


CARRIED = {"pair": (_pair_copies, 4, lambda a: a.shape[:1] + a.shape[2:]), "scatter": (_scatter_copies, 3, lambda a: a.shape),
           "share": (_share_copies, 1, lambda a: a.shape)}


def matmul_tn(x, y, scale=1.0, carry=None):
    t, k = x.shape
    n = y.shape[1]
    tk = _tile(k, 1024) if k % 1024 == 0 else _tile(k, 1408)
    tn = n if n <= 1536 else (1024 if n % 1024 == 0 else 1408)
    assert n % tn == 0
    tt = _tile(t, 1024)
    grid = (k // tk, n // tn, t // tt)
    arrays = [] if carry is None else list(carry[1])
    na = len(arrays)

    def body(x_ref, y_ref, *rest):
        o_ref = rest[na]
        if carry is not None:
            make, _, _ = CARRIED[carry[0]]
            copies = lambda: make(rest[:na], rest[na + 1:2 * na + 1], rest[2 * na + 1], rest[2 * na + 2])
            pid = [pl.program_id(a) for a in range(3)]

            @pl.when((pid[0] == 0) & (pid[1] == 0) & (pid[2] == 0))
            def _():
                for cp in copies():
                    cp.start()

        @pl.when(pl.program_id(2) == 0)
        def _():
            o_ref[...] = jnp.zeros_like(o_ref)

        yv = y_ref[...]
        if scale != 1.0:
            yv = yv * scale
        o_ref[...] += _dot_tn(x_ref[...].astype(BF16), yv.astype(BF16))

        if carry is not None:
            @pl.when((pid[0] == grid[0] - 1) & (pid[1] == grid[1] - 1) & (pid[2] == grid[2] - 1))
            def _():
                for cp in copies():
                    cp.wait()

    out_shape = [jax.ShapeDtypeStruct((k, n), F32)]
    scratch = []
    if carry is not None:
        _, per_array, out_of = CARRIED[carry[0]]
        out_shape += [jax.ShapeDtypeStruct(out_of(a), a.dtype) for a in arrays]
        scratch = [pltpu.SemaphoreType.DMA((per_array * na,))] * 2
    res = pl.pallas_call(
        body, name="matmul_tn" if carry is None else "matmul_tn_" + carry[0],
        grid=grid,
        in_specs=[pl.BlockSpec((tt, tk), lambda i, j, s: (s, i)), pl.BlockSpec((tt, tn), lambda i, j, s: (s, j))] + [ANY] * na,
        out_specs=[pl.BlockSpec((tk, tn), lambda i, j, s: (i, j))] + [ANY] * na,
        out_shape=out_shape, scratch_shapes=scratch,
        compiler_params=_arb(3),
    )(x, y, *arrays)
    return res[0] if carry is None else (res[0], list(res[1:]))


FFN_CHUNKS = 2


class _FfnGather:
    def __init__(self, gu_sh, d_sh, gu_full, d_full, send_sems, recv_sems, local_sems):
        self.sh, self.full = (gu_sh, d_sh), (gu_full, d_full)
        self.send_sems, self.recv_sems, self.local_sems = send_sems, recv_sems, local_sems
        self.mx, self.my, self.mc = _me()
        self.me = 2 * self.mx + self.my
        self.w4, self.f4 = gu_sh.shape[1], d_sh.shape[0]
        self.hg, self.hd = gu_sh.shape[0] // 2, d_sh.shape[0] // 2

    def _window(self, a, chip, core=None):
        gu_full, d_full = self.full
        if a == 0:
            rows = pl.ds(0, 2 * self.hg) if core is None else pl.ds(pl.multiple_of(core * self.hg, 16), self.hg)
            return gu_full.at[rows, pl.ds(pl.multiple_of(chip * self.w4, LANE), self.w4)]
        if core is None:
            return d_full.at[pl.ds(pl.multiple_of(chip * self.f4, 16), self.f4), :]
        return d_full.at[pl.ds(pl.multiple_of(chip * self.f4 + core * self.hd, 16), self.hd), :]

    def _half(self, a):
        n = (self.hg, self.hd)[a]
        return self.sh[a].at[pl.ds(pl.multiple_of(self.mc * n, 16), n), :]

    def _peer(self, k):
        return _flip(self.mx, k >> 1), _flip(self.my, k & 1)

    def _fetch(self, k, a):
        px, py = self._peer(k)
        i = 2 * (k - 1) + a
        return pltpu.make_async_remote_copy(src_ref=self._half(a), dst_ref=self._window(a, self.me, self.mc),
                                            send_sem=self.send_sems.at[i], recv_sem=self.recv_sems.at[i],
                                            device_id=(px, py, self.mc), device_id_type=MESH)

    def _relay(self, k, a):
        px, py = self._peer(k)
        got = self._window(a, 2 * px + py, self.mc)
        i = 6 + 2 * (k - 1) + a
        return pltpu.make_async_remote_copy(src_ref=got, dst_ref=got, send_sem=self.send_sems.at[i],
                                            recv_sem=self.recv_sems.at[i], device_id=(self.mx, self.my, 1 - self.mc),
                                            device_id_type=MESH)

    def _own(self, a):
        return pltpu.make_async_copy(self.sh[a], self._window(a, self.me), self.local_sems.at[a])

    def start(self):
        for a in (0, 1):
            self._own(a).start()
            for k in (1, 2, 3):
                self._fetch(k, a).start()

    def relay(self):
        for k in (1, 2, 3):
            for a in (0, 1):
                self._fetch(k, a).wait_recv()
                self._relay(k, a).start()

    def finish(self):
        for k in (1, 2, 3):
            for a in (0, 1):
                self._fetch(k, a).wait_send()
                self._relay(k, a).wait()
        for a in (0, 1):
            self._own(a).wait()


def ffn_fwd(h, nw, wgu, wd, nxt=None):
    t, d = h.shape
    f = wd.shape[0]
    fc = f // FFN_CHUNKS
    tm = _tile(t, 512)
    nsteps = t // tm

    def body(h_ref, nw_ref, wgu_hbm, wd_hbm, *rest):
        if nxt is None:
            o_ref, gu_ref, a_ref, wgu_ref, wd_ref, sem = rest
            gather = None
        else:
            (gu_sh, d_sh, o_ref, gu_ref, a_ref, gu_full, d_full, wgu_ref, wd_ref, sem,
             send_sems, recv_sems, local_sems) = rest
            gather = _FfnGather(gu_sh, d_sh, gu_full, d_full, send_sems, recv_sems, local_sems)
            pl.when(pl.program_id(0) == 0)(gather.start)
        _resident(wgu_hbm, wgu_ref, sem)
        _resident(wd_hbm, wd_ref, sem)
        if gather is not None:
            pl.when(pl.program_id(0) == nsteps // 2)(gather.relay)
        x = h_ref[...]
        hn = _rms(x, nw_ref[...])[0].astype(BF16)
        acc = jnp.zeros((tm, d), F32)
        for c in range(FFN_CHUNKS):
            gs, us = slice(c * fc, (c + 1) * fc), slice(f + c * fc, f + (c + 1) * fc)
            g = _dot(hn, wgu_ref[:, gs])
            u = _dot(hn, wgu_ref[:, us])
            a = (_silu(g) * u).astype(BF16)
            gu_ref[:, gs] = g.astype(BF16)
            gu_ref[:, us] = u.astype(BF16)
            a_ref[:, gs] = a
            acc = acc + _dot(a, wd_ref[gs, :])
        o_ref[...] = x + 0.5 * acc
        if gather is not None:
            pl.when(pl.program_id(0) == nsteps - 1)(gather.finish)

    row = lambda w: pl.BlockSpec((tm, w), lambda i: (i, 0))
    in_specs = [row(d), pl.BlockSpec((1, d), lambda i: (0, 0)), ANY, ANY]
    out_specs = [row(d), row(2 * f), row(f)]
    out_shape = [jax.ShapeDtypeStruct((t, d), F32), jax.ShapeDtypeStruct((t, 2 * f), BF16),
                 jax.ShapeDtypeStruct((t, f), BF16)]
    scratch = [pltpu.VMEM((d, 2 * f), BF16), pltpu.VMEM((f, d), BF16), pltpu.SemaphoreType.DMA]
    args = (h, nw, wgu, wd)
    if nxt is not None:
        in_specs += [ANY, ANY]
        out_specs += [ANY, ANY]
        out_shape += [jax.ShapeDtypeStruct((d, 2 * f), BF16), jax.ShapeDtypeStruct((f, d), BF16)]
        scratch += [pltpu.SemaphoreType.DMA((12,)), pltpu.SemaphoreType.DMA((12,)), pltpu.SemaphoreType.DMA((2,))]
        args += tuple(nxt)
    return pl.pallas_call(
        body, name="ffn_fwd_gather" if nxt is not None else "ffn_fwd",
        grid=(nsteps,),
        in_specs=in_specs, out_specs=out_specs, out_shape=out_shape, scratch_shapes=scratch,
        compiler_params=_arb(1),
    )(*args)


def ffn_bwd(h, nw, wgu, wd, gu, dh):
    t, d = h.shape
    f = wd.shape[0]
    fc = f // FFN_CHUNKS
    tm = _tile(t, 256)

    def body(h_ref, nw_ref, wgu_hbm, wd_hbm, gu_ref, dh_ref, o_ref, hn_ref, dgu_ref, dnw_ref, wgu_ref, wd_ref, sem):
        _resident(wgu_hbm, wgu_ref, sem)
        _resident(wd_hbm, wd_ref, sem)

        @pl.when(pl.program_id(0) == 0)
        def _():
            dnw_ref[...] = jnp.zeros_like(dnw_ref)

        nw = nw_ref[...]
        hn32, xhat, r = _rms(h_ref[...], nw)
        hn_ref[...] = hn32.astype(BF16)
        dh = dh_ref[...]
        dout = (0.5 * dh).astype(BF16)
        dhn = jnp.zeros((tm, d), F32)
        for c in range(FFN_CHUNKS):
            gs, us = slice(c * fc, (c + 1) * fc), slice(f + c * fc, f + (c + 1) * fc)
            g = gu_ref[:, gs].astype(F32)
            u = gu_ref[:, us].astype(F32)
            sg = _sigmoid(g)
            sil = g * sg
            da = _dot_nt(dout, wd_ref[gs, :])
            dg = (da * u * (sg * (1.0 + g * (1.0 - sg)))).astype(BF16)
            du = (da * sil).astype(BF16)
            dgu_ref[:, gs] = dg
            dgu_ref[:, us] = du
            dhn = dhn + _dot_nt(dg, wgu_ref[:, gs]) + _dot_nt(du, wgu_ref[:, us])
        dx, dnw = _rms_bwd(dhn, nw, xhat, r)
        dnw_ref[...] += dnw
        o_ref[...] = dh + dx

    row = pl.BlockSpec((tm, d), lambda i: (i, 0))
    vec = pl.BlockSpec((1, d), lambda i: (0, 0))
    return pl.pallas_call(
        body, name="ffn_bwd",
        grid=(t // tm,),
        in_specs=[row, vec, ANY, ANY, pl.BlockSpec((tm, 2 * f), lambda i: (i, 0)), row],
        out_specs=[row, row, pl.BlockSpec((tm, 2 * f), lambda i: (i, 0)), vec],
        out_shape=[jax.ShapeDtypeStruct((t, d), F32), jax.ShapeDtypeStruct((t, d), BF16),
                   jax.ShapeDtypeStruct((t, 2 * f), BF16), jax.ShapeDtypeStruct((1, d), F32)],
        scratch_shapes=[pltpu.VMEM((d, 2 * f), BF16), pltpu.VMEM((f, d), BF16), pltpu.SemaphoreType.DMA],
        compiler_params=_arb(1),
    )(h, nw, wgu, wd, gu, dh)


def loss_head(h, nw, target):
    t, d = h.shape
    tm = _tile(t, 512)

    def body(h_ref, nw_ref, tg_ref, dh_ref, loss_ref, dnw_ref):
        @pl.when(pl.program_id(0) == 0)
        def _():
            loss_ref[...] = jnp.zeros_like(loss_ref)
            dnw_ref[...] = jnp.zeros_like(dnw_ref)

        nw = nw_ref[...]
        y, xhat, r = _rms(h_ref[...], nw)
        e = y - tg_ref[...]
        loss_ref[...] += 0.5 * jnp.sum(jnp.mean(e * e, axis=-1, keepdims=True), axis=0, keepdims=True)
        dx, dnw = _rms_bwd(e * (1.0 / d), nw, xhat, r)
        dnw_ref[...] += dnw
        dh_ref[...] = dx

    row = pl.BlockSpec((tm, d), lambda i: (i, 0))
    vec = pl.BlockSpec((1, d), lambda i: (0, 0))
    return pl.pallas_call(
        body, name="loss_head",
        grid=(t // tm,),
        in_specs=[row, vec, row],
        out_specs=[row, pl.BlockSpec((8, LANE), lambda i: (0, 0)), vec],
        out_shape=[jax.ShapeDtypeStruct((t, d), F32), jax.ShapeDtypeStruct((8, LANE), F32),
                   jax.ShapeDtypeStruct((1, d), F32)],
        compiler_params=_arb(1),
    )(h, nw, target)


def adamw(w, g, m, v):
    r, c = w.shape
    tr = r
    while tr * c * 4 > (1 << 20) and tr % 16 == 0:
        tr //= 2

    def body(w_ref, g_ref, m_ref, v_ref, d_ref, nm_ref, nv_ref):
        g = g_ref[...]
        m = ADAM_B1 * m_ref[...] + (1.0 - ADAM_B1) * g
        v = ADAM_B2 * v_ref[...] + (1.0 - ADAM_B2) * (g * g)
        m_hat = m / (1.0 - ADAM_B1 ** ADAM_STEP)
        v_hat = v / (1.0 - ADAM_B2 ** ADAM_STEP)
        d_ref[...] = -ADAM_LR * (m_hat / (jnp.sqrt(v_hat) + ADAM_EPS) + ADAM_WD * w_ref[...])
        nm_ref[...] = m
        nv_ref[...] = v

    blk = pl.BlockSpec((tr, c), lambda i: (i, 0))
    return pl.pallas_call(
        body, name="adamw",
        grid=(r // tr,),
        in_specs=[blk] * 4, out_specs=[blk] * 3,
        out_shape=[jax.ShapeDtypeStruct((r, c), F32)] * 3,
        compiler_params=_arb(1),
    )(w, g, m, v)


ATTN_P_W = ATTN_Q_W + 2 * ATTN_KV_W
ATTN_SCALE = ATTN_HEAD_DIM ** -0.5


def _rope_group(t, tab, sign):
    return (t * tab[:, 0:LANE] + sign * pltpu.roll(t, 8, 1) * tab[:, LANE:2 * LANE]
            + sign * pltpu.roll(t, LANE - 8, 1) * tab[:, 2 * LANE:3 * LANE])


def _rope(t, tab, sign=1.0):
    return jnp.concatenate([_rope_group(t[:, s:s + LANE], tab, sign) for s in range(0, t.shape[1], LANE)], axis=1)


def _attn_heads(qs, ks, vs, sinks, first):
    b = ATTN_BLOCK
    rows = ATTN_GROUP * b
    qi = _iota2((rows, 2 * b), 0) % b
    kj = _iota2((rows, 2 * b), 1)
    rel = qi + b - kj
    valid = (rel >= 0) & (rel < b) & ((kj >= b) | jnp.logical_not(first))
    ss = _each(lambda q, k: jnp.where(valid, _mm1_nt(q, k) * ATTN_SCALE, NEG), qs, ks)
    ms = _each(lambda s, sink: lax.stop_gradient(jnp.maximum(jnp.max(s, axis=-1, keepdims=True), sink)), ss, sinks)
    ps = _each(lambda s, m: jnp.exp(s - m), ss, ms)
    dens = _each(lambda p, sink, m: jnp.sum(p, axis=-1, keepdims=True) + jnp.exp(sink - m), ps, sinks, ms)
    return _each(lambda p, den, v: _mm1(p / den, v), ps, dens, vs)


def _attn_prepare(p_ref, kvp_ref, tab_ref, tabp_ref, sink_ref):
    b = ATTN_BLOCK
    hd = ATTN_HEAD_DIM
    tab, tabp = tab_ref[...], tabp_ref[...]
    q = _rope(p_ref[:, 0:ATTN_Q_W], tab)
    kc = _rope(p_ref[:, ATTN_Q_W:ATTN_Q_W + ATTN_KV_W], tab)
    kp = _rope(kvp_ref[:, 0:ATTN_KV_W], tabp)
    vc = p_ref[:, ATTN_Q_W + ATTN_KV_W:ATTN_P_W]
    vp = kvp_ref[:, ATTN_KV_W:2 * ATTN_KV_W]
    sk = sink_ref[...]
    qs, ks, vs, sinks = [], [], [], []
    for h in range(ATTN_KV_HEADS):
        heads = [ATTN_GROUP * h + g for g in range(ATTN_GROUP)]
        qs.append(jnp.concatenate([q[:, i * hd:(i + 1) * hd] for i in heads], axis=0))
        ks.append(jnp.concatenate([kp[:, h * hd:(h + 1) * hd], kc[:, h * hd:(h + 1) * hd]], axis=0))
        vs.append(jnp.concatenate([vp[:, h * hd:(h + 1) * hd], vc[:, h * hd:(h + 1) * hd]], axis=0))
        sinks.append(jnp.concatenate([jnp.broadcast_to(sk[:, i:i + 1], (b, 1)) for i in heads], axis=0))
    return qs, ks, vs, sinks


def _unstack_heads(xs):
    b = ATTN_BLOCK
    return jnp.concatenate([x[g * b:(g + 1) * b] for x in xs for g in range(ATTN_GROUP)], axis=1)


def _attn_specs(nb):
    b = ATTN_BLOCK
    prev = lambda n: jnp.maximum(n - 1, 0)
    return [pl.BlockSpec((b, ATTN_P_W), lambda n: (n, 0)),
            pl.BlockSpec((b, 2 * ATTN_KV_W), lambda n: (prev(n), ATTN_Q_W // (2 * ATTN_KV_W))),
            pl.BlockSpec((b, 3 * LANE), lambda n: (n, 0)),
            pl.BlockSpec((b, 3 * LANE), lambda n: (prev(n), 0)),
            pl.BlockSpec((1, ATTN_Q_HEADS), lambda n: (0, 0))]


def attn_fwd(p, tab, sinks):
    t = p.shape[0]
    nb = t // ATTN_BLOCK

    def body(p_ref, kvp_ref, tab_ref, tabp_ref, sink_ref, o_ref):
        qs, ks, vs, sk = _attn_prepare(p_ref, kvp_ref, tab_ref, tabp_ref, sink_ref)
        os_ = _attn_heads(qs, ks, vs, sk, pl.program_id(0) == 0)
        o_ref[...] = _unstack_heads(os_).astype(BF16)

    return pl.pallas_call(
        body, name="attn_fwd",
        grid=(nb,),
        in_specs=_attn_specs(nb),
        out_specs=pl.BlockSpec((ATTN_BLOCK, ATTN_Q_W), lambda n: (n, 0)),
        out_shape=jax.ShapeDtypeStruct((t, ATTN_Q_W), BF16),
        compiler_params=_arb(1),
    )(p, p, tab, tab, sinks)


def attn_bwd(p, tab, sinks, do):
    t = p.shape[0]
    b = ATTN_BLOCK
    hd = ATTN_HEAD_DIM
    nb = t // b

    def body(p_ref, kvp_ref, tab_ref, tabp_ref, sink_ref, do_ref, dq_ref, dkvc_ref, dkvp_ref, dsink_ref):
        n = pl.program_id(0)

        @pl.when(n == 0)
        def _():
            dsink_ref[...] = jnp.zeros_like(dsink_ref)

        qs, ks, vs, sk = _attn_prepare(p_ref, kvp_ref, tab_ref, tabp_ref, sink_ref)
        first = n == 0
        _, vjp = jax.vjp(lambda a, bb, c, d: _attn_heads(a, bb, c, d, first), qs, ks, vs, sk)
        do = do_ref[...]
        dos = [jnp.concatenate([do[:, i * hd:(i + 1) * hd] for i in range(ATTN_GROUP * h, ATTN_GROUP * (h + 1))], axis=0)
               for h in range(ATTN_KV_HEADS)]
        dqs, dks, dvs, dsk = vjp(dos)
        tab, tabp = tab_ref[...], tabp_ref[...]
        dq_ref[...] = _rope(_unstack_heads([x.astype(F32) for x in dqs]), tab, -1.0).astype(BF16)
        dkc = jnp.concatenate([x.astype(F32)[b:] for x in dks], axis=1)
        dkp = jnp.concatenate([x.astype(F32)[:b] for x in dks], axis=1)
        dkvc_ref[:, 0:ATTN_KV_W] = _rope(dkc, tab, -1.0)
        dkvp_ref[:, 0:ATTN_KV_W] = _rope(dkp, tabp, -1.0)
        dkvc_ref[:, ATTN_KV_W:] = jnp.concatenate([x.astype(F32)[b:] for x in dvs], axis=1)
        dkvp_ref[:, ATTN_KV_W:] = jnp.concatenate([x.astype(F32)[:b] for x in dvs], axis=1)
        parts = [jnp.sum(d[g * b:(g + 1) * b], axis=0, keepdims=True) for d in dsk for g in range(ATTN_GROUP)]
        dsink_ref[...] += jnp.concatenate(parts, axis=1)

    blk = lambda w: pl.BlockSpec((b, w), lambda n: (n, 0))
    return pl.pallas_call(
        body, name="attn_bwd",
        grid=(nb,),
        in_specs=_attn_specs(nb) + [blk(ATTN_Q_W)],
        out_specs=[blk(ATTN_Q_W), blk(2 * ATTN_KV_W), blk(2 * ATTN_KV_W),
                   pl.BlockSpec((1, ATTN_Q_HEADS), lambda n: (0, 0))],
        out_shape=[jax.ShapeDtypeStruct((t, ATTN_Q_W), BF16), jax.ShapeDtypeStruct((t, 2 * ATTN_KV_W), F32),
                   jax.ShapeDtypeStruct((t, 2 * ATTN_KV_W), F32), jax.ShapeDtypeStruct((1, ATTN_Q_HEADS), F32)],
        compiler_params=_arb(1),
    )(p, p, tab, tab, sinks, do)


def kv_combine(dkvc, dkvp):
    t, w = dkvc.shape
    b = ATTN_BLOCK
    nb = t // b

    def body(c_ref, p_ref, o_ref):
        nxt = jnp.where(pl.program_id(0) < nb - 1, p_ref[...], 0.0)
        o_ref[...] = (c_ref[...] + nxt).astype(BF16)

    return pl.pallas_call(
        body, name="kv_combine",
        grid=(nb,),
        in_specs=[pl.BlockSpec((b, w), lambda n: (n, 0)),
                  pl.BlockSpec((b, w), lambda n: (jnp.minimum(n + 1, nb - 1), 0))],
        out_specs=pl.BlockSpec((b, w), lambda n: (n, 0)),
        out_shape=jax.ShapeDtypeStruct((t, w), BF16),
        compiler_params=_arb(1),
    )(dkvc, dkvp)


def _me():
    return lax.axis_index("x"), lax.axis_index("y"), lax.axis_index("c")


def _flip(v, bit):
    return 1 - v if bit else v


def _chip_index():
    return 2 * lax.axis_index("x") + lax.axis_index("y")


def allgather_chips(x, split):
    r, c = x.shape
    rh = r // 2 if split else r

    def body(x_ref, o_ref, send_sems, recv_sems):
        mx, my, mc = _me()
        me = 2 * mx + my
        rows = pl.ds(pl.multiple_of(mc * rh, 16), rh) if split else pl.ds(0, r)
        fetched, passed = [], []
        for k in (1, 2, 3):
            peer = (_flip(mx, k >> 1), _flip(my, k & 1), mc)
            cp = pltpu.make_async_remote_copy(src_ref=x_ref.at[rows], dst_ref=o_ref.at[me, rows],
                                              send_sem=send_sems.at[k - 1], recv_sem=recv_sems.at[k - 1],
                                              device_id=peer, device_id_type=MESH)
            cp.start()
            fetched.append(cp)
        if split:
            for k in (1, 2, 3):
                fetched[k - 1].wait_recv()
                theirs = o_ref.at[2 * _flip(mx, k >> 1) + _flip(my, k & 1), rows]
                cp = pltpu.make_async_remote_copy(src_ref=theirs, dst_ref=theirs, send_sem=send_sems.at[2 + k],
                                                  recv_sem=recv_sems.at[2 + k], device_id=(mx, my, 1 - mc),
                                                  device_id_type=MESH)
                cp.start()
                passed.append(cp)
            for cp in fetched:
                cp.wait_send()
            for cp in passed:
                cp.wait()
        else:
            for cp in fetched:
                cp.wait()

    got = pl.pallas_call(
        body, name="allgather_chips",
        in_specs=[ANY], out_specs=ANY,
        out_shape=jax.ShapeDtypeStruct((4, r, c), x.dtype),
        scratch_shapes=[pltpu.SemaphoreType.DMA((6,)), pltpu.SemaphoreType.DMA((6,))],
    )(x)
    return lax.dynamic_update_slice(got, x[None], (_chip_index(), 0, 0))


def exchange(kind, arrays):
    make, per_array, out_of = CARRIED[kind]
    na = len(arrays)

    def body(*refs):
        copies = make(refs[:na], refs[na:2 * na], refs[2 * na], refs[2 * na + 1])
        for cp in copies:
            cp.start()
        for cp in copies:
            cp.wait()

    return pl.pallas_call(
        body, name=kind + "_exchange",
        in_specs=[ANY] * na, out_specs=[ANY] * na,
        out_shape=[jax.ShapeDtypeStruct(out_of(a), a.dtype) for a in arrays],
        scratch_shapes=[pltpu.SemaphoreType.DMA((per_array * na,))] * 2,
    )(*arrays)


def _row_tile(r, c):
    tr = 8
    while r % (2 * tr) == 0 and 2 * tr * c * 4 <= (2 << 20):
        tr *= 2
    return tr


def add_pair(g, got, core):
    _, _, r, c = g.shape
    tr = _row_tile(r, c)

    def body(core_ref, g_ref, got_ref, p32_ref, p16_ref):
        s = g_ref[...] + got_ref[...]
        p32_ref[...] = s
        p16_ref[...] = s.astype(BF16)

    blk = pl.BlockSpec((None, tr, c), lambda j, i, core_ref: (j, i, 0))
    return pl.pallas_call(
        body, name="add_pair",
        grid_spec=pltpu.PrefetchScalarGridSpec(
            num_scalar_prefetch=1, grid=(4, r // tr),
            in_specs=[pl.BlockSpec((None, None, tr, c), lambda j, i, core_ref: (j, core_ref[0], i, 0)), blk],
            out_specs=[blk, blk]),
        out_shape=[jax.ShapeDtypeStruct((4, r, c), F32), jax.ShapeDtypeStruct((4, r, c), BF16)],
        compiler_params=_arb(2),
    )(core, g, got)


def sum_slots(p32, q16, order):
    _, r, c = p32.shape
    tr = _row_tile(r, c)

    def body(order_ref, own_ref, a_ref, b_ref, c_ref, o_ref):
        o_ref[...] = ((own_ref[...] + a_ref[...].astype(F32)) + b_ref[...].astype(F32)) + c_ref[...].astype(F32)

    slot = lambda k: pl.BlockSpec((None, tr, c), functools.partial(lambda k, i, order_ref: (order_ref[k], i, 0), k))
    return pl.pallas_call(
        body, name="sum_slots",
        grid_spec=pltpu.PrefetchScalarGridSpec(
            num_scalar_prefetch=1, grid=(r // tr,),
            in_specs=[slot(0), slot(1), slot(2), slot(3)],
            out_specs=pl.BlockSpec((tr, c), lambda i, order_ref: (i, 0))),
        out_shape=jax.ShapeDtypeStruct((r, c), F32),
        compiler_params=_arb(1),
    )(order, p32, q16, q16, q16)


def allreduce_small(x):
    r, c = x.shape

    def body(x_ref, o_ref, buf, send_sems, recv_sems):
        mx, my, mc = _me()
        me = 4 * mx + 2 * my + mc
        buf[pl.ds(me, 1)] = x_ref[...][None]
        copies = []
        for k in range(1, 8):
            peer = (_flip(mx, k >> 2), _flip(my, (k >> 1) & 1), _flip(mc, k & 1))
            cp = pltpu.make_async_remote_copy(src_ref=x_ref, dst_ref=buf.at[me], send_sem=send_sems.at[k - 1],
                                              recv_sem=recv_sems.at[k - 1], device_id=peer, device_id_type=MESH)
            cp.start()
            copies.append(cp)
        for cp in copies:
            cp.wait()
        acc = buf[0]
        for d in range(1, 8):
            acc = acc + buf[d]
        o_ref[...] = acc

    return pl.pallas_call(
        body, name="allreduce_small",
        out_shape=jax.ShapeDtypeStruct((r, c), F32),
        scratch_shapes=[pltpu.VMEM((8, r, c), F32), pltpu.SemaphoreType.DMA((7,)), pltpu.SemaphoreType.DMA((7,))],
    )(x)


class GradReduce:
    def __init__(self, grads, shard_shapes):
        self.shapes = shard_shapes
        self.groups = _by_width(shard_shapes)
        self.core = lax.axis_index("c")
        me = _chip_index()
        self.order = jnp.stack([me, me ^ 1, me ^ 2, me ^ 3]).astype(jnp.int32)
        self.g = []
        for names in self.groups.values():
            slots = []
            for j in range(4):
                parts = []
                for n in names:
                    w = shard_shapes[n][1 + SHARDED[n]]
                    parts += [lax.slice_in_dim(gl, j * w, (j + 1) * w, axis=SHARDED[n]) for gl in grads[n]]
                slots.append(jnp.concatenate(parts, axis=0))
            g = jnp.stack(slots)
            self.g.append(g.reshape(4, 2, g.shape[1] // 2, g.shape[2]))
        self.p32 = self.p16 = None
        self.q16 = [None] * len(self.g)
        n = len(self.g)
        self.pieces = [list(range(2, n)), [1], [0]] if n > 2 else [[i] for i in reversed(range(n))]
        self.pieces = [p for p in self.pieces if p]
        self.stage = 0

    def next_job(self):
        if self.stage == 0:
            return "pair", self.g
        if self.stage <= len(self.pieces):
            return "scatter", [self.p16[i] for i in self.pieces[self.stage - 1]]
        return None

    def deliver(self, res):
        if self.stage == 0:
            core = self.core.astype(jnp.int32)[None]
            pairs = [add_pair(g, got, core) for g, got in zip(self.g, res)]
            self.p32, self.p16 = [p[0] for p in pairs], [p[1] for p in pairs]
        else:
            for i, q in zip(self.pieces[self.stage - 1], res):
                self.q16[i] = q
        self.stage += 1

    def finish(self):
        while (job := self.next_job()) is not None:
            self.deliver(exchange(*job))
        mine = [sum_slots(p, q, self.order) for p, q in zip(self.p32, self.q16)]
        theirs = exchange("share", mine)
        out = {}
        for names, m, t in zip(self.groups.values(), mine, theirs):
            red = jnp.where(self.core == 0, jnp.concatenate([m, t]), jnp.concatenate([t, m]))
            off = 0
            for n in names:
                layers, a, _ = self.shapes[n]
                out[n] = red[off:off + layers * a].reshape(self.shapes[n])
                off += layers * a
        return out


PACK_W = 1024
PACK_ROWS = 1024

SHARDED = {"ffn1_w_gate_up": 1, "ffn1_w_down": 0, "ffn2_w_gate_up": 1, "ffn2_w_down": 0, "attn_w_in": 1,
           "attn_w_out": 0, "gdn_w_in": 1, "gdn_w_out": 0, "gdn_conv_w": 1}
REPLICATED = ["ffn1_norm", "mix_norm", "ffn2_norm", "attn_b_in", "attn_sinks", "attn_b_out", "gdn_A_log",
              "gdn_dt_bias", "gdn_norm_w", "final_norm"]
WEIGHTS = ["ffn1_norm", "ffn1_w_gate_up", "ffn1_w_down", "mix_norm", "ffn2_norm", "ffn2_w_gate_up", "ffn2_w_down",
           "attn_w_in", "attn_b_in", "attn_sinks", "attn_w_out", "attn_b_out", "gdn_w_in", "gdn_conv_w", "gdn_A_log",
           "gdn_dt_bias", "gdn_norm_w", "gdn_w_out", "final_norm"]


def _pack_rows(flats, dtype, width, row_multiple):
    flat = jnp.concatenate([f.astype(dtype).reshape(-1) for f in flats])
    per = width * row_multiple
    pad = (-flat.shape[0]) % per
    if pad:
        flat = jnp.concatenate([flat, jnp.zeros((pad,), dtype)])
    return flat.reshape(-1, width)


def _by_width(shapes):
    groups = {}
    for n, shape in shapes.items():
        groups.setdefault(shape[-1], []).append(n)
    return groups


def _gather_weights(shards, dtype, split=True):
    full = {}
    for width, names in _by_width({n: s.shape for n, s in shards.items()}).items():
        packed = jnp.concatenate([shards[n].astype(dtype).reshape(-1, width) for n in names], axis=0)
        got = allgather_chips(packed, split)
        off = 0
        for n in names:
            layers, a, _ = shards[n].shape
            full[n] = [jnp.concatenate([got[j, off + l * a:off + (l + 1) * a] for j in range(4)], axis=SHARDED[n])
                       for l in range(layers)]
            off += layers * a
    return full


def _small_pack(items):
    rows = []
    for a in items:
        f = a.astype(F32).reshape(-1)
        pad = (-f.shape[0]) % LANE
        rows.append(jnp.concatenate([f, jnp.zeros((pad,), F32)]) if pad else f)
    return _pack_rows(rows, F32, LANE, 8)


def _small_unpack(buf, shapes):
    flat = buf.reshape(-1)
    out, off = [], 0
    for shape in shapes:
        size = 1
        for s in shape:
            size *= s
        out.append(flat[off:off + size].reshape(shape))
        off += size + (-size) % LANE
    return out


def _rope_table(positions):
    t = positions.shape[0]
    inv_freq = ROPE_THETA ** (-jnp.arange(0, ROPE_DIM, 2, dtype=F32) / ROPE_DIM)
    ang = positions.astype(F32)[:, None] * inv_freq
    cos, sin = jnp.cos(ang), jnp.sin(ang)
    rest = ATTN_HEAD_DIM - ROPE_DIM
    zeros = lambda n: jnp.zeros((t, n), F32)
    c64 = jnp.concatenate([cos, cos, jnp.ones((t, rest), F32)], axis=1)
    s_up = jnp.concatenate([zeros(ROPE_DIM // 2), sin, zeros(rest)], axis=1)
    s_dn = jnp.concatenate([-sin, zeros(ROPE_DIM // 2 + rest)], axis=1)
    return jnp.concatenate([c64, c64, s_up, s_up, s_dn, s_dn], axis=1)


def _as2d(a):
    return a.reshape(-1, a.shape[-1]) if a.ndim > 1 else a.reshape(1, -1)


def kernel(x, positions, ffn1_norm, ffn1_w_gate_up, ffn1_w_down, mix_norm, ffn2_norm, ffn2_w_gate_up, ffn2_w_down, attn_w_in, attn_b_in, attn_sinks, attn_w_out, attn_b_out, gdn_w_in, gdn_conv_w, gdn_A_log, gdn_dt_bias, gdn_norm_w, gdn_w_out, final_norm, loss_target, m_ffn1_norm, m_ffn1_w_gate_up, m_ffn1_w_down, m_mix_norm, m_ffn2_norm, m_ffn2_w_gate_up, m_ffn2_w_down, m_attn_w_in, m_attn_b_in, m_attn_sinks, m_attn_w_out, m_attn_b_out, m_gdn_w_in, m_gdn_conv_w, m_gdn_A_log, m_gdn_dt_bias, m_gdn_norm_w, m_gdn_w_out, m_final_norm, v_ffn1_norm, v_ffn1_w_gate_up, v_ffn1_w_down, v_mix_norm, v_ffn2_norm, v_ffn2_w_gate_up, v_ffn2_w_down, v_attn_w_in, v_attn_b_in, v_attn_sinks, v_attn_w_out, v_attn_b_out, v_gdn_w_in, v_gdn_conv_w, v_gdn_A_log, v_gdn_dt_bias, v_gdn_norm_w, v_gdn_w_out, v_final_norm):
    given = dict(locals())
    w = {n: given[n] for n in WEIGHTS}
    d = D_MODEL
    h = x[0]
    target = loss_target[0]
    depth = ffn1_norm.shape[0]

    big = [n for n in SHARDED if n != "gdn_conv_w"]
    up_front = {n: (w[n][0:1] if n.startswith("ffn1") else w[n]) for n in big if not n.startswith("ffn2")}
    full = _gather_weights(up_front, BF16)
    ffn_order = [(tag, l) for l in range(depth) for tag in ("ffn1", "ffn2")]
    ffn_w = {ffn_order[0]: (full["ffn1_w_gate_up"][0], full["ffn1_w_down"][0])}

    def run_ffn(tag, l, h):
        i = ffn_order.index((tag, l))
        wgu, wd = ffn_w[(tag, l)]
        nw = w[tag + "_norm"][l][None]
        if i + 1 == len(ffn_order):
            return ffn_fwd(h, nw, wgu, wd)
        ntag, nl = ffn_order[i + 1]
        shards = (w[ntag + "_w_gate_up"][nl].astype(BF16), w[ntag + "_w_down"][nl].astype(BF16))
        h, gu, a, ngu, nd = ffn_fwd(h, nw, wgu, wd, shards)
        ffn_w[(ntag, nl)] = (ngu, nd)
        return h, gu, a

    conv_full = _gather_weights({"gdn_conv_w": gdn_conv_w}, F32, split=False)["gdn_conv_w"]
    tab = _rope_table(positions[0])
    zero_d = jnp.zeros((1, d), F32)

    def gdn_params(j):
        w_in = full["gdn_w_in"][j]
        w_cat = jnp.concatenate([w_in, jnp.zeros((d, GDN_PROJ_W - w_in.shape[1]), BF16)], axis=1)
        conv = jnp.concatenate([conv_full[j], jnp.zeros((8 - GDN_CONV, GDN_QKV_W), F32)], axis=0)
        lanes = lambda vec: jnp.concatenate([jnp.zeros((GC_G,), F32), vec, jnp.zeros((LANE - GC_G - GDN_HEADS,), F32)])
        par = jnp.concatenate([lanes(gdn_A_log[j])[None], lanes(gdn_dt_bias[j])[None], jnp.zeros((6, LANE), F32)], axis=0)
        return w_cat, conv, par

    saved = []
    for l in range(depth):
        j = l // 2
        rec = {"h1": h}
        h, rec["ffn1_gu"], rec["ffn1_a"] = run_ffn("ffn1", l, h)
        rec["h2"] = h
        if l % 2 == 0:
            p = norm_proj(h, mix_norm[l][None], full["attn_w_in"][j], attn_b_in[j][None])
            o = attn_fwd(p, tab, attn_sinks[j][None])
            h = linear_residual(h, o, full["attn_w_out"][j], attn_b_out[j][None])
            rec.update(p=p, o=o)
        else:
            w_cat, conv, par = gdn_params(j)
            proj = norm_proj(h, mix_norm[l][None], w_cat, jnp.zeros((1, GDN_PROJ_W), F32))
            qkv, gc = gdn_pre_fwd(proj, conv, par)
            on, st = gdn_chunk_fwd(qkv, gc, proj, gdn_norm_w[j][None])
            h = linear_residual(h, on, full["gdn_w_out"][j], zero_d)
            rec.update(proj=proj, qkv=qkv, gc=gc, on=on, st=st, w_cat=w_cat, conv=conv, par=par)
        rec["h3"] = h
        h, rec["ffn2_gu"], rec["ffn2_a"] = run_ffn("ffn2", l, h)
        saved.append(rec)

    dh, loss_tile, d_final = loss_head(h, final_norm[None], target)

    g = {n: [None] * w[n].shape[0] for n in WEIGHTS if n != "final_norm"}
    reducers = []

    def layer_of(n, i):
        return i if n.startswith("ffn") else (2 * i if n.startswith("attn") else 2 * i + 1)

    def phase_reduce(lo, hi):
        picks = {n: [i for i in range(w[n].shape[0]) if lo <= layer_of(n, i) < hi] for n in big}
        return GradReduce({n: [g[n][i] for i in idx] for n, idx in picks.items()},
                          {n: (len(idx),) + w[n].shape[1:] for n, idx in picks.items()})

    def tn(x, y, scale=1.0):
        job = reducers[0].next_job() if reducers and x.shape[1] * y.shape[1] >= D_MODEL * D_FF else None
        if job is None:
            return matmul_tn(x, y, scale)
        out, res = matmul_tn(x, y, scale, carry=job)
        reducers[0].deliver(res)
        return out
    for l in reversed(range(depth)):
        j = l // 2
        rec = saved[l]

        def ffn_back(tag, h_in, dh):
            wgu, wd = ffn_w[(tag, l)]
            dh_new, hn, dgu, dn = ffn_bwd(h_in, w[tag + "_norm"][l][None], wgu, wd, rec[tag + "_gu"], dh)
            g[tag + "_w_gate_up"][l] = tn(hn, dgu)
            g[tag + "_w_down"][l] = tn(rec[tag + "_a"], dh, 0.5)
            g[tag + "_norm"][l] = dn[0]
            return dh_new

        dh = ffn_back("ffn2", rec["h3"], dh)
        if l % 2 == 0:
            w_in, w_out = full["attn_w_in"][j], full["attn_w_out"][j]
            do, db_out = matmul_nt(dh, w_out)
            g["attn_w_out"][j] = matmul_tn(rec["o"], dh)
            g["attn_b_out"][j] = db_out[0]
            dq, dkvc, dkvp, dsink = attn_bwd(rec["p"], tab, attn_sinks[j][None], do)
            dkv = kv_combine(dkvc, dkvp)
            dh, hn, dn, cs_q, cs_kv = norm_proj_bwd(rec["h2"], mix_norm[l][None], dh, [dq, dkv],
                                                    [w_in[:, :ATTN_Q_W], w_in[:, ATTN_Q_W:]])
            g["attn_w_in"][j] = jnp.concatenate([matmul_tn(hn, dq), matmul_tn(hn, dkv)], axis=1)
            g["attn_b_in"][j] = jnp.concatenate([cs_q[0], cs_kv[0]])
            g["attn_sinks"][j] = dsink[0]
        else:
            w_cat, conv, par = rec["w_cat"], rec["conv"], rec["par"]
            d_on, _ = matmul_nt(dh, full["gdn_w_out"][j])
            g["gdn_w_out"][j] = matmul_tn(rec["on"], dh)
            dq, dk, dv, dz, dgc_heads, dnw = gdn_chunk_bwd(rec["qkv"], rec["gc"], rec["proj"], gdn_norm_w[j][None],
                                                           rec["st"], d_on)
            dy, dba, dpar = gdn_pre_bwd(rec["proj"], conv, par, dq, dk, dv, dgc_heads)
            dx, dconv = gdn_conv_bwd(rec["proj"], conv, dy)
            nz = GDN_QKV_W + GDN_HEADS * GDN_DK
            dh, hn, dn, _, _, _ = norm_proj_bwd(rec["h2"], mix_norm[l][None], dh, [dx, dz, dba],
                                                [w_cat[:, :GDN_QKV_W], w_cat[:, GDN_QKV_W:nz], w_cat[:, nz:]])
            g["gdn_w_in"][j] = jnp.concatenate(
                [tn(hn, dx), matmul_tn(hn, dz), matmul_tn(hn, dba)[:, :2 * GDN_HEADS]], axis=1)
            g["gdn_conv_w"][j] = dconv[:GDN_CONV]
            g["gdn_A_log"][j] = dpar[0, GDN_HEADS:2 * GDN_HEADS]
            g["gdn_dt_bias"][j] = dpar[1, GDN_HEADS:2 * GDN_HEADS]
            g["gdn_norm_w"][j] = dnw[0]
        g["mix_norm"][l] = dn[0]
        dh = ffn_back("ffn1", rec["h1"], dh)
        if l == depth // 2:
            reducers.append(phase_reduce(depth // 2, depth))
    grad_x = dh[None]

    small_names = REPLICATED + ["gdn_conv_w"]
    local = {n: jnp.stack(g[n]) for n in small_names if n != "final_norm"}
    local["final_norm"] = d_final[0]
    small_shapes = [(1,)] + [local[n].shape for n in small_names]
    small = allreduce_small(_small_pack([loss_tile[0, 0:1]] + [local[n] for n in small_names]))
    small = _small_unpack(small, small_shapes)
    loss = small[0][0]
    grads = dict(zip(small_names, small[1:]))
    conv_cols = gdn_conv_w.shape[2]
    grads["gdn_conv_w"] = lax.dynamic_slice_in_dim(grads["gdn_conv_w"], _chip_index() * conv_cols, conv_cols, axis=2)
    upper, lower = reducers[0].finish(), phase_reduce(0, depth // 2).finish()
    grads.update({n: jnp.concatenate([lower[n], upper[n]], axis=0) for n in big})

    delta, new_m, new_v = {}, {}, {}
    for n in SHARDED:
        dl, nm, nv = adamw(_as2d(w[n]), _as2d(grads[n]), _as2d(given["m_" + n]), _as2d(given["v_" + n]))
        delta[n], new_m[n], new_v[n] = dl.reshape(w[n].shape), nm.reshape(w[n].shape), nv.reshape(w[n].shape)
    shapes = [w[n].shape for n in REPLICATED]
    packed = [_small_pack([src[n] for n in REPLICATED]) for src in
              (w, grads, {n: given["m_" + n] for n in REPLICATED}, {n: given["v_" + n] for n in REPLICATED})]
    for dst, buf in zip((delta, new_m, new_v), adamw(*packed)):
        dst.update(zip(REPLICATED, _small_unpack(buf, shapes)))

    return (loss, grad_x, *[grads[n] for n in WEIGHTS], *[delta[n] for n in WEIGHTS],
            *[new_m[n] for n in WEIGHTS], *[new_v[n] for n in WEIGHTS])
```

```python
import functools

import jax
import jax.numpy as jnp
from jax import lax
from jax.experimental import pallas as pl
from jax.experimental.pallas import tpu as pltpu

F32 = jnp.float32
BF16 = jnp.bfloat16
HI = lax.Precision.HIGHEST
MESH = pl.DeviceIdType.MESH

D_MODEL = 1024
D_FF = 2816
DEPTH = 4
NORM_EPS = 1e-6
LANE = 128

ATTN_Q_HEADS = 16
ATTN_KV_HEADS = 4
ATTN_HEAD_DIM = 64
ATTN_GROUP = 4
ATTN_BLOCK = 128
ROPE_DIM = 16
ROPE_THETA = 500000.0
ATTN_Q_W = 1024
ATTN_KV_W = 256

GDN_HEADS = 8
GDN_DK = 128
GDN_CONV = 4
GDN_CHUNK = 64
GDN_QKV_W = 3072

ADAM_LR = 0.001
ADAM_B1 = 0.9
ADAM_B2 = 0.999
ADAM_EPS = 1e-08
ADAM_WD = 0.01
ADAM_STEP = 10

NEG = -1e30


def _dot(a, b, prec=None):
    return lax.dot_general(a, b, (((1,), (0,)), ((), ())), precision=prec, preferred_element_type=F32)


def _dot_nt(a, b, prec=None):
    return lax.dot_general(a, b, (((1,), (1,)), ((), ())), precision=prec, preferred_element_type=F32)


def _dot_tn(a, b, prec=None):
    return lax.dot_general(a, b, (((0,), (0,)), ((), ())), precision=prec, preferred_element_type=F32)


def _bdot(a, b):
    return _dot(a.astype(BF16), b.astype(BF16))


def _bdot_nt(a, b):
    return _dot_nt(a.astype(BF16), b.astype(BF16))


def _bdot_tn(a, b):
    return _dot_tn(a.astype(BF16), b.astype(BF16))


def _sigmoid(x):
    return 1.0 / (1.0 + jnp.exp(-x))


def _silu(x):
    return x * _sigmoid(x)


def _silu_grad(x):
    s = _sigmoid(x)
    return s * (1.0 + x * (1.0 - s))


def _rms(x, w):
    r = lax.rsqrt(jnp.mean(x * x, axis=-1, keepdims=True) + NORM_EPS)
    xhat = x * r
    return xhat * w, xhat, r


def _rms_bwd(dy, w, xhat, r):
    dxhat = dy * w
    dx = r * (dxhat - xhat * jnp.mean(dxhat * xhat, axis=-1, keepdims=True))
    dw = jnp.sum(dy * xhat, axis=0, keepdims=True)
    return dx, dw


def _arb(n):
    return pltpu.CompilerParams(dimension_semantics=("arbitrary",) * n)


def _tile(n, want):
    t = min(n, want)
    assert n % t == 0, (n, want)
    return t


def _iota2(shape, dim):
    return lax.broadcasted_iota(jnp.int32, shape, dim)


_NN = (((1,), (0,)), ((), ()))
_NT = (((1,), (1,)), ((), ()))
_TN = (((0,), (0,)), ((), ()))


def _raw1(a, b, dn):
    return lax.dot_general(a.astype(BF16), b.astype(BF16), dn, preferred_element_type=F32)


def _raw3(a, b, dn):
    ah, bh = a.astype(BF16), b.astype(BF16)
    al, bl = (a - ah.astype(F32)).astype(BF16), (b - bh.astype(F32)).astype(BF16)
    f = lambda x, y: lax.dot_general(x, y, dn, preferred_element_type=F32)
    return f(ah, bh) + f(ah, bl) + f(al, bh)


def _make_mm(raw):
    @jax.custom_vjp
    def mm(a, b):
        return raw(a, b, _NN)

    mm.defvjp(lambda a, b: (raw(a, b, _NN), (a, b)),
              lambda res, ct: (raw(ct, res[1], _NT), raw(res[0], ct, _TN)))

    @jax.custom_vjp
    def mm_nt(a, b):
        return raw(a, b, _NT)

    mm_nt.defvjp(lambda a, b: (raw(a, b, _NT), (a, b)),
                 lambda res, ct: (raw(ct, res[1], _NN), raw(ct, res[0], _TN)))
    return mm, mm_nt


_mm1, _mm1_nt = _make_mm(_raw1)
_mm3, _mm3_nt = _make_mm(_raw3)


def _eye(n):
    return (_iota2((n, n), 0) == _iota2((n, n), 1)).astype(F32)


def _each(f, *lists):
    return [f(*t) for t in zip(*lists)]


def _inv_newton(mats):
    eye = _eye(mats[0].shape[0])
    ps = [-a for a in mats]
    ms = [eye + p for p in ps]
    k = 1
    while 2 * k < GDN_CHUNK:
        ps = [_raw1(p, p, _NN) for p in ps]
        ms = _each(lambda m, p: m + _raw1(m, p, _NN), ms, ps)
        k *= 2
    rs = _each(lambda a, m: eye - m - _raw3(a, m, _NN), mats, ms)
    return _each(lambda m, r: m + _raw1(m, r, _NN), ms, rs)


@jax.custom_vjp
def _unit_lower_inv(mats):
    return _inv_newton(mats)


def _unit_lower_inv_fwd(mats):
    ms = _inv_newton(mats)
    return ms, ms


def _unit_lower_inv_bwd(ms, dms):
    ts = _each(lambda m, dm: _raw3(m, dm, _TN), ms, dms)
    return (_each(lambda t, m: -_raw3(t, m, _NT), ts, ms),)


_unit_lower_inv.defvjp(_unit_lower_inv_fwd, _unit_lower_inv_bwd)


def _gdn_local(q, k, v, dcb, dcb128, dlb128, bb128):
    n = q[0].shape[0]
    r, s = _iota2((n, n), 0), _iota2((n, n), 1)
    same = (r // GDN_CHUNK) == (s // GDN_CHUNK)
    causal, strict = same & (r >= s), same & (r > s)
    eye = _eye(n)
    decay_l = [jnp.exp(jnp.where(causal, d - d.T, NEG)) for d in dcb]
    kb = _each(lambda a, b: a * b, k, bb128)
    a = _each(lambda x, y, dl: jnp.where(strict, _mm1_nt(x, y) * dl, 0.0), kb, k, decay_l)
    m_off = [m - eye for m in _unit_lower_inv(a)]
    edc = [jnp.exp(d) for d in dcb128]
    rhs = _each(lambda vv, bb, kk, e: jnp.concatenate([vv * bb, kk * e], axis=1), v, bb128, kb, edc)
    sol = _each(lambda x, m: x + _mm3(m, x), rhs, m_off)
    aqk = _each(lambda x, y, dl: jnp.where(causal, _mm1_nt(x, y) * dl, 0.0), q, k, decay_l)
    q_dec = _each(lambda x, e: x * e, q, edc)
    k_dec = _each(lambda x, dl, dc: x * jnp.exp(dl - dc), k, dlb128, dcb128)
    return sol, aqk, q_dec, k_dec


def _gated_norm(o, z, nw):
    r = lax.rsqrt(jnp.mean(o * o, axis=-1, keepdims=True) + NORM_EPS)
    return o * r * nw * _silu(z)


def _gated_norm_bwd(dy, o, z, nw):
    r = lax.rsqrt(jnp.mean(o * o, axis=-1, keepdims=True) + NORM_EPS)
    xhat = o * r
    sz = _silu(z)
    dxhat = dy * nw * sz
    do = r * (dxhat - xhat * jnp.mean(dxhat * xhat, axis=-1, keepdims=True))
    dz = dy * xhat * nw * _silu_grad(z)
    dnw = jnp.sum(dy * xhat * sz, axis=0, keepdims=True)
    return do, dz, dnw


GDN_HEAD_GROUP = 8
GC_BETA, GC_G, GC_DECAY, GC_LAST =0, GDN_HEADS, 2 * GDN_HEADS, 3 * GDN_HEADS


def _gate_cols(gc, h, rows):
    lane = _iota2((rows, LANE), 1)
    col = lambda off: jnp.sum(jnp.where(lane == off + h, gc, 0.0), axis=-1, keepdims=True)
    return col(GC_BETA), col(GC_DECAY), col(GC_LAST)


def _gdn_local_args(q_ref, k_ref, v_ref, gc_ref, hp, n):
    gc = gc_ref[...]
    lanes = [slice(hh * LANE, (hh + 1) * LANE) for hh in range(hp)]
    cols = [_gate_cols(gc, pl.program_id(0) * hp + hh, n) for hh in range(hp)]
    bcast = lambda i, w: [jnp.broadcast_to(c[i], (n, w)) for c in cols]
    args = ([q_ref[:, ls] for ls in lanes], [k_ref[:, ls] for ls in lanes], [v_ref[:, ls] for ls in lanes],
            bcast(1, n), bcast(1, LANE), bcast(2, LANE), bcast(0, LANE))
    return args, [c[2] for c in cols], lanes


def gdn_chunk_fwd(qkv, gc, proj, norm_w):
    t = qkv.shape[0]
    c = GDN_CHUNK
    tc = _tile(t, 256)
    nsub = tc // c

    hp = GDN_HEAD_GROUP

    def body(q_ref, k_ref, v_ref, gc_ref, z_ref, nw_ref, on_ref, st_ref, s_ref):
        @pl.when(pl.program_id(1) == 0)
        def _():
            s_ref[...] = jnp.zeros_like(s_ref)

        args, dl, lanes = _gdn_local_args(q_ref, k_ref, v_ref, gc_ref, hp, tc)
        sol, aqk, q_dec, k_dec = _gdn_local(*args)
        u = [x[:, :GDN_DK] for x in sol]
        w = [x[:, GDN_DK:] for x in sol]
        s = [s_ref[hh] for hh in range(hp)]
        v_new, o_state = [[] for _ in range(hp)], [[] for _ in range(hp)]
        for j in range(nsub):
            sl = slice(j * c, (j + 1) * c)
            for hh in range(hp):
                st_ref[hh, j] = s[hh]
            vn = _each(lambda uu, ww, ss: uu[sl] - _bdot(ww[sl], ss), u, w, s)
            os_ = _each(lambda qq, ss: _bdot(qq[sl], ss), q_dec, s)
            s = _each(lambda ss, d, kk, vv: ss * jnp.exp(d[j * c:j * c + 1]) + _bdot_tn(kk[sl], vv), s, dl, k_dec, vn)
            for hh in range(hp):
                v_new[hh].append(vn[hh])
                o_state[hh].append(os_[hh])
        for hh in range(hp):
            s_ref[hh] = s[hh]
        cat = lambda xs: jnp.concatenate(xs, axis=0)
        o = _each(lambda os_, aa, vv: cat(os_) + _bdot(aa, cat(vv)), o_state, aqk, v_new)
        for hh, ls in enumerate(lanes):
            on_ref[:, ls] = _gated_norm(o[hh], z_ref[:, ls], nw_ref[...]).astype(BF16)

    col = lambda off: pl.BlockSpec((tc, hp * LANE), lambda g, i: (i, off // hp + g))
    return pl.pallas_call(
        body, name="gdn_chunk_fwd",
        grid=(GDN_HEADS // hp, t // tc),
        in_specs=[col(0), col(GDN_HEADS), col(2 * GDN_HEADS),
                  pl.BlockSpec((tc, LANE), lambda g, i: (i, 0)),
                  col(GDN_QKV_W // LANE),
                  pl.BlockSpec((1, LANE), lambda g, i: (0, 0))],
        out_specs=[col(0),
                   pl.BlockSpec((hp, nsub, GDN_DK, GDN_DK), lambda g, i: (g, i, 0, 0))],
        out_shape=[jax.ShapeDtypeStruct((t, GDN_HEADS * GDN_DK), BF16),
                   jax.ShapeDtypeStruct((GDN_HEADS, t // c, GDN_DK, GDN_DK), F32)],
        scratch_shapes=[pltpu.VMEM((hp, GDN_DK, GDN_DK), F32)],
        compiler_params=_arb(2),
    )(qkv, qkv, qkv, gc, proj, norm_w)


def gdn_chunk_bwd(qkv, gc, proj, norm_w, states, d_on):
    t = qkv.shape[0]
    c = GDN_CHUNK
    tc = _tile(t, 256)
    nsub = tc // c
    nblk = t // tc
    hp = GDN_HEAD_GROUP

    def body(q_ref, k_ref, v_ref, gc_ref, z_ref, nw_ref, st_ref, don_ref,
             dq_ref, dk_ref, dv_ref, dz_ref, dgc_ref, dnw_ref, ds_ref):
        @pl.when(pl.program_id(1) == 0)
        def _():
            ds_ref[...] = jnp.zeros_like(ds_ref)

        @pl.when((pl.program_id(1) == 0) & (pl.program_id(0) == 0))
        def _():
            dnw_ref[...] = jnp.zeros_like(dnw_ref)

        rows = [slice(j * c, (j + 1) * c) for j in range(nsub)]
        cat = lambda xs: jnp.concatenate(xs, axis=0)
        lsum = lambda x: jnp.sum(x, axis=-1, keepdims=True)
        lane = _iota2((tc, LANE), 1)
        row = _iota2((tc, 1), 0)
        heads = range(hp)
        args, dl, lanes = _gdn_local_args(q_ref, k_ref, v_ref, gc_ref, hp, tc)
        (sol, aqk, q_dec, k_dec), vjp = jax.vjp(_gdn_local, *args)
        u = [x[:, :GDN_DK] for x in sol]
        w = [x[:, GDN_DK:] for x in sol]
        states = [[st_ref[hh, j] for j in range(nsub)] for hh in heads]
        v_new = _each(lambda uu, ww, st: [uu[sl] - _bdot(ww[sl], s) for sl, s in zip(rows, st)], u, w, states)
        v_all = [cat(v) for v in v_new]
        o = _each(lambda qq, st, aa, vv: cat([_bdot(qq[sl], s) for sl, s in zip(rows, st)]) + _bdot(aa, vv),
                  q_dec, states, aqk, v_all)
        nw = nw_ref[...]
        gn = [_gated_norm_bwd(don_ref[:, ls], o[hh], z_ref[:, ls], nw) for hh, ls in enumerate(lanes)]
        do = [x[0] for x in gn]
        for hh, ls in enumerate(lanes):
            dnw_ref[...] += gn[hh][2]
            dz_ref[:, ls] = gn[hh][1].astype(BF16)
        d_aqk = _each(_bdot_nt, do, v_all)
        dv_o = _each(_bdot_tn, aqk, do)
        ds = [ds_ref[hh] for hh in heads]
        d_u, d_w, d_qdec, d_kdec, d_last = ([[None] * nsub for _ in heads] for _ in range(5))
        for j in reversed(range(nsub)):
            sl = rows[j]
            sj = [states[hh][j] for hh in heads]
            cd = [jnp.exp(d[j * c:j * c + 1]) for d in dl]
            du = _each(lambda dvo, kk, dd: dvo[sl] + _bdot(kk[sl], dd), dv_o, k_dec, ds)
            dqd = _each(lambda dd, s: _bdot_nt(dd[sl], s), do, sj)
            dkd = _each(lambda vv, dd: _bdot_nt(vv[j], dd), v_new, ds)
            dla = _each(lambda s, dd, cc: jnp.sum(lsum(s * dd), axis=0, keepdims=True) * cc, sj, ds, cd)
            dw = _each(lambda x, s: -_bdot_nt(x, s), du, sj)
            ds = _each(lambda qq, dd, cc, dsn, ww, x: _bdot_tn(qq[sl], dd[sl]) + cc * dsn - _bdot_tn(ww[sl], x),
                       q_dec, do, cd, ds, w, du)
            for hh in heads:
                d_u[hh][j], d_w[hh][j], d_qdec[hh][j], d_kdec[hh][j], d_last[hh][j] = du[hh], dw[hh], dqd[hh], dkd[hh], dla[hh]
        for hh in heads:
            ds_ref[hh] = ds[hh]
        d_sol = _each(lambda a, b: jnp.concatenate([cat(a), cat(b)], axis=1), d_u, d_w)
        dq, dk, dv, d_dcb, d_dcb128, d_dlb128, d_bb = vjp((d_sol, d_aqk, [cat(x) for x in d_qdec], [cat(x) for x in d_kdec]))
        dgc = jnp.zeros((tc, LANE), F32)
        for hh, ls in enumerate(lanes):
            h = pl.program_id(0) * hp + hh
            dq_ref[:, ls] = dq[hh]
            dk_ref[:, ls] = dk[hh]
            dv_ref[:, ls] = dv[hh]
            d_dl = lsum(d_dlb128[hh])
            for j in range(nsub):
                d_dl = d_dl + jnp.where(row == j * c, d_last[hh][j], 0.0)
            dgc = dgc + jnp.where(lane == GC_BETA + h, lsum(d_bb[hh]),
                                  jnp.where(lane == GC_DECAY + h, lsum(d_dcb[hh]) + lsum(d_dcb128[hh]),
                                            jnp.where(lane == GC_LAST + h, d_dl, 0.0)))
        dgc_ref[0] = dgc

    rev = lambda i: nblk - 1 - i
    col = lambda off: pl.BlockSpec((tc, hp * LANE), lambda g, i: (rev(i), off // hp + g))
    return pl.pallas_call(
        body, name="gdn_chunk_bwd",
        grid=(GDN_HEADS // hp, nblk),
        in_specs=[col(0), col(GDN_HEADS), col(2 * GDN_HEADS),
                  pl.BlockSpec((tc, LANE), lambda g, i: (rev(i), 0)),
                  col(GDN_QKV_W // LANE),
                  pl.BlockSpec((1, LANE), lambda g, i: (0, 0)),
                  pl.BlockSpec((hp, nsub, GDN_DK, GDN_DK), lambda g, i: (g, rev(i), 0, 0)),
                  col(0)],
        out_specs=[col(0), col(0), col(0), col(0),
                   pl.BlockSpec((1, tc, LANE), lambda g, i: (g, rev(i), 0)),
                   pl.BlockSpec((1, LANE), lambda g, i: (0, 0))],
        out_shape=[
            jax.ShapeDtypeStruct((t, GDN_HEADS * GDN_DK), F32),
            jax.ShapeDtypeStruct((t, GDN_HEADS * GDN_DK), F32),
            jax.ShapeDtypeStruct((t, GDN_HEADS * GDN_DK), F32),
            jax.ShapeDtypeStruct((t, GDN_HEADS * GDN_DK), BF16),
            jax.ShapeDtypeStruct((GDN_HEADS // hp, t, LANE), F32),
            jax.ShapeDtypeStruct((1, LANE), F32)],
        scratch_shapes=[pltpu.VMEM((hp, GDN_DK, GDN_DK), F32)],
        compiler_params=_arb(2),
    )(qkv, qkv, qkv, gc, proj, norm_w, states, d_on)


GDN_PROJ_W = GDN_QKV_W + 1024 + LANE
GDN_Q_SCALE = GDN_DK ** -0.5


def _shift_rows(ext, shift, lo, n):
    if shift == 0:
        return ext[lo:lo + n]
    return pltpu.roll(ext, shift, 0)[lo:lo + n]


def _conv_fwd(x, halo, w):
    n = x.shape[0]
    ext = jnp.concatenate([halo, x], axis=0)
    y = w[GDN_CONV - 1:GDN_CONV] * x
    for j in range(GDN_CONV - 1):
        y = y + w[j:j + 1] * _shift_rows(ext, GDN_CONV - 1 - j, 8, n)
    return y


def _chunk_masks(n):
    r, s = _iota2((n, n), 0), _iota2((n, n), 1)
    same = (r // GDN_CHUNK) == (s // GDN_CHUNK)
    return (same & (r >= s)).astype(F32), (same & (r <= s)).astype(F32), same.astype(F32)


def _softplus(x):
    return jnp.maximum(x, 0.0) + jnp.log(1.0 + jnp.exp(-jnp.abs(x)))


def _l2n(t):
    rs = lax.rsqrt(jnp.sum(t * t, axis=-1, keepdims=True) + NORM_EPS)
    return t * rs, rs


def gdn_pre_fwd(proj, conv_w, gate_par):
    t = proj.shape[0]
    tm = _tile(t, 256)

    def body(x_ref, halo_ref, ba_ref, w_ref, gp_ref, qkv_ref, gc_ref):
        i = pl.program_id(0)
        halo = jnp.where(i > 0, halo_ref[...], 0.0)
        y = _silu(_conv_fwd(x_ref[...], halo, w_ref[...]))
        for hh in range(2 * GDN_HEADS):
            sl = slice(hh * LANE, (hh + 1) * LANE)
            tn, _ = _l2n(y[:, sl])
            qkv_ref[:, sl] = tn * GDN_Q_SCALE if hh < GDN_HEADS else tn
        qkv_ref[:, 2 * GDN_HEADS * LANE:] = y[:, 2 * GDN_HEADS * LANE:]
        ba = ba_ref[...]
        lane = _iota2(ba.shape, 1)
        gp = gp_ref[...]
        is_a = (lane >= GC_G) & (lane < GC_G + GDN_HEADS)
        g = jnp.where(is_a, -jnp.exp(gp[0:1]) * _softplus(ba + gp[1:2]), 0.0)
        tri, _, same = _chunk_masks(tm)
        decay = pltpu.roll(_dot(tri, g, HI), GC_DECAY - GC_G, 1)
        last = pltpu.roll(_dot(same, g, HI), GC_LAST - GC_G, 1)
        gc_ref[...] = jnp.where(lane < GDN_HEADS, _sigmoid(ba), g) + decay + last

    return pl.pallas_call(
        body, name="gdn_pre_fwd",
        grid=(t // tm,),
        in_specs=[pl.BlockSpec((tm, GDN_QKV_W), lambda i: (i, 0)),
                  pl.BlockSpec((8, GDN_QKV_W), lambda i: (jnp.maximum(i * (tm // 8) - 1, 0), 0)),
                  pl.BlockSpec((tm, LANE), lambda i: (i, (GDN_QKV_W + 1024) // LANE)),
                  pl.BlockSpec((8, GDN_QKV_W), lambda i: (0, 0)),
                  pl.BlockSpec((8, LANE), lambda i: (0, 0))],
        out_specs=[pl.BlockSpec((tm, GDN_QKV_W), lambda i: (i, 0)),
                   pl.BlockSpec((tm, LANE), lambda i: (i, 0))],
        out_shape=[jax.ShapeDtypeStruct((t, GDN_QKV_W), F32), jax.ShapeDtypeStruct((t, LANE), F32)],
        compiler_params=_arb(1),
    )(proj, proj, proj, conv_w, gate_par)


def gdn_pre_bwd(proj, conv_w, gate_par, dq, dk, dv, dgc_heads):
    t = proj.shape[0]
    tm = _tile(t, 256)

    def body(x_ref, halo_ref, ba_ref, w_ref, gp_ref, dq_ref, dk_ref, dv_ref, dgc_ref, dy_ref, dba_ref, dgp_ref):
        i = pl.program_id(0)

        @pl.when(i == 0)
        def _():
            dgp_ref[...] = jnp.zeros_like(dgp_ref)

        halo = jnp.where(i > 0, halo_ref[...], 0.0)
        y = _conv_fwd(x_ref[...], halo, w_ref[...])
        for hh in range(3 * GDN_HEADS):
            sl = slice(hh * LANE, (hh + 1) * LANE)
            hsl = slice((hh % GDN_HEADS) * LANE, (hh % GDN_HEADS + 1) * LANE)
            yy = y[:, sl]
            if hh < 2 * GDN_HEADS:
                tn, rs = _l2n(_silu(yy))
                dtn = dq_ref[:, hsl] * GDN_Q_SCALE if hh < GDN_HEADS else dk_ref[:, hsl]
                dsil = rs * (dtn - tn * jnp.sum(dtn * tn, axis=-1, keepdims=True))
            else:
                dsil = dv_ref[:, hsl]
            dy_ref[:, sl] = dsil * _silu_grad(yy)
        dgc = dgc_ref[0]
        for hh in range(1, dgc_heads.shape[0]):
            dgc = dgc + dgc_ref[hh]
        ba = ba_ref[...]
        lane = _iota2(ba.shape, 1)
        _, tri_t, same = _chunk_masks(tm)
        d_decay = jnp.where((lane >= GC_DECAY) & (lane < GC_DECAY + GDN_HEADS), dgc, 0.0)
        d_last = jnp.where((lane >= GC_LAST) & (lane < GC_LAST + GDN_HEADS), dgc, 0.0)
        dgc = (jnp.where(lane < GDN_HEADS, dgc, 0.0) + pltpu.roll(_dot(tri_t, d_decay, HI), LANE - (GC_DECAY - GC_G), 1)
               + pltpu.roll(_dot(same, d_last, HI), LANE - (GC_LAST - GC_G), 1))
        gp = gp_ref[...]
        xg = ba + gp[1:2]
        ea = jnp.exp(gp[0:1])
        sp = _softplus(xg)
        sb = _sigmoid(ba)
        is_b = lane < GDN_HEADS
        is_a = (lane >= GDN_HEADS) & (lane < 2 * GDN_HEADS)
        d_pre = jnp.where(is_a, dgc * (-ea) * _sigmoid(xg), 0.0)
        dba_ref[...] = jnp.where(is_b, dgc * sb * (1.0 - sb), d_pre).astype(BF16)
        d_alog = jnp.sum(jnp.where(is_a, dgc * (-ea) * sp, 0.0), axis=0, keepdims=True)
        d_dtb = jnp.sum(d_pre, axis=0, keepdims=True)
        row = _iota2((8, LANE), 0)
        dgp_ref[...] += jnp.where(row == 0, d_alog, jnp.where(row == 1, d_dtb, 0.0))

    hspec = pl.BlockSpec((tm, GDN_HEADS * LANE), lambda i: (i, 0))
    return pl.pallas_call(
        body, name="gdn_pre_bwd",
        grid=(t // tm,),
        in_specs=[pl.BlockSpec((tm, GDN_QKV_W), lambda i: (i, 0)),
                  pl.BlockSpec((8, GDN_QKV_W), lambda i: (jnp.maximum(i * (tm // 8) - 1, 0), 0)),
                  pl.BlockSpec((tm, LANE), lambda i: (i, (GDN_QKV_W + 1024) // LANE)),
                  pl.BlockSpec((8, GDN_QKV_W), lambda i: (0, 0)),
                  pl.BlockSpec((8, LANE), lambda i: (0, 0)),
                  hspec, hspec, hspec,
                  pl.BlockSpec((dgc_heads.shape[0], tm, LANE), lambda i: (0, i, 0))],
        out_specs=[pl.BlockSpec((tm, GDN_QKV_W), lambda i: (i, 0)),
                   pl.BlockSpec((tm, LANE), lambda i: (i, 0)),
                   pl.BlockSpec((8, LANE), lambda i: (0, 0))],
        out_shape=[jax.ShapeDtypeStruct((t, GDN_QKV_W), F32), jax.ShapeDtypeStruct((t, LANE), BF16),
                   jax.ShapeDtypeStruct((8, LANE), F32)],
        compiler_params=_arb(1),
    )(proj, proj, proj, conv_w, gate_par, dq, dk, dv, dgc_heads)


def gdn_conv_bwd(proj, conv_w, dy):
    t = proj.shape[0]
    tm = _tile(t, 256)
    nblk = t // tm

    def body(x_ref, halo_ref, w_ref, dy_ref, dyn_ref, dx_ref, dw_ref):
        i = pl.program_id(0)

        @pl.when(i == 0)
        def _():
            dw_ref[...] = jnp.zeros_like(dw_ref)

        w = w_ref[...]
        dy = dy_ref[...]
        ext_dy = jnp.concatenate([dy, jnp.where(i < nblk - 1, dyn_ref[...], 0.0)], axis=0)
        ext_x = jnp.concatenate([jnp.where(i > 0, halo_ref[...], 0.0), x_ref[...]], axis=0)
        dx = w[GDN_CONV - 1:GDN_CONV] * dy
        rows = [jnp.sum(dy * x_ref[...], axis=0, keepdims=True)]
        for j in range(GDN_CONV - 1):
            sh = GDN_CONV - 1 - j
            dx = dx + w[j:j + 1] * _shift_rows(ext_dy, tm + 8 - sh, 0, tm)
            rows.insert(j, jnp.sum(dy * _shift_rows(ext_x, sh, 8, tm), axis=0, keepdims=True))
        dx_ref[...] = dx.astype(BF16)
        row = _iota2((8, GDN_QKV_W), 0)
        acc = jnp.zeros((8, GDN_QKV_W), F32)
        for j in range(GDN_CONV):
            acc = acc + jnp.where(row == j, rows[j], 0.0)
        dw_ref[...] += acc

    return pl.pallas_call(
        body, name="gdn_conv_bwd",
        grid=(nblk,),
        in_specs=[pl.BlockSpec((tm, GDN_QKV_W), lambda i: (i, 0)),
                  pl.BlockSpec((8, GDN_QKV_W), lambda i: (jnp.maximum(i * (tm // 8) - 1, 0), 0)),
                  pl.BlockSpec((8, GDN_QKV_W), lambda i: (0, 0)),
                  pl.BlockSpec((tm, GDN_QKV_W), lambda i: (i, 0)),
                  pl.BlockSpec((8, GDN_QKV_W), lambda i: (jnp.minimum((i + 1) * (tm // 8), t // 8 - 1), 0))],
        out_specs=[pl.BlockSpec((tm, GDN_QKV_W), lambda i: (i, 0)),
                   pl.BlockSpec((8, GDN_QKV_W), lambda i: (0, 0))],
        out_shape=[jax.ShapeDtypeStruct((t, GDN_QKV_W), BF16), jax.ShapeDtypeStruct((8, GDN_QKV_W), F32)],
        compiler_params=_arb(1),
    )(proj, proj, conv_w, dy, dy)


def _resident(w_hbm, w_vmem, sem):
    @pl.when(pl.program_id(0) == 0)
    def _():
        cp = pltpu.make_async_copy(w_hbm, w_vmem, sem)
        cp.start()
        cp.wait()


ANY = pl.BlockSpec(memory_space=pl.ANY)


def norm_proj(h, nw, w, bias):
    t, d = h.shape
    n = w.shape[1]
    tm = _tile(t, 256)
    nc = _tile(n, 1536) if n % 1536 == 0 else _tile(n, 1408)

    def body(h_ref, nw_ref, w_hbm, b_ref, o_ref, w_ref, sem):
        _resident(w_hbm, w_ref, sem)
        hn = _rms(h_ref[...], nw_ref[...])[0].astype(BF16)
        for c0 in range(0, n, nc):
            o_ref[:, c0:c0 + nc] = _dot(hn, w_ref[:, c0:c0 + nc]) + b_ref[:, c0:c0 + nc]

    return pl.pallas_call(
        body, name="norm_proj",
        grid=(t // tm,),
        in_specs=[pl.BlockSpec((tm, d), lambda i: (i, 0)), pl.BlockSpec((1, d), lambda i: (0, 0)), ANY,
                  pl.BlockSpec((1, n), lambda i: (0, 0))],
        out_specs=pl.BlockSpec((tm, n), lambda i: (i, 0)),
        out_shape=jax.ShapeDtypeStruct((t, n), F32),
        scratch_shapes=[pltpu.VMEM((d, n), BF16), pltpu.SemaphoreType.DMA],
        compiler_params=_arb(1),
    )(h, nw, w, bias)


def linear_residual(h, x, w, bias):
    t, d = h.shape
    k = x.shape[1]
    tm = _tile(t, 512)

    def body(h_ref, x_ref, w_hbm, b_ref, o_ref, w_ref, sem):
        _resident(w_hbm, w_ref, sem)
        o_ref[...] = h_ref[...] + _dot(x_ref[...], w_ref[...]) + b_ref[...]

    return pl.pallas_call(
        body, name="linear_residual",
        grid=(t // tm,),
        in_specs=[pl.BlockSpec((tm, d), lambda i: (i, 0)), pl.BlockSpec((tm, k), lambda i: (i, 0)), ANY,
                  pl.BlockSpec((1, d), lambda i: (0, 0))],
        out_specs=pl.BlockSpec((tm, d), lambda i: (i, 0)),
        out_shape=jax.ShapeDtypeStruct((t, d), F32),
        scratch_shapes=[pltpu.VMEM((k, d), BF16), pltpu.SemaphoreType.DMA],
        compiler_params=_arb(1),
    )(h, x, w, bias)


def matmul_nt(dy, w):
    t, d = dy.shape
    k = w.shape[0]
    tm = _tile(t, 512)

    def body(dy_ref, w_hbm, o_ref, cs_ref, w_ref, sem):
        _resident(w_hbm, w_ref, sem)

        @pl.when(pl.program_id(0) == 0)
        def _():
            cs_ref[...] = jnp.zeros_like(cs_ref)

        dy = dy_ref[...]
        cs_ref[...] += jnp.sum(dy, axis=0, keepdims=True)
        o_ref[...] = _dot_nt(dy.astype(BF16), w_ref[...])

    return pl.pallas_call(
        body, name="matmul_nt",
        grid=(t // tm,),
        in_specs=[pl.BlockSpec((tm, d), lambda i: (i, 0)), ANY],
        out_specs=[pl.BlockSpec((tm, k), lambda i: (i, 0)), pl.BlockSpec((1, d), lambda i: (0, 0))],
        out_shape=[jax.ShapeDtypeStruct((t, k), F32), jax.ShapeDtypeStruct((1, d), F32)],
        scratch_shapes=[pltpu.VMEM((k, d), BF16), pltpu.SemaphoreType.DMA],
        compiler_params=_arb(1),
    )(dy, w)


def norm_proj_bwd(h, nw, dh, dps, ws):
    t, d = h.shape
    np_ = len(dps)
    ns = [w.shape[1] for w in ws]
    tm = _tile(t, 256)

    def body(*refs):
        h_ref, nw_ref, dh_ref = refs[:3]
        dp_refs = refs[3:3 + np_]
        w_hbms = refs[3 + np_:3 + 2 * np_]
        o_ref, hn_ref, dnw_ref = refs[3 + 2 * np_:6 + 2 * np_]
        cs_refs = refs[6 + 2 * np_:6 + 3 * np_]
        w_refs = refs[6 + 3 * np_:6 + 4 * np_]
        sem = refs[6 + 4 * np_]
        for a, b in zip(w_hbms, w_refs):
            _resident(a, b, sem)

        @pl.when(pl.program_id(0) == 0)
        def _():
            dnw_ref[...] = jnp.zeros_like(dnw_ref)
            for c in cs_refs:
                c[...] = jnp.zeros_like(c)

        nw = nw_ref[...]
        hn, xhat, r = _rms(h_ref[...], nw)
        hn_ref[...] = hn.astype(BF16)
        dhn = jnp.zeros((tm, d), F32)
        for dp_ref, w_ref, cs_ref in zip(dp_refs, w_refs, cs_refs):
            dp = dp_ref[...]
            cs_ref[...] += jnp.sum(dp.astype(F32), axis=0, keepdims=True)
            dhn = dhn + _dot_nt(dp, w_ref[...])
        dx, dnw = _rms_bwd(dhn, nw, xhat, r)
        dnw_ref[...] += dnw
        o_ref[...] = dh_ref[...] + dx

    row = pl.BlockSpec((tm, d), lambda i: (i, 0))
    vec = pl.BlockSpec((1, d), lambda i: (0, 0))
    return pl.pallas_call(
        body, name="norm_proj_bwd",
        grid=(t // tm,),
        in_specs=[row, vec, row] + [pl.BlockSpec((tm, n), lambda i: (i, 0)) for n in ns] + [ANY] * np_,
        out_specs=[row, row, vec] + [pl.BlockSpec((1, n), lambda i: (0, 0)) for n in ns],
        out_shape=[jax.ShapeDtypeStruct((t, d), F32), jax.ShapeDtypeStruct((t, d), BF16),
                   jax.ShapeDtypeStruct((1, d), F32)] + [jax.ShapeDtypeStruct((1, n), F32) for n in ns],
        scratch_shapes=[pltpu.VMEM((d, n), BF16) for n in ns] + [pltpu.SemaphoreType.DMA],
        compiler_params=_arb(1),
    )(h, nw, dh, *dps, *ws)


def _pair_copies(g_refs, got_refs, send_sems, recv_sems):
    mx, my, mc = _me()
    copies = []
    for g_ref, got_ref in zip(g_refs, got_refs):
        for j in range(4):
            i = len(copies)
            copies.append(pltpu.make_async_remote_copy(
                src_ref=g_ref.at[j, 1 - mc], dst_ref=got_ref.at[j], send_sem=send_sems.at[i], recv_sem=recv_sems.at[i],
                device_id=(mx, my, 1 - mc), device_id_type=MESH))
    return copies


def _scatter_copies(p_refs, q_refs, send_sems, recv_sems):
    mx, my, mc = _me()
    copies = []
    for p_ref, q_ref in zip(p_refs, q_refs):
        for k in (1, 2, 3):
            px, py = _flip(mx, k >> 1), _flip(my, k & 1)
            i = len(copies)
            copies.append(pltpu.make_async_remote_copy(
                src_ref=p_ref.at[2 * px + py], dst_ref=q_ref.at[2 * mx + my], send_sem=send_sems.at[i],
                recv_sem=recv_sems.at[i], device_id=(px, py, mc), device_id_type=MESH))
    return copies


def _share_copies(o_refs, out_refs, send_sems, recv_sems):
    mx, my, mc = _me()
    return [pltpu.make_async_remote_copy(src_ref=o_ref, dst_ref=out_ref, send_sem=send_sems.at[i], recv_sem=recv_sems.at[i],
                                         device_id=(mx, my, 1 - mc), device_id_type=MESH)
            for i, (o_ref, out_ref) in enumerate(zip(o_refs, out_refs))]


CARRIED = {"pair": (_pair_copies, 4, lambda a: a.shape[:1] + a.shape[2:]), "scatter": (_scatter_copies, 3, lambda a: a.shape),
           "share": (_share_copies, 1, lambda a: a.shape)}


def matmul_tn(x, y, scale=1.0, carry=None):
    t, k = x.shape
    n = y.shape[1]
    tk = _tile(k, 1024) if k % 1024 == 0 else _tile(k, 1408)
    tn = n if n <= 1536 else (1024 if n % 1024 == 0 else 1408)
    assert n % tn == 0
    tt = _tile(t, 1024)
    grid = (k // tk, n // tn, t // tt)
    arrays = [] if carry is None else list(carry[1])
    na = len(arrays)

    def body(x_ref, y_ref, *rest):
        o_ref = rest[na]
        if carry is not None:
            make, _, _ = CARRIED[carry[0]]
            copies = lambda: make(rest[:na], rest[na + 1:2 * na + 1], rest[2 * na + 1], rest[2 * na + 2])
            pid = [pl.program_id(a) for a in range(3)]

            @pl.when((pid[0] == 0) & (pid[1] == 0) & (pid[2] == 0))
            def _():
                for cp in copies():
                    cp.start()

        @pl.when(pl.program_id(2) == 0)
        def _():
            o_ref[...] = jnp.zeros_like(o_ref)

        yv = y_ref[...]
        if scale != 1.0:
            yv = yv * scale
        o_ref[...] += _dot_tn(x_ref[...].astype(BF16), yv.astype(BF16))

        if carry is not None:
            @pl.when((pid[0] == grid[0] - 1) & (pid[1] == grid[1] - 1) & (pid[2] == grid[2] - 1))
            def _():
                for cp in copies():
                    cp.wait()

    out_shape = [jax.ShapeDtypeStruct((k, n), F32)]
    scratch = []
    if carry is not None:
        _, per_array, out_of = CARRIED[carry[0]]
        out_shape += [jax.ShapeDtypeStruct(out_of(a), a.dtype) for a in arrays]
        scratch = [pltpu.SemaphoreType.DMA((per_array * na,))] * 2
    res = pl.pallas_call(
        body, name="matmul_tn" if carry is None else "matmul_tn_" + carry[0],
        grid=grid,
        in_specs=[pl.BlockSpec((tt, tk), lambda i, j, s: (s, i)), pl.BlockSpec((tt, tn), lambda i, j, s: (s, j))] + [ANY] * na,
        out_specs=[pl.BlockSpec((tk, tn), lambda i, j, s: (i, j))] + [ANY] * na,
        out_shape=out_shape, scratch_shapes=scratch,
        compiler_params=_arb(3),
    )(x, y, *arrays)
    return res[0] if carry is None else (res[0], list(res[1:]))


FFN_CHUNKS = 2


class _FfnGather:
    def __init__(self, gu_sh, d_sh, gu_full, d_full, send_sems, recv_sems, local_sems):
        self.sh, self.full = (gu_sh, d_sh), (gu_full, d_full)
        self.send_sems, self.recv_sems, self.local_sems = send_sems, recv_sems, local_sems
        self.mx, self.my, self.mc = _me()
        self.me = 2 * self.mx + self.my
        self.w4, self.f4 = gu_sh.shape[1], d_sh.shape[0]
        self.hg, self.hd = gu_sh.shape[0] // 2, d_sh.shape[0] // 2

    def _window(self, a, chip, core=None):
        gu_full, d_full = self.full
        if a == 0:
            rows = pl.ds(0, 2 * self.hg) if core is None else pl.ds(pl.multiple_of(core * self.hg, 16), self.hg)
            return gu_full.at[rows, pl.ds(pl.multiple_of(chip * self.w4, LANE), self.w4)]
        if core is None:
            return d_full.at[pl.ds(pl.multiple_of(chip * self.f4, 16), self.f4), :]
        return d_full.at[pl.ds(pl.multiple_of(chip * self.f4 + core * self.hd, 16), self.hd), :]

    def _half(self, a):
        n = (self.hg, self.hd)[a]
        return self.sh[a].at[pl.ds(pl.multiple_of(self.mc * n, 16), n), :]

    def _peer(self, k):
        return _flip(self.mx, k >> 1), _flip(self.my, k & 1)

    def _fetch(self, k, a):
        px, py = self._peer(k)
        i = 2 * (k - 1) + a
        return pltpu.make_async_remote_copy(src_ref=self._half(a), dst_ref=self._window(a, self.me, self.mc),
                                            send_sem=self.send_sems.at[i], recv_sem=self.recv_sems.at[i],
                                            device_id=(px, py, self.mc), device_id_type=MESH)

    def _relay(self, k, a):
        px, py = self._peer(k)
        got = self._window(a, 2 * px + py, self.mc)
        i = 6 + 2 * (k - 1) + a
        return pltpu.make_async_remote_copy(src_ref=got, dst_ref=got, send_sem=self.send_sems.at[i],
                                            recv_sem=self.recv_sems.at[i], device_id=(self.mx, self.my, 1 - self.mc),
                                            device_id_type=MESH)

    def _own(self, a):
        return pltpu.make_async_copy(self.sh[a], self._window(a, self.me), self.local_sems.at[a])

    def start(self):
        for a in (0, 1):
            self._own(a).start()
            for k in (1, 2, 3):
                self._fetch(k, a).start()

    def relay(self):
        for k in (1, 2, 3):
            for a in (0, 1):
                self._fetch(k, a).wait_recv()
                self._relay(k, a).start()

    def finish(self):
        for k in (1, 2, 3):
            for a in (0, 1):
                self._fetch(k, a).wait_send()
                self._relay(k, a).wait()
        for a in (0, 1):
            self._own(a).wait()


def ffn_fwd(h, nw, wgu, wd, nxt=None):
    t, d = h.shape
    f = wd.shape[0]
    fc = f // FFN_CHUNKS
    tm = _tile(t, 512)
    nsteps = t // tm

    def body(h_ref, nw_ref, wgu_hbm, wd_hbm, *rest):
        if nxt is None:
            o_ref, gu_ref, a_ref, wgu_ref, wd_ref, sem = rest
            gather = None
        else:
            (gu_sh, d_sh, o_ref, gu_ref, a_ref, gu_full, d_full, wgu_ref, wd_ref, sem,
             send_sems, recv_sems, local_sems) = rest
            gather = _FfnGather(gu_sh, d_sh, gu_full, d_full, send_sems, recv_sems, local_sems)
            pl.when(pl.program_id(0) == 0)(gather.start)
        _resident(wgu_hbm, wgu_ref, sem)
        _resident(wd_hbm, wd_ref, sem)
        if gather is not None:
            pl.when(pl.program_id(0) == nsteps // 2)(gather.relay)
        x = h_ref[...]
        hn = _rms(x, nw_ref[...])[0].astype(BF16)
        acc = jnp.zeros((tm, d), F32)
        for c in range(FFN_CHUNKS):
            gs, us = slice(c * fc, (c + 1) * fc), slice(f + c * fc, f + (c + 1) * fc)
            g = _dot(hn, wgu_ref[:, gs])
            u = _dot(hn, wgu_ref[:, us])
            a = (_silu(g) * u).astype(BF16)
            gu_ref[:, gs] = g.astype(BF16)
            gu_ref[:, us] = u.astype(BF16)
            a_ref[:, gs] = a
            acc = acc + _dot(a, wd_ref[gs, :])
        o_ref[...] = x + 0.5 * acc
        if gather is not None:
            pl.when(pl.program_id(0) == nsteps - 1)(gather.finish)

    row = lambda w: pl.BlockSpec((tm, w), lambda i: (i, 0))
    in_specs = [row(d), pl.BlockSpec((1, d), lambda i: (0, 0)), ANY, ANY]
    out_specs = [row(d), row(2 * f), row(f)]
    out_shape = [jax.ShapeDtypeStruct((t, d), F32), jax.ShapeDtypeStruct((t, 2 * f), BF16),
                 jax.ShapeDtypeStruct((t, f), BF16)]
    scratch = [pltpu.VMEM((d, 2 * f), BF16), pltpu.VMEM((f, d), BF16), pltpu.SemaphoreType.DMA]
    args = (h, nw, wgu, wd)
    if nxt is not None:
        in_specs += [ANY, ANY]
        out_specs += [ANY, ANY]
        out_shape += [jax.ShapeDtypeStruct((d, 2 * f), BF16), jax.ShapeDtypeStruct((f, d), BF16)]
        scratch += [pltpu.SemaphoreType.DMA((12,)), pltpu.SemaphoreType.DMA((12,)), pltpu.SemaphoreType.DMA((2,))]
        args += tuple(nxt)
    return pl.pallas_call(
        body, name="ffn_fwd_gather" if nxt is not None else "ffn_fwd",
        grid=(nsteps,),
        in_specs=in_specs, out_specs=out_specs, out_shape=out_shape, scratch_shapes=scratch,
        compiler_params=_arb(1),
    )(*args)


def ffn_bwd(h, nw, wgu, wd, gu, dh):
    t, d = h.shape
    f = wd.shape[0]
    fc = f // FFN_CHUNKS
    tm = _tile(t, 256)

    def body(h_ref, nw_ref, wgu_hbm, wd_hbm, gu_ref, dh_ref, o_ref, hn_ref, dgu_ref, dnw_ref, wgu_ref, wd_ref, sem):
        _resident(wgu_hbm, wgu_ref, sem)
        _resident(wd_hbm, wd_ref, sem)

        @pl.when(pl.program_id(0) == 0)
        def _():
            dnw_ref[...] = jnp.zeros_like(dnw_ref)

        nw = nw_ref[...]
        hn32, xhat, r = _rms(h_ref[...], nw)
        hn_ref[...] = hn32.astype(BF16)
        dh = dh_ref[...]
        dout = (0.5 * dh).astype(BF16)
        dhn = jnp.zeros((tm, d), F32)
        for c in range(FFN_CHUNKS):
            gs, us = slice(c * fc, (c + 1) * fc), slice(f + c * fc, f + (c + 1) * fc)
            g = gu_ref[:, gs].astype(F32)
            u = gu_ref[:, us].astype(F32)
            sg = _sigmoid(g)
            sil = g * sg
            da = _dot_nt(dout, wd_ref[gs, :])
            dg = (da * u * (sg * (1.0 + g * (1.0 - sg)))).astype(BF16)
            du = (da * sil).astype(BF16)
            dgu_ref[:, gs] = dg
            dgu_ref[:, us] = du
            dhn = dhn + _dot_nt(dg, wgu_ref[:, gs]) + _dot_nt(du, wgu_ref[:, us])
        dx, dnw = _rms_bwd(dhn, nw, xhat, r)
        dnw_ref[...] += dnw
        o_ref[...] = dh + dx

    row = pl.BlockSpec((tm, d), lambda i: (i, 0))
    vec = pl.BlockSpec((1, d), lambda i: (0, 0))
    return pl.pallas_call(
        body, name="ffn_bwd",
        grid=(t // tm,),
        in_specs=[row, vec, ANY, ANY, pl.BlockSpec((tm, 2 * f), lambda i: (i, 0)), row],
        out_specs=[row, row, pl.BlockSpec((tm, 2 * f), lambda i: (i, 0)), vec],
        out_shape=[jax.ShapeDtypeStruct((t, d), F32), jax.ShapeDtypeStruct((t, d), BF16),
                   jax.ShapeDtypeStruct((t, 2 * f), BF16), jax.ShapeDtypeStruct((1, d), F32)],
        scratch_shapes=[pltpu.VMEM((d, 2 * f), BF16), pltpu.VMEM((f, d), BF16), pltpu.SemaphoreType.DMA],
        compiler_params=_arb(1),
    )(h, nw, wgu, wd, gu, dh)


def loss_head(h, nw, target):
    t, d = h.shape
    tm = _tile(t, 512)

    def body(h_ref, nw_ref, tg_ref, dh_ref, loss_ref, dnw_ref):
        @pl.when(pl.program_id(0) == 0)
        def _():
            loss_ref[...] = jnp.zeros_like(loss_ref)
            dnw_ref[...] = jnp.zeros_like(dnw_ref)

        nw = nw_ref[...]
        y, xhat, r = _rms(h_ref[...], nw)
        e = y - tg_ref[...]
        loss_ref[...] += 0.5 * jnp.sum(jnp.mean(e * e, axis=-1, keepdims=True), axis=0, keepdims=True)
        dx, dnw = _rms_bwd(e * (1.0 / d), nw, xhat, r)
        dnw_ref[...] += dnw
        dh_ref[...] = dx

    row = pl.BlockSpec((tm, d), lambda i: (i, 0))
    vec = pl.BlockSpec((1, d), lambda i: (0, 0))
    return pl.pallas_call(
        body, name="loss_head",
        grid=(t // tm,),
        in_specs=[row, vec, row],
        out_specs=[row, pl.BlockSpec((8, LANE), lambda i: (0, 0)), vec],
        out_shape=[jax.ShapeDtypeStruct((t, d), F32), jax.ShapeDtypeStruct((8, LANE), F32),
                   jax.ShapeDtypeStruct((1, d), F32)],
        compiler_params=_arb(1),
    )(h, nw, target)


def adamw(w, g, m, v):
    r, c = w.shape
    tr = r
    while tr * c * 4 > (1 << 20) and tr % 16 == 0:
        tr //= 2

    def body(w_ref, g_ref, m_ref, v_ref, d_ref, nm_ref, nv_ref):
        g = g_ref[...]
        m = ADAM_B1 * m_ref[...] + (1.0 - ADAM_B1) * g
        v = ADAM_B2 * v_ref[...] + (1.0 - ADAM_B2) * (g * g)
        m_hat = m / (1.0 - ADAM_B1 ** ADAM_STEP)
        v_hat = v / (1.0 - ADAM_B2 ** ADAM_STEP)
        d_ref[...] = -ADAM_LR * (m_hat / (jnp.sqrt(v_hat) + ADAM_EPS) + ADAM_WD * w_ref[...])
        nm_ref[...] = m
        nv_ref[...] = v

    blk = pl.BlockSpec((tr, c), lambda i: (i, 0))
    return pl.pallas_call(
        body, name="adamw",
        grid=(r // tr,),
        in_specs=[blk] * 4, out_specs=[blk] * 3,
        out_shape=[jax.ShapeDtypeStruct((r, c), F32)] * 3,
        compiler_params=_arb(1),
    )(w, g, m, v)


ATTN_P_W = ATTN_Q_W + 2 * ATTN_KV_W
ATTN_SCALE = ATTN_HEAD_DIM ** -0.5


def _rope_group(t, tab, sign):
    return (t * tab[:, 0:LANE] + sign * pltpu.roll(t, 8, 1) * tab[:, LANE:2 * LANE]
            + sign * pltpu.roll(t, LANE - 8, 1) * tab[:, 2 * LANE:3 * LANE])


def _rope(t, tab, sign=1.0):
    return jnp.concatenate([_rope_group(t[:, s:s + LANE], tab, sign) for s in range(0, t.shape[1], LANE)], axis=1)


def _attn_heads(qs, ks, vs, sinks, first):
    b = ATTN_BLOCK
    rows = ATTN_GROUP * b
    qi = _iota2((rows, 2 * b), 0) % b
    kj = _iota2((rows, 2 * b), 1)
    rel = qi + b - kj
    valid = (rel >= 0) & (rel < b) & ((kj >= b) | jnp.logical_not(first))
    ss = _each(lambda q, k: jnp.where(valid, _mm1_nt(q, k) * ATTN_SCALE, NEG), qs, ks)
    ms = _each(lambda s, sink: lax.stop_gradient(jnp.maximum(jnp.max(s, axis=-1, keepdims=True), sink)), ss, sinks)
    ps = _each(lambda s, m: jnp.exp(s - m), ss, ms)
    dens = _each(lambda p, sink, m: jnp.sum(p, axis=-1, keepdims=True) + jnp.exp(sink - m), ps, sinks, ms)
    return _each(lambda p, den, v: _mm1(p / den, v), ps, dens, vs)


def _attn_prepare(p_ref, kvp_ref, tab_ref, tabp_ref, sink_ref):
    b = ATTN_BLOCK
    hd = ATTN_HEAD_DIM
    tab, tabp = tab_ref[...], tabp_ref[...]
    q = _rope(p_ref[:, 0:ATTN_Q_W], tab)
    kc = _rope(p_ref[:, ATTN_Q_W:ATTN_Q_W + ATTN_KV_W], tab)
    kp = _rope(kvp_ref[:, 0:ATTN_KV_W], tabp)
    vc = p_ref[:, ATTN_Q_W + ATTN_KV_W:ATTN_P_W]
    vp = kvp_ref[:, ATTN_KV_W:2 * ATTN_KV_W]
    sk = sink_ref[...]
    qs, ks, vs, sinks = [], [], [], []
    for h in range(ATTN_KV_HEADS):
        heads = [ATTN_GROUP * h + g for g in range(ATTN_GROUP)]
        qs.append(jnp.concatenate([q[:, i * hd:(i + 1) * hd] for i in heads], axis=0))
        ks.append(jnp.concatenate([kp[:, h * hd:(h + 1) * hd], kc[:, h * hd:(h + 1) * hd]], axis=0))
        vs.append(jnp.concatenate([vp[:, h * hd:(h + 1) * hd], vc[:, h * hd:(h + 1) * hd]], axis=0))
        sinks.append(jnp.concatenate([jnp.broadcast_to(sk[:, i:i + 1], (b, 1)) for i in heads], axis=0))
    return qs, ks, vs, sinks


def _unstack_heads(xs):
    b = ATTN_BLOCK
    return jnp.concatenate([x[g * b:(g + 1) * b] for x in xs for g in range(ATTN_GROUP)], axis=1)


def _attn_specs(nb):
    b = ATTN_BLOCK
    prev = lambda n: jnp.maximum(n - 1, 0)
    return [pl.BlockSpec((b, ATTN_P_W), lambda n: (n, 0)),
            pl.BlockSpec((b, 2 * ATTN_KV_W), lambda n: (prev(n), ATTN_Q_W // (2 * ATTN_KV_W))),
            pl.BlockSpec((b, 3 * LANE), lambda n: (n, 0)),
            pl.BlockSpec((b, 3 * LANE), lambda n: (prev(n), 0)),
            pl.BlockSpec((1, ATTN_Q_HEADS), lambda n: (0, 0))]


def attn_fwd(p, tab, sinks):
    t = p.shape[0]
    nb = t // ATTN_BLOCK

    def body(p_ref, kvp_ref, tab_ref, tabp_ref, sink_ref, o_ref):
        qs, ks, vs, sk = _attn_prepare(p_ref, kvp_ref, tab_ref, tabp_ref, sink_ref)
        os_ = _attn_heads(qs, ks, vs, sk, pl.program_id(0) == 0)
        o_ref[...] = _unstack_heads(os_).astype(BF16)

    return pl.pallas_call(
        body, name="attn_fwd",
        grid=(nb,),
        in_specs=_attn_specs(nb),
        out_specs=pl.BlockSpec((ATTN_BLOCK, ATTN_Q_W), lambda n: (n, 0)),
        out_shape=jax.ShapeDtypeStruct((t, ATTN_Q_W), BF16),
        compiler_params=_arb(1),
    )(p, p, tab, tab, sinks)


def attn_bwd(p, tab, sinks, do):
    t = p.shape[0]
    b = ATTN_BLOCK
    hd = ATTN_HEAD_DIM
    nb = t // b

    def body(p_ref, kvp_ref, tab_ref, tabp_ref, sink_ref, do_ref, dq_ref, dkvc_ref, dkvp_ref, dsink_ref):
        n = pl.program_id(0)

        @pl.when(n == 0)
        def _():
            dsink_ref[...] = jnp.zeros_like(dsink_ref)

        qs, ks, vs, sk = _attn_prepare(p_ref, kvp_ref, tab_ref, tabp_ref, sink_ref)
        first = n == 0
        _, vjp = jax.vjp(lambda a, bb, c, d: _attn_heads(a, bb, c, d, first), qs, ks, vs, sk)
        do = do_ref[...]
        dos = [jnp.concatenate([do[:, i * hd:(i + 1) * hd] for i in range(ATTN_GROUP * h, ATTN_GROUP * (h + 1))], axis=0)
               for h in range(ATTN_KV_HEADS)]
        dqs, dks, dvs, dsk = vjp(dos)
        tab, tabp = tab_ref[...], tabp_ref[...]
        dq_ref[...] = _rope(_unstack_heads([x.astype(F32) for x in dqs]), tab, -1.0).astype(BF16)
        dkc = jnp.concatenate([x.astype(F32)[b:] for x in dks], axis=1)
        dkp = jnp.concatenate([x.astype(F32)[:b] for x in dks], axis=1)
        dkvc_ref[:, 0:ATTN_KV_W] = _rope(dkc, tab, -1.0)
        dkvp_ref[:, 0:ATTN_KV_W] = _rope(dkp, tabp, -1.0)
        dkvc_ref[:, ATTN_KV_W:] = jnp.concatenate([x.astype(F32)[b:] for x in dvs], axis=1)
        dkvp_ref[:, ATTN_KV_W:] = jnp.concatenate([x.astype(F32)[:b] for x in dvs], axis=1)
        parts = [jnp.sum(d[g * b:(g + 1) * b], axis=0, keepdims=True) for d in dsk for g in range(ATTN_GROUP)]
        dsink_ref[...] += jnp.concatenate(parts, axis=1)

    blk = lambda w: pl.BlockSpec((b, w), lambda n: (n, 0))
    return pl.pallas_call(
        body, name="attn_bwd",
        grid=(nb,),
        in_specs=_attn_specs(nb) + [blk(ATTN_Q_W)],
        out_specs=[blk(ATTN_Q_W), blk(2 * ATTN_KV_W), blk(2 * ATTN_KV_W),
                   pl.BlockSpec((1, ATTN_Q_HEADS), lambda n: (0, 0))],
        out_shape=[jax.ShapeDtypeStruct((t, ATTN_Q_W), BF16), jax.ShapeDtypeStruct((t, 2 * ATTN_KV_W), F32),
                   jax.ShapeDtypeStruct((t, 2 * ATTN_KV_W), F32), jax.ShapeDtypeStruct((1, ATTN_Q_HEADS), F32)],
        compiler_params=_arb(1),
    )(p, p, tab, tab, sinks, do)


def kv_combine(dkvc, dkvp):
    t, w = dkvc.shape
    b = ATTN_BLOCK
    nb = t // b

    def body(c_ref, p_ref, o_ref):
        nxt = jnp.where(pl.program_id(0) < nb - 1, p_ref[...], 0.0)
        o_ref[...] = (c_ref[...] + nxt).astype(BF16)

    return pl.pallas_call(
        body, name="kv_combine",
        grid=(nb,),
        in_specs=[pl.BlockSpec((b, w), lambda n: (n, 0)),
                  pl.BlockSpec((b, w), lambda n: (jnp.minimum(n + 1, nb - 1), 0))],
        out_specs=pl.BlockSpec((b, w), lambda n: (n, 0)),
        out_shape=jax.ShapeDtypeStruct((t, w), BF16),
        compiler_params=_arb(1),
    )(dkvc, dkvp)


def _me():
    return lax.axis_index("x"), lax.axis_index("y"), lax.axis_index("c")


def _flip(v, bit):
    return 1 - v if bit else v


def _chip_index():
    return 2 * lax.axis_index("x") + lax.axis_index("y")


def allgather_chips(x, split):
    r, c = x.shape
    rh = r // 2 if split else r

    def body(x_ref, o_ref, send_sems, recv_sems):
        mx, my, mc = _me()
        me = 2 * mx + my
        rows = pl.ds(pl.multiple_of(mc * rh, 16), rh) if split else pl.ds(0, r)
        fetched, passed = [], []
        for k in (1, 2, 3):
            peer = (_flip(mx, k >> 1), _flip(my, k & 1), mc)
            cp = pltpu.make_async_remote_copy(src_ref=x_ref.at[rows], dst_ref=o_ref.at[me, rows],
                                              send_sem=send_sems.at[k - 1], recv_sem=recv_sems.at[k - 1],
                                              device_id=peer, device_id_type=MESH)
            cp.start()
            fetched.append(cp)
        if split:
            for k in (1, 2, 3):
                fetched[k - 1].wait_recv()
                theirs = o_ref.at[2 * _flip(mx, k >> 1) + _flip(my, k & 1), rows]
                cp = pltpu.make_async_remote_copy(src_ref=theirs, dst_ref=theirs, send_sem=send_sems.at[2 + k],
                                                  recv_sem=recv_sems.at[2 + k], device_id=(mx, my, 1 - mc),
                                                  device_id_type=MESH)
                cp.start()
                passed.append(cp)
            for cp in fetched:
                cp.wait_send()
            for cp in passed:
                cp.wait()
        else:
            for cp in fetched:
                cp.wait()

    got = pl.pallas_call(
        body, name="allgather_chips",
        in_specs=[ANY], out_specs=ANY,
        out_shape=jax.ShapeDtypeStruct((4, r, c), x.dtype),
        scratch_shapes=[pltpu.SemaphoreType.DMA((6,)), pltpu.SemaphoreType.DMA((6,))],
    )(x)
    return lax.dynamic_update_slice(got, x[None], (_chip_index(), 0, 0))


def exchange(kind, arrays):
    make, per_array, out_of = CARRIED[kind]
    na = len(arrays)

    def body(*refs):
        copies = make(refs[:na], refs[na:2 * na], refs[2 * na], refs[2 * na + 1])
        for cp in copies:
            cp.start()
        for cp in copies:
            cp.wait()

    return pl.pallas_call(
        body, name=kind + "_exchange",
        in_specs=[ANY] * na, out_specs=[ANY] * na,
        out_shape=[jax.ShapeDtypeStruct(out_of(a), a.dtype) for a in arrays],
        scratch_shapes=[pltpu.SemaphoreType.DMA((per_array * na,))] * 2,
    )(*arrays)


def _row_tile(r, c):
    tr = 8
    while r % (2 * tr) == 0 and 2 * tr * c * 4 <= (2 << 20):
        tr *= 2
    return tr


def add_pair(g, got, core):
    _, _, r, c = g.shape
    tr = _row_tile(r, c)

    def body(core_ref, g_ref, got_ref, p32_ref, p16_ref):
        s = g_ref[...] + got_ref[...]
        p32_ref[...] = s
        p16_ref[...] = s.astype(BF16)

    blk = pl.BlockSpec((None, tr, c), lambda j, i, core_ref: (j, i, 0))
    return pl.pallas_call(
        body, name="add_pair",
        grid_spec=pltpu.PrefetchScalarGridSpec(
            num_scalar_prefetch=1, grid=(4, r // tr),
            in_specs=[pl.BlockSpec((None, None, tr, c), lambda j, i, core_ref: (j, core_ref[0], i, 0)), blk],
            out_specs=[blk, blk]),
        out_shape=[jax.ShapeDtypeStruct((4, r, c), F32), jax.ShapeDtypeStruct((4, r, c), BF16)],
        compiler_params=_arb(2),
    )(core, g, got)


def sum_slots(p32, q16, order):
    _, r, c = p32.shape
    tr = _row_tile(r, c)

    def body(order_ref, own_ref, a_ref, b_ref, c_ref, o_ref):
        o_ref[...] = ((own_ref[...] + a_ref[...].astype(F32)) + b_ref[...].astype(F32)) + c_ref[...].astype(F32)

    slot = lambda k: pl.BlockSpec((None, tr, c), functools.partial(lambda k, i, order_ref: (order_ref[k], i, 0), k))
    return pl.pallas_call(
        body, name="sum_slots",
        grid_spec=pltpu.PrefetchScalarGridSpec(
            num_scalar_prefetch=1, grid=(r // tr,),
            in_specs=[slot(0), slot(1), slot(2), slot(3)],
            out_specs=pl.BlockSpec((tr, c), lambda i, order_ref: (i, 0))),
        out_shape=jax.ShapeDtypeStruct((r, c), F32),
        compiler_params=_arb(1),
    )(order, p32, q16, q16, q16)


def allreduce_small(x):
    r, c = x.shape

    def body(x_ref, o_ref, buf, send_sems, recv_sems):
        mx, my, mc = _me()
        me = 4 * mx + 2 * my + mc
        buf[pl.ds(me, 1)] = x_ref[...][None]
        copies = []
        for k in range(1, 8):
            peer = (_flip(mx, k >> 2), _flip(my, (k >> 1) & 1), _flip(mc, k & 1))
            cp = pltpu.make_async_remote_copy(src_ref=x_ref, dst_ref=buf.at[me], send_sem=send_sems.at[k - 1],
                                              recv_sem=recv_sems.at[k - 1], device_id=peer, device_id_type=MESH)
            cp.start()
            copies.append(cp)
        for cp in copies:
            cp.wait()
        acc = buf[0]
        for d in range(1, 8):
            acc = acc + buf[d]
        o_ref[...] = acc

    return pl.pallas_call(
        body, name="allreduce_small",
        out_shape=jax.ShapeDtypeStruct((r, c), F32),
        scratch_shapes=[pltpu.VMEM((8, r, c), F32), pltpu.SemaphoreType.DMA((7,)), pltpu.SemaphoreType.DMA((7,))],
    )(x)


class GradReduce:
    def __init__(self, grads, shard_shapes):
        self.shapes = shard_shapes
        self.groups = _by_width(shard_shapes)
        self.core = lax.axis_index("c")
        me = _chip_index()
        self.order = jnp.stack([me, me ^ 1, me ^ 2, me ^ 3]).astype(jnp.int32)
        self.g = []
        for names in self.groups.values():
            slots = []
            for j in range(4):
                parts = []
                for n in names:
                    w = shard_shapes[n][1 + SHARDED[n]]
                    parts += [lax.slice_in_dim(gl, j * w, (j + 1) * w, axis=SHARDED[n]) for gl in grads[n]]
                slots.append(jnp.concatenate(parts, axis=0))
            g = jnp.stack(slots)
            self.g.append(g.reshape(4, 2, g.shape[1] // 2, g.shape[2]))
        self.p32 = self.p16 = None
        self.q16 = [None] * len(self.g)
        n = len(self.g)
        self.pieces = [list(range(2, n)), [1], [0]] if n > 2 else [[i] for i in reversed(range(n))]
        self.pieces = [p for p in self.pieces if p]
        self.stage = 0

    def next_job(self):
        if self.stage == 0:
            return "pair", self.g
        if self.stage <= len(self.pieces):
            return "scatter", [self.p16[i] for i in self.pieces[self.stage - 1]]
        return None

    def deliver(self, res):
        if self.stage == 0:
            core = self.core.astype(jnp.int32)[None]
            pairs = [add_pair(g, got, core) for g, got in zip(self.g, res)]
            self.p32, self.p16 = [p[0] for p in pairs], [p[1] for p in pairs]
        else:
            for i, q in zip(self.pieces[self.stage - 1], res):
                self.q16[i] = q
        self.stage += 1

    def finish(self):
        while (job := self.next_job()) is not None:
            self.deliver(exchange(*job))
        mine = [sum_slots(p, q, self.order) for p, q in zip(self.p32, self.q16)]
        theirs = exchange("share", mine)
        out = {}
        for names, m, t in zip(self.groups.values(), mine, theirs):
            red = jnp.where(self.core == 0, jnp.concatenate([m, t]), jnp.concatenate([t, m]))
            off = 0
            for n in names:
                layers, a, _ = self.shapes[n]
                out[n] = red[off:off + layers * a].reshape(self.shapes[n])
                off += layers * a
        return out


PACK_W = 1024
PACK_ROWS = 1024

SHARDED = {"ffn1_w_gate_up": 1, "ffn1_w_down": 0, "ffn2_w_gate_up": 1, "ffn2_w_down": 0, "attn_w_in": 1,
           "attn_w_out": 0, "gdn_w_in": 1, "gdn_w_out": 0, "gdn_conv_w": 1}
REPLICATED = ["ffn1_norm", "mix_norm", "ffn2_norm", "attn_b_in", "attn_sinks", "attn_b_out", "gdn_A_log",
              "gdn_dt_bias", "gdn_norm_w", "final_norm"]
WEIGHTS = ["ffn1_norm", "ffn1_w_gate_up", "ffn1_w_down", "mix_norm", "ffn2_norm", "ffn2_w_gate_up", "ffn2_w_down",
           "attn_w_in", "attn_b_in", "attn_sinks", "attn_w_out", "attn_b_out", "gdn_w_in", "gdn_conv_w", "gdn_A_log",
           "gdn_dt_bias", "gdn_norm_w", "gdn_w_out", "final_norm"]


def _pack_rows(flats, dtype, width, row_multiple):
    flat = jnp.concatenate([f.astype(dtype).reshape(-1) for f in flats])
    per = width * row_multiple
    pad = (-flat.shape[0]) % per
    if pad:
        flat = jnp.concatenate([flat, jnp.zeros((pad,), dtype)])
    return flat.reshape(-1, width)


def _by_width(shapes):
    groups = {}
    for n, shape in shapes.items():
        groups.setdefault(shape[-1], []).append(n)
    return groups


def _gather_weights(shards, dtype, split=True):
    full = {}
    for width, names in _by_width({n: s.shape for n, s in shards.items()}).items():
        packed = jnp.concatenate([shards[n].astype(dtype).reshape(-1, width) for n in names], axis=0)
        got = allgather_chips(packed, split)
        off = 0
        for n in names:
            layers, a, _ = shards[n].shape
            full[n] = [jnp.concatenate([got[j, off + l * a:off + (l + 1) * a] for j in range(4)], axis=SHARDED[n])
                       for l in range(layers)]
            off += layers * a
    return full


def _small_pack(items):
    rows = []
    for a in items:
        f = a.astype(F32).reshape(-1)
        pad = (-f.shape[0]) % LANE
        rows.append(jnp.concatenate([f, jnp.zeros((pad,), F32)]) if pad else f)
    return _pack_rows(rows, F32, LANE, 8)


def _small_unpack(buf, shapes):
    flat = buf.reshape(-1)
    out, off = [], 0
    for shape in shapes:
        size = 1
        for s in shape:
            size *= s
        out.append(flat[off:off + size].reshape(shape))
        off += size + (-size) % LANE
    return out


def _rope_table(positions):
    t = positions.shape[0]
    inv_freq = ROPE_THETA ** (-jnp.arange(0, ROPE_DIM, 2, dtype=F32) / ROPE_DIM)
    ang = positions.astype(F32)[:, None] * inv_freq
    cos, sin = jnp.cos(ang), jnp.sin(ang)
    rest = ATTN_HEAD_DIM - ROPE_DIM
    zeros = lambda n: jnp.zeros((t, n), F32)
    c64 = jnp.concatenate([cos, cos, jnp.ones((t, rest), F32)], axis=1)
    s_up = jnp.concatenate([zeros(ROPE_DIM // 2), sin, zeros(rest)], axis=1)
    s_dn = jnp.concatenate([-sin, zeros(ROPE_DIM // 2 + rest)], axis=1)
    return jnp.concatenate([c64, c64, s_up, s_up, s_dn, s_dn], axis=1)


def _as2d(a):
    return a.reshape(-1, a.shape[-1]) if a.ndim > 1 else a.reshape(1, -1)


def kernel(x, positions, ffn1_norm, ffn1_w_gate_up, ffn1_w_down, mix_norm, ffn2_norm, ffn2_w_gate_up, ffn2_w_down, attn_w_in, attn_b_in, attn_sinks, attn_w_out, attn_b_out, gdn_w_in, gdn_conv_w, gdn_A_log, gdn_dt_bias, gdn_norm_w, gdn_w_out, final_norm, loss_target, m_ffn1_norm, m_ffn1_w_gate_up, m_ffn1_w_down, m_mix_norm, m_ffn2_norm, m_ffn2_w_gate_up, m_ffn2_w_down, m_attn_w_in, m_attn_b_in, m_attn_sinks, m_attn_w_out, m_attn_b_out, m_gdn_w_in, m_gdn_conv_w, m_gdn_A_log, m_gdn_dt_bias, m_gdn_norm_w, m_gdn_w_out, m_final_norm, v_ffn1_norm, v_ffn1_w_gate_up, v_ffn1_w_down, v_mix_norm, v_ffn2_norm, v_ffn2_w_gate_up, v_ffn2_w_down, v_attn_w_in, v_attn_b_in, v_attn_sinks, v_attn_w_out, v_attn_b_out, v_gdn_w_in, v_gdn_conv_w, v_gdn_A_log, v_gdn_dt_bias, v_gdn_norm_w, v_gdn_w_out, v_final_norm):
    given = dict(locals())
    w = {n: given[n] for n in WEIGHTS}
    d = D_MODEL
    h = x[0]
    target = loss_target[0]
    depth = ffn1_norm.shape[0]

    big = [n for n in SHARDED if n != "gdn_conv_w"]
    up_front = {n: (w[n][0:1] if n.startswith("ffn1") else w[n]) for n in big if not n.startswith("ffn2")}
    full = _gather_weights(up_front, BF16)
    ffn_order = [(tag, l) for l in range(depth) for tag in ("ffn1", "ffn2")]
    ffn_w = {ffn_order[0]: (full["ffn1_w_gate_up"][0], full["ffn1_w_down"][0])}

    def run_ffn(tag, l, h):
        i = ffn_order.index((tag, l))
        wgu, wd = ffn_w[(tag, l)]
        nw = w[tag + "_norm"][l][None]
        if i + 1 == len(ffn_order):
            return ffn_fwd(h, nw, wgu, wd)
        ntag, nl = ffn_order[i + 1]
        shards = (w[ntag + "_w_gate_up"][nl].astype(BF16), w[ntag + "_w_down"][nl].astype(BF16))
        h, gu, a, ngu, nd = ffn_fwd(h, nw, wgu, wd, shards)
        ffn_w[(ntag, nl)] = (ngu, nd)
        return h, gu, a

    conv_full = _gather_weights({"gdn_conv_w": gdn_conv_w}, F32, split=False)["gdn_conv_w"]
    tab = _rope_table(positions[0])
    zero_d = jnp.zeros((1, d), F32)

    def gdn_params(j):
        w_in = full["gdn_w_in"][j]
        w_cat = jnp.concatenate([w_in, jnp.zeros((d, GDN_PROJ_W - w_in.shape[1]), BF16)], axis=1)
        conv = jnp.concatenate([conv_full[j], jnp.zeros((8 - GDN_CONV, GDN_QKV_W), F32)], axis=0)
        lanes = lambda vec: jnp.concatenate([jnp.zeros((GC_G,), F32), vec, jnp.zeros((LANE - GC_G - GDN_HEADS,), F32)])
        par = jnp.concatenate([lanes(gdn_A_log[j])[None], lanes(gdn_dt_bias[j])[None], jnp.zeros((6, LANE), F32)], axis=0)
        return w_cat, conv, par

    saved = []
    for l in range(depth):
        j = l // 2
        rec = {"h1": h}
        h, rec["ffn1_gu"], rec["ffn1_a"] = run_ffn("ffn1", l, h)
        rec["h2"] = h
        if l % 2 == 0:
            p = norm_proj(h, mix_norm[l][None], full["attn_w_in"][j], attn_b_in[j][None])
            o = attn_fwd(p, tab, attn_sinks[j][None])
            h = linear_residual(h, o, full["attn_w_out"][j], attn_b_out[j][None])
            rec.update(p=p, o=o)
        else:
            w_cat, conv, par = gdn_params(j)
            proj = norm_proj(h, mix_norm[l][None], w_cat, jnp.zeros((1, GDN_PROJ_W), F32))
            qkv, gc = gdn_pre_fwd(proj, conv, par)
            on, st = gdn_chunk_fwd(qkv, gc, proj, gdn_norm_w[j][None])
            h = linear_residual(h, on, full["gdn_w_out"][j], zero_d)
            rec.update(proj=proj, qkv=qkv, gc=gc, on=on, st=st, w_cat=w_cat, conv=conv, par=par)
        rec["h3"] = h
        h, rec["ffn2_gu"], rec["ffn2_a"] = run_ffn("ffn2", l, h)
        saved.append(rec)

    dh, loss_tile, d_final = loss_head(h, final_norm[None], target)

    g = {n: [None] * w[n].shape[0] for n in WEIGHTS if n != "final_norm"}
    reducers = []

    def layer_of(n, i):
        return i if n.startswith("ffn") else (2 * i if n.startswith("attn") else 2 * i + 1)

    def layer_reduce(layer):
        picks = {n: idx for n in big if (idx := [i for i in range(w[n].shape[0]) if layer_of(n, i) == layer])}
        return GradReduce({n: [g[n][i] for i in idx] for n, idx in picks.items()},
                          {n: (len(idx),) + w[n].shape[1:] for n, idx in picks.items()})

    def tn(x, y, scale=1.0):
        big_enough = x.shape[1] * y.shape[1] >= D_MODEL * D_FF
        red = next((r for r in reducers if r.next_job() is not None), None) if big_enough else None
        if red is None:
            return matmul_tn(x, y, scale)
        out, res = matmul_tn(x, y, scale, carry=red.next_job())
        red.deliver(res)
        return out
    for l in reversed(range(depth)):
        j = l // 2
        rec = saved[l]

        def ffn_back(tag, h_in, dh):
            wgu, wd = ffn_w[(tag, l)]
            dh_new, hn, dgu, dn = ffn_bwd(h_in, w[tag + "_norm"][l][None], wgu, wd, rec[tag + "_gu"], dh)
            g[tag + "_w_gate_up"][l] = tn(hn, dgu)
            g[tag + "_w_down"][l] = tn(rec[tag + "_a"], dh, 0.5)
            g[tag + "_norm"][l] = dn[0]
            return dh_new

        dh = ffn_back("ffn2", rec["h3"], dh)
        if l % 2 == 0:
            w_in, w_out = full["attn_w_in"][j], full["attn_w_out"][j]
            do, db_out = matmul_nt(dh, w_out)
            g["attn_w_out"][j] = matmul_tn(rec["o"], dh)
            g["attn_b_out"][j] = db_out[0]
            dq, dkvc, dkvp, dsink = attn_bwd(rec["p"], tab, attn_sinks[j][None], do)
            dkv = kv_combine(dkvc, dkvp)
            dh, hn, dn, cs_q, cs_kv = norm_proj_bwd(rec["h2"], mix_norm[l][None], dh, [dq, dkv],
                                                    [w_in[:, :ATTN_Q_W], w_in[:, ATTN_Q_W:]])
            g["attn_w_in"][j] = jnp.concatenate([matmul_tn(hn, dq), matmul_tn(hn, dkv)], axis=1)
            g["attn_b_in"][j] = jnp.concatenate([cs_q[0], cs_kv[0]])
            g["attn_sinks"][j] = dsink[0]
        else:
            w_cat, conv, par = rec["w_cat"], rec["conv"], rec["par"]
            d_on, _ = matmul_nt(dh, full["gdn_w_out"][j])
            g["gdn_w_out"][j] = matmul_tn(rec["on"], dh)
            dq, dk, dv, dz, dgc_heads, dnw = gdn_chunk_bwd(rec["qkv"], rec["gc"], rec["proj"], gdn_norm_w[j][None],
                                                           rec["st"], d_on)
            dy, dba, dpar = gdn_pre_bwd(rec["proj"], conv, par, dq, dk, dv, dgc_heads)
            dx, dconv = gdn_conv_bwd(rec["proj"], conv, dy)
            nz = GDN_QKV_W + GDN_HEADS * GDN_DK
            dh, hn, dn, _, _, _ = norm_proj_bwd(rec["h2"], mix_norm[l][None], dh, [dx, dz, dba],
                                                [w_cat[:, :GDN_QKV_W], w_cat[:, GDN_QKV_W:nz], w_cat[:, nz:]])
            g["gdn_w_in"][j] = jnp.concatenate(
                [tn(hn, dx), matmul_tn(hn, dz), matmul_tn(hn, dba)[:, :2 * GDN_HEADS]], axis=1)
            g["gdn_conv_w"][j] = dconv[:GDN_CONV]
            g["gdn_A_log"][j] = dpar[0, GDN_HEADS:2 * GDN_HEADS]
            g["gdn_dt_bias"][j] = dpar[1, GDN_HEADS:2 * GDN_HEADS]
            g["gdn_norm_w"][j] = dnw[0]
        g["mix_norm"][l] = dn[0]
        dh = ffn_back("ffn1", rec["h1"], dh)
        if l > 0:
            reducers.append(layer_reduce(l))
    grad_x = dh[None]

    small_names = REPLICATED + ["gdn_conv_w"]
    local = {n: jnp.stack(g[n]) for n in small_names if n != "final_norm"}
    local["final_norm"] = d_final[0]
    small_shapes = [(1,)] + [local[n].shape for n in small_names]
    small = allreduce_small(_small_pack([loss_tile[0, 0:1]] + [local[n] for n in small_names]))
    small = _small_unpack(small, small_shapes)
    loss = small[0][0]
    grads = dict(zip(small_names, small[1:]))
    conv_cols = gdn_conv_w.shape[2]
    grads["gdn_conv_w"] = lax.dynamic_slice_in_dim(grads["gdn_conv_w"], _chip_index() * conv_cols, conv_cols, axis=2)
    by_layer = [layer_reduce(0).finish()] + [r.finish() for r in reversed(reducers)]
    grads.update({n: jnp.concatenate([part[n] for part in by_layer if n in part], axis=0) for n in big})

    delta, new_m, new_v = {}, {}, {}
    for n in SHARDED:
        dl, nm, nv = adamw(_as2d(w[n]), _as2d(grads[n]), _as2d(given["m_" + n]), _as2d(given["v_" + n]))
        delta[n], new_m[n], new_v[n] = dl.reshape(w[n].shape), nm.reshape(w[n].shape), nv.reshape(w[n].shape)
    shapes = [w[n].shape for n in REPLICATED]
    packed = [_small_pack([src[n] for n in REPLICATED]) for src in
              (w, grads, {n: given["m_" + n] for n in REPLICATED}, {n: given["v_" + n] for n in REPLICATED})]
    for dst, buf in zip((delta, new_m, new_v), adamw(*packed)):
        dst.update(zip(REPLICATED, _small_unpack(buf, shapes)))

    return (loss, grad_x, *[grads[n] for n in WEIGHTS], *[delta[n] for n in WEIGHTS],
            *[new_m[n] for n in WEIGHTS], *[new_v[n] for n in WEIGHTS])
```

```python
import functools

import jax
import jax.numpy as jnp
from jax import lax
from jax.experimental import pallas as pl
from jax.experimental.pallas import tpu as pltpu

F32 = jnp.float32
BF16 = jnp.bfloat16
HI = lax.Precision.HIGHEST
MESH = pl.DeviceIdType.MESH

D_MODEL = 1024
D_FF = 2816
DEPTH = 4
NORM_EPS = 1e-6
LANE = 128

ATTN_Q_HEADS = 16
ATTN_KV_HEADS = 4
ATTN_HEAD_DIM = 64
ATTN_GROUP = 4
ATTN_BLOCK = 128
ROPE_DIM = 16
ROPE_THETA = 500000.0
ATTN_Q_W = 1024
ATTN_KV_W = 256

GDN_HEADS = 8
GDN_DK = 128
GDN_CONV = 4
GDN_CHUNK = 64
GDN_QKV_W = 3072

ADAM_LR = 0.001
ADAM_B1 = 0.9
ADAM_B2 = 0.999
ADAM_EPS = 1e-08
ADAM_WD = 0.01
ADAM_STEP = 10

NEG = -1e30


def _dot(a, b, prec=None):
    return lax.dot_general(a, b, (((1,), (0,)), ((), ())), precision=prec, preferred_element_type=F32)


def _dot_nt(a, b, prec=None):
    return lax.dot_general(a, b, (((1,), (1,)), ((), ())), precision=prec, preferred_element_type=F32)


def _dot_tn(a, b, prec=None):
    return lax.dot_general(a, b, (((0,), (0,)), ((), ())), precision=prec, preferred_element_type=F32)


def _bdot(a, b):
    return _dot(a.astype(BF16), b.astype(BF16))


def _bdot_nt(a, b):
    return _dot_nt(a.astype(BF16), b.astype(BF16))


def _bdot_tn(a, b):
    return _dot_tn(a.astype(BF16), b.astype(BF16))


def _sigmoid(x):
    return 1.0 / (1.0 + jnp.exp(-x))


def _silu(x):
    return x * _sigmoid(x)


def _silu_grad(x):
    s = _sigmoid(x)
    return s * (1.0 + x * (1.0 - s))


def _rms(x, w):
    r = lax.rsqrt(jnp.mean(x * x, axis=-1, keepdims=True) + NORM_EPS)
    xhat = x * r
    return xhat * w, xhat, r


def _rms_bwd(dy, w, xhat, r):
    dxhat = dy * w
    dx = r * (dxhat - xhat * jnp.mean(dxhat * xhat, axis=-1, keepdims=True))
    dw = jnp.sum(dy * xhat, axis=0, keepdims=True)
    return dx, dw


def _arb(n):
    return pltpu.CompilerParams(dimension_semantics=("arbitrary",) * n)


def _tile(n, want):
    t = min(n, want)
    assert n % t == 0, (n, want)
    return t


def _iota2(shape, dim):
    return lax.broadcasted_iota(jnp.int32, shape, dim)


_NN = (((1,), (0,)), ((), ()))
_NT = (((1,), (1,)), ((), ()))
_TN = (((0,), (0,)), ((), ()))


def _raw1(a, b, dn):
    return lax.dot_general(a.astype(BF16), b.astype(BF16), dn, preferred_element_type=F32)


def _raw3(a, b, dn):
    ah, bh = a.astype(BF16), b.astype(BF16)
    al, bl = (a - ah.astype(F32)).astype(BF16), (b - bh.astype(F32)).astype(BF16)
    f = lambda x, y: lax.dot_general(x, y, dn, preferred_element_type=F32)
    return f(ah, bh) + f(ah, bl) + f(al, bh)


def _make_mm(raw):
    @jax.custom_vjp
    def mm(a, b):
        return raw(a, b, _NN)

    mm.defvjp(lambda a, b: (raw(a, b, _NN), (a, b)),
              lambda res, ct: (raw(ct, res[1], _NT), raw(res[0], ct, _TN)))

    @jax.custom_vjp
    def mm_nt(a, b):
        return raw(a, b, _NT)

    mm_nt.defvjp(lambda a, b: (raw(a, b, _NT), (a, b)),
                 lambda res, ct: (raw(ct, res[1], _NN), raw(ct, res[0], _TN)))
    return mm, mm_nt


_mm1, _mm1_nt = _make_mm(_raw1)
_mm3, _mm3_nt = _make_mm(_raw3)


def _eye(n):
    return (_iota2((n, n), 0) == _iota2((n, n), 1)).astype(F32)


def _each(f, *lists):
    return [f(*t) for t in zip(*lists)]


def _inv_newton(mats):
    eye = _eye(mats[0].shape[0])
    ps = [-a for a in mats]
    ms = [eye + p for p in ps]
    k = 1
    while 2 * k < GDN_CHUNK:
        ps = [_raw1(p, p, _NN) for p in ps]
        ms = _each(lambda m, p: m + _raw1(m, p, _NN), ms, ps)
        k *= 2
    rs = _each(lambda a, m: eye - m - _raw3(a, m, _NN), mats, ms)
    return _each(lambda m, r: m + _raw1(m, r, _NN), ms, rs)


@jax.custom_vjp
def _unit_lower_inv(mats):
    return _inv_newton(mats)


def _unit_lower_inv_fwd(mats):
    ms = _inv_newton(mats)
    return ms, ms


def _unit_lower_inv_bwd(ms, dms):
    ts = _each(lambda m, dm: _raw3(m, dm, _TN), ms, dms)
    return (_each(lambda t, m: -_raw3(t, m, _NT), ts, ms),)


_unit_lower_inv.defvjp(_unit_lower_inv_fwd, _unit_lower_inv_bwd)


def _gdn_local(q, k, v, dcb, dcb128, dlb128, bb128):
    n = q[0].shape[0]
    r, s = _iota2((n, n), 0), _iota2((n, n), 1)
    same = (r // GDN_CHUNK) == (s // GDN_CHUNK)
    causal, strict = same & (r >= s), same & (r > s)
    eye = _eye(n)
    decay_l = [jnp.exp(jnp.where(causal, d - d.T, NEG)) for d in dcb]
    kb = _each(lambda a, b: a * b, k, bb128)
    a = _each(lambda x, y, dl: jnp.where(strict, _mm1_nt(x, y) * dl, 0.0), kb, k, decay_l)
    m_off = [m - eye for m in _unit_lower_inv(a)]
    edc = [jnp.exp(d) for d in dcb128]
    rhs = _each(lambda vv, bb, kk, e: jnp.concatenate([vv * bb, kk * e], axis=1), v, bb128, kb, edc)
    sol = _each(lambda x, m: x + _mm3(m, x), rhs, m_off)
    aqk = _each(lambda x, y, dl: jnp.where(causal, _mm1_nt(x, y) * dl, 0.0), q, k, decay_l)
    q_dec = _each(lambda x, e: x * e, q, edc)
    k_dec = _each(lambda x, dl, dc: x * jnp.exp(dl - dc), k, dlb128, dcb128)
    return sol, aqk, q_dec, k_dec


def _gated_norm(o, z, nw):
    r = lax.rsqrt(jnp.mean(o * o, axis=-1, keepdims=True) + NORM_EPS)
    return o * r * nw * _silu(z)


def _gated_norm_bwd(dy, o, z, nw):
    r = lax.rsqrt(jnp.mean(o * o, axis=-1, keepdims=True) + NORM_EPS)
    xhat = o * r
    sz = _silu(z)
    dxhat = dy * nw * sz
    do = r * (dxhat - xhat * jnp.mean(dxhat * xhat, axis=-1, keepdims=True))
    dz = dy * xhat * nw * _silu_grad(z)
    dnw = jnp.sum(dy * xhat * sz, axis=0, keepdims=True)
    return do, dz, dnw


GDN_HEAD_GROUP = 8
GC_BETA, GC_G, GC_DECAY, GC_LAST =0, GDN_HEADS, 2 * GDN_HEADS, 3 * GDN_HEADS


def _gate_cols(gc, h, rows):
    lane = _iota2((rows, LANE), 1)
    col = lambda off: jnp.sum(jnp.where(lane == off + h, gc, 0.0), axis=-1, keepdims=True)
    return col(GC_BETA), col(GC_DECAY), col(GC_LAST)


def _gdn_local_args(q_ref, k_ref, v_ref, gc_ref, hp, n):
    gc = gc_ref[...]
    lanes = [slice(hh * LANE, (hh + 1) * LANE) for hh in range(hp)]
    cols = [_gate_cols(gc, pl.program_id(0) * hp + hh, n) for hh in range(hp)]
    bcast = lambda i, w: [jnp.broadcast_to(c[i], (n, w)) for c in cols]
    args = ([q_ref[:, ls] for ls in lanes], [k_ref[:, ls] for ls in lanes], [v_ref[:, ls] for ls in lanes],
            bcast(1, n), bcast(1, LANE), bcast(2, LANE), bcast(0, LANE))
    return args, [c[2] for c in cols], lanes


def gdn_chunk_fwd(qkv, gc, proj, norm_w):
    t = qkv.shape[0]
    c = GDN_CHUNK
    tc = _tile(t, 256)
    nsub = tc // c

    hp = GDN_HEAD_GROUP

    def body(q_ref, k_ref, v_ref, gc_ref, z_ref, nw_ref, on_ref, st_ref, s_ref):
        @pl.when(pl.program_id(1) == 0)
        def _():
            s_ref[...] = jnp.zeros_like(s_ref)

        args, dl, lanes = _gdn_local_args(q_ref, k_ref, v_ref, gc_ref, hp, tc)
        sol, aqk, q_dec, k_dec = _gdn_local(*args)
        u = [x[:, :GDN_DK] for x in sol]
        w = [x[:, GDN_DK:] for x in sol]
        s = [s_ref[hh] for hh in range(hp)]
        v_new, o_state = [[] for _ in range(hp)], [[] for _ in range(hp)]
        for j in range(nsub):
            sl = slice(j * c, (j + 1) * c)
            for hh in range(hp):
                st_ref[hh, j] = s[hh]
            vn = _each(lambda uu, ww, ss: uu[sl] - _bdot(ww[sl], ss), u, w, s)
            os_ = _each(lambda qq, ss: _bdot(qq[sl], ss), q_dec, s)
            s = _each(lambda ss, d, kk, vv: ss * jnp.exp(d[j * c:j * c + 1]) + _bdot_tn(kk[sl], vv), s, dl, k_dec, vn)
            for hh in range(hp):
                v_new[hh].append(vn[hh])
                o_state[hh].append(os_[hh])
        for hh in range(hp):
            s_ref[hh] = s[hh]
        cat = lambda xs: jnp.concatenate(xs, axis=0)
        o = _each(lambda os_, aa, vv: cat(os_) + _bdot(aa, cat(vv)), o_state, aqk, v_new)
        for hh, ls in enumerate(lanes):
            on_ref[:, ls] = _gated_norm(o[hh], z_ref[:, ls], nw_ref[...]).astype(BF16)

    col = lambda off: pl.BlockSpec((tc, hp * LANE), lambda g, i: (i, off // hp + g))
    return pl.pallas_call(
        body, name="gdn_chunk_fwd",
        grid=(GDN_HEADS // hp, t // tc),
        in_specs=[col(0), col(GDN_HEADS), col(2 * GDN_HEADS),
                  pl.BlockSpec((tc, LANE), lambda g, i: (i, 0)),
                  col(GDN_QKV_W // LANE),
                  pl.BlockSpec((1, LANE), lambda g, i: (0, 0))],
        out_specs=[col(0),
                   pl.BlockSpec((hp, nsub, GDN_DK, GDN_DK), lambda g, i: (g, i, 0, 0))],
        out_shape=[jax.ShapeDtypeStruct((t, GDN_HEADS * GDN_DK), BF16),
                   jax.ShapeDtypeStruct((GDN_HEADS, t // c, GDN_DK, GDN_DK), F32)],
        scratch_shapes=[pltpu.VMEM((hp, GDN_DK, GDN_DK), F32)],
        compiler_params=_arb(2),
    )(qkv, qkv, qkv, gc, proj, norm_w)


def gdn_chunk_bwd(qkv, gc, proj, norm_w, states, d_on):
    t = qkv.shape[0]
    c = GDN_CHUNK
    tc = _tile(t, 256)
    nsub = tc // c
    nblk = t // tc
    hp = GDN_HEAD_GROUP

    def body(q_ref, k_ref, v_ref, gc_ref, z_ref, nw_ref, st_ref, don_ref,
             dq_ref, dk_ref, dv_ref, dz_ref, dgc_ref, dnw_ref, ds_ref):
        @pl.when(pl.program_id(1) == 0)
        def _():
            ds_ref[...] = jnp.zeros_like(ds_ref)

        @pl.when((pl.program_id(1) == 0) & (pl.program_id(0) == 0))
        def _():
            dnw_ref[...] = jnp.zeros_like(dnw_ref)

        rows = [slice(j * c, (j + 1) * c) for j in range(nsub)]
        cat = lambda xs: jnp.concatenate(xs, axis=0)
        lsum = lambda x: jnp.sum(x, axis=-1, keepdims=True)
        lane = _iota2((tc, LANE), 1)
        row = _iota2((tc, 1), 0)
        heads = range(hp)
        args, dl, lanes = _gdn_local_args(q_ref, k_ref, v_ref, gc_ref, hp, tc)
        (sol, aqk, q_dec, k_dec), vjp = jax.vjp(_gdn_local, *args)
        u = [x[:, :GDN_DK] for x in sol]
        w = [x[:, GDN_DK:] for x in sol]
        states = [[st_ref[hh, j] for j in range(nsub)] for hh in heads]
        v_new = _each(lambda uu, ww, st: [uu[sl] - _bdot(ww[sl], s) for sl, s in zip(rows, st)], u, w, states)
        v_all = [cat(v) for v in v_new]
        o = _each(lambda qq, st, aa, vv: cat([_bdot(qq[sl], s) for sl, s in zip(rows, st)]) + _bdot(aa, vv),
                  q_dec, states, aqk, v_all)
        nw = nw_ref[...]
        gn = [_gated_norm_bwd(don_ref[:, ls], o[hh], z_ref[:, ls], nw) for hh, ls in enumerate(lanes)]
        do = [x[0] for x in gn]
        for hh, ls in enumerate(lanes):
            dnw_ref[...] += gn[hh][2]
            dz_ref[:, ls] = gn[hh][1].astype(BF16)
        d_aqk = _each(_bdot_nt, do, v_all)
        dv_o = _each(_bdot_tn, aqk, do)
        ds = [ds_ref[hh] for hh in heads]
        d_u, d_w, d_qdec, d_kdec, d_last = ([[None] * nsub for _ in heads] for _ in range(5))
        for j in reversed(range(nsub)):
            sl = rows[j]
            sj = [states[hh][j] for hh in heads]
            cd = [jnp.exp(d[j * c:j * c + 1]) for d in dl]
            du = _each(lambda dvo, kk, dd: dvo[sl] + _bdot(kk[sl], dd), dv_o, k_dec, ds)
            dqd = _each(lambda dd, s: _bdot_nt(dd[sl], s), do, sj)
            dkd = _each(lambda vv, dd: _bdot_nt(vv[j], dd), v_new, ds)
            dla = _each(lambda s, dd, cc: jnp.sum(lsum(s * dd), axis=0, keepdims=True) * cc, sj, ds, cd)
            dw = _each(lambda x, s: -_bdot_nt(x, s), du, sj)
            ds = _each(lambda qq, dd, cc, dsn, ww, x: _bdot_tn(qq[sl], dd[sl]) + cc * dsn - _bdot_tn(ww[sl], x),
                       q_dec, do, cd, ds, w, du)
            for hh in heads:
                d_u[hh][j], d_w[hh][j], d_qdec[hh][j], d_kdec[hh][j], d_last[hh][j] = du[hh], dw[hh], dqd[hh], dkd[hh], dla[hh]
        for hh in heads:
            ds_ref[hh] = ds[hh]
        d_sol = _each(lambda a, b: jnp.concatenate([cat(a), cat(b)], axis=1), d_u, d_w)
        dq, dk, dv, d_dcb, d_dcb128, d_dlb128, d_bb = vjp((d_sol, d_aqk, [cat(x) for x in d_qdec], [cat(x) for x in d_kdec]))
        dgc = jnp.zeros((tc, LANE), F32)
        for hh, ls in enumerate(lanes):
            h = pl.program_id(0) * hp + hh
            dq_ref[:, ls] = dq[hh]
            dk_ref[:, ls] = dk[hh]
            dv_ref[:, ls] = dv[hh]
            d_dl = lsum(d_dlb128[hh])
            for j in range(nsub):
                d_dl = d_dl + jnp.where(row == j * c, d_last[hh][j], 0.0)
            dgc = dgc + jnp.where(lane == GC_BETA + h, lsum(d_bb[hh]),
                                  jnp.where(lane == GC_DECAY + h, lsum(d_dcb[hh]) + lsum(d_dcb128[hh]),
                                            jnp.where(lane == GC_LAST + h, d_dl, 0.0)))
        dgc_ref[0] = dgc

    rev = lambda i: nblk - 1 - i
    col = lambda off: pl.BlockSpec((tc, hp * LANE), lambda g, i: (rev(i), off // hp + g))
    return pl.pallas_call(
        body, name="gdn_chunk_bwd",
        grid=(GDN_HEADS // hp, nblk),
        in_specs=[col(0), col(GDN_HEADS), col(2 * GDN_HEADS),
                  pl.BlockSpec((tc, LANE), lambda g, i: (rev(i), 0)),
                  col(GDN_QKV_W // LANE),
                  pl.BlockSpec((1, LANE), lambda g, i: (0, 0)),
                  pl.BlockSpec((hp, nsub, GDN_DK, GDN_DK), lambda g, i: (g, rev(i), 0, 0)),
                  col(0)],
        out_specs=[col(0), col(0), col(0), col(0),
                   pl.BlockSpec((1, tc, LANE), lambda g, i: (g, rev(i), 0)),
                   pl.BlockSpec((1, LANE), lambda g, i: (0, 0))],
        out_shape=[
            jax.ShapeDtypeStruct((t, GDN_HEADS * GDN_DK), F32),
            jax.ShapeDtypeStruct((t, GDN_HEADS * GDN_DK), F32),
            jax.ShapeDtypeStruct((t, GDN_HEADS * GDN_DK), F32),
            jax.ShapeDtypeStruct((t, GDN_HEADS * GDN_DK), BF16),
            jax.ShapeDtypeStruct((GDN_HEADS // hp, t, LANE), F32),
            jax.ShapeDtypeStruct((1, LANE), F32)],
        scratch_shapes=[pltpu.VMEM((hp, GDN_DK, GDN_DK), F32)],
        compiler_params=_arb(2),
    )(qkv, qkv, qkv, gc, proj, norm_w, states, d_on)


GDN_PROJ_W = GDN_QKV_W + 1024 + LANE
GDN_Q_SCALE = GDN_DK ** -0.5


def _shift_rows(ext, shift, lo, n):
    if shift == 0:
        return ext[lo:lo + n]
    return pltpu.roll(ext, shift, 0)[lo:lo + n]


def _conv_fwd(x, halo, w):
    n = x.shape[0]
    ext = jnp.concatenate([halo, x], axis=0)
    y = w[GDN_CONV - 1:GDN_CONV] * x
    for j in range(GDN_CONV - 1):
        y = y + w[j:j + 1] * _shift_rows(ext, GDN_CONV - 1 - j, 8, n)
    return y


def _chunk_masks(n):
    r, s = _iota2((n, n), 0), _iota2((n, n), 1)
    same = (r // GDN_CHUNK) == (s // GDN_CHUNK)
    return (same & (r >= s)).astype(F32), (same & (r <= s)).astype(F32), same.astype(F32)


def _softplus(x):
    return jnp.maximum(x, 0.0) + jnp.log(1.0 + jnp.exp(-jnp.abs(x)))


def _l2n(t):
    rs = lax.rsqrt(jnp.sum(t * t, axis=-1, keepdims=True) + NORM_EPS)
    return t * rs, rs


def gdn_pre_fwd(proj, conv_w, gate_par):
    t = proj.shape[0]
    tm = _tile(t, 256)

    def body(x_ref, halo_ref, ba_ref, w_ref, gp_ref, qkv_ref, gc_ref):
        i = pl.program_id(0)
        halo = jnp.where(i > 0, halo_ref[...], 0.0)
        y = _silu(_conv_fwd(x_ref[...], halo, w_ref[...]))
        for hh in range(2 * GDN_HEADS):
            sl = slice(hh * LANE, (hh + 1) * LANE)
            tn, _ = _l2n(y[:, sl])
            qkv_ref[:, sl] = tn * GDN_Q_SCALE if hh < GDN_HEADS else tn
        qkv_ref[:, 2 * GDN_HEADS * LANE:] = y[:, 2 * GDN_HEADS * LANE:]
        ba = ba_ref[...]
        lane = _iota2(ba.shape, 1)
        gp = gp_ref[...]
        is_a = (lane >= GC_G) & (lane < GC_G + GDN_HEADS)
        g = jnp.where(is_a, -jnp.exp(gp[0:1]) * _softplus(ba + gp[1:2]), 0.0)
        tri, _, same = _chunk_masks(tm)
        decay = pltpu.roll(_dot(tri, g, HI), GC_DECAY - GC_G, 1)
        last = pltpu.roll(_dot(same, g, HI), GC_LAST - GC_G, 1)
        gc_ref[...] = jnp.where(lane < GDN_HEADS, _sigmoid(ba), g) + decay + last

    return pl.pallas_call(
        body, name="gdn_pre_fwd",
        grid=(t // tm,),
        in_specs=[pl.BlockSpec((tm, GDN_QKV_W), lambda i: (i, 0)),
                  pl.BlockSpec((8, GDN_QKV_W), lambda i: (jnp.maximum(i * (tm // 8) - 1, 0), 0)),
                  pl.BlockSpec((tm, LANE), lambda i: (i, (GDN_QKV_W + 1024) // LANE)),
                  pl.BlockSpec((8, GDN_QKV_W), lambda i: (0, 0)),
                  pl.BlockSpec((8, LANE), lambda i: (0, 0))],
        out_specs=[pl.BlockSpec((tm, GDN_QKV_W), lambda i: (i, 0)),
                   pl.BlockSpec((tm, LANE), lambda i: (i, 0))],
        out_shape=[jax.ShapeDtypeStruct((t, GDN_QKV_W), F32), jax.ShapeDtypeStruct((t, LANE), F32)],
        compiler_params=_arb(1),
    )(proj, proj, proj, conv_w, gate_par)


def gdn_pre_bwd(proj, conv_w, gate_par, dq, dk, dv, dgc_heads):
    t = proj.shape[0]
    tm = _tile(t, 256)

    def body(x_ref, halo_ref, ba_ref, w_ref, gp_ref, dq_ref, dk_ref, dv_ref, dgc_ref, dy_ref, dba_ref, dgp_ref):
        i = pl.program_id(0)

        @pl.when(i == 0)
        def _():
            dgp_ref[...] = jnp.zeros_like(dgp_ref)

        halo = jnp.where(i > 0, halo_ref[...], 0.0)
        y = _conv_fwd(x_ref[...], halo, w_ref[...])
        for hh in range(3 * GDN_HEADS):
            sl = slice(hh * LANE, (hh + 1) * LANE)
            hsl = slice((hh % GDN_HEADS) * LANE, (hh % GDN_HEADS + 1) * LANE)
            yy = y[:, sl]
            if hh < 2 * GDN_HEADS:
                tn, rs = _l2n(_silu(yy))
                dtn = dq_ref[:, hsl] * GDN_Q_SCALE if hh < GDN_HEADS else dk_ref[:, hsl]
                dsil = rs * (dtn - tn * jnp.sum(dtn * tn, axis=-1, keepdims=True))
            else:
                dsil = dv_ref[:, hsl]
            dy_ref[:, sl] = dsil * _silu_grad(yy)
        dgc = dgc_ref[0]
        for hh in range(1, dgc_heads.shape[0]):
            dgc = dgc + dgc_ref[hh]
        ba = ba_ref[...]
        lane = _iota2(ba.shape, 1)
        _, tri_t, same = _chunk_masks(tm)
        d_decay = jnp.where((lane >= GC_DECAY) & (lane < GC_DECAY + GDN_HEADS), dgc, 0.0)
        d_last = jnp.where((lane >= GC_LAST) & (lane < GC_LAST + GDN_HEADS), dgc, 0.0)
        dgc = (jnp.where(lane < GDN_HEADS, dgc, 0.0) + pltpu.roll(_dot(tri_t, d_decay, HI), LANE - (GC_DECAY - GC_G), 1)
               + pltpu.roll(_dot(same, d_last, HI), LANE - (GC_LAST - GC_G), 1))
        gp = gp_ref[...]
        xg = ba + gp[1:2]
        ea = jnp.exp(gp[0:1])
        sp = _softplus(xg)
        sb = _sigmoid(ba)
        is_b = lane < GDN_HEADS
        is_a = (lane >= GDN_HEADS) & (lane < 2 * GDN_HEADS)
        d_pre = jnp.where(is_a, dgc * (-ea) * _sigmoid(xg), 0.0)
        dba_ref[...] = jnp.where(is_b, dgc * sb * (1.0 - sb), d_pre).astype(BF16)
        d_alog = jnp.sum(jnp.where(is_a, dgc * (-ea) * sp, 0.0), axis=0, keepdims=True)
        d_dtb = jnp.sum(d_pre, axis=0, keepdims=True)
        row = _iota2((8, LANE), 0)
        dgp_ref[...] += jnp.where(row == 0, d_alog, jnp.where(row == 1, d_dtb, 0.0))

    hspec = pl.BlockSpec((tm, GDN_HEADS * LANE), lambda i: (i, 0))
    return pl.pallas_call(
        body, name="gdn_pre_bwd",
        grid=(t // tm,),
        in_specs=[pl.BlockSpec((tm, GDN_QKV_W), lambda i: (i, 0)),
                  pl.BlockSpec((8, GDN_QKV_W), lambda i: (jnp.maximum(i * (tm // 8) - 1, 0), 0)),
                  pl.BlockSpec((tm, LANE), lambda i: (i, (GDN_QKV_W + 1024) // LANE)),
                  pl.BlockSpec((8, GDN_QKV_W), lambda i: (0, 0)),
                  pl.BlockSpec((8, LANE), lambda i: (0, 0)),
                  hspec, hspec, hspec,
                  pl.BlockSpec((dgc_heads.shape[0], tm, LANE), lambda i: (0, i, 0))],
        out_specs=[pl.BlockSpec((tm, GDN_QKV_W), lambda i: (i, 0)),
                   pl.BlockSpec((tm, LANE), lambda i: (i, 0)),
                   pl.BlockSpec((8, LANE), lambda i: (0, 0))],
        out_shape=[jax.ShapeDtypeStruct((t, GDN_QKV_W), F32), jax.ShapeDtypeStruct((t, LANE), BF16),
                   jax.ShapeDtypeStruct((8, LANE), F32)],
        compiler_params=_arb(1),
    )(proj, proj, proj, conv_w, gate_par, dq, dk, dv, dgc_heads)


def gdn_conv_bwd(proj, conv_w, dy):
    t = proj.shape[0]
    tm = _tile(t, 256)
    nblk = t // tm

    def body(x_ref, halo_ref, w_ref, dy_ref, dyn_ref, dx_ref, dw_ref):
        i = pl.program_id(0)

        @pl.when(i == 0)
        def _():
            dw_ref[...] = jnp.zeros_like(dw_ref)

        w = w_ref[...]
        dy = dy_ref[...]
        ext_dy = jnp.concatenate([dy, jnp.where(i < nblk - 1, dyn_ref[...], 0.0)], axis=0)
        ext_x = jnp.concatenate([jnp.where(i > 0, halo_ref[...], 0.0), x_ref[...]], axis=0)
        dx = w[GDN_CONV - 1:GDN_CONV] * dy
        rows = [jnp.sum(dy * x_ref[...], axis=0, keepdims=True)]
        for j in range(GDN_CONV - 1):
            sh = GDN_CONV - 1 - j
            dx = dx + w[j:j + 1] * _shift_rows(ext_dy, tm + 8 - sh, 0, tm)
            rows.insert(j, jnp.sum(dy * _shift_rows(ext_x, sh, 8, tm), axis=0, keepdims=True))
        dx_ref[...] = dx.astype(BF16)
        row = _iota2((8, GDN_QKV_W), 0)
        acc = jnp.zeros((8, GDN_QKV_W), F32)
        for j in range(GDN_CONV):
            acc = acc + jnp.where(row == j, rows[j], 0.0)
        dw_ref[...] += acc

    return pl.pallas_call(
        body, name="gdn_conv_bwd",
        grid=(nblk,),
        in_specs=[pl.BlockSpec((tm, GDN_QKV_W), lambda i: (i, 0)),
                  pl.BlockSpec((8, GDN_QKV_W), lambda i: (jnp.maximum(i * (tm // 8) - 1, 0), 0)),
                  pl.BlockSpec((8, GDN_QKV_W), lambda i: (0, 0)),
                  pl.BlockSpec((tm, GDN_QKV_W), lambda i: (i, 0)),
                  pl.BlockSpec((8, GDN_QKV_W), lambda i: (jnp.minimum((i + 1) * (tm // 8), t // 8 - 1), 0))],
        out_specs=[pl.BlockSpec((tm, GDN_QKV_W), lambda i: (i, 0)),
                   pl.BlockSpec((8, GDN_QKV_W), lambda i: (0, 0))],
        out_shape=[jax.ShapeDtypeStruct((t, GDN_QKV_W), BF16), jax.ShapeDtypeStruct((8, GDN_QKV_W), F32)],
        compiler_params=_arb(1),
    )(proj, proj, conv_w, dy, dy)


def _resident(w_hbm, w_vmem, sem):
    @pl.when(pl.program_id(0) == 0)
    def _():
        cp = pltpu.make_async_copy(w_hbm, w_vmem, sem)
        cp.start()
        cp.wait()


ANY = pl.BlockSpec(memory_space=pl.ANY)


def norm_proj(h, nw, w, bias):
    t, d = h.shape
    n = w.shape[1]
    tm = _tile(t, 256)
    nc = _tile(n, 1536) if n % 1536 == 0 else _tile(n, 1408)

    def body(h_ref, nw_ref, w_hbm, b_ref, o_ref, w_ref, sem):
        _resident(w_hbm, w_ref, sem)
        hn = _rms(h_ref[...], nw_ref[...])[0].astype(BF16)
        for c0 in range(0, n, nc):
            o_ref[:, c0:c0 + nc] = _dot(hn, w_ref[:, c0:c0 + nc]) + b_ref[:, c0:c0 + nc]

    return pl.pallas_call(
        body, name="norm_proj",
        grid=(t // tm,),
        in_specs=[pl.BlockSpec((tm, d), lambda i: (i, 0)), pl.BlockSpec((1, d), lambda i: (0, 0)), ANY,
                  pl.BlockSpec((1, n), lambda i: (0, 0))],
        out_specs=pl.BlockSpec((tm, n), lambda i: (i, 0)),
        out_shape=jax.ShapeDtypeStruct((t, n), F32),
        scratch_shapes=[pltpu.VMEM((d, n), BF16), pltpu.SemaphoreType.DMA],
        compiler_params=_arb(1),
    )(h, nw, w, bias)


def linear_residual(h, x, w, bias):
    t, d = h.shape
    k = x.shape[1]
    tm = _tile(t, 512)

    def body(h_ref, x_ref, w_hbm, b_ref, o_ref, w_ref, sem):
        _resident(w_hbm, w_ref, sem)
        o_ref[...] = h_ref[...] + _dot(x_ref[...], w_ref[...]) + b_ref[...]

    return pl.pallas_call(
        body, name="linear_residual",
        grid=(t // tm,),
        in_specs=[pl.BlockSpec((tm, d), lambda i: (i, 0)), pl.BlockSpec((tm, k), lambda i: (i, 0)), ANY,
                  pl.BlockSpec((1, d), lambda i: (0, 0))],
        out_specs=pl.BlockSpec((tm, d), lambda i: (i, 0)),
        out_shape=jax.ShapeDtypeStruct((t, d), F32),
        scratch_shapes=[pltpu.VMEM((k, d), BF16), pltpu.SemaphoreType.DMA],
        compiler_params=_arb(1),
    )(h, x, w, bias)


def matmul_nt(dy, w):
    t, d = dy.shape
    k = w.shape[0]
    tm = _tile(t, 512)

    def body(dy_ref, w_hbm, o_ref, cs_ref, w_ref, sem):
        _resident(w_hbm, w_ref, sem)

        @pl.when(pl.program_id(0) == 0)
        def _():
            cs_ref[...] = jnp.zeros_like(cs_ref)

        dy = dy_ref[...]
        cs_ref[...] += jnp.sum(dy, axis=0, keepdims=True)
        o_ref[...] = _dot_nt(dy.astype(BF16), w_ref[...])

    return pl.pallas_call(
        body, name="matmul_nt",
        grid=(t // tm,),
        in_specs=[pl.BlockSpec((tm, d), lambda i: (i, 0)), ANY],
        out_specs=[pl.BlockSpec((tm, k), lambda i: (i, 0)), pl.BlockSpec((1, d), lambda i: (0, 0))],
        out_shape=[jax.ShapeDtypeStruct((t, k), F32), jax.ShapeDtypeStruct((1, d), F32)],
        scratch_shapes=[pltpu.VMEM((k, d), BF16), pltpu.SemaphoreType.DMA],
        compiler_params=_arb(1),
    )(dy, w)


def norm_proj_bwd(h, nw, dh, dps, ws):
    t, d = h.shape
    np_ = len(dps)
    ns = [w.shape[1] for w in ws]
    tm = _tile(t, 256)

    def body(*refs):
        h_ref, nw_ref, dh_ref = refs[:3]
        dp_refs = refs[3:3 + np_]
        w_hbms = refs[3 + np_:3 + 2 * np_]
        o_ref, hn_ref, dnw_ref = refs[3 + 2 * np_:6 + 2 * np_]
        cs_refs = refs[6 + 2 * np_:6 + 3 * np_]
        w_refs = refs[6 + 3 * np_:6 + 4 * np_]
        sem = refs[6 + 4 * np_]
        for a, b in zip(w_hbms, w_refs):
            _resident(a, b, sem)

        @pl.when(pl.program_id(0) == 0)
        def _():
            dnw_ref[...] = jnp.zeros_like(dnw_ref)
            for c in cs_refs:
                c[...] = jnp.zeros_like(c)

        nw = nw_ref[...]
        hn, xhat, r = _rms(h_ref[...], nw)
        hn_ref[...] = hn.astype(BF16)
        dhn = jnp.zeros((tm, d), F32)
        for dp_ref, w_ref, cs_ref in zip(dp_refs, w_refs, cs_refs):
            dp = dp_ref[...]
            cs_ref[...] += jnp.sum(dp.astype(F32), axis=0, keepdims=True)
            dhn = dhn + _dot_nt(dp, w_ref[...])
        dx, dnw = _rms_bwd(dhn, nw, xhat, r)
        dnw_ref[...] += dnw
        o_ref[...] = dh_ref[...] + dx

    row = pl.BlockSpec((tm, d), lambda i: (i, 0))
    vec = pl.BlockSpec((1, d), lambda i: (0, 0))
    return pl.pallas_call(
        body, name="norm_proj_bwd",
        grid=(t // tm,),
        in_specs=[row, vec, row] + [pl.BlockSpec((tm, n), lambda i: (i, 0)) for n in ns] + [ANY] * np_,
        out_specs=[row, row, vec] + [pl.BlockSpec((1, n), lambda i: (0, 0)) for n in ns],
        out_shape=[jax.ShapeDtypeStruct((t, d), F32), jax.ShapeDtypeStruct((t, d), BF16),
                   jax.ShapeDtypeStruct((1, d), F32)] + [jax.ShapeDtypeStruct((1, n), F32) for n in ns],
        scratch_shapes=[pltpu.VMEM((d, n), BF16) for n in ns] + [pltpu.SemaphoreType.DMA],
        compiler_params=_arb(1),
    )(h, nw, dh, *dps, *ws)


def _pair_copies(g_refs, got_refs, send_sems, recv_sems):
    mx, my, mc = _me()
    copies = []
    for g_ref, got_ref in zip(g_refs, got_refs):
        for j in range(4):
            i = len(copies)
            copies.append(pltpu.make_async_remote_copy(
                src_ref=g_ref.at[j, 1 - mc], dst_ref=got_ref.at[j], send_sem=send_sems.at[i], recv_sem=recv_sems.at[i],
                device_id=(mx, my, 1 - mc), device_id_type=MESH))
    return copies


def _scatter_copies(p_refs, q_refs, send_sems, recv_sems):
    mx, my, mc = _me()
    copies = []
    for p_ref, q_ref in zip(p_refs, q_refs):
        for k in (1, 2, 3):
            px, py = _flip(mx, k >> 1), _flip(my, k & 1)
            i = len(copies)
            copies.append(pltpu.make_async_remote_copy(
                src_ref=p_ref.at[2 * px + py], dst_ref=q_ref.at[2 * mx + my], send_sem=send_sems.at[i],
                recv_sem=recv_sems.at[i], device_id=(px, py, mc), device_id_type=MESH))
    return copies


def _share_copies(o_refs, out_refs, send_sems, recv_sems):
    mx, my, mc = _me()
    return [pltpu.make_async_remote_copy(src_ref=o_ref, dst_ref=out_ref, send_sem=send_sems.at[i], recv_sem=recv_sems.at[i],
                                         device_id=(mx, my, 1 - mc), device_id_type=MESH)
            for i, (o_ref, out_ref) in enumerate(zip(o_refs, out_refs))]


CARRIED = {"pair": (_pair_copies, 4, lambda a: a.shape[:1] + a.shape[2:]), "scatter": (_scatter_copies, 3, lambda a: a.shape),
           "share": (_share_copies, 1, lambda a: a.shape)}


def matmul_tn(x, y, scale=1.0, carry=None):
    t, k = x.shape
    n = y.shape[1]
    tk = _tile(k, 1024) if k % 1024 == 0 else _tile(k, 1408)
    tn = n if n <= 1536 else (1024 if n % 1024 == 0 else 1408)
    assert n % tn == 0
    tt = _tile(t, 2048)
    grid = (k // tk, n // tn, t // tt)
    arrays = [] if carry is None else list(carry[1])
    na = len(arrays)

    def body(x_ref, y_ref, *rest):
        o_ref = rest[na]
        if carry is not None:
            make, _, _ = CARRIED[carry[0]]
            copies = lambda: make(rest[:na], rest[na + 1:2 * na + 1], rest[2 * na + 1], rest[2 * na + 2])
            pid = [pl.program_id(a) for a in range(3)]

            @pl.when((pid[0] == 0) & (pid[1] == 0) & (pid[2] == 0))
            def _():
                for cp in copies():
                    cp.start()

        @pl.when(pl.program_id(2) == 0)
        def _():
            o_ref[...] = jnp.zeros_like(o_ref)

        yv = y_ref[...]
        if scale != 1.0:
            yv = yv * scale
        o_ref[...] += _dot_tn(x_ref[...].astype(BF16), yv.astype(BF16))

        if carry is not None:
            @pl.when((pid[0] == grid[0] - 1) & (pid[1] == grid[1] - 1) & (pid[2] == grid[2] - 1))
            def _():
                for cp in copies():
                    cp.wait()

    out_shape = [jax.ShapeDtypeStruct((k, n), F32)]
    scratch = []
    if carry is not None:
        _, per_array, out_of = CARRIED[carry[0]]
        out_shape += [jax.ShapeDtypeStruct(out_of(a), a.dtype) for a in arrays]
        scratch = [pltpu.SemaphoreType.DMA((per_array * na,))] * 2
    res = pl.pallas_call(
        body, name="matmul_tn" if carry is None else "matmul_tn_" + carry[0],
        grid=grid,
        in_specs=[pl.BlockSpec((tt, tk), lambda i, j, s: (s, i)), pl.BlockSpec((tt, tn), lambda i, j, s: (s, j))] + [ANY] * na,
        out_specs=[pl.BlockSpec((tk, tn), lambda i, j, s: (i, j))] + [ANY] * na,
        out_shape=out_shape, scratch_shapes=scratch,
        compiler_params=_arb(3),
    )(x, y, *arrays)
    return res[0] if carry is None else (res[0], list(res[1:]))


FFN_CHUNKS = 2


class _FfnGather:
    def __init__(self, gu_sh, d_sh, gu_full, d_full, send_sems, recv_sems, local_sems):
        self.sh, self.full = (gu_sh, d_sh), (gu_full, d_full)
        self.send_sems, self.recv_sems, self.local_sems = send_sems, recv_sems, local_sems
        self.mx, self.my, self.mc = _me()
        self.me = 2 * self.mx + self.my
        self.w4, self.f4 = gu_sh.shape[1], d_sh.shape[0]
        self.hg, self.hd = gu_sh.shape[0] // 2, d_sh.shape[0] // 2

    def _window(self, a, chip, core=None):
        gu_full, d_full = self.full
        if a == 0:
            rows = pl.ds(0, 2 * self.hg) if core is None else pl.ds(pl.multiple_of(core * self.hg, 16), self.hg)
            return gu_full.at[rows, pl.ds(pl.multiple_of(chip * self.w4, LANE), self.w4)]
        if core is None:
            return d_full.at[pl.ds(pl.multiple_of(chip * self.f4, 16), self.f4), :]
        return d_full.at[pl.ds(pl.multiple_of(chip * self.f4 + core * self.hd, 16), self.hd), :]

    def _half(self, a):
        n = (self.hg, self.hd)[a]
        return self.sh[a].at[pl.ds(pl.multiple_of(self.mc * n, 16), n), :]

    def _peer(self, k):
        return _flip(self.mx, k >> 1), _flip(self.my, k & 1)

    def _fetch(self, k, a):
        px, py = self._peer(k)
        i = 2 * (k - 1) + a
        return pltpu.make_async_remote_copy(src_ref=self._half(a), dst_ref=self._window(a, self.me, self.mc),
                                            send_sem=self.send_sems.at[i], recv_sem=self.recv_sems.at[i],
                                            device_id=(px, py, self.mc), device_id_type=MESH)

    def _relay(self, k, a):
        px, py = self._peer(k)
        got = self._window(a, 2 * px + py, self.mc)
        i = 6 + 2 * (k - 1) + a
        return pltpu.make_async_remote_copy(src_ref=got, dst_ref=got, send_sem=self.send_sems.at[i],
                                            recv_sem=self.recv_sems.at[i], device_id=(self.mx, self.my, 1 - self.mc),
                                            device_id_type=MESH)

    def _own(self, a):
        return pltpu.make_async_copy(self.sh[a], self._window(a, self.me), self.local_sems.at[a])

    def start(self):
        for a in (0, 1):
            self._own(a).start()
            for k in (1, 2, 3):
                self._fetch(k, a).start()

    def relay(self):
        for k in (1, 2, 3):
            for a in (0, 1):
                self._fetch(k, a).wait_recv()
                self._relay(k, a).start()

    def finish(self):
        for k in (1, 2, 3):
            for a in (0, 1):
                self._fetch(k, a).wait_send()
                self._relay(k, a).wait()
        for a in (0, 1):
            self._own(a).wait()


def ffn_fwd(h, nw, wgu, wd, nxt=None):
    t, d = h.shape
    f = wd.shape[0]
    fc = f // FFN_CHUNKS
    tm = _tile(t, 512)
    nsteps = t // tm

    def body(h_ref, nw_ref, wgu_hbm, wd_hbm, *rest):
        if nxt is None:
            o_ref, gu_ref, a_ref, wgu_ref, wd_ref, sem = rest
            gather = None
        else:
            (gu_sh, d_sh, o_ref, gu_ref, a_ref, gu_full, d_full, wgu_ref, wd_ref, sem,
             send_sems, recv_sems, local_sems) = rest
            gather = _FfnGather(gu_sh, d_sh, gu_full, d_full, send_sems, recv_sems, local_sems)
            pl.when(pl.program_id(0) == 0)(gather.start)
        _resident(wgu_hbm, wgu_ref, sem)
        _resident(wd_hbm, wd_ref, sem)
        if gather is not None:
            pl.when(pl.program_id(0) == nsteps // 2)(gather.relay)
        x = h_ref[...]
        hn = _rms(x, nw_ref[...])[0].astype(BF16)
        acc = jnp.zeros((tm, d), F32)
        for c in range(FFN_CHUNKS):
            gs, us = slice(c * fc, (c + 1) * fc), slice(f + c * fc, f + (c + 1) * fc)
            g = _dot(hn, wgu_ref[:, gs])
            u = _dot(hn, wgu_ref[:, us])
            a = (_silu(g) * u).astype(BF16)
            gu_ref[:, gs] = g.astype(BF16)
            gu_ref[:, us] = u.astype(BF16)
            a_ref[:, gs] = a
            acc = acc + _dot(a, wd_ref[gs, :])
        o_ref[...] = x + 0.5 * acc
        if gather is not None:
            pl.when(pl.program_id(0) == nsteps - 1)(gather.finish)

    row = lambda w: pl.BlockSpec((tm, w), lambda i: (i, 0))
    in_specs = [row(d), pl.BlockSpec((1, d), lambda i: (0, 0)), ANY, ANY]
    out_specs = [row(d), row(2 * f), row(f)]
    out_shape = [jax.ShapeDtypeStruct((t, d), F32), jax.ShapeDtypeStruct((t, 2 * f), BF16),
                 jax.ShapeDtypeStruct((t, f), BF16)]
    scratch = [pltpu.VMEM((d, 2 * f), BF16), pltpu.VMEM((f, d), BF16), pltpu.SemaphoreType.DMA]
    args = (h, nw, wgu, wd)
    if nxt is not None:
        in_specs += [ANY, ANY]
        out_specs += [ANY, ANY]
        out_shape += [jax.ShapeDtypeStruct((d, 2 * f), BF16), jax.ShapeDtypeStruct((f, d), BF16)]
        scratch += [pltpu.SemaphoreType.DMA((12,)), pltpu.SemaphoreType.DMA((12,)), pltpu.SemaphoreType.DMA((2,))]
        args += tuple(nxt)
    return pl.pallas_call(
        body, name="ffn_fwd_gather" if nxt is not None else "ffn_fwd",
        grid=(nsteps,),
        in_specs=in_specs, out_specs=out_specs, out_shape=out_shape, scratch_shapes=scratch,
        compiler_params=_arb(1),
    )(*args)


def ffn_bwd(h, nw, wgu, wd, gu, dh):
    t, d = h.shape
    f = wd.shape[0]
    fc = f // FFN_CHUNKS
    tm = _tile(t, 256)

    def body(h_ref, nw_ref, wgu_hbm, wd_hbm, gu_ref, dh_ref, o_ref, hn_ref, dgu_ref, dnw_ref, wgu_ref, wd_ref, sem):
        _resident(wgu_hbm, wgu_ref, sem)
        _resident(wd_hbm, wd_ref, sem)

        @pl.when(pl.program_id(0) == 0)
        def _():
            dnw_ref[...] = jnp.zeros_like(dnw_ref)

        nw = nw_ref[...]
        hn32, xhat, r = _rms(h_ref[...], nw)
        hn_ref[...] = hn32.astype(BF16)
        dh = dh_ref[...]
        dout = (0.5 * dh).astype(BF16)
        dhn = jnp.zeros((tm, d), F32)
        for c in range(FFN_CHUNKS):
            gs, us = slice(c * fc, (c + 1) * fc), slice(f + c * fc, f + (c + 1) * fc)
            g = gu_ref[:, gs].astype(F32)
            u = gu_ref[:, us].astype(F32)
            sg = _sigmoid(g)
            sil = g * sg
            da = _dot_nt(dout, wd_ref[gs, :])
            dg = (da * u * (sg * (1.0 + g * (1.0 - sg)))).astype(BF16)
            du = (da * sil).astype(BF16)
            dgu_ref[:, gs] = dg
            dgu_ref[:, us] = du
            dhn = dhn + _dot_nt(dg, wgu_ref[:, gs]) + _dot_nt(du, wgu_ref[:, us])
        dx, dnw = _rms_bwd(dhn, nw, xhat, r)
        dnw_ref[...] += dnw
        o_ref[...] = dh + dx

    row = pl.BlockSpec((tm, d), lambda i: (i, 0))
    vec = pl.BlockSpec((1, d), lambda i: (0, 0))
    return pl.pallas_call(
        body, name="ffn_bwd",
        grid=(t // tm,),
        in_specs=[row, vec, ANY, ANY, pl.BlockSpec((tm, 2 * f), lambda i: (i, 0)), row],
        out_specs=[row, row, pl.BlockSpec((tm, 2 * f), lambda i: (i, 0)), vec],
        out_shape=[jax.ShapeDtypeStruct((t, d), F32), jax.ShapeDtypeStruct((t, d), BF16),
                   jax.ShapeDtypeStruct((t, 2 * f), BF16), jax.ShapeDtypeStruct((1, d), F32)],
        scratch_shapes=[pltpu.VMEM((d, 2 * f), BF16), pltpu.VMEM((f, d), BF16), pltpu.SemaphoreType.DMA],
        compiler_params=_arb(1),
    )(h, nw, wgu, wd, gu, dh)


def loss_head(h, nw, target):
    t, d = h.shape
    tm = _tile(t, 512)

    def body(h_ref, nw_ref, tg_ref, dh_ref, loss_ref, dnw_ref):
        @pl.when(pl.program_id(0) == 0)
        def _():
            loss_ref[...] = jnp.zeros_like(loss_ref)
            dnw_ref[...] = jnp.zeros_like(dnw_ref)

        nw = nw_ref[...]
        y, xhat, r = _rms(h_ref[...], nw)
        e = y - tg_ref[...]
        loss_ref[...] += 0.5 * jnp.sum(jnp.mean(e * e, axis=-1, keepdims=True), axis=0, keepdims=True)
        dx, dnw = _rms_bwd(e * (1.0 / d), nw, xhat, r)
        dnw_ref[...] += dnw
        dh_ref[...] = dx

    row = pl.BlockSpec((tm, d), lambda i: (i, 0))
    vec = pl.BlockSpec((1, d), lambda i: (0, 0))
    return pl.pallas_call(
        body, name="loss_head",
        grid=(t // tm,),
        in_specs=[row, vec, row],
        out_specs=[row, pl.BlockSpec((8, LANE), lambda i: (0, 0)), vec],
        out_shape=[jax.ShapeDtypeStruct((t, d), F32), jax.ShapeDtypeStruct((8, LANE), F32),
                   jax.ShapeDtypeStruct((1, d), F32)],
        compiler_params=_arb(1),
    )(h, nw, target)


def adamw(w, g, m, v):
    r, c = w.shape
    tr = r
    while tr * c * 4 > (1 << 20) and tr % 16 == 0:
        tr //= 2

    def body(w_ref, g_ref, m_ref, v_ref, d_ref, nm_ref, nv_ref):
        g = g_ref[...]
        m = ADAM_B1 * m_ref[...] + (1.0 - ADAM_B1) * g
        v = ADAM_B2 * v_ref[...] + (1.0 - ADAM_B2) * (g * g)
        m_hat = m / (1.0 - ADAM_B1 ** ADAM_STEP)
        v_hat = v / (1.0 - ADAM_B2 ** ADAM_STEP)
        d_ref[...] = -ADAM_LR * (m_hat / (jnp.sqrt(v_hat) + ADAM_EPS) + ADAM_WD * w_ref[...])
        nm_ref[...] = m
        nv_ref[...] = v

    blk = pl.BlockSpec((tr, c), lambda i: (i, 0))
    return pl.pallas_call(
        body, name="adamw",
        grid=(r // tr,),
        in_specs=[blk] * 4, out_specs=[blk] * 3,
        out_shape=[jax.ShapeDtypeStruct((r, c), F32)] * 3,
        compiler_params=_arb(1),
    )(w, g, m, v)


ATTN_P_W = ATTN_Q_W + 2 * ATTN_KV_W
ATTN_SCALE = ATTN_HEAD_DIM ** -0.5


def _rope_group(t, tab, sign):
    return (t * tab[:, 0:LANE] + sign * pltpu.roll(t, 8, 1) * tab[:, LANE:2 * LANE]
            + sign * pltpu.roll(t, LANE - 8, 1) * tab[:, 2 * LANE:3 * LANE])


def _rope(t, tab, sign=1.0):
    return jnp.concatenate([_rope_group(t[:, s:s + LANE], tab, sign) for s in range(0, t.shape[1], LANE)], axis=1)


def _attn_heads(qs, ks, vs, sinks, first):
    b = ATTN_BLOCK
    rows = ATTN_GROUP * b
    qi = _iota2((rows, 2 * b), 0) % b
    kj = _iota2((rows, 2 * b), 1)
    rel = qi + b - kj
    valid = (rel >= 0) & (rel < b) & ((kj >= b) | jnp.logical_not(first))
    ss = _each(lambda q, k: jnp.where(valid, _mm1_nt(q, k) * ATTN_SCALE, NEG), qs, ks)
    ms = _each(lambda s, sink: lax.stop_gradient(jnp.maximum(jnp.max(s, axis=-1, keepdims=True), sink)), ss, sinks)
    ps = _each(lambda s, m: jnp.exp(s - m), ss, ms)
    dens = _each(lambda p, sink, m: jnp.sum(p, axis=-1, keepdims=True) + jnp.exp(sink - m), ps, sinks, ms)
    return _each(lambda p, den, v: _mm1(p / den, v), ps, dens, vs)


def _attn_prepare(p_ref, kvp_ref, tab_ref, tabp_ref, sink_ref):
    b = ATTN_BLOCK
    hd = ATTN_HEAD_DIM
    tab, tabp = tab_ref[...], tabp_ref[...]
    q = _rope(p_ref[:, 0:ATTN_Q_W], tab)
    kc = _rope(p_ref[:, ATTN_Q_W:ATTN_Q_W + ATTN_KV_W], tab)
    kp = _rope(kvp_ref[:, 0:ATTN_KV_W], tabp)
    vc = p_ref[:, ATTN_Q_W + ATTN_KV_W:ATTN_P_W]
    vp = kvp_ref[:, ATTN_KV_W:2 * ATTN_KV_W]
    sk = sink_ref[...]
    qs, ks, vs, sinks = [], [], [], []
    for h in range(ATTN_KV_HEADS):
        heads = [ATTN_GROUP * h + g for g in range(ATTN_GROUP)]
        qs.append(jnp.concatenate([q[:, i * hd:(i + 1) * hd] for i in heads], axis=0))
        ks.append(jnp.concatenate([kp[:, h * hd:(h + 1) * hd], kc[:, h * hd:(h + 1) * hd]], axis=0))
        vs.append(jnp.concatenate([vp[:, h * hd:(h + 1) * hd], vc[:, h * hd:(h + 1) * hd]], axis=0))
        sinks.append(jnp.concatenate([jnp.broadcast_to(sk[:, i:i + 1], (b, 1)) for i in heads], axis=0))
    return qs, ks, vs, sinks


def _unstack_heads(xs):
    b = ATTN_BLOCK
    return jnp.concatenate([x[g * b:(g + 1) * b] for x in xs for g in range(ATTN_GROUP)], axis=1)


def _attn_specs(nb):
    b = ATTN_BLOCK
    prev = lambda n: jnp.maximum(n - 1, 0)
    return [pl.BlockSpec((b, ATTN_P_W), lambda n: (n, 0)),
            pl.BlockSpec((b, 2 * ATTN_KV_W), lambda n: (prev(n), ATTN_Q_W // (2 * ATTN_KV_W))),
            pl.BlockSpec((b, 3 * LANE), lambda n: (n, 0)),
            pl.BlockSpec((b, 3 * LANE), lambda n: (prev(n), 0)),
            pl.BlockSpec((1, ATTN_Q_HEADS), lambda n: (0, 0))]


def attn_fwd(p, tab, sinks):
    t = p.shape[0]
    nb = t // ATTN_BLOCK

    def body(p_ref, kvp_ref, tab_ref, tabp_ref, sink_ref, o_ref):
        qs, ks, vs, sk = _attn_prepare(p_ref, kvp_ref, tab_ref, tabp_ref, sink_ref)
        os_ = _attn_heads(qs, ks, vs, sk, pl.program_id(0) == 0)
        o_ref[...] = _unstack_heads(os_).astype(BF16)

    return pl.pallas_call(
        body, name="attn_fwd",
        grid=(nb,),
        in_specs=_attn_specs(nb),
        out_specs=pl.BlockSpec((ATTN_BLOCK, ATTN_Q_W), lambda n: (n, 0)),
        out_shape=jax.ShapeDtypeStruct((t, ATTN_Q_W), BF16),
        compiler_params=_arb(1),
    )(p, p, tab, tab, sinks)


def attn_bwd(p, tab, sinks, do):
    t = p.shape[0]
    b = ATTN_BLOCK
    hd = ATTN_HEAD_DIM
    nb = t // b

    def body(p_ref, kvp_ref, tab_ref, tabp_ref, sink_ref, do_ref, dq_ref, dkvc_ref, dkvp_ref, dsink_ref):
        n = pl.program_id(0)

        @pl.when(n == 0)
        def _():
            dsink_ref[...] = jnp.zeros_like(dsink_ref)

        qs, ks, vs, sk = _attn_prepare(p_ref, kvp_ref, tab_ref, tabp_ref, sink_ref)
        first = n == 0
        _, vjp = jax.vjp(lambda a, bb, c, d: _attn_heads(a, bb, c, d, first), qs, ks, vs, sk)
        do = do_ref[...]
        dos = [jnp.concatenate([do[:, i * hd:(i + 1) * hd] for i in range(ATTN_GROUP * h, ATTN_GROUP * (h + 1))], axis=0)
               for h in range(ATTN_KV_HEADS)]
        dqs, dks, dvs, dsk = vjp(dos)
        tab, tabp = tab_ref[...], tabp_ref[...]
        dq_ref[...] = _rope(_unstack_heads([x.astype(F32) for x in dqs]), tab, -1.0).astype(BF16)
        dkc = jnp.concatenate([x.astype(F32)[b:] for x in dks], axis=1)
        dkp = jnp.concatenate([x.astype(F32)[:b] for x in dks], axis=1)
        dkvc_ref[:, 0:ATTN_KV_W] = _rope(dkc, tab, -1.0)
        dkvp_ref[:, 0:ATTN_KV_W] = _rope(dkp, tabp, -1.0)
        dkvc_ref[:, ATTN_KV_W:] = jnp.concatenate([x.astype(F32)[b:] for x in dvs], axis=1)
        dkvp_ref[:, ATTN_KV_W:] = jnp.concatenate([x.astype(F32)[:b] for x in dvs], axis=1)
        parts = [jnp.sum(d[g * b:(g + 1) * b], axis=0, keepdims=True) for d in dsk for g in range(ATTN_GROUP)]
        dsink_ref[...] += jnp.concatenate(parts, axis=1)

    blk = lambda w: pl.BlockSpec((b, w), lambda n: (n, 0))
    return pl.pallas_call(
        body, name="attn_bwd",
        grid=(nb,),
        in_specs=_attn_specs(nb) + [blk(ATTN_Q_W)],
        out_specs=[blk(ATTN_Q_W), blk(2 * ATTN_KV_W), blk(2 * ATTN_KV_W),
                   pl.BlockSpec((1, ATTN_Q_HEADS), lambda n: (0, 0))],
        out_shape=[jax.ShapeDtypeStruct((t, ATTN_Q_W), BF16), jax.ShapeDtypeStruct((t, 2 * ATTN_KV_W), F32),
                   jax.ShapeDtypeStruct((t, 2 * ATTN_KV_W), F32), jax.ShapeDtypeStruct((1, ATTN_Q_HEADS), F32)],
        compiler_params=_arb(1),
    )(p, p, tab, tab, sinks, do)


def kv_combine(dkvc, dkvp):
    t, w = dkvc.shape
    b = ATTN_BLOCK
    nb = t // b

    def body(c_ref, p_ref, o_ref):
        nxt = jnp.where(pl.program_id(0) < nb - 1, p_ref[...], 0.0)
        o_ref[...] = (c_ref[...] + nxt).astype(BF16)

    return pl.pallas_call(
        body, name="kv_combine",
        grid=(nb,),
        in_specs=[pl.BlockSpec((b, w), lambda n: (n, 0)),
                  pl.BlockSpec((b, w), lambda n: (jnp.minimum(n + 1, nb - 1), 0))],
        out_specs=pl.BlockSpec((b, w), lambda n: (n, 0)),
        out_shape=jax.ShapeDtypeStruct((t, w), BF16),
        compiler_params=_arb(1),
    )(dkvc, dkvp)


def _me():
    return lax.axis_index("x"), lax.axis_index("y"), lax.axis_index("c")


def _flip(v, bit):
    return 1 - v if bit else v


def _chip_index():
    return 2 * lax.axis_index("x") + lax.axis_index("y")


def allgather_chips(x, split):
    r, c = x.shape
    rh = r // 2 if split else r

    def body(x_ref, o_ref, send_sems, recv_sems):
        mx, my, mc = _me()
        me = 2 * mx + my
        rows = pl.ds(pl.multiple_of(mc * rh, 16), rh) if split else pl.ds(0, r)
        fetched, passed = [], []
        for k in (1, 2, 3):
            peer = (_flip(mx, k >> 1), _flip(my, k & 1), mc)
            cp = pltpu.make_async_remote_copy(src_ref=x_ref.at[rows], dst_ref=o_ref.at[me, rows],
                                              send_sem=send_sems.at[k - 1], recv_sem=recv_sems.at[k - 1],
                                              device_id=peer, device_id_type=MESH)
            cp.start()
            fetched.append(cp)
        if split:
            for k in (1, 2, 3):
                fetched[k - 1].wait_recv()
                theirs = o_ref.at[2 * _flip(mx, k >> 1) + _flip(my, k & 1), rows]
                cp = pltpu.make_async_remote_copy(src_ref=theirs, dst_ref=theirs, send_sem=send_sems.at[2 + k],
                                                  recv_sem=recv_sems.at[2 + k], device_id=(mx, my, 1 - mc),
                                                  device_id_type=MESH)
                cp.start()
                passed.append(cp)
            for cp in fetched:
                cp.wait_send()
            for cp in passed:
                cp.wait()
        else:
            for cp in fetched:
                cp.wait()

    got = pl.pallas_call(
        body, name="allgather_chips",
        in_specs=[ANY], out_specs=ANY,
        out_shape=jax.ShapeDtypeStruct((4, r, c), x.dtype),
        scratch_shapes=[pltpu.SemaphoreType.DMA((6,)), pltpu.SemaphoreType.DMA((6,))],
    )(x)
    return lax.dynamic_update_slice(got, x[None], (_chip_index(), 0, 0))


def exchange(kind, arrays):
    make, per_array, out_of = CARRIED[kind]
    na = len(arrays)

    def body(*refs):
        copies = make(refs[:na], refs[na:2 * na], refs[2 * na], refs[2 * na + 1])
        for cp in copies:
            cp.start()
        for cp in copies:
            cp.wait()

    return pl.pallas_call(
        body, name=kind + "_exchange",
        in_specs=[ANY] * na, out_specs=[ANY] * na,
        out_shape=[jax.ShapeDtypeStruct(out_of(a), a.dtype) for a in arrays],
        scratch_shapes=[pltpu.SemaphoreType.DMA((per_array * na,))] * 2,
    )(*arrays)


def _row_tile(r, c):
    return max(d for d in range(16, r + 1, 16) if r % d == 0 and d * c * 4 <= (2 << 20))


def add_pair(g, got, core):
    _, _, r, c = g.shape
    tr = _row_tile(r, c)

    def body(core_ref, g_ref, got_ref, p32_ref, p16_ref):
        s = g_ref[...] + got_ref[...]
        p32_ref[...] = s
        p16_ref[...] = s.astype(BF16)

    blk = pl.BlockSpec((None, tr, c), lambda j, i, core_ref: (j, i, 0))
    return pl.pallas_call(
        body, name="add_pair",
        grid_spec=pltpu.PrefetchScalarGridSpec(
            num_scalar_prefetch=1, grid=(4, r // tr),
            in_specs=[pl.BlockSpec((None, None, tr, c), lambda j, i, core_ref: (j, core_ref[0], i, 0)), blk],
            out_specs=[blk, blk]),
        out_shape=[jax.ShapeDtypeStruct((4, r, c), F32), jax.ShapeDtypeStruct((4, r, c), BF16)],
        compiler_params=_arb(2),
    )(core, g, got)


def sum_slots(p32, q16, order):
    _, r, c = p32.shape
    tr = _row_tile(r, c)

    def body(order_ref, own_ref, a_ref, b_ref, c_ref, o_ref):
        o_ref[...] = ((own_ref[...] + a_ref[...].astype(F32)) + b_ref[...].astype(F32)) + c_ref[...].astype(F32)

    slot = lambda k: pl.BlockSpec((None, tr, c), functools.partial(lambda k, i, order_ref: (order_ref[k], i, 0), k))
    return pl.pallas_call(
        body, name="sum_slots",
        grid_spec=pltpu.PrefetchScalarGridSpec(
            num_scalar_prefetch=1, grid=(r // tr,),
            in_specs=[slot(0), slot(1), slot(2), slot(3)],
            out_specs=pl.BlockSpec((tr, c), lambda i, order_ref: (i, 0))),
        out_shape=jax.ShapeDtypeStruct((r, c), F32),
        compiler_params=_arb(1),
    )(order, p32, q16, q16, q16)


def allreduce_small(x):
    r, c = x.shape

    def body(x_ref, o_ref, buf, send_sems, recv_sems):
        mx, my, mc = _me()
        me = 4 * mx + 2 * my + mc
        buf[pl.ds(me, 1)] = x_ref[...][None]
        copies = []
        for k in range(1, 8):
            peer = (_flip(mx, k >> 2), _flip(my, (k >> 1) & 1), _flip(mc, k & 1))
            cp = pltpu.make_async_remote_copy(src_ref=x_ref, dst_ref=buf.at[me], send_sem=send_sems.at[k - 1],
                                              recv_sem=recv_sems.at[k - 1], device_id=peer, device_id_type=MESH)
            cp.start()
            copies.append(cp)
        for cp in copies:
            cp.wait()
        acc = buf[0]
        for d in range(1, 8):
            acc = acc + buf[d]
        o_ref[...] = acc

    return pl.pallas_call(
        body, name="allreduce_small",
        out_shape=jax.ShapeDtypeStruct((r, c), F32),
        scratch_shapes=[pltpu.VMEM((8, r, c), F32), pltpu.SemaphoreType.DMA((7,)), pltpu.SemaphoreType.DMA((7,))],
    )(x)


class GradReduce:
    def __init__(self, grads, shard_shapes):
        self.shapes = shard_shapes
        self.groups = _by_width(shard_shapes)
        self.core = lax.axis_index("c")
        me = _chip_index()
        self.order = jnp.stack([me, me ^ 1, me ^ 2, me ^ 3]).astype(jnp.int32)
        self.g = []
        for names in self.groups.values():
            slots = []
            for j in range(4):
                parts = []
                for n in names:
                    w = shard_shapes[n][1 + SHARDED[n]]
                    parts += [lax.slice_in_dim(gl, j * w, (j + 1) * w, axis=SHARDED[n]) for gl in grads[n]]
                slots.append(jnp.concatenate(parts, axis=0))
            g = jnp.stack(slots)
            self.g.append(g.reshape(4, 2, g.shape[1] // 2, g.shape[2]))
        self.p32 = self.p16 = None
        self.q16 = [None] * len(self.g)
        n = len(self.g)
        self.pieces = [list(range(2, n)), [1], [0]] if n > 2 else [[i] for i in reversed(range(n))]
        self.pieces = [p for p in self.pieces if p]
        self.stage = 0

    def next_job(self):
        if self.stage == 0:
            return "pair", self.g
        if self.stage <= len(self.pieces):
            return "scatter", [self.p16[i] for i in self.pieces[self.stage - 1]]
        return None

    def deliver(self, res):
        if self.stage == 0:
            core = self.core.astype(jnp.int32)[None]
            pairs = [add_pair(g, got, core) for g, got in zip(self.g, res)]
            self.p32, self.p16 = [p[0] for p in pairs], [p[1] for p in pairs]
        else:
            for i, q in zip(self.pieces[self.stage - 1], res):
                self.q16[i] = q
        self.stage += 1

    def finish(self):
        while (job := self.next_job()) is not None:
            self.deliver(exchange(*job))
        mine = [sum_slots(p, q, self.order) for p, q in zip(self.p32, self.q16)]
        theirs = exchange("share", mine)
        out = {}
        for names, m, t in zip(self.groups.values(), mine, theirs):
            red = jnp.where(self.core == 0, jnp.concatenate([m, t]), jnp.concatenate([t, m]))
            off = 0
            for n in names:
                layers, a, _ = self.shapes[n]
                out[n] = red[off:off + layers * a].reshape(self.shapes[n])
                off += layers * a
        return out


PACK_W = 1024
PACK_ROWS = 1024

SHARDED = {"ffn1_w_gate_up": 1, "ffn1_w_down": 0, "ffn2_w_gate_up": 1, "ffn2_w_down": 0, "attn_w_in": 1,
           "attn_w_out": 0, "gdn_w_in": 1, "gdn_w_out": 0, "gdn_conv_w": 1}
REPLICATED = ["ffn1_norm", "mix_norm", "ffn2_norm", "attn_b_in", "attn_sinks", "attn_b_out", "gdn_A_log",
              "gdn_dt_bias", "gdn_norm_w", "final_norm"]
WEIGHTS = ["ffn1_norm", "ffn1_w_gate_up", "ffn1_w_down", "mix_norm", "ffn2_norm", "ffn2_w_gate_up", "ffn2_w_down",
           "attn_w_in", "attn_b_in", "attn_sinks", "attn_w_out", "attn_b_out", "gdn_w_in", "gdn_conv_w", "gdn_A_log",
           "gdn_dt_bias", "gdn_norm_w", "gdn_w_out", "final_norm"]


def _pack_rows(flats, dtype, width, row_multiple):
    flat = jnp.concatenate([f.astype(dtype).reshape(-1) for f in flats])
    per = width * row_multiple
    pad = (-flat.shape[0]) % per
    if pad:
        flat = jnp.concatenate([flat, jnp.zeros((pad,), dtype)])
    return flat.reshape(-1, width)


def _by_width(shapes):
    groups = {}
    for n, shape in shapes.items():
        groups.setdefault(shape[-1], []).append(n)
    return groups


def _gather_weights(shards, dtype, split=True):
    full = {}
    for width, names in _by_width({n: s.shape for n, s in shards.items()}).items():
        packed = jnp.concatenate([shards[n].astype(dtype).reshape(-1, width) for n in names], axis=0)
        got = allgather_chips(packed, split)
        off = 0
        for n in names:
            layers, a, _ = shards[n].shape
            full[n] = [jnp.concatenate([got[j, off + l * a:off + (l + 1) * a] for j in range(4)], axis=SHARDED[n])
                       for l in range(layers)]
            off += layers * a
    return full


def _small_pack(items):
    rows = []
    for a in items:
        f = a.astype(F32).reshape(-1)
        pad = (-f.shape[0]) % LANE
        rows.append(jnp.concatenate([f, jnp.zeros((pad,), F32)]) if pad else f)
    return _pack_rows(rows, F32, LANE, 8)


def _small_unpack(buf, shapes):
    flat = buf.reshape(-1)
    out, off = [], 0
    for shape in shapes:
        size = 1
        for s in shape:
            size *= s
        out.append(flat[off:off + size].reshape(shape))
        off += size + (-size) % LANE
    return out


def _rope_table(positions):
    t = positions.shape[0]
    inv_freq = ROPE_THETA ** (-jnp.arange(0, ROPE_DIM, 2, dtype=F32) / ROPE_DIM)
    ang = positions.astype(F32)[:, None] * inv_freq
    cos, sin = jnp.cos(ang), jnp.sin(ang)
    rest = ATTN_HEAD_DIM - ROPE_DIM
    zeros = lambda n: jnp.zeros((t, n), F32)
    c64 = jnp.concatenate([cos, cos, jnp.ones((t, rest), F32)], axis=1)
    s_up = jnp.concatenate([zeros(ROPE_DIM // 2), sin, zeros(rest)], axis=1)
    s_dn = jnp.concatenate([-sin, zeros(ROPE_DIM // 2 + rest)], axis=1)
    return jnp.concatenate([c64, c64, s_up, s_up, s_dn, s_dn], axis=1)


def _as2d(a):
    return a.reshape(-1, a.shape[-1]) if a.ndim > 1 else a.reshape(1, -1)


def kernel(x, positions, ffn1_norm, ffn1_w_gate_up, ffn1_w_down, mix_norm, ffn2_norm, ffn2_w_gate_up, ffn2_w_down, attn_w_in, attn_b_in, attn_sinks, attn_w_out, attn_b_out, gdn_w_in, gdn_conv_w, gdn_A_log, gdn_dt_bias, gdn_norm_w, gdn_w_out, final_norm, loss_target, m_ffn1_norm, m_ffn1_w_gate_up, m_ffn1_w_down, m_mix_norm, m_ffn2_norm, m_ffn2_w_gate_up, m_ffn2_w_down, m_attn_w_in, m_attn_b_in, m_attn_sinks, m_attn_w_out, m_attn_b_out, m_gdn_w_in, m_gdn_conv_w, m_gdn_A_log, m_gdn_dt_bias, m_gdn_norm_w, m_gdn_w_out, m_final_norm, v_ffn1_norm, v_ffn1_w_gate_up, v_ffn1_w_down, v_mix_norm, v_ffn2_norm, v_ffn2_w_gate_up, v_ffn2_w_down, v_attn_w_in, v_attn_b_in, v_attn_sinks, v_attn_w_out, v_attn_b_out, v_gdn_w_in, v_gdn_conv_w, v_gdn_A_log, v_gdn_dt_bias, v_gdn_norm_w, v_gdn_w_out, v_final_norm):
    given = dict(locals())
    w = {n: given[n] for n in WEIGHTS}
    d = D_MODEL
    h = x[0]
    target = loss_target[0]
    depth = ffn1_norm.shape[0]

    big = [n for n in SHARDED if n != "gdn_conv_w"]
    up_front = {n: (w[n][0:1] if n.startswith("ffn1") else w[n]) for n in big if not n.startswith("ffn2")}
    full = _gather_weights(up_front, BF16)
    ffn_order = [(tag, l) for l in range(depth) for tag in ("ffn1", "ffn2")]
    ffn_w = {ffn_order[0]: (full["ffn1_w_gate_up"][0], full["ffn1_w_down"][0])}

    def run_ffn(tag, l, h):
        i = ffn_order.index((tag, l))
        wgu, wd = ffn_w[(tag, l)]
        nw = w[tag + "_norm"][l][None]
        if i + 1 == len(ffn_order):
            return ffn_fwd(h, nw, wgu, wd)
        ntag, nl = ffn_order[i + 1]
        shards = (w[ntag + "_w_gate_up"][nl].astype(BF16), w[ntag + "_w_down"][nl].astype(BF16))
        h, gu, a, ngu, nd = ffn_fwd(h, nw, wgu, wd, shards)
        ffn_w[(ntag, nl)] = (ngu, nd)
        return h, gu, a

    conv_full = _gather_weights({"gdn_conv_w": gdn_conv_w}, F32, split=False)["gdn_conv_w"]
    tab = _rope_table(positions[0])
    zero_d = jnp.zeros((1, d), F32)

    def gdn_params(j):
        w_in = full["gdn_w_in"][j]
        w_cat = jnp.concatenate([w_in, jnp.zeros((d, GDN_PROJ_W - w_in.shape[1]), BF16)], axis=1)
        conv = jnp.concatenate([conv_full[j], jnp.zeros((8 - GDN_CONV, GDN_QKV_W), F32)], axis=0)
        lanes = lambda vec: jnp.concatenate([jnp.zeros((GC_G,), F32), vec, jnp.zeros((LANE - GC_G - GDN_HEADS,), F32)])
        par = jnp.concatenate([lanes(gdn_A_log[j])[None], lanes(gdn_dt_bias[j])[None], jnp.zeros((6, LANE), F32)], axis=0)
        return w_cat, conv, par

    saved = []
    for l in range(depth):
        j = l // 2
        rec = {"h1": h}
        h, rec["ffn1_gu"], rec["ffn1_a"] = run_ffn("ffn1", l, h)
        rec["h2"] = h
        if l % 2 == 0:
            p = norm_proj(h, mix_norm[l][None], full["attn_w_in"][j], attn_b_in[j][None])
            o = attn_fwd(p, tab, attn_sinks[j][None])
            h = linear_residual(h, o, full["attn_w_out"][j], attn_b_out[j][None])
            rec.update(p=p, o=o)
        else:
            w_cat, conv, par = gdn_params(j)
            proj = norm_proj(h, mix_norm[l][None], w_cat, jnp.zeros((1, GDN_PROJ_W), F32))
            qkv, gc = gdn_pre_fwd(proj, conv, par)
            on, st = gdn_chunk_fwd(qkv, gc, proj, gdn_norm_w[j][None])
            h = linear_residual(h, on, full["gdn_w_out"][j], zero_d)
            rec.update(proj=proj, qkv=qkv, gc=gc, on=on, st=st, w_cat=w_cat, conv=conv, par=par)
        rec["h3"] = h
        h, rec["ffn2_gu"], rec["ffn2_a"] = run_ffn("ffn2", l, h)
        saved.append(rec)

    dh, loss_tile, d_final = loss_head(h, final_norm[None], target)

    g = {n: [None] * w[n].shape[0] for n in WEIGHTS if n != "final_norm"}
    reducers = []

    def layer_of(n, i):
        return i if n.startswith("ffn") else (2 * i if n.startswith("attn") else 2 * i + 1)

    def layer_reduce(layer):
        picks = {n: idx for n in big if (idx := [i for i in range(w[n].shape[0]) if layer_of(n, i) == layer])}
        return GradReduce({n: [g[n][i] for i in idx] for n, idx in picks.items()},
                          {n: (len(idx),) + w[n].shape[1:] for n, idx in picks.items()})

    def tn(x, y, scale=1.0):
        big_enough = x.shape[1] * y.shape[1] >= D_MODEL * D_FF
        red = next((r for r in reducers if r.next_job() is not None), None) if big_enough else None
        if red is None:
            return matmul_tn(x, y, scale)
        out, res = matmul_tn(x, y, scale, carry=red.next_job())
        red.deliver(res)
        return out
    for l in reversed(range(depth)):
        j = l // 2
        rec = saved[l]

        def ffn_back(tag, h_in, dh):
            wgu, wd = ffn_w[(tag, l)]
            dh_new, hn, dgu, dn = ffn_bwd(h_in, w[tag + "_norm"][l][None], wgu, wd, rec[tag + "_gu"], dh)
            g[tag + "_w_gate_up"][l] = tn(hn, dgu)
            g[tag + "_w_down"][l] = tn(rec[tag + "_a"], dh, 0.5)
            g[tag + "_norm"][l] = dn[0]
            return dh_new

        dh = ffn_back("ffn2", rec["h3"], dh)
        if l % 2 == 0:
            w_in, w_out = full["attn_w_in"][j], full["attn_w_out"][j]
            do, db_out = matmul_nt(dh, w_out)
            g["attn_w_out"][j] = matmul_tn(rec["o"], dh)
            g["attn_b_out"][j] = db_out[0]
            dq, dkvc, dkvp, dsink = attn_bwd(rec["p"], tab, attn_sinks[j][None], do)
            dkv = kv_combine(dkvc, dkvp)
            dh, hn, dn, cs_q, cs_kv = norm_proj_bwd(rec["h2"], mix_norm[l][None], dh, [dq, dkv],
                                                    [w_in[:, :ATTN_Q_W], w_in[:, ATTN_Q_W:]])
            g["attn_w_in"][j] = jnp.concatenate([matmul_tn(hn, dq), matmul_tn(hn, dkv)], axis=1)
            g["attn_b_in"][j] = jnp.concatenate([cs_q[0], cs_kv[0]])
            g["attn_sinks"][j] = dsink[0]
        else:
            w_cat, conv, par = rec["w_cat"], rec["conv"], rec["par"]
            d_on, _ = matmul_nt(dh, full["gdn_w_out"][j])
            g["gdn_w_out"][j] = matmul_tn(rec["on"], dh)
            dq, dk, dv, dz, dgc_heads, dnw = gdn_chunk_bwd(rec["qkv"], rec["gc"], rec["proj"], gdn_norm_w[j][None],
                                                           rec["st"], d_on)
            dy, dba, dpar = gdn_pre_bwd(rec["proj"], conv, par, dq, dk, dv, dgc_heads)
            dx, dconv = gdn_conv_bwd(rec["proj"], conv, dy)
            nz = GDN_QKV_W + GDN_HEADS * GDN_DK
            dh, hn, dn, _, _, _ = norm_proj_bwd(rec["h2"], mix_norm[l][None], dh, [dx, dz, dba],
                                                [w_cat[:, :GDN_QKV_W], w_cat[:, GDN_QKV_W:nz], w_cat[:, nz:]])
            g["gdn_w_in"][j] = jnp.concatenate(
                [tn(hn, dx), matmul_tn(hn, dz), matmul_tn(hn, dba)[:, :2 * GDN_HEADS]], axis=1)
            g["gdn_conv_w"][j] = dconv[:GDN_CONV]
            g["gdn_A_log"][j] = dpar[0, GDN_HEADS:2 * GDN_HEADS]
            g["gdn_dt_bias"][j] = dpar[1, GDN_HEADS:2 * GDN_HEADS]
            g["gdn_norm_w"][j] = dnw[0]
        g["mix_norm"][l] = dn[0]
        dh = ffn_back("ffn1", rec["h1"], dh)
        if l > 0:
            reducers.append(layer_reduce(l))
    grad_x = dh[None]

    small_names = REPLICATED + ["gdn_conv_w"]
    local = {n: jnp.stack(g[n]) for n in small_names if n != "final_norm"}
    local["final_norm"] = d_final[0]
    small_shapes = [(1,)] + [local[n].shape for n in small_names]
    small = allreduce_small(_small_pack([loss_tile[0, 0:1]] + [local[n] for n in small_names]))
    small = _small_unpack(small, small_shapes)
    loss = small[0][0]
    grads = dict(zip(small_names, small[1:]))
    conv_cols = gdn_conv_w.shape[2]
    grads["gdn_conv_w"] = lax.dynamic_slice_in_dim(grads["gdn_conv_w"], _chip_index() * conv_cols, conv_cols, axis=2)
    by_layer = [layer_reduce(0).finish()] + [r.finish() for r in reversed(reducers)]
    grads.update({n: jnp.concatenate([part[n] for part in by_layer if n in part], axis=0) for n in big})

    delta, new_m, new_v = {}, {}, {}
    for n in SHARDED:
        dl, nm, nv = adamw(_as2d(w[n]), _as2d(grads[n]), _as2d(given["m_" + n]), _as2d(given["v_" + n]))
        delta[n], new_m[n], new_v[n] = dl.reshape(w[n].shape), nm.reshape(w[n].shape), nv.reshape(w[n].shape)
    shapes = [w[n].shape for n in REPLICATED]
    packed = [_small_pack([src[n] for n in REPLICATED]) for src in
              (w, grads, {n: given["m_" + n] for n in REPLICATED}, {n: given["v_" + n] for n in REPLICATED})]
    for dst, buf in zip((delta, new_m, new_v), adamw(*packed)):
        dst.update(zip(REPLICATED, _small_unpack(buf, shapes)))

    return (loss, grad_x, *[grads[n] for n in WEIGHTS], *[delta[n] for n in WEIGHTS],
            *[new_m[n] for n in WEIGHTS], *[new_v[n] for n in WEIGHTS])
```

```python
import functools

import jax
import jax.numpy as jnp
from jax import lax
from jax.experimental import pallas as pl
from jax.experimental.pallas import tpu as pltpu

F32 = jnp.float32
BF16 = jnp.bfloat16
HI = lax.Precision.HIGHEST
MESH = pl.DeviceIdType.MESH

D_MODEL = 1024
D_FF = 2816
DEPTH = 4
NORM_EPS = 1e-6
LANE = 128

ATTN_Q_HEADS = 16
ATTN_KV_HEADS = 4
ATTN_HEAD_DIM = 64
ATTN_GROUP = 4
ATTN_BLOCK = 128
ROPE_DIM = 16
ROPE_THETA = 500000.0
ATTN_Q_W = 1024
ATTN_KV_W = 256

GDN_HEADS = 8
GDN_DK = 128
GDN_CONV = 4
GDN_CHUNK = 64
GDN_QKV_W = 3072

ADAM_LR = 0.001
ADAM_B1 = 0.9
ADAM_B2 = 0.999
ADAM_EPS = 1e-08
ADAM_WD = 0.01
ADAM_STEP = 10

NEG = -1e30


def _dot(a, b, prec=None):
    return lax.dot_general(a, b, (((1,), (0,)), ((), ())), precision=prec, preferred_element_type=F32)


def _dot_nt(a, b, prec=None):
    return lax.dot_general(a, b, (((1,), (1,)), ((), ())), precision=prec, preferred_element_type=F32)


def _dot_tn(a, b, prec=None):
    return lax.dot_general(a, b, (((0,), (0,)), ((), ())), precision=prec, preferred_element_type=F32)


def _bdot(a, b):
    return _dot(a.astype(BF16), b.astype(BF16))


def _bdot_nt(a, b):
    return _dot_nt(a.astype(BF16), b.astype(BF16))


def _bdot_tn(a, b):
    return _dot_tn(a.astype(BF16), b.astype(BF16))


def _sigmoid(x):
    return 1.0 / (1.0 + jnp.exp(-x))


def _silu(x):
    return x * _sigmoid(x)


def _silu_grad(x):
    s = _sigmoid(x)
    return s * (1.0 + x * (1.0 - s))


def _rms(x, w):
    r = lax.rsqrt(jnp.mean(x * x, axis=-1, keepdims=True) + NORM_EPS)
    xhat = x * r
    return xhat * w, xhat, r


def _rms_bwd(dy, w, xhat, r):
    dxhat = dy * w
    dx = r * (dxhat - xhat * jnp.mean(dxhat * xhat, axis=-1, keepdims=True))
    dw = jnp.sum(dy * xhat, axis=0, keepdims=True)
    return dx, dw


def _arb(n):
    return pltpu.CompilerParams(dimension_semantics=("arbitrary",) * n)


def _tile(n, want):
    t = min(n, want)
    assert n % t == 0, (n, want)
    return t


def _iota2(shape, dim):
    return lax.broadcasted_iota(jnp.int32, shape, dim)


_NN = (((1,), (0,)), ((), ()))
_NT = (((1,), (1,)), ((), ()))
_TN = (((0,), (0,)), ((), ()))


def _raw1(a, b, dn):
    return lax.dot_general(a.astype(BF16), b.astype(BF16), dn, preferred_element_type=F32)


def _raw3(a, b, dn):
    ah, bh = a.astype(BF16), b.astype(BF16)
    al, bl = (a - ah.astype(F32)).astype(BF16), (b - bh.astype(F32)).astype(BF16)
    f = lambda x, y: lax.dot_general(x, y, dn, preferred_element_type=F32)
    return f(ah, bh) + f(ah, bl) + f(al, bh)


def _make_mm(raw):
    @jax.custom_vjp
    def mm(a, b):
        return raw(a, b, _NN)

    mm.defvjp(lambda a, b: (raw(a, b, _NN), (a, b)),
              lambda res, ct: (raw(ct, res[1], _NT), raw(res[0], ct, _TN)))

    @jax.custom_vjp
    def mm_nt(a, b):
        return raw(a, b, _NT)

    mm_nt.defvjp(lambda a, b: (raw(a, b, _NT), (a, b)),
                 lambda res, ct: (raw(ct, res[1], _NN), raw(ct, res[0], _TN)))
    return mm, mm_nt


_mm1, _mm1_nt = _make_mm(_raw1)
_mm3, _mm3_nt = _make_mm(_raw3)


def _eye(n):
    return (_iota2((n, n), 0) == _iota2((n, n), 1)).astype(F32)


def _each(f, *lists):
    return [f(*t) for t in zip(*lists)]


def _inv_newton(mats):
    eye = _eye(mats[0].shape[0])
    ps = [-a for a in mats]
    ms = [eye + p for p in ps]
    k = 1
    while 2 * k < GDN_CHUNK:
        ps = [_raw1(p, p, _NN) for p in ps]
        ms = _each(lambda m, p: m + _raw1(m, p, _NN), ms, ps)
        k *= 2
    rs = _each(lambda a, m: eye - m - _raw3(a, m, _NN), mats, ms)
    return _each(lambda m, r: m + _raw1(m, r, _NN), ms, rs)


def _inv_cotangent(ms, dms):
    ts = _each(lambda m, dm: _raw3(m, dm, _TN), ms, dms)
    return _each(lambda t, m: -_raw3(t, m, _NT), ts, ms)


def _chunk_causal(n):
    r, s = _iota2((n, n), 0), _iota2((n, n), 1)
    same = (r // GDN_CHUNK) == (s // GDN_CHUNK)
    return same & (r >= s), same & (r > s)


def _gdn_a(k, dcb, bb128):
    causal, strict = _chunk_causal(k[0].shape[0])
    decay_l = [jnp.exp(jnp.where(causal, d - d.T, NEG)) for d in dcb]
    kb = _each(lambda a, b: a * b, k, bb128)
    return _each(lambda x, y, dl: jnp.where(strict, _mm1_nt(x, y) * dl, 0.0), kb, k, decay_l)


def _gdn_rest(q, k, v, dcb, dcb128, dlb128, bb128, m_off):
    causal, _ = _chunk_causal(q[0].shape[0])
    decay_l = [jnp.exp(jnp.where(causal, d - d.T, NEG)) for d in dcb]
    kb = _each(lambda a, b: a * b, k, bb128)
    edc = [jnp.exp(d) for d in dcb128]
    rhs = _each(lambda vv, bb, kk, e: jnp.concatenate([vv * bb, kk * e], axis=1), v, bb128, kb, edc)
    sol = _each(lambda x, m: x + _mm3(m, x), rhs, m_off)
    aqk = _each(lambda x, y, dl: jnp.where(causal, _mm1_nt(x, y) * dl, 0.0), q, k, decay_l)
    q_dec = _each(lambda x, e: x * e, q, edc)
    k_dec = _each(lambda x, dl, dc: x * jnp.exp(dl - dc), k, dlb128, dcb128)
    return sol, aqk, q_dec, k_dec


def _gdn_local(q, k, v, dcb, dcb128, dlb128, bb128):
    eye = _eye(q[0].shape[0])
    m_off = [m - eye for m in _inv_newton(_gdn_a(k, dcb, bb128))]
    return _gdn_rest(q, k, v, dcb, dcb128, dlb128, bb128, m_off) + (m_off,)


def _gated_norm(o, z, nw):
    r = lax.rsqrt(jnp.mean(o * o, axis=-1, keepdims=True) + NORM_EPS)
    return o * r * nw * _silu(z)


def _gated_norm_bwd(dy, o, z, nw):
    r = lax.rsqrt(jnp.mean(o * o, axis=-1, keepdims=True) + NORM_EPS)
    xhat = o * r
    sz = _silu(z)
    dxhat = dy * nw * sz
    do = r * (dxhat - xhat * jnp.mean(dxhat * xhat, axis=-1, keepdims=True))
    dz = dy * xhat * nw * _silu_grad(z)
    dnw = jnp.sum(dy * xhat * sz, axis=0, keepdims=True)
    return do, dz, dnw


GDN_HEAD_GROUP = 8
GC_BETA, GC_G, GC_DECAY, GC_LAST =0, GDN_HEADS, 2 * GDN_HEADS, 3 * GDN_HEADS


def _gate_cols(gc, h, rows):
    lane = _iota2((rows, LANE), 1)
    col = lambda off: jnp.sum(jnp.where(lane == off + h, gc, 0.0), axis=-1, keepdims=True)
    return col(GC_BETA), col(GC_DECAY), col(GC_LAST)


def _gdn_local_args(q_ref, k_ref, v_ref, gc_ref, hp, n):
    gc = gc_ref[...]
    lanes = [slice(hh * LANE, (hh + 1) * LANE) for hh in range(hp)]
    cols = [_gate_cols(gc, pl.program_id(0) * hp + hh, n) for hh in range(hp)]
    bcast = lambda i, w: [jnp.broadcast_to(c[i], (n, w)) for c in cols]
    args = ([q_ref[:, ls] for ls in lanes], [k_ref[:, ls] for ls in lanes], [v_ref[:, ls] for ls in lanes],
            bcast(1, n), bcast(1, LANE), bcast(2, LANE), bcast(0, LANE))
    return args, [c[2] for c in cols], lanes


def gdn_chunk_fwd(qkv, gc, proj, norm_w):
    t = qkv.shape[0]
    c = GDN_CHUNK
    tc = _tile(t, 256)
    nsub = tc // c

    hp = GDN_HEAD_GROUP

    def body(q_ref, k_ref, v_ref, gc_ref, z_ref, nw_ref, on_ref, st_ref, m_ref, s_ref):
        @pl.when(pl.program_id(1) == 0)
        def _():
            s_ref[...] = jnp.zeros_like(s_ref)

        args, dl, lanes = _gdn_local_args(q_ref, k_ref, v_ref, gc_ref, hp, tc)
        sol, aqk, q_dec, k_dec, m_off = _gdn_local(*args)
        for hh in range(hp):
            m_ref[hh] = m_off[hh]
        u = [x[:, :GDN_DK] for x in sol]
        w = [x[:, GDN_DK:] for x in sol]
        s = [s_ref[hh] for hh in range(hp)]
        v_new, o_state = [[] for _ in range(hp)], [[] for _ in range(hp)]
        for j in range(nsub):
            sl = slice(j * c, (j + 1) * c)
            for hh in range(hp):
                st_ref[hh, j] = s[hh]
            vn = _each(lambda uu, ww, ss: uu[sl] - _bdot(ww[sl], ss), u, w, s)
            os_ = _each(lambda qq, ss: _bdot(qq[sl], ss), q_dec, s)
            s = _each(lambda ss, d, kk, vv: ss * jnp.exp(d[j * c:j * c + 1]) + _bdot_tn(kk[sl], vv), s, dl, k_dec, vn)
            for hh in range(hp):
                v_new[hh].append(vn[hh])
                o_state[hh].append(os_[hh])
        for hh in range(hp):
            s_ref[hh] = s[hh]
        cat = lambda xs: jnp.concatenate(xs, axis=0)
        o = _each(lambda os_, aa, vv: cat(os_) + _bdot(aa, cat(vv)), o_state, aqk, v_new)
        for hh, ls in enumerate(lanes):
            on_ref[:, ls] = _gated_norm(o[hh], z_ref[:, ls], nw_ref[...]).astype(BF16)

    col = lambda off: pl.BlockSpec((tc, hp * LANE), lambda g, i: (i, off // hp + g))
    return pl.pallas_call(
        body, name="gdn_chunk_fwd",
        grid=(GDN_HEADS // hp, t // tc),
        in_specs=[col(0), col(GDN_HEADS), col(2 * GDN_HEADS),
                  pl.BlockSpec((tc, LANE), lambda g, i: (i, 0)),
                  col(GDN_QKV_W // LANE),
                  pl.BlockSpec((1, LANE), lambda g, i: (0, 0))],
        out_specs=[col(0),
                   pl.BlockSpec((hp, nsub, GDN_DK, GDN_DK), lambda g, i: (g, i, 0, 0)),
                   pl.BlockSpec((hp, tc, tc), lambda g, i: (g, i, 0))],
        out_shape=[jax.ShapeDtypeStruct((t, GDN_HEADS * GDN_DK), BF16),
                   jax.ShapeDtypeStruct((GDN_HEADS, t // c, GDN_DK, GDN_DK), F32),
                   jax.ShapeDtypeStruct((GDN_HEADS, t, tc), F32)],
        scratch_shapes=[pltpu.VMEM((hp, GDN_DK, GDN_DK), F32)],
        compiler_params=_arb(2),
    )(qkv, qkv, qkv, gc, proj, norm_w)


def gdn_chunk_bwd(qkv, gc, proj, norm_w, states, m_offs, d_on):
    t = qkv.shape[0]
    c = GDN_CHUNK
    tc = _tile(t, 256)
    nsub = tc // c
    nblk = t // tc
    hp = GDN_HEAD_GROUP

    def body(q_ref, k_ref, v_ref, gc_ref, z_ref, nw_ref, st_ref, m_ref, don_ref,
             dq_ref, dk_ref, dv_ref, dz_ref, dgc_ref, dnw_ref, ds_ref):
        @pl.when(pl.program_id(1) == 0)
        def _():
            ds_ref[...] = jnp.zeros_like(ds_ref)

        @pl.when((pl.program_id(1) == 0) & (pl.program_id(0) == 0))
        def _():
            dnw_ref[...] = jnp.zeros_like(dnw_ref)

        rows = [slice(j * c, (j + 1) * c) for j in range(nsub)]
        cat = lambda xs: jnp.concatenate(xs, axis=0)
        lsum = lambda x: jnp.sum(x, axis=-1, keepdims=True)
        lane = _iota2((tc, LANE), 1)
        row = _iota2((tc, 1), 0)
        heads = range(hp)
        args, dl, lanes = _gdn_local_args(q_ref, k_ref, v_ref, gc_ref, hp, tc)
        m_off = [m_ref[hh] for hh in heads]
        (sol, aqk, q_dec, k_dec), vjp = jax.vjp(_gdn_rest, *args, m_off)
        u = [x[:, :GDN_DK] for x in sol]
        w = [x[:, GDN_DK:] for x in sol]
        states = [[st_ref[hh, j] for j in range(nsub)] for hh in heads]
        v_new = _each(lambda uu, ww, st: [uu[sl] - _bdot(ww[sl], s) for sl, s in zip(rows, st)], u, w, states)
        v_all = [cat(v) for v in v_new]
        o = _each(lambda qq, st, aa, vv: cat([_bdot(qq[sl], s) for sl, s in zip(rows, st)]) + _bdot(aa, vv),
                  q_dec, states, aqk, v_all)
        nw = nw_ref[...]
        gn = [_gated_norm_bwd(don_ref[:, ls], o[hh], z_ref[:, ls], nw) for hh, ls in enumerate(lanes)]
        do = [x[0] for x in gn]
        for hh, ls in enumerate(lanes):
            dnw_ref[...] += gn[hh][2]
            dz_ref[:, ls] = gn[hh][1].astype(BF16)
        d_aqk = _each(_bdot_nt, do, v_all)
        dv_o = _each(_bdot_tn, aqk, do)
        ds = [ds_ref[hh] for hh in heads]
        d_u, d_w, d_qdec, d_kdec, d_last = ([[None] * nsub for _ in heads] for _ in range(5))
        for j in reversed(range(nsub)):
            sl = rows[j]
            sj = [states[hh][j] for hh in heads]
            cd = [jnp.exp(d[j * c:j * c + 1]) for d in dl]
            du = _each(lambda dvo, kk, dd: dvo[sl] + _bdot(kk[sl], dd), dv_o, k_dec, ds)
            dqd = _each(lambda dd, s: _bdot_nt(dd[sl], s), do, sj)
            dkd = _each(lambda vv, dd: _bdot_nt(vv[j], dd), v_new, ds)
            dla = _each(lambda s, dd, cc: jnp.sum(lsum(s * dd), axis=0, keepdims=True) * cc, sj, ds, cd)
            dw = _each(lambda x, s: -_bdot_nt(x, s), du, sj)
            ds = _each(lambda qq, dd, cc, dsn, ww, x: _bdot_tn(qq[sl], dd[sl]) + cc * dsn - _bdot_tn(ww[sl], x),
                       q_dec, do, cd, ds, w, du)
            for hh in heads:
                d_u[hh][j], d_w[hh][j], d_qdec[hh][j], d_kdec[hh][j], d_last[hh][j] = du[hh], dw[hh], dqd[hh], dkd[hh], dla[hh]
        for hh in heads:
            ds_ref[hh] = ds[hh]
        d_sol = _each(lambda a, b: jnp.concatenate([cat(a), cat(b)], axis=1), d_u, d_w)
        dq, dk, dv, d_dcb, d_dcb128, d_dlb128, d_bb, d_m = vjp(
            (d_sol, d_aqk, [cat(x) for x in d_qdec], [cat(x) for x in d_kdec]))
        eye = _eye(tc)
        d_a = _inv_cotangent([m + eye for m in m_off], d_m)
        k_args = (args[1], args[3], args[6])
        dk_a, d_dcb_a, d_bb_a = jax.vjp(_gdn_a, *k_args)[1](d_a)
        add = lambda xs, ys: _each(lambda x, y: x + y, xs, ys)
        dk, d_dcb, d_bb = add(dk, dk_a), add(d_dcb, d_dcb_a), add(d_bb, d_bb_a)
        dgc = jnp.zeros((tc, LANE), F32)
        for hh, ls in enumerate(lanes):
            h = pl.program_id(0) * hp + hh
            dq_ref[:, ls] = dq[hh]
            dk_ref[:, ls] = dk[hh]
            dv_ref[:, ls] = dv[hh]
            d_dl = lsum(d_dlb128[hh])
            for j in range(nsub):
                d_dl = d_dl + jnp.where(row == j * c, d_last[hh][j], 0.0)
            dgc = dgc + jnp.where(lane == GC_BETA + h, lsum(d_bb[hh]),
                                  jnp.where(lane == GC_DECAY + h, lsum(d_dcb[hh]) + lsum(d_dcb128[hh]),
                                            jnp.where(lane == GC_LAST + h, d_dl, 0.0)))
        dgc_ref[0] = dgc

    rev = lambda i: nblk - 1 - i
    col = lambda off: pl.BlockSpec((tc, hp * LANE), lambda g, i: (rev(i), off // hp + g))
    return pl.pallas_call(
        body, name="gdn_chunk_bwd",
        grid=(GDN_HEADS // hp, nblk),
        in_specs=[col(0), col(GDN_HEADS), col(2 * GDN_HEADS),
                  pl.BlockSpec((tc, LANE), lambda g, i: (rev(i), 0)),
                  col(GDN_QKV_W // LANE),
                  pl.BlockSpec((1, LANE), lambda g, i: (0, 0)),
                  pl.BlockSpec((hp, nsub, GDN_DK, GDN_DK), lambda g, i: (g, rev(i), 0, 0)),
                  pl.BlockSpec((hp, tc, tc), lambda g, i: (g, rev(i), 0)),
                  col(0)],
        out_specs=[col(0), col(0), col(0), col(0),
                   pl.BlockSpec((1, tc, LANE), lambda g, i: (g, rev(i), 0)),
                   pl.BlockSpec((1, LANE), lambda g, i: (0, 0))],
        out_shape=[
            jax.ShapeDtypeStruct((t, GDN_HEADS * GDN_DK), F32),
            jax.ShapeDtypeStruct((t, GDN_HEADS * GDN_DK), F32),
            jax.ShapeDtypeStruct((t, GDN_HEADS * GDN_DK), F32),
            jax.ShapeDtypeStruct((t, GDN_HEADS * GDN_DK), BF16),
            jax.ShapeDtypeStruct((GDN_HEADS // hp, t, LANE), F32),
            jax.ShapeDtypeStruct((1, LANE), F32)],
        scratch_shapes=[pltpu.VMEM((hp, GDN_DK, GDN_DK), F32)],
        compiler_params=_arb(2),
    )(qkv, qkv, qkv, gc, proj, norm_w, states, m_offs, d_on)


GDN_PROJ_W = GDN_QKV_W + 1024 + LANE
GDN_Q_SCALE = GDN_DK ** -0.5


def _shift_rows(ext, shift, lo, n):
    if shift == 0:
        return ext[lo:lo + n]
    return pltpu.roll(ext, shift, 0)[lo:lo + n]


def _conv_fwd(x, halo, w):
    n = x.shape[0]
    ext = jnp.concatenate([halo, x], axis=0)
    y = w[GDN_CONV - 1:GDN_CONV] * x
    for j in range(GDN_CONV - 1):
        y = y + w[j:j + 1] * _shift_rows(ext, GDN_CONV - 1 - j, 8, n)
    return y


def _chunk_masks(n):
    r, s = _iota2((n, n), 0), _iota2((n, n), 1)
    same = (r // GDN_CHUNK) == (s // GDN_CHUNK)
    return (same & (r >= s)).astype(F32), (same & (r <= s)).astype(F32), same.astype(F32)


def _softplus(x):
    return jnp.maximum(x, 0.0) + jnp.log(1.0 + jnp.exp(-jnp.abs(x)))


def _l2n(t):
    rs = lax.rsqrt(jnp.sum(t * t, axis=-1, keepdims=True) + NORM_EPS)
    return t * rs, rs


def gdn_pre_fwd(proj, conv_w, gate_par):
    t = proj.shape[0]
    tm = _tile(t, 256)

    def body(x_ref, halo_ref, ba_ref, w_ref, gp_ref, qkv_ref, gc_ref):
        i = pl.program_id(0)
        halo = jnp.where(i > 0, halo_ref[...], 0.0)
        y = _silu(_conv_fwd(x_ref[...], halo, w_ref[...]))
        for hh in range(2 * GDN_HEADS):
            sl = slice(hh * LANE, (hh + 1) * LANE)
            tn, _ = _l2n(y[:, sl])
            qkv_ref[:, sl] = tn * GDN_Q_SCALE if hh < GDN_HEADS else tn
        qkv_ref[:, 2 * GDN_HEADS * LANE:] = y[:, 2 * GDN_HEADS * LANE:]
        ba = ba_ref[...]
        lane = _iota2(ba.shape, 1)
        gp = gp_ref[...]
        is_a = (lane >= GC_G) & (lane < GC_G + GDN_HEADS)
        g = jnp.where(is_a, -jnp.exp(gp[0:1]) * _softplus(ba + gp[1:2]), 0.0)
        tri, _, same = _chunk_masks(tm)
        decay = pltpu.roll(_dot(tri, g, HI), GC_DECAY - GC_G, 1)
        last = pltpu.roll(_dot(same, g, HI), GC_LAST - GC_G, 1)
        gc_ref[...] = jnp.where(lane < GDN_HEADS, _sigmoid(ba), g) + decay + last

    return pl.pallas_call(
        body, name="gdn_pre_fwd",
        grid=(t // tm,),
        in_specs=[pl.BlockSpec((tm, GDN_QKV_W), lambda i: (i, 0)),
                  pl.BlockSpec((8, GDN_QKV_W), lambda i: (jnp.maximum(i * (tm // 8) - 1, 0), 0)),
                  pl.BlockSpec((tm, LANE), lambda i: (i, (GDN_QKV_W + 1024) // LANE)),
                  pl.BlockSpec((8, GDN_QKV_W), lambda i: (0, 0)),
                  pl.BlockSpec((8, LANE), lambda i: (0, 0))],
        out_specs=[pl.BlockSpec((tm, GDN_QKV_W), lambda i: (i, 0)),
                   pl.BlockSpec((tm, LANE), lambda i: (i, 0))],
        out_shape=[jax.ShapeDtypeStruct((t, GDN_QKV_W), F32), jax.ShapeDtypeStruct((t, LANE), F32)],
        compiler_params=_arb(1),
    )(proj, proj, proj, conv_w, gate_par)


def gdn_pre_bwd(proj, conv_w, gate_par, dq, dk, dv, dgc_heads):
    t = proj.shape[0]
    tm = _tile(t, 256)

    def body(x_ref, halo_ref, ba_ref, w_ref, gp_ref, dq_ref, dk_ref, dv_ref, dgc_ref, dy_ref, dba_ref, dgp_ref):
        i = pl.program_id(0)

        @pl.when(i == 0)
        def _():
            dgp_ref[...] = jnp.zeros_like(dgp_ref)

        halo = jnp.where(i > 0, halo_ref[...], 0.0)
        y = _conv_fwd(x_ref[...], halo, w_ref[...])
        for hh in range(3 * GDN_HEADS):
            sl = slice(hh * LANE, (hh + 1) * LANE)
            hsl = slice((hh % GDN_HEADS) * LANE, (hh % GDN_HEADS + 1) * LANE)
            yy = y[:, sl]
            if hh < 2 * GDN_HEADS:
                tn, rs = _l2n(_silu(yy))
                dtn = dq_ref[:, hsl] * GDN_Q_SCALE if hh < GDN_HEADS else dk_ref[:, hsl]
                dsil = rs * (dtn - tn * jnp.sum(dtn * tn, axis=-1, keepdims=True))
            else:
                dsil = dv_ref[:, hsl]
            dy_ref[:, sl] = dsil * _silu_grad(yy)
        dgc = dgc_ref[0]
        for hh in range(1, dgc_heads.shape[0]):
            dgc = dgc + dgc_ref[hh]
        ba = ba_ref[...]
        lane = _iota2(ba.shape, 1)
        _, tri_t, same = _chunk_masks(tm)
        d_decay = jnp.where((lane >= GC_DECAY) & (lane < GC_DECAY + GDN_HEADS), dgc, 0.0)
        d_last = jnp.where((lane >= GC_LAST) & (lane < GC_LAST + GDN_HEADS), dgc, 0.0)
        dgc = (jnp.where(lane < GDN_HEADS, dgc, 0.0) + pltpu.roll(_dot(tri_t, d_decay, HI), LANE - (GC_DECAY - GC_G), 1)
               + pltpu.roll(_dot(same, d_last, HI), LANE - (GC_LAST - GC_G), 1))
        gp = gp_ref[...]
        xg = ba + gp[1:2]
        ea = jnp.exp(gp[0:1])
        sp = _softplus(xg)
        sb = _sigmoid(ba)
        is_b = lane < GDN_HEADS
        is_a = (lane >= GDN_HEADS) & (lane < 2 * GDN_HEADS)
        d_pre = jnp.where(is_a, dgc * (-ea) * _sigmoid(xg), 0.0)
        dba_ref[...] = jnp.where(is_b, dgc * sb * (1.0 - sb), d_pre).astype(BF16)
        d_alog = jnp.sum(jnp.where(is_a, dgc * (-ea) * sp, 0.0), axis=0, keepdims=True)
        d_dtb = jnp.sum(d_pre, axis=0, keepdims=True)
        row = _iota2((8, LANE), 0)
        dgp_ref[...] += jnp.where(row == 0, d_alog, jnp.where(row == 1, d_dtb, 0.0))

    hspec = pl.BlockSpec((tm, GDN_HEADS * LANE), lambda i: (i, 0))
    return pl.pallas_call(
        body, name="gdn_pre_bwd",
        grid=(t // tm,),
        in_specs=[pl.BlockSpec((tm, GDN_QKV_W), lambda i: (i, 0)),
                  pl.BlockSpec((8, GDN_QKV_W), lambda i: (jnp.maximum(i * (tm // 8) - 1, 0), 0)),
                  pl.BlockSpec((tm, LANE), lambda i: (i, (GDN_QKV_W + 1024) // LANE)),
                  pl.BlockSpec((8, GDN_QKV_W), lambda i: (0, 0)),
                  pl.BlockSpec((8, LANE), lambda i: (0, 0)),
                  hspec, hspec, hspec,
                  pl.BlockSpec((dgc_heads.shape[0], tm, LANE), lambda i: (0, i, 0))],
        out_specs=[pl.BlockSpec((tm, GDN_QKV_W), lambda i: (i, 0)),
                   pl.BlockSpec((tm, LANE), lambda i: (i, 0)),
                   pl.BlockSpec((8, LANE), lambda i: (0, 0))],
        out_shape=[jax.ShapeDtypeStruct((t, GDN_QKV_W), F32), jax.ShapeDtypeStruct((t, LANE), BF16),
                   jax.ShapeDtypeStruct((8, LANE), F32)],
        compiler_params=_arb(1),
    )(proj, proj, proj, conv_w, gate_par, dq, dk, dv, dgc_heads)


def gdn_conv_bwd(proj, conv_w, dy):
    t = proj.shape[0]
    tm = _tile(t, 256)
    nblk = t // tm

    def body(x_ref, halo_ref, w_ref, dy_ref, dyn_ref, dx_ref, dw_ref):
        i = pl.program_id(0)

        @pl.when(i == 0)
        def _():
            dw_ref[...] = jnp.zeros_like(dw_ref)

        w = w_ref[...]
        dy = dy_ref[...]
        ext_dy = jnp.concatenate([dy, jnp.where(i < nblk - 1, dyn_ref[...], 0.0)], axis=0)
        ext_x = jnp.concatenate([jnp.where(i > 0, halo_ref[...], 0.0), x_ref[...]], axis=0)
        dx = w[GDN_CONV - 1:GDN_CONV] * dy
        rows = [jnp.sum(dy * x_ref[...], axis=0, keepdims=True)]
        for j in range(GDN_CONV - 1):
            sh = GDN_CONV - 1 - j
            dx = dx + w[j:j + 1] * _shift_rows(ext_dy, tm + 8 - sh, 0, tm)
            rows.insert(j, jnp.sum(dy * _shift_rows(ext_x, sh, 8, tm), axis=0, keepdims=True))
        dx_ref[...] = dx.astype(BF16)
        row = _iota2((8, GDN_QKV_W), 0)
        acc = jnp.zeros((8, GDN_QKV_W), F32)
        for j in range(GDN_CONV):
            acc = acc + jnp.where(row == j, rows[j], 0.0)
        dw_ref[...] += acc

    return pl.pallas_call(
        body, name="gdn_conv_bwd",
        grid=(nblk,),
        in_specs=[pl.BlockSpec((tm, GDN_QKV_W), lambda i: (i, 0)),
                  pl.BlockSpec((8, GDN_QKV_W), lambda i: (jnp.maximum(i * (tm // 8) - 1, 0), 0)),
                  pl.BlockSpec((8, GDN_QKV_W), lambda i: (0, 0)),
                  pl.BlockSpec((tm, GDN_QKV_W), lambda i: (i, 0)),
                  pl.BlockSpec((8, GDN_QKV_W), lambda i: (jnp.minimum((i + 1) * (tm // 8), t // 8 - 1), 0))],
        out_specs=[pl.BlockSpec((tm, GDN_QKV_W), lambda i: (i, 0)),
                   pl.BlockSpec((8, GDN_QKV_W), lambda i: (0, 0))],
        out_shape=[jax.ShapeDtypeStruct((t, GDN_QKV_W), BF16), jax.ShapeDtypeStruct((8, GDN_QKV_W), F32)],
        compiler_params=_arb(1),
    )(proj, proj, conv_w, dy, dy)


def _resident(w_hbm, w_vmem, sem):
    @pl.when(pl.program_id(0) == 0)
    def _():
        cp = pltpu.make_async_copy(w_hbm, w_vmem, sem)
        cp.start()
        cp.wait()


ANY = pl.BlockSpec(memory_space=pl.ANY)


def norm_proj(h, nw, w, bias):
    t, d = h.shape
    n = w.shape[1]
    tm = _tile(t, 256)
    nc = _tile(n, 1536) if n % 1536 == 0 else _tile(n, 1408)

    def body(h_ref, nw_ref, w_hbm, b_ref, o_ref, w_ref, sem):
        _resident(w_hbm, w_ref, sem)
        hn = _rms(h_ref[...], nw_ref[...])[0].astype(BF16)
        for c0 in range(0, n, nc):
            o_ref[:, c0:c0 + nc] = _dot(hn, w_ref[:, c0:c0 + nc]) + b_ref[:, c0:c0 + nc]

    return pl.pallas_call(
        body, name="norm_proj",
        grid=(t // tm,),
        in_specs=[pl.BlockSpec((tm, d), lambda i: (i, 0)), pl.BlockSpec((1, d), lambda i: (0, 0)), ANY,
                  pl.BlockSpec((1, n), lambda i: (0, 0))],
        out_specs=pl.BlockSpec((tm, n), lambda i: (i, 0)),
        out_shape=jax.ShapeDtypeStruct((t, n), F32),
        scratch_shapes=[pltpu.VMEM((d, n), BF16), pltpu.SemaphoreType.DMA],
        compiler_params=_arb(1),
    )(h, nw, w, bias)


def linear_residual(h, x, w, bias):
    t, d = h.shape
    k = x.shape[1]
    tm = _tile(t, 512)

    def body(h_ref, x_ref, w_hbm, b_ref, o_ref, w_ref, sem):
        _resident(w_hbm, w_ref, sem)
        o_ref[...] = h_ref[...] + _dot(x_ref[...], w_ref[...]) + b_ref[...]

    return pl.pallas_call(
        body, name="linear_residual",
        grid=(t // tm,),
        in_specs=[pl.BlockSpec((tm, d), lambda i: (i, 0)), pl.BlockSpec((tm, k), lambda i: (i, 0)), ANY,
                  pl.BlockSpec((1, d), lambda i: (0, 0))],
        out_specs=pl.BlockSpec((tm, d), lambda i: (i, 0)),
        out_shape=jax.ShapeDtypeStruct((t, d), F32),
        scratch_shapes=[pltpu.VMEM((k, d), BF16), pltpu.SemaphoreType.DMA],
        compiler_params=_arb(1),
    )(h, x, w, bias)


def matmul_nt(dy, w):
    t, d = dy.shape
    k = w.shape[0]
    tm = _tile(t, 512)

    def body(dy_ref, w_hbm, o_ref, cs_ref, w_ref, sem):
        _resident(w_hbm, w_ref, sem)

        @pl.when(pl.program_id(0) == 0)
        def _():
            cs_ref[...] = jnp.zeros_like(cs_ref)

        dy = dy_ref[...]
        cs_ref[...] += jnp.sum(dy, axis=0, keepdims=True)
        o_ref[...] = _dot_nt(dy.astype(BF16), w_ref[...])

    return pl.pallas_call(
        body, name="matmul_nt",
        grid=(t // tm,),
        in_specs=[pl.BlockSpec((tm, d), lambda i: (i, 0)), ANY],
        out_specs=[pl.BlockSpec((tm, k), lambda i: (i, 0)), pl.BlockSpec((1, d), lambda i: (0, 0))],
        out_shape=[jax.ShapeDtypeStruct((t, k), F32), jax.ShapeDtypeStruct((1, d), F32)],
        scratch_shapes=[pltpu.VMEM((k, d), BF16), pltpu.SemaphoreType.DMA],
        compiler_params=_arb(1),
    )(dy, w)


def norm_proj_bwd(h, nw, dh, dps, ws):
    t, d = h.shape
    np_ = len(dps)
    ns = [w.shape[1] for w in ws]
    tm = _tile(t, 256)

    def body(*refs):
        h_ref, nw_ref, dh_ref = refs[:3]
        dp_refs = refs[3:3 + np_]
        w_hbms = refs[3 + np_:3 + 2 * np_]
        o_ref, hn_ref, dnw_ref = refs[3 + 2 * np_:6 + 2 * np_]
        cs_refs = refs[6 + 2 * np_:6 + 3 * np_]
        w_refs = refs[6 + 3 * np_:6 + 4 * np_]
        sem = refs[6 + 4 * np_]
        for a, b in zip(w_hbms, w_refs):
            _resident(a, b, sem)

        @pl.when(pl.program_id(0) == 0)
        def _():
            dnw_ref[...] = jnp.zeros_like(dnw_ref)
            for c in cs_refs:
                c[...] = jnp.zeros_like(c)

        nw = nw_ref[...]
        hn, xhat, r = _rms(h_ref[...], nw)
        hn_ref[...] = hn.astype(BF16)
        dhn = jnp.zeros((tm, d), F32)
        for dp_ref, w_ref, cs_ref in zip(dp_refs, w_refs, cs_refs):
            dp = dp_ref[...]
            cs_ref[...] += jnp.sum(dp.astype(F32), axis=0, keepdims=True)
            dhn = dhn + _dot_nt(dp, w_ref[...])
        dx, dnw = _rms_bwd(dhn, nw, xhat, r)
        dnw_ref[...] += dnw
        o_ref[...] = dh_ref[...] + dx

    row = pl.BlockSpec((tm, d), lambda i: (i, 0))
    vec = pl.BlockSpec((1, d), lambda i: (0, 0))
    return pl.pallas_call(
        body, name="norm_proj_bwd",
        grid=(t // tm,),
        in_specs=[row, vec, row] + [pl.BlockSpec((tm, n), lambda i: (i, 0)) for n in ns] + [ANY] * np_,
        out_specs=[row, row, vec] + [pl.BlockSpec((1, n), lambda i: (0, 0)) for n in ns],
        out_shape=[jax.ShapeDtypeStruct((t, d), F32), jax.ShapeDtypeStruct((t, d), BF16),
                   jax.ShapeDtypeStruct((1, d), F32)] + [jax.ShapeDtypeStruct((1, n), F32) for n in ns],
        scratch_shapes=[pltpu.VMEM((d, n), BF16) for n in ns] + [pltpu.SemaphoreType.DMA],
        compiler_params=_arb(1),
    )(h, nw, dh, *dps, *ws)


def _pair_copies(g_refs, got_refs, send_sems, recv_sems):
    mx, my, mc = _me()
    copies = []
    for g_ref, got_ref in zip(g_refs, got_refs):
        for j in range(4):
            i = len(copies)
            copies.append(pltpu.make_async_remote_copy(
                src_ref=g_ref.at[j, 1 - mc], dst_ref=got_ref.at[j], send_sem=send_sems.at[i], recv_sem=recv_sems.at[i],
                device_id=(mx, my, 1 - mc), device_id_type=MESH))
    return copies


def _scatter_copies(p_refs, q_refs, send_sems, recv_sems):
    mx, my, mc = _me()
    copies = []
    for p_ref, q_ref in zip(p_refs, q_refs):
        for k in (1, 2, 3):
            px, py = _flip(mx, k >> 1), _flip(my, k & 1)
            i = len(copies)
            copies.append(pltpu.make_async_remote_copy(
                src_ref=p_ref.at[2 * px + py], dst_ref=q_ref.at[2 * mx + my], send_sem=send_sems.at[i],
                recv_sem=recv_sems.at[i], device_id=(px, py, mc), device_id_type=MESH))
    return copies


def _share_copies(o_refs, out_refs, send_sems, recv_sems):
    mx, my, mc = _me()
    return [pltpu.make_async_remote_copy(src_ref=o_ref, dst_ref=out_ref, send_sem=send_sems.at[i], recv_sem=recv_sems.at[i],
                                         device_id=(mx, my, 1 - mc), device_id_type=MESH)
            for i, (o_ref, out_ref) in enumerate(zip(o_refs, out_refs))]


CARRIED = {"pair": (_pair_copies, 4, lambda a: a.shape[:1] + a.shape[2:]), "scatter": (_scatter_copies, 3, lambda a: a.shape),
           "share": (_share_copies, 1, lambda a: a.shape)}


def matmul_tn(x, y, scale=1.0, carry=None):
    t, k = x.shape
    n = y.shape[1]
    tk = _tile(k, 1024) if k % 1024 == 0 else _tile(k, 1408)
    tn = n if n <= 1536 else (1024 if n % 1024 == 0 else 1408)
    assert n % tn == 0
    tt = _tile(t, 2048)
    grid = (k // tk, n // tn, t // tt)
    arrays = [] if carry is None else list(carry[1])
    na = len(arrays)

    def body(x_ref, y_ref, *rest):
        o_ref = rest[na]
        if carry is not None:
            make, _, _ = CARRIED[carry[0]]
            copies = lambda: make(rest[:na], rest[na + 1:2 * na + 1], rest[2 * na + 1], rest[2 * na + 2])
            pid = [pl.program_id(a) for a in range(3)]

            @pl.when((pid[0] == 0) & (pid[1] == 0) & (pid[2] == 0))
            def _():
                for cp in copies():
                    cp.start()

        @pl.when(pl.program_id(2) == 0)
        def _():
            o_ref[...] = jnp.zeros_like(o_ref)

        yv = y_ref[...]
        if scale != 1.0:
            yv = yv * scale
        o_ref[...] += _dot_tn(x_ref[...].astype(BF16), yv.astype(BF16))

        if carry is not None:
            @pl.when((pid[0] == grid[0] - 1) & (pid[1] == grid[1] - 1) & (pid[2] == grid[2] - 1))
            def _():
                for cp in copies():
                    cp.wait()

    out_shape = [jax.ShapeDtypeStruct((k, n), F32)]
    scratch = []
    if carry is not None:
        _, per_array, out_of = CARRIED[carry[0]]
        out_shape += [jax.ShapeDtypeStruct(out_of(a), a.dtype) for a in arrays]
        scratch = [pltpu.SemaphoreType.DMA((per_array * na,))] * 2
    res = pl.pallas_call(
        body, name="matmul_tn" if carry is None else "matmul_tn_" + carry[0],
        grid=grid,
        in_specs=[pl.BlockSpec((tt, tk), lambda i, j, s: (s, i)), pl.BlockSpec((tt, tn), lambda i, j, s: (s, j))] + [ANY] * na,
        out_specs=[pl.BlockSpec((tk, tn), lambda i, j, s: (i, j))] + [ANY] * na,
        out_shape=out_shape, scratch_shapes=scratch,
        compiler_params=_arb(3),
    )(x, y, *arrays)
    return res[0] if carry is None else (res[0], list(res[1:]))


FFN_CHUNKS = 2


class _FfnGather:
    def __init__(self, gu_sh, d_sh, gu_full, d_full, send_sems, recv_sems, local_sems):
        self.sh, self.full = (gu_sh, d_sh), (gu_full, d_full)
        self.send_sems, self.recv_sems, self.local_sems = send_sems, recv_sems, local_sems
        self.mx, self.my, self.mc = _me()
        self.me = 2 * self.mx + self.my
        self.w4, self.f4 = gu_sh.shape[1], d_sh.shape[0]
        self.hg, self.hd = gu_sh.shape[0] // 2, d_sh.shape[0] // 2

    def _window(self, a, chip, core=None):
        gu_full, d_full = self.full
        if a == 0:
            rows = pl.ds(0, 2 * self.hg) if core is None else pl.ds(pl.multiple_of(core * self.hg, 16), self.hg)
            return gu_full.at[rows, pl.ds(pl.multiple_of(chip * self.w4, LANE), self.w4)]
        if core is None:
            return d_full.at[pl.ds(pl.multiple_of(chip * self.f4, 16), self.f4), :]
        return d_full.at[pl.ds(pl.multiple_of(chip * self.f4 + core * self.hd, 16), self.hd), :]

    def _half(self, a):
        n = (self.hg, self.hd)[a]
        return self.sh[a].at[pl.ds(pl.multiple_of(self.mc * n, 16), n), :]

    def _peer(self, k):
        return _flip(self.mx, k >> 1), _flip(self.my, k & 1)

    def _fetch(self, k, a):
        px, py = self._peer(k)
        i = 2 * (k - 1) + a
        return pltpu.make_async_remote_copy(src_ref=self._half(a), dst_ref=self._window(a, self.me, self.mc),
                                            send_sem=self.send_sems.at[i], recv_sem=self.recv_sems.at[i],
                                            device_id=(px, py, self.mc), device_id_type=MESH)

    def _relay(self, k, a):
        px, py = self._peer(k)
        got = self._window(a, 2 * px + py, self.mc)
        i = 6 + 2 * (k - 1) + a
        return pltpu.make_async_remote_copy(src_ref=got, dst_ref=got, send_sem=self.send_sems.at[i],
                                            recv_sem=self.recv_sems.at[i], device_id=(self.mx, self.my, 1 - self.mc),
                                            device_id_type=MESH)

    def _own(self, a):
        return pltpu.make_async_copy(self.sh[a], self._window(a, self.me), self.local_sems.at[a])

    def start(self):
        for a in (0, 1):
            self._own(a).start()
            for k in (1, 2, 3):
                self._fetch(k, a).start()

    def relay(self):
        for k in (1, 2, 3):
            for a in (0, 1):
                self._fetch(k, a).wait_recv()
                self._relay(k, a).start()

    def finish(self):
        for k in (1, 2, 3):
            for a in (0, 1):
                self._fetch(k, a).wait_send()
                self._relay(k, a).wait()
        for a in (0, 1):
            self._own(a).wait()


def ffn_fwd(h, nw, wgu, wd, nxt=None):
    t, d = h.shape
    f = wd.shape[0]
    fc = f // FFN_CHUNKS
    tm = _tile(t, 512)
    nsteps = t // tm

    def body(h_ref, nw_ref, wgu_hbm, wd_hbm, *rest):
        if nxt is None:
            o_ref, gu_ref, a_ref, wgu_ref, wd_ref, sem = rest
            gather = None
        else:
            (gu_sh, d_sh, o_ref, gu_ref, a_ref, gu_full, d_full, wgu_ref, wd_ref, sem,
             send_sems, recv_sems, local_sems) = rest
            gather = _FfnGather(gu_sh, d_sh, gu_full, d_full, send_sems, recv_sems, local_sems)
            pl.when(pl.program_id(0) == 0)(gather.start)
        _resident(wgu_hbm, wgu_ref, sem)
        _resident(wd_hbm, wd_ref, sem)
        if gather is not None:
            pl.when(pl.program_id(0) == nsteps // 2)(gather.relay)
        x = h_ref[...]
        hn = _rms(x, nw_ref[...])[0].astype(BF16)
        acc = jnp.zeros((tm, d), F32)
        for c in range(FFN_CHUNKS):
            gs, us = slice(c * fc, (c + 1) * fc), slice(f + c * fc, f + (c + 1) * fc)
            g = _dot(hn, wgu_ref[:, gs])
            u = _dot(hn, wgu_ref[:, us])
            a = (_silu(g) * u).astype(BF16)
            gu_ref[:, gs] = g.astype(BF16)
            gu_ref[:, us] = u.astype(BF16)
            a_ref[:, gs] = a
            acc = acc + _dot(a, wd_ref[gs, :])
        o_ref[...] = x + 0.5 * acc
        if gather is not None:
            pl.when(pl.program_id(0) == nsteps - 1)(gather.finish)

    row = lambda w: pl.BlockSpec((tm, w), lambda i: (i, 0))
    in_specs = [row(d), pl.BlockSpec((1, d), lambda i: (0, 0)), ANY, ANY]
    out_specs = [row(d), row(2 * f), row(f)]
    out_shape = [jax.ShapeDtypeStruct((t, d), F32), jax.ShapeDtypeStruct((t, 2 * f), BF16),
                 jax.ShapeDtypeStruct((t, f), BF16)]
    scratch = [pltpu.VMEM((d, 2 * f), BF16), pltpu.VMEM((f, d), BF16), pltpu.SemaphoreType.DMA]
    args = (h, nw, wgu, wd)
    if nxt is not None:
        in_specs += [ANY, ANY]
        out_specs += [ANY, ANY]
        out_shape += [jax.ShapeDtypeStruct((d, 2 * f), BF16), jax.ShapeDtypeStruct((f, d), BF16)]
        scratch += [pltpu.SemaphoreType.DMA((12,)), pltpu.SemaphoreType.DMA((12,)), pltpu.SemaphoreType.DMA((2,))]
        args += tuple(nxt)
    return pl.pallas_call(
        body, name="ffn_fwd_gather" if nxt is not None else "ffn_fwd",
        grid=(nsteps,),
        in_specs=in_specs, out_specs=out_specs, out_shape=out_shape, scratch_shapes=scratch,
        compiler_params=_arb(1),
    )(*args)


def ffn_bwd(h, nw, wgu, wd, gu, dh):
    t, d = h.shape
    f = wd.shape[0]
    fc = f // FFN_CHUNKS
    tm = _tile(t, 256)

    def body(h_ref, nw_ref, wgu_hbm, wd_hbm, gu_ref, dh_ref, o_ref, hn_ref, dgu_ref, dnw_ref, wgu_ref, wd_ref, sem):
        _resident(wgu_hbm, wgu_ref, sem)
        _resident(wd_hbm, wd_ref, sem)

        @pl.when(pl.program_id(0) == 0)
        def _():
            dnw_ref[...] = jnp.zeros_like(dnw_ref)

        nw = nw_ref[...]
        hn32, xhat, r = _rms(h_ref[...], nw)
        hn_ref[...] = hn32.astype(BF16)
        dh = dh_ref[...]
        dout = (0.5 * dh).astype(BF16)
        dhn = jnp.zeros((tm, d), F32)
        for c in range(FFN_CHUNKS):
            gs, us = slice(c * fc, (c + 1) * fc), slice(f + c * fc, f + (c + 1) * fc)
            g = gu_ref[:, gs].astype(F32)
            u = gu_ref[:, us].astype(F32)
            sg = _sigmoid(g)
            sil = g * sg
            da = _dot_nt(dout, wd_ref[gs, :])
            dg = (da * u * (sg * (1.0 + g * (1.0 - sg)))).astype(BF16)
            du = (da * sil).astype(BF16)
            dgu_ref[:, gs] = dg
            dgu_ref[:, us] = du
            dhn = dhn + _dot_nt(dg, wgu_ref[:, gs]) + _dot_nt(du, wgu_ref[:, us])
        dx, dnw = _rms_bwd(dhn, nw, xhat, r)
        dnw_ref[...] += dnw
        o_ref[...] = dh + dx

    row = pl.BlockSpec((tm, d), lambda i: (i, 0))
    vec = pl.BlockSpec((1, d), lambda i: (0, 0))
    return pl.pallas_call(
        body, name="ffn_bwd",
        grid=(t // tm,),
        in_specs=[row, vec, ANY, ANY, pl.BlockSpec((tm, 2 * f), lambda i: (i, 0)), row],
        out_specs=[row, row, pl.BlockSpec((tm, 2 * f), lambda i: (i, 0)), vec],
        out_shape=[jax.ShapeDtypeStruct((t, d), F32), jax.ShapeDtypeStruct((t, d), BF16),
                   jax.ShapeDtypeStruct((t, 2 * f), BF16), jax.ShapeDtypeStruct((1, d), F32)],
        scratch_shapes=[pltpu.VMEM((d, 2 * f), BF16), pltpu.VMEM((f, d), BF16), pltpu.SemaphoreType.DMA],
        compiler_params=_arb(1),
    )(h, nw, wgu, wd, gu, dh)


def loss_head(h, nw, target):
    t, d = h.shape
    tm = _tile(t, 512)

    def body(h_ref, nw_ref, tg_ref, dh_ref, loss_ref, dnw_ref):
        @pl.when(pl.program_id(0) == 0)
        def _():
            loss_ref[...] = jnp.zeros_like(loss_ref)
            dnw_ref[...] = jnp.zeros_like(dnw_ref)

        nw = nw_ref[...]
        y, xhat, r = _rms(h_ref[...], nw)
        e = y - tg_ref[...]
        loss_ref[...] += 0.5 * jnp.sum(jnp.mean(e * e, axis=-1, keepdims=True), axis=0, keepdims=True)
        dx, dnw = _rms_bwd(e * (1.0 / d), nw, xhat, r)
        dnw_ref[...] += dnw
        dh_ref[...] = dx

    row = pl.BlockSpec((tm, d), lambda i: (i, 0))
    vec = pl.BlockSpec((1, d), lambda i: (0, 0))
    return pl.pallas_call(
        body, name="loss_head",
        grid=(t // tm,),
        in_specs=[row, vec, row],
        out_specs=[row, pl.BlockSpec((8, LANE), lambda i: (0, 0)), vec],
        out_shape=[jax.ShapeDtypeStruct((t, d), F32), jax.ShapeDtypeStruct((8, LANE), F32),
                   jax.ShapeDtypeStruct((1, d), F32)],
        compiler_params=_arb(1),
    )(h, nw, target)


def adamw(w, g, m, v):
    r, c = w.shape
    tr = r
    while tr * c * 4 > (1 << 20) and tr % 16 == 0:
        tr //= 2

    def body(w_ref, g_ref, m_ref, v_ref, d_ref, nm_ref, nv_ref):
        g = g_ref[...]
        m = ADAM_B1 * m_ref[...] + (1.0 - ADAM_B1) * g
        v = ADAM_B2 * v_ref[...] + (1.0 - ADAM_B2) * (g * g)
        m_hat = m / (1.0 - ADAM_B1 ** ADAM_STEP)
        v_hat = v / (1.0 - ADAM_B2 ** ADAM_STEP)
        d_ref[...] = -ADAM_LR * (m_hat / (jnp.sqrt(v_hat) + ADAM_EPS) + ADAM_WD * w_ref[...])
        nm_ref[...] = m
        nv_ref[...] = v

    blk = pl.BlockSpec((tr, c), lambda i: (i, 0))
    return pl.pallas_call(
        body, name="adamw",
        grid=(r // tr,),
        in_specs=[blk] * 4, out_specs=[blk] * 3,
        out_shape=[jax.ShapeDtypeStruct((r, c), F32)] * 3,
        compiler_params=_arb(1),
    )(w, g, m, v)


ATTN_P_W = ATTN_Q_W + 2 * ATTN_KV_W
ATTN_SCALE = ATTN_HEAD_DIM ** -0.5


def _rope_group(t, tab, sign):
    return (t * tab[:, 0:LANE] + sign * pltpu.roll(t, 8, 1) * tab[:, LANE:2 * LANE]
            + sign * pltpu.roll(t, LANE - 8, 1) * tab[:, 2 * LANE:3 * LANE])


def _rope(t, tab, sign=1.0):
    return jnp.concatenate([_rope_group(t[:, s:s + LANE], tab, sign) for s in range(0, t.shape[1], LANE)], axis=1)


def _attn_heads(qs, ks, vs, sinks, first):
    b = ATTN_BLOCK
    rows = ATTN_GROUP * b
    qi = _iota2((rows, 2 * b), 0) % b
    kj = _iota2((rows, 2 * b), 1)
    rel = qi + b - kj
    valid = (rel >= 0) & (rel < b) & ((kj >= b) | jnp.logical_not(first))
    ss = _each(lambda q, k: jnp.where(valid, _mm1_nt(q, k) * ATTN_SCALE, NEG), qs, ks)
    ms = _each(lambda s, sink: lax.stop_gradient(jnp.maximum(jnp.max(s, axis=-1, keepdims=True), sink)), ss, sinks)
    ps = _each(lambda s, m: jnp.exp(s - m), ss, ms)
    dens = _each(lambda p, sink, m: jnp.sum(p, axis=-1, keepdims=True) + jnp.exp(sink - m), ps, sinks, ms)
    return _each(lambda p, den, v: _mm1(p / den, v), ps, dens, vs)


def _attn_prepare(p_ref, kvp_ref, tab_ref, tabp_ref, sink_ref):
    b = ATTN_BLOCK
    hd = ATTN_HEAD_DIM
    tab, tabp = tab_ref[...], tabp_ref[...]
    q = _rope(p_ref[:, 0:ATTN_Q_W], tab)
    kc = _rope(p_ref[:, ATTN_Q_W:ATTN_Q_W + ATTN_KV_W], tab)
    kp = _rope(kvp_ref[:, 0:ATTN_KV_W], tabp)
    vc = p_ref[:, ATTN_Q_W + ATTN_KV_W:ATTN_P_W]
    vp = kvp_ref[:, ATTN_KV_W:2 * ATTN_KV_W]
    sk = sink_ref[...]
    qs, ks, vs, sinks = [], [], [], []
    for h in range(ATTN_KV_HEADS):
        heads = [ATTN_GROUP * h + g for g in range(ATTN_GROUP)]
        qs.append(jnp.concatenate([q[:, i * hd:(i + 1) * hd] for i in heads], axis=0))
        ks.append(jnp.concatenate([kp[:, h * hd:(h + 1) * hd], kc[:, h * hd:(h + 1) * hd]], axis=0))
        vs.append(jnp.concatenate([vp[:, h * hd:(h + 1) * hd], vc[:, h * hd:(h + 1) * hd]], axis=0))
        sinks.append(jnp.concatenate([jnp.broadcast_to(sk[:, i:i + 1], (b, 1)) for i in heads], axis=0))
    return qs, ks, vs, sinks


def _unstack_heads(xs):
    b = ATTN_BLOCK
    return jnp.concatenate([x[g * b:(g + 1) * b] for x in xs for g in range(ATTN_GROUP)], axis=1)


def _attn_specs(nb):
    b = ATTN_BLOCK
    prev = lambda n: jnp.maximum(n - 1, 0)
    return [pl.BlockSpec((b, ATTN_P_W), lambda n: (n, 0)),
            pl.BlockSpec((b, 2 * ATTN_KV_W), lambda n: (prev(n), ATTN_Q_W // (2 * ATTN_KV_W))),
            pl.BlockSpec((b, 3 * LANE), lambda n: (n, 0)),
            pl.BlockSpec((b, 3 * LANE), lambda n: (prev(n), 0)),
            pl.BlockSpec((1, ATTN_Q_HEADS), lambda n: (0, 0))]


def attn_fwd(p, tab, sinks):
    t = p.shape[0]
    nb = t // ATTN_BLOCK

    def body(p_ref, kvp_ref, tab_ref, tabp_ref, sink_ref, o_ref):
        qs, ks, vs, sk = _attn_prepare(p_ref, kvp_ref, tab_ref, tabp_ref, sink_ref)
        os_ = _attn_heads(qs, ks, vs, sk, pl.program_id(0) == 0)
        o_ref[...] = _unstack_heads(os_).astype(BF16)

    return pl.pallas_call(
        body, name="attn_fwd",
        grid=(nb,),
        in_specs=_attn_specs(nb),
        out_specs=pl.BlockSpec((ATTN_BLOCK, ATTN_Q_W), lambda n: (n, 0)),
        out_shape=jax.ShapeDtypeStruct((t, ATTN_Q_W), BF16),
        compiler_params=_arb(1),
    )(p, p, tab, tab, sinks)


def attn_bwd(p, tab, sinks, do):
    t = p.shape[0]
    b = ATTN_BLOCK
    hd = ATTN_HEAD_DIM
    nb = t // b

    def body(p_ref, kvp_ref, tab_ref, tabp_ref, sink_ref, do_ref, dq_ref, dkvc_ref, dkvp_ref, dsink_ref):
        n = pl.program_id(0)

        @pl.when(n == 0)
        def _():
            dsink_ref[...] = jnp.zeros_like(dsink_ref)

        qs, ks, vs, sk = _attn_prepare(p_ref, kvp_ref, tab_ref, tabp_ref, sink_ref)
        first = n == 0
        _, vjp = jax.vjp(lambda a, bb, c, d: _attn_heads(a, bb, c, d, first), qs, ks, vs, sk)
        do = do_ref[...]
        dos = [jnp.concatenate([do[:, i * hd:(i + 1) * hd] for i in range(ATTN_GROUP * h, ATTN_GROUP * (h + 1))], axis=0)
               for h in range(ATTN_KV_HEADS)]
        dqs, dks, dvs, dsk = vjp(dos)
        tab, tabp = tab_ref[...], tabp_ref[...]
        dq_ref[...] = _rope(_unstack_heads([x.astype(F32) for x in dqs]), tab, -1.0).astype(BF16)
        dkc = jnp.concatenate([x.astype(F32)[b:] for x in dks], axis=1)
        dkp = jnp.concatenate([x.astype(F32)[:b] for x in dks], axis=1)
        dkvc_ref[:, 0:ATTN_KV_W] = _rope(dkc, tab, -1.0)
        dkvp_ref[:, 0:ATTN_KV_W] = _rope(dkp, tabp, -1.0)
        dkvc_ref[:, ATTN_KV_W:] = jnp.concatenate([x.astype(F32)[b:] for x in dvs], axis=1)
        dkvp_ref[:, ATTN_KV_W:] = jnp.concatenate([x.astype(F32)[:b] for x in dvs], axis=1)
        parts = [jnp.sum(d[g * b:(g + 1) * b], axis=0, keepdims=True) for d in dsk for g in range(ATTN_GROUP)]
        dsink_ref[...] += jnp.concatenate(parts, axis=1)

    blk = lambda w: pl.BlockSpec((b, w), lambda n: (n, 0))
    return pl.pallas_call(
        body, name="attn_bwd",
        grid=(nb,),
        in_specs=_attn_specs(nb) + [blk(ATTN_Q_W)],
        out_specs=[blk(ATTN_Q_W), blk(2 * ATTN_KV_W), blk(2 * ATTN_KV_W),
                   pl.BlockSpec((1, ATTN_Q_HEADS), lambda n: (0, 0))],
        out_shape=[jax.ShapeDtypeStruct((t, ATTN_Q_W), BF16), jax.ShapeDtypeStruct((t, 2 * ATTN_KV_W), F32),
                   jax.ShapeDtypeStruct((t, 2 * ATTN_KV_W), F32), jax.ShapeDtypeStruct((1, ATTN_Q_HEADS), F32)],
        compiler_params=_arb(1),
    )(p, p, tab, tab, sinks, do)


def kv_combine(dkvc, dkvp):
    t, w = dkvc.shape
    b = ATTN_BLOCK
    nb = t // b

    def body(c_ref, p_ref, o_ref):
        nxt = jnp.where(pl.program_id(0) < nb - 1, p_ref[...], 0.0)
        o_ref[...] = (c_ref[...] + nxt).astype(BF16)

    return pl.pallas_call(
        body, name="kv_combine",
        grid=(nb,),
        in_specs=[pl.BlockSpec((b, w), lambda n: (n, 0)),
                  pl.BlockSpec((b, w), lambda n: (jnp.minimum(n + 1, nb - 1), 0))],
        out_specs=pl.BlockSpec((b, w), lambda n: (n, 0)),
        out_shape=jax.ShapeDtypeStruct((t, w), BF16),
        compiler_params=_arb(1),
    )(dkvc, dkvp)


def _me():
    return lax.axis_index("x"), lax.axis_index("y"), lax.axis_index("c")


def _flip(v, bit):
    return 1 - v if bit else v


def _chip_index():
    return 2 * lax.axis_index("x") + lax.axis_index("y")


def allgather_chips(x, split):
    r, c = x.shape
    rh = r // 2 if split else r

    def body(x_ref, o_ref, send_sems, recv_sems):
        mx, my, mc = _me()
        me = 2 * mx + my
        rows = pl.ds(pl.multiple_of(mc * rh, 16), rh) if split else pl.ds(0, r)
        fetched, passed = [], []
        for k in (1, 2, 3):
            peer = (_flip(mx, k >> 1), _flip(my, k & 1), mc)
            cp = pltpu.make_async_remote_copy(src_ref=x_ref.at[rows], dst_ref=o_ref.at[me, rows],
                                              send_sem=send_sems.at[k - 1], recv_sem=recv_sems.at[k - 1],
                                              device_id=peer, device_id_type=MESH)
            cp.start()
            fetched.append(cp)
        if split:
            for k in (1, 2, 3):
                fetched[k - 1].wait_recv()
                theirs = o_ref.at[2 * _flip(mx, k >> 1) + _flip(my, k & 1), rows]
                cp = pltpu.make_async_remote_copy(src_ref=theirs, dst_ref=theirs, send_sem=send_sems.at[2 + k],
                                                  recv_sem=recv_sems.at[2 + k], device_id=(mx, my, 1 - mc),
                                                  device_id_type=MESH)
                cp.start()
                passed.append(cp)
            for cp in fetched:
                cp.wait_send()
            for cp in passed:
                cp.wait()
        else:
            for cp in fetched:
                cp.wait()

    got = pl.pallas_call(
        body, name="allgather_chips",
        in_specs=[ANY], out_specs=ANY,
        out_shape=jax.ShapeDtypeStruct((4, r, c), x.dtype),
        scratch_shapes=[pltpu.SemaphoreType.DMA((6,)), pltpu.SemaphoreType.DMA((6,))],
    )(x)
    return lax.dynamic_update_slice(got, x[None], (_chip_index(), 0, 0))


def exchange(kind, arrays):
    make, per_array, out_of = CARRIED[kind]
    na = len(arrays)

    def body(*refs):
        copies = make(refs[:na], refs[na:2 * na], refs[2 * na], refs[2 * na + 1])
        for cp in copies:
            cp.start()
        for cp in copies:
            cp.wait()

    return pl.pallas_call(
        body, name=kind + "_exchange",
        in_specs=[ANY] * na, out_specs=[ANY] * na,
        out_shape=[jax.ShapeDtypeStruct(out_of(a), a.dtype) for a in arrays],
        scratch_shapes=[pltpu.SemaphoreType.DMA((per_array * na,))] * 2,
    )(*arrays)


def _row_tile(r, c):
    return max(d for d in range(16, r + 1, 16) if r % d == 0 and d * c * 4 <= (2 << 20))


def add_pair(g, got, core):
    _, _, r, c = g.shape
    tr = _row_tile(r, c)

    def body(core_ref, g_ref, got_ref, p32_ref, p16_ref):
        s = g_ref[...] + got_ref[...]
        p32_ref[...] = s
        p16_ref[...] = s.astype(BF16)

    blk = pl.BlockSpec((None, tr, c), lambda j, i, core_ref: (j, i, 0))
    return pl.pallas_call(
        body, name="add_pair",
        grid_spec=pltpu.PrefetchScalarGridSpec(
            num_scalar_prefetch=1, grid=(4, r // tr),
            in_specs=[pl.BlockSpec((None, None, tr, c), lambda j, i, core_ref: (j, core_ref[0], i, 0)), blk],
            out_specs=[blk, blk]),
        out_shape=[jax.ShapeDtypeStruct((4, r, c), F32), jax.ShapeDtypeStruct((4, r, c), BF16)],
        compiler_params=_arb(2),
    )(core, g, got)


def sum_slots(p32, q16, order):
    _, r, c = p32.shape
    tr = _row_tile(r, c)

    def body(order_ref, own_ref, a_ref, b_ref, c_ref, o_ref):
        o_ref[...] = ((own_ref[...] + a_ref[...].astype(F32)) + b_ref[...].astype(F32)) + c_ref[...].astype(F32)

    slot = lambda k: pl.BlockSpec((None, tr, c), functools.partial(lambda k, i, order_ref: (order_ref[k], i, 0), k))
    return pl.pallas_call(
        body, name="sum_slots",
        grid_spec=pltpu.PrefetchScalarGridSpec(
            num_scalar_prefetch=1, grid=(r // tr,),
            in_specs=[slot(0), slot(1), slot(2), slot(3)],
            out_specs=pl.BlockSpec((tr, c), lambda i, order_ref: (i, 0))),
        out_shape=jax.ShapeDtypeStruct((r, c), F32),
        compiler_params=_arb(1),
    )(order, p32, q16, q16, q16)


def allreduce_small(x):
    r, c = x.shape

    def body(x_ref, o_ref, buf, send_sems, recv_sems):
        mx, my, mc = _me()
        me = 4 * mx + 2 * my + mc
        buf[pl.ds(me, 1)] = x_ref[...][None]
        copies = []
        for k in range(1, 8):
            peer = (_flip(mx, k >> 2), _flip(my, (k >> 1) & 1), _flip(mc, k & 1))
            cp = pltpu.make_async_remote_copy(src_ref=x_ref, dst_ref=buf.at[me], send_sem=send_sems.at[k - 1],
                                              recv_sem=recv_sems.at[k - 1], device_id=peer, device_id_type=MESH)
            cp.start()
            copies.append(cp)
        for cp in copies:
            cp.wait()
        acc = buf[0]
        for d in range(1, 8):
            acc = acc + buf[d]
        o_ref[...] = acc

    return pl.pallas_call(
        body, name="allreduce_small",
        out_shape=jax.ShapeDtypeStruct((r, c), F32),
        scratch_shapes=[pltpu.VMEM((8, r, c), F32), pltpu.SemaphoreType.DMA((7,)), pltpu.SemaphoreType.DMA((7,))],
    )(x)


class GradReduce:
    def __init__(self, grads, shard_shapes):
        self.shapes = shard_shapes
        self.groups = _by_width(shard_shapes)
        self.core = lax.axis_index("c")
        me = _chip_index()
        self.order = jnp.stack([me, me ^ 1, me ^ 2, me ^ 3]).astype(jnp.int32)
        self.g = []
        for names in self.groups.values():
            slots = []
            for j in range(4):
                parts = []
                for n in names:
                    w = shard_shapes[n][1 + SHARDED[n]]
                    parts += [lax.slice_in_dim(gl, j * w, (j + 1) * w, axis=SHARDED[n]) for gl in grads[n]]
                slots.append(jnp.concatenate(parts, axis=0))
            g = jnp.stack(slots)
            self.g.append(g.reshape(4, 2, g.shape[1] // 2, g.shape[2]))
        self.p32 = self.p16 = None
        self.q16 = [None] * len(self.g)
        n = len(self.g)
        self.pieces = [list(range(2, n)), [1], [0]] if n > 2 else [[i] for i in reversed(range(n))]
        self.pieces = [p for p in self.pieces if p]
        self.stage = 0

    def next_job(self):
        if self.stage == 0:
            return "pair", self.g
        if self.stage <= len(self.pieces):
            return "scatter", [self.p16[i] for i in self.pieces[self.stage - 1]]
        return None

    def deliver(self, res):
        if self.stage == 0:
            core = self.core.astype(jnp.int32)[None]
            pairs = [add_pair(g, got, core) for g, got in zip(self.g, res)]
            self.p32, self.p16 = [p[0] for p in pairs], [p[1] for p in pairs]
        else:
            for i, q in zip(self.pieces[self.stage - 1], res):
                self.q16[i] = q
        self.stage += 1

    def finish(self):
        while (job := self.next_job()) is not None:
            self.deliver(exchange(*job))
        mine = [sum_slots(p, q, self.order) for p, q in zip(self.p32, self.q16)]
        theirs = exchange("share", mine)
        out = {}
        for names, m, t in zip(self.groups.values(), mine, theirs):
            red = jnp.where(self.core == 0, jnp.concatenate([m, t]), jnp.concatenate([t, m]))
            off = 0
            for n in names:
                layers, a, _ = self.shapes[n]
                out[n] = red[off:off + layers * a].reshape(self.shapes[n])
                off += layers * a
        return out


PACK_W = 1024
PACK_ROWS = 1024

SHARDED = {"ffn1_w_gate_up": 1, "ffn1_w_down": 0, "ffn2_w_gate_up": 1, "ffn2_w_down": 0, "attn_w_in": 1,
           "attn_w_out": 0, "gdn_w_in": 1, "gdn_w_out": 0, "gdn_conv_w": 1}
REPLICATED = ["ffn1_norm", "mix_norm", "ffn2_norm", "attn_b_in", "attn_sinks", "attn_b_out", "gdn_A_log",
              "gdn_dt_bias", "gdn_norm_w", "final_norm"]
WEIGHTS = ["ffn1_norm", "ffn1_w_gate_up", "ffn1_w_down", "mix_norm", "ffn2_norm", "ffn2_w_gate_up", "ffn2_w_down",
           "attn_w_in", "attn_b_in", "attn_sinks", "attn_w_out", "attn_b_out", "gdn_w_in", "gdn_conv_w", "gdn_A_log",
           "gdn_dt_bias", "gdn_norm_w", "gdn_w_out", "final_norm"]


def _pack_rows(flats, dtype, width, row_multiple):
    flat = jnp.concatenate([f.astype(dtype).reshape(-1) for f in flats])
    per = width * row_multiple
    pad = (-flat.shape[0]) % per
    if pad:
        flat = jnp.concatenate([flat, jnp.zeros((pad,), dtype)])
    return flat.reshape(-1, width)


def _by_width(shapes):
    groups = {}
    for n, shape in shapes.items():
        groups.setdefault(shape[-1], []).append(n)
    return groups


def _gather_weights(shards, dtype, split=True):
    full = {}
    for width, names in _by_width({n: s.shape for n, s in shards.items()}).items():
        packed = jnp.concatenate([shards[n].astype(dtype).reshape(-1, width) for n in names], axis=0)
        got = allgather_chips(packed, split)
        off = 0
        for n in names:
            layers, a, _ = shards[n].shape
            full[n] = [jnp.concatenate([got[j, off + l * a:off + (l + 1) * a] for j in range(4)], axis=SHARDED[n])
                       for l in range(layers)]
            off += layers * a
    return full


def _small_pack(items):
    rows = []
    for a in items:
        f = a.astype(F32).reshape(-1)
        pad = (-f.shape[0]) % LANE
        rows.append(jnp.concatenate([f, jnp.zeros((pad,), F32)]) if pad else f)
    return _pack_rows(rows, F32, LANE, 8)


def _small_unpack(buf, shapes):
    flat = buf.reshape(-1)
    out, off = [], 0
    for shape in shapes:
        size = 1
        for s in shape:
            size *= s
        out.append(flat[off:off + size].reshape(shape))
        off += size + (-size) % LANE
    return out


def _rope_table(positions):
    t = positions.shape[0]
    inv_freq = ROPE_THETA ** (-jnp.arange(0, ROPE_DIM, 2, dtype=F32) / ROPE_DIM)
    ang = positions.astype(F32)[:, None] * inv_freq
    cos, sin = jnp.cos(ang), jnp.sin(ang)
    rest = ATTN_HEAD_DIM - ROPE_DIM
    zeros = lambda n: jnp.zeros((t, n), F32)
    c64 = jnp.concatenate([cos, cos, jnp.ones((t, rest), F32)], axis=1)
    s_up = jnp.concatenate([zeros(ROPE_DIM // 2), sin, zeros(rest)], axis=1)
    s_dn = jnp.concatenate([-sin, zeros(ROPE_DIM // 2 + rest)], axis=1)
    return jnp.concatenate([c64, c64, s_up, s_up, s_dn, s_dn], axis=1)


def _as2d(a):
    return a.reshape(-1, a.shape[-1]) if a.ndim > 1 else a.reshape(1, -1)


def kernel(x, positions, ffn1_norm, ffn1_w_gate_up, ffn1_w_down, mix_norm, ffn2_norm, ffn2_w_gate_up, ffn2_w_down, attn_w_in, attn_b_in, attn_sinks, attn_w_out, attn_b_out, gdn_w_in, gdn_conv_w, gdn_A_log, gdn_dt_bias, gdn_norm_w, gdn_w_out, final_norm, loss_target, m_ffn1_norm, m_ffn1_w_gate_up, m_ffn1_w_down, m_mix_norm, m_ffn2_norm, m_ffn2_w_gate_up, m_ffn2_w_down, m_attn_w_in, m_attn_b_in, m_attn_sinks, m_attn_w_out, m_attn_b_out, m_gdn_w_in, m_gdn_conv_w, m_gdn_A_log, m_gdn_dt_bias, m_gdn_norm_w, m_gdn_w_out, m_final_norm, v_ffn1_norm, v_ffn1_w_gate_up, v_ffn1_w_down, v_mix_norm, v_ffn2_norm, v_ffn2_w_gate_up, v_ffn2_w_down, v_attn_w_in, v_attn_b_in, v_attn_sinks, v_attn_w_out, v_attn_b_out, v_gdn_w_in, v_gdn_conv_w, v_gdn_A_log, v_gdn_dt_bias, v_gdn_norm_w, v_gdn_w_out, v_final_norm):
    given = dict(locals())
    w = {n: given[n] for n in WEIGHTS}
    d = D_MODEL
    h = x[0]
    target = loss_target[0]
    depth = ffn1_norm.shape[0]

    big = [n for n in SHARDED if n != "gdn_conv_w"]
    up_front = {n: (w[n][0:1] if n.startswith("ffn1") else w[n]) for n in big if not n.startswith("ffn2")}
    full = _gather_weights(up_front, BF16)
    ffn_order = [(tag, l) for l in range(depth) for tag in ("ffn1", "ffn2")]
    ffn_w = {ffn_order[0]: (full["ffn1_w_gate_up"][0], full["ffn1_w_down"][0])}

    def run_ffn(tag, l, h):
        i = ffn_order.index((tag, l))
        wgu, wd = ffn_w[(tag, l)]
        nw = w[tag + "_norm"][l][None]
        if i + 1 == len(ffn_order):
            return ffn_fwd(h, nw, wgu, wd)
        ntag, nl = ffn_order[i + 1]
        shards = (w[ntag + "_w_gate_up"][nl].astype(BF16), w[ntag + "_w_down"][nl].astype(BF16))
        h, gu, a, ngu, nd = ffn_fwd(h, nw, wgu, wd, shards)
        ffn_w[(ntag, nl)] = (ngu, nd)
        return h, gu, a

    conv_full = _gather_weights({"gdn_conv_w": gdn_conv_w}, F32, split=False)["gdn_conv_w"]
    tab = _rope_table(positions[0])
    zero_d = jnp.zeros((1, d), F32)

    def gdn_params(j):
        w_in = full["gdn_w_in"][j]
        w_cat = jnp.concatenate([w_in, jnp.zeros((d, GDN_PROJ_W - w_in.shape[1]), BF16)], axis=1)
        conv = jnp.concatenate([conv_full[j], jnp.zeros((8 - GDN_CONV, GDN_QKV_W), F32)], axis=0)
        lanes = lambda vec: jnp.concatenate([jnp.zeros((GC_G,), F32), vec, jnp.zeros((LANE - GC_G - GDN_HEADS,), F32)])
        par = jnp.concatenate([lanes(gdn_A_log[j])[None], lanes(gdn_dt_bias[j])[None], jnp.zeros((6, LANE), F32)], axis=0)
        return w_cat, conv, par

    saved = []
    for l in range(depth):
        j = l // 2
        rec = {"h1": h}
        h, rec["ffn1_gu"], rec["ffn1_a"] = run_ffn("ffn1", l, h)
        rec["h2"] = h
        if l % 2 == 0:
            p = norm_proj(h, mix_norm[l][None], full["attn_w_in"][j], attn_b_in[j][None])
            o = attn_fwd(p, tab, attn_sinks[j][None])
            h = linear_residual(h, o, full["attn_w_out"][j], attn_b_out[j][None])
            rec.update(p=p, o=o)
        else:
            w_cat, conv, par = gdn_params(j)
            proj = norm_proj(h, mix_norm[l][None], w_cat, jnp.zeros((1, GDN_PROJ_W), F32))
            qkv, gc = gdn_pre_fwd(proj, conv, par)
            on, st, rec["m_offs"] = gdn_chunk_fwd(qkv, gc, proj, gdn_norm_w[j][None])
            h = linear_residual(h, on, full["gdn_w_out"][j], zero_d)
            rec.update(proj=proj, qkv=qkv, gc=gc, on=on, st=st, w_cat=w_cat, conv=conv, par=par)
        rec["h3"] = h
        h, rec["ffn2_gu"], rec["ffn2_a"] = run_ffn("ffn2", l, h)
        saved.append(rec)

    dh, loss_tile, d_final = loss_head(h, final_norm[None], target)

    g = {n: [None] * w[n].shape[0] for n in WEIGHTS if n != "final_norm"}
    reducers = []

    def layer_of(n, i):
        return i if n.startswith("ffn") else (2 * i if n.startswith("attn") else 2 * i + 1)

    def layer_reduce(layer):
        picks = {n: idx for n in big if (idx := [i for i in range(w[n].shape[0]) if layer_of(n, i) == layer])}
        return GradReduce({n: [g[n][i] for i in idx] for n, idx in picks.items()},
                          {n: (len(idx),) + w[n].shape[1:] for n, idx in picks.items()})

    def tn(x, y, scale=1.0):
        big_enough = x.shape[1] * y.shape[1] >= D_MODEL * D_FF
        red = next((r for r in reducers if r.next_job() is not None), None) if big_enough else None
        if red is None:
            return matmul_tn(x, y, scale)
        out, res = matmul_tn(x, y, scale, carry=red.next_job())
        red.deliver(res)
        return out
    for l in reversed(range(depth)):
        j = l // 2
        rec = saved[l]

        def ffn_back(tag, h_in, dh):
            wgu, wd = ffn_w[(tag, l)]
            dh_new, hn, dgu, dn = ffn_bwd(h_in, w[tag + "_norm"][l][None], wgu, wd, rec[tag + "_gu"], dh)
            g[tag + "_w_gate_up"][l] = tn(hn, dgu)
            g[tag + "_w_down"][l] = tn(rec[tag + "_a"], dh, 0.5)
            g[tag + "_norm"][l] = dn[0]
            return dh_new

        dh = ffn_back("ffn2", rec["h3"], dh)
        if l % 2 == 0:
            w_in, w_out = full["attn_w_in"][j], full["attn_w_out"][j]
            do, db_out = matmul_nt(dh, w_out)
            g["attn_w_out"][j] = matmul_tn(rec["o"], dh)
            g["attn_b_out"][j] = db_out[0]
            dq, dkvc, dkvp, dsink = attn_bwd(rec["p"], tab, attn_sinks[j][None], do)
            dkv = kv_combine(dkvc, dkvp)
            dh, hn, dn, cs_q, cs_kv = norm_proj_bwd(rec["h2"], mix_norm[l][None], dh, [dq, dkv],
                                                    [w_in[:, :ATTN_Q_W], w_in[:, ATTN_Q_W:]])
            g["attn_w_in"][j] = jnp.concatenate([matmul_tn(hn, dq), matmul_tn(hn, dkv)], axis=1)
            g["attn_b_in"][j] = jnp.concatenate([cs_q[0], cs_kv[0]])
            g["attn_sinks"][j] = dsink[0]
        else:
            w_cat, conv, par = rec["w_cat"], rec["conv"], rec["par"]
            d_on, _ = matmul_nt(dh, full["gdn_w_out"][j])
            g["gdn_w_out"][j] = matmul_tn(rec["on"], dh)
            dq, dk, dv, dz, dgc_heads, dnw = gdn_chunk_bwd(rec["qkv"], rec["gc"], rec["proj"], gdn_norm_w[j][None],
                                                           rec["st"], rec["m_offs"], d_on)
            dy, dba, dpar = gdn_pre_bwd(rec["proj"], conv, par, dq, dk, dv, dgc_heads)
            dx, dconv = gdn_conv_bwd(rec["proj"], conv, dy)
            nz = GDN_QKV_W + GDN_HEADS * GDN_DK
            dh, hn, dn, _, _, _ = norm_proj_bwd(rec["h2"], mix_norm[l][None], dh, [dx, dz, dba],
                                                [w_cat[:, :GDN_QKV_W], w_cat[:, GDN_QKV_W:nz], w_cat[:, nz:]])
            g["gdn_w_in"][j] = jnp.concatenate(
                [tn(hn, dx), matmul_tn(hn, dz), matmul_tn(hn, dba)[:, :2 * GDN_HEADS]], axis=1)
            g["gdn_conv_w"][j] = dconv[:GDN_CONV]
            g["gdn_A_log"][j] = dpar[0, GDN_HEADS:2 * GDN_HEADS]
            g["gdn_dt_bias"][j] = dpar[1, GDN_HEADS:2 * GDN_HEADS]
            g["gdn_norm_w"][j] = dnw[0]
        g["mix_norm"][l] = dn[0]
        dh = ffn_back("ffn1", rec["h1"], dh)
        if l > 0:
            reducers.append(layer_reduce(l))
    grad_x = dh[None]

    small_names = REPLICATED + ["gdn_conv_w"]
    local = {n: jnp.stack(g[n]) for n in small_names if n != "final_norm"}
    local["final_norm"] = d_final[0]
    small_shapes = [(1,)] + [local[n].shape for n in small_names]
    small = allreduce_small(_small_pack([loss_tile[0, 0:1]] + [local[n] for n in small_names]))
    small = _small_unpack(small, small_shapes)
    loss = small[0][0]
    grads = dict(zip(small_names, small[1:]))
    conv_cols = gdn_conv_w.shape[2]
    grads["gdn_conv_w"] = lax.dynamic_slice_in_dim(grads["gdn_conv_w"], _chip_index() * conv_cols, conv_cols, axis=2)
    by_layer = [layer_reduce(0).finish()] + [r.finish() for r in reversed(reducers)]
    grads.update({n: jnp.concatenate([part[n] for part in by_layer if n in part], axis=0) for n in big})

    delta, new_m, new_v = {}, {}, {}
    for n in SHARDED:
        dl, nm, nv = adamw(_as2d(w[n]), _as2d(grads[n]), _as2d(given["m_" + n]), _as2d(given["v_" + n]))
        delta[n], new_m[n], new_v[n] = dl.reshape(w[n].shape), nm.reshape(w[n].shape), nv.reshape(w[n].shape)
    shapes = [w[n].shape for n in REPLICATED]
    packed = [_small_pack([src[n] for n in REPLICATED]) for src in
              (w, grads, {n: given["m_" + n] for n in REPLICATED}, {n: given["v_" + n] for n in REPLICATED})]
    for dst, buf in zip((delta, new_m, new_v), adamw(*packed)):
        dst.update(zip(REPLICATED, _small_unpack(buf, shapes)))

    return (loss, grad_x, *[grads[n] for n in WEIGHTS], *[delta[n] for n in WEIGHTS],
            *[new_m[n] for n in WEIGHTS], *[new_v[n] for n in WEIGHTS])
```

```python
import functools

import jax
import jax.numpy as jnp
from jax import lax
from jax.experimental import pallas as pl
from jax.experimental.pallas import tpu as pltpu

F32 = jnp.float32
BF16 = jnp.bfloat16
HI = lax.Precision.HIGHEST
MESH = pl.DeviceIdType.MESH

D_MODEL = 1024
D_FF = 2816
DEPTH = 4
NORM_EPS = 1e-6
LANE = 128

ATTN_Q_HEADS = 16
ATTN_KV_HEADS = 4
ATTN_HEAD_DIM = 64
ATTN_GROUP = 4
ATTN_BLOCK = 128
ROPE_DIM = 16
ROPE_THETA = 500000.0
ATTN_Q_W = 1024
ATTN_KV_W = 256

GDN_HEADS = 8
GDN_DK = 128
GDN_CONV = 4
GDN_CHUNK = 64
GDN_QKV_W = 3072

ADAM_LR = 0.001
ADAM_B1 = 0.9
ADAM_B2 = 0.999
ADAM_EPS = 1e-08
ADAM_WD = 0.01
ADAM_STEP = 10

NEG = -1e30


def _dot(a, b, prec=None):
    return lax.dot_general(a, b, (((1,), (0,)), ((), ())), precision=prec, preferred_element_type=F32)


def _dot_nt(a, b, prec=None):
    return lax.dot_general(a, b, (((1,), (1,)), ((), ())), precision=prec, preferred_element_type=F32)


def _dot_tn(a, b, prec=None):
    return lax.dot_general(a, b, (((0,), (0,)), ((), ())), precision=prec, preferred_element_type=F32)


def _bdot(a, b):
    return _dot(a.astype(BF16), b.astype(BF16))


def _bdot_nt(a, b):
    return _dot_nt(a.astype(BF16), b.astype(BF16))


def _bdot_tn(a, b):
    return _dot_tn(a.astype(BF16), b.astype(BF16))


def _sigmoid(x):
    return 1.0 / (1.0 + jnp.exp(-x))


def _silu(x):
    return x * _sigmoid(x)


def _silu_grad(x):
    s = _sigmoid(x)
    return s * (1.0 + x * (1.0 - s))


def _rms(x, w):
    r = lax.rsqrt(jnp.mean(x * x, axis=-1, keepdims=True) + NORM_EPS)
    xhat = x * r
    return xhat * w, xhat, r


def _rms_bwd(dy, w, xhat, r):
    dxhat = dy * w
    dx = r * (dxhat - xhat * jnp.mean(dxhat * xhat, axis=-1, keepdims=True))
    dw = jnp.sum(dy * xhat, axis=0, keepdims=True)
    return dx, dw


def _arb(n):
    return pltpu.CompilerParams(dimension_semantics=("arbitrary",) * n)


def _tile(n, want):
    t = min(n, want)
    assert n % t == 0, (n, want)
    return t


def _iota2(shape, dim):
    return lax.broadcasted_iota(jnp.int32, shape, dim)


_NN = (((1,), (0,)), ((), ()))
_NT = (((1,), (1,)), ((), ()))
_TN = (((0,), (0,)), ((), ()))


def _raw1(a, b, dn):
    return lax.dot_general(a.astype(BF16), b.astype(BF16), dn, preferred_element_type=F32)


def _raw3(a, b, dn):
    ah, bh = a.astype(BF16), b.astype(BF16)
    al, bl = (a - ah.astype(F32)).astype(BF16), (b - bh.astype(F32)).astype(BF16)
    f = lambda x, y: lax.dot_general(x, y, dn, preferred_element_type=F32)
    return f(ah, bh) + f(ah, bl) + f(al, bh)


def _make_mm(raw):
    @jax.custom_vjp
    def mm(a, b):
        return raw(a, b, _NN)

    mm.defvjp(lambda a, b: (raw(a, b, _NN), (a, b)),
              lambda res, ct: (raw(ct, res[1], _NT), raw(res[0], ct, _TN)))

    @jax.custom_vjp
    def mm_nt(a, b):
        return raw(a, b, _NT)

    mm_nt.defvjp(lambda a, b: (raw(a, b, _NT), (a, b)),
                 lambda res, ct: (raw(ct, res[1], _NN), raw(ct, res[0], _TN)))
    return mm, mm_nt


_mm1, _mm1_nt = _make_mm(_raw1)
_mm3, _mm3_nt = _make_mm(_raw3)


def _eye(n):
    return (_iota2((n, n), 0) == _iota2((n, n), 1)).astype(F32)


def _each(f, *lists):
    return [f(*t) for t in zip(*lists)]


def _inv_newton(mats):
    eye = _eye(mats[0].shape[0])
    ps = [-a for a in mats]
    ms = [eye + p for p in ps]
    k = 1
    while 2 * k < GDN_CHUNK:
        ps = [_raw1(p, p, _NN) for p in ps]
        ms = _each(lambda m, p: m + _raw1(m, p, _NN), ms, ps)
        k *= 2
    rs = _each(lambda a, m: eye - m - _raw3(a, m, _NN), mats, ms)
    return _each(lambda m, r: m + _raw1(m, r, _NN), ms, rs)


def _inv_cotangent(ms, dms):
    ts = _each(lambda m, dm: _raw3(m, dm, _TN), ms, dms)
    return _each(lambda t, m: -_raw3(t, m, _NT), ts, ms)


def _chunk_causal(n):
    r, s = _iota2((n, n), 0), _iota2((n, n), 1)
    same = (r // GDN_CHUNK) == (s // GDN_CHUNK)
    return same & (r >= s), same & (r > s)


def _gdn_a(k, dcb, bb128):
    causal, strict = _chunk_causal(k[0].shape[0])
    decay_l = [jnp.exp(jnp.where(causal, d - d.T, NEG)) for d in dcb]
    kb = _each(lambda a, b: a * b, k, bb128)
    return _each(lambda x, y, dl: jnp.where(strict, _mm1_nt(x, y) * dl, 0.0), kb, k, decay_l)


def _gdn_rest(q, k, v, dcb, dcb128, dlb128, bb128, m_off):
    causal, _ = _chunk_causal(q[0].shape[0])
    decay_l = [jnp.exp(jnp.where(causal, d - d.T, NEG)) for d in dcb]
    kb = _each(lambda a, b: a * b, k, bb128)
    edc = [jnp.exp(d) for d in dcb128]
    rhs = _each(lambda vv, bb, kk, e: jnp.concatenate([vv * bb, kk * e], axis=1), v, bb128, kb, edc)
    sol = _each(lambda x, m: x + _mm3(m, x), rhs, m_off)
    aqk = _each(lambda x, y, dl: jnp.where(causal, _mm1_nt(x, y) * dl, 0.0), q, k, decay_l)
    q_dec = _each(lambda x, e: x * e, q, edc)
    k_dec = _each(lambda x, dl, dc: x * jnp.exp(dl - dc), k, dlb128, dcb128)
    return sol, aqk, q_dec, k_dec


def _gdn_local(q, k, v, dcb, dcb128, dlb128, bb128):
    eye = _eye(q[0].shape[0])
    m_off = [m - eye for m in _inv_newton(_gdn_a(k, dcb, bb128))]
    return _gdn_rest(q, k, v, dcb, dcb128, dlb128, bb128, m_off) + (m_off,)


def _gated_norm(o, z, nw):
    r = lax.rsqrt(jnp.mean(o * o, axis=-1, keepdims=True) + NORM_EPS)
    return o * r * nw * _silu(z)


def _gated_norm_bwd(dy, o, z, nw):
    r = lax.rsqrt(jnp.mean(o * o, axis=-1, keepdims=True) + NORM_EPS)
    xhat = o * r
    sz = _silu(z)
    dxhat = dy * nw * sz
    do = r * (dxhat - xhat * jnp.mean(dxhat * xhat, axis=-1, keepdims=True))
    dz = dy * xhat * nw * _silu_grad(z)
    dnw = jnp.sum(dy * xhat * sz, axis=0, keepdims=True)
    return do, dz, dnw


GDN_HEAD_GROUP = 8
GC_BETA, GC_G, GC_DECAY, GC_LAST =0, GDN_HEADS, 2 * GDN_HEADS, 3 * GDN_HEADS


def _gate_cols(gc, h, rows):
    lane = _iota2((rows, LANE), 1)
    col = lambda off: jnp.sum(jnp.where(lane == off + h, gc, 0.0), axis=-1, keepdims=True)
    return col(GC_BETA), col(GC_DECAY), col(GC_LAST)


def _gdn_local_args(q_ref, k_ref, v_ref, gc_ref, hp, n):
    gc = gc_ref[...]
    lanes = [slice(hh * LANE, (hh + 1) * LANE) for hh in range(hp)]
    cols = [_gate_cols(gc, pl.program_id(0) * hp + hh, n) for hh in range(hp)]
    bcast = lambda i, w: [jnp.broadcast_to(c[i], (n, w)) for c in cols]
    args = ([q_ref[:, ls] for ls in lanes], [k_ref[:, ls] for ls in lanes], [v_ref[:, ls] for ls in lanes],
            bcast(1, n), bcast(1, LANE), bcast(2, LANE), bcast(0, LANE))
    return args, [c[2] for c in cols], lanes


def gdn_chunk_fwd(qkv, gc, proj, norm_w):
    t = qkv.shape[0]
    c = GDN_CHUNK
    tc = _tile(t, 256)
    nsub = tc // c

    hp = GDN_HEAD_GROUP

    def body(q_ref, k_ref, v_ref, gc_ref, z_ref, nw_ref, on_ref, st_ref, m_ref, s_ref):
        @pl.when(pl.program_id(1) == 0)
        def _():
            s_ref[...] = jnp.zeros_like(s_ref)

        args, dl, lanes = _gdn_local_args(q_ref, k_ref, v_ref, gc_ref, hp, tc)
        sol, aqk, q_dec, k_dec, m_off = _gdn_local(*args)
        for hh in range(hp):
            m_ref[hh] = m_off[hh]
        u = [x[:, :GDN_DK] for x in sol]
        w = [x[:, GDN_DK:] for x in sol]
        s = [s_ref[hh] for hh in range(hp)]
        v_new, o_state = [[] for _ in range(hp)], [[] for _ in range(hp)]
        for j in range(nsub):
            sl = slice(j * c, (j + 1) * c)
            for hh in range(hp):
                st_ref[hh, j] = s[hh]
            vn = _each(lambda uu, ww, ss: uu[sl] - _bdot(ww[sl], ss), u, w, s)
            os_ = _each(lambda qq, ss: _bdot(qq[sl], ss), q_dec, s)
            s = _each(lambda ss, d, kk, vv: ss * jnp.exp(d[j * c:j * c + 1]) + _bdot_tn(kk[sl], vv), s, dl, k_dec, vn)
            for hh in range(hp):
                v_new[hh].append(vn[hh])
                o_state[hh].append(os_[hh])
        for hh in range(hp):
            s_ref[hh] = s[hh]
        cat = lambda xs: jnp.concatenate(xs, axis=0)
        o = _each(lambda os_, aa, vv: cat(os_) + _bdot(aa, cat(vv)), o_state, aqk, v_new)
        for hh, ls in enumerate(lanes):
            on_ref[:, ls] = _gated_norm(o[hh], z_ref[:, ls], nw_ref[...]).astype(BF16)

    col = lambda off: pl.BlockSpec((tc, hp * LANE), lambda g, i: (i, off // hp + g))
    return pl.pallas_call(
        body, name="gdn_chunk_fwd",
        grid=(GDN_HEADS // hp, t // tc),
        in_specs=[col(0), col(GDN_HEADS), col(2 * GDN_HEADS),
                  pl.BlockSpec((tc, LANE), lambda g, i: (i, 0)),
                  col(GDN_QKV_W // LANE),
                  pl.BlockSpec((1, LANE), lambda g, i: (0, 0))],
        out_specs=[col(0),
                   pl.BlockSpec((hp, nsub, GDN_DK, GDN_DK), lambda g, i: (g, i, 0, 0)),
                   pl.BlockSpec((hp, tc, tc), lambda g, i: (g, i, 0))],
        out_shape=[jax.ShapeDtypeStruct((t, GDN_HEADS * GDN_DK), BF16),
                   jax.ShapeDtypeStruct((GDN_HEADS, t // c, GDN_DK, GDN_DK), F32),
                   jax.ShapeDtypeStruct((GDN_HEADS, t, tc), F32)],
        scratch_shapes=[pltpu.VMEM((hp, GDN_DK, GDN_DK), F32)],
        compiler_params=_arb(2),
    )(qkv, qkv, qkv, gc, proj, norm_w)


def gdn_chunk_bwd(qkv, gc, proj, norm_w, states, m_offs, d_on):
    t = qkv.shape[0]
    c = GDN_CHUNK
    tc = _tile(t, 256)
    nsub = tc // c
    nblk = t // tc
    hp = GDN_HEAD_GROUP

    def body(q_ref, k_ref, v_ref, gc_ref, z_ref, nw_ref, st_ref, m_ref, don_ref,
             dq_ref, dk_ref, dv_ref, dz_ref, dgc_ref, dnw_ref, ds_ref):
        @pl.when(pl.program_id(1) == 0)
        def _():
            ds_ref[...] = jnp.zeros_like(ds_ref)

        @pl.when((pl.program_id(1) == 0) & (pl.program_id(0) == 0))
        def _():
            dnw_ref[...] = jnp.zeros_like(dnw_ref)

        rows = [slice(j * c, (j + 1) * c) for j in range(nsub)]
        cat = lambda xs: jnp.concatenate(xs, axis=0)
        lsum = lambda x: jnp.sum(x, axis=-1, keepdims=True)
        lane = _iota2((tc, LANE), 1)
        row = _iota2((tc, 1), 0)
        heads = range(hp)
        args, dl, lanes = _gdn_local_args(q_ref, k_ref, v_ref, gc_ref, hp, tc)
        m_off = [m_ref[hh] for hh in heads]
        (sol, aqk, q_dec, k_dec), vjp = jax.vjp(_gdn_rest, *args, m_off)
        u = [x[:, :GDN_DK] for x in sol]
        w = [x[:, GDN_DK:] for x in sol]
        states = [[st_ref[hh, j] for j in range(nsub)] for hh in heads]
        v_new = _each(lambda uu, ww, st: [uu[sl] - _bdot(ww[sl], s) for sl, s in zip(rows, st)], u, w, states)
        v_all = [cat(v) for v in v_new]
        o = _each(lambda qq, st, aa, vv: cat([_bdot(qq[sl], s) for sl, s in zip(rows, st)]) + _bdot(aa, vv),
                  q_dec, states, aqk, v_all)
        nw = nw_ref[...]
        gn = [_gated_norm_bwd(don_ref[:, ls], o[hh], z_ref[:, ls], nw) for hh, ls in enumerate(lanes)]
        do = [x[0] for x in gn]
        for hh, ls in enumerate(lanes):
            dnw_ref[...] += gn[hh][2]
            dz_ref[:, ls] = gn[hh][1].astype(BF16)
        d_aqk = _each(_bdot_nt, do, v_all)
        dv_o = _each(_bdot_tn, aqk, do)
        ds = [ds_ref[hh] for hh in heads]
        d_u, d_w, d_qdec, d_kdec, d_last = ([[None] * nsub for _ in heads] for _ in range(5))
        for j in reversed(range(nsub)):
            sl = rows[j]
            sj = [states[hh][j] for hh in heads]
            cd = [jnp.exp(d[j * c:j * c + 1]) for d in dl]
            du = _each(lambda dvo, kk, dd: dvo[sl] + _bdot(kk[sl], dd), dv_o, k_dec, ds)
            dqd = _each(lambda dd, s: _bdot_nt(dd[sl], s), do, sj)
            dkd = _each(lambda vv, dd: _bdot_nt(vv[j], dd), v_new, ds)
            dla = _each(lambda s, dd, cc: jnp.sum(lsum(s * dd), axis=0, keepdims=True) * cc, sj, ds, cd)
            dw = _each(lambda x, s: -_bdot_nt(x, s), du, sj)
            ds = _each(lambda qq, dd, cc, dsn, ww, x: _bdot_tn(qq[sl], dd[sl]) + cc * dsn - _bdot_tn(ww[sl], x),
                       q_dec, do, cd, ds, w, du)
            for hh in heads:
                d_u[hh][j], d_w[hh][j], d_qdec[hh][j], d_kdec[hh][j], d_last[hh][j] = du[hh], dw[hh], dqd[hh], dkd[hh], dla[hh]
        for hh in heads:
            ds_ref[hh] = ds[hh]
        d_sol = _each(lambda a, b: jnp.concatenate([cat(a), cat(b)], axis=1), d_u, d_w)
        dq, dk, dv, d_dcb, d_dcb128, d_dlb128, d_bb, d_m = vjp(
            (d_sol, d_aqk, [cat(x) for x in d_qdec], [cat(x) for x in d_kdec]))
        eye = _eye(tc)
        d_a = _inv_cotangent([m + eye for m in m_off], d_m)
        k_args = (args[1], args[3], args[6])
        dk_a, d_dcb_a, d_bb_a = jax.vjp(_gdn_a, *k_args)[1](d_a)
        add = lambda xs, ys: _each(lambda x, y: x + y, xs, ys)
        dk, d_dcb, d_bb = add(dk, dk_a), add(d_dcb, d_dcb_a), add(d_bb, d_bb_a)
        dgc = jnp.zeros((tc, LANE), F32)
        for hh, ls in enumerate(lanes):
            h = pl.program_id(0) * hp + hh
            dq_ref[:, ls] = dq[hh]
            dk_ref[:, ls] = dk[hh]
            dv_ref[:, ls] = dv[hh]
            d_dl = lsum(d_dlb128[hh])
            for j in range(nsub):
                d_dl = d_dl + jnp.where(row == j * c, d_last[hh][j], 0.0)
            dgc = dgc + jnp.where(lane == GC_BETA + h, lsum(d_bb[hh]),
                                  jnp.where(lane == GC_DECAY + h, lsum(d_dcb[hh]) + lsum(d_dcb128[hh]),
                                            jnp.where(lane == GC_LAST + h, d_dl, 0.0)))
        dgc_ref[0] = dgc

    rev = lambda i: nblk - 1 - i
    col = lambda off: pl.BlockSpec((tc, hp * LANE), lambda g, i: (rev(i), off // hp + g))
    return pl.pallas_call(
        body, name="gdn_chunk_bwd",
        grid=(GDN_HEADS // hp, nblk),
        in_specs=[col(0), col(GDN_HEADS), col(2 * GDN_HEADS),
                  pl.BlockSpec((tc, LANE), lambda g, i: (rev(i), 0)),
                  col(GDN_QKV_W // LANE),
                  pl.BlockSpec((1, LANE), lambda g, i: (0, 0)),
                  pl.BlockSpec((hp, nsub, GDN_DK, GDN_DK), lambda g, i: (g, rev(i), 0, 0)),
                  pl.BlockSpec((hp, tc, tc), lambda g, i: (g, rev(i), 0)),
                  col(0)],
        out_specs=[col(0), col(0), col(0), col(0),
                   pl.BlockSpec((1, tc, LANE), lambda g, i: (g, rev(i), 0)),
                   pl.BlockSpec((1, LANE), lambda g, i: (0, 0))],
        out_shape=[
            jax.ShapeDtypeStruct((t, GDN_HEADS * GDN_DK), F32),
            jax.ShapeDtypeStruct((t, GDN_HEADS * GDN_DK), F32),
            jax.ShapeDtypeStruct((t, GDN_HEADS * GDN_DK), F32),
            jax.ShapeDtypeStruct((t, GDN_HEADS * GDN_DK), BF16),
            jax.ShapeDtypeStruct((GDN_HEADS // hp, t, LANE), F32),
            jax.ShapeDtypeStruct((1, LANE), F32)],
        scratch_shapes=[pltpu.VMEM((hp, GDN_DK, GDN_DK), F32)],
        compiler_params=_arb(2),
    )(qkv, qkv, qkv, gc, proj, norm_w, states, m_offs, d_on)


GDN_PROJ_W = GDN_QKV_W + 1024 + LANE
GDN_Q_SCALE = GDN_DK ** -0.5


def _shift_rows(ext, shift, lo, n):
    if shift == 0:
        return ext[lo:lo + n]
    return pltpu.roll(ext, shift, 0)[lo:lo + n]


def _conv_fwd(x, halo, w):
    n = x.shape[0]
    ext = jnp.concatenate([halo, x], axis=0)
    y = w[GDN_CONV - 1:GDN_CONV] * x
    for j in range(GDN_CONV - 1):
        y = y + w[j:j + 1] * _shift_rows(ext, GDN_CONV - 1 - j, 8, n)
    return y


def _chunk_masks(n):
    r, s = _iota2((n, n), 0), _iota2((n, n), 1)
    same = (r // GDN_CHUNK) == (s // GDN_CHUNK)
    return (same & (r >= s)).astype(F32), (same & (r <= s)).astype(F32), same.astype(F32)


def _softplus(x):
    return jnp.maximum(x, 0.0) + jnp.log(1.0 + jnp.exp(-jnp.abs(x)))


def _l2n(t):
    rs = lax.rsqrt(jnp.sum(t * t, axis=-1, keepdims=True) + NORM_EPS)
    return t * rs, rs


def gdn_pre_fwd(proj, conv_w, gate_par):
    t = proj.shape[0]
    tm = _tile(t, 256)

    def body(x_ref, halo_ref, ba_ref, w_ref, gp_ref, qkv_ref, gc_ref):
        i = pl.program_id(0)
        halo = jnp.where(i > 0, halo_ref[...], 0.0)
        y = _silu(_conv_fwd(x_ref[...], halo, w_ref[...]))
        for hh in range(2 * GDN_HEADS):
            sl = slice(hh * LANE, (hh + 1) * LANE)
            tn, _ = _l2n(y[:, sl])
            qkv_ref[:, sl] = tn * GDN_Q_SCALE if hh < GDN_HEADS else tn
        qkv_ref[:, 2 * GDN_HEADS * LANE:] = y[:, 2 * GDN_HEADS * LANE:]
        ba = ba_ref[...]
        lane = _iota2(ba.shape, 1)
        gp = gp_ref[...]
        is_a = (lane >= GC_G) & (lane < GC_G + GDN_HEADS)
        g = jnp.where(is_a, -jnp.exp(gp[0:1]) * _softplus(ba + gp[1:2]), 0.0)
        tri, _, same = _chunk_masks(tm)
        decay = pltpu.roll(_dot(tri, g, HI), GC_DECAY - GC_G, 1)
        last = pltpu.roll(_dot(same, g, HI), GC_LAST - GC_G, 1)
        gc_ref[...] = jnp.where(lane < GDN_HEADS, _sigmoid(ba), g) + decay + last

    return pl.pallas_call(
        body, name="gdn_pre_fwd",
        grid=(t // tm,),
        in_specs=[pl.BlockSpec((tm, GDN_QKV_W), lambda i: (i, 0)),
                  pl.BlockSpec((8, GDN_QKV_W), lambda i: (jnp.maximum(i * (tm // 8) - 1, 0), 0)),
                  pl.BlockSpec((tm, LANE), lambda i: (i, (GDN_QKV_W + 1024) // LANE)),
                  pl.BlockSpec((8, GDN_QKV_W), lambda i: (0, 0)),
                  pl.BlockSpec((8, LANE), lambda i: (0, 0))],
        out_specs=[pl.BlockSpec((tm, GDN_QKV_W), lambda i: (i, 0)),
                   pl.BlockSpec((tm, LANE), lambda i: (i, 0))],
        out_shape=[jax.ShapeDtypeStruct((t, GDN_QKV_W), F32), jax.ShapeDtypeStruct((t, LANE), F32)],
        compiler_params=_arb(1),
    )(proj, proj, proj, conv_w, gate_par)


def gdn_pre_bwd(proj, conv_w, gate_par, dq, dk, dv, dgc_heads):
    t = proj.shape[0]
    tm = _tile(t, 256)

    def body(x_ref, halo_ref, ba_ref, w_ref, gp_ref, dq_ref, dk_ref, dv_ref, dgc_ref, dy_ref, dba_ref, dgp_ref):
        i = pl.program_id(0)

        @pl.when(i == 0)
        def _():
            dgp_ref[...] = jnp.zeros_like(dgp_ref)

        halo = jnp.where(i > 0, halo_ref[...], 0.0)
        y = _conv_fwd(x_ref[...], halo, w_ref[...])
        for hh in range(3 * GDN_HEADS):
            sl = slice(hh * LANE, (hh + 1) * LANE)
            hsl = slice((hh % GDN_HEADS) * LANE, (hh % GDN_HEADS + 1) * LANE)
            yy = y[:, sl]
            if hh < 2 * GDN_HEADS:
                tn, rs = _l2n(_silu(yy))
                dtn = dq_ref[:, hsl] * GDN_Q_SCALE if hh < GDN_HEADS else dk_ref[:, hsl]
                dsil = rs * (dtn - tn * jnp.sum(dtn * tn, axis=-1, keepdims=True))
            else:
                dsil = dv_ref[:, hsl]
            dy_ref[:, sl] = dsil * _silu_grad(yy)
        dgc = dgc_ref[0]
        for hh in range(1, dgc_heads.shape[0]):
            dgc = dgc + dgc_ref[hh]
        ba = ba_ref[...]
        lane = _iota2(ba.shape, 1)
        _, tri_t, same = _chunk_masks(tm)
        d_decay = jnp.where((lane >= GC_DECAY) & (lane < GC_DECAY + GDN_HEADS), dgc, 0.0)
        d_last = jnp.where((lane >= GC_LAST) & (lane < GC_LAST + GDN_HEADS), dgc, 0.0)
        dgc = (jnp.where(lane < GDN_HEADS, dgc, 0.0) + pltpu.roll(_dot(tri_t, d_decay, HI), LANE - (GC_DECAY - GC_G), 1)
               + pltpu.roll(_dot(same, d_last, HI), LANE - (GC_LAST - GC_G), 1))
        gp = gp_ref[...]
        xg = ba + gp[1:2]
        ea = jnp.exp(gp[0:1])
        sp = _softplus(xg)
        sb = _sigmoid(ba)
        is_b = lane < GDN_HEADS
        is_a = (lane >= GDN_HEADS) & (lane < 2 * GDN_HEADS)
        d_pre = jnp.where(is_a, dgc * (-ea) * _sigmoid(xg), 0.0)
        dba_ref[...] = jnp.where(is_b, dgc * sb * (1.0 - sb), d_pre).astype(BF16)
        d_alog = jnp.sum(jnp.where(is_a, dgc * (-ea) * sp, 0.0), axis=0, keepdims=True)
        d_dtb = jnp.sum(d_pre, axis=0, keepdims=True)
        row = _iota2((8, LANE), 0)
        dgp_ref[...] += jnp.where(row == 0, d_alog, jnp.where(row == 1, d_dtb, 0.0))

    hspec = pl.BlockSpec((tm, GDN_HEADS * LANE), lambda i: (i, 0))
    return pl.pallas_call(
        body, name="gdn_pre_bwd",
        grid=(t // tm,),
        in_specs=[pl.BlockSpec((tm, GDN_QKV_W), lambda i: (i, 0)),
                  pl.BlockSpec((8, GDN_QKV_W), lambda i: (jnp.maximum(i * (tm // 8) - 1, 0), 0)),
                  pl.BlockSpec((tm, LANE), lambda i: (i, (GDN_QKV_W + 1024) // LANE)),
                  pl.BlockSpec((8, GDN_QKV_W), lambda i: (0, 0)),
                  pl.BlockSpec((8, LANE), lambda i: (0, 0)),
                  hspec, hspec, hspec,
                  pl.BlockSpec((dgc_heads.shape[0], tm, LANE), lambda i: (0, i, 0))],
        out_specs=[pl.BlockSpec((tm, GDN_QKV_W), lambda i: (i, 0)),
                   pl.BlockSpec((tm, LANE), lambda i: (i, 0)),
                   pl.BlockSpec((8, LANE), lambda i: (0, 0))],
        out_shape=[jax.ShapeDtypeStruct((t, GDN_QKV_W), F32), jax.ShapeDtypeStruct((t, LANE), BF16),
                   jax.ShapeDtypeStruct((8, LANE), F32)],
        compiler_params=_arb(1),
    )(proj, proj, proj, conv_w, gate_par, dq, dk, dv, dgc_heads)


def gdn_conv_bwd(proj, conv_w, dy):
    t = proj.shape[0]
    tm = _tile(t, 256)
    nblk = t // tm

    def body(x_ref, halo_ref, w_ref, dy_ref, dyn_ref, dx_ref, dw_ref):
        i = pl.program_id(0)

        @pl.when(i == 0)
        def _():
            dw_ref[...] = jnp.zeros_like(dw_ref)

        w = w_ref[...]
        dy = dy_ref[...]
        ext_dy = jnp.concatenate([dy, jnp.where(i < nblk - 1, dyn_ref[...], 0.0)], axis=0)
        ext_x = jnp.concatenate([jnp.where(i > 0, halo_ref[...], 0.0), x_ref[...]], axis=0)
        dx = w[GDN_CONV - 1:GDN_CONV] * dy
        rows = [jnp.sum(dy * x_ref[...], axis=0, keepdims=True)]
        for j in range(GDN_CONV - 1):
            sh = GDN_CONV - 1 - j
            dx = dx + w[j:j + 1] * _shift_rows(ext_dy, tm + 8 - sh, 0, tm)
            rows.insert(j, jnp.sum(dy * _shift_rows(ext_x, sh, 8, tm), axis=0, keepdims=True))
        dx_ref[...] = dx.astype(BF16)
        row = _iota2((8, GDN_QKV_W), 0)
        acc = jnp.zeros((8, GDN_QKV_W), F32)
        for j in range(GDN_CONV):
            acc = acc + jnp.where(row == j, rows[j], 0.0)
        dw_ref[...] += acc

    return pl.pallas_call(
        body, name="gdn_conv_bwd",
        grid=(nblk,),
        in_specs=[pl.BlockSpec((tm, GDN_QKV_W), lambda i: (i, 0)),
                  pl.BlockSpec((8, GDN_QKV_W), lambda i: (jnp.maximum(i * (tm // 8) - 1, 0), 0)),
                  pl.BlockSpec((8, GDN_QKV_W), lambda i: (0, 0)),
                  pl.BlockSpec((tm, GDN_QKV_W), lambda i: (i, 0)),
                  pl.BlockSpec((8, GDN_QKV_W), lambda i: (jnp.minimum((i + 1) * (tm // 8), t // 8 - 1), 0))],
        out_specs=[pl.BlockSpec((tm, GDN_QKV_W), lambda i: (i, 0)),
                   pl.BlockSpec((8, GDN_QKV_W), lambda i: (0, 0))],
        out_shape=[jax.ShapeDtypeStruct((t, GDN_QKV_W), BF16), jax.ShapeDtypeStruct((8, GDN_QKV_W), F32)],
        compiler_params=_arb(1),
    )(proj, proj, conv_w, dy, dy)


def _resident(w_hbm, w_vmem, sem):
    @pl.when(pl.program_id(0) == 0)
    def _():
        cp = pltpu.make_async_copy(w_hbm, w_vmem, sem)
        cp.start()
        cp.wait()


ANY = pl.BlockSpec(memory_space=pl.ANY)


def norm_proj(h, nw, w, bias):
    t, d = h.shape
    n = w.shape[1]
    tm = _tile(t, 256)
    nc = _tile(n, 1536) if n % 1536 == 0 else _tile(n, 1408)

    def body(h_ref, nw_ref, w_hbm, b_ref, o_ref, w_ref, sem):
        _resident(w_hbm, w_ref, sem)
        hn = _rms(h_ref[...], nw_ref[...])[0].astype(BF16)
        for c0 in range(0, n, nc):
            o_ref[:, c0:c0 + nc] = _dot(hn, w_ref[:, c0:c0 + nc]) + b_ref[:, c0:c0 + nc]

    return pl.pallas_call(
        body, name="norm_proj",
        grid=(t // tm,),
        in_specs=[pl.BlockSpec((tm, d), lambda i: (i, 0)), pl.BlockSpec((1, d), lambda i: (0, 0)), ANY,
                  pl.BlockSpec((1, n), lambda i: (0, 0))],
        out_specs=pl.BlockSpec((tm, n), lambda i: (i, 0)),
        out_shape=jax.ShapeDtypeStruct((t, n), F32),
        scratch_shapes=[pltpu.VMEM((d, n), BF16), pltpu.SemaphoreType.DMA],
        compiler_params=_arb(1),
    )(h, nw, w, bias)


def linear_residual(h, x, w, bias):
    t, d = h.shape
    k = x.shape[1]
    tm = _tile(t, 512)

    def body(h_ref, x_ref, w_hbm, b_ref, o_ref, w_ref, sem):
        _resident(w_hbm, w_ref, sem)
        o_ref[...] = h_ref[...] + _dot(x_ref[...], w_ref[...]) + b_ref[...]

    return pl.pallas_call(
        body, name="linear_residual",
        grid=(t // tm,),
        in_specs=[pl.BlockSpec((tm, d), lambda i: (i, 0)), pl.BlockSpec((tm, k), lambda i: (i, 0)), ANY,
                  pl.BlockSpec((1, d), lambda i: (0, 0))],
        out_specs=pl.BlockSpec((tm, d), lambda i: (i, 0)),
        out_shape=jax.ShapeDtypeStruct((t, d), F32),
        scratch_shapes=[pltpu.VMEM((k, d), BF16), pltpu.SemaphoreType.DMA],
        compiler_params=_arb(1),
    )(h, x, w, bias)


def matmul_nt(dy, w):
    t, d = dy.shape
    k = w.shape[0]
    tm = _tile(t, 512)

    def body(dy_ref, w_hbm, o_ref, cs_ref, w_ref, sem):
        _resident(w_hbm, w_ref, sem)

        @pl.when(pl.program_id(0) == 0)
        def _():
            cs_ref[...] = jnp.zeros_like(cs_ref)

        dy = dy_ref[...]
        cs_ref[...] += jnp.sum(dy, axis=0, keepdims=True)
        o_ref[...] = _dot_nt(dy.astype(BF16), w_ref[...])

    return pl.pallas_call(
        body, name="matmul_nt",
        grid=(t // tm,),
        in_specs=[pl.BlockSpec((tm, d), lambda i: (i, 0)), ANY],
        out_specs=[pl.BlockSpec((tm, k), lambda i: (i, 0)), pl.BlockSpec((1, d), lambda i: (0, 0))],
        out_shape=[jax.ShapeDtypeStruct((t, k), F32), jax.ShapeDtypeStruct((1, d), F32)],
        scratch_shapes=[pltpu.VMEM((k, d), BF16), pltpu.SemaphoreType.DMA],
        compiler_params=_arb(1),
    )(dy, w)


def norm_proj_bwd(h, nw, dh, dps, ws):
    t, d = h.shape
    np_ = len(dps)
    ns = [w.shape[1] for w in ws]
    tm = _tile(t, 256)

    def body(*refs):
        h_ref, nw_ref, dh_ref = refs[:3]
        dp_refs = refs[3:3 + np_]
        w_hbms = refs[3 + np_:3 + 2 * np_]
        o_ref, hn_ref, dnw_ref = refs[3 + 2 * np_:6 + 2 * np_]
        cs_refs = refs[6 + 2 * np_:6 + 3 * np_]
        w_refs = refs[6 + 3 * np_:6 + 4 * np_]
        sem = refs[6 + 4 * np_]
        for a, b in zip(w_hbms, w_refs):
            _resident(a, b, sem)

        @pl.when(pl.program_id(0) == 0)
        def _():
            dnw_ref[...] = jnp.zeros_like(dnw_ref)
            for c in cs_refs:
                c[...] = jnp.zeros_like(c)

        nw = nw_ref[...]
        hn, xhat, r = _rms(h_ref[...], nw)
        hn_ref[...] = hn.astype(BF16)
        dhn = jnp.zeros((tm, d), F32)
        for dp_ref, w_ref, cs_ref in zip(dp_refs, w_refs, cs_refs):
            dp = dp_ref[...]
            cs_ref[...] += jnp.sum(dp.astype(F32), axis=0, keepdims=True)
            dhn = dhn + _dot_nt(dp, w_ref[...])
        dx, dnw = _rms_bwd(dhn, nw, xhat, r)
        dnw_ref[...] += dnw
        o_ref[...] = dh_ref[...] + dx

    row = pl.BlockSpec((tm, d), lambda i: (i, 0))
    vec = pl.BlockSpec((1, d), lambda i: (0, 0))
    return pl.pallas_call(
        body, name="norm_proj_bwd",
        grid=(t // tm,),
        in_specs=[row, vec, row] + [pl.BlockSpec((tm, n), lambda i: (i, 0)) for n in ns] + [ANY] * np_,
        out_specs=[row, row, vec] + [pl.BlockSpec((1, n), lambda i: (0, 0)) for n in ns],
        out_shape=[jax.ShapeDtypeStruct((t, d), F32), jax.ShapeDtypeStruct((t, d), BF16),
                   jax.ShapeDtypeStruct((1, d), F32)] + [jax.ShapeDtypeStruct((1, n), F32) for n in ns],
        scratch_shapes=[pltpu.VMEM((d, n), BF16) for n in ns] + [pltpu.SemaphoreType.DMA],
        compiler_params=_arb(1),
    )(h, nw, dh, *dps, *ws)


def _pair_copies(g_refs, got_refs, send_sems, recv_sems):
    mx, my, mc = _me()
    copies = []
    for g_ref, got_ref in zip(g_refs, got_refs):
        for j in range(4):
            i = len(copies)
            copies.append(pltpu.make_async_remote_copy(
                src_ref=g_ref.at[j, 1 - mc], dst_ref=got_ref.at[j], send_sem=send_sems.at[i], recv_sem=recv_sems.at[i],
                device_id=(mx, my, 1 - mc), device_id_type=MESH))
    return copies


def _scatter_copies(p_refs, q_refs, send_sems, recv_sems):
    mx, my, mc = _me()
    copies = []
    for p_ref, q_ref in zip(p_refs, q_refs):
        for k in (1, 2, 3):
            px, py = _flip(mx, k >> 1), _flip(my, k & 1)
            i = len(copies)
            copies.append(pltpu.make_async_remote_copy(
                src_ref=p_ref.at[2 * px + py], dst_ref=q_ref.at[2 * mx + my], send_sem=send_sems.at[i],
                recv_sem=recv_sems.at[i], device_id=(px, py, mc), device_id_type=MESH))
    return copies


def _share_copies(o_refs, out_refs, send_sems, recv_sems):
    mx, my, mc = _me()
    return [pltpu.make_async_remote_copy(src_ref=o_ref, dst_ref=out_ref, send_sem=send_sems.at[i], recv_sem=recv_sems.at[i],
                                         device_id=(mx, my, 1 - mc), device_id_type=MESH)
            for i, (o_ref, out_ref) in enumerate(zip(o_refs, out_refs))]


CARRIED = {"pair": (_pair_copies, 4, lambda a: a.shape[:1] + a.shape[2:]), "scatter": (_scatter_copies, 3, lambda a: a.shape),
           "share": (_share_copies, 1, lambda a: a.shape)}


def matmul_tn(x, y, scale=1.0, carry=None):
    t, k = x.shape
    n = y.shape[1]
    tk = _tile(k, 1024) if k % 1024 == 0 else _tile(k, 1408)
    tn = n if n <= 1536 else (1024 if n % 1024 == 0 else 1408)
    assert n % tn == 0
    tt = _tile(t, 2048)
    grid = (k // tk, n // tn, t // tt)
    arrays = [] if carry is None else list(carry[1])
    na = len(arrays)

    def body(x_ref, y_ref, *rest):
        o_ref = rest[na]
        if carry is not None:
            make, _, _ = CARRIED[carry[0]]
            copies = lambda: make(rest[:na], rest[na + 1:2 * na + 1], rest[2 * na + 1], rest[2 * na + 2])
            pid = [pl.program_id(a) for a in range(3)]

            @pl.when((pid[0] == 0) & (pid[1] == 0) & (pid[2] == 0))
            def _():
                for cp in copies():
                    cp.start()

        @pl.when(pl.program_id(2) == 0)
        def _():
            o_ref[...] = jnp.zeros_like(o_ref)

        yv = y_ref[...]
        if scale != 1.0:
            yv = yv * scale
        o_ref[...] += _dot_tn(x_ref[...].astype(BF16), yv.astype(BF16))

        if carry is not None:
            @pl.when((pid[0] == grid[0] - 1) & (pid[1] == grid[1] - 1) & (pid[2] == grid[2] - 1))
            def _():
                for cp in copies():
                    cp.wait()

    out_shape = [jax.ShapeDtypeStruct((k, n), F32)]
    scratch = []
    if carry is not None:
        _, per_array, out_of = CARRIED[carry[0]]
        out_shape += [jax.ShapeDtypeStruct(out_of(a), a.dtype) for a in arrays]
        scratch = [pltpu.SemaphoreType.DMA((per_array * na,))] * 2
    res = pl.pallas_call(
        body, name="matmul_tn" if carry is None else "matmul_tn_" + carry[0],
        grid=grid,
        in_specs=[pl.BlockSpec((tt, tk), lambda i, j, s: (s, i)), pl.BlockSpec((tt, tn), lambda i, j, s: (s, j))] + [ANY] * na,
        out_specs=[pl.BlockSpec((tk, tn), lambda i, j, s: (i, j))] + [ANY] * na,
        out_shape=out_shape, scratch_shapes=scratch,
        compiler_params=_arb(3),
    )(x, y, *arrays)
    return res[0] if carry is None else (res[0], list(res[1:]))


FFN_CHUNKS = 2


class _FfnGather:
    def __init__(self, gu_sh, d_sh, gu_full, d_full, send_sems, recv_sems, local_sems):
        self.sh, self.full = (gu_sh, d_sh), (gu_full, d_full)
        self.send_sems, self.recv_sems, self.local_sems = send_sems, recv_sems, local_sems
        self.mx, self.my, self.mc = _me()
        self.me = 2 * self.mx + self.my
        self.w4, self.f4 = gu_sh.shape[1], d_sh.shape[0]
        self.hg, self.hd = gu_sh.shape[0] // 2, d_sh.shape[0] // 2

    def _window(self, a, chip, core=None):
        gu_full, d_full = self.full
        if a == 0:
            rows = pl.ds(0, 2 * self.hg) if core is None else pl.ds(pl.multiple_of(core * self.hg, 16), self.hg)
            return gu_full.at[rows, pl.ds(pl.multiple_of(chip * self.w4, LANE), self.w4)]
        if core is None:
            return d_full.at[pl.ds(pl.multiple_of(chip * self.f4, 16), self.f4), :]
        return d_full.at[pl.ds(pl.multiple_of(chip * self.f4 + core * self.hd, 16), self.hd), :]

    def _half(self, a):
        n = (self.hg, self.hd)[a]
        return self.sh[a].at[pl.ds(pl.multiple_of(self.mc * n, 16), n), :]

    def _peer(self, k):
        return _flip(self.mx, k >> 1), _flip(self.my, k & 1)

    def _fetch(self, k, a):
        px, py = self._peer(k)
        i = 2 * (k - 1) + a
        return pltpu.make_async_remote_copy(src_ref=self._half(a), dst_ref=self._window(a, self.me, self.mc),
                                            send_sem=self.send_sems.at[i], recv_sem=self.recv_sems.at[i],
                                            device_id=(px, py, self.mc), device_id_type=MESH)

    def _relay(self, k, a):
        px, py = self._peer(k)
        got = self._window(a, 2 * px + py, self.mc)
        i = 6 + 2 * (k - 1) + a
        return pltpu.make_async_remote_copy(src_ref=got, dst_ref=got, send_sem=self.send_sems.at[i],
                                            recv_sem=self.recv_sems.at[i], device_id=(self.mx, self.my, 1 - self.mc),
                                            device_id_type=MESH)

    def _own(self, a):
        return pltpu.make_async_copy(self.sh[a], self._window(a, self.me), self.local_sems.at[a])

    def start(self):
        for a in (0, 1):
            self._own(a).start()
            for k in (1, 2, 3):
                self._fetch(k, a).start()

    def relay(self):
        for k in (1, 2, 3):
            for a in (0, 1):
                self._fetch(k, a).wait_recv()
                self._relay(k, a).start()

    def finish(self):
        for k in (1, 2, 3):
            for a in (0, 1):
                self._fetch(k, a).wait_send()
                self._relay(k, a).wait()
        for a in (0, 1):
            self._own(a).wait()


def ffn_fwd(h, nw, wgu, wd, nxt=None, also=()):
    t, d = h.shape
    f = wd.shape[0]
    fc = f // FFN_CHUNKS
    tm = _tile(t, 512)
    nsteps = t // tm
    also = list(also)
    na = len(also)

    def body(h_ref, nw_ref, wgu_hbm, wd_hbm, *rest):
        gathers = []
        if nxt is None:
            o_ref, gu_ref, a_ref, wgu_ref, wd_ref, sem = rest
        else:
            gu_sh, d_sh = rest[:2]
            o_ref, gu_ref, a_ref, gu_full, d_full = rest[2 + na:7 + na]
            wgu_ref, wd_ref, sem, send_sems, recv_sems, local_sems = rest[7 + 2 * na:13 + 2 * na]
            gathers.append(_FfnGather(gu_sh, d_sh, gu_full, d_full, send_sems, recv_sems, local_sems))
            if na:
                gathers.append(_SplitGather(rest[2:2 + na], rest[7 + na:7 + 2 * na], *rest[13 + 2 * na:]))

        @pl.when(pl.program_id(0) == 0)
        def _():
            for gather in gathers:
                gather.start()

        _resident(wgu_hbm, wgu_ref, sem)
        _resident(wd_hbm, wd_ref, sem)

        @pl.when(pl.program_id(0) == nsteps // 2)
        def _():
            for gather in gathers:
                gather.relay()

        x = h_ref[...]
        hn = _rms(x, nw_ref[...])[0].astype(BF16)
        acc = jnp.zeros((tm, d), F32)
        for c in range(FFN_CHUNKS):
            gs, us = slice(c * fc, (c + 1) * fc), slice(f + c * fc, f + (c + 1) * fc)
            g = _dot(hn, wgu_ref[:, gs])
            u = _dot(hn, wgu_ref[:, us])
            a = (_silu(g) * u).astype(BF16)
            gu_ref[:, gs] = g.astype(BF16)
            gu_ref[:, us] = u.astype(BF16)
            a_ref[:, gs] = a
            acc = acc + _dot(a, wd_ref[gs, :])
        o_ref[...] = x + 0.5 * acc

        @pl.when(pl.program_id(0) == nsteps - 1)
        def _():
            for gather in gathers:
                gather.finish()

    row = lambda w: pl.BlockSpec((tm, w), lambda i: (i, 0))
    in_specs = [row(d), pl.BlockSpec((1, d), lambda i: (0, 0)), ANY, ANY]
    out_specs = [row(d), row(2 * f), row(f)]
    out_shape = [jax.ShapeDtypeStruct((t, d), F32), jax.ShapeDtypeStruct((t, 2 * f), BF16),
                 jax.ShapeDtypeStruct((t, f), BF16)]
    scratch = [pltpu.VMEM((d, 2 * f), BF16), pltpu.VMEM((f, d), BF16), pltpu.SemaphoreType.DMA]
    args = (h, nw, wgu, wd)
    if nxt is not None:
        in_specs += [ANY] * (2 + na)
        out_specs += [ANY] * (2 + na)
        out_shape += [jax.ShapeDtypeStruct((d, 2 * f), BF16), jax.ShapeDtypeStruct((f, d), BF16)]
        out_shape += [jax.ShapeDtypeStruct((4,) + a.shape, a.dtype) for a in also]
        scratch += [pltpu.SemaphoreType.DMA((12,)), pltpu.SemaphoreType.DMA((12,)), pltpu.SemaphoreType.DMA((2,))]
        if na:
            scratch += [pltpu.SemaphoreType.DMA((6 * na,)), pltpu.SemaphoreType.DMA((6 * na,))]
        args += tuple(nxt) + tuple(also)
    return pl.pallas_call(
        body, name="ffn_fwd_gather" if nxt is not None else "ffn_fwd",
        grid=(nsteps,),
        in_specs=in_specs, out_specs=out_specs, out_shape=out_shape, scratch_shapes=scratch,
        compiler_params=_arb(1),
    )(*args)


def ffn_bwd(h, nw, wgu, wd, gu, dh):
    t, d = h.shape
    f = wd.shape[0]
    fc = f // FFN_CHUNKS
    tm = _tile(t, 256)

    def body(h_ref, nw_ref, wgu_hbm, wd_hbm, gu_ref, dh_ref, o_ref, hn_ref, dgu_ref, dnw_ref, wgu_ref, wd_ref, sem):
        _resident(wgu_hbm, wgu_ref, sem)
        _resident(wd_hbm, wd_ref, sem)

        @pl.when(pl.program_id(0) == 0)
        def _():
            dnw_ref[...] = jnp.zeros_like(dnw_ref)

        nw = nw_ref[...]
        hn32, xhat, r = _rms(h_ref[...], nw)
        hn_ref[...] = hn32.astype(BF16)
        dh = dh_ref[...]
        dout = (0.5 * dh).astype(BF16)
        dhn = jnp.zeros((tm, d), F32)
        for c in range(FFN_CHUNKS):
            gs, us = slice(c * fc, (c + 1) * fc), slice(f + c * fc, f + (c + 1) * fc)
            g = gu_ref[:, gs].astype(F32)
            u = gu_ref[:, us].astype(F32)
            sg = _sigmoid(g)
            sil = g * sg
            da = _dot_nt(dout, wd_ref[gs, :])
            dg = (da * u * (sg * (1.0 + g * (1.0 - sg)))).astype(BF16)
            du = (da * sil).astype(BF16)
            dgu_ref[:, gs] = dg
            dgu_ref[:, us] = du
            dhn = dhn + _dot_nt(dg, wgu_ref[:, gs]) + _dot_nt(du, wgu_ref[:, us])
        dx, dnw = _rms_bwd(dhn, nw, xhat, r)
        dnw_ref[...] += dnw
        o_ref[...] = dh + dx

    row = pl.BlockSpec((tm, d), lambda i: (i, 0))
    vec = pl.BlockSpec((1, d), lambda i: (0, 0))
    return pl.pallas_call(
        body, name="ffn_bwd",
        grid=(t // tm,),
        in_specs=[row, vec, ANY, ANY, pl.BlockSpec((tm, 2 * f), lambda i: (i, 0)), row],
        out_specs=[row, row, pl.BlockSpec((tm, 2 * f), lambda i: (i, 0)), vec],
        out_shape=[jax.ShapeDtypeStruct((t, d), F32), jax.ShapeDtypeStruct((t, d), BF16),
                   jax.ShapeDtypeStruct((t, 2 * f), BF16), jax.ShapeDtypeStruct((1, d), F32)],
        scratch_shapes=[pltpu.VMEM((d, 2 * f), BF16), pltpu.VMEM((f, d), BF16), pltpu.SemaphoreType.DMA],
        compiler_params=_arb(1),
    )(h, nw, wgu, wd, gu, dh)


def loss_head(h, nw, target):
    t, d = h.shape
    tm = _tile(t, 512)

    def body(h_ref, nw_ref, tg_ref, dh_ref, loss_ref, dnw_ref):
        @pl.when(pl.program_id(0) == 0)
        def _():
            loss_ref[...] = jnp.zeros_like(loss_ref)
            dnw_ref[...] = jnp.zeros_like(dnw_ref)

        nw = nw_ref[...]
        y, xhat, r = _rms(h_ref[...], nw)
        e = y - tg_ref[...]
        loss_ref[...] += 0.5 * jnp.sum(jnp.mean(e * e, axis=-1, keepdims=True), axis=0, keepdims=True)
        dx, dnw = _rms_bwd(e * (1.0 / d), nw, xhat, r)
        dnw_ref[...] += dnw
        dh_ref[...] = dx

    row = pl.BlockSpec((tm, d), lambda i: (i, 0))
    vec = pl.BlockSpec((1, d), lambda i: (0, 0))
    return pl.pallas_call(
        body, name="loss_head",
        grid=(t // tm,),
        in_specs=[row, vec, row],
        out_specs=[row, pl.BlockSpec((8, LANE), lambda i: (0, 0)), vec],
        out_shape=[jax.ShapeDtypeStruct((t, d), F32), jax.ShapeDtypeStruct((8, LANE), F32),
                   jax.ShapeDtypeStruct((1, d), F32)],
        compiler_params=_arb(1),
    )(h, nw, target)


def adamw(w, g, m, v):
    r, c = w.shape
    tr = r
    while tr * c * 4 > (1 << 20) and tr % 16 == 0:
        tr //= 2

    def body(w_ref, g_ref, m_ref, v_ref, d_ref, nm_ref, nv_ref):
        g = g_ref[...]
        m = ADAM_B1 * m_ref[...] + (1.0 - ADAM_B1) * g
        v = ADAM_B2 * v_ref[...] + (1.0 - ADAM_B2) * (g * g)
        m_hat = m / (1.0 - ADAM_B1 ** ADAM_STEP)
        v_hat = v / (1.0 - ADAM_B2 ** ADAM_STEP)
        d_ref[...] = -ADAM_LR * (m_hat / (jnp.sqrt(v_hat) + ADAM_EPS) + ADAM_WD * w_ref[...])
        nm_ref[...] = m
        nv_ref[...] = v

    blk = pl.BlockSpec((tr, c), lambda i: (i, 0))
    return pl.pallas_call(
        body, name="adamw",
        grid=(r // tr,),
        in_specs=[blk] * 4, out_specs=[blk] * 3,
        out_shape=[jax.ShapeDtypeStruct((r, c), F32)] * 3,
        compiler_params=_arb(1),
    )(w, g, m, v)


ATTN_P_W = ATTN_Q_W + 2 * ATTN_KV_W
ATTN_SCALE = ATTN_HEAD_DIM ** -0.5


def _rope_group(t, tab, sign):
    return (t * tab[:, 0:LANE] + sign * pltpu.roll(t, 8, 1) * tab[:, LANE:2 * LANE]
            + sign * pltpu.roll(t, LANE - 8, 1) * tab[:, 2 * LANE:3 * LANE])


def _rope(t, tab, sign=1.0):
    return jnp.concatenate([_rope_group(t[:, s:s + LANE], tab, sign) for s in range(0, t.shape[1], LANE)], axis=1)


def _attn_heads(qs, ks, vs, sinks, first):
    b = ATTN_BLOCK
    rows = ATTN_GROUP * b
    qi = _iota2((rows, 2 * b), 0) % b
    kj = _iota2((rows, 2 * b), 1)
    rel = qi + b - kj
    valid = (rel >= 0) & (rel < b) & ((kj >= b) | jnp.logical_not(first))
    ss = _each(lambda q, k: jnp.where(valid, _mm1_nt(q, k) * ATTN_SCALE, NEG), qs, ks)
    ms = _each(lambda s, sink: lax.stop_gradient(jnp.maximum(jnp.max(s, axis=-1, keepdims=True), sink)), ss, sinks)
    ps = _each(lambda s, m: jnp.exp(s - m), ss, ms)
    dens = _each(lambda p, sink, m: jnp.sum(p, axis=-1, keepdims=True) + jnp.exp(sink - m), ps, sinks, ms)
    return _each(lambda p, den, v: _mm1(p / den, v), ps, dens, vs)


def _attn_prepare(p_ref, kvp_ref, tab_ref, tabp_ref, sink_ref):
    b = ATTN_BLOCK
    hd = ATTN_HEAD_DIM
    tab, tabp = tab_ref[...], tabp_ref[...]
    q = _rope(p_ref[:, 0:ATTN_Q_W], tab)
    kc = _rope(p_ref[:, ATTN_Q_W:ATTN_Q_W + ATTN_KV_W], tab)
    kp = _rope(kvp_ref[:, 0:ATTN_KV_W], tabp)
    vc = p_ref[:, ATTN_Q_W + ATTN_KV_W:ATTN_P_W]
    vp = kvp_ref[:, ATTN_KV_W:2 * ATTN_KV_W]
    sk = sink_ref[...]
    qs, ks, vs, sinks = [], [], [], []
    for h in range(ATTN_KV_HEADS):
        heads = [ATTN_GROUP * h + g for g in range(ATTN_GROUP)]
        qs.append(jnp.concatenate([q[:, i * hd:(i + 1) * hd] for i in heads], axis=0))
        ks.append(jnp.concatenate([kp[:, h * hd:(h + 1) * hd], kc[:, h * hd:(h + 1) * hd]], axis=0))
        vs.append(jnp.concatenate([vp[:, h * hd:(h + 1) * hd], vc[:, h * hd:(h + 1) * hd]], axis=0))
        sinks.append(jnp.concatenate([jnp.broadcast_to(sk[:, i:i + 1], (b, 1)) for i in heads], axis=0))
    return qs, ks, vs, sinks


def _unstack_heads(xs):
    b = ATTN_BLOCK
    return jnp.concatenate([x[g * b:(g + 1) * b] for x in xs for g in range(ATTN_GROUP)], axis=1)


def _attn_specs(nb):
    b = ATTN_BLOCK
    prev = lambda n: jnp.maximum(n - 1, 0)
    return [pl.BlockSpec((b, ATTN_P_W), lambda n: (n, 0)),
            pl.BlockSpec((b, 2 * ATTN_KV_W), lambda n: (prev(n), ATTN_Q_W // (2 * ATTN_KV_W))),
            pl.BlockSpec((b, 3 * LANE), lambda n: (n, 0)),
            pl.BlockSpec((b, 3 * LANE), lambda n: (prev(n), 0)),
            pl.BlockSpec((1, ATTN_Q_HEADS), lambda n: (0, 0))]


def attn_fwd(p, tab, sinks):
    t = p.shape[0]
    nb = t // ATTN_BLOCK

    def body(p_ref, kvp_ref, tab_ref, tabp_ref, sink_ref, o_ref):
        qs, ks, vs, sk = _attn_prepare(p_ref, kvp_ref, tab_ref, tabp_ref, sink_ref)
        os_ = _attn_heads(qs, ks, vs, sk, pl.program_id(0) == 0)
        o_ref[...] = _unstack_heads(os_).astype(BF16)

    return pl.pallas_call(
        body, name="attn_fwd",
        grid=(nb,),
        in_specs=_attn_specs(nb),
        out_specs=pl.BlockSpec((ATTN_BLOCK, ATTN_Q_W), lambda n: (n, 0)),
        out_shape=jax.ShapeDtypeStruct((t, ATTN_Q_W), BF16),
        compiler_params=_arb(1),
    )(p, p, tab, tab, sinks)


def attn_bwd(p, tab, sinks, do):
    t = p.shape[0]
    b = ATTN_BLOCK
    hd = ATTN_HEAD_DIM
    nb = t // b

    def body(p_ref, kvp_ref, tab_ref, tabp_ref, sink_ref, do_ref, dq_ref, dkvc_ref, dkvp_ref, dsink_ref):
        n = pl.program_id(0)

        @pl.when(n == 0)
        def _():
            dsink_ref[...] = jnp.zeros_like(dsink_ref)

        qs, ks, vs, sk = _attn_prepare(p_ref, kvp_ref, tab_ref, tabp_ref, sink_ref)
        first = n == 0
        _, vjp = jax.vjp(lambda a, bb, c, d: _attn_heads(a, bb, c, d, first), qs, ks, vs, sk)
        do = do_ref[...]
        dos = [jnp.concatenate([do[:, i * hd:(i + 1) * hd] for i in range(ATTN_GROUP * h, ATTN_GROUP * (h + 1))], axis=0)
               for h in range(ATTN_KV_HEADS)]
        dqs, dks, dvs, dsk = vjp(dos)
        tab, tabp = tab_ref[...], tabp_ref[...]
        dq_ref[...] = _rope(_unstack_heads([x.astype(F32) for x in dqs]), tab, -1.0).astype(BF16)
        dkc = jnp.concatenate([x.astype(F32)[b:] for x in dks], axis=1)
        dkp = jnp.concatenate([x.astype(F32)[:b] for x in dks], axis=1)
        dkvc_ref[:, 0:ATTN_KV_W] = _rope(dkc, tab, -1.0)
        dkvp_ref[:, 0:ATTN_KV_W] = _rope(dkp, tabp, -1.0)
        dkvc_ref[:, ATTN_KV_W:] = jnp.concatenate([x.astype(F32)[b:] for x in dvs], axis=1)
        dkvp_ref[:, ATTN_KV_W:] = jnp.concatenate([x.astype(F32)[:b] for x in dvs], axis=1)
        parts = [jnp.sum(d[g * b:(g + 1) * b], axis=0, keepdims=True) for d in dsk for g in range(ATTN_GROUP)]
        dsink_ref[...] += jnp.concatenate(parts, axis=1)

    blk = lambda w: pl.BlockSpec((b, w), lambda n: (n, 0))
    return pl.pallas_call(
        body, name="attn_bwd",
        grid=(nb,),
        in_specs=_attn_specs(nb) + [blk(ATTN_Q_W)],
        out_specs=[blk(ATTN_Q_W), blk(2 * ATTN_KV_W), blk(2 * ATTN_KV_W),
                   pl.BlockSpec((1, ATTN_Q_HEADS), lambda n: (0, 0))],
        out_shape=[jax.ShapeDtypeStruct((t, ATTN_Q_W), BF16), jax.ShapeDtypeStruct((t, 2 * ATTN_KV_W), F32),
                   jax.ShapeDtypeStruct((t, 2 * ATTN_KV_W), F32), jax.ShapeDtypeStruct((1, ATTN_Q_HEADS), F32)],
        compiler_params=_arb(1),
    )(p, p, tab, tab, sinks, do)


def kv_combine(dkvc, dkvp):
    t, w = dkvc.shape
    b = ATTN_BLOCK
    nb = t // b

    def body(c_ref, p_ref, o_ref):
        nxt = jnp.where(pl.program_id(0) < nb - 1, p_ref[...], 0.0)
        o_ref[...] = (c_ref[...] + nxt).astype(BF16)

    return pl.pallas_call(
        body, name="kv_combine",
        grid=(nb,),
        in_specs=[pl.BlockSpec((b, w), lambda n: (n, 0)),
                  pl.BlockSpec((b, w), lambda n: (jnp.minimum(n + 1, nb - 1), 0))],
        out_specs=pl.BlockSpec((b, w), lambda n: (n, 0)),
        out_shape=jax.ShapeDtypeStruct((t, w), BF16),
        compiler_params=_arb(1),
    )(dkvc, dkvp)


def _me():
    return lax.axis_index("x"), lax.axis_index("y"), lax.axis_index("c")


def _flip(v, bit):
    return 1 - v if bit else v


def _chip_index():
    return 2 * lax.axis_index("x") + lax.axis_index("y")


class _SplitGather:
    def __init__(self, x_refs, o_refs, send_sems, recv_sems):
        self.x_refs, self.o_refs, self.send_sems, self.recv_sems = x_refs, o_refs, send_sems, recv_sems
        self.mx, self.my, self.mc = _me()
        self.me = 2 * self.mx + self.my

    def _rows(self, a):
        rh = self.x_refs[a].shape[0] // 2
        return pl.ds(pl.multiple_of(self.mc * rh, 16), rh)

    def _fetch(self, a, k):
        rows, i = self._rows(a), 6 * a + k - 1
        return pltpu.make_async_remote_copy(
            src_ref=self.x_refs[a].at[rows], dst_ref=self.o_refs[a].at[self.me, rows], send_sem=self.send_sems.at[i],
            recv_sem=self.recv_sems.at[i], device_id=(_flip(self.mx, k >> 1), _flip(self.my, k & 1), self.mc),
            device_id_type=MESH)

    def _relay(self, a, k):
        theirs = self.o_refs[a].at[2 * _flip(self.mx, k >> 1) + _flip(self.my, k & 1), self._rows(a)]
        i = 6 * a + 2 + k
        return pltpu.make_async_remote_copy(src_ref=theirs, dst_ref=theirs, send_sem=self.send_sems.at[i],
                                            recv_sem=self.recv_sems.at[i], device_id=(self.mx, self.my, 1 - self.mc),
                                            device_id_type=MESH)

    def _each_copy(self):
        return [(a, k) for a in range(len(self.x_refs)) for k in (1, 2, 3)]

    def start(self):
        for a, k in self._each_copy():
            self._fetch(a, k).start()

    def relay(self):
        for a, k in self._each_copy():
            self._fetch(a, k).wait_recv()
            self._relay(a, k).start()

    def finish(self):
        for a, k in self._each_copy():
            self._fetch(a, k).wait_send()
            self._relay(a, k).wait()


def _fill_own_slot(got, x):
    return lax.dynamic_update_slice(got, x[None], (_chip_index(), 0, 0))


def allgather_chips(x, split):
    r, c = x.shape

    def body(x_ref, o_ref, send_sems, recv_sems):
        if split:
            gather = _SplitGather([x_ref], [o_ref], send_sems, recv_sems)
            gather.start()
            gather.relay()
            gather.finish()
            return
        mx, my, mc = _me()
        fetched = []
        for k in (1, 2, 3):
            cp = pltpu.make_async_remote_copy(src_ref=x_ref, dst_ref=o_ref.at[2 * mx + my],
                                              send_sem=send_sems.at[k - 1], recv_sem=recv_sems.at[k - 1],
                                              device_id=(_flip(mx, k >> 1), _flip(my, k & 1), mc), device_id_type=MESH)
            cp.start()
            fetched.append(cp)
        for cp in fetched:
            cp.wait()

    got = pl.pallas_call(
        body, name="allgather_chips",
        in_specs=[ANY], out_specs=ANY,
        out_shape=jax.ShapeDtypeStruct((4, r, c), x.dtype),
        scratch_shapes=[pltpu.SemaphoreType.DMA((6,)), pltpu.SemaphoreType.DMA((6,))],
    )(x)
    return _fill_own_slot(got, x)


def exchange(kind, arrays):
    make, per_array, out_of = CARRIED[kind]
    na = len(arrays)

    def body(*refs):
        copies = make(refs[:na], refs[na:2 * na], refs[2 * na], refs[2 * na + 1])
        for cp in copies:
            cp.start()
        for cp in copies:
            cp.wait()

    return pl.pallas_call(
        body, name=kind + "_exchange",
        in_specs=[ANY] * na, out_specs=[ANY] * na,
        out_shape=[jax.ShapeDtypeStruct(out_of(a), a.dtype) for a in arrays],
        scratch_shapes=[pltpu.SemaphoreType.DMA((per_array * na,))] * 2,
    )(*arrays)


def _row_tile(r, c):
    return max(d for d in range(16, r + 1, 16) if r % d == 0 and d * c * 4 <= (2 << 20))


def add_pair(g, got, core):
    _, _, r, c = g.shape
    tr = _row_tile(r, c)

    def body(core_ref, g_ref, got_ref, p32_ref, p16_ref):
        s = g_ref[...] + got_ref[...]
        p32_ref[...] = s
        p16_ref[...] = s.astype(BF16)

    blk = pl.BlockSpec((None, tr, c), lambda j, i, core_ref: (j, i, 0))
    return pl.pallas_call(
        body, name="add_pair",
        grid_spec=pltpu.PrefetchScalarGridSpec(
            num_scalar_prefetch=1, grid=(4, r // tr),
            in_specs=[pl.BlockSpec((None, None, tr, c), lambda j, i, core_ref: (j, core_ref[0], i, 0)), blk],
            out_specs=[blk, blk]),
        out_shape=[jax.ShapeDtypeStruct((4, r, c), F32), jax.ShapeDtypeStruct((4, r, c), BF16)],
        compiler_params=_arb(2),
    )(core, g, got)


def sum_slots(p32, q16, order):
    _, r, c = p32.shape
    tr = _row_tile(r, c)

    def body(order_ref, own_ref, a_ref, b_ref, c_ref, o_ref):
        o_ref[...] = ((own_ref[...] + a_ref[...].astype(F32)) + b_ref[...].astype(F32)) + c_ref[...].astype(F32)

    slot = lambda k: pl.BlockSpec((None, tr, c), functools.partial(lambda k, i, order_ref: (order_ref[k], i, 0), k))
    return pl.pallas_call(
        body, name="sum_slots",
        grid_spec=pltpu.PrefetchScalarGridSpec(
            num_scalar_prefetch=1, grid=(r // tr,),
            in_specs=[slot(0), slot(1), slot(2), slot(3)],
            out_specs=pl.BlockSpec((tr, c), lambda i, order_ref: (i, 0))),
        out_shape=jax.ShapeDtypeStruct((r, c), F32),
        compiler_params=_arb(1),
    )(order, p32, q16, q16, q16)


def allreduce_small(x):
    r, c = x.shape

    def body(x_ref, o_ref, buf, send_sems, recv_sems):
        mx, my, mc = _me()
        me = 4 * mx + 2 * my + mc
        buf[pl.ds(me, 1)] = x_ref[...][None]
        copies = []
        for k in range(1, 8):
            peer = (_flip(mx, k >> 2), _flip(my, (k >> 1) & 1), _flip(mc, k & 1))
            cp = pltpu.make_async_remote_copy(src_ref=x_ref, dst_ref=buf.at[me], send_sem=send_sems.at[k - 1],
                                              recv_sem=recv_sems.at[k - 1], device_id=peer, device_id_type=MESH)
            cp.start()
            copies.append(cp)
        for cp in copies:
            cp.wait()
        acc = buf[0]
        for d in range(1, 8):
            acc = acc + buf[d]
        o_ref[...] = acc

    return pl.pallas_call(
        body, name="allreduce_small",
        out_shape=jax.ShapeDtypeStruct((r, c), F32),
        scratch_shapes=[pltpu.VMEM((8, r, c), F32), pltpu.SemaphoreType.DMA((7,)), pltpu.SemaphoreType.DMA((7,))],
    )(x)


class GradReduce:
    def __init__(self, grads, shard_shapes):
        self.shapes = shard_shapes
        self.groups = _by_width(shard_shapes)
        self.core = lax.axis_index("c")
        me = _chip_index()
        self.order = jnp.stack([me, me ^ 1, me ^ 2, me ^ 3]).astype(jnp.int32)
        self.g = []
        for names in self.groups.values():
            slots = []
            for j in range(4):
                parts = []
                for n in names:
                    w = shard_shapes[n][1 + SHARDED[n]]
                    parts += [lax.slice_in_dim(gl, j * w, (j + 1) * w, axis=SHARDED[n]) for gl in grads[n]]
                slots.append(jnp.concatenate(parts, axis=0))
            g = jnp.stack(slots)
            self.g.append(g.reshape(4, 2, g.shape[1] // 2, g.shape[2]))
        self.p32 = self.p16 = None
        self.q16 = [None] * len(self.g)
        n = len(self.g)
        self.pieces = [list(range(2, n)), [1], [0]] if n > 2 else [[i] for i in reversed(range(n))]
        self.pieces = [p for p in self.pieces if p]
        self.stage = 0

    def next_job(self):
        if self.stage == 0:
            return "pair", self.g
        if self.stage <= len(self.pieces):
            return "scatter", [self.p16[i] for i in self.pieces[self.stage - 1]]
        return None

    def deliver(self, res):
        if self.stage == 0:
            core = self.core.astype(jnp.int32)[None]
            pairs = [add_pair(g, got, core) for g, got in zip(self.g, res)]
            self.p32, self.p16 = [p[0] for p in pairs], [p[1] for p in pairs]
        else:
            for i, q in zip(self.pieces[self.stage - 1], res):
                self.q16[i] = q
        self.stage += 1

    def finish(self):
        while (job := self.next_job()) is not None:
            self.deliver(exchange(*job))
        mine = [sum_slots(p, q, self.order) for p, q in zip(self.p32, self.q16)]
        theirs = exchange("share", mine)
        out = {}
        for names, m, t in zip(self.groups.values(), mine, theirs):
            red = jnp.where(self.core == 0, jnp.concatenate([m, t]), jnp.concatenate([t, m]))
            off = 0
            for n in names:
                layers, a, _ = self.shapes[n]
                out[n] = red[off:off + layers * a].reshape(self.shapes[n])
                off += layers * a
        return out


PACK_W = 1024
PACK_ROWS = 1024

SHARDED = {"ffn1_w_gate_up": 1, "ffn1_w_down": 0, "ffn2_w_gate_up": 1, "ffn2_w_down": 0, "attn_w_in": 1,
           "attn_w_out": 0, "gdn_w_in": 1, "gdn_w_out": 0, "gdn_conv_w": 1}
REPLICATED = ["ffn1_norm", "mix_norm", "ffn2_norm", "attn_b_in", "attn_sinks", "attn_b_out", "gdn_A_log",
              "gdn_dt_bias", "gdn_norm_w", "final_norm"]
WEIGHTS = ["ffn1_norm", "ffn1_w_gate_up", "ffn1_w_down", "mix_norm", "ffn2_norm", "ffn2_w_gate_up", "ffn2_w_down",
           "attn_w_in", "attn_b_in", "attn_sinks", "attn_w_out", "attn_b_out", "gdn_w_in", "gdn_conv_w", "gdn_A_log",
           "gdn_dt_bias", "gdn_norm_w", "gdn_w_out", "final_norm"]


def _pack_rows(flats, dtype, width, row_multiple):
    flat = jnp.concatenate([f.astype(dtype).reshape(-1) for f in flats])
    per = width * row_multiple
    pad = (-flat.shape[0]) % per
    if pad:
        flat = jnp.concatenate([flat, jnp.zeros((pad,), dtype)])
    return flat.reshape(-1, width)


def _by_width(shapes):
    groups = {}
    for n, shape in shapes.items():
        groups.setdefault(shape[-1], []).append(n)
    return groups


def _pack_groups(shards, dtype):
    groups = _by_width({n: s.shape for n, s in shards.items()})
    packed = [jnp.concatenate([shards[n].astype(dtype).reshape(-1, width) for n in names], axis=0)
              for width, names in groups.items()]

    def unpack(gots):
        full = {}
        for names, got in zip(groups.values(), gots):
            off = 0
            for n in names:
                layers, a, _ = shards[n].shape
                full[n] = [jnp.concatenate([got[j, off + l * a:off + (l + 1) * a] for j in range(4)], axis=SHARDED[n])
                           for l in range(layers)]
                off += layers * a
        return full

    return packed, unpack


def _gather_weights(shards, dtype, split=True):
    packed, unpack = _pack_groups(shards, dtype)
    return unpack([allgather_chips(p, split) for p in packed])


def _small_pack(items):
    rows = []
    for a in items:
        f = a.astype(F32).reshape(-1)
        pad = (-f.shape[0]) % LANE
        rows.append(jnp.concatenate([f, jnp.zeros((pad,), F32)]) if pad else f)
    return _pack_rows(rows, F32, LANE, 8)


def _small_unpack(buf, shapes):
    flat = buf.reshape(-1)
    out, off = [], 0
    for shape in shapes:
        size = 1
        for s in shape:
            size *= s
        out.append(flat[off:off + size].reshape(shape))
        off += size + (-size) % LANE
    return out


def _rope_table(positions):
    t = positions.shape[0]
    inv_freq = ROPE_THETA ** (-jnp.arange(0, ROPE_DIM, 2, dtype=F32) / ROPE_DIM)
    ang = positions.astype(F32)[:, None] * inv_freq
    cos, sin = jnp.cos(ang), jnp.sin(ang)
    rest = ATTN_HEAD_DIM - ROPE_DIM
    zeros = lambda n: jnp.zeros((t, n), F32)
    c64 = jnp.concatenate([cos, cos, jnp.ones((t, rest), F32)], axis=1)
    s_up = jnp.concatenate([zeros(ROPE_DIM // 2), sin, zeros(rest)], axis=1)
    s_dn = jnp.concatenate([-sin, zeros(ROPE_DIM // 2 + rest)], axis=1)
    return jnp.concatenate([c64, c64, s_up, s_up, s_dn, s_dn], axis=1)


def _as2d(a):
    return a.reshape(-1, a.shape[-1]) if a.ndim > 1 else a.reshape(1, -1)


def kernel(x, positions, ffn1_norm, ffn1_w_gate_up, ffn1_w_down, mix_norm, ffn2_norm, ffn2_w_gate_up, ffn2_w_down, attn_w_in, attn_b_in, attn_sinks, attn_w_out, attn_b_out, gdn_w_in, gdn_conv_w, gdn_A_log, gdn_dt_bias, gdn_norm_w, gdn_w_out, final_norm, loss_target, m_ffn1_norm, m_ffn1_w_gate_up, m_ffn1_w_down, m_mix_norm, m_ffn2_norm, m_ffn2_w_gate_up, m_ffn2_w_down, m_attn_w_in, m_attn_b_in, m_attn_sinks, m_attn_w_out, m_attn_b_out, m_gdn_w_in, m_gdn_conv_w, m_gdn_A_log, m_gdn_dt_bias, m_gdn_norm_w, m_gdn_w_out, m_final_norm, v_ffn1_norm, v_ffn1_w_gate_up, v_ffn1_w_down, v_mix_norm, v_ffn2_norm, v_ffn2_w_gate_up, v_ffn2_w_down, v_attn_w_in, v_attn_b_in, v_attn_sinks, v_attn_w_out, v_attn_b_out, v_gdn_w_in, v_gdn_conv_w, v_gdn_A_log, v_gdn_dt_bias, v_gdn_norm_w, v_gdn_w_out, v_final_norm):
    given = dict(locals())
    w = {n: given[n] for n in WEIGHTS}
    d = D_MODEL
    h = x[0]
    target = loss_target[0]
    depth = ffn1_norm.shape[0]

    big = [n for n in SHARDED if n != "gdn_conv_w"]
    first = _gather_weights({n: w[n][0:1] for n in ("ffn1_w_gate_up", "ffn1_w_down")}, BF16)
    ffn_order = [(tag, l) for l in range(depth) for tag in ("ffn1", "ffn2")]
    ffn_w = {ffn_order[0]: (first["ffn1_w_gate_up"][0], first["ffn1_w_down"][0])}
    mixers_packed, mixers_unpack = _pack_groups({n: w[n] for n in big if not n.startswith("ffn")}, BF16)
    full = {}

    def run_ffn(tag, l, h):
        i = ffn_order.index((tag, l))
        wgu, wd = ffn_w[(tag, l)]
        nw = w[tag + "_norm"][l][None]
        if i + 1 == len(ffn_order):
            return ffn_fwd(h, nw, wgu, wd)
        ntag, nl = ffn_order[i + 1]
        shards = (w[ntag + "_w_gate_up"][nl].astype(BF16), w[ntag + "_w_down"][nl].astype(BF16))
        h, gu, a, ngu, nd, *gots = ffn_fwd(h, nw, wgu, wd, shards, also=mixers_packed if i == 0 else ())
        ffn_w[(ntag, nl)] = (ngu, nd)
        if i == 0:
            full.update(mixers_unpack([_fill_own_slot(got, x) for got, x in zip(gots, mixers_packed)]))
        return h, gu, a

    conv_full = _gather_weights({"gdn_conv_w": gdn_conv_w}, F32, split=False)["gdn_conv_w"]
    tab = _rope_table(positions[0])
    zero_d = jnp.zeros((1, d), F32)

    def gdn_params(j):
        w_in = full["gdn_w_in"][j]
        w_cat = jnp.concatenate([w_in, jnp.zeros((d, GDN_PROJ_W - w_in.shape[1]), BF16)], axis=1)
        conv = jnp.concatenate([conv_full[j], jnp.zeros((8 - GDN_CONV, GDN_QKV_W), F32)], axis=0)
        lanes = lambda vec: jnp.concatenate([jnp.zeros((GC_G,), F32), vec, jnp.zeros((LANE - GC_G - GDN_HEADS,), F32)])
        par = jnp.concatenate([lanes(gdn_A_log[j])[None], lanes(gdn_dt_bias[j])[None], jnp.zeros((6, LANE), F32)], axis=0)
        return w_cat, conv, par

    saved = []
    for l in range(depth):
        j = l // 2
        rec = {"h1": h}
        h, rec["ffn1_gu"], rec["ffn1_a"] = run_ffn("ffn1", l, h)
        rec["h2"] = h
        if l % 2 == 0:
            p = norm_proj(h, mix_norm[l][None], full["attn_w_in"][j], attn_b_in[j][None])
            o = attn_fwd(p, tab, attn_sinks[j][None])
            h = linear_residual(h, o, full["attn_w_out"][j], attn_b_out[j][None])
            rec.update(p=p, o=o)
        else:
            w_cat, conv, par = gdn_params(j)
            proj = norm_proj(h, mix_norm[l][None], w_cat, jnp.zeros((1, GDN_PROJ_W), F32))
            qkv, gc = gdn_pre_fwd(proj, conv, par)
            on, st, rec["m_offs"] = gdn_chunk_fwd(qkv, gc, proj, gdn_norm_w[j][None])
            h = linear_residual(h, on, full["gdn_w_out"][j], zero_d)
            rec.update(proj=proj, qkv=qkv, gc=gc, on=on, st=st, w_cat=w_cat, conv=conv, par=par)
        rec["h3"] = h
        h, rec["ffn2_gu"], rec["ffn2_a"] = run_ffn("ffn2", l, h)
        saved.append(rec)

    dh, loss_tile, d_final = loss_head(h, final_norm[None], target)

    g = {n: [None] * w[n].shape[0] for n in WEIGHTS if n != "final_norm"}
    reducers = []

    def layer_of(n, i):
        return i if n.startswith("ffn") else (2 * i if n.startswith("attn") else 2 * i + 1)

    def layer_reduce(layer):
        picks = {n: idx for n in big if (idx := [i for i in range(w[n].shape[0]) if layer_of(n, i) == layer])}
        return GradReduce({n: [g[n][i] for i in idx] for n, idx in picks.items()},
                          {n: (len(idx),) + w[n].shape[1:] for n, idx in picks.items()})

    def tn(x, y, scale=1.0):
        big_enough = x.shape[1] * y.shape[1] >= D_MODEL * D_FF
        red = next((r for r in reducers if r.next_job() is not None), None) if big_enough else None
        if red is None:
            return matmul_tn(x, y, scale)
        out, res = matmul_tn(x, y, scale, carry=red.next_job())
        red.deliver(res)
        return out
    for l in reversed(range(depth)):
        j = l // 2
        rec = saved[l]

        def ffn_back(tag, h_in, dh):
            wgu, wd = ffn_w[(tag, l)]
            dh_new, hn, dgu, dn = ffn_bwd(h_in, w[tag + "_norm"][l][None], wgu, wd, rec[tag + "_gu"], dh)
            g[tag + "_w_gate_up"][l] = tn(hn, dgu)
            g[tag + "_w_down"][l] = tn(rec[tag + "_a"], dh, 0.5)
            g[tag + "_norm"][l] = dn[0]
            return dh_new

        dh = ffn_back("ffn2", rec["h3"], dh)
        if l % 2 == 0:
            w_in, w_out = full["attn_w_in"][j], full["attn_w_out"][j]
            do, db_out = matmul_nt(dh, w_out)
            g["attn_w_out"][j] = matmul_tn(rec["o"], dh)
            g["attn_b_out"][j] = db_out[0]
            dq, dkvc, dkvp, dsink = attn_bwd(rec["p"], tab, attn_sinks[j][None], do)
            dkv = kv_combine(dkvc, dkvp)
            dh, hn, dn, cs_q, cs_kv = norm_proj_bwd(rec["h2"], mix_norm[l][None], dh, [dq, dkv],
                                                    [w_in[:, :ATTN_Q_W], w_in[:, ATTN_Q_W:]])
            g["attn_w_in"][j] = jnp.concatenate([matmul_tn(hn, dq), matmul_tn(hn, dkv)], axis=1)
            g["attn_b_in"][j] = jnp.concatenate([cs_q[0], cs_kv[0]])
            g["attn_sinks"][j] = dsink[0]
        else:
            w_cat, conv, par = rec["w_cat"], rec["conv"], rec["par"]
            d_on, _ = matmul_nt(dh, full["gdn_w_out"][j])
            g["gdn_w_out"][j] = matmul_tn(rec["on"], dh)
            dq, dk, dv, dz, dgc_heads, dnw = gdn_chunk_bwd(rec["qkv"], rec["gc"], rec["proj"], gdn_norm_w[j][None],
                                                           rec["st"], rec["m_offs"], d_on)
            dy, dba, dpar = gdn_pre_bwd(rec["proj"], conv, par, dq, dk, dv, dgc_heads)
            dx, dconv = gdn_conv_bwd(rec["proj"], conv, dy)
            nz = GDN_QKV_W + GDN_HEADS * GDN_DK
            dh, hn, dn, _, _, _ = norm_proj_bwd(rec["h2"], mix_norm[l][None], dh, [dx, dz, dba],
                                                [w_cat[:, :GDN_QKV_W], w_cat[:, GDN_QKV_W:nz], w_cat[:, nz:]])
            g["gdn_w_in"][j] = jnp.concatenate(
                [tn(hn, dx), matmul_tn(hn, dz), matmul_tn(hn, dba)[:, :2 * GDN_HEADS]], axis=1)
            g["gdn_conv_w"][j] = dconv[:GDN_CONV]
            g["gdn_A_log"][j] = dpar[0, GDN_HEADS:2 * GDN_HEADS]
            g["gdn_dt_bias"][j] = dpar[1, GDN_HEADS:2 * GDN_HEADS]
            g["gdn_norm_w"][j] = dnw[0]
        g["mix_norm"][l] = dn[0]
        dh = ffn_back("ffn1", rec["h1"], dh)
        if l > 0:
            reducers.append(layer_reduce(l))
    grad_x = dh[None]

    small_names = REPLICATED + ["gdn_conv_w"]
    local = {n: jnp.stack(g[n]) for n in small_names if n != "final_norm"}
    local["final_norm"] = d_final[0]
    small_shapes = [(1,)] + [local[n].shape for n in small_names]
    small = allreduce_small(_small_pack([loss_tile[0, 0:1]] + [local[n] for n in small_names]))
    small = _small_unpack(small, small_shapes)
    loss = small[0][0]
    grads = dict(zip(small_names, small[1:]))
    conv_cols = gdn_conv_w.shape[2]
    grads["gdn_conv_w"] = lax.dynamic_slice_in_dim(grads["gdn_conv_w"], _chip_index() * conv_cols, conv_cols, axis=2)
    by_layer = [layer_reduce(0).finish()] + [r.finish() for r in reversed(reducers)]
    grads.update({n: jnp.concatenate([part[n] for part in by_layer if n in part], axis=0) for n in big})

    delta, new_m, new_v = {}, {}, {}
    for n in SHARDED:
        dl, nm, nv = adamw(_as2d(w[n]), _as2d(grads[n]), _as2d(given["m_" + n]), _as2d(given["v_" + n]))
        delta[n], new_m[n], new_v[n] = dl.reshape(w[n].shape), nm.reshape(w[n].shape), nv.reshape(w[n].shape)
    shapes = [w[n].shape for n in REPLICATED]
    packed = [_small_pack([src[n] for n in REPLICATED]) for src in
              (w, grads, {n: given["m_" + n] for n in REPLICATED}, {n: given["v_" + n] for n in REPLICATED})]
    for dst, buf in zip((delta, new_m, new_v), adamw(*packed)):
        dst.update(zip(REPLICATED, _small_unpack(buf, shapes)))

    return (loss, grad_x, *[grads[n] for n in WEIGHTS], *[delta[n] for n in WEIGHTS],
            *[new_m[n] for n in WEIGHTS], *[new_v[n] for n in WEIGHTS])
```

```python
import functools

import jax
import jax.numpy as jnp
from jax import lax
from jax.experimental import pallas as pl
from jax.experimental.pallas import tpu as pltpu

F32 = jnp.float32
BF16 = jnp.bfloat16
HI = lax.Precision.HIGHEST
MESH = pl.DeviceIdType.MESH

D_MODEL = 1024
D_FF = 2816
NORM_EPS = 1e-6
LANE = 128

ATTN_Q_HEADS = 16
ATTN_KV_HEADS = 4
ATTN_HEAD_DIM = 64
ATTN_GROUP = 4
ATTN_BLOCK = 128
ROPE_DIM = 16
ROPE_THETA = 500000.0
ATTN_Q_W = 1024
ATTN_KV_W = 256

GDN_HEADS = 8
GDN_DK = 128
GDN_CONV = 4
GDN_CHUNK = 64
GDN_QKV_W = 3072

ADAM_LR = 0.001
ADAM_B1 = 0.9
ADAM_B2 = 0.999
ADAM_EPS = 1e-08
ADAM_WD = 0.01
ADAM_STEP = 10

NEG = -1e30


def _dot(a, b, prec=None):
    return lax.dot_general(a, b, (((1,), (0,)), ((), ())), precision=prec, preferred_element_type=F32)


def _dot_nt(a, b, prec=None):
    return lax.dot_general(a, b, (((1,), (1,)), ((), ())), precision=prec, preferred_element_type=F32)


def _dot_tn(a, b, prec=None):
    return lax.dot_general(a, b, (((0,), (0,)), ((), ())), precision=prec, preferred_element_type=F32)


def _bdot(a, b):
    return _dot(a.astype(BF16), b.astype(BF16))


def _bdot_nt(a, b):
    return _dot_nt(a.astype(BF16), b.astype(BF16))


def _bdot_tn(a, b):
    return _dot_tn(a.astype(BF16), b.astype(BF16))


def _sigmoid(x):
    return 1.0 / (1.0 + jnp.exp(-x))


def _silu(x):
    return x * _sigmoid(x)


def _silu_grad(x):
    s = _sigmoid(x)
    return s * (1.0 + x * (1.0 - s))


def _rms(x, w):
    r = lax.rsqrt(jnp.mean(x * x, axis=-1, keepdims=True) + NORM_EPS)
    xhat = x * r
    return xhat * w, xhat, r


def _rms_bwd(dy, w, xhat, r):
    dxhat = dy * w
    dx = r * (dxhat - xhat * jnp.mean(dxhat * xhat, axis=-1, keepdims=True))
    dw = jnp.sum(dy * xhat, axis=0, keepdims=True)
    return dx, dw


def _arb(n):
    return pltpu.CompilerParams(dimension_semantics=("arbitrary",) * n)


def _tile(n, want):
    t = min(n, want)
    assert n % t == 0, (n, want)
    return t


def _iota2(shape, dim):
    return lax.broadcasted_iota(jnp.int32, shape, dim)


_NN = (((1,), (0,)), ((), ()))
_NT = (((1,), (1,)), ((), ()))
_TN = (((0,), (0,)), ((), ()))


def _raw1(a, b, dn):
    return lax.dot_general(a.astype(BF16), b.astype(BF16), dn, preferred_element_type=F32)


def _raw3(a, b, dn):
    ah, bh = a.astype(BF16), b.astype(BF16)
    al, bl = (a - ah.astype(F32)).astype(BF16), (b - bh.astype(F32)).astype(BF16)
    f = lambda x, y: lax.dot_general(x, y, dn, preferred_element_type=F32)
    return f(ah, bh) + f(ah, bl) + f(al, bh)


def _make_mm(raw):
    @jax.custom_vjp
    def mm(a, b):
        return raw(a, b, _NN)

    mm.defvjp(lambda a, b: (raw(a, b, _NN), (a, b)),
              lambda res, ct: (raw(ct, res[1], _NT), raw(res[0], ct, _TN)))

    @jax.custom_vjp
    def mm_nt(a, b):
        return raw(a, b, _NT)

    mm_nt.defvjp(lambda a, b: (raw(a, b, _NT), (a, b)),
                 lambda res, ct: (raw(ct, res[1], _NN), raw(ct, res[0], _TN)))
    return mm, mm_nt


_mm1, _mm1_nt = _make_mm(_raw1)
_mm3, _mm3_nt = _make_mm(_raw3)


def _eye(n):
    return (_iota2((n, n), 0) == _iota2((n, n), 1)).astype(F32)


def _each(f, *lists):
    return [f(*t) for t in zip(*lists)]


def _inv_newton(mats):
    eye = _eye(mats[0].shape[0])
    ps = [-a for a in mats]
    ms = [eye + p for p in ps]
    k = 1
    while 2 * k < GDN_CHUNK:
        ps = [_raw1(p, p, _NN) for p in ps]
        ms = _each(lambda m, p: m + _raw1(m, p, _NN), ms, ps)
        k *= 2
    rs = _each(lambda a, m: eye - m - _raw3(a, m, _NN), mats, ms)
    return _each(lambda m, r: m + _raw1(m, r, _NN), ms, rs)


def _inv_cotangent(ms, dms):
    ts = _each(lambda m, dm: _raw3(m, dm, _TN), ms, dms)
    return _each(lambda t, m: -_raw3(t, m, _NT), ts, ms)


def _chunk_causal(n):
    r, s = _iota2((n, n), 0), _iota2((n, n), 1)
    same = (r // GDN_CHUNK) == (s // GDN_CHUNK)
    return same & (r >= s), same & (r > s)


def _gdn_a(k, dcb, bb128):
    causal, strict = _chunk_causal(k[0].shape[0])
    decay_l = [jnp.exp(jnp.where(causal, d - d.T, NEG)) for d in dcb]
    kb = _each(lambda a, b: a * b, k, bb128)
    return _each(lambda x, y, dl: jnp.where(strict, _mm1_nt(x, y) * dl, 0.0), kb, k, decay_l)


def _gdn_rest(q, k, v, dcb, dcb128, dlb128, bb128, m_off):
    causal, _ = _chunk_causal(q[0].shape[0])
    decay_l = [jnp.exp(jnp.where(causal, d - d.T, NEG)) for d in dcb]
    kb = _each(lambda a, b: a * b, k, bb128)
    edc = [jnp.exp(d) for d in dcb128]
    rhs = _each(lambda vv, bb, kk, e: jnp.concatenate([vv * bb, kk * e], axis=1), v, bb128, kb, edc)
    sol = _each(lambda x, m: x + _mm3(m, x), rhs, m_off)
    aqk = _each(lambda x, y, dl: jnp.where(causal, _mm1_nt(x, y) * dl, 0.0), q, k, decay_l)
    q_dec = _each(lambda x, e: x * e, q, edc)
    k_dec = _each(lambda x, dl, dc: x * jnp.exp(dl - dc), k, dlb128, dcb128)
    return sol, aqk, q_dec, k_dec


def _gdn_local(q, k, v, dcb, dcb128, dlb128, bb128):
    eye = _eye(q[0].shape[0])
    m_off = [m - eye for m in _inv_newton(_gdn_a(k, dcb, bb128))]
    return _gdn_rest(q, k, v, dcb, dcb128, dlb128, bb128, m_off) + (m_off,)


def _gated_norm(o, z, nw):
    r = lax.rsqrt(jnp.mean(o * o, axis=-1, keepdims=True) + NORM_EPS)
    return o * r * nw * _silu(z)


def _gated_norm_bwd(dy, o, z, nw):
    r = lax.rsqrt(jnp.mean(o * o, axis=-1, keepdims=True) + NORM_EPS)
    xhat = o * r
    sz = _silu(z)
    dxhat = dy * nw * sz
    do = r * (dxhat - xhat * jnp.mean(dxhat * xhat, axis=-1, keepdims=True))
    dz = dy * xhat * nw * _silu_grad(z)
    dnw = jnp.sum(dy * xhat * sz, axis=0, keepdims=True)
    return do, dz, dnw


GDN_HEAD_GROUP = 8
GC_BETA, GC_G, GC_DECAY, GC_LAST = 0, GDN_HEADS, 2 * GDN_HEADS, 3 * GDN_HEADS


def _gate_cols(gc, h, rows):
    lane = _iota2((rows, LANE), 1)
    col = lambda off: jnp.sum(jnp.where(lane == off + h, gc, 0.0), axis=-1, keepdims=True)
    return col(GC_BETA), col(GC_DECAY), col(GC_LAST)


def _gdn_local_args(q_ref, k_ref, v_ref, gc_ref, hp, n):
    gc = gc_ref[...]
    lanes = [slice(hh * LANE, (hh + 1) * LANE) for hh in range(hp)]
    cols = [_gate_cols(gc, pl.program_id(0) * hp + hh, n) for hh in range(hp)]
    bcast = lambda i, w: [jnp.broadcast_to(c[i], (n, w)) for c in cols]
    args = ([q_ref[:, ls] for ls in lanes], [k_ref[:, ls] for ls in lanes], [v_ref[:, ls] for ls in lanes],
            bcast(1, n), bcast(1, LANE), bcast(2, LANE), bcast(0, LANE))
    return args, [c[2] for c in cols], lanes


def gdn_chunk_fwd(qkv, gc, proj, norm_w):
    t = qkv.shape[0]
    c = GDN_CHUNK
    tc = _tile(t, 256)
    nsub = tc // c

    hp = GDN_HEAD_GROUP

    def body(q_ref, k_ref, v_ref, gc_ref, z_ref, nw_ref, on_ref, st_ref, m_ref, s_ref):
        @pl.when(pl.program_id(1) == 0)
        def _():
            s_ref[...] = jnp.zeros_like(s_ref)

        args, dl, lanes = _gdn_local_args(q_ref, k_ref, v_ref, gc_ref, hp, tc)
        sol, aqk, q_dec, k_dec, m_off = _gdn_local(*args)
        for hh in range(hp):
            m_ref[hh] = m_off[hh]
        u = [x[:, :GDN_DK] for x in sol]
        w = [x[:, GDN_DK:] for x in sol]
        s = [s_ref[hh] for hh in range(hp)]
        v_new, o_state = [[] for _ in range(hp)], [[] for _ in range(hp)]
        for j in range(nsub):
            sl = slice(j * c, (j + 1) * c)
            for hh in range(hp):
                st_ref[hh, j] = s[hh]
            vn = _each(lambda uu, ww, ss: uu[sl] - _bdot(ww[sl], ss), u, w, s)
            os_ = _each(lambda qq, ss: _bdot(qq[sl], ss), q_dec, s)
            s = _each(lambda ss, d, kk, vv: ss * jnp.exp(d[j * c:j * c + 1]) + _bdot_tn(kk[sl], vv), s, dl, k_dec, vn)
            for hh in range(hp):
                v_new[hh].append(vn[hh])
                o_state[hh].append(os_[hh])
        for hh in range(hp):
            s_ref[hh] = s[hh]
        cat = lambda xs: jnp.concatenate(xs, axis=0)
        o = _each(lambda os_, aa, vv: cat(os_) + _bdot(aa, cat(vv)), o_state, aqk, v_new)
        for hh, ls in enumerate(lanes):
            on_ref[:, ls] = _gated_norm(o[hh], z_ref[:, ls], nw_ref[...]).astype(BF16)

    col = lambda off: pl.BlockSpec((tc, hp * LANE), lambda g, i: (i, off // hp + g))
    return pl.pallas_call(
        body, name="gdn_chunk_fwd",
        grid=(GDN_HEADS // hp, t // tc),
        in_specs=[col(0), col(GDN_HEADS), col(2 * GDN_HEADS),
                  pl.BlockSpec((tc, LANE), lambda g, i: (i, 0)),
                  col(GDN_QKV_W // LANE),
                  pl.BlockSpec((1, LANE), lambda g, i: (0, 0))],
        out_specs=[col(0),
                   pl.BlockSpec((hp, nsub, GDN_DK, GDN_DK), lambda g, i: (g, i, 0, 0)),
                   pl.BlockSpec((hp, tc, tc), lambda g, i: (g, i, 0))],
        out_shape=[jax.ShapeDtypeStruct((t, GDN_HEADS * GDN_DK), BF16),
                   jax.ShapeDtypeStruct((GDN_HEADS, t // c, GDN_DK, GDN_DK), F32),
                   jax.ShapeDtypeStruct((GDN_HEADS, t, tc), F32)],
        scratch_shapes=[pltpu.VMEM((hp, GDN_DK, GDN_DK), F32)],
        compiler_params=_arb(2),
    )(qkv, qkv, qkv, gc, proj, norm_w)


def gdn_chunk_bwd(qkv, gc, proj, norm_w, states, m_offs, d_on):
    t = qkv.shape[0]
    c = GDN_CHUNK
    tc = _tile(t, 256)
    nsub = tc // c
    nblk = t // tc
    hp = GDN_HEAD_GROUP

    def body(q_ref, k_ref, v_ref, gc_ref, z_ref, nw_ref, st_ref, m_ref, don_ref,
             dq_ref, dk_ref, dv_ref, dz_ref, dgc_ref, dnw_ref, ds_ref):
        @pl.when(pl.program_id(1) == 0)
        def _():
            ds_ref[...] = jnp.zeros_like(ds_ref)

        @pl.when((pl.program_id(1) == 0) & (pl.program_id(0) == 0))
        def _():
            dnw_ref[...] = jnp.zeros_like(dnw_ref)

        rows = [slice(j * c, (j + 1) * c) for j in range(nsub)]
        cat = lambda xs: jnp.concatenate(xs, axis=0)
        lsum = lambda x: jnp.sum(x, axis=-1, keepdims=True)
        lane = _iota2((tc, LANE), 1)
        row = _iota2((tc, 1), 0)
        heads = range(hp)
        args, dl, lanes = _gdn_local_args(q_ref, k_ref, v_ref, gc_ref, hp, tc)
        m_off = [m_ref[hh] for hh in heads]
        (sol, aqk, q_dec, k_dec), vjp = jax.vjp(_gdn_rest, *args, m_off)
        u = [x[:, :GDN_DK] for x in sol]
        w = [x[:, GDN_DK:] for x in sol]
        states = [[st_ref[hh, j] for j in range(nsub)] for hh in heads]
        v_new = _each(lambda uu, ww, st: [uu[sl] - _bdot(ww[sl], s) for sl, s in zip(rows, st)], u, w, states)
        v_all = [cat(v) for v in v_new]
        o = _each(lambda qq, st, aa, vv: cat([_bdot(qq[sl], s) for sl, s in zip(rows, st)]) + _bdot(aa, vv),
                  q_dec, states, aqk, v_all)
        nw = nw_ref[...]
        gn = [_gated_norm_bwd(don_ref[:, ls], o[hh], z_ref[:, ls], nw) for hh, ls in enumerate(lanes)]
        do = [x[0] for x in gn]
        for hh, ls in enumerate(lanes):
            dnw_ref[...] += gn[hh][2]
            dz_ref[:, ls] = gn[hh][1].astype(BF16)
        d_aqk = _each(_bdot_nt, do, v_all)
        dv_o = _each(_bdot_tn, aqk, do)
        ds = [ds_ref[hh] for hh in heads]
        d_u, d_w, d_qdec, d_kdec, d_last = ([[None] * nsub for _ in heads] for _ in range(5))
        for j in reversed(range(nsub)):
            sl = rows[j]
            sj = [states[hh][j] for hh in heads]
            cd = [jnp.exp(d[j * c:j * c + 1]) for d in dl]
            du = _each(lambda dvo, kk, dd: dvo[sl] + _bdot(kk[sl], dd), dv_o, k_dec, ds)
            dqd = _each(lambda dd, s: _bdot_nt(dd[sl], s), do, sj)
            dkd = _each(lambda vv, dd: _bdot_nt(vv[j], dd), v_new, ds)
            dla = _each(lambda s, dd, cc: jnp.sum(lsum(s * dd), axis=0, keepdims=True) * cc, sj, ds, cd)
            dw = _each(lambda x, s: -_bdot_nt(x, s), du, sj)
            ds = _each(lambda qq, dd, cc, dsn, ww, x: _bdot_tn(qq[sl], dd[sl]) + cc * dsn - _bdot_tn(ww[sl], x),
                       q_dec, do, cd, ds, w, du)
            for hh in heads:
                d_u[hh][j], d_w[hh][j], d_qdec[hh][j], d_kdec[hh][j], d_last[hh][j] = du[hh], dw[hh], dqd[hh], dkd[hh], dla[hh]
        for hh in heads:
            ds_ref[hh] = ds[hh]
        d_sol = _each(lambda a, b: jnp.concatenate([cat(a), cat(b)], axis=1), d_u, d_w)
        dq, dk, dv, d_dcb, d_dcb128, d_dlb128, d_bb, d_m = vjp(
            (d_sol, d_aqk, [cat(x) for x in d_qdec], [cat(x) for x in d_kdec]))
        eye = _eye(tc)
        d_a = _inv_cotangent([m + eye for m in m_off], d_m)
        k_args = (args[1], args[3], args[6])
        dk_a, d_dcb_a, d_bb_a = jax.vjp(_gdn_a, *k_args)[1](d_a)
        add = lambda xs, ys: _each(lambda x, y: x + y, xs, ys)
        dk, d_dcb, d_bb = add(dk, dk_a), add(d_dcb, d_dcb_a), add(d_bb, d_bb_a)
        dgc = jnp.zeros((tc, LANE), F32)
        for hh, ls in enumerate(lanes):
            h = pl.program_id(0) * hp + hh
            dq_ref[:, ls] = dq[hh]
            dk_ref[:, ls] = dk[hh]
            dv_ref[:, ls] = dv[hh]
            d_dl = lsum(d_dlb128[hh])
            for j in range(nsub):
                d_dl = d_dl + jnp.where(row == j * c, d_last[hh][j], 0.0)
            dgc = dgc + jnp.where(lane == GC_BETA + h, lsum(d_bb[hh]),
                                  jnp.where(lane == GC_DECAY + h, lsum(d_dcb[hh]) + lsum(d_dcb128[hh]),
                                            jnp.where(lane == GC_LAST + h, d_dl, 0.0)))
        dgc_ref[0] = dgc

    rev = lambda i: nblk - 1 - i
    col = lambda off: pl.BlockSpec((tc, hp * LANE), lambda g, i: (rev(i), off // hp + g))
    return pl.pallas_call(
        body, name="gdn_chunk_bwd",
        grid=(GDN_HEADS // hp, nblk),
        in_specs=[col(0), col(GDN_HEADS), col(2 * GDN_HEADS),
                  pl.BlockSpec((tc, LANE), lambda g, i: (rev(i), 0)),
                  col(GDN_QKV_W // LANE),
                  pl.BlockSpec((1, LANE), lambda g, i: (0, 0)),
                  pl.BlockSpec((hp, nsub, GDN_DK, GDN_DK), lambda g, i: (g, rev(i), 0, 0)),
                  pl.BlockSpec((hp, tc, tc), lambda g, i: (g, rev(i), 0)),
                  col(0)],
        out_specs=[col(0), col(0), col(0), col(0),
                   pl.BlockSpec((1, tc, LANE), lambda g, i: (g, rev(i), 0)),
                   pl.BlockSpec((1, LANE), lambda g, i: (0, 0))],
        out_shape=[
            jax.ShapeDtypeStruct((t, GDN_HEADS * GDN_DK), F32),
            jax.ShapeDtypeStruct((t, GDN_HEADS * GDN_DK), F32),
            jax.ShapeDtypeStruct((t, GDN_HEADS * GDN_DK), F32),
            jax.ShapeDtypeStruct((t, GDN_HEADS * GDN_DK), BF16),
            jax.ShapeDtypeStruct((GDN_HEADS // hp, t, LANE), F32),
            jax.ShapeDtypeStruct((1, LANE), F32)],
        scratch_shapes=[pltpu.VMEM((hp, GDN_DK, GDN_DK), F32)],
        compiler_params=_arb(2),
    )(qkv, qkv, qkv, gc, proj, norm_w, states, m_offs, d_on)


GDN_PROJ_W = GDN_QKV_W + 1024 + LANE
GDN_Q_SCALE = GDN_DK ** -0.5


def _shift_rows(ext, shift, lo, n):
    if shift == 0:
        return ext[lo:lo + n]
    return pltpu.roll(ext, shift, 0)[lo:lo + n]


def _conv_fwd(x, halo, w):
    n = x.shape[0]
    ext = jnp.concatenate([halo, x], axis=0)
    y = w[GDN_CONV - 1:GDN_CONV] * x
    for j in range(GDN_CONV - 1):
        y = y + w[j:j + 1] * _shift_rows(ext, GDN_CONV - 1 - j, 8, n)
    return y


def _chunk_masks(n):
    r, s = _iota2((n, n), 0), _iota2((n, n), 1)
    same = (r // GDN_CHUNK) == (s // GDN_CHUNK)
    return (same & (r >= s)).astype(F32), (same & (r <= s)).astype(F32), same.astype(F32)


def _softplus(x):
    return jnp.maximum(x, 0.0) + jnp.log(1.0 + jnp.exp(-jnp.abs(x)))


def _l2n(t):
    rs = lax.rsqrt(jnp.sum(t * t, axis=-1, keepdims=True) + NORM_EPS)
    return t * rs, rs


def gdn_pre_fwd(proj, conv_w, gate_par):
    t = proj.shape[0]
    tm = _tile(t, 256)

    def body(x_ref, halo_ref, ba_ref, w_ref, gp_ref, qkv_ref, gc_ref):
        i = pl.program_id(0)
        halo = jnp.where(i > 0, halo_ref[...], 0.0)
        y = _silu(_conv_fwd(x_ref[...], halo, w_ref[...]))
        for hh in range(2 * GDN_HEADS):
            sl = slice(hh * LANE, (hh + 1) * LANE)
            tn, _ = _l2n(y[:, sl])
            qkv_ref[:, sl] = tn * GDN_Q_SCALE if hh < GDN_HEADS else tn
        qkv_ref[:, 2 * GDN_HEADS * LANE:] = y[:, 2 * GDN_HEADS * LANE:]
        ba = ba_ref[...]
        lane = _iota2(ba.shape, 1)
        gp = gp_ref[...]
        is_a = (lane >= GC_G) & (lane < GC_G + GDN_HEADS)
        g = jnp.where(is_a, -jnp.exp(gp[0:1]) * _softplus(ba + gp[1:2]), 0.0)
        tri, _, same = _chunk_masks(tm)
        decay = pltpu.roll(_dot(tri, g, HI), GC_DECAY - GC_G, 1)
        last = pltpu.roll(_dot(same, g, HI), GC_LAST - GC_G, 1)
        gc_ref[...] = jnp.where(lane < GDN_HEADS, _sigmoid(ba), g) + decay + last

    return pl.pallas_call(
        body, name="gdn_pre_fwd",
        grid=(t // tm,),
        in_specs=[pl.BlockSpec((tm, GDN_QKV_W), lambda i: (i, 0)),
                  pl.BlockSpec((8, GDN_QKV_W), lambda i: (jnp.maximum(i * (tm // 8) - 1, 0), 0)),
                  pl.BlockSpec((tm, LANE), lambda i: (i, (GDN_QKV_W + 1024) // LANE)),
                  pl.BlockSpec((8, GDN_QKV_W), lambda i: (0, 0)),
                  pl.BlockSpec((8, LANE), lambda i: (0, 0))],
        out_specs=[pl.BlockSpec((tm, GDN_QKV_W), lambda i: (i, 0)),
                   pl.BlockSpec((tm, LANE), lambda i: (i, 0))],
        out_shape=[jax.ShapeDtypeStruct((t, GDN_QKV_W), F32), jax.ShapeDtypeStruct((t, LANE), F32)],
        compiler_params=_arb(1),
    )(proj, proj, proj, conv_w, gate_par)


def gdn_pre_bwd(proj, conv_w, gate_par, dq, dk, dv, dgc_heads):
    t = proj.shape[0]
    tm = _tile(t, 256)

    def body(x_ref, halo_ref, ba_ref, w_ref, gp_ref, dq_ref, dk_ref, dv_ref, dgc_ref, dy_ref, dba_ref, dgp_ref):
        i = pl.program_id(0)

        @pl.when(i == 0)
        def _():
            dgp_ref[...] = jnp.zeros_like(dgp_ref)

        halo = jnp.where(i > 0, halo_ref[...], 0.0)
        y = _conv_fwd(x_ref[...], halo, w_ref[...])
        for hh in range(3 * GDN_HEADS):
            sl = slice(hh * LANE, (hh + 1) * LANE)
            hsl = slice((hh % GDN_HEADS) * LANE, (hh % GDN_HEADS + 1) * LANE)
            yy = y[:, sl]
            if hh < 2 * GDN_HEADS:
                tn, rs = _l2n(_silu(yy))
                dtn = dq_ref[:, hsl] * GDN_Q_SCALE if hh < GDN_HEADS else dk_ref[:, hsl]
                dsil = rs * (dtn - tn * jnp.sum(dtn * tn, axis=-1, keepdims=True))
            else:
                dsil = dv_ref[:, hsl]
            dy_ref[:, sl] = dsil * _silu_grad(yy)
        dgc = dgc_ref[0]
        for hh in range(1, dgc_heads.shape[0]):
            dgc = dgc + dgc_ref[hh]
        ba = ba_ref[...]
        lane = _iota2(ba.shape, 1)
        _, tri_t, same = _chunk_masks(tm)
        d_decay = jnp.where((lane >= GC_DECAY) & (lane < GC_DECAY + GDN_HEADS), dgc, 0.0)
        d_last = jnp.where((lane >= GC_LAST) & (lane < GC_LAST + GDN_HEADS), dgc, 0.0)
        dgc = (jnp.where(lane < GDN_HEADS, dgc, 0.0) + pltpu.roll(_dot(tri_t, d_decay, HI), LANE - (GC_DECAY - GC_G), 1)
               + pltpu.roll(_dot(same, d_last, HI), LANE - (GC_LAST - GC_G), 1))
        gp = gp_ref[...]
        xg = ba + gp[1:2]
        ea = jnp.exp(gp[0:1])
        sp = _softplus(xg)
        sb = _sigmoid(ba)
        is_b = lane < GDN_HEADS
        is_a = (lane >= GDN_HEADS) & (lane < 2 * GDN_HEADS)
        d_pre = jnp.where(is_a, dgc * (-ea) * _sigmoid(xg), 0.0)
        dba_ref[...] = jnp.where(is_b, dgc * sb * (1.0 - sb), d_pre).astype(BF16)
        d_alog = jnp.sum(jnp.where(is_a, dgc * (-ea) * sp, 0.0), axis=0, keepdims=True)
        d_dtb = jnp.sum(d_pre, axis=0, keepdims=True)
        row = _iota2((8, LANE), 0)
        dgp_ref[...] += jnp.where(row == 0, d_alog, jnp.where(row == 1, d_dtb, 0.0))

    hspec = pl.BlockSpec((tm, GDN_HEADS * LANE), lambda i: (i, 0))
    return pl.pallas_call(
        body, name="gdn_pre_bwd",
        grid=(t // tm,),
        in_specs=[pl.BlockSpec((tm, GDN_QKV_W), lambda i: (i, 0)),
                  pl.BlockSpec((8, GDN_QKV_W), lambda i: (jnp.maximum(i * (tm // 8) - 1, 0), 0)),
                  pl.BlockSpec((tm, LANE), lambda i: (i, (GDN_QKV_W + 1024) // LANE)),
                  pl.BlockSpec((8, GDN_QKV_W), lambda i: (0, 0)),
                  pl.BlockSpec((8, LANE), lambda i: (0, 0)),
                  hspec, hspec, hspec,
                  pl.BlockSpec((dgc_heads.shape[0], tm, LANE), lambda i: (0, i, 0))],
        out_specs=[pl.BlockSpec((tm, GDN_QKV_W), lambda i: (i, 0)),
                   pl.BlockSpec((tm, LANE), lambda i: (i, 0)),
                   pl.BlockSpec((8, LANE), lambda i: (0, 0))],
        out_shape=[jax.ShapeDtypeStruct((t, GDN_QKV_W), F32), jax.ShapeDtypeStruct((t, LANE), BF16),
                   jax.ShapeDtypeStruct((8, LANE), F32)],
        compiler_params=_arb(1),
    )(proj, proj, proj, conv_w, gate_par, dq, dk, dv, dgc_heads)


def gdn_conv_bwd(proj, conv_w, dy):
    t = proj.shape[0]
    tm = _tile(t, 256)
    nblk = t // tm

    def body(x_ref, halo_ref, w_ref, dy_ref, dyn_ref, dx_ref, dw_ref):
        i = pl.program_id(0)

        @pl.when(i == 0)
        def _():
            dw_ref[...] = jnp.zeros_like(dw_ref)

        w = w_ref[...]
        dy = dy_ref[...]
        ext_dy = jnp.concatenate([dy, jnp.where(i < nblk - 1, dyn_ref[...], 0.0)], axis=0)
        ext_x = jnp.concatenate([jnp.where(i > 0, halo_ref[...], 0.0), x_ref[...]], axis=0)
        dx = w[GDN_CONV - 1:GDN_CONV] * dy
        rows = [jnp.sum(dy * x_ref[...], axis=0, keepdims=True)]
        for j in range(GDN_CONV - 1):
            sh = GDN_CONV - 1 - j
            dx = dx + w[j:j + 1] * _shift_rows(ext_dy, tm + 8 - sh, 0, tm)
            rows.insert(j, jnp.sum(dy * _shift_rows(ext_x, sh, 8, tm), axis=0, keepdims=True))
        dx_ref[...] = dx.astype(BF16)
        row = _iota2((8, GDN_QKV_W), 0)
        acc = jnp.zeros((8, GDN_QKV_W), F32)
        for j in range(GDN_CONV):
            acc = acc + jnp.where(row == j, rows[j], 0.0)
        dw_ref[...] += acc

    return pl.pallas_call(
        body, name="gdn_conv_bwd",
        grid=(nblk,),
        in_specs=[pl.BlockSpec((tm, GDN_QKV_W), lambda i: (i, 0)),
                  pl.BlockSpec((8, GDN_QKV_W), lambda i: (jnp.maximum(i * (tm // 8) - 1, 0), 0)),
                  pl.BlockSpec((8, GDN_QKV_W), lambda i: (0, 0)),
                  pl.BlockSpec((tm, GDN_QKV_W), lambda i: (i, 0)),
                  pl.BlockSpec((8, GDN_QKV_W), lambda i: (jnp.minimum((i + 1) * (tm // 8), t // 8 - 1), 0))],
        out_specs=[pl.BlockSpec((tm, GDN_QKV_W), lambda i: (i, 0)),
                   pl.BlockSpec((8, GDN_QKV_W), lambda i: (0, 0))],
        out_shape=[jax.ShapeDtypeStruct((t, GDN_QKV_W), BF16), jax.ShapeDtypeStruct((8, GDN_QKV_W), F32)],
        compiler_params=_arb(1),
    )(proj, proj, conv_w, dy, dy)


def _resident(w_hbm, w_vmem, sem):
    @pl.when(pl.program_id(0) == 0)
    def _():
        cp = pltpu.make_async_copy(w_hbm, w_vmem, sem)
        cp.start()
        cp.wait()


ANY = pl.BlockSpec(memory_space=pl.ANY)


def norm_proj(h, nw, w, bias):
    t, d = h.shape
    n = w.shape[1]
    tm = _tile(t, 256)
    nc = _tile(n, 1536) if n % 1536 == 0 else _tile(n, 1408)

    def body(h_ref, nw_ref, w_hbm, b_ref, o_ref, w_ref, sem):
        _resident(w_hbm, w_ref, sem)
        hn = _rms(h_ref[...], nw_ref[...])[0].astype(BF16)
        for c0 in range(0, n, nc):
            o_ref[:, c0:c0 + nc] = _dot(hn, w_ref[:, c0:c0 + nc]) + b_ref[:, c0:c0 + nc]

    return pl.pallas_call(
        body, name="norm_proj",
        grid=(t // tm,),
        in_specs=[pl.BlockSpec((tm, d), lambda i: (i, 0)), pl.BlockSpec((1, d), lambda i: (0, 0)), ANY,
                  pl.BlockSpec((1, n), lambda i: (0, 0))],
        out_specs=pl.BlockSpec((tm, n), lambda i: (i, 0)),
        out_shape=jax.ShapeDtypeStruct((t, n), F32),
        scratch_shapes=[pltpu.VMEM((d, n), BF16), pltpu.SemaphoreType.DMA],
        compiler_params=_arb(1),
    )(h, nw, w, bias)


def linear_residual(h, x, w, bias):
    t, d = h.shape
    k = x.shape[1]
    tm = _tile(t, 512)

    def body(h_ref, x_ref, w_hbm, b_ref, o_ref, w_ref, sem):
        _resident(w_hbm, w_ref, sem)
        o_ref[...] = h_ref[...] + _dot(x_ref[...], w_ref[...]) + b_ref[...]

    return pl.pallas_call(
        body, name="linear_residual",
        grid=(t // tm,),
        in_specs=[pl.BlockSpec((tm, d), lambda i: (i, 0)), pl.BlockSpec((tm, k), lambda i: (i, 0)), ANY,
                  pl.BlockSpec((1, d), lambda i: (0, 0))],
        out_specs=pl.BlockSpec((tm, d), lambda i: (i, 0)),
        out_shape=jax.ShapeDtypeStruct((t, d), F32),
        scratch_shapes=[pltpu.VMEM((k, d), BF16), pltpu.SemaphoreType.DMA],
        compiler_params=_arb(1),
    )(h, x, w, bias)


def matmul_nt(dy, w):
    t, d = dy.shape
    k = w.shape[0]
    tm = _tile(t, 512)

    def body(dy_ref, w_hbm, o_ref, cs_ref, w_ref, sem):
        _resident(w_hbm, w_ref, sem)

        @pl.when(pl.program_id(0) == 0)
        def _():
            cs_ref[...] = jnp.zeros_like(cs_ref)

        dy = dy_ref[...]
        cs_ref[...] += jnp.sum(dy, axis=0, keepdims=True)
        o_ref[...] = _dot_nt(dy.astype(BF16), w_ref[...])

    return pl.pallas_call(
        body, name="matmul_nt",
        grid=(t // tm,),
        in_specs=[pl.BlockSpec((tm, d), lambda i: (i, 0)), ANY],
        out_specs=[pl.BlockSpec((tm, k), lambda i: (i, 0)), pl.BlockSpec((1, d), lambda i: (0, 0))],
        out_shape=[jax.ShapeDtypeStruct((t, k), F32), jax.ShapeDtypeStruct((1, d), F32)],
        scratch_shapes=[pltpu.VMEM((k, d), BF16), pltpu.SemaphoreType.DMA],
        compiler_params=_arb(1),
    )(dy, w)


def norm_proj_bwd(h, nw, dh, dps, ws):
    t, d = h.shape
    np_ = len(dps)
    ns = [w.shape[1] for w in ws]
    tm = _tile(t, 256)

    def body(*refs):
        h_ref, nw_ref, dh_ref = refs[:3]
        dp_refs = refs[3:3 + np_]
        w_hbms = refs[3 + np_:3 + 2 * np_]
        o_ref, hn_ref, dnw_ref = refs[3 + 2 * np_:6 + 2 * np_]
        cs_refs = refs[6 + 2 * np_:6 + 3 * np_]
        w_refs = refs[6 + 3 * np_:6 + 4 * np_]
        sem = refs[6 + 4 * np_]
        for a, b in zip(w_hbms, w_refs):
            _resident(a, b, sem)

        @pl.when(pl.program_id(0) == 0)
        def _():
            dnw_ref[...] = jnp.zeros_like(dnw_ref)
            for c in cs_refs:
                c[...] = jnp.zeros_like(c)

        nw = nw_ref[...]
        hn, xhat, r = _rms(h_ref[...], nw)
        hn_ref[...] = hn.astype(BF16)
        dhn = jnp.zeros((tm, d), F32)
        for dp_ref, w_ref, cs_ref in zip(dp_refs, w_refs, cs_refs):
            dp = dp_ref[...]
            cs_ref[...] += jnp.sum(dp.astype(F32), axis=0, keepdims=True)
            dhn = dhn + _dot_nt(dp, w_ref[...])
        dx, dnw = _rms_bwd(dhn, nw, xhat, r)
        dnw_ref[...] += dnw
        o_ref[...] = dh_ref[...] + dx

    row = pl.BlockSpec((tm, d), lambda i: (i, 0))
    vec = pl.BlockSpec((1, d), lambda i: (0, 0))
    return pl.pallas_call(
        body, name="norm_proj_bwd",
        grid=(t // tm,),
        in_specs=[row, vec, row] + [pl.BlockSpec((tm, n), lambda i: (i, 0)) for n in ns] + [ANY] * np_,
        out_specs=[row, row, vec] + [pl.BlockSpec((1, n), lambda i: (0, 0)) for n in ns],
        out_shape=[jax.ShapeDtypeStruct((t, d), F32), jax.ShapeDtypeStruct((t, d), BF16),
                   jax.ShapeDtypeStruct((1, d), F32)] + [jax.ShapeDtypeStruct((1, n), F32) for n in ns],
        scratch_shapes=[pltpu.VMEM((d, n), BF16) for n in ns] + [pltpu.SemaphoreType.DMA],
        compiler_params=_arb(1),
    )(h, nw, dh, *dps, *ws)


def _pair_copies(g_refs, got_refs, send_sems, recv_sems):
    mx, my, mc = _me()
    copies = []
    for g_ref, got_ref in zip(g_refs, got_refs):
        for j in range(4):
            i = len(copies)
            copies.append(pltpu.make_async_remote_copy(
                src_ref=g_ref.at[j, 1 - mc], dst_ref=got_ref.at[j], send_sem=send_sems.at[i], recv_sem=recv_sems.at[i],
                device_id=(mx, my, 1 - mc), device_id_type=MESH))
    return copies


def _scatter_copies(p_refs, q_refs, send_sems, recv_sems):
    mx, my, mc = _me()
    copies = []
    for p_ref, q_ref in zip(p_refs, q_refs):
        for k in (1, 2, 3):
            px, py = _flip(mx, k >> 1), _flip(my, k & 1)
            i = len(copies)
            copies.append(pltpu.make_async_remote_copy(
                src_ref=p_ref.at[2 * px + py], dst_ref=q_ref.at[2 * mx + my], send_sem=send_sems.at[i],
                recv_sem=recv_sems.at[i], device_id=(px, py, mc), device_id_type=MESH))
    return copies


def _share_copies(o_refs, out_refs, send_sems, recv_sems):
    mx, my, mc = _me()
    return [pltpu.make_async_remote_copy(src_ref=o_ref, dst_ref=out_ref, send_sem=send_sems.at[i], recv_sem=recv_sems.at[i],
                                         device_id=(mx, my, 1 - mc), device_id_type=MESH)
            for i, (o_ref, out_ref) in enumerate(zip(o_refs, out_refs))]


CARRIED = {"pair": (_pair_copies, 4, lambda a: a.shape[:1] + a.shape[2:]), "scatter": (_scatter_copies, 3, lambda a: a.shape),
           "share": (_share_copies, 1, lambda a: a.shape)}


def matmul_tn(x, y, scale=1.0, carry=None):
    t, k = x.shape
    n = y.shape[1]
    tk = _tile(k, 1024) if k % 1024 == 0 else _tile(k, 1408)
    tn = n if n <= 1536 else (1024 if n % 1024 == 0 else 1408)
    assert n % tn == 0
    tt = _tile(t, 2048)
    grid = (k // tk, n // tn, t // tt)
    arrays = [] if carry is None else list(carry[1])
    na = len(arrays)

    def body(x_ref, y_ref, *rest):
        o_ref = rest[na]
        if carry is not None:
            make, _, _ = CARRIED[carry[0]]
            copies = lambda: make(rest[:na], rest[na + 1:2 * na + 1], rest[2 * na + 1], rest[2 * na + 2])
            pid = [pl.program_id(a) for a in range(3)]

            @pl.when((pid[0] == 0) & (pid[1] == 0) & (pid[2] == 0))
            def _():
                for cp in copies():
                    cp.start()

        @pl.when(pl.program_id(2) == 0)
        def _():
            o_ref[...] = jnp.zeros_like(o_ref)

        yv = y_ref[...]
        if scale != 1.0:
            yv = yv * scale
        o_ref[...] += _dot_tn(x_ref[...].astype(BF16), yv.astype(BF16))

        if carry is not None:
            @pl.when((pid[0] == grid[0] - 1) & (pid[1] == grid[1] - 1) & (pid[2] == grid[2] - 1))
            def _():
                for cp in copies():
                    cp.wait()

    out_shape = [jax.ShapeDtypeStruct((k, n), F32)]
    scratch = []
    if carry is not None:
        _, per_array, out_of = CARRIED[carry[0]]
        out_shape += [jax.ShapeDtypeStruct(out_of(a), a.dtype) for a in arrays]
        scratch = [pltpu.SemaphoreType.DMA((per_array * na,))] * 2
    res = pl.pallas_call(
        body, name="matmul_tn" if carry is None else "matmul_tn_" + carry[0],
        grid=grid,
        in_specs=[pl.BlockSpec((tt, tk), lambda i, j, s: (s, i)), pl.BlockSpec((tt, tn), lambda i, j, s: (s, j))] + [ANY] * na,
        out_specs=[pl.BlockSpec((tk, tn), lambda i, j, s: (i, j))] + [ANY] * na,
        out_shape=out_shape, scratch_shapes=scratch,
        compiler_params=_arb(3),
    )(x, y, *arrays)
    return res[0] if carry is None else (res[0], list(res[1:]))


FFN_CHUNKS = 2


class _FfnGather:
    def __init__(self, gu_sh, d_sh, gu_full, d_full, send_sems, recv_sems, local_sems):
        self.sh, self.full = (gu_sh, d_sh), (gu_full, d_full)
        self.send_sems, self.recv_sems, self.local_sems = send_sems, recv_sems, local_sems
        self.mx, self.my, self.mc = _me()
        self.me = 2 * self.mx + self.my
        self.w4, self.f4 = gu_sh.shape[1], d_sh.shape[0]
        self.hg, self.hd = gu_sh.shape[0] // 2, d_sh.shape[0] // 2

    def _window(self, a, chip, core=None):
        gu_full, d_full = self.full
        if a == 0:
            rows = pl.ds(0, 2 * self.hg) if core is None else pl.ds(pl.multiple_of(core * self.hg, 16), self.hg)
            return gu_full.at[rows, pl.ds(pl.multiple_of(chip * self.w4, LANE), self.w4)]
        if core is None:
            return d_full.at[pl.ds(pl.multiple_of(chip * self.f4, 16), self.f4), :]
        return d_full.at[pl.ds(pl.multiple_of(chip * self.f4 + core * self.hd, 16), self.hd), :]

    def _half(self, a):
        n = (self.hg, self.hd)[a]
        return self.sh[a].at[pl.ds(pl.multiple_of(self.mc * n, 16), n), :]

    def _peer(self, k):
        return _flip(self.mx, k >> 1), _flip(self.my, k & 1)

    def _fetch(self, k, a):
        px, py = self._peer(k)
        i = 2 * (k - 1) + a
        return pltpu.make_async_remote_copy(src_ref=self._half(a), dst_ref=self._window(a, self.me, self.mc),
                                            send_sem=self.send_sems.at[i], recv_sem=self.recv_sems.at[i],
                                            device_id=(px, py, self.mc), device_id_type=MESH)

    def _relay(self, k, a):
        px, py = self._peer(k)
        got = self._window(a, 2 * px + py, self.mc)
        i = 6 + 2 * (k - 1) + a
        return pltpu.make_async_remote_copy(src_ref=got, dst_ref=got, send_sem=self.send_sems.at[i],
                                            recv_sem=self.recv_sems.at[i], device_id=(self.mx, self.my, 1 - self.mc),
                                            device_id_type=MESH)

    def _own(self, a):
        return pltpu.make_async_copy(self.sh[a], self._window(a, self.me), self.local_sems.at[a])

    def start(self):
        for a in (0, 1):
            self._own(a).start()
            for k in (1, 2, 3):
                self._fetch(k, a).start()

    def relay(self):
        for k in (1, 2, 3):
            for a in (0, 1):
                self._fetch(k, a).wait_recv()
                self._relay(k, a).start()

    def finish(self):
        for k in (1, 2, 3):
            for a in (0, 1):
                self._fetch(k, a).wait_send()
                self._relay(k, a).wait()
        for a in (0, 1):
            self._own(a).wait()


def ffn_fwd(h, nw, wgu, wd, nxt=None, also=()):
    t, d = h.shape
    f = wd.shape[0]
    fc = f // FFN_CHUNKS
    tm = _tile(t, 512)
    nsteps = t // tm
    also = list(also)
    na = len(also)

    def body(h_ref, nw_ref, wgu_hbm, wd_hbm, *rest):
        gathers = []
        if nxt is None:
            o_ref, gu_ref, a_ref, wgu_ref, wd_ref, sem = rest
        else:
            gu_sh, d_sh = rest[:2]
            o_ref, gu_ref, a_ref, gu_full, d_full = rest[2 + na:7 + na]
            wgu_ref, wd_ref, sem, send_sems, recv_sems, local_sems = rest[7 + 2 * na:13 + 2 * na]
            gathers.append(_FfnGather(gu_sh, d_sh, gu_full, d_full, send_sems, recv_sems, local_sems))
            if na:
                gathers.append(_SplitGather(rest[2:2 + na], rest[7 + na:7 + 2 * na], *rest[13 + 2 * na:]))

        @pl.when(pl.program_id(0) == 0)
        def _():
            for gather in gathers:
                gather.start()

        _resident(wgu_hbm, wgu_ref, sem)
        _resident(wd_hbm, wd_ref, sem)

        @pl.when(pl.program_id(0) == (3 * nsteps) // 4)
        def _():
            for gather in gathers:
                gather.relay()

        x = h_ref[...]
        hn = _rms(x, nw_ref[...])[0].astype(BF16)
        acc = jnp.zeros((tm, d), F32)
        for c in range(FFN_CHUNKS):
            gs, us = slice(c * fc, (c + 1) * fc), slice(f + c * fc, f + (c + 1) * fc)
            g = _dot(hn, wgu_ref[:, gs])
            u = _dot(hn, wgu_ref[:, us])
            a = (_silu(g) * u).astype(BF16)
            gu_ref[:, gs] = g.astype(BF16)
            gu_ref[:, us] = u.astype(BF16)
            a_ref[:, gs] = a
            acc = acc + _dot(a, wd_ref[gs, :])
        o_ref[...] = x + 0.5 * acc

        @pl.when(pl.program_id(0) == nsteps - 1)
        def _():
            for gather in gathers:
                gather.finish()

    row = lambda w: pl.BlockSpec((tm, w), lambda i: (i, 0))
    in_specs = [row(d), pl.BlockSpec((1, d), lambda i: (0, 0)), ANY, ANY]
    out_specs = [row(d), row(2 * f), row(f)]
    out_shape = [jax.ShapeDtypeStruct((t, d), F32), jax.ShapeDtypeStruct((t, 2 * f), BF16),
                 jax.ShapeDtypeStruct((t, f), BF16)]
    scratch = [pltpu.VMEM((d, 2 * f), BF16), pltpu.VMEM((f, d), BF16), pltpu.SemaphoreType.DMA]
    args = (h, nw, wgu, wd)
    if nxt is not None:
        in_specs += [ANY] * (2 + na)
        out_specs += [ANY] * (2 + na)
        out_shape += [jax.ShapeDtypeStruct((d, 2 * f), BF16), jax.ShapeDtypeStruct((f, d), BF16)]
        out_shape += [jax.ShapeDtypeStruct((4,) + a.shape, a.dtype) for a in also]
        scratch += [pltpu.SemaphoreType.DMA((12,)), pltpu.SemaphoreType.DMA((12,)), pltpu.SemaphoreType.DMA((2,))]
        if na:
            scratch += [pltpu.SemaphoreType.DMA((6 * na,)), pltpu.SemaphoreType.DMA((6 * na,))]
        args += tuple(nxt) + tuple(also)
    return pl.pallas_call(
        body, name="ffn_fwd_gather" if nxt is not None else "ffn_fwd",
        grid=(nsteps,),
        in_specs=in_specs, out_specs=out_specs, out_shape=out_shape, scratch_shapes=scratch,
        compiler_params=_arb(1),
    )(*args)


def ffn_bwd(h, nw, wgu, wd, gu, dh):
    t, d = h.shape
    f = wd.shape[0]
    fc = f // FFN_CHUNKS
    tm = _tile(t, 256)

    def body(h_ref, nw_ref, wgu_hbm, wd_hbm, gu_ref, dh_ref, o_ref, hn_ref, dgu_ref, dnw_ref, wgu_ref, wd_ref, sem):
        _resident(wgu_hbm, wgu_ref, sem)
        _resident(wd_hbm, wd_ref, sem)

        @pl.when(pl.program_id(0) == 0)
        def _():
            dnw_ref[...] = jnp.zeros_like(dnw_ref)

        nw = nw_ref[...]
        hn32, xhat, r = _rms(h_ref[...], nw)
        hn_ref[...] = hn32.astype(BF16)
        dh = dh_ref[...]
        dout = (0.5 * dh).astype(BF16)
        dhn = jnp.zeros((tm, d), F32)
        for c in range(FFN_CHUNKS):
            gs, us = slice(c * fc, (c + 1) * fc), slice(f + c * fc, f + (c + 1) * fc)
            g = gu_ref[:, gs].astype(F32)
            u = gu_ref[:, us].astype(F32)
            sg = _sigmoid(g)
            sil = g * sg
            da = _dot_nt(dout, wd_ref[gs, :])
            dg = (da * u * (sg * (1.0 + g * (1.0 - sg)))).astype(BF16)
            du = (da * sil).astype(BF16)
            dgu_ref[:, gs] = dg
            dgu_ref[:, us] = du
            dhn = dhn + _dot_nt(dg, wgu_ref[:, gs]) + _dot_nt(du, wgu_ref[:, us])
        dx, dnw = _rms_bwd(dhn, nw, xhat, r)
        dnw_ref[...] += dnw
        o_ref[...] = dh + dx

    row = pl.BlockSpec((tm, d), lambda i: (i, 0))
    vec = pl.BlockSpec((1, d), lambda i: (0, 0))
    return pl.pallas_call(
        body, name="ffn_bwd",
        grid=(t // tm,),
        in_specs=[row, vec, ANY, ANY, pl.BlockSpec((tm, 2 * f), lambda i: (i, 0)), row],
        out_specs=[row, row, pl.BlockSpec((tm, 2 * f), lambda i: (i, 0)), vec],
        out_shape=[jax.ShapeDtypeStruct((t, d), F32), jax.ShapeDtypeStruct((t, d), BF16),
                   jax.ShapeDtypeStruct((t, 2 * f), BF16), jax.ShapeDtypeStruct((1, d), F32)],
        scratch_shapes=[pltpu.VMEM((d, 2 * f), BF16), pltpu.VMEM((f, d), BF16), pltpu.SemaphoreType.DMA],
        compiler_params=_arb(1),
    )(h, nw, wgu, wd, gu, dh)


def loss_head(h, nw, target):
    t, d = h.shape
    tm = _tile(t, 512)

    def body(h_ref, nw_ref, tg_ref, dh_ref, loss_ref, dnw_ref):
        @pl.when(pl.program_id(0) == 0)
        def _():
            loss_ref[...] = jnp.zeros_like(loss_ref)
            dnw_ref[...] = jnp.zeros_like(dnw_ref)

        nw = nw_ref[...]
        y, xhat, r = _rms(h_ref[...], nw)
        e = y - tg_ref[...]
        loss_ref[...] += 0.5 * jnp.sum(jnp.mean(e * e, axis=-1, keepdims=True), axis=0, keepdims=True)
        dx, dnw = _rms_bwd(e * (1.0 / d), nw, xhat, r)
        dnw_ref[...] += dnw
        dh_ref[...] = dx

    row = pl.BlockSpec((tm, d), lambda i: (i, 0))
    vec = pl.BlockSpec((1, d), lambda i: (0, 0))
    return pl.pallas_call(
        body, name="loss_head",
        grid=(t // tm,),
        in_specs=[row, vec, row],
        out_specs=[row, pl.BlockSpec((8, LANE), lambda i: (0, 0)), vec],
        out_shape=[jax.ShapeDtypeStruct((t, d), F32), jax.ShapeDtypeStruct((8, LANE), F32),
                   jax.ShapeDtypeStruct((1, d), F32)],
        compiler_params=_arb(1),
    )(h, nw, target)


def adamw(w, g, m, v):
    r, c = w.shape
    tr = r
    while tr * c * 4 > (1 << 20) and tr % 16 == 0:
        tr //= 2

    def body(w_ref, g_ref, m_ref, v_ref, d_ref, nm_ref, nv_ref):
        g = g_ref[...]
        m = ADAM_B1 * m_ref[...] + (1.0 - ADAM_B1) * g
        v = ADAM_B2 * v_ref[...] + (1.0 - ADAM_B2) * (g * g)
        m_hat = m / (1.0 - ADAM_B1 ** ADAM_STEP)
        v_hat = v / (1.0 - ADAM_B2 ** ADAM_STEP)
        d_ref[...] = -ADAM_LR * (m_hat / (jnp.sqrt(v_hat) + ADAM_EPS) + ADAM_WD * w_ref[...])
        nm_ref[...] = m
        nv_ref[...] = v

    blk = pl.BlockSpec((tr, c), lambda i: (i, 0))
    return pl.pallas_call(
        body, name="adamw",
        grid=(r // tr,),
        in_specs=[blk] * 4, out_specs=[blk] * 3,
        out_shape=[jax.ShapeDtypeStruct((r, c), F32)] * 3,
        compiler_params=_arb(1),
    )(w, g, m, v)


ATTN_P_W = ATTN_Q_W + 2 * ATTN_KV_W
ATTN_SCALE = ATTN_HEAD_DIM ** -0.5


def _rope_group(t, tab, sign):
    return (t * tab[:, 0:LANE] + sign * pltpu.roll(t, 8, 1) * tab[:, LANE:2 * LANE]
            + sign * pltpu.roll(t, LANE - 8, 1) * tab[:, 2 * LANE:3 * LANE])


def _rope(t, tab, sign=1.0):
    return jnp.concatenate([_rope_group(t[:, s:s + LANE], tab, sign) for s in range(0, t.shape[1], LANE)], axis=1)


def _attn_heads(qs, ks, vs, sinks, first):
    b = ATTN_BLOCK
    rows = ATTN_GROUP * b
    qi = _iota2((rows, 2 * b), 0) % b
    kj = _iota2((rows, 2 * b), 1)
    rel = qi + b - kj
    valid = (rel >= 0) & (rel < b) & ((kj >= b) | jnp.logical_not(first))
    ss = _each(lambda q, k: jnp.where(valid, _mm1_nt(q, k) * ATTN_SCALE, NEG), qs, ks)
    ms = _each(lambda s, sink: lax.stop_gradient(jnp.maximum(jnp.max(s, axis=-1, keepdims=True), sink)), ss, sinks)
    ps = _each(lambda s, m: jnp.exp(s - m), ss, ms)
    dens = _each(lambda p, sink, m: jnp.sum(p, axis=-1, keepdims=True) + jnp.exp(sink - m), ps, sinks, ms)
    return _each(lambda p, den, v: _mm1(p / den, v), ps, dens, vs)


def _attn_prepare(p_ref, kvp_ref, tab_ref, tabp_ref, sink_ref):
    b = ATTN_BLOCK
    hd = ATTN_HEAD_DIM
    tab, tabp = tab_ref[...], tabp_ref[...]
    q = _rope(p_ref[:, 0:ATTN_Q_W], tab)
    kc = _rope(p_ref[:, ATTN_Q_W:ATTN_Q_W + ATTN_KV_W], tab)
    kp = _rope(kvp_ref[:, 0:ATTN_KV_W], tabp)
    vc = p_ref[:, ATTN_Q_W + ATTN_KV_W:ATTN_P_W]
    vp = kvp_ref[:, ATTN_KV_W:2 * ATTN_KV_W]
    sk = sink_ref[...]
    qs, ks, vs, sinks = [], [], [], []
    for h in range(ATTN_KV_HEADS):
        heads = [ATTN_GROUP * h + g for g in range(ATTN_GROUP)]
        qs.append(jnp.concatenate([q[:, i * hd:(i + 1) * hd] for i in heads], axis=0))
        ks.append(jnp.concatenate([kp[:, h * hd:(h + 1) * hd], kc[:, h * hd:(h + 1) * hd]], axis=0))
        vs.append(jnp.concatenate([vp[:, h * hd:(h + 1) * hd], vc[:, h * hd:(h + 1) * hd]], axis=0))
        sinks.append(jnp.concatenate([jnp.broadcast_to(sk[:, i:i + 1], (b, 1)) for i in heads], axis=0))
    return qs, ks, vs, sinks


def _unstack_heads(xs):
    b = ATTN_BLOCK
    return jnp.concatenate([x[g * b:(g + 1) * b] for x in xs for g in range(ATTN_GROUP)], axis=1)


def _attn_specs(nb):
    b = ATTN_BLOCK
    prev = lambda n: jnp.maximum(n - 1, 0)
    return [pl.BlockSpec((b, ATTN_P_W), lambda n: (n, 0)),
            pl.BlockSpec((b, 2 * ATTN_KV_W), lambda n: (prev(n), ATTN_Q_W // (2 * ATTN_KV_W))),
            pl.BlockSpec((b, 3 * LANE), lambda n: (n, 0)),
            pl.BlockSpec((b, 3 * LANE), lambda n: (prev(n), 0)),
            pl.BlockSpec((1, ATTN_Q_HEADS), lambda n: (0, 0))]


def attn_fwd(p, tab, sinks):
    t = p.shape[0]
    nb = t // ATTN_BLOCK

    def body(p_ref, kvp_ref, tab_ref, tabp_ref, sink_ref, o_ref):
        qs, ks, vs, sk = _attn_prepare(p_ref, kvp_ref, tab_ref, tabp_ref, sink_ref)
        os_ = _attn_heads(qs, ks, vs, sk, pl.program_id(0) == 0)
        o_ref[...] = _unstack_heads(os_).astype(BF16)

    return pl.pallas_call(
        body, name="attn_fwd",
        grid=(nb,),
        in_specs=_attn_specs(nb),
        out_specs=pl.BlockSpec((ATTN_BLOCK, ATTN_Q_W), lambda n: (n, 0)),
        out_shape=jax.ShapeDtypeStruct((t, ATTN_Q_W), BF16),
        compiler_params=_arb(1),
    )(p, p, tab, tab, sinks)


def attn_bwd(p, tab, sinks, do):
    t = p.shape[0]
    b = ATTN_BLOCK
    hd = ATTN_HEAD_DIM
    nb = t // b

    def body(p_ref, kvp_ref, tab_ref, tabp_ref, sink_ref, do_ref, dq_ref, dkvc_ref, dkvp_ref, dsink_ref):
        n = pl.program_id(0)

        @pl.when(n == 0)
        def _():
            dsink_ref[...] = jnp.zeros_like(dsink_ref)

        qs, ks, vs, sk = _attn_prepare(p_ref, kvp_ref, tab_ref, tabp_ref, sink_ref)
        first = n == 0
        _, vjp = jax.vjp(lambda a, bb, c, d: _attn_heads(a, bb, c, d, first), qs, ks, vs, sk)
        do = do_ref[...]
        dos = [jnp.concatenate([do[:, i * hd:(i + 1) * hd] for i in range(ATTN_GROUP * h, ATTN_GROUP * (h + 1))], axis=0)
               for h in range(ATTN_KV_HEADS)]
        dqs, dks, dvs, dsk = vjp(dos)
        tab, tabp = tab_ref[...], tabp_ref[...]
        dq_ref[...] = _rope(_unstack_heads([x.astype(F32) for x in dqs]), tab, -1.0).astype(BF16)
        dkc = jnp.concatenate([x.astype(F32)[b:] for x in dks], axis=1)
        dkp = jnp.concatenate([x.astype(F32)[:b] for x in dks], axis=1)
        dkvc_ref[:, 0:ATTN_KV_W] = _rope(dkc, tab, -1.0)
        dkvp_ref[:, 0:ATTN_KV_W] = _rope(dkp, tabp, -1.0)
        dkvc_ref[:, ATTN_KV_W:] = jnp.concatenate([x.astype(F32)[b:] for x in dvs], axis=1)
        dkvp_ref[:, ATTN_KV_W:] = jnp.concatenate([x.astype(F32)[:b] for x in dvs], axis=1)
        parts = [jnp.sum(d[g * b:(g + 1) * b], axis=0, keepdims=True) for d in dsk for g in range(ATTN_GROUP)]
        dsink_ref[...] += jnp.concatenate(parts, axis=1)

    blk = lambda w: pl.BlockSpec((b, w), lambda n: (n, 0))
    return pl.pallas_call(
        body, name="attn_bwd",
        grid=(nb,),
        in_specs=_attn_specs(nb) + [blk(ATTN_Q_W)],
        out_specs=[blk(ATTN_Q_W), blk(2 * ATTN_KV_W), blk(2 * ATTN_KV_W),
                   pl.BlockSpec((1, ATTN_Q_HEADS), lambda n: (0, 0))],
        out_shape=[jax.ShapeDtypeStruct((t, ATTN_Q_W), BF16), jax.ShapeDtypeStruct((t, 2 * ATTN_KV_W), F32),
                   jax.ShapeDtypeStruct((t, 2 * ATTN_KV_W), F32), jax.ShapeDtypeStruct((1, ATTN_Q_HEADS), F32)],
        compiler_params=_arb(1),
    )(p, p, tab, tab, sinks, do)


def kv_combine(dkvc, dkvp):
    t, w = dkvc.shape
    b = ATTN_BLOCK
    nb = t // b

    def body(c_ref, p_ref, o_ref):
        nxt = jnp.where(pl.program_id(0) < nb - 1, p_ref[...], 0.0)
        o_ref[...] = (c_ref[...] + nxt).astype(BF16)

    return pl.pallas_call(
        body, name="kv_combine",
        grid=(nb,),
        in_specs=[pl.BlockSpec((b, w), lambda n: (n, 0)),
                  pl.BlockSpec((b, w), lambda n: (jnp.minimum(n + 1, nb - 1), 0))],
        out_specs=pl.BlockSpec((b, w), lambda n: (n, 0)),
        out_shape=jax.ShapeDtypeStruct((t, w), BF16),
        compiler_params=_arb(1),
    )(dkvc, dkvp)


def _me():
    return lax.axis_index("x"), lax.axis_index("y"), lax.axis_index("c")


def _flip(v, bit):
    return 1 - v if bit else v


def _chip_index():
    return 2 * lax.axis_index("x") + lax.axis_index("y")


class _SplitGather:
    def __init__(self, x_refs, o_refs, send_sems, recv_sems):
        self.x_refs, self.o_refs, self.send_sems, self.recv_sems = x_refs, o_refs, send_sems, recv_sems
        self.mx, self.my, self.mc = _me()
        self.me = 2 * self.mx + self.my

    def _rows(self, a):
        rh = self.x_refs[a].shape[0] // 2
        return pl.ds(pl.multiple_of(self.mc * rh, 16), rh)

    def _fetch(self, a, k):
        rows, i = self._rows(a), 6 * a + k - 1
        return pltpu.make_async_remote_copy(
            src_ref=self.x_refs[a].at[rows], dst_ref=self.o_refs[a].at[self.me, rows], send_sem=self.send_sems.at[i],
            recv_sem=self.recv_sems.at[i], device_id=(_flip(self.mx, k >> 1), _flip(self.my, k & 1), self.mc),
            device_id_type=MESH)

    def _relay(self, a, k):
        theirs = self.o_refs[a].at[2 * _flip(self.mx, k >> 1) + _flip(self.my, k & 1), self._rows(a)]
        i = 6 * a + 2 + k
        return pltpu.make_async_remote_copy(src_ref=theirs, dst_ref=theirs, send_sem=self.send_sems.at[i],
                                            recv_sem=self.recv_sems.at[i], device_id=(self.mx, self.my, 1 - self.mc),
                                            device_id_type=MESH)

    def _each_copy(self):
        return [(a, k) for a in range(len(self.x_refs)) for k in (1, 2, 3)]

    def start(self):
        for a, k in self._each_copy():
            self._fetch(a, k).start()

    def relay(self):
        for a, k in self._each_copy():
            self._fetch(a, k).wait_recv()
            self._relay(a, k).start()

    def finish(self):
        for a, k in self._each_copy():
            self._fetch(a, k).wait_send()
            self._relay(a, k).wait()


def _fill_own_slot(got, x):
    return lax.dynamic_update_slice(got, x[None], (_chip_index(), 0, 0))


def allgather_chips(x, split):
    r, c = x.shape

    def body(x_ref, o_ref, send_sems, recv_sems):
        if split:
            gather = _SplitGather([x_ref], [o_ref], send_sems, recv_sems)
            gather.start()
            gather.relay()
            gather.finish()
            return
        mx, my, mc = _me()
        fetched = []
        for k in (1, 2, 3):
            cp = pltpu.make_async_remote_copy(src_ref=x_ref, dst_ref=o_ref.at[2 * mx + my],
                                              send_sem=send_sems.at[k - 1], recv_sem=recv_sems.at[k - 1],
                                              device_id=(_flip(mx, k >> 1), _flip(my, k & 1), mc), device_id_type=MESH)
            cp.start()
            fetched.append(cp)
        for cp in fetched:
            cp.wait()

    got = pl.pallas_call(
        body, name="allgather_chips",
        in_specs=[ANY], out_specs=ANY,
        out_shape=jax.ShapeDtypeStruct((4, r, c), x.dtype),
        scratch_shapes=[pltpu.SemaphoreType.DMA((6,)), pltpu.SemaphoreType.DMA((6,))],
    )(x)
    return _fill_own_slot(got, x)


def exchange(kind, arrays):
    make, per_array, out_of = CARRIED[kind]
    na = len(arrays)

    def body(*refs):
        copies = make(refs[:na], refs[na:2 * na], refs[2 * na], refs[2 * na + 1])
        for cp in copies:
            cp.start()
        for cp in copies:
            cp.wait()

    return pl.pallas_call(
        body, name=kind + "_exchange",
        in_specs=[ANY] * na, out_specs=[ANY] * na,
        out_shape=[jax.ShapeDtypeStruct(out_of(a), a.dtype) for a in arrays],
        scratch_shapes=[pltpu.SemaphoreType.DMA((per_array * na,))] * 2,
    )(*arrays)


def _row_tile(r, c):
    return max(d for d in range(16, r + 1, 16) if r % d == 0 and d * c * 4 <= (2 << 20))


def add_pair(g, got, core):
    _, _, r, c = g.shape
    tr = _row_tile(r, c)

    def body(core_ref, g_ref, got_ref, p32_ref, p16_ref):
        s = g_ref[...] + got_ref[...]
        p32_ref[...] = s
        p16_ref[...] = s.astype(BF16)

    blk = pl.BlockSpec((None, tr, c), lambda j, i, core_ref: (j, i, 0))
    return pl.pallas_call(
        body, name="add_pair",
        grid_spec=pltpu.PrefetchScalarGridSpec(
            num_scalar_prefetch=1, grid=(4, r // tr),
            in_specs=[pl.BlockSpec((None, None, tr, c), lambda j, i, core_ref: (j, core_ref[0], i, 0)), blk],
            out_specs=[blk, blk]),
        out_shape=[jax.ShapeDtypeStruct((4, r, c), F32), jax.ShapeDtypeStruct((4, r, c), BF16)],
        compiler_params=_arb(2),
    )(core, g, got)


def sum_slots(p32, q16, order):
    _, r, c = p32.shape
    tr = _row_tile(r, c)

    def body(order_ref, own_ref, a_ref, b_ref, c_ref, o_ref):
        o_ref[...] = ((own_ref[...] + a_ref[...].astype(F32)) + b_ref[...].astype(F32)) + c_ref[...].astype(F32)

    slot = lambda k: pl.BlockSpec((None, tr, c), functools.partial(lambda k, i, order_ref: (order_ref[k], i, 0), k))
    return pl.pallas_call(
        body, name="sum_slots",
        grid_spec=pltpu.PrefetchScalarGridSpec(
            num_scalar_prefetch=1, grid=(r // tr,),
            in_specs=[slot(0), slot(1), slot(2), slot(3)],
            out_specs=pl.BlockSpec((tr, c), lambda i, order_ref: (i, 0))),
        out_shape=jax.ShapeDtypeStruct((r, c), F32),
        compiler_params=_arb(1),
    )(order, p32, q16, q16, q16)


def allreduce_small(x):
    r, c = x.shape

    def body(x_ref, o_ref, buf, send_sems, recv_sems):
        mx, my, mc = _me()
        me = 4 * mx + 2 * my + mc
        buf[pl.ds(me, 1)] = x_ref[...][None]
        copies = []
        for k in range(1, 8):
            peer = (_flip(mx, k >> 2), _flip(my, (k >> 1) & 1), _flip(mc, k & 1))
            cp = pltpu.make_async_remote_copy(src_ref=x_ref, dst_ref=buf.at[me], send_sem=send_sems.at[k - 1],
                                              recv_sem=recv_sems.at[k - 1], device_id=peer, device_id_type=MESH)
            cp.start()
            copies.append(cp)
        for cp in copies:
            cp.wait()
        acc = buf[0]
        for d in range(1, 8):
            acc = acc + buf[d]
        o_ref[...] = acc

    return pl.pallas_call(
        body, name="allreduce_small",
        out_shape=jax.ShapeDtypeStruct((r, c), F32),
        scratch_shapes=[pltpu.VMEM((8, r, c), F32), pltpu.SemaphoreType.DMA((7,)), pltpu.SemaphoreType.DMA((7,))],
    )(x)


class GradReduce:
    def __init__(self, grads, shard_shapes):
        self.shapes = shard_shapes
        self.groups = _by_width(shard_shapes)
        self.core = lax.axis_index("c")
        me = _chip_index()
        self.order = jnp.stack([me, me ^ 1, me ^ 2, me ^ 3]).astype(jnp.int32)
        self.g = []
        for names in self.groups.values():
            slots = []
            for j in range(4):
                parts = []
                for n in names:
                    w = shard_shapes[n][1 + SHARDED[n]]
                    parts += [lax.slice_in_dim(gl, j * w, (j + 1) * w, axis=SHARDED[n]) for gl in grads[n]]
                slots.append(jnp.concatenate(parts, axis=0))
            g = jnp.stack(slots)
            self.g.append(g.reshape(4, 2, g.shape[1] // 2, g.shape[2]))
        self.p32 = self.p16 = None
        self.q16 = [None] * len(self.g)
        n = len(self.g)
        self.pieces = [list(range(2, n)), [1], [0]] if n > 2 else [[i] for i in reversed(range(n))]
        self.pieces = [p for p in self.pieces if p]
        self.stage = 0

    def next_job(self):
        if self.stage == 0:
            return "pair", self.g
        if self.stage <= len(self.pieces):
            return "scatter", [self.p16[i] for i in self.pieces[self.stage - 1]]
        return None

    def deliver(self, res):
        if self.stage == 0:
            core = self.core.astype(jnp.int32)[None]
            pairs = [add_pair(g, got, core) for g, got in zip(self.g, res)]
            self.p32, self.p16 = [p[0] for p in pairs], [p[1] for p in pairs]
        else:
            for i, q in zip(self.pieces[self.stage - 1], res):
                self.q16[i] = q
        self.stage += 1

    def finish(self):
        while (job := self.next_job()) is not None:
            self.deliver(exchange(*job))
        mine = [sum_slots(p, q, self.order) for p, q in zip(self.p32, self.q16)]
        theirs = exchange("share", mine)
        out = {}
        for names, m, t in zip(self.groups.values(), mine, theirs):
            red = jnp.where(self.core == 0, jnp.concatenate([m, t]), jnp.concatenate([t, m]))
            off = 0
            for n in names:
                layers, a, _ = self.shapes[n]
                out[n] = red[off:off + layers * a].reshape(self.shapes[n])
                off += layers * a
        return out


SHARDED = {"ffn1_w_gate_up": 1, "ffn1_w_down": 0, "ffn2_w_gate_up": 1, "ffn2_w_down": 0, "attn_w_in": 1,
           "attn_w_out": 0, "gdn_w_in": 1, "gdn_w_out": 0, "gdn_conv_w": 1}
REPLICATED = ["ffn1_norm", "mix_norm", "ffn2_norm", "attn_b_in", "attn_sinks", "attn_b_out", "gdn_A_log",
              "gdn_dt_bias", "gdn_norm_w", "final_norm"]
WEIGHTS = ["ffn1_norm", "ffn1_w_gate_up", "ffn1_w_down", "mix_norm", "ffn2_norm", "ffn2_w_gate_up", "ffn2_w_down",
           "attn_w_in", "attn_b_in", "attn_sinks", "attn_w_out", "attn_b_out", "gdn_w_in", "gdn_conv_w", "gdn_A_log",
           "gdn_dt_bias", "gdn_norm_w", "gdn_w_out", "final_norm"]


def _pack_rows(flats, dtype, width, row_multiple):
    flat = jnp.concatenate([f.astype(dtype).reshape(-1) for f in flats])
    per = width * row_multiple
    pad = (-flat.shape[0]) % per
    if pad:
        flat = jnp.concatenate([flat, jnp.zeros((pad,), dtype)])
    return flat.reshape(-1, width)


def _by_width(shapes):
    groups = {}
    for n, shape in shapes.items():
        groups.setdefault(shape[-1], []).append(n)
    return groups


def _pack_groups(shards, dtype):
    groups = _by_width({n: s.shape for n, s in shards.items()})
    packed = [jnp.concatenate([shards[n].astype(dtype).reshape(-1, width) for n in names], axis=0)
              for width, names in groups.items()]

    def unpack(gots):
        full = {}
        for names, got in zip(groups.values(), gots):
            off = 0
            for n in names:
                layers, a, _ = shards[n].shape
                full[n] = [jnp.concatenate([got[j, off + l * a:off + (l + 1) * a] for j in range(4)], axis=SHARDED[n])
                           for l in range(layers)]
                off += layers * a
        return full

    return packed, unpack


def _gather_weights(shards, dtype, split=True):
    packed, unpack = _pack_groups(shards, dtype)
    return unpack([allgather_chips(p, split) for p in packed])


def _small_pack(items):
    rows = []
    for a in items:
        f = a.astype(F32).reshape(-1)
        pad = (-f.shape[0]) % LANE
        rows.append(jnp.concatenate([f, jnp.zeros((pad,), F32)]) if pad else f)
    return _pack_rows(rows, F32, LANE, 8)


def _small_unpack(buf, shapes):
    flat = buf.reshape(-1)
    out, off = [], 0
    for shape in shapes:
        size = 1
        for s in shape:
            size *= s
        out.append(flat[off:off + size].reshape(shape))
        off += size + (-size) % LANE
    return out


def _rope_table(positions):
    t = positions.shape[0]
    inv_freq = ROPE_THETA ** (-jnp.arange(0, ROPE_DIM, 2, dtype=F32) / ROPE_DIM)
    ang = positions.astype(F32)[:, None] * inv_freq
    cos, sin = jnp.cos(ang), jnp.sin(ang)
    rest = ATTN_HEAD_DIM - ROPE_DIM
    zeros = lambda n: jnp.zeros((t, n), F32)
    c64 = jnp.concatenate([cos, cos, jnp.ones((t, rest), F32)], axis=1)
    s_up = jnp.concatenate([zeros(ROPE_DIM // 2), sin, zeros(rest)], axis=1)
    s_dn = jnp.concatenate([-sin, zeros(ROPE_DIM // 2 + rest)], axis=1)
    return jnp.concatenate([c64, c64, s_up, s_up, s_dn, s_dn], axis=1)


def _as2d(a):
    return a.reshape(-1, a.shape[-1]) if a.ndim > 1 else a.reshape(1, -1)


def kernel(x, positions, ffn1_norm, ffn1_w_gate_up, ffn1_w_down, mix_norm, ffn2_norm, ffn2_w_gate_up, ffn2_w_down, attn_w_in, attn_b_in, attn_sinks, attn_w_out, attn_b_out, gdn_w_in, gdn_conv_w, gdn_A_log, gdn_dt_bias, gdn_norm_w, gdn_w_out, final_norm, loss_target, m_ffn1_norm, m_ffn1_w_gate_up, m_ffn1_w_down, m_mix_norm, m_ffn2_norm, m_ffn2_w_gate_up, m_ffn2_w_down, m_attn_w_in, m_attn_b_in, m_attn_sinks, m_attn_w_out, m_attn_b_out, m_gdn_w_in, m_gdn_conv_w, m_gdn_A_log, m_gdn_dt_bias, m_gdn_norm_w, m_gdn_w_out, m_final_norm, v_ffn1_norm, v_ffn1_w_gate_up, v_ffn1_w_down, v_mix_norm, v_ffn2_norm, v_ffn2_w_gate_up, v_ffn2_w_down, v_attn_w_in, v_attn_b_in, v_attn_sinks, v_attn_w_out, v_attn_b_out, v_gdn_w_in, v_gdn_conv_w, v_gdn_A_log, v_gdn_dt_bias, v_gdn_norm_w, v_gdn_w_out, v_final_norm):
    given = dict(locals())
    w = {n: given[n] for n in WEIGHTS}
    d = D_MODEL
    h = x[0]
    target = loss_target[0]
    depth = ffn1_norm.shape[0]

    big = [n for n in SHARDED if n != "gdn_conv_w"]
    first = _gather_weights({n: w[n][0:1] for n in ("ffn1_w_gate_up", "ffn1_w_down")}, BF16)
    ffn_order = [(tag, l) for l in range(depth) for tag in ("ffn1", "ffn2")]
    ffn_w = {ffn_order[0]: (first["ffn1_w_gate_up"][0], first["ffn1_w_down"][0])}
    mixers_packed, mixers_unpack = _pack_groups({n: w[n] for n in big if not n.startswith("ffn")}, BF16)
    full = {}

    def run_ffn(tag, l, h):
        i = ffn_order.index((tag, l))
        wgu, wd = ffn_w[(tag, l)]
        nw = w[tag + "_norm"][l][None]
        if i + 1 == len(ffn_order):
            return ffn_fwd(h, nw, wgu, wd)
        ntag, nl = ffn_order[i + 1]
        shards = (w[ntag + "_w_gate_up"][nl].astype(BF16), w[ntag + "_w_down"][nl].astype(BF16))
        h, gu, a, ngu, nd, *gots = ffn_fwd(h, nw, wgu, wd, shards, also=mixers_packed if i == 0 else ())
        ffn_w[(ntag, nl)] = (ngu, nd)
        if i == 0:
            full.update(mixers_unpack([_fill_own_slot(got, x) for got, x in zip(gots, mixers_packed)]))
        return h, gu, a

    conv_full = _gather_weights({"gdn_conv_w": gdn_conv_w}, F32, split=False)["gdn_conv_w"]
    tab = _rope_table(positions[0])
    zero_d = jnp.zeros((1, d), F32)

    def gdn_params(j):
        w_in = full["gdn_w_in"][j]
        w_cat = jnp.concatenate([w_in, jnp.zeros((d, GDN_PROJ_W - w_in.shape[1]), BF16)], axis=1)
        conv = jnp.concatenate([conv_full[j], jnp.zeros((8 - GDN_CONV, GDN_QKV_W), F32)], axis=0)
        lanes = lambda vec: jnp.concatenate([jnp.zeros((GC_G,), F32), vec, jnp.zeros((LANE - GC_G - GDN_HEADS,), F32)])
        par = jnp.concatenate([lanes(gdn_A_log[j])[None], lanes(gdn_dt_bias[j])[None], jnp.zeros((6, LANE), F32)], axis=0)
        return w_cat, conv, par

    saved = []
    for l in range(depth):
        j = l // 2
        rec = {"h1": h}
        h, rec["ffn1_gu"], rec["ffn1_a"] = run_ffn("ffn1", l, h)
        rec["h2"] = h
        if l % 2 == 0:
            p = norm_proj(h, mix_norm[l][None], full["attn_w_in"][j], attn_b_in[j][None])
            o = attn_fwd(p, tab, attn_sinks[j][None])
            h = linear_residual(h, o, full["attn_w_out"][j], attn_b_out[j][None])
            rec.update(p=p, o=o)
        else:
            w_cat, conv, par = gdn_params(j)
            proj = norm_proj(h, mix_norm[l][None], w_cat, jnp.zeros((1, GDN_PROJ_W), F32))
            qkv, gc = gdn_pre_fwd(proj, conv, par)
            on, st, rec["m_offs"] = gdn_chunk_fwd(qkv, gc, proj, gdn_norm_w[j][None])
            h = linear_residual(h, on, full["gdn_w_out"][j], zero_d)
            rec.update(proj=proj, qkv=qkv, gc=gc, on=on, st=st, w_cat=w_cat, conv=conv, par=par)
        rec["h3"] = h
        h, rec["ffn2_gu"], rec["ffn2_a"] = run_ffn("ffn2", l, h)
        saved.append(rec)

    dh, loss_tile, d_final = loss_head(h, final_norm[None], target)

    g = {n: [None] * w[n].shape[0] for n in WEIGHTS if n != "final_norm"}
    reducers = []

    def layer_of(n, i):
        return i if n.startswith("ffn") else (2 * i if n.startswith("attn") else 2 * i + 1)

    def layer_reduce(layer):
        picks = {n: idx for n in big if (idx := [i for i in range(w[n].shape[0]) if layer_of(n, i) == layer])}
        return GradReduce({n: [g[n][i] for i in idx] for n, idx in picks.items()},
                          {n: (len(idx),) + w[n].shape[1:] for n, idx in picks.items()})

    def tn(x, y, scale=1.0):
        big_enough = x.shape[1] * y.shape[1] >= D_MODEL * D_FF
        red = next((r for r in reducers if r.next_job() is not None), None) if big_enough else None
        if red is None:
            return matmul_tn(x, y, scale)
        out, res = matmul_tn(x, y, scale, carry=red.next_job())
        red.deliver(res)
        return out
    for l in reversed(range(depth)):
        j = l // 2
        rec = saved[l]

        def ffn_back(tag, h_in, dh):
            wgu, wd = ffn_w[(tag, l)]
            dh_new, hn, dgu, dn = ffn_bwd(h_in, w[tag + "_norm"][l][None], wgu, wd, rec[tag + "_gu"], dh)
            g[tag + "_w_gate_up"][l] = tn(hn, dgu)
            g[tag + "_w_down"][l] = tn(rec[tag + "_a"], dh, 0.5)
            g[tag + "_norm"][l] = dn[0]
            return dh_new

        dh = ffn_back("ffn2", rec["h3"], dh)
        if l % 2 == 0:
            w_in, w_out = full["attn_w_in"][j], full["attn_w_out"][j]
            do, db_out = matmul_nt(dh, w_out)
            g["attn_w_out"][j] = matmul_tn(rec["o"], dh)
            g["attn_b_out"][j] = db_out[0]
            dq, dkvc, dkvp, dsink = attn_bwd(rec["p"], tab, attn_sinks[j][None], do)
            dkv = kv_combine(dkvc, dkvp)
            dh, hn, dn, cs_q, cs_kv = norm_proj_bwd(rec["h2"], mix_norm[l][None], dh, [dq, dkv],
                                                    [w_in[:, :ATTN_Q_W], w_in[:, ATTN_Q_W:]])
            g["attn_w_in"][j] = jnp.concatenate([matmul_tn(hn, dq), matmul_tn(hn, dkv)], axis=1)
            g["attn_b_in"][j] = jnp.concatenate([cs_q[0], cs_kv[0]])
            g["attn_sinks"][j] = dsink[0]
        else:
            w_cat, conv, par = rec["w_cat"], rec["conv"], rec["par"]
            d_on, _ = matmul_nt(dh, full["gdn_w_out"][j])
            g["gdn_w_out"][j] = matmul_tn(rec["on"], dh)
            dq, dk, dv, dz, dgc_heads, dnw = gdn_chunk_bwd(rec["qkv"], rec["gc"], rec["proj"], gdn_norm_w[j][None],
                                                           rec["st"], rec["m_offs"], d_on)
            dy, dba, dpar = gdn_pre_bwd(rec["proj"], conv, par, dq, dk, dv, dgc_heads)
            dx, dconv = gdn_conv_bwd(rec["proj"], conv, dy)
            nz = GDN_QKV_W + GDN_HEADS * GDN_DK
            dh, hn, dn, _, _, _ = norm_proj_bwd(rec["h2"], mix_norm[l][None], dh, [dx, dz, dba],
                                                [w_cat[:, :GDN_QKV_W], w_cat[:, GDN_QKV_W:nz], w_cat[:, nz:]])
            g["gdn_w_in"][j] = jnp.concatenate(
                [tn(hn, dx), matmul_tn(hn, dz), matmul_tn(hn, dba)[:, :2 * GDN_HEADS]], axis=1)
            g["gdn_conv_w"][j] = dconv[:GDN_CONV]
            g["gdn_A_log"][j] = dpar[0, GDN_HEADS:2 * GDN_HEADS]
            g["gdn_dt_bias"][j] = dpar[1, GDN_HEADS:2 * GDN_HEADS]
            g["gdn_norm_w"][j] = dnw[0]
        g["mix_norm"][l] = dn[0]
        dh = ffn_back("ffn1", rec["h1"], dh)
        if l > 0:
            reducers.append(layer_reduce(l))
    grad_x = dh[None]

    small_names = REPLICATED + ["gdn_conv_w"]
    local = {n: jnp.stack(g[n]) for n in small_names if n != "final_norm"}
    local["final_norm"] = d_final[0]
    small_shapes = [(1,)] + [local[n].shape for n in small_names]
    small = allreduce_small(_small_pack([loss_tile[0, 0:1]] + [local[n] for n in small_names]))
    small = _small_unpack(small, small_shapes)
    loss = small[0][0]
    grads = dict(zip(small_names, small[1:]))
    conv_cols = gdn_conv_w.shape[2]
    grads["gdn_conv_w"] = lax.dynamic_slice_in_dim(grads["gdn_conv_w"], _chip_index() * conv_cols, conv_cols, axis=2)
    by_layer = [layer_reduce(0).finish()] + [r.finish() for r in reversed(reducers)]
    grads.update({n: jnp.concatenate([part[n] for part in by_layer if n in part], axis=0) for n in big})

    delta, new_m, new_v = {}, {}, {}
    for n in SHARDED:
        dl, nm, nv = adamw(_as2d(w[n]), _as2d(grads[n]), _as2d(given["m_" + n]), _as2d(given["v_" + n]))
        delta[n], new_m[n], new_v[n] = dl.reshape(w[n].shape), nm.reshape(w[n].shape), nv.reshape(w[n].shape)
    shapes = [w[n].shape for n in REPLICATED]
    packed = [_small_pack([src[n] for n in REPLICATED]) for src in
              (w, grads, {n: given["m_" + n] for n in REPLICATED}, {n: given["v_" + n] for n in REPLICATED})]
    for dst, buf in zip((delta, new_m, new_v), adamw(*packed)):
        dst.update(zip(REPLICATED, _small_unpack(buf, shapes)))

    return (loss, grad_x, *[grads[n] for n in WEIGHTS], *[delta[n] for n in WEIGHTS],
            *[new_m[n] for n in WEIGHTS], *[new_v[n] for n in WEIGHTS])
```

```python
import functools

import jax
import jax.numpy as jnp
from jax import lax
from jax.experimental import pallas as pl
from jax.experimental.pallas import tpu as pltpu

F32 = jnp.float32
BF16 = jnp.bfloat16
HI = lax.Precision.HIGHEST
MESH = pl.DeviceIdType.MESH

D_MODEL = 1024
D_FF = 2816
NORM_EPS = 1e-6
LANE = 128

ATTN_Q_HEADS = 16
ATTN_KV_HEADS = 4
ATTN_HEAD_DIM = 64
ATTN_GROUP = 4
ATTN_BLOCK = 128
ROPE_DIM = 16
ROPE_THETA = 500000.0
ATTN_Q_W = 1024
ATTN_KV_W = 256

GDN_HEADS = 8
GDN_DK = 128
GDN_CONV = 4
GDN_CHUNK = 64
GDN_QKV_W = 3072

ADAM_LR = 0.001
ADAM_B1 = 0.9
ADAM_B2 = 0.999
ADAM_EPS = 1e-08
ADAM_WD = 0.01
ADAM_STEP = 10

NEG = -1e30


def _dot(a, b, prec=None):
    return lax.dot_general(a, b, (((1,), (0,)), ((), ())), precision=prec, preferred_element_type=F32)


def _dot_nt(a, b, prec=None):
    return lax.dot_general(a, b, (((1,), (1,)), ((), ())), precision=prec, preferred_element_type=F32)


def _dot_tn(a, b, prec=None):
    return lax.dot_general(a, b, (((0,), (0,)), ((), ())), precision=prec, preferred_element_type=F32)


def _bdot(a, b):
    return _dot(a.astype(BF16), b.astype(BF16))


def _bdot_nt(a, b):
    return _dot_nt(a.astype(BF16), b.astype(BF16))


def _bdot_tn(a, b):
    return _dot_tn(a.astype(BF16), b.astype(BF16))


def _sigmoid(x):
    return 1.0 / (1.0 + jnp.exp(-x))


def _silu(x):
    return x * _sigmoid(x)


def _silu_grad(x):
    s = _sigmoid(x)
    return s * (1.0 + x * (1.0 - s))


def _rms(x, w):
    r = lax.rsqrt(jnp.mean(x * x, axis=-1, keepdims=True) + NORM_EPS)
    xhat = x * r
    return xhat * w, xhat, r


def _rms_bwd(dy, w, xhat, r):
    dxhat = dy * w
    dx = r * (dxhat - xhat * jnp.mean(dxhat * xhat, axis=-1, keepdims=True))
    dw = jnp.sum(dy * xhat, axis=0, keepdims=True)
    return dx, dw


def _arb(n):
    return pltpu.CompilerParams(dimension_semantics=("arbitrary",) * n)


def _tile(n, want):
    t = min(n, want)
    assert n % t == 0, (n, want)
    return t


def _iota2(shape, dim):
    return lax.broadcasted_iota(jnp.int32, shape, dim)


_NN = (((1,), (0,)), ((), ()))
_NT = (((1,), (1,)), ((), ()))
_TN = (((0,), (0,)), ((), ()))


def _raw1(a, b, dn):
    return lax.dot_general(a.astype(BF16), b.astype(BF16), dn, preferred_element_type=F32)


def _raw3(a, b, dn):
    ah, bh = a.astype(BF16), b.astype(BF16)
    al, bl = (a - ah.astype(F32)).astype(BF16), (b - bh.astype(F32)).astype(BF16)
    f = lambda x, y: lax.dot_general(x, y, dn, preferred_element_type=F32)
    return f(ah, bh) + f(ah, bl) + f(al, bh)


def _make_mm(raw):
    @jax.custom_vjp
    def mm(a, b):
        return raw(a, b, _NN)

    mm.defvjp(lambda a, b: (raw(a, b, _NN), (a, b)),
              lambda res, ct: (raw(ct, res[1], _NT), raw(res[0], ct, _TN)))

    @jax.custom_vjp
    def mm_nt(a, b):
        return raw(a, b, _NT)

    mm_nt.defvjp(lambda a, b: (raw(a, b, _NT), (a, b)),
                 lambda res, ct: (raw(ct, res[1], _NN), raw(ct, res[0], _TN)))
    return mm, mm_nt


_mm1, _mm1_nt = _make_mm(_raw1)
_mm3, _mm3_nt = _make_mm(_raw3)


def _eye(n):
    return (_iota2((n, n), 0) == _iota2((n, n), 1)).astype(F32)


def _each(f, *lists):
    return [f(*t) for t in zip(*lists)]


def _inv_newton(mats):
    eye = _eye(mats[0].shape[0])
    ps = [-a for a in mats]
    ms = [eye + p for p in ps]
    k = 1
    while 2 * k < GDN_CHUNK:
        ps = [_raw1(p, p, _NN) for p in ps]
        ms = _each(lambda m, p: m + _raw1(m, p, _NN), ms, ps)
        k *= 2
    rs = _each(lambda a, m: eye - m - _raw3(a, m, _NN), mats, ms)
    return _each(lambda m, r: m + _raw1(m, r, _NN), ms, rs)


def _inv_cotangent(ms, dms):
    ts = _each(lambda m, dm: _raw3(m, dm, _TN), ms, dms)
    return _each(lambda t, m: -_raw3(t, m, _NT), ts, ms)


def _chunk_causal(n):
    r, s = _iota2((n, n), 0), _iota2((n, n), 1)
    same = (r // GDN_CHUNK) == (s // GDN_CHUNK)
    return same & (r >= s), same & (r > s)


def _gdn_a(k, dcb, bb128):
    causal, strict = _chunk_causal(k[0].shape[0])
    decay_l = [jnp.exp(jnp.where(causal, d - d.T, NEG)) for d in dcb]
    kb = _each(lambda a, b: a * b, k, bb128)
    return _each(lambda x, y, dl: jnp.where(strict, _mm1_nt(x, y) * dl, 0.0), kb, k, decay_l)


def _gdn_rest(q, k, v, dcb, dcb128, dlb128, bb128, m_off):
    causal, _ = _chunk_causal(q[0].shape[0])
    decay_l = [jnp.exp(jnp.where(causal, d - d.T, NEG)) for d in dcb]
    kb = _each(lambda a, b: a * b, k, bb128)
    edc = [jnp.exp(d) for d in dcb128]
    rhs = _each(lambda vv, bb, kk, e: jnp.concatenate([vv * bb, kk * e], axis=1), v, bb128, kb, edc)
    sol = _each(lambda x, m: x + _mm3(m, x), rhs, m_off)
    aqk = _each(lambda x, y, dl: jnp.where(causal, _mm1_nt(x, y) * dl, 0.0), q, k, decay_l)
    q_dec = _each(lambda x, e: x * e, q, edc)
    k_dec = _each(lambda x, dl, dc: x * jnp.exp(dl - dc), k, dlb128, dcb128)
    return sol, aqk, q_dec, k_dec


def _gdn_local(q, k, v, dcb, dcb128, dlb128, bb128):
    eye = _eye(q[0].shape[0])
    m_off = [m - eye for m in _inv_newton(_gdn_a(k, dcb, bb128))]
    return _gdn_rest(q, k, v, dcb, dcb128, dlb128, bb128, m_off) + (m_off,)


def _gated_norm(o, z, nw):
    r = lax.rsqrt(jnp.mean(o * o, axis=-1, keepdims=True) + NORM_EPS)
    return o * r * nw * _silu(z)


def _gated_norm_bwd(dy, o, z, nw):
    r = lax.rsqrt(jnp.mean(o * o, axis=-1, keepdims=True) + NORM_EPS)
    xhat = o * r
    sz = _silu(z)
    dxhat = dy * nw * sz
    do = r * (dxhat - xhat * jnp.mean(dxhat * xhat, axis=-1, keepdims=True))
    dz = dy * xhat * nw * _silu_grad(z)
    dnw = jnp.sum(dy * xhat * sz, axis=0, keepdims=True)
    return do, dz, dnw


GDN_HEAD_GROUP = 8
GC_BETA, GC_G, GC_DECAY, GC_LAST = 0, GDN_HEADS, 2 * GDN_HEADS, 3 * GDN_HEADS


def _gate_cols(gc, h, rows):
    lane = _iota2((rows, LANE), 1)
    col = lambda off: jnp.sum(jnp.where(lane == off + h, gc, 0.0), axis=-1, keepdims=True)
    return col(GC_BETA), col(GC_DECAY), col(GC_LAST)


def _gdn_local_args(q_ref, k_ref, v_ref, gc_ref, hp, n):
    gc = gc_ref[...]
    lanes = [slice(hh * LANE, (hh + 1) * LANE) for hh in range(hp)]
    cols = [_gate_cols(gc, pl.program_id(0) * hp + hh, n) for hh in range(hp)]
    bcast = lambda i, w: [jnp.broadcast_to(c[i], (n, w)) for c in cols]
    args = ([q_ref[:, ls] for ls in lanes], [k_ref[:, ls] for ls in lanes], [v_ref[:, ls] for ls in lanes],
            bcast(1, n), bcast(1, LANE), bcast(2, LANE), bcast(0, LANE))
    return args, [c[2] for c in cols], lanes


def gdn_chunk_fwd(qkv, gc, proj, norm_w):
    t = qkv.shape[0]
    c = GDN_CHUNK
    tc = _tile(t, 256)
    nsub = tc // c

    hp = GDN_HEAD_GROUP

    def body(q_ref, k_ref, v_ref, gc_ref, z_ref, nw_ref, on_ref, st_ref, m_ref, s_ref):
        @pl.when(pl.program_id(1) == 0)
        def _():
            s_ref[...] = jnp.zeros_like(s_ref)

        args, dl, lanes = _gdn_local_args(q_ref, k_ref, v_ref, gc_ref, hp, tc)
        sol, aqk, q_dec, k_dec, m_off = _gdn_local(*args)
        for hh in range(hp):
            m_ref[hh] = m_off[hh]
        u = [x[:, :GDN_DK] for x in sol]
        w = [x[:, GDN_DK:] for x in sol]
        s = [s_ref[hh] for hh in range(hp)]
        v_new, o_state = [[] for _ in range(hp)], [[] for _ in range(hp)]
        for j in range(nsub):
            sl = slice(j * c, (j + 1) * c)
            for hh in range(hp):
                st_ref[hh, j] = s[hh]
            vn = _each(lambda uu, ww, ss: uu[sl] - _bdot(ww[sl], ss), u, w, s)
            os_ = _each(lambda qq, ss: _bdot(qq[sl], ss), q_dec, s)
            s = _each(lambda ss, d, kk, vv: ss * jnp.exp(d[j * c:j * c + 1]) + _bdot_tn(kk[sl], vv), s, dl, k_dec, vn)
            for hh in range(hp):
                v_new[hh].append(vn[hh])
                o_state[hh].append(os_[hh])
        for hh in range(hp):
            s_ref[hh] = s[hh]
        cat = lambda xs: jnp.concatenate(xs, axis=0)
        o = _each(lambda os_, aa, vv: cat(os_) + _bdot(aa, cat(vv)), o_state, aqk, v_new)
        for hh, ls in enumerate(lanes):
            on_ref[:, ls] = _gated_norm(o[hh], z_ref[:, ls], nw_ref[...]).astype(BF16)

    col = lambda off: pl.BlockSpec((tc, hp * LANE), lambda g, i: (i, off // hp + g))
    return pl.pallas_call(
        body, name="gdn_chunk_fwd",
        grid=(GDN_HEADS // hp, t // tc),
        in_specs=[col(0), col(GDN_HEADS), col(2 * GDN_HEADS),
                  pl.BlockSpec((tc, LANE), lambda g, i: (i, 0)),
                  col(GDN_QKV_W // LANE),
                  pl.BlockSpec((1, LANE), lambda g, i: (0, 0))],
        out_specs=[col(0),
                   pl.BlockSpec((hp, nsub, GDN_DK, GDN_DK), lambda g, i: (g, i, 0, 0)),
                   pl.BlockSpec((hp, tc, tc), lambda g, i: (g, i, 0))],
        out_shape=[jax.ShapeDtypeStruct((t, GDN_HEADS * GDN_DK), BF16),
                   jax.ShapeDtypeStruct((GDN_HEADS, t // c, GDN_DK, GDN_DK), F32),
                   jax.ShapeDtypeStruct((GDN_HEADS, t, tc), F32)],
        scratch_shapes=[pltpu.VMEM((hp, GDN_DK, GDN_DK), F32)],
        compiler_params=_arb(2),
    )(qkv, qkv, qkv, gc, proj, norm_w)


def gdn_chunk_bwd(qkv, gc, proj, norm_w, states, m_offs, d_on):
    t = qkv.shape[0]
    c = GDN_CHUNK
    tc = _tile(t, 256)
    nsub = tc // c
    nblk = t // tc
    hp = GDN_HEAD_GROUP

    def body(q_ref, k_ref, v_ref, gc_ref, z_ref, nw_ref, st_ref, m_ref, don_ref,
             dq_ref, dk_ref, dv_ref, dz_ref, dgc_ref, dnw_ref, ds_ref):
        @pl.when(pl.program_id(1) == 0)
        def _():
            ds_ref[...] = jnp.zeros_like(ds_ref)

        @pl.when((pl.program_id(1) == 0) & (pl.program_id(0) == 0))
        def _():
            dnw_ref[...] = jnp.zeros_like(dnw_ref)

        rows = [slice(j * c, (j + 1) * c) for j in range(nsub)]
        cat = lambda xs: jnp.concatenate(xs, axis=0)
        lsum = lambda x: jnp.sum(x, axis=-1, keepdims=True)
        lane = _iota2((tc, LANE), 1)
        row = _iota2((tc, 1), 0)
        heads = range(hp)
        args, dl, lanes = _gdn_local_args(q_ref, k_ref, v_ref, gc_ref, hp, tc)
        m_off = [m_ref[hh] for hh in heads]
        (sol, aqk, q_dec, k_dec), vjp = jax.vjp(_gdn_rest, *args, m_off)
        u = [x[:, :GDN_DK] for x in sol]
        w = [x[:, GDN_DK:] for x in sol]
        states = [[st_ref[hh, j] for j in range(nsub)] for hh in heads]
        v_new = _each(lambda uu, ww, st: [uu[sl] - _bdot(ww[sl], s) for sl, s in zip(rows, st)], u, w, states)
        v_all = [cat(v) for v in v_new]
        o = _each(lambda qq, st, aa, vv: cat([_bdot(qq[sl], s) for sl, s in zip(rows, st)]) + _bdot(aa, vv),
                  q_dec, states, aqk, v_all)
        nw = nw_ref[...]
        gn = [_gated_norm_bwd(don_ref[:, ls], o[hh], z_ref[:, ls], nw) for hh, ls in enumerate(lanes)]
        do = [x[0] for x in gn]
        for hh, ls in enumerate(lanes):
            dnw_ref[...] += gn[hh][2]
            dz_ref[:, ls] = gn[hh][1].astype(BF16)
        d_aqk = _each(_bdot_nt, do, v_all)
        dv_o = _each(_bdot_tn, aqk, do)
        ds = [ds_ref[hh] for hh in heads]
        d_u, d_w, d_qdec, d_kdec, d_last = ([[None] * nsub for _ in heads] for _ in range(5))
        for j in reversed(range(nsub)):
            sl = rows[j]
            sj = [states[hh][j] for hh in heads]
            cd = [jnp.exp(d[j * c:j * c + 1]) for d in dl]
            du = _each(lambda dvo, kk, dd: dvo[sl] + _bdot(kk[sl], dd), dv_o, k_dec, ds)
            dqd = _each(lambda dd, s: _bdot_nt(dd[sl], s), do, sj)
            dkd = _each(lambda vv, dd: _bdot_nt(vv[j], dd), v_new, ds)
            dla = _each(lambda s, dd, cc: jnp.sum(lsum(s * dd), axis=0, keepdims=True) * cc, sj, ds, cd)
            dw = _each(lambda x, s: -_bdot_nt(x, s), du, sj)
            ds = _each(lambda qq, dd, cc, dsn, ww, x: _bdot_tn(qq[sl], dd[sl]) + cc * dsn - _bdot_tn(ww[sl], x),
                       q_dec, do, cd, ds, w, du)
            for hh in heads:
                d_u[hh][j], d_w[hh][j], d_qdec[hh][j], d_kdec[hh][j], d_last[hh][j] = du[hh], dw[hh], dqd[hh], dkd[hh], dla[hh]
        for hh in heads:
            ds_ref[hh] = ds[hh]
        d_sol = _each(lambda a, b: jnp.concatenate([cat(a), cat(b)], axis=1), d_u, d_w)
        dq, dk, dv, d_dcb, d_dcb128, d_dlb128, d_bb, d_m = vjp(
            (d_sol, d_aqk, [cat(x) for x in d_qdec], [cat(x) for x in d_kdec]))
        eye = _eye(tc)
        d_a = _inv_cotangent([m + eye for m in m_off], d_m)
        k_args = (args[1], args[3], args[6])
        dk_a, d_dcb_a, d_bb_a = jax.vjp(_gdn_a, *k_args)[1](d_a)
        add = lambda xs, ys: _each(lambda x, y: x + y, xs, ys)
        dk, d_dcb, d_bb = add(dk, dk_a), add(d_dcb, d_dcb_a), add(d_bb, d_bb_a)
        dgc = jnp.zeros((tc, LANE), F32)
        for hh, ls in enumerate(lanes):
            h = pl.program_id(0) * hp + hh
            dq_ref[:, ls] = dq[hh]
            dk_ref[:, ls] = dk[hh]
            dv_ref[:, ls] = dv[hh]
            d_dl = lsum(d_dlb128[hh])
            for j in range(nsub):
                d_dl = d_dl + jnp.where(row == j * c, d_last[hh][j], 0.0)
            dgc = dgc + jnp.where(lane == GC_BETA + h, lsum(d_bb[hh]),
                                  jnp.where(lane == GC_DECAY + h, lsum(d_dcb[hh]) + lsum(d_dcb128[hh]),
                                            jnp.where(lane == GC_LAST + h, d_dl, 0.0)))
        dgc_ref[0] = dgc

    rev = lambda i: nblk - 1 - i
    col = lambda off: pl.BlockSpec((tc, hp * LANE), lambda g, i: (rev(i), off // hp + g))
    return pl.pallas_call(
        body, name="gdn_chunk_bwd",
        grid=(GDN_HEADS // hp, nblk),
        in_specs=[col(0), col(GDN_HEADS), col(2 * GDN_HEADS),
                  pl.BlockSpec((tc, LANE), lambda g, i: (rev(i), 0)),
                  col(GDN_QKV_W // LANE),
                  pl.BlockSpec((1, LANE), lambda g, i: (0, 0)),
                  pl.BlockSpec((hp, nsub, GDN_DK, GDN_DK), lambda g, i: (g, rev(i), 0, 0)),
                  pl.BlockSpec((hp, tc, tc), lambda g, i: (g, rev(i), 0)),
                  col(0)],
        out_specs=[col(0), col(0), col(0), col(0),
                   pl.BlockSpec((1, tc, LANE), lambda g, i: (g, rev(i), 0)),
                   pl.BlockSpec((1, LANE), lambda g, i: (0, 0))],
        out_shape=[
            jax.ShapeDtypeStruct((t, GDN_HEADS * GDN_DK), F32),
            jax.ShapeDtypeStruct((t, GDN_HEADS * GDN_DK), F32),
            jax.ShapeDtypeStruct((t, GDN_HEADS * GDN_DK), F32),
            jax.ShapeDtypeStruct((t, GDN_HEADS * GDN_DK), BF16),
            jax.ShapeDtypeStruct((GDN_HEADS // hp, t, LANE), F32),
            jax.ShapeDtypeStruct((1, LANE), F32)],
        scratch_shapes=[pltpu.VMEM((hp, GDN_DK, GDN_DK), F32)],
        compiler_params=_arb(2),
    )(qkv, qkv, qkv, gc, proj, norm_w, states, m_offs, d_on)


GDN_PROJ_W = GDN_QKV_W + 1024 + LANE
GDN_Q_SCALE = GDN_DK ** -0.5


def _shift_rows(ext, shift, lo, n):
    if shift == 0:
        return ext[lo:lo + n]
    return pltpu.roll(ext, shift, 0)[lo:lo + n]


def _conv_fwd(x, halo, w):
    n = x.shape[0]
    ext = jnp.concatenate([halo, x], axis=0)
    y = w[GDN_CONV - 1:GDN_CONV] * x
    for j in range(GDN_CONV - 1):
        y = y + w[j:j + 1] * _shift_rows(ext, GDN_CONV - 1 - j, 8, n)
    return y


def _chunk_masks(n):
    r, s = _iota2((n, n), 0), _iota2((n, n), 1)
    same = (r // GDN_CHUNK) == (s // GDN_CHUNK)
    return (same & (r >= s)).astype(F32), (same & (r <= s)).astype(F32), same.astype(F32)


def _softplus(x):
    return jnp.maximum(x, 0.0) + jnp.log(1.0 + jnp.exp(-jnp.abs(x)))


def _l2n(t):
    rs = lax.rsqrt(jnp.sum(t * t, axis=-1, keepdims=True) + NORM_EPS)
    return t * rs, rs


def gdn_pre_fwd(proj, conv_w, gate_par):
    t = proj.shape[0]
    tm = _tile(t, 256)

    def body(x_ref, halo_ref, ba_ref, w_ref, gp_ref, qkv_ref, gc_ref):
        i = pl.program_id(0)
        halo = jnp.where(i > 0, halo_ref[...], 0.0)
        y = _silu(_conv_fwd(x_ref[...], halo, w_ref[...]))
        for hh in range(2 * GDN_HEADS):
            sl = slice(hh * LANE, (hh + 1) * LANE)
            tn, _ = _l2n(y[:, sl])
            qkv_ref[:, sl] = tn * GDN_Q_SCALE if hh < GDN_HEADS else tn
        qkv_ref[:, 2 * GDN_HEADS * LANE:] = y[:, 2 * GDN_HEADS * LANE:]
        ba = ba_ref[...]
        lane = _iota2(ba.shape, 1)
        gp = gp_ref[...]
        is_a = (lane >= GC_G) & (lane < GC_G + GDN_HEADS)
        g = jnp.where(is_a, -jnp.exp(gp[0:1]) * _softplus(ba + gp[1:2]), 0.0)
        tri, _, same = _chunk_masks(tm)
        decay = pltpu.roll(_dot(tri, g, HI), GC_DECAY - GC_G, 1)
        last = pltpu.roll(_dot(same, g, HI), GC_LAST - GC_G, 1)
        gc_ref[...] = jnp.where(lane < GDN_HEADS, _sigmoid(ba), g) + decay + last

    return pl.pallas_call(
        body, name="gdn_pre_fwd",
        grid=(t // tm,),
        in_specs=[pl.BlockSpec((tm, GDN_QKV_W), lambda i: (i, 0)),
                  pl.BlockSpec((8, GDN_QKV_W), lambda i: (jnp.maximum(i * (tm // 8) - 1, 0), 0)),
                  pl.BlockSpec((tm, LANE), lambda i: (i, (GDN_QKV_W + 1024) // LANE)),
                  pl.BlockSpec((8, GDN_QKV_W), lambda i: (0, 0)),
                  pl.BlockSpec((8, LANE), lambda i: (0, 0))],
        out_specs=[pl.BlockSpec((tm, GDN_QKV_W), lambda i: (i, 0)),
                   pl.BlockSpec((tm, LANE), lambda i: (i, 0))],
        out_shape=[jax.ShapeDtypeStruct((t, GDN_QKV_W), F32), jax.ShapeDtypeStruct((t, LANE), F32)],
        compiler_params=_arb(1),
    )(proj, proj, proj, conv_w, gate_par)


def gdn_pre_bwd(proj, conv_w, gate_par, dq, dk, dv, dgc_heads):
    t = proj.shape[0]
    tm = _tile(t, 256)

    def body(x_ref, halo_ref, ba_ref, w_ref, gp_ref, dq_ref, dk_ref, dv_ref, dgc_ref, dy_ref, dba_ref, dgp_ref):
        i = pl.program_id(0)

        @pl.when(i == 0)
        def _():
            dgp_ref[...] = jnp.zeros_like(dgp_ref)

        halo = jnp.where(i > 0, halo_ref[...], 0.0)
        y = _conv_fwd(x_ref[...], halo, w_ref[...])
        for hh in range(3 * GDN_HEADS):
            sl = slice(hh * LANE, (hh + 1) * LANE)
            hsl = slice((hh % GDN_HEADS) * LANE, (hh % GDN_HEADS + 1) * LANE)
            yy = y[:, sl]
            if hh < 2 * GDN_HEADS:
                tn, rs = _l2n(_silu(yy))
                dtn = dq_ref[:, hsl] * GDN_Q_SCALE if hh < GDN_HEADS else dk_ref[:, hsl]
                dsil = rs * (dtn - tn * jnp.sum(dtn * tn, axis=-1, keepdims=True))
            else:
                dsil = dv_ref[:, hsl]
            dy_ref[:, sl] = dsil * _silu_grad(yy)
        dgc = dgc_ref[0]
        for hh in range(1, dgc_heads.shape[0]):
            dgc = dgc + dgc_ref[hh]
        ba = ba_ref[...]
        lane = _iota2(ba.shape, 1)
        _, tri_t, same = _chunk_masks(tm)
        d_decay = jnp.where((lane >= GC_DECAY) & (lane < GC_DECAY + GDN_HEADS), dgc, 0.0)
        d_last = jnp.where((lane >= GC_LAST) & (lane < GC_LAST + GDN_HEADS), dgc, 0.0)
        dgc = (jnp.where(lane < GDN_HEADS, dgc, 0.0) + pltpu.roll(_dot(tri_t, d_decay, HI), LANE - (GC_DECAY - GC_G), 1)
               + pltpu.roll(_dot(same, d_last, HI), LANE - (GC_LAST - GC_G), 1))
        gp = gp_ref[...]
        xg = ba + gp[1:2]
        ea = jnp.exp(gp[0:1])
        sp = _softplus(xg)
        sb = _sigmoid(ba)
        is_b = lane < GDN_HEADS
        is_a = (lane >= GDN_HEADS) & (lane < 2 * GDN_HEADS)
        d_pre = jnp.where(is_a, dgc * (-ea) * _sigmoid(xg), 0.0)
        dba_ref[...] = jnp.where(is_b, dgc * sb * (1.0 - sb), d_pre).astype(BF16)
        d_alog = jnp.sum(jnp.where(is_a, dgc * (-ea) * sp, 0.0), axis=0, keepdims=True)
        d_dtb = jnp.sum(d_pre, axis=0, keepdims=True)
        row = _iota2((8, LANE), 0)
        dgp_ref[...] += jnp.where(row == 0, d_alog, jnp.where(row == 1, d_dtb, 0.0))

    hspec = pl.BlockSpec((tm, GDN_HEADS * LANE), lambda i: (i, 0))
    return pl.pallas_call(
        body, name="gdn_pre_bwd",
        grid=(t // tm,),
        in_specs=[pl.BlockSpec((tm, GDN_QKV_W), lambda i: (i, 0)),
                  pl.BlockSpec((8, GDN_QKV_W), lambda i: (jnp.maximum(i * (tm // 8) - 1, 0), 0)),
                  pl.BlockSpec((tm, LANE), lambda i: (i, (GDN_QKV_W + 1024) // LANE)),
                  pl.BlockSpec((8, GDN_QKV_W), lambda i: (0, 0)),
                  pl.BlockSpec((8, LANE), lambda i: (0, 0)),
                  hspec, hspec, hspec,
                  pl.BlockSpec((dgc_heads.shape[0], tm, LANE), lambda i: (0, i, 0))],
        out_specs=[pl.BlockSpec((tm, GDN_QKV_W), lambda i: (i, 0)),
                   pl.BlockSpec((tm, LANE), lambda i: (i, 0)),
                   pl.BlockSpec((8, LANE), lambda i: (0, 0))],
        out_shape=[jax.ShapeDtypeStruct((t, GDN_QKV_W), F32), jax.ShapeDtypeStruct((t, LANE), BF16),
                   jax.ShapeDtypeStruct((8, LANE), F32)],
        compiler_params=_arb(1),
    )(proj, proj, proj, conv_w, gate_par, dq, dk, dv, dgc_heads)


def gdn_conv_bwd(proj, conv_w, dy):
    t = proj.shape[0]
    tm = _tile(t, 256)
    nblk = t // tm

    def body(x_ref, halo_ref, w_ref, dy_ref, dyn_ref, dx_ref, dw_ref):
        i = pl.program_id(0)

        @pl.when(i == 0)
        def _():
            dw_ref[...] = jnp.zeros_like(dw_ref)

        w = w_ref[...]
        dy = dy_ref[...]
        ext_dy = jnp.concatenate([dy, jnp.where(i < nblk - 1, dyn_ref[...], 0.0)], axis=0)
        ext_x = jnp.concatenate([jnp.where(i > 0, halo_ref[...], 0.0), x_ref[...]], axis=0)
        dx = w[GDN_CONV - 1:GDN_CONV] * dy
        rows = [jnp.sum(dy * x_ref[...], axis=0, keepdims=True)]
        for j in range(GDN_CONV - 1):
            sh = GDN_CONV - 1 - j
            dx = dx + w[j:j + 1] * _shift_rows(ext_dy, tm + 8 - sh, 0, tm)
            rows.insert(j, jnp.sum(dy * _shift_rows(ext_x, sh, 8, tm), axis=0, keepdims=True))
        dx_ref[...] = dx.astype(BF16)
        row = _iota2((8, GDN_QKV_W), 0)
        acc = jnp.zeros((8, GDN_QKV_W), F32)
        for j in range(GDN_CONV):
            acc = acc + jnp.where(row == j, rows[j], 0.0)
        dw_ref[...] += acc

    return pl.pallas_call(
        body, name="gdn_conv_bwd",
        grid=(nblk,),
        in_specs=[pl.BlockSpec((tm, GDN_QKV_W), lambda i: (i, 0)),
                  pl.BlockSpec((8, GDN_QKV_W), lambda i: (jnp.maximum(i * (tm // 8) - 1, 0), 0)),
                  pl.BlockSpec((8, GDN_QKV_W), lambda i: (0, 0)),
                  pl.BlockSpec((tm, GDN_QKV_W), lambda i: (i, 0)),
                  pl.BlockSpec((8, GDN_QKV_W), lambda i: (jnp.minimum((i + 1) * (tm // 8), t // 8 - 1), 0))],
        out_specs=[pl.BlockSpec((tm, GDN_QKV_W), lambda i: (i, 0)),
                   pl.BlockSpec((8, GDN_QKV_W), lambda i: (0, 0))],
        out_shape=[jax.ShapeDtypeStruct((t, GDN_QKV_W), BF16), jax.ShapeDtypeStruct((8, GDN_QKV_W), F32)],
        compiler_params=_arb(1),
    )(proj, proj, conv_w, dy, dy)


def _resident(w_hbm, w_vmem, sem):
    @pl.when(pl.program_id(0) == 0)
    def _():
        cp = pltpu.make_async_copy(w_hbm, w_vmem, sem)
        cp.start()
        cp.wait()


ANY = pl.BlockSpec(memory_space=pl.ANY)


def norm_proj(h, nw, w, bias):
    t, d = h.shape
    n = w.shape[1]
    tm = _tile(t, 512)
    nc = _tile(n, 1536) if n % 1536 == 0 else _tile(n, 1408)

    def body(h_ref, nw_ref, w_hbm, b_ref, o_ref, w_ref, sem):
        _resident(w_hbm, w_ref, sem)
        hn = _rms(h_ref[...], nw_ref[...])[0].astype(BF16)
        for c0 in range(0, n, nc):
            o_ref[:, c0:c0 + nc] = _dot(hn, w_ref[:, c0:c0 + nc]) + b_ref[:, c0:c0 + nc]

    return pl.pallas_call(
        body, name="norm_proj",
        grid=(t // tm,),
        in_specs=[pl.BlockSpec((tm, d), lambda i: (i, 0)), pl.BlockSpec((1, d), lambda i: (0, 0)), ANY,
                  pl.BlockSpec((1, n), lambda i: (0, 0))],
        out_specs=pl.BlockSpec((tm, n), lambda i: (i, 0)),
        out_shape=jax.ShapeDtypeStruct((t, n), F32),
        scratch_shapes=[pltpu.VMEM((d, n), BF16), pltpu.SemaphoreType.DMA],
        compiler_params=_arb(1),
    )(h, nw, w, bias)


def linear_residual(h, x, w, bias):
    t, d = h.shape
    k = x.shape[1]
    tm = _tile(t, 512)

    def body(h_ref, x_ref, w_hbm, b_ref, o_ref, w_ref, sem):
        _resident(w_hbm, w_ref, sem)
        o_ref[...] = h_ref[...] + _dot(x_ref[...], w_ref[...]) + b_ref[...]

    return pl.pallas_call(
        body, name="linear_residual",
        grid=(t // tm,),
        in_specs=[pl.BlockSpec((tm, d), lambda i: (i, 0)), pl.BlockSpec((tm, k), lambda i: (i, 0)), ANY,
                  pl.BlockSpec((1, d), lambda i: (0, 0))],
        out_specs=pl.BlockSpec((tm, d), lambda i: (i, 0)),
        out_shape=jax.ShapeDtypeStruct((t, d), F32),
        scratch_shapes=[pltpu.VMEM((k, d), BF16), pltpu.SemaphoreType.DMA],
        compiler_params=_arb(1),
    )(h, x, w, bias)


def matmul_nt(dy, w):
    t, d = dy.shape
    k = w.shape[0]
    tm = _tile(t, 512)

    def body(dy_ref, w_hbm, o_ref, cs_ref, w_ref, sem):
        _resident(w_hbm, w_ref, sem)

        @pl.when(pl.program_id(0) == 0)
        def _():
            cs_ref[...] = jnp.zeros_like(cs_ref)

        dy = dy_ref[...]
        cs_ref[...] += jnp.sum(dy, axis=0, keepdims=True)
        o_ref[...] = _dot_nt(dy.astype(BF16), w_ref[...])

    return pl.pallas_call(
        body, name="matmul_nt",
        grid=(t // tm,),
        in_specs=[pl.BlockSpec((tm, d), lambda i: (i, 0)), ANY],
        out_specs=[pl.BlockSpec((tm, k), lambda i: (i, 0)), pl.BlockSpec((1, d), lambda i: (0, 0))],
        out_shape=[jax.ShapeDtypeStruct((t, k), F32), jax.ShapeDtypeStruct((1, d), F32)],
        scratch_shapes=[pltpu.VMEM((k, d), BF16), pltpu.SemaphoreType.DMA],
        compiler_params=_arb(1),
    )(dy, w)


def norm_proj_bwd(h, nw, dh, dps, ws):
    t, d = h.shape
    np_ = len(dps)
    ns = [w.shape[1] for w in ws]
    tm = _tile(t, 512)

    def body(*refs):
        h_ref, nw_ref, dh_ref = refs[:3]
        dp_refs = refs[3:3 + np_]
        w_hbms = refs[3 + np_:3 + 2 * np_]
        o_ref, hn_ref, dnw_ref = refs[3 + 2 * np_:6 + 2 * np_]
        cs_refs = refs[6 + 2 * np_:6 + 3 * np_]
        w_refs = refs[6 + 3 * np_:6 + 4 * np_]
        sem = refs[6 + 4 * np_]
        for a, b in zip(w_hbms, w_refs):
            _resident(a, b, sem)

        @pl.when(pl.program_id(0) == 0)
        def _():
            dnw_ref[...] = jnp.zeros_like(dnw_ref)
            for c in cs_refs:
                c[...] = jnp.zeros_like(c)

        nw = nw_ref[...]
        hn, xhat, r = _rms(h_ref[...], nw)
        hn_ref[...] = hn.astype(BF16)
        dhn = jnp.zeros((tm, d), F32)
        for dp_ref, w_ref, cs_ref in zip(dp_refs, w_refs, cs_refs):
            dp = dp_ref[...]
            cs_ref[...] += jnp.sum(dp.astype(F32), axis=0, keepdims=True)
            dhn = dhn + _dot_nt(dp, w_ref[...])
        dx, dnw = _rms_bwd(dhn, nw, xhat, r)
        dnw_ref[...] += dnw
        o_ref[...] = dh_ref[...] + dx

    row = pl.BlockSpec((tm, d), lambda i: (i, 0))
    vec = pl.BlockSpec((1, d), lambda i: (0, 0))
    return pl.pallas_call(
        body, name="norm_proj_bwd",
        grid=(t // tm,),
        in_specs=[row, vec, row] + [pl.BlockSpec((tm, n), lambda i: (i, 0)) for n in ns] + [ANY] * np_,
        out_specs=[row, row, vec] + [pl.BlockSpec((1, n), lambda i: (0, 0)) for n in ns],
        out_shape=[jax.ShapeDtypeStruct((t, d), F32), jax.ShapeDtypeStruct((t, d), BF16),
                   jax.ShapeDtypeStruct((1, d), F32)] + [jax.ShapeDtypeStruct((1, n), F32) for n in ns],
        scratch_shapes=[pltpu.VMEM((d, n), BF16) for n in ns] + [pltpu.SemaphoreType.DMA],
        compiler_params=_arb(1),
    )(h, nw, dh, *dps, *ws)


def _pair_copies(g_refs, got_refs, send_sems, recv_sems):
    mx, my, mc = _me()
    copies = []
    for g_ref, got_ref in zip(g_refs, got_refs):
        for j in range(4):
            i = len(copies)
            copies.append(pltpu.make_async_remote_copy(
                src_ref=g_ref.at[j, 1 - mc], dst_ref=got_ref.at[j], send_sem=send_sems.at[i], recv_sem=recv_sems.at[i],
                device_id=(mx, my, 1 - mc), device_id_type=MESH))
    return copies


def _scatter_copies(p_refs, q_refs, send_sems, recv_sems):
    mx, my, mc = _me()
    copies = []
    for p_ref, q_ref in zip(p_refs, q_refs):
        for k in (1, 2, 3):
            px, py = _flip(mx, k >> 1), _flip(my, k & 1)
            i = len(copies)
            copies.append(pltpu.make_async_remote_copy(
                src_ref=p_ref.at[2 * px + py], dst_ref=q_ref.at[2 * mx + my], send_sem=send_sems.at[i],
                recv_sem=recv_sems.at[i], device_id=(px, py, mc), device_id_type=MESH))
    return copies


def _share_copies(o_refs, out_refs, send_sems, recv_sems):
    mx, my, mc = _me()
    return [pltpu.make_async_remote_copy(src_ref=o_ref, dst_ref=out_ref, send_sem=send_sems.at[i], recv_sem=recv_sems.at[i],
                                         device_id=(mx, my, 1 - mc), device_id_type=MESH)
            for i, (o_ref, out_ref) in enumerate(zip(o_refs, out_refs))]


CARRIED = {"pair": (_pair_copies, 4, lambda a: a.shape[:1] + a.shape[2:]), "scatter": (_scatter_copies, 3, lambda a: a.shape),
           "share": (_share_copies, 1, lambda a: a.shape)}


def matmul_tn(x, y, scale=1.0, carry=None):
    t, k = x.shape
    n = y.shape[1]
    tk = _tile(k, 1024) if k % 1024 == 0 else _tile(k, 1408)
    tn = n if n <= 1536 else (1024 if n % 1024 == 0 else 1408)
    assert n % tn == 0
    tt = _tile(t, 2048)
    grid = (k // tk, n // tn, t // tt)
    arrays = [] if carry is None else list(carry[1])
    na = len(arrays)

    def body(x_ref, y_ref, *rest):
        o_ref = rest[na]
        if carry is not None:
            make, _, _ = CARRIED[carry[0]]
            copies = lambda: make(rest[:na], rest[na + 1:2 * na + 1], rest[2 * na + 1], rest[2 * na + 2])
            pid = [pl.program_id(a) for a in range(3)]

            @pl.when((pid[0] == 0) & (pid[1] == 0) & (pid[2] == 0))
            def _():
                for cp in copies():
                    cp.start()

        @pl.when(pl.program_id(2) == 0)
        def _():
            o_ref[...] = jnp.zeros_like(o_ref)

        yv = y_ref[...]
        if scale != 1.0:
            yv = yv * scale
        o_ref[...] += _dot_tn(x_ref[...].astype(BF16), yv.astype(BF16))

        if carry is not None:
            @pl.when((pid[0] == grid[0] - 1) & (pid[1] == grid[1] - 1) & (pid[2] == grid[2] - 1))
            def _():
                for cp in copies():
                    cp.wait()

    out_shape = [jax.ShapeDtypeStruct((k, n), F32)]
    scratch = []
    if carry is not None:
        _, per_array, out_of = CARRIED[carry[0]]
        out_shape += [jax.ShapeDtypeStruct(out_of(a), a.dtype) for a in arrays]
        scratch = [pltpu.SemaphoreType.DMA((per_array * na,))] * 2
    res = pl.pallas_call(
        body, name="matmul_tn" if carry is None else "matmul_tn_" + carry[0],
        grid=grid,
        in_specs=[pl.BlockSpec((tt, tk), lambda i, j, s: (s, i)), pl.BlockSpec((tt, tn), lambda i, j, s: (s, j))] + [ANY] * na,
        out_specs=[pl.BlockSpec((tk, tn), lambda i, j, s: (i, j))] + [ANY] * na,
        out_shape=out_shape, scratch_shapes=scratch,
        compiler_params=_arb(3),
    )(x, y, *arrays)
    return res[0] if carry is None else (res[0], list(res[1:]))


FFN_CHUNKS = 2


class _FfnGather:
    def __init__(self, gu_sh, d_sh, gu_full, d_full, send_sems, recv_sems, local_sems):
        self.sh, self.full = (gu_sh, d_sh), (gu_full, d_full)
        self.send_sems, self.recv_sems, self.local_sems = send_sems, recv_sems, local_sems
        self.mx, self.my, self.mc = _me()
        self.me = 2 * self.mx + self.my
        self.w4, self.f4 = gu_sh.shape[1], d_sh.shape[0]
        self.hg, self.hd = gu_sh.shape[0] // 2, d_sh.shape[0] // 2

    def _window(self, a, chip, core=None):
        gu_full, d_full = self.full
        if a == 0:
            rows = pl.ds(0, 2 * self.hg) if core is None else pl.ds(pl.multiple_of(core * self.hg, 16), self.hg)
            return gu_full.at[rows, pl.ds(pl.multiple_of(chip * self.w4, LANE), self.w4)]
        if core is None:
            return d_full.at[pl.ds(pl.multiple_of(chip * self.f4, 16), self.f4), :]
        return d_full.at[pl.ds(pl.multiple_of(chip * self.f4 + core * self.hd, 16), self.hd), :]

    def _half(self, a):
        n = (self.hg, self.hd)[a]
        return self.sh[a].at[pl.ds(pl.multiple_of(self.mc * n, 16), n), :]

    def _peer(self, k):
        return _flip(self.mx, k >> 1), _flip(self.my, k & 1)

    def _fetch(self, k, a):
        px, py = self._peer(k)
        i = 2 * (k - 1) + a
        return pltpu.make_async_remote_copy(src_ref=self._half(a), dst_ref=self._window(a, self.me, self.mc),
                                            send_sem=self.send_sems.at[i], recv_sem=self.recv_sems.at[i],
                                            device_id=(px, py, self.mc), device_id_type=MESH)

    def _relay(self, k, a):
        px, py = self._peer(k)
        got = self._window(a, 2 * px + py, self.mc)
        i = 6 + 2 * (k - 1) + a
        return pltpu.make_async_remote_copy(src_ref=got, dst_ref=got, send_sem=self.send_sems.at[i],
                                            recv_sem=self.recv_sems.at[i], device_id=(self.mx, self.my, 1 - self.mc),
                                            device_id_type=MESH)

    def _own(self, a):
        return pltpu.make_async_copy(self.sh[a], self._window(a, self.me), self.local_sems.at[a])

    def start(self):
        for a in (0, 1):
            self._own(a).start()
            for k in (1, 2, 3):
                self._fetch(k, a).start()

    def relay(self):
        for k in (1, 2, 3):
            for a in (0, 1):
                self._fetch(k, a).wait_recv()
                self._relay(k, a).start()

    def finish(self):
        for k in (1, 2, 3):
            for a in (0, 1):
                self._fetch(k, a).wait_send()
                self._relay(k, a).wait()
        for a in (0, 1):
            self._own(a).wait()


def ffn_fwd(h, nw, wgu, wd, nxt=None, also=()):
    t, d = h.shape
    f = wd.shape[0]
    fc = f // FFN_CHUNKS
    tm = _tile(t, 512)
    nsteps = t // tm
    also = list(also)
    na = len(also)

    def body(h_ref, nw_ref, wgu_hbm, wd_hbm, *rest):
        gathers = []
        if nxt is None:
            o_ref, gu_ref, a_ref, wgu_ref, wd_ref, sem = rest
        else:
            gu_sh, d_sh = rest[:2]
            o_ref, gu_ref, a_ref, gu_full, d_full = rest[2 + na:7 + na]
            wgu_ref, wd_ref, sem, send_sems, recv_sems, local_sems = rest[7 + 2 * na:13 + 2 * na]
            gathers.append(_FfnGather(gu_sh, d_sh, gu_full, d_full, send_sems, recv_sems, local_sems))
            if na:
                gathers.append(_SplitGather(rest[2:2 + na], rest[7 + na:7 + 2 * na], *rest[13 + 2 * na:]))

        @pl.when(pl.program_id(0) == 0)
        def _():
            for gather in gathers:
                gather.start()

        _resident(wgu_hbm, wgu_ref, sem)
        _resident(wd_hbm, wd_ref, sem)

        @pl.when(pl.program_id(0) == (3 * nsteps) // 4)
        def _():
            for gather in gathers:
                gather.relay()

        x = h_ref[...]
        hn = _rms(x, nw_ref[...])[0].astype(BF16)
        acc = jnp.zeros((tm, d), F32)
        for c in range(FFN_CHUNKS):
            gs, us = slice(c * fc, (c + 1) * fc), slice(f + c * fc, f + (c + 1) * fc)
            g = _dot(hn, wgu_ref[:, gs])
            u = _dot(hn, wgu_ref[:, us])
            a = (_silu(g) * u).astype(BF16)
            gu_ref[:, gs] = g.astype(BF16)
            gu_ref[:, us] = u.astype(BF16)
            a_ref[:, gs] = a
            acc = acc + _dot(a, wd_ref[gs, :])
        o_ref[...] = x + 0.5 * acc

        @pl.when(pl.program_id(0) == nsteps - 1)
        def _():
            for gather in gathers:
                gather.finish()

    row = lambda w: pl.BlockSpec((tm, w), lambda i: (i, 0))
    in_specs = [row(d), pl.BlockSpec((1, d), lambda i: (0, 0)), ANY, ANY]
    out_specs = [row(d), row(2 * f), row(f)]
    out_shape = [jax.ShapeDtypeStruct((t, d), F32), jax.ShapeDtypeStruct((t, 2 * f), BF16),
                 jax.ShapeDtypeStruct((t, f), BF16)]
    scratch = [pltpu.VMEM((d, 2 * f), BF16), pltpu.VMEM((f, d), BF16), pltpu.SemaphoreType.DMA]
    args = (h, nw, wgu, wd)
    if nxt is not None:
        in_specs += [ANY] * (2 + na)
        out_specs += [ANY] * (2 + na)
        out_shape += [jax.ShapeDtypeStruct((d, 2 * f), BF16), jax.ShapeDtypeStruct((f, d), BF16)]
        out_shape += [jax.ShapeDtypeStruct((4,) + a.shape, a.dtype) for a in also]
        scratch += [pltpu.SemaphoreType.DMA((12,)), pltpu.SemaphoreType.DMA((12,)), pltpu.SemaphoreType.DMA((2,))]
        if na:
            scratch += [pltpu.SemaphoreType.DMA((6 * na,)), pltpu.SemaphoreType.DMA((6 * na,))]
        args += tuple(nxt) + tuple(also)
    return pl.pallas_call(
        body, name="ffn_fwd_gather" if nxt is not None else "ffn_fwd",
        grid=(nsteps,),
        in_specs=in_specs, out_specs=out_specs, out_shape=out_shape, scratch_shapes=scratch,
        compiler_params=_arb(1),
    )(*args)


def ffn_bwd(h, nw, wgu, wd, gu, dh):
    t, d = h.shape
    f = wd.shape[0]
    fc = f // FFN_CHUNKS
    tm = _tile(t, 256)

    def body(h_ref, nw_ref, wgu_hbm, wd_hbm, gu_ref, dh_ref, o_ref, hn_ref, dgu_ref, dnw_ref, wgu_ref, wd_ref, sem):
        _resident(wgu_hbm, wgu_ref, sem)
        _resident(wd_hbm, wd_ref, sem)

        @pl.when(pl.program_id(0) == 0)
        def _():
            dnw_ref[...] = jnp.zeros_like(dnw_ref)

        nw = nw_ref[...]
        hn32, xhat, r = _rms(h_ref[...], nw)
        hn_ref[...] = hn32.astype(BF16)
        dh = dh_ref[...]
        dout = (0.5 * dh).astype(BF16)
        dhn = jnp.zeros((tm, d), F32)
        for c in range(FFN_CHUNKS):
            gs, us = slice(c * fc, (c + 1) * fc), slice(f + c * fc, f + (c + 1) * fc)
            g = gu_ref[:, gs].astype(F32)
            u = gu_ref[:, us].astype(F32)
            sg = _sigmoid(g)
            sil = g * sg
            da = _dot_nt(dout, wd_ref[gs, :])
            dg = (da * u * (sg * (1.0 + g * (1.0 - sg)))).astype(BF16)
            du = (da * sil).astype(BF16)
            dgu_ref[:, gs] = dg
            dgu_ref[:, us] = du
            dhn = dhn + _dot_nt(dg, wgu_ref[:, gs]) + _dot_nt(du, wgu_ref[:, us])
        dx, dnw = _rms_bwd(dhn, nw, xhat, r)
        dnw_ref[...] += dnw
        o_ref[...] = dh + dx

    row = pl.BlockSpec((tm, d), lambda i: (i, 0))
    vec = pl.BlockSpec((1, d), lambda i: (0, 0))
    return pl.pallas_call(
        body, name="ffn_bwd",
        grid=(t // tm,),
        in_specs=[row, vec, ANY, ANY, pl.BlockSpec((tm, 2 * f), lambda i: (i, 0)), row],
        out_specs=[row, row, pl.BlockSpec((tm, 2 * f), lambda i: (i, 0)), vec],
        out_shape=[jax.ShapeDtypeStruct((t, d), F32), jax.ShapeDtypeStruct((t, d), BF16),
                   jax.ShapeDtypeStruct((t, 2 * f), BF16), jax.ShapeDtypeStruct((1, d), F32)],
        scratch_shapes=[pltpu.VMEM((d, 2 * f), BF16), pltpu.VMEM((f, d), BF16), pltpu.SemaphoreType.DMA],
        compiler_params=_arb(1),
    )(h, nw, wgu, wd, gu, dh)


def loss_head(h, nw, target):
    t, d = h.shape
    tm = _tile(t, 512)

    def body(h_ref, nw_ref, tg_ref, dh_ref, loss_ref, dnw_ref):
        @pl.when(pl.program_id(0) == 0)
        def _():
            loss_ref[...] = jnp.zeros_like(loss_ref)
            dnw_ref[...] = jnp.zeros_like(dnw_ref)

        nw = nw_ref[...]
        y, xhat, r = _rms(h_ref[...], nw)
        e = y - tg_ref[...]
        loss_ref[...] += 0.5 * jnp.sum(jnp.mean(e * e, axis=-1, keepdims=True), axis=0, keepdims=True)
        dx, dnw = _rms_bwd(e * (1.0 / d), nw, xhat, r)
        dnw_ref[...] += dnw
        dh_ref[...] = dx

    row = pl.BlockSpec((tm, d), lambda i: (i, 0))
    vec = pl.BlockSpec((1, d), lambda i: (0, 0))
    return pl.pallas_call(
        body, name="loss_head",
        grid=(t // tm,),
        in_specs=[row, vec, row],
        out_specs=[row, pl.BlockSpec((8, LANE), lambda i: (0, 0)), vec],
        out_shape=[jax.ShapeDtypeStruct((t, d), F32), jax.ShapeDtypeStruct((8, LANE), F32),
                   jax.ShapeDtypeStruct((1, d), F32)],
        compiler_params=_arb(1),
    )(h, nw, target)


def adamw(w, g, m, v):
    r, c = w.shape
    tr = r
    while tr * c * 4 > (1 << 20) and tr % 16 == 0:
        tr //= 2

    def body(w_ref, g_ref, m_ref, v_ref, d_ref, nm_ref, nv_ref):
        g = g_ref[...]
        m = ADAM_B1 * m_ref[...] + (1.0 - ADAM_B1) * g
        v = ADAM_B2 * v_ref[...] + (1.0 - ADAM_B2) * (g * g)
        m_hat = m / (1.0 - ADAM_B1 ** ADAM_STEP)
        v_hat = v / (1.0 - ADAM_B2 ** ADAM_STEP)
        d_ref[...] = -ADAM_LR * (m_hat / (jnp.sqrt(v_hat) + ADAM_EPS) + ADAM_WD * w_ref[...])
        nm_ref[...] = m
        nv_ref[...] = v

    blk = pl.BlockSpec((tr, c), lambda i: (i, 0))
    return pl.pallas_call(
        body, name="adamw",
        grid=(r // tr,),
        in_specs=[blk] * 4, out_specs=[blk] * 3,
        out_shape=[jax.ShapeDtypeStruct((r, c), F32)] * 3,
        compiler_params=_arb(1),
    )(w, g, m, v)


ATTN_P_W = ATTN_Q_W + 2 * ATTN_KV_W
ATTN_SCALE = ATTN_HEAD_DIM ** -0.5


def _rope_group(t, tab, sign):
    return (t * tab[:, 0:LANE] + sign * pltpu.roll(t, 8, 1) * tab[:, LANE:2 * LANE]
            + sign * pltpu.roll(t, LANE - 8, 1) * tab[:, 2 * LANE:3 * LANE])


def _rope(t, tab, sign=1.0):
    return jnp.concatenate([_rope_group(t[:, s:s + LANE], tab, sign) for s in range(0, t.shape[1], LANE)], axis=1)


def _attn_heads(qs, ks, vs, sinks, first):
    b = ATTN_BLOCK
    rows = ATTN_GROUP * b
    qi = _iota2((rows, 2 * b), 0) % b
    kj = _iota2((rows, 2 * b), 1)
    rel = qi + b - kj
    valid = (rel >= 0) & (rel < b) & ((kj >= b) | jnp.logical_not(first))
    ss = _each(lambda q, k: jnp.where(valid, _mm1_nt(q, k) * ATTN_SCALE, NEG), qs, ks)
    ms = _each(lambda s, sink: lax.stop_gradient(jnp.maximum(jnp.max(s, axis=-1, keepdims=True), sink)), ss, sinks)
    ps = _each(lambda s, m: jnp.exp(s - m), ss, ms)
    dens = _each(lambda p, sink, m: jnp.sum(p, axis=-1, keepdims=True) + jnp.exp(sink - m), ps, sinks, ms)
    return _each(lambda p, den, v: _mm1(p / den, v), ps, dens, vs)


def _attn_prepare(p_ref, kvp_ref, tab_ref, tabp_ref, sink_ref):
    b = ATTN_BLOCK
    hd = ATTN_HEAD_DIM
    tab, tabp = tab_ref[...], tabp_ref[...]
    q = _rope(p_ref[:, 0:ATTN_Q_W], tab)
    kc = _rope(p_ref[:, ATTN_Q_W:ATTN_Q_W + ATTN_KV_W], tab)
    kp = _rope(kvp_ref[:, 0:ATTN_KV_W], tabp)
    vc = p_ref[:, ATTN_Q_W + ATTN_KV_W:ATTN_P_W]
    vp = kvp_ref[:, ATTN_KV_W:2 * ATTN_KV_W]
    sk = sink_ref[...]
    qs, ks, vs, sinks = [], [], [], []
    for h in range(ATTN_KV_HEADS):
        heads = [ATTN_GROUP * h + g for g in range(ATTN_GROUP)]
        qs.append(jnp.concatenate([q[:, i * hd:(i + 1) * hd] for i in heads], axis=0))
        ks.append(jnp.concatenate([kp[:, h * hd:(h + 1) * hd], kc[:, h * hd:(h + 1) * hd]], axis=0))
        vs.append(jnp.concatenate([vp[:, h * hd:(h + 1) * hd], vc[:, h * hd:(h + 1) * hd]], axis=0))
        sinks.append(jnp.concatenate([jnp.broadcast_to(sk[:, i:i + 1], (b, 1)) for i in heads], axis=0))
    return qs, ks, vs, sinks


def _unstack_heads(xs):
    b = ATTN_BLOCK
    return jnp.concatenate([x[g * b:(g + 1) * b] for x in xs for g in range(ATTN_GROUP)], axis=1)


def _attn_specs(nb):
    b = ATTN_BLOCK
    prev = lambda n: jnp.maximum(n - 1, 0)
    return [pl.BlockSpec((b, ATTN_P_W), lambda n: (n, 0)),
            pl.BlockSpec((b, 2 * ATTN_KV_W), lambda n: (prev(n), ATTN_Q_W // (2 * ATTN_KV_W))),
            pl.BlockSpec((b, 3 * LANE), lambda n: (n, 0)),
            pl.BlockSpec((b, 3 * LANE), lambda n: (prev(n), 0)),
            pl.BlockSpec((1, ATTN_Q_HEADS), lambda n: (0, 0))]


def attn_fwd(p, tab, sinks):
    t = p.shape[0]
    nb = t // ATTN_BLOCK

    def body(p_ref, kvp_ref, tab_ref, tabp_ref, sink_ref, o_ref):
        qs, ks, vs, sk = _attn_prepare(p_ref, kvp_ref, tab_ref, tabp_ref, sink_ref)
        os_ = _attn_heads(qs, ks, vs, sk, pl.program_id(0) == 0)
        o_ref[...] = _unstack_heads(os_).astype(BF16)

    return pl.pallas_call(
        body, name="attn_fwd",
        grid=(nb,),
        in_specs=_attn_specs(nb),
        out_specs=pl.BlockSpec((ATTN_BLOCK, ATTN_Q_W), lambda n: (n, 0)),
        out_shape=jax.ShapeDtypeStruct((t, ATTN_Q_W), BF16),
        compiler_params=_arb(1),
    )(p, p, tab, tab, sinks)


def attn_bwd(p, tab, sinks, do):
    t = p.shape[0]
    b = ATTN_BLOCK
    hd = ATTN_HEAD_DIM
    nb = t // b

    def body(p_ref, kvp_ref, tab_ref, tabp_ref, sink_ref, do_ref, dq_ref, dkvc_ref, dkvp_ref, dsink_ref):
        n = pl.program_id(0)

        @pl.when(n == 0)
        def _():
            dsink_ref[...] = jnp.zeros_like(dsink_ref)

        qs, ks, vs, sk = _attn_prepare(p_ref, kvp_ref, tab_ref, tabp_ref, sink_ref)
        first = n == 0
        _, vjp = jax.vjp(lambda a, bb, c, d: _attn_heads(a, bb, c, d, first), qs, ks, vs, sk)
        do = do_ref[...]
        dos = [jnp.concatenate([do[:, i * hd:(i + 1) * hd] for i in range(ATTN_GROUP * h, ATTN_GROUP * (h + 1))], axis=0)
               for h in range(ATTN_KV_HEADS)]
        dqs, dks, dvs, dsk = vjp(dos)
        tab, tabp = tab_ref[...], tabp_ref[...]
        dq_ref[...] = _rope(_unstack_heads([x.astype(F32) for x in dqs]), tab, -1.0).astype(BF16)
        dkc = jnp.concatenate([x.astype(F32)[b:] for x in dks], axis=1)
        dkp = jnp.concatenate([x.astype(F32)[:b] for x in dks], axis=1)
        dkvc_ref[:, 0:ATTN_KV_W] = _rope(dkc, tab, -1.0)
        dkvp_ref[:, 0:ATTN_KV_W] = _rope(dkp, tabp, -1.0)
        dkvc_ref[:, ATTN_KV_W:] = jnp.concatenate([x.astype(F32)[b:] for x in dvs], axis=1)
        dkvp_ref[:, ATTN_KV_W:] = jnp.concatenate([x.astype(F32)[:b] for x in dvs], axis=1)
        parts = [jnp.sum(d[g * b:(g + 1) * b], axis=0, keepdims=True) for d in dsk for g in range(ATTN_GROUP)]
        dsink_ref[...] += jnp.concatenate(parts, axis=1)

    blk = lambda w: pl.BlockSpec((b, w), lambda n: (n, 0))
    return pl.pallas_call(
        body, name="attn_bwd",
        grid=(nb,),
        in_specs=_attn_specs(nb) + [blk(ATTN_Q_W)],
        out_specs=[blk(ATTN_Q_W), blk(2 * ATTN_KV_W), blk(2 * ATTN_KV_W),
                   pl.BlockSpec((1, ATTN_Q_HEADS), lambda n: (0, 0))],
        out_shape=[jax.ShapeDtypeStruct((t, ATTN_Q_W), BF16), jax.ShapeDtypeStruct((t, 2 * ATTN_KV_W), F32),
                   jax.ShapeDtypeStruct((t, 2 * ATTN_KV_W), F32), jax.ShapeDtypeStruct((1, ATTN_Q_HEADS), F32)],
        compiler_params=_arb(1),
    )(p, p, tab, tab, sinks, do)


def kv_combine(dkvc, dkvp):
    t, w = dkvc.shape
    b = ATTN_BLOCK
    nb = t // b

    def body(c_ref, p_ref, o_ref):
        nxt = jnp.where(pl.program_id(0) < nb - 1, p_ref[...], 0.0)
        o_ref[...] = (c_ref[...] + nxt).astype(BF16)

    return pl.pallas_call(
        body, name="kv_combine",
        grid=(nb,),
        in_specs=[pl.BlockSpec((b, w), lambda n: (n, 0)),
                  pl.BlockSpec((b, w), lambda n: (jnp.minimum(n + 1, nb - 1), 0))],
        out_specs=pl.BlockSpec((b, w), lambda n: (n, 0)),
        out_shape=jax.ShapeDtypeStruct((t, w), BF16),
        compiler_params=_arb(1),
    )(dkvc, dkvp)


def _me():
    return lax.axis_index("x"), lax.axis_index("y"), lax.axis_index("c")


def _flip(v, bit):
    return 1 - v if bit else v


def _chip_index():
    return 2 * lax.axis_index("x") + lax.axis_index("y")


class _SplitGather:
    def __init__(self, x_refs, o_refs, send_sems, recv_sems):
        self.x_refs, self.o_refs, self.send_sems, self.recv_sems = x_refs, o_refs, send_sems, recv_sems
        self.mx, self.my, self.mc = _me()
        self.me = 2 * self.mx + self.my

    def _rows(self, a):
        rh = self.x_refs[a].shape[0] // 2
        return pl.ds(pl.multiple_of(self.mc * rh, 16), rh)

    def _fetch(self, a, k):
        rows, i = self._rows(a), 6 * a + k - 1
        return pltpu.make_async_remote_copy(
            src_ref=self.x_refs[a].at[rows], dst_ref=self.o_refs[a].at[self.me, rows], send_sem=self.send_sems.at[i],
            recv_sem=self.recv_sems.at[i], device_id=(_flip(self.mx, k >> 1), _flip(self.my, k & 1), self.mc),
            device_id_type=MESH)

    def _relay(self, a, k):
        theirs = self.o_refs[a].at[2 * _flip(self.mx, k >> 1) + _flip(self.my, k & 1), self._rows(a)]
        i = 6 * a + 2 + k
        return pltpu.make_async_remote_copy(src_ref=theirs, dst_ref=theirs, send_sem=self.send_sems.at[i],
                                            recv_sem=self.recv_sems.at[i], device_id=(self.mx, self.my, 1 - self.mc),
                                            device_id_type=MESH)

    def _each_copy(self):
        return [(a, k) for a in range(len(self.x_refs)) for k in (1, 2, 3)]

    def start(self):
        for a, k in self._each_copy():
            self._fetch(a, k).start()

    def relay(self):
        for a, k in self._each_copy():
            self._fetch(a, k).wait_recv()
            self._relay(a, k).start()

    def finish(self):
        for a, k in self._each_copy():
            self._fetch(a, k).wait_send()
            self._relay(a, k).wait()


def _fill_own_slot(got, x):
    return lax.dynamic_update_slice(got, x[None], (_chip_index(), 0, 0))


def allgather_chips(x, split):
    r, c = x.shape

    def body(x_ref, o_ref, send_sems, recv_sems):
        if split:
            gather = _SplitGather([x_ref], [o_ref], send_sems, recv_sems)
            gather.start()
            gather.relay()
            gather.finish()
            return
        mx, my, mc = _me()
        fetched = []
        for k in (1, 2, 3):
            cp = pltpu.make_async_remote_copy(src_ref=x_ref, dst_ref=o_ref.at[2 * mx + my],
                                              send_sem=send_sems.at[k - 1], recv_sem=recv_sems.at[k - 1],
                                              device_id=(_flip(mx, k >> 1), _flip(my, k & 1), mc), device_id_type=MESH)
            cp.start()
            fetched.append(cp)
        for cp in fetched:
            cp.wait()

    got = pl.pallas_call(
        body, name="allgather_chips",
        in_specs=[ANY], out_specs=ANY,
        out_shape=jax.ShapeDtypeStruct((4, r, c), x.dtype),
        scratch_shapes=[pltpu.SemaphoreType.DMA((6,)), pltpu.SemaphoreType.DMA((6,))],
    )(x)
    return _fill_own_slot(got, x)


def exchange(kind, arrays):
    make, per_array, out_of = CARRIED[kind]
    na = len(arrays)

    def body(*refs):
        copies = make(refs[:na], refs[na:2 * na], refs[2 * na], refs[2 * na + 1])
        for cp in copies:
            cp.start()
        for cp in copies:
            cp.wait()

    return pl.pallas_call(
        body, name=kind + "_exchange",
        in_specs=[ANY] * na, out_specs=[ANY] * na,
        out_shape=[jax.ShapeDtypeStruct(out_of(a), a.dtype) for a in arrays],
        scratch_shapes=[pltpu.SemaphoreType.DMA((per_array * na,))] * 2,
    )(*arrays)


def _row_tile(r, c):
    return max(d for d in range(16, r + 1, 16) if r % d == 0 and d * c * 4 <= (2 << 20))


def add_pair(g, got, core):
    _, _, r, c = g.shape
    tr = _row_tile(r, c)

    def body(core_ref, g_ref, got_ref, p32_ref, p16_ref):
        s = g_ref[...] + got_ref[...]
        p32_ref[...] = s
        p16_ref[...] = s.astype(BF16)

    blk = pl.BlockSpec((None, tr, c), lambda j, i, core_ref: (j, i, 0))
    return pl.pallas_call(
        body, name="add_pair",
        grid_spec=pltpu.PrefetchScalarGridSpec(
            num_scalar_prefetch=1, grid=(4, r // tr),
            in_specs=[pl.BlockSpec((None, None, tr, c), lambda j, i, core_ref: (j, core_ref[0], i, 0)), blk],
            out_specs=[blk, blk]),
        out_shape=[jax.ShapeDtypeStruct((4, r, c), F32), jax.ShapeDtypeStruct((4, r, c), BF16)],
        compiler_params=_arb(2),
    )(core, g, got)


def sum_slots(p32, q16, order):
    _, r, c = p32.shape
    tr = _row_tile(r, c)

    def body(order_ref, own_ref, a_ref, b_ref, c_ref, o_ref):
        o_ref[...] = ((own_ref[...] + a_ref[...].astype(F32)) + b_ref[...].astype(F32)) + c_ref[...].astype(F32)

    slot = lambda k: pl.BlockSpec((None, tr, c), functools.partial(lambda k, i, order_ref: (order_ref[k], i, 0), k))
    return pl.pallas_call(
        body, name="sum_slots",
        grid_spec=pltpu.PrefetchScalarGridSpec(
            num_scalar_prefetch=1, grid=(r // tr,),
            in_specs=[slot(0), slot(1), slot(2), slot(3)],
            out_specs=pl.BlockSpec((tr, c), lambda i, order_ref: (i, 0))),
        out_shape=jax.ShapeDtypeStruct((r, c), F32),
        compiler_params=_arb(1),
    )(order, p32, q16, q16, q16)


def allreduce_small(x):
    r, c = x.shape

    def body(x_ref, o_ref, buf, send_sems, recv_sems):
        mx, my, mc = _me()
        me = 4 * mx + 2 * my + mc
        buf[pl.ds(me, 1)] = x_ref[...][None]
        copies = []
        for k in range(1, 8):
            peer = (_flip(mx, k >> 2), _flip(my, (k >> 1) & 1), _flip(mc, k & 1))
            cp = pltpu.make_async_remote_copy(src_ref=x_ref, dst_ref=buf.at[me], send_sem=send_sems.at[k - 1],
                                              recv_sem=recv_sems.at[k - 1], device_id=peer, device_id_type=MESH)
            cp.start()
            copies.append(cp)
        for cp in copies:
            cp.wait()
        acc = buf[0]
        for d in range(1, 8):
            acc = acc + buf[d]
        o_ref[...] = acc

    return pl.pallas_call(
        body, name="allreduce_small",
        out_shape=jax.ShapeDtypeStruct((r, c), F32),
        scratch_shapes=[pltpu.VMEM((8, r, c), F32), pltpu.SemaphoreType.DMA((7,)), pltpu.SemaphoreType.DMA((7,))],
    )(x)


class GradReduce:
    def __init__(self, grads, shard_shapes):
        self.shapes = shard_shapes
        self.groups = _by_width(shard_shapes)
        self.core = lax.axis_index("c")
        me = _chip_index()
        self.order = jnp.stack([me, me ^ 1, me ^ 2, me ^ 3]).astype(jnp.int32)
        self.g = []
        for names in self.groups.values():
            slots = []
            for j in range(4):
                parts = []
                for n in names:
                    w = shard_shapes[n][1 + SHARDED[n]]
                    parts += [lax.slice_in_dim(gl, j * w, (j + 1) * w, axis=SHARDED[n]) for gl in grads[n]]
                slots.append(jnp.concatenate(parts, axis=0))
            g = jnp.stack(slots)
            self.g.append(g.reshape(4, 2, g.shape[1] // 2, g.shape[2]))
        self.p32 = self.p16 = None
        self.q16 = [None] * len(self.g)
        n = len(self.g)
        self.pieces = [list(range(2, n)), [1], [0]] if n > 2 else [[i] for i in reversed(range(n))]
        self.pieces = [p for p in self.pieces if p]
        self.stage = 0

    def next_job(self):
        if self.stage == 0:
            return "pair", self.g
        if self.stage <= len(self.pieces):
            return "scatter", [self.p16[i] for i in self.pieces[self.stage - 1]]
        return None

    def deliver(self, res):
        if self.stage == 0:
            core = self.core.astype(jnp.int32)[None]
            pairs = [add_pair(g, got, core) for g, got in zip(self.g, res)]
            self.p32, self.p16 = [p[0] for p in pairs], [p[1] for p in pairs]
        else:
            for i, q in zip(self.pieces[self.stage - 1], res):
                self.q16[i] = q
        self.stage += 1

    def finish(self):
        while (job := self.next_job()) is not None:
            self.deliver(exchange(*job))
        mine = [sum_slots(p, q, self.order) for p, q in zip(self.p32, self.q16)]
        theirs = exchange("share", mine)
        out = {}
        for names, m, t in zip(self.groups.values(), mine, theirs):
            red = jnp.where(self.core == 0, jnp.concatenate([m, t]), jnp.concatenate([t, m]))
            off = 0
            for n in names:
                layers, a, _ = self.shapes[n]
                out[n] = red[off:off + layers * a].reshape(self.shapes[n])
                off += layers * a
        return out


SHARDED = {"ffn1_w_gate_up": 1, "ffn1_w_down": 0, "ffn2_w_gate_up": 1, "ffn2_w_down": 0, "attn_w_in": 1,
           "attn_w_out": 0, "gdn_w_in": 1, "gdn_w_out": 0, "gdn_conv_w": 1}
REPLICATED = ["ffn1_norm", "mix_norm", "ffn2_norm", "attn_b_in", "attn_sinks", "attn_b_out", "gdn_A_log",
              "gdn_dt_bias", "gdn_norm_w", "final_norm"]
WEIGHTS = ["ffn1_norm", "ffn1_w_gate_up", "ffn1_w_down", "mix_norm", "ffn2_norm", "ffn2_w_gate_up", "ffn2_w_down",
           "attn_w_in", "attn_b_in", "attn_sinks", "attn_w_out", "attn_b_out", "gdn_w_in", "gdn_conv_w", "gdn_A_log",
           "gdn_dt_bias", "gdn_norm_w", "gdn_w_out", "final_norm"]


def _pack_rows(flats, dtype, width, row_multiple):
    flat = jnp.concatenate([f.astype(dtype).reshape(-1) for f in flats])
    per = width * row_multiple
    pad = (-flat.shape[0]) % per
    if pad:
        flat = jnp.concatenate([flat, jnp.zeros((pad,), dtype)])
    return flat.reshape(-1, width)


def _by_width(shapes):
    groups = {}
    for n, shape in shapes.items():
        groups.setdefault(shape[-1], []).append(n)
    return groups


def _pack_groups(shards, dtype):
    groups = _by_width({n: s.shape for n, s in shards.items()})
    packed = [jnp.concatenate([shards[n].astype(dtype).reshape(-1, width) for n in names], axis=0)
              for width, names in groups.items()]

    def unpack(gots):
        full = {}
        for names, got in zip(groups.values(), gots):
            off = 0
            for n in names:
                layers, a, _ = shards[n].shape
                full[n] = [jnp.concatenate([got[j, off + l * a:off + (l + 1) * a] for j in range(4)], axis=SHARDED[n])
                           for l in range(layers)]
                off += layers * a
        return full

    return packed, unpack


def _gather_weights(shards, dtype, split=True):
    packed, unpack = _pack_groups(shards, dtype)
    return unpack([allgather_chips(p, split) for p in packed])


def _small_pack(items):
    rows = []
    for a in items:
        f = a.astype(F32).reshape(-1)
        pad = (-f.shape[0]) % LANE
        rows.append(jnp.concatenate([f, jnp.zeros((pad,), F32)]) if pad else f)
    return _pack_rows(rows, F32, LANE, 8)


def _small_unpack(buf, shapes):
    flat = buf.reshape(-1)
    out, off = [], 0
    for shape in shapes:
        size = 1
        for s in shape:
            size *= s
        out.append(flat[off:off + size].reshape(shape))
        off += size + (-size) % LANE
    return out


def _rope_table(positions):
    t = positions.shape[0]
    inv_freq = ROPE_THETA ** (-jnp.arange(0, ROPE_DIM, 2, dtype=F32) / ROPE_DIM)
    ang = positions.astype(F32)[:, None] * inv_freq
    cos, sin = jnp.cos(ang), jnp.sin(ang)
    rest = ATTN_HEAD_DIM - ROPE_DIM
    zeros = lambda n: jnp.zeros((t, n), F32)
    c64 = jnp.concatenate([cos, cos, jnp.ones((t, rest), F32)], axis=1)
    s_up = jnp.concatenate([zeros(ROPE_DIM // 2), sin, zeros(rest)], axis=1)
    s_dn = jnp.concatenate([-sin, zeros(ROPE_DIM // 2 + rest)], axis=1)
    return jnp.concatenate([c64, c64, s_up, s_up, s_dn, s_dn], axis=1)


def _as2d(a):
    return a.reshape(-1, a.shape[-1]) if a.ndim > 1 else a.reshape(1, -1)


def kernel(x, positions, ffn1_norm, ffn1_w_gate_up, ffn1_w_down, mix_norm, ffn2_norm, ffn2_w_gate_up, ffn2_w_down, attn_w_in, attn_b_in, attn_sinks, attn_w_out, attn_b_out, gdn_w_in, gdn_conv_w, gdn_A_log, gdn_dt_bias, gdn_norm_w, gdn_w_out, final_norm, loss_target, m_ffn1_norm, m_ffn1_w_gate_up, m_ffn1_w_down, m_mix_norm, m_ffn2_norm, m_ffn2_w_gate_up, m_ffn2_w_down, m_attn_w_in, m_attn_b_in, m_attn_sinks, m_attn_w_out, m_attn_b_out, m_gdn_w_in, m_gdn_conv_w, m_gdn_A_log, m_gdn_dt_bias, m_gdn_norm_w, m_gdn_w_out, m_final_norm, v_ffn1_norm, v_ffn1_w_gate_up, v_ffn1_w_down, v_mix_norm, v_ffn2_norm, v_ffn2_w_gate_up, v_ffn2_w_down, v_attn_w_in, v_attn_b_in, v_attn_sinks, v_attn_w_out, v_attn_b_out, v_gdn_w_in, v_gdn_conv_w, v_gdn_A_log, v_gdn_dt_bias, v_gdn_norm_w, v_gdn_w_out, v_final_norm):
    given = dict(locals())
    w = {n: given[n] for n in WEIGHTS}
    d = D_MODEL
    h = x[0]
    target = loss_target[0]
    depth = ffn1_norm.shape[0]

    big = [n for n in SHARDED if n != "gdn_conv_w"]
    first = _gather_weights({n: w[n][0:1] for n in ("ffn1_w_gate_up", "ffn1_w_down")}, BF16)
    ffn_order = [(tag, l) for l in range(depth) for tag in ("ffn1", "ffn2")]
    ffn_w = {ffn_order[0]: (first["ffn1_w_gate_up"][0], first["ffn1_w_down"][0])}
    mixers_packed, mixers_unpack = _pack_groups({n: w[n] for n in big if not n.startswith("ffn")}, BF16)
    full = {}

    def run_ffn(tag, l, h):
        i = ffn_order.index((tag, l))
        wgu, wd = ffn_w[(tag, l)]
        nw = w[tag + "_norm"][l][None]
        if i + 1 == len(ffn_order):
            return ffn_fwd(h, nw, wgu, wd)
        ntag, nl = ffn_order[i + 1]
        shards = (w[ntag + "_w_gate_up"][nl].astype(BF16), w[ntag + "_w_down"][nl].astype(BF16))
        h, gu, a, ngu, nd, *gots = ffn_fwd(h, nw, wgu, wd, shards, also=mixers_packed if i == 0 else ())
        ffn_w[(ntag, nl)] = (ngu, nd)
        if i == 0:
            full.update(mixers_unpack([_fill_own_slot(got, x) for got, x in zip(gots, mixers_packed)]))
        return h, gu, a

    conv_full = _gather_weights({"gdn_conv_w": gdn_conv_w}, F32, split=False)["gdn_conv_w"]
    tab = _rope_table(positions[0])
    zero_d = jnp.zeros((1, d), F32)

    def gdn_params(j):
        w_in = full["gdn_w_in"][j]
        w_cat = jnp.concatenate([w_in, jnp.zeros((d, GDN_PROJ_W - w_in.shape[1]), BF16)], axis=1)
        conv = jnp.concatenate([conv_full[j], jnp.zeros((8 - GDN_CONV, GDN_QKV_W), F32)], axis=0)
        lanes = lambda vec: jnp.concatenate([jnp.zeros((GC_G,), F32), vec, jnp.zeros((LANE - GC_G - GDN_HEADS,), F32)])
        par = jnp.concatenate([lanes(gdn_A_log[j])[None], lanes(gdn_dt_bias[j])[None], jnp.zeros((6, LANE), F32)], axis=0)
        return w_cat, conv, par

    saved = []
    for l in range(depth):
        j = l // 2
        rec = {"h1": h}
        h, rec["ffn1_gu"], rec["ffn1_a"] = run_ffn("ffn1", l, h)
        rec["h2"] = h
        if l % 2 == 0:
            p = norm_proj(h, mix_norm[l][None], full["attn_w_in"][j], attn_b_in[j][None])
            o = attn_fwd(p, tab, attn_sinks[j][None])
            h = linear_residual(h, o, full["attn_w_out"][j], attn_b_out[j][None])
            rec.update(p=p, o=o)
        else:
            w_cat, conv, par = gdn_params(j)
            proj = norm_proj(h, mix_norm[l][None], w_cat, jnp.zeros((1, GDN_PROJ_W), F32))
            qkv, gc = gdn_pre_fwd(proj, conv, par)
            on, st, rec["m_offs"] = gdn_chunk_fwd(qkv, gc, proj, gdn_norm_w[j][None])
            h = linear_residual(h, on, full["gdn_w_out"][j], zero_d)
            rec.update(proj=proj, qkv=qkv, gc=gc, on=on, st=st, w_cat=w_cat, conv=conv, par=par)
        rec["h3"] = h
        h, rec["ffn2_gu"], rec["ffn2_a"] = run_ffn("ffn2", l, h)
        saved.append(rec)

    dh, loss_tile, d_final = loss_head(h, final_norm[None], target)

    g = {n: [None] * w[n].shape[0] for n in WEIGHTS if n != "final_norm"}
    reducers = []

    def layer_of(n, i):
        return i if n.startswith("ffn") else (2 * i if n.startswith("attn") else 2 * i + 1)

    def layer_reduce(layer):
        picks = {n: idx for n in big if (idx := [i for i in range(w[n].shape[0]) if layer_of(n, i) == layer])}
        return GradReduce({n: [g[n][i] for i in idx] for n, idx in picks.items()},
                          {n: (len(idx),) + w[n].shape[1:] for n, idx in picks.items()})

    def tn(x, y, scale=1.0):
        big_enough = x.shape[1] * y.shape[1] >= D_MODEL * D_FF
        red = next((r for r in reducers if r.next_job() is not None), None) if big_enough else None
        if red is None:
            return matmul_tn(x, y, scale)
        out, res = matmul_tn(x, y, scale, carry=red.next_job())
        red.deliver(res)
        return out
    for l in reversed(range(depth)):
        j = l // 2
        rec = saved[l]

        def ffn_back(tag, h_in, dh):
            wgu, wd = ffn_w[(tag, l)]
            dh_new, hn, dgu, dn = ffn_bwd(h_in, w[tag + "_norm"][l][None], wgu, wd, rec[tag + "_gu"], dh)
            g[tag + "_w_gate_up"][l] = tn(hn, dgu)
            g[tag + "_w_down"][l] = tn(rec[tag + "_a"], dh, 0.5)
            g[tag + "_norm"][l] = dn[0]
            return dh_new

        dh = ffn_back("ffn2", rec["h3"], dh)
        if l % 2 == 0:
            w_in, w_out = full["attn_w_in"][j], full["attn_w_out"][j]
            do, db_out = matmul_nt(dh, w_out)
            g["attn_w_out"][j] = matmul_tn(rec["o"], dh)
            g["attn_b_out"][j] = db_out[0]
            dq, dkvc, dkvp, dsink = attn_bwd(rec["p"], tab, attn_sinks[j][None], do)
            dkv = kv_combine(dkvc, dkvp)
            dh, hn, dn, cs_q, cs_kv = norm_proj_bwd(rec["h2"], mix_norm[l][None], dh, [dq, dkv],
                                                    [w_in[:, :ATTN_Q_W], w_in[:, ATTN_Q_W:]])
            g["attn_w_in"][j] = jnp.concatenate([matmul_tn(hn, dq), matmul_tn(hn, dkv)], axis=1)
            g["attn_b_in"][j] = jnp.concatenate([cs_q[0], cs_kv[0]])
            g["attn_sinks"][j] = dsink[0]
        else:
            w_cat, conv, par = rec["w_cat"], rec["conv"], rec["par"]
            d_on, _ = matmul_nt(dh, full["gdn_w_out"][j])
            g["gdn_w_out"][j] = matmul_tn(rec["on"], dh)
            dq, dk, dv, dz, dgc_heads, dnw = gdn_chunk_bwd(rec["qkv"], rec["gc"], rec["proj"], gdn_norm_w[j][None],
                                                           rec["st"], rec["m_offs"], d_on)
            dy, dba, dpar = gdn_pre_bwd(rec["proj"], conv, par, dq, dk, dv, dgc_heads)
            dx, dconv = gdn_conv_bwd(rec["proj"], conv, dy)
            nz = GDN_QKV_W + GDN_HEADS * GDN_DK
            dh, hn, dn, _, _, _ = norm_proj_bwd(rec["h2"], mix_norm[l][None], dh, [dx, dz, dba],
                                                [w_cat[:, :GDN_QKV_W], w_cat[:, GDN_QKV_W:nz], w_cat[:, nz:]])
            g["gdn_w_in"][j] = jnp.concatenate(
                [tn(hn, dx), matmul_tn(hn, dz), matmul_tn(hn, dba)[:, :2 * GDN_HEADS]], axis=1)
            g["gdn_conv_w"][j] = dconv[:GDN_CONV]
            g["gdn_A_log"][j] = dpar[0, GDN_HEADS:2 * GDN_HEADS]
            g["gdn_dt_bias"][j] = dpar[1, GDN_HEADS:2 * GDN_HEADS]
            g["gdn_norm_w"][j] = dnw[0]
        g["mix_norm"][l] = dn[0]
        dh = ffn_back("ffn1", rec["h1"], dh)
        if l > 0:
            reducers.append(layer_reduce(l))
    grad_x = dh[None]

    small_names = REPLICATED + ["gdn_conv_w"]
    local = {n: jnp.stack(g[n]) for n in small_names if n != "final_norm"}
    local["final_norm"] = d_final[0]
    small_shapes = [(1,)] + [local[n].shape for n in small_names]
    small = allreduce_small(_small_pack([loss_tile[0, 0:1]] + [local[n] for n in small_names]))
    small = _small_unpack(small, small_shapes)
    loss = small[0][0]
    grads = dict(zip(small_names, small[1:]))
    conv_cols = gdn_conv_w.shape[2]
    grads["gdn_conv_w"] = lax.dynamic_slice_in_dim(grads["gdn_conv_w"], _chip_index() * conv_cols, conv_cols, axis=2)
    by_layer = [layer_reduce(0).finish()] + [r.finish() for r in reversed(reducers)]
    grads.update({n: jnp.concatenate([part[n] for part in by_layer if n in part], axis=0) for n in big})

    delta, new_m, new_v = {}, {}, {}
    for n in SHARDED:
        dl, nm, nv = adamw(_as2d(w[n]), _as2d(grads[n]), _as2d(given["m_" + n]), _as2d(given["v_" + n]))
        delta[n], new_m[n], new_v[n] = dl.reshape(w[n].shape), nm.reshape(w[n].shape), nv.reshape(w[n].shape)
    shapes = [w[n].shape for n in REPLICATED]
    packed = [_small_pack([src[n] for n in REPLICATED]) for src in
              (w, grads, {n: given["m_" + n] for n in REPLICATED}, {n: given["v_" + n] for n in REPLICATED})]
    for dst, buf in zip((delta, new_m, new_v), adamw(*packed)):
        dst.update(zip(REPLICATED, _small_unpack(buf, shapes)))

    return (loss, grad_x, *[grads[n] for n in WEIGHTS], *[delta[n] for n in WEIGHTS],
            *[new_m[n] for n in WEIGHTS], *[new_v[n] for n in WEIGHTS])
```

```python
import functools

import jax
import jax.numpy as jnp
from jax import lax
from jax.experimental import pallas as pl
from jax.experimental.pallas import tpu as pltpu

F32 = jnp.float32
BF16 = jnp.bfloat16
HI = lax.Precision.HIGHEST
MESH = pl.DeviceIdType.MESH

D_MODEL = 1024
D_FF = 2816
NORM_EPS = 1e-6
LANE = 128

ATTN_Q_HEADS = 16
ATTN_KV_HEADS = 4
ATTN_HEAD_DIM = 64
ATTN_GROUP = 4
ATTN_BLOCK = 128
ROPE_DIM = 16
ROPE_THETA = 500000.0
ATTN_Q_W = 1024
ATTN_KV_W = 256

GDN_HEADS = 8
GDN_DK = 128
GDN_CONV = 4
GDN_CHUNK = 64
GDN_QKV_W = 3072

ADAM_LR = 0.001
ADAM_B1 = 0.9
ADAM_B2 = 0.999
ADAM_EPS = 1e-08
ADAM_WD = 0.01
ADAM_STEP = 10

NEG = -1e30


def _dot(a, b, prec=None):
    return lax.dot_general(a, b, (((1,), (0,)), ((), ())), precision=prec, preferred_element_type=F32)


def _dot_nt(a, b, prec=None):
    return lax.dot_general(a, b, (((1,), (1,)), ((), ())), precision=prec, preferred_element_type=F32)


def _dot_tn(a, b, prec=None):
    return lax.dot_general(a, b, (((0,), (0,)), ((), ())), precision=prec, preferred_element_type=F32)


def _bdot(a, b):
    return _dot(a.astype(BF16), b.astype(BF16))


def _bdot_nt(a, b):
    return _dot_nt(a.astype(BF16), b.astype(BF16))


def _bdot_tn(a, b):
    return _dot_tn(a.astype(BF16), b.astype(BF16))


def _sigmoid(x):
    return 1.0 / (1.0 + jnp.exp(-x))


def _silu(x):
    return x * _sigmoid(x)


def _silu_grad(x):
    s = _sigmoid(x)
    return s * (1.0 + x * (1.0 - s))


def _rms(x, w):
    r = lax.rsqrt(jnp.mean(x * x, axis=-1, keepdims=True) + NORM_EPS)
    xhat = x * r
    return xhat * w, xhat, r


def _rms_bwd(dy, w, xhat, r):
    dxhat = dy * w
    dx = r * (dxhat - xhat * jnp.mean(dxhat * xhat, axis=-1, keepdims=True))
    dw = jnp.sum(dy * xhat, axis=0, keepdims=True)
    return dx, dw


def _arb(n):
    return pltpu.CompilerParams(dimension_semantics=("arbitrary",) * n)


def _tile(n, want):
    t = min(n, want)
    assert n % t == 0, (n, want)
    return t


def _iota2(shape, dim):
    return lax.broadcasted_iota(jnp.int32, shape, dim)


_NN = (((1,), (0,)), ((), ()))
_NT = (((1,), (1,)), ((), ()))
_TN = (((0,), (0,)), ((), ()))


def _raw1(a, b, dn):
    return lax.dot_general(a.astype(BF16), b.astype(BF16), dn, preferred_element_type=F32)


def _raw3(a, b, dn):
    ah, bh = a.astype(BF16), b.astype(BF16)
    al, bl = (a - ah.astype(F32)).astype(BF16), (b - bh.astype(F32)).astype(BF16)
    f = lambda x, y: lax.dot_general(x, y, dn, preferred_element_type=F32)
    return f(ah, bh) + f(ah, bl) + f(al, bh)


def _make_mm(raw):
    @jax.custom_vjp
    def mm(a, b):
        return raw(a, b, _NN)

    mm.defvjp(lambda a, b: (raw(a, b, _NN), (a, b)),
              lambda res, ct: (raw(ct, res[1], _NT), raw(res[0], ct, _TN)))

    @jax.custom_vjp
    def mm_nt(a, b):
        return raw(a, b, _NT)

    mm_nt.defvjp(lambda a, b: (raw(a, b, _NT), (a, b)),
                 lambda res, ct: (raw(ct, res[1], _NN), raw(ct, res[0], _TN)))
    return mm, mm_nt


_mm1, _mm1_nt = _make_mm(_raw1)
_mm3, _mm3_nt = _make_mm(_raw3)


def _eye(n):
    return (_iota2((n, n), 0) == _iota2((n, n), 1)).astype(F32)


def _each(f, *lists):
    return [f(*t) for t in zip(*lists)]


def _inv_newton(mats):
    eye = _eye(mats[0].shape[0])
    ps = [-a for a in mats]
    ms = [eye + p for p in ps]
    k = 1
    while 2 * k < GDN_CHUNK:
        ps = [_raw1(p, p, _NN) for p in ps]
        ms = _each(lambda m, p: m + _raw1(m, p, _NN), ms, ps)
        k *= 2
    rs = _each(lambda a, m: eye - m - _raw3(a, m, _NN), mats, ms)
    return _each(lambda m, r: m + _raw1(m, r, _NN), ms, rs)


def _inv_cotangent(ms, dms):
    ts = _each(lambda m, dm: _raw3(m, dm, _TN), ms, dms)
    return _each(lambda t, m: -_raw3(t, m, _NT), ts, ms)


def _chunk_causal(n):
    r, s = _iota2((n, n), 0), _iota2((n, n), 1)
    same = (r // GDN_CHUNK) == (s // GDN_CHUNK)
    return same & (r >= s), same & (r > s)


def _gdn_a(k, dcb, bb128):
    causal, strict = _chunk_causal(k[0].shape[0])
    decay_l = [jnp.exp(jnp.where(causal, d - d.T, NEG)) for d in dcb]
    kb = _each(lambda a, b: a * b, k, bb128)
    return _each(lambda x, y, dl: jnp.where(strict, _mm1_nt(x, y) * dl, 0.0), kb, k, decay_l)


def _gdn_rest(q, k, v, dcb, dcb128, dlb128, bb128, m_off):
    causal, _ = _chunk_causal(q[0].shape[0])
    decay_l = [jnp.exp(jnp.where(causal, d - d.T, NEG)) for d in dcb]
    kb = _each(lambda a, b: a * b, k, bb128)
    edc = [jnp.exp(d) for d in dcb128]
    rhs = _each(lambda vv, bb, kk, e: jnp.concatenate([vv * bb, kk * e], axis=1), v, bb128, kb, edc)
    sol = _each(lambda x, m: x + _mm3(m, x), rhs, m_off)
    aqk = _each(lambda x, y, dl: jnp.where(causal, _mm1_nt(x, y) * dl, 0.0), q, k, decay_l)
    q_dec = _each(lambda x, e: x * e, q, edc)
    k_dec = _each(lambda x, dl, dc: x * jnp.exp(dl - dc), k, dlb128, dcb128)
    return sol, aqk, q_dec, k_dec


def _gdn_local(q, k, v, dcb, dcb128, dlb128, bb128):
    eye = _eye(q[0].shape[0])
    m_off = [m - eye for m in _inv_newton(_gdn_a(k, dcb, bb128))]
    return _gdn_rest(q, k, v, dcb, dcb128, dlb128, bb128, m_off) + (m_off,)


def _gated_norm(o, z, nw):
    r = lax.rsqrt(jnp.mean(o * o, axis=-1, keepdims=True) + NORM_EPS)
    return o * r * nw * _silu(z)


def _gated_norm_bwd(dy, o, z, nw):
    r = lax.rsqrt(jnp.mean(o * o, axis=-1, keepdims=True) + NORM_EPS)
    xhat = o * r
    sz = _silu(z)
    dxhat = dy * nw * sz
    do = r * (dxhat - xhat * jnp.mean(dxhat * xhat, axis=-1, keepdims=True))
    dz = dy * xhat * nw * _silu_grad(z)
    dnw = jnp.sum(dy * xhat * sz, axis=0, keepdims=True)
    return do, dz, dnw


GDN_HEAD_GROUP = 8
GC_BETA, GC_G, GC_DECAY, GC_LAST = 0, GDN_HEADS, 2 * GDN_HEADS, 3 * GDN_HEADS


def _gate_cols(gc, h, rows):
    lane = _iota2((rows, LANE), 1)
    col = lambda off: jnp.sum(jnp.where(lane == off + h, gc, 0.0), axis=-1, keepdims=True)
    return col(GC_BETA), col(GC_DECAY), col(GC_LAST)


def _gdn_local_args(q_ref, k_ref, v_ref, gc_ref, hp, n):
    gc = gc_ref[...]
    lanes = [slice(hh * LANE, (hh + 1) * LANE) for hh in range(hp)]
    cols = [_gate_cols(gc, pl.program_id(0) * hp + hh, n) for hh in range(hp)]
    bcast = lambda i, w: [jnp.broadcast_to(c[i], (n, w)) for c in cols]
    args = ([q_ref[:, ls] for ls in lanes], [k_ref[:, ls] for ls in lanes], [v_ref[:, ls] for ls in lanes],
            bcast(1, n), bcast(1, LANE), bcast(2, LANE), bcast(0, LANE))
    return args, [c[2] for c in cols], lanes


def gdn_chunk_fwd(qkv, gc, proj, norm_w):
    t = qkv.shape[0]
    c = GDN_CHUNK
    tc = _tile(t, 256)
    nsub = tc // c

    hp = GDN_HEAD_GROUP

    def body(q_ref, k_ref, v_ref, gc_ref, z_ref, nw_ref, on_ref, st_ref, m_ref, s_ref):
        @pl.when(pl.program_id(1) == 0)
        def _():
            s_ref[...] = jnp.zeros_like(s_ref)

        args, dl, lanes = _gdn_local_args(q_ref, k_ref, v_ref, gc_ref, hp, tc)
        sol, aqk, q_dec, k_dec, m_off = _gdn_local(*args)
        for hh in range(hp):
            m_ref[hh] = m_off[hh]
        u = [x[:, :GDN_DK] for x in sol]
        w = [x[:, GDN_DK:] for x in sol]
        s = [s_ref[hh] for hh in range(hp)]
        v_new, o_state = [[] for _ in range(hp)], [[] for _ in range(hp)]
        for j in range(nsub):
            sl = slice(j * c, (j + 1) * c)
            for hh in range(hp):
                st_ref[hh, j] = s[hh]
            vn = _each(lambda uu, ww, ss: uu[sl] - _bdot(ww[sl], ss), u, w, s)
            os_ = _each(lambda qq, ss: _bdot(qq[sl], ss), q_dec, s)
            s = _each(lambda ss, d, kk, vv: ss * jnp.exp(d[j * c:j * c + 1]) + _bdot_tn(kk[sl], vv), s, dl, k_dec, vn)
            for hh in range(hp):
                v_new[hh].append(vn[hh])
                o_state[hh].append(os_[hh])
        for hh in range(hp):
            s_ref[hh] = s[hh]
        cat = lambda xs: jnp.concatenate(xs, axis=0)
        o = _each(lambda os_, aa, vv: cat(os_) + _bdot(aa, cat(vv)), o_state, aqk, v_new)
        for hh, ls in enumerate(lanes):
            on_ref[:, ls] = _gated_norm(o[hh], z_ref[:, ls], nw_ref[...]).astype(BF16)

    col = lambda off: pl.BlockSpec((tc, hp * LANE), lambda g, i: (i, off // hp + g))
    return pl.pallas_call(
        body, name="gdn_chunk_fwd",
        grid=(GDN_HEADS // hp, t // tc),
        in_specs=[col(0), col(GDN_HEADS), col(2 * GDN_HEADS),
                  pl.BlockSpec((tc, LANE), lambda g, i: (i, 0)),
                  col(GDN_QKV_W // LANE),
                  pl.BlockSpec((1, LANE), lambda g, i: (0, 0))],
        out_specs=[col(0),
                   pl.BlockSpec((hp, nsub, GDN_DK, GDN_DK), lambda g, i: (g, i, 0, 0)),
                   pl.BlockSpec((hp, tc, tc), lambda g, i: (g, i, 0))],
        out_shape=[jax.ShapeDtypeStruct((t, GDN_HEADS * GDN_DK), BF16),
                   jax.ShapeDtypeStruct((GDN_HEADS, t // c, GDN_DK, GDN_DK), F32),
                   jax.ShapeDtypeStruct((GDN_HEADS, t, tc), F32)],
        scratch_shapes=[pltpu.VMEM((hp, GDN_DK, GDN_DK), F32)],
        compiler_params=_arb(2),
    )(qkv, qkv, qkv, gc, proj, norm_w)


def gdn_chunk_bwd(qkv, gc, proj, norm_w, states, m_offs, d_on):
    t = qkv.shape[0]
    c = GDN_CHUNK
    tc = _tile(t, 256)
    nsub = tc // c
    nblk = t // tc
    hp = GDN_HEAD_GROUP

    def body(q_ref, k_ref, v_ref, gc_ref, z_ref, nw_ref, st_ref, m_ref, don_ref,
             dq_ref, dk_ref, dv_ref, dz_ref, dgc_ref, dnw_ref, ds_ref):
        @pl.when(pl.program_id(1) == 0)
        def _():
            ds_ref[...] = jnp.zeros_like(ds_ref)

        @pl.when((pl.program_id(1) == 0) & (pl.program_id(0) == 0))
        def _():
            dnw_ref[...] = jnp.zeros_like(dnw_ref)

        rows = [slice(j * c, (j + 1) * c) for j in range(nsub)]
        cat = lambda xs: jnp.concatenate(xs, axis=0)
        lsum = lambda x: jnp.sum(x, axis=-1, keepdims=True)
        lane = _iota2((tc, LANE), 1)
        row = _iota2((tc, 1), 0)
        heads = range(hp)
        args, dl, lanes = _gdn_local_args(q_ref, k_ref, v_ref, gc_ref, hp, tc)
        m_off = [m_ref[hh] for hh in heads]
        (sol, aqk, q_dec, k_dec), vjp = jax.vjp(_gdn_rest, *args, m_off)
        u = [x[:, :GDN_DK] for x in sol]
        w = [x[:, GDN_DK:] for x in sol]
        states = [[st_ref[hh, j] for j in range(nsub)] for hh in heads]
        v_new = _each(lambda uu, ww, st: [uu[sl] - _bdot(ww[sl], s) for sl, s in zip(rows, st)], u, w, states)
        v_all = [cat(v) for v in v_new]
        o = _each(lambda qq, st, aa, vv: cat([_bdot(qq[sl], s) for sl, s in zip(rows, st)]) + _bdot(aa, vv),
                  q_dec, states, aqk, v_all)
        nw = nw_ref[...]
        gn = [_gated_norm_bwd(don_ref[:, ls], o[hh], z_ref[:, ls], nw) for hh, ls in enumerate(lanes)]
        do = [x[0] for x in gn]
        for hh, ls in enumerate(lanes):
            dnw_ref[...] += gn[hh][2]
            dz_ref[:, ls] = gn[hh][1].astype(BF16)
        d_aqk = _each(_bdot_nt, do, v_all)
        dv_o = _each(_bdot_tn, aqk, do)
        ds = [ds_ref[hh] for hh in heads]
        d_u, d_w, d_qdec, d_kdec, d_last = ([[None] * nsub for _ in heads] for _ in range(5))
        for j in reversed(range(nsub)):
            sl = rows[j]
            sj = [states[hh][j] for hh in heads]
            cd = [jnp.exp(d[j * c:j * c + 1]) for d in dl]
            du = _each(lambda dvo, kk, dd: dvo[sl] + _bdot(kk[sl], dd), dv_o, k_dec, ds)
            dqd = _each(lambda dd, s: _bdot_nt(dd[sl], s), do, sj)
            dkd = _each(lambda vv, dd: _bdot_nt(vv[j], dd), v_new, ds)
            dla = _each(lambda s, dd, cc: jnp.sum(lsum(s * dd), axis=0, keepdims=True) * cc, sj, ds, cd)
            dw = _each(lambda x, s: -_bdot_nt(x, s), du, sj)
            ds = _each(lambda qq, dd, cc, dsn, ww, x: _bdot_tn(qq[sl], dd[sl]) + cc * dsn - _bdot_tn(ww[sl], x),
                       q_dec, do, cd, ds, w, du)
            for hh in heads:
                d_u[hh][j], d_w[hh][j], d_qdec[hh][j], d_kdec[hh][j], d_last[hh][j] = du[hh], dw[hh], dqd[hh], dkd[hh], dla[hh]
        for hh in heads:
            ds_ref[hh] = ds[hh]
        d_sol = _each(lambda a, b: jnp.concatenate([cat(a), cat(b)], axis=1), d_u, d_w)
        dq, dk, dv, d_dcb, d_dcb128, d_dlb128, d_bb, d_m = vjp(
            (d_sol, d_aqk, [cat(x) for x in d_qdec], [cat(x) for x in d_kdec]))
        eye = _eye(tc)
        d_a = _inv_cotangent([m + eye for m in m_off], d_m)
        k_args = (args[1], args[3], args[6])
        dk_a, d_dcb_a, d_bb_a = jax.vjp(_gdn_a, *k_args)[1](d_a)
        add = lambda xs, ys: _each(lambda x, y: x + y, xs, ys)
        dk, d_dcb, d_bb = add(dk, dk_a), add(d_dcb, d_dcb_a), add(d_bb, d_bb_a)
        dgc = jnp.zeros((tc, LANE), F32)
        for hh, ls in enumerate(lanes):
            h = pl.program_id(0) * hp + hh
            dq_ref[:, ls] = dq[hh]
            dk_ref[:, ls] = dk[hh]
            dv_ref[:, ls] = dv[hh]
            d_dl = lsum(d_dlb128[hh])
            for j in range(nsub):
                d_dl = d_dl + jnp.where(row == j * c, d_last[hh][j], 0.0)
            dgc = dgc + jnp.where(lane == GC_BETA + h, lsum(d_bb[hh]),
                                  jnp.where(lane == GC_DECAY + h, lsum(d_dcb[hh]) + lsum(d_dcb128[hh]),
                                            jnp.where(lane == GC_LAST + h, d_dl, 0.0)))
        dgc_ref[0] = dgc

    rev = lambda i: nblk - 1 - i
    col = lambda off: pl.BlockSpec((tc, hp * LANE), lambda g, i: (rev(i), off // hp + g))
    return pl.pallas_call(
        body, name="gdn_chunk_bwd",
        grid=(GDN_HEADS // hp, nblk),
        in_specs=[col(0), col(GDN_HEADS), col(2 * GDN_HEADS),
                  pl.BlockSpec((tc, LANE), lambda g, i: (rev(i), 0)),
                  col(GDN_QKV_W // LANE),
                  pl.BlockSpec((1, LANE), lambda g, i: (0, 0)),
                  pl.BlockSpec((hp, nsub, GDN_DK, GDN_DK), lambda g, i: (g, rev(i), 0, 0)),
                  pl.BlockSpec((hp, tc, tc), lambda g, i: (g, rev(i), 0)),
                  col(0)],
        out_specs=[col(0), col(0), col(0), col(0),
                   pl.BlockSpec((1, tc, LANE), lambda g, i: (g, rev(i), 0)),
                   pl.BlockSpec((1, LANE), lambda g, i: (0, 0))],
        out_shape=[
            jax.ShapeDtypeStruct((t, GDN_HEADS * GDN_DK), F32),
            jax.ShapeDtypeStruct((t, GDN_HEADS * GDN_DK), F32),
            jax.ShapeDtypeStruct((t, GDN_HEADS * GDN_DK), F32),
            jax.ShapeDtypeStruct((t, GDN_HEADS * GDN_DK), BF16),
            jax.ShapeDtypeStruct((GDN_HEADS // hp, t, LANE), F32),
            jax.ShapeDtypeStruct((1, LANE), F32)],
        scratch_shapes=[pltpu.VMEM((hp, GDN_DK, GDN_DK), F32)],
        compiler_params=_arb(2),
    )(qkv, qkv, qkv, gc, proj, norm_w, states, m_offs, d_on)


GDN_PROJ_W = GDN_QKV_W + 1024 + LANE
GDN_Q_SCALE = GDN_DK ** -0.5


def _shift_rows(ext, shift, lo, n):
    if shift == 0:
        return ext[lo:lo + n]
    return pltpu.roll(ext, shift, 0)[lo:lo + n]


def _conv_fwd(x, halo, w):
    n = x.shape[0]
    ext = jnp.concatenate([halo, x], axis=0)
    y = w[GDN_CONV - 1:GDN_CONV] * x
    for j in range(GDN_CONV - 1):
        y = y + w[j:j + 1] * _shift_rows(ext, GDN_CONV - 1 - j, 8, n)
    return y


def _chunk_masks(n):
    r, s = _iota2((n, n), 0), _iota2((n, n), 1)
    same = (r // GDN_CHUNK) == (s // GDN_CHUNK)
    return (same & (r >= s)).astype(F32), (same & (r <= s)).astype(F32), same.astype(F32)


def _softplus(x):
    return jnp.maximum(x, 0.0) + jnp.log(1.0 + jnp.exp(-jnp.abs(x)))


def _l2n(t):
    rs = lax.rsqrt(jnp.sum(t * t, axis=-1, keepdims=True) + NORM_EPS)
    return t * rs, rs


def gdn_pre_fwd(proj, conv_w, gate_par):
    t = proj.shape[0]
    tm = _tile(t, 256)

    def body(x_ref, halo_ref, ba_ref, w_ref, gp_ref, qkv_ref, gc_ref):
        i = pl.program_id(0)
        halo = jnp.where(i > 0, halo_ref[...], 0.0)
        y = _silu(_conv_fwd(x_ref[...], halo, w_ref[...]))
        for hh in range(2 * GDN_HEADS):
            sl = slice(hh * LANE, (hh + 1) * LANE)
            tn, _ = _l2n(y[:, sl])
            qkv_ref[:, sl] = tn * GDN_Q_SCALE if hh < GDN_HEADS else tn
        qkv_ref[:, 2 * GDN_HEADS * LANE:] = y[:, 2 * GDN_HEADS * LANE:]
        ba = ba_ref[...]
        lane = _iota2(ba.shape, 1)
        gp = gp_ref[...]
        is_a = (lane >= GC_G) & (lane < GC_G + GDN_HEADS)
        g = jnp.where(is_a, -jnp.exp(gp[0:1]) * _softplus(ba + gp[1:2]), 0.0)
        tri, _, same = _chunk_masks(tm)
        decay = pltpu.roll(_dot(tri, g, HI), GC_DECAY - GC_G, 1)
        last = pltpu.roll(_dot(same, g, HI), GC_LAST - GC_G, 1)
        gc_ref[...] = jnp.where(lane < GDN_HEADS, _sigmoid(ba), g) + decay + last

    return pl.pallas_call(
        body, name="gdn_pre_fwd",
        grid=(t // tm,),
        in_specs=[pl.BlockSpec((tm, GDN_QKV_W), lambda i: (i, 0)),
                  pl.BlockSpec((8, GDN_QKV_W), lambda i: (jnp.maximum(i * (tm // 8) - 1, 0), 0)),
                  pl.BlockSpec((tm, LANE), lambda i: (i, (GDN_QKV_W + 1024) // LANE)),
                  pl.BlockSpec((8, GDN_QKV_W), lambda i: (0, 0)),
                  pl.BlockSpec((8, LANE), lambda i: (0, 0))],
        out_specs=[pl.BlockSpec((tm, GDN_QKV_W), lambda i: (i, 0)),
                   pl.BlockSpec((tm, LANE), lambda i: (i, 0))],
        out_shape=[jax.ShapeDtypeStruct((t, GDN_QKV_W), F32), jax.ShapeDtypeStruct((t, LANE), F32)],
        compiler_params=_arb(1),
    )(proj, proj, proj, conv_w, gate_par)


def gdn_pre_bwd(proj, conv_w, gate_par, dq, dk, dv, dgc_heads):
    t = proj.shape[0]
    tm = _tile(t, 256)

    def body(x_ref, halo_ref, ba_ref, w_ref, gp_ref, dq_ref, dk_ref, dv_ref, dgc_ref, dy_ref, dba_ref, dgp_ref):
        i = pl.program_id(0)

        @pl.when(i == 0)
        def _():
            dgp_ref[...] = jnp.zeros_like(dgp_ref)

        halo = jnp.where(i > 0, halo_ref[...], 0.0)
        y = _conv_fwd(x_ref[...], halo, w_ref[...])
        for hh in range(3 * GDN_HEADS):
            sl = slice(hh * LANE, (hh + 1) * LANE)
            hsl = slice((hh % GDN_HEADS) * LANE, (hh % GDN_HEADS + 1) * LANE)
            yy = y[:, sl]
            if hh < 2 * GDN_HEADS:
                tn, rs = _l2n(_silu(yy))
                dtn = dq_ref[:, hsl] * GDN_Q_SCALE if hh < GDN_HEADS else dk_ref[:, hsl]
                dsil = rs * (dtn - tn * jnp.sum(dtn * tn, axis=-1, keepdims=True))
            else:
                dsil = dv_ref[:, hsl]
            dy_ref[:, sl] = dsil * _silu_grad(yy)
        dgc = dgc_ref[0]
        for hh in range(1, dgc_heads.shape[0]):
            dgc = dgc + dgc_ref[hh]
        ba = ba_ref[...]
        lane = _iota2(ba.shape, 1)
        _, tri_t, same = _chunk_masks(tm)
        d_decay = jnp.where((lane >= GC_DECAY) & (lane < GC_DECAY + GDN_HEADS), dgc, 0.0)
        d_last = jnp.where((lane >= GC_LAST) & (lane < GC_LAST + GDN_HEADS), dgc, 0.0)
        dgc = (jnp.where(lane < GDN_HEADS, dgc, 0.0) + pltpu.roll(_dot(tri_t, d_decay, HI), LANE - (GC_DECAY - GC_G), 1)
               + pltpu.roll(_dot(same, d_last, HI), LANE - (GC_LAST - GC_G), 1))
        gp = gp_ref[...]
        xg = ba + gp[1:2]
        ea = jnp.exp(gp[0:1])
        sp = _softplus(xg)
        sb = _sigmoid(ba)
        is_b = lane < GDN_HEADS
        is_a = (lane >= GDN_HEADS) & (lane < 2 * GDN_HEADS)
        d_pre = jnp.where(is_a, dgc * (-ea) * _sigmoid(xg), 0.0)
        dba_ref[...] = jnp.where(is_b, dgc * sb * (1.0 - sb), d_pre).astype(BF16)
        d_alog = jnp.sum(jnp.where(is_a, dgc * (-ea) * sp, 0.0), axis=0, keepdims=True)
        d_dtb = jnp.sum(d_pre, axis=0, keepdims=True)
        row = _iota2((8, LANE), 0)
        dgp_ref[...] += jnp.where(row == 0, d_alog, jnp.where(row == 1, d_dtb, 0.0))

    hspec = pl.BlockSpec((tm, GDN_HEADS * LANE), lambda i: (i, 0))
    return pl.pallas_call(
        body, name="gdn_pre_bwd",
        grid=(t // tm,),
        in_specs=[pl.BlockSpec((tm, GDN_QKV_W), lambda i: (i, 0)),
                  pl.BlockSpec((8, GDN_QKV_W), lambda i: (jnp.maximum(i * (tm // 8) - 1, 0), 0)),
                  pl.BlockSpec((tm, LANE), lambda i: (i, (GDN_QKV_W + 1024) // LANE)),
                  pl.BlockSpec((8, GDN_QKV_W), lambda i: (0, 0)),
                  pl.BlockSpec((8, LANE), lambda i: (0, 0)),
                  hspec, hspec, hspec,
                  pl.BlockSpec((dgc_heads.shape[0], tm, LANE), lambda i: (0, i, 0))],
        out_specs=[pl.BlockSpec((tm, GDN_QKV_W), lambda i: (i, 0)),
                   pl.BlockSpec((tm, LANE), lambda i: (i, 0)),
                   pl.BlockSpec((8, LANE), lambda i: (0, 0))],
        out_shape=[jax.ShapeDtypeStruct((t, GDN_QKV_W), F32), jax.ShapeDtypeStruct((t, LANE), BF16),
                   jax.ShapeDtypeStruct((8, LANE), F32)],
        compiler_params=_arb(1),
    )(proj, proj, proj, conv_w, gate_par, dq, dk, dv, dgc_heads)


def gdn_conv_bwd(proj, conv_w, dy):
    t = proj.shape[0]
    tm = _tile(t, 256)
    nblk = t // tm

    def body(x_ref, halo_ref, w_ref, dy_ref, dyn_ref, dx_ref, dw_ref):
        i = pl.program_id(0)

        @pl.when(i == 0)
        def _():
            dw_ref[...] = jnp.zeros_like(dw_ref)

        w = w_ref[...]
        dy = dy_ref[...]
        ext_dy = jnp.concatenate([dy, jnp.where(i < nblk - 1, dyn_ref[...], 0.0)], axis=0)
        ext_x = jnp.concatenate([jnp.where(i > 0, halo_ref[...], 0.0), x_ref[...]], axis=0)
        dx = w[GDN_CONV - 1:GDN_CONV] * dy
        rows = [jnp.sum(dy * x_ref[...], axis=0, keepdims=True)]
        for j in range(GDN_CONV - 1):
            sh = GDN_CONV - 1 - j
            dx = dx + w[j:j + 1] * _shift_rows(ext_dy, tm + 8 - sh, 0, tm)
            rows.insert(j, jnp.sum(dy * _shift_rows(ext_x, sh, 8, tm), axis=0, keepdims=True))
        dx_ref[...] = dx.astype(BF16)
        row = _iota2((8, GDN_QKV_W), 0)
        acc = jnp.zeros((8, GDN_QKV_W), F32)
        for j in range(GDN_CONV):
            acc = acc + jnp.where(row == j, rows[j], 0.0)
        dw_ref[...] += acc

    return pl.pallas_call(
        body, name="gdn_conv_bwd",
        grid=(nblk,),
        in_specs=[pl.BlockSpec((tm, GDN_QKV_W), lambda i: (i, 0)),
                  pl.BlockSpec((8, GDN_QKV_W), lambda i: (jnp.maximum(i * (tm // 8) - 1, 0), 0)),
                  pl.BlockSpec((8, GDN_QKV_W), lambda i: (0, 0)),
                  pl.BlockSpec((tm, GDN_QKV_W), lambda i: (i, 0)),
                  pl.BlockSpec((8, GDN_QKV_W), lambda i: (jnp.minimum((i + 1) * (tm // 8), t // 8 - 1), 0))],
        out_specs=[pl.BlockSpec((tm, GDN_QKV_W), lambda i: (i, 0)),
                   pl.BlockSpec((8, GDN_QKV_W), lambda i: (0, 0))],
        out_shape=[jax.ShapeDtypeStruct((t, GDN_QKV_W), BF16), jax.ShapeDtypeStruct((8, GDN_QKV_W), F32)],
        compiler_params=_arb(1),
    )(proj, proj, conv_w, dy, dy)


def _resident(w_hbm, w_vmem, sem):
    @pl.when(pl.program_id(0) == 0)
    def _():
        cp = pltpu.make_async_copy(w_hbm, w_vmem, sem)
        cp.start()
        cp.wait()


ANY = pl.BlockSpec(memory_space=pl.ANY)


def norm_proj(h, nw, w, bias):
    t, d = h.shape
    n = w.shape[1]
    tm = _tile(t, 512)
    nc = _tile(n, 1536) if n % 1536 == 0 else _tile(n, 1408)

    def body(h_ref, nw_ref, w_hbm, b_ref, o_ref, w_ref, sem):
        _resident(w_hbm, w_ref, sem)
        hn = _rms(h_ref[...], nw_ref[...])[0].astype(BF16)
        for c0 in range(0, n, nc):
            o_ref[:, c0:c0 + nc] = _dot(hn, w_ref[:, c0:c0 + nc]) + b_ref[:, c0:c0 + nc]

    return pl.pallas_call(
        body, name="norm_proj",
        grid=(t // tm,),
        in_specs=[pl.BlockSpec((tm, d), lambda i: (i, 0)), pl.BlockSpec((1, d), lambda i: (0, 0)), ANY,
                  pl.BlockSpec((1, n), lambda i: (0, 0))],
        out_specs=pl.BlockSpec((tm, n), lambda i: (i, 0)),
        out_shape=jax.ShapeDtypeStruct((t, n), F32),
        scratch_shapes=[pltpu.VMEM((d, n), BF16), pltpu.SemaphoreType.DMA],
        compiler_params=_arb(1),
    )(h, nw, w, bias)


def linear_residual(h, x, w, bias):
    t, d = h.shape
    k = x.shape[1]
    tm = _tile(t, 512)

    def body(h_ref, x_ref, w_hbm, b_ref, o_ref, w_ref, sem):
        _resident(w_hbm, w_ref, sem)
        o_ref[...] = h_ref[...] + _dot(x_ref[...], w_ref[...]) + b_ref[...]

    return pl.pallas_call(
        body, name="linear_residual",
        grid=(t // tm,),
        in_specs=[pl.BlockSpec((tm, d), lambda i: (i, 0)), pl.BlockSpec((tm, k), lambda i: (i, 0)), ANY,
                  pl.BlockSpec((1, d), lambda i: (0, 0))],
        out_specs=pl.BlockSpec((tm, d), lambda i: (i, 0)),
        out_shape=jax.ShapeDtypeStruct((t, d), F32),
        scratch_shapes=[pltpu.VMEM((k, d), BF16), pltpu.SemaphoreType.DMA],
        compiler_params=_arb(1),
    )(h, x, w, bias)


def matmul_nt(dy, w):
    t, d = dy.shape
    k = w.shape[0]
    tm = _tile(t, 512)

    def body(dy_ref, w_hbm, o_ref, cs_ref, w_ref, sem):
        _resident(w_hbm, w_ref, sem)

        @pl.when(pl.program_id(0) == 0)
        def _():
            cs_ref[...] = jnp.zeros_like(cs_ref)

        dy = dy_ref[...]
        cs_ref[...] += jnp.sum(dy, axis=0, keepdims=True)
        o_ref[...] = _dot_nt(dy.astype(BF16), w_ref[...])

    return pl.pallas_call(
        body, name="matmul_nt",
        grid=(t // tm,),
        in_specs=[pl.BlockSpec((tm, d), lambda i: (i, 0)), ANY],
        out_specs=[pl.BlockSpec((tm, k), lambda i: (i, 0)), pl.BlockSpec((1, d), lambda i: (0, 0))],
        out_shape=[jax.ShapeDtypeStruct((t, k), F32), jax.ShapeDtypeStruct((1, d), F32)],
        scratch_shapes=[pltpu.VMEM((k, d), BF16), pltpu.SemaphoreType.DMA],
        compiler_params=_arb(1),
    )(dy, w)


def norm_proj_bwd(h, nw, dh, dps, ws):
    t, d = h.shape
    np_ = len(dps)
    ns = [w.shape[1] for w in ws]
    tm = _tile(t, 512)

    def body(*refs):
        h_ref, nw_ref, dh_ref = refs[:3]
        dp_refs = refs[3:3 + np_]
        w_hbms = refs[3 + np_:3 + 2 * np_]
        o_ref, hn_ref, dnw_ref = refs[3 + 2 * np_:6 + 2 * np_]
        cs_refs = refs[6 + 2 * np_:6 + 3 * np_]
        w_refs = refs[6 + 3 * np_:6 + 4 * np_]
        sem = refs[6 + 4 * np_]
        for a, b in zip(w_hbms, w_refs):
            _resident(a, b, sem)

        @pl.when(pl.program_id(0) == 0)
        def _():
            dnw_ref[...] = jnp.zeros_like(dnw_ref)
            for c in cs_refs:
                c[...] = jnp.zeros_like(c)

        nw = nw_ref[...]
        hn, xhat, r = _rms(h_ref[...], nw)
        hn_ref[...] = hn.astype(BF16)
        dhn = jnp.zeros((tm, d), F32)
        for dp_ref, w_ref, cs_ref in zip(dp_refs, w_refs, cs_refs):
            dp = dp_ref[...]
            cs_ref[...] += jnp.sum(dp.astype(F32), axis=0, keepdims=True)
            dhn = dhn + _dot_nt(dp, w_ref[...])
        dx, dnw = _rms_bwd(dhn, nw, xhat, r)
        dnw_ref[...] += dnw
        o_ref[...] = dh_ref[...] + dx

    row = pl.BlockSpec((tm, d), lambda i: (i, 0))
    vec = pl.BlockSpec((1, d), lambda i: (0, 0))
    return pl.pallas_call(
        body, name="norm_proj_bwd",
        grid=(t // tm,),
        in_specs=[row, vec, row] + [pl.BlockSpec((tm, n), lambda i: (i, 0)) for n in ns] + [ANY] * np_,
        out_specs=[row, row, vec] + [pl.BlockSpec((1, n), lambda i: (0, 0)) for n in ns],
        out_shape=[jax.ShapeDtypeStruct((t, d), F32), jax.ShapeDtypeStruct((t, d), BF16),
                   jax.ShapeDtypeStruct((1, d), F32)] + [jax.ShapeDtypeStruct((1, n), F32) for n in ns],
        scratch_shapes=[pltpu.VMEM((d, n), BF16) for n in ns] + [pltpu.SemaphoreType.DMA],
        compiler_params=_arb(1),
    )(h, nw, dh, *dps, *ws)


def _pair_copies(g_refs, got_refs, send_sems, recv_sems):
    mx, my, mc = _me()
    copies = []
    for g_ref, got_ref in zip(g_refs, got_refs):
        for j in range(4):
            i = len(copies)
            copies.append(pltpu.make_async_remote_copy(
                src_ref=g_ref.at[j, 1 - mc], dst_ref=got_ref.at[j], send_sem=send_sems.at[i], recv_sem=recv_sems.at[i],
                device_id=(mx, my, 1 - mc), device_id_type=MESH))
    return copies


def _scatter_copies(p_refs, q_refs, send_sems, recv_sems):
    mx, my, mc = _me()
    copies = []
    for p_ref, q_ref in zip(p_refs, q_refs):
        for k in (1, 2, 3):
            px, py = _flip(mx, k >> 1), _flip(my, k & 1)
            i = len(copies)
            copies.append(pltpu.make_async_remote_copy(
                src_ref=p_ref.at[2 * px + py], dst_ref=q_ref.at[2 * mx + my], send_sem=send_sems.at[i],
                recv_sem=recv_sems.at[i], device_id=(px, py, mc), device_id_type=MESH))
    return copies


def _share_copies(o_refs, out_refs, send_sems, recv_sems):
    mx, my, mc = _me()
    return [pltpu.make_async_remote_copy(src_ref=o_ref, dst_ref=out_ref, send_sem=send_sems.at[i], recv_sem=recv_sems.at[i],
                                         device_id=(mx, my, 1 - mc), device_id_type=MESH)
            for i, (o_ref, out_ref) in enumerate(zip(o_refs, out_refs))]


CARRIED = {"pair": (_pair_copies, 4, lambda a: a.shape[:1] + a.shape[2:]), "scatter": (_scatter_copies, 3, lambda a: a.shape),
           "share": (_share_copies, 1, lambda a: a.shape)}


def matmul_tn(x, y, scale=1.0, carry=None):
    t, k = x.shape
    n = y.shape[1]
    tk = _tile(k, 1024) if k % 1024 == 0 else _tile(k, 1408)
    tn = n if n <= 1536 else (1024 if n % 1024 == 0 else 1408)
    assert n % tn == 0
    tt = _tile(t, 2048)
    grid = (k // tk, n // tn, t // tt)
    arrays = [] if carry is None else list(carry[1])
    na = len(arrays)

    def body(x_ref, y_ref, *rest):
        o_ref = rest[na]
        if carry is not None:
            make, _, _ = CARRIED[carry[0]]
            copies = lambda: make(rest[:na], rest[na + 1:2 * na + 1], rest[2 * na + 1], rest[2 * na + 2])
            pid = [pl.program_id(a) for a in range(3)]

            @pl.when((pid[0] == 0) & (pid[1] == 0) & (pid[2] == 0))
            def _():
                for cp in copies():
                    cp.start()

        @pl.when(pl.program_id(2) == 0)
        def _():
            o_ref[...] = jnp.zeros_like(o_ref)

        yv = y_ref[...]
        if scale != 1.0:
            yv = yv * scale
        o_ref[...] += _dot_tn(x_ref[...].astype(BF16), yv.astype(BF16))

        if carry is not None:
            @pl.when((pid[0] == grid[0] - 1) & (pid[1] == grid[1] - 1) & (pid[2] == grid[2] - 1))
            def _():
                for cp in copies():
                    cp.wait()

    out_shape = [jax.ShapeDtypeStruct((k, n), F32)]
    scratch = []
    if carry is not None:
        _, per_array, out_of = CARRIED[carry[0]]
        out_shape += [jax.ShapeDtypeStruct(out_of(a), a.dtype) for a in arrays]
        scratch = [pltpu.SemaphoreType.DMA((per_array * na,))] * 2
    res = pl.pallas_call(
        body, name="matmul_tn" if carry is None else "matmul_tn_" + carry[0],
        grid=grid,
        in_specs=[pl.BlockSpec((tt, tk), lambda i, j, s: (s, i)), pl.BlockSpec((tt, tn), lambda i, j, s: (s, j))] + [ANY] * na,
        out_specs=[pl.BlockSpec((tk, tn), lambda i, j, s: (i, j))] + [ANY] * na,
        out_shape=out_shape, scratch_shapes=scratch,
        compiler_params=_arb(3),
    )(x, y, *arrays)
    return res[0] if carry is None else (res[0], list(res[1:]))


FFN_CHUNKS = 2


class _FfnGather:
    def __init__(self, gu_sh, d_sh, gu_full, d_full, send_sems, recv_sems, local_sems):
        self.sh, self.full = (gu_sh, d_sh), (gu_full, d_full)
        self.send_sems, self.recv_sems, self.local_sems = send_sems, recv_sems, local_sems
        self.mx, self.my, self.mc = _me()
        self.me = 2 * self.mx + self.my
        self.w4, self.f4 = gu_sh.shape[1], d_sh.shape[0]
        self.hg, self.hd = gu_sh.shape[0] // 2, d_sh.shape[0] // 2

    def _window(self, a, chip, core=None):
        gu_full, d_full = self.full
        if a == 0:
            rows = pl.ds(0, 2 * self.hg) if core is None else pl.ds(pl.multiple_of(core * self.hg, 16), self.hg)
            return gu_full.at[rows, pl.ds(pl.multiple_of(chip * self.w4, LANE), self.w4)]
        if core is None:
            return d_full.at[pl.ds(pl.multiple_of(chip * self.f4, 16), self.f4), :]
        return d_full.at[pl.ds(pl.multiple_of(chip * self.f4 + core * self.hd, 16), self.hd), :]

    def _half(self, a):
        n = (self.hg, self.hd)[a]
        return self.sh[a].at[pl.ds(pl.multiple_of(self.mc * n, 16), n), :]

    def _peer(self, k):
        return _flip(self.mx, k >> 1), _flip(self.my, k & 1)

    def _fetch(self, k, a):
        px, py = self._peer(k)
        i = 2 * (k - 1) + a
        return pltpu.make_async_remote_copy(src_ref=self._half(a), dst_ref=self._window(a, self.me, self.mc),
                                            send_sem=self.send_sems.at[i], recv_sem=self.recv_sems.at[i],
                                            device_id=(px, py, self.mc), device_id_type=MESH)

    def _relay(self, k, a):
        px, py = self._peer(k)
        got = self._window(a, 2 * px + py, self.mc)
        i = 6 + 2 * (k - 1) + a
        return pltpu.make_async_remote_copy(src_ref=got, dst_ref=got, send_sem=self.send_sems.at[i],
                                            recv_sem=self.recv_sems.at[i], device_id=(self.mx, self.my, 1 - self.mc),
                                            device_id_type=MESH)

    def _own(self, a):
        return pltpu.make_async_copy(self.sh[a], self._window(a, self.me), self.local_sems.at[a])

    def start(self):
        for a in (0, 1):
            self._own(a).start()
            for k in (1, 2, 3):
                self._fetch(k, a).start()

    def relay(self):
        for k in (1, 2, 3):
            for a in (0, 1):
                self._fetch(k, a).wait_recv()
                self._relay(k, a).start()

    def finish(self):
        for k in (1, 2, 3):
            for a in (0, 1):
                self._fetch(k, a).wait_send()
                self._relay(k, a).wait()
        for a in (0, 1):
            self._own(a).wait()


def ffn_fwd(h, nw, wgu, wd, nxt=None, also=()):
    t, d = h.shape
    f = wd.shape[0]
    fc = f // FFN_CHUNKS
    tm = _tile(t, 512)
    nsteps = t // tm
    also = list(also)
    na = len(also)

    def body(h_ref, nw_ref, wgu_hbm, wd_hbm, *rest):
        gathers = []
        if nxt is None:
            o_ref, gu_ref, a_ref, wgu_ref, wd_ref, sem = rest
        else:
            gu_sh, d_sh = rest[:2]
            o_ref, gu_ref, a_ref, gu_full, d_full = rest[2 + na:7 + na]
            wgu_ref, wd_ref, sem, send_sems, recv_sems, local_sems = rest[7 + 2 * na:13 + 2 * na]
            gathers.append(_FfnGather(gu_sh, d_sh, gu_full, d_full, send_sems, recv_sems, local_sems))
            if na:
                gathers.append(_SplitGather(rest[2:2 + na], rest[7 + na:7 + 2 * na], *rest[13 + 2 * na:]))

        @pl.when(pl.program_id(0) == 0)
        def _():
            for gather in gathers:
                gather.start()

        _resident(wgu_hbm, wgu_ref, sem)
        _resident(wd_hbm, wd_ref, sem)

        @pl.when(pl.program_id(0) == (3 * nsteps) // 4)
        def _():
            for gather in gathers:
                gather.relay()

        x = h_ref[...]
        hn = _rms(x, nw_ref[...])[0].astype(BF16)
        acc = jnp.zeros((tm, d), F32)
        for c in range(FFN_CHUNKS):
            gs, us = slice(c * fc, (c + 1) * fc), slice(f + c * fc, f + (c + 1) * fc)
            g = _dot(hn, wgu_ref[:, gs])
            u = _dot(hn, wgu_ref[:, us])
            a = (_silu(g) * u).astype(BF16)
            gu_ref[:, gs] = g.astype(BF16)
            gu_ref[:, us] = u.astype(BF16)
            a_ref[:, gs] = a
            acc = acc + _dot(a, wd_ref[gs, :])
        o_ref[...] = x + 0.5 * acc

        @pl.when(pl.program_id(0) == nsteps - 1)
        def _():
            for gather in gathers:
                gather.finish()

    row = lambda w: pl.BlockSpec((tm, w), lambda i: (i, 0))
    in_specs = [row(d), pl.BlockSpec((1, d), lambda i: (0, 0)), ANY, ANY]
    out_specs = [row(d), row(2 * f), row(f)]
    out_shape = [jax.ShapeDtypeStruct((t, d), F32), jax.ShapeDtypeStruct((t, 2 * f), BF16),
                 jax.ShapeDtypeStruct((t, f), BF16)]
    scratch = [pltpu.VMEM((d, 2 * f), BF16), pltpu.VMEM((f, d), BF16), pltpu.SemaphoreType.DMA]
    args = (h, nw, wgu, wd)
    if nxt is not None:
        in_specs += [ANY] * (2 + na)
        out_specs += [ANY] * (2 + na)
        out_shape += [jax.ShapeDtypeStruct((d, 2 * f), BF16), jax.ShapeDtypeStruct((f, d), BF16)]
        out_shape += [jax.ShapeDtypeStruct((4,) + a.shape, a.dtype) for a in also]
        scratch += [pltpu.SemaphoreType.DMA((12,)), pltpu.SemaphoreType.DMA((12,)), pltpu.SemaphoreType.DMA((2,))]
        if na:
            scratch += [pltpu.SemaphoreType.DMA((6 * na,)), pltpu.SemaphoreType.DMA((6 * na,))]
        args += tuple(nxt) + tuple(also)
    return pl.pallas_call(
        body, name="ffn_fwd_gather" if nxt is not None else "ffn_fwd",
        grid=(nsteps,),
        in_specs=in_specs, out_specs=out_specs, out_shape=out_shape, scratch_shapes=scratch,
        compiler_params=_arb(1),
    )(*args)


def ffn_bwd(h, nw, wgu, wd, gu, dh):
    t, d = h.shape
    f = wd.shape[0]
    fc = f // FFN_CHUNKS
    tm = _tile(t, 256)

    def body(h_ref, nw_ref, wgu_hbm, wd_hbm, gu_ref, dh_ref, o_ref, hn_ref, dgu_ref, dnw_ref, wgu_ref, wd_ref, sem):
        _resident(wgu_hbm, wgu_ref, sem)
        _resident(wd_hbm, wd_ref, sem)

        @pl.when(pl.program_id(0) == 0)
        def _():
            dnw_ref[...] = jnp.zeros_like(dnw_ref)

        nw = nw_ref[...]
        hn32, xhat, r = _rms(h_ref[...], nw)
        hn_ref[...] = hn32.astype(BF16)
        dh = dh_ref[...]
        dout = (0.5 * dh).astype(BF16)
        dhn = jnp.zeros((tm, d), F32)
        for c in range(FFN_CHUNKS):
            gs, us = slice(c * fc, (c + 1) * fc), slice(f + c * fc, f + (c + 1) * fc)
            g = gu_ref[:, gs].astype(F32)
            u = gu_ref[:, us].astype(F32)
            sg = _sigmoid(g)
            sil = g * sg
            da = _dot_nt(dout, wd_ref[gs, :])
            dg = (da * u * (sg * (1.0 + g * (1.0 - sg)))).astype(BF16)
            du = (da * sil).astype(BF16)
            dgu_ref[:, gs] = dg
            dgu_ref[:, us] = du
            dhn = dhn + _dot_nt(dg, wgu_ref[:, gs]) + _dot_nt(du, wgu_ref[:, us])
        dx, dnw = _rms_bwd(dhn, nw, xhat, r)
        dnw_ref[...] += dnw
        o_ref[...] = dh + dx

    row = pl.BlockSpec((tm, d), lambda i: (i, 0))
    vec = pl.BlockSpec((1, d), lambda i: (0, 0))
    return pl.pallas_call(
        body, name="ffn_bwd",
        grid=(t // tm,),
        in_specs=[row, vec, ANY, ANY, pl.BlockSpec((tm, 2 * f), lambda i: (i, 0)), row],
        out_specs=[row, row, pl.BlockSpec((tm, 2 * f), lambda i: (i, 0)), vec],
        out_shape=[jax.ShapeDtypeStruct((t, d), F32), jax.ShapeDtypeStruct((t, d), BF16),
                   jax.ShapeDtypeStruct((t, 2 * f), BF16), jax.ShapeDtypeStruct((1, d), F32)],
        scratch_shapes=[pltpu.VMEM((d, 2 * f), BF16), pltpu.VMEM((f, d), BF16), pltpu.SemaphoreType.DMA],
        compiler_params=_arb(1),
    )(h, nw, wgu, wd, gu, dh)


def loss_head(h, nw, target):
    t, d = h.shape
    tm = _tile(t, 512)

    def body(h_ref, nw_ref, tg_ref, dh_ref, loss_ref, dnw_ref):
        @pl.when(pl.program_id(0) == 0)
        def _():
            loss_ref[...] = jnp.zeros_like(loss_ref)
            dnw_ref[...] = jnp.zeros_like(dnw_ref)

        nw = nw_ref[...]
        y, xhat, r = _rms(h_ref[...], nw)
        e = y - tg_ref[...]
        loss_ref[...] += 0.5 * jnp.sum(jnp.mean(e * e, axis=-1, keepdims=True), axis=0, keepdims=True)
        dx, dnw = _rms_bwd(e * (1.0 / d), nw, xhat, r)
        dnw_ref[...] += dnw
        dh_ref[...] = dx

    row = pl.BlockSpec((tm, d), lambda i: (i, 0))
    vec = pl.BlockSpec((1, d), lambda i: (0, 0))
    return pl.pallas_call(
        body, name="loss_head",
        grid=(t // tm,),
        in_specs=[row, vec, row],
        out_specs=[row, pl.BlockSpec((8, LANE), lambda i: (0, 0)), vec],
        out_shape=[jax.ShapeDtypeStruct((t, d), F32), jax.ShapeDtypeStruct((8, LANE), F32),
                   jax.ShapeDtypeStruct((1, d), F32)],
        compiler_params=_arb(1),
    )(h, nw, target)


def adamw(w, g, m, v):
    r, c = w.shape
    tr = r
    while tr * c * 4 > (1 << 20) and tr % 16 == 0:
        tr //= 2

    def body(w_ref, g_ref, m_ref, v_ref, d_ref, nm_ref, nv_ref):
        g = g_ref[...]
        m = ADAM_B1 * m_ref[...] + (1.0 - ADAM_B1) * g
        v = ADAM_B2 * v_ref[...] + (1.0 - ADAM_B2) * (g * g)
        m_hat = m / (1.0 - ADAM_B1 ** ADAM_STEP)
        v_hat = v / (1.0 - ADAM_B2 ** ADAM_STEP)
        d_ref[...] = -ADAM_LR * (m_hat / (jnp.sqrt(v_hat) + ADAM_EPS) + ADAM_WD * w_ref[...])
        nm_ref[...] = m
        nv_ref[...] = v

    blk = pl.BlockSpec((tr, c), lambda i: (i, 0))
    return pl.pallas_call(
        body, name="adamw",
        grid=(r // tr,),
        in_specs=[blk] * 4, out_specs=[blk] * 3,
        out_shape=[jax.ShapeDtypeStruct((r, c), F32)] * 3,
        compiler_params=_arb(1),
    )(w, g, m, v)


ATTN_P_W = ATTN_Q_W + 2 * ATTN_KV_W
ATTN_SCALE = ATTN_HEAD_DIM ** -0.5


def _rope_group(t, tab, sign):
    return (t * tab[:, 0:LANE] + sign * pltpu.roll(t, 8, 1) * tab[:, LANE:2 * LANE]
            + sign * pltpu.roll(t, LANE - 8, 1) * tab[:, 2 * LANE:3 * LANE])


def _rope(t, tab, sign=1.0):
    return jnp.concatenate([_rope_group(t[:, s:s + LANE], tab, sign) for s in range(0, t.shape[1], LANE)], axis=1)


def _attn_heads(qs, ks, vs, sinks, first):
    b = ATTN_BLOCK
    rows = ATTN_GROUP * b
    qi = _iota2((rows, 2 * b), 0) % b
    kj = _iota2((rows, 2 * b), 1)
    rel = qi + b - kj
    valid = (rel >= 0) & (rel < b) & ((kj >= b) | jnp.logical_not(first))
    ss = _each(lambda q, k: jnp.where(valid, _mm1_nt(q, k) * ATTN_SCALE, NEG), qs, ks)
    ms = _each(lambda s, sink: lax.stop_gradient(jnp.maximum(jnp.max(s, axis=-1, keepdims=True), sink)), ss, sinks)
    ps = _each(lambda s, m: jnp.exp(s - m), ss, ms)
    dens = _each(lambda p, sink, m: jnp.sum(p, axis=-1, keepdims=True) + jnp.exp(sink - m), ps, sinks, ms)
    return _each(lambda p, den, v: _mm1(p / den, v), ps, dens, vs)


def _attn_prepare(p_ref, kvp_ref, tab_ref, tabp_ref, sink_ref):
    b = ATTN_BLOCK
    hd = ATTN_HEAD_DIM
    tab, tabp = tab_ref[...], tabp_ref[...]
    q = _rope(p_ref[:, 0:ATTN_Q_W], tab)
    kc = _rope(p_ref[:, ATTN_Q_W:ATTN_Q_W + ATTN_KV_W], tab)
    kp = _rope(kvp_ref[:, 0:ATTN_KV_W], tabp)
    vc = p_ref[:, ATTN_Q_W + ATTN_KV_W:ATTN_P_W]
    vp = kvp_ref[:, ATTN_KV_W:2 * ATTN_KV_W]
    sk = sink_ref[...]
    qs, ks, vs, sinks = [], [], [], []
    for h in range(ATTN_KV_HEADS):
        heads = [ATTN_GROUP * h + g for g in range(ATTN_GROUP)]
        qs.append(jnp.concatenate([q[:, i * hd:(i + 1) * hd] for i in heads], axis=0))
        ks.append(jnp.concatenate([kp[:, h * hd:(h + 1) * hd], kc[:, h * hd:(h + 1) * hd]], axis=0))
        vs.append(jnp.concatenate([vp[:, h * hd:(h + 1) * hd], vc[:, h * hd:(h + 1) * hd]], axis=0))
        sinks.append(jnp.concatenate([jnp.broadcast_to(sk[:, i:i + 1], (b, 1)) for i in heads], axis=0))
    return qs, ks, vs, sinks


def _unstack_heads(xs):
    b = ATTN_BLOCK
    return jnp.concatenate([x[g * b:(g + 1) * b] for x in xs for g in range(ATTN_GROUP)], axis=1)


def _attn_specs(nb):
    b = ATTN_BLOCK
    prev = lambda n: jnp.maximum(n - 1, 0)
    return [pl.BlockSpec((b, ATTN_P_W), lambda n: (n, 0)),
            pl.BlockSpec((b, 2 * ATTN_KV_W), lambda n: (prev(n), ATTN_Q_W // (2 * ATTN_KV_W))),
            pl.BlockSpec((b, 3 * LANE), lambda n: (n, 0)),
            pl.BlockSpec((b, 3 * LANE), lambda n: (prev(n), 0)),
            pl.BlockSpec((1, ATTN_Q_HEADS), lambda n: (0, 0))]


def attn_fwd(p, tab, sinks):
    t = p.shape[0]
    nb = t // ATTN_BLOCK

    def body(p_ref, kvp_ref, tab_ref, tabp_ref, sink_ref, o_ref):
        qs, ks, vs, sk = _attn_prepare(p_ref, kvp_ref, tab_ref, tabp_ref, sink_ref)
        os_ = _attn_heads(qs, ks, vs, sk, pl.program_id(0) == 0)
        o_ref[...] = _unstack_heads(os_).astype(BF16)

    return pl.pallas_call(
        body, name="attn_fwd",
        grid=(nb,),
        in_specs=_attn_specs(nb),
        out_specs=pl.BlockSpec((ATTN_BLOCK, ATTN_Q_W), lambda n: (n, 0)),
        out_shape=jax.ShapeDtypeStruct((t, ATTN_Q_W), BF16),
        compiler_params=_arb(1),
    )(p, p, tab, tab, sinks)


def attn_bwd(p, tab, sinks, do):
    t = p.shape[0]
    b = ATTN_BLOCK
    hd = ATTN_HEAD_DIM
    nb = t // b

    def body(p_ref, kvp_ref, tab_ref, tabp_ref, sink_ref, do_ref, dq_ref, dkvc_ref, dkvp_ref, dsink_ref):
        n = pl.program_id(0)

        @pl.when(n == 0)
        def _():
            dsink_ref[...] = jnp.zeros_like(dsink_ref)

        qs, ks, vs, sk = _attn_prepare(p_ref, kvp_ref, tab_ref, tabp_ref, sink_ref)
        first = n == 0
        _, vjp = jax.vjp(lambda a, bb, c, d: _attn_heads(a, bb, c, d, first), qs, ks, vs, sk)
        do = do_ref[...]
        dos = [jnp.concatenate([do[:, i * hd:(i + 1) * hd] for i in range(ATTN_GROUP * h, ATTN_GROUP * (h + 1))], axis=0)
               for h in range(ATTN_KV_HEADS)]
        dqs, dks, dvs, dsk = vjp(dos)
        tab, tabp = tab_ref[...], tabp_ref[...]
        dq_ref[...] = _rope(_unstack_heads([x.astype(F32) for x in dqs]), tab, -1.0).astype(BF16)
        dkc = jnp.concatenate([x.astype(F32)[b:] for x in dks], axis=1)
        dkp = jnp.concatenate([x.astype(F32)[:b] for x in dks], axis=1)
        dkvc_ref[:, 0:ATTN_KV_W] = _rope(dkc, tab, -1.0)
        dkvp_ref[:, 0:ATTN_KV_W] = _rope(dkp, tabp, -1.0)
        dkvc_ref[:, ATTN_KV_W:] = jnp.concatenate([x.astype(F32)[b:] for x in dvs], axis=1)
        dkvp_ref[:, ATTN_KV_W:] = jnp.concatenate([x.astype(F32)[:b] for x in dvs], axis=1)
        parts = [jnp.sum(d[g * b:(g + 1) * b], axis=0, keepdims=True) for d in dsk for g in range(ATTN_GROUP)]
        dsink_ref[...] += jnp.concatenate(parts, axis=1)

    blk = lambda w: pl.BlockSpec((b, w), lambda n: (n, 0))
    return pl.pallas_call(
        body, name="attn_bwd",
        grid=(nb,),
        in_specs=_attn_specs(nb) + [blk(ATTN_Q_W)],
        out_specs=[blk(ATTN_Q_W), blk(2 * ATTN_KV_W), blk(2 * ATTN_KV_W),
                   pl.BlockSpec((1, ATTN_Q_HEADS), lambda n: (0, 0))],
        out_shape=[jax.ShapeDtypeStruct((t, ATTN_Q_W), BF16), jax.ShapeDtypeStruct((t, 2 * ATTN_KV_W), F32),
                   jax.ShapeDtypeStruct((t, 2 * ATTN_KV_W), F32), jax.ShapeDtypeStruct((1, ATTN_Q_HEADS), F32)],
        compiler_params=_arb(1),
    )(p, p, tab, tab, sinks, do)


def kv_combine(dkvc, dkvp):
    t, w = dkvc.shape
    b = ATTN_BLOCK
    tm = _tile(t, 4 * b)
    nt = t // tm

    def body(c_ref, p_ref, pn_ref, o_ref):
        nxt = jnp.where(pl.program_id(0) < nt - 1, pn_ref[...], 0.0)
        shifted = jnp.concatenate([p_ref[b:, :], nxt], axis=0) if tm > b else nxt
        o_ref[...] = (c_ref[...] + shifted).astype(BF16)

    row = pl.BlockSpec((tm, w), lambda n: (n, 0))
    return pl.pallas_call(
        body, name="kv_combine",
        grid=(nt,),
        in_specs=[row, row, pl.BlockSpec((b, w), lambda n: (jnp.minimum((n + 1) * (tm // b), t // b - 1), 0))],
        out_specs=row,
        out_shape=jax.ShapeDtypeStruct((t, w), BF16),
        compiler_params=_arb(1),
    )(dkvc, dkvp, dkvp)


def _me():
    return lax.axis_index("x"), lax.axis_index("y"), lax.axis_index("c")


def _flip(v, bit):
    return 1 - v if bit else v


def _chip_index():
    return 2 * lax.axis_index("x") + lax.axis_index("y")


class _SplitGather:
    def __init__(self, x_refs, o_refs, send_sems, recv_sems):
        self.x_refs, self.o_refs, self.send_sems, self.recv_sems = x_refs, o_refs, send_sems, recv_sems
        self.mx, self.my, self.mc = _me()
        self.me = 2 * self.mx + self.my

    def _rows(self, a):
        rh = self.x_refs[a].shape[0] // 2
        return pl.ds(pl.multiple_of(self.mc * rh, 16), rh)

    def _fetch(self, a, k):
        rows, i = self._rows(a), 6 * a + k - 1
        return pltpu.make_async_remote_copy(
            src_ref=self.x_refs[a].at[rows], dst_ref=self.o_refs[a].at[self.me, rows], send_sem=self.send_sems.at[i],
            recv_sem=self.recv_sems.at[i], device_id=(_flip(self.mx, k >> 1), _flip(self.my, k & 1), self.mc),
            device_id_type=MESH)

    def _relay(self, a, k):
        theirs = self.o_refs[a].at[2 * _flip(self.mx, k >> 1) + _flip(self.my, k & 1), self._rows(a)]
        i = 6 * a + 2 + k
        return pltpu.make_async_remote_copy(src_ref=theirs, dst_ref=theirs, send_sem=self.send_sems.at[i],
                                            recv_sem=self.recv_sems.at[i], device_id=(self.mx, self.my, 1 - self.mc),
                                            device_id_type=MESH)

    def _each_copy(self):
        return [(a, k) for a in range(len(self.x_refs)) for k in (1, 2, 3)]

    def start(self):
        for a, k in self._each_copy():
            self._fetch(a, k).start()

    def relay(self):
        for a, k in self._each_copy():
            self._fetch(a, k).wait_recv()
            self._relay(a, k).start()

    def finish(self):
        for a, k in self._each_copy():
            self._fetch(a, k).wait_send()
            self._relay(a, k).wait()


def _fill_own_slot(got, x):
    return lax.dynamic_update_slice(got, x[None], (_chip_index(), 0, 0))


def allgather_chips(x, split):
    r, c = x.shape

    def body(x_ref, o_ref, send_sems, recv_sems):
        if split:
            gather = _SplitGather([x_ref], [o_ref], send_sems, recv_sems)
            gather.start()
            gather.relay()
            gather.finish()
            return
        mx, my, mc = _me()
        fetched = []
        for k in (1, 2, 3):
            cp = pltpu.make_async_remote_copy(src_ref=x_ref, dst_ref=o_ref.at[2 * mx + my],
                                              send_sem=send_sems.at[k - 1], recv_sem=recv_sems.at[k - 1],
                                              device_id=(_flip(mx, k >> 1), _flip(my, k & 1), mc), device_id_type=MESH)
            cp.start()
            fetched.append(cp)
        for cp in fetched:
            cp.wait()

    got = pl.pallas_call(
        body, name="allgather_chips",
        in_specs=[ANY], out_specs=ANY,
        out_shape=jax.ShapeDtypeStruct((4, r, c), x.dtype),
        scratch_shapes=[pltpu.SemaphoreType.DMA((6,)), pltpu.SemaphoreType.DMA((6,))],
    )(x)
    return _fill_own_slot(got, x)


def exchange(kind, arrays):
    make, per_array, out_of = CARRIED[kind]
    na = len(arrays)

    def body(*refs):
        copies = make(refs[:na], refs[na:2 * na], refs[2 * na], refs[2 * na + 1])
        for cp in copies:
            cp.start()
        for cp in copies:
            cp.wait()

    return pl.pallas_call(
        body, name=kind + "_exchange",
        in_specs=[ANY] * na, out_specs=[ANY] * na,
        out_shape=[jax.ShapeDtypeStruct(out_of(a), a.dtype) for a in arrays],
        scratch_shapes=[pltpu.SemaphoreType.DMA((per_array * na,))] * 2,
    )(*arrays)


def _row_tile(r, c):
    return max(d for d in range(16, r + 1, 16) if r % d == 0 and d * c * 4 <= (2 << 20))


def add_pair(g, got, core):
    _, _, r, c = g.shape
    tr = _row_tile(r, c)

    def body(core_ref, g_ref, got_ref, p32_ref, p16_ref):
        s = g_ref[...] + got_ref[...]
        p32_ref[...] = s
        p16_ref[...] = s.astype(BF16)

    blk = pl.BlockSpec((None, tr, c), lambda j, i, core_ref: (j, i, 0))
    return pl.pallas_call(
        body, name="add_pair",
        grid_spec=pltpu.PrefetchScalarGridSpec(
            num_scalar_prefetch=1, grid=(4, r // tr),
            in_specs=[pl.BlockSpec((None, None, tr, c), lambda j, i, core_ref: (j, core_ref[0], i, 0)), blk],
            out_specs=[blk, blk]),
        out_shape=[jax.ShapeDtypeStruct((4, r, c), F32), jax.ShapeDtypeStruct((4, r, c), BF16)],
        compiler_params=_arb(2),
    )(core, g, got)


def sum_slots(p32, q16, order):
    _, r, c = p32.shape
    tr = _row_tile(r, c)

    def body(order_ref, own_ref, a_ref, b_ref, c_ref, o_ref):
        o_ref[...] = ((own_ref[...] + a_ref[...].astype(F32)) + b_ref[...].astype(F32)) + c_ref[...].astype(F32)

    slot = lambda k: pl.BlockSpec((None, tr, c), functools.partial(lambda k, i, order_ref: (order_ref[k], i, 0), k))
    return pl.pallas_call(
        body, name="sum_slots",
        grid_spec=pltpu.PrefetchScalarGridSpec(
            num_scalar_prefetch=1, grid=(r // tr,),
            in_specs=[slot(0), slot(1), slot(2), slot(3)],
            out_specs=pl.BlockSpec((tr, c), lambda i, order_ref: (i, 0))),
        out_shape=jax.ShapeDtypeStruct((r, c), F32),
        compiler_params=_arb(1),
    )(order, p32, q16, q16, q16)


def allreduce_small(x):
    r, c = x.shape

    def body(x_ref, o_ref, buf, send_sems, recv_sems):
        mx, my, mc = _me()
        me = 4 * mx + 2 * my + mc
        buf[pl.ds(me, 1)] = x_ref[...][None]
        copies = []
        for k in range(1, 8):
            peer = (_flip(mx, k >> 2), _flip(my, (k >> 1) & 1), _flip(mc, k & 1))
            cp = pltpu.make_async_remote_copy(src_ref=x_ref, dst_ref=buf.at[me], send_sem=send_sems.at[k - 1],
                                              recv_sem=recv_sems.at[k - 1], device_id=peer, device_id_type=MESH)
            cp.start()
            copies.append(cp)
        for cp in copies:
            cp.wait()
        acc = buf[0]
        for d in range(1, 8):
            acc = acc + buf[d]
        o_ref[...] = acc

    return pl.pallas_call(
        body, name="allreduce_small",
        out_shape=jax.ShapeDtypeStruct((r, c), F32),
        scratch_shapes=[pltpu.VMEM((8, r, c), F32), pltpu.SemaphoreType.DMA((7,)), pltpu.SemaphoreType.DMA((7,))],
    )(x)


class GradReduce:
    def __init__(self, grads, shard_shapes):
        self.shapes = shard_shapes
        self.groups = _by_width(shard_shapes)
        self.core = lax.axis_index("c")
        me = _chip_index()
        self.order = jnp.stack([me, me ^ 1, me ^ 2, me ^ 3]).astype(jnp.int32)
        self.g = []
        for names in self.groups.values():
            slots = []
            for j in range(4):
                parts = []
                for n in names:
                    w = shard_shapes[n][1 + SHARDED[n]]
                    parts += [lax.slice_in_dim(gl, j * w, (j + 1) * w, axis=SHARDED[n]) for gl in grads[n]]
                slots.append(jnp.concatenate(parts, axis=0))
            g = jnp.stack(slots)
            self.g.append(g.reshape(4, 2, g.shape[1] // 2, g.shape[2]))
        self.p32 = self.p16 = None
        self.q16 = [None] * len(self.g)
        n = len(self.g)
        self.pieces = [list(range(2, n)), [1], [0]] if n > 2 else [[i] for i in reversed(range(n))]
        self.pieces = [p for p in self.pieces if p]
        self.stage = 0

    def next_job(self):
        if self.stage == 0:
            return "pair", self.g
        if self.stage <= len(self.pieces):
            return "scatter", [self.p16[i] for i in self.pieces[self.stage - 1]]
        return None

    def deliver(self, res):
        if self.stage == 0:
            core = self.core.astype(jnp.int32)[None]
            pairs = [add_pair(g, got, core) for g, got in zip(self.g, res)]
            self.p32, self.p16 = [p[0] for p in pairs], [p[1] for p in pairs]
        else:
            for i, q in zip(self.pieces[self.stage - 1], res):
                self.q16[i] = q
        self.stage += 1

    def finish(self):
        while (job := self.next_job()) is not None:
            self.deliver(exchange(*job))
        mine = [sum_slots(p, q, self.order) for p, q in zip(self.p32, self.q16)]
        theirs = exchange("share", mine)
        out = {}
        for names, m, t in zip(self.groups.values(), mine, theirs):
            red = jnp.where(self.core == 0, jnp.concatenate([m, t]), jnp.concatenate([t, m]))
            off = 0
            for n in names:
                layers, a, _ = self.shapes[n]
                out[n] = red[off:off + layers * a].reshape(self.shapes[n])
                off += layers * a
        return out


SHARDED = {"ffn1_w_gate_up": 1, "ffn1_w_down": 0, "ffn2_w_gate_up": 1, "ffn2_w_down": 0, "attn_w_in": 1,
           "attn_w_out": 0, "gdn_w_in": 1, "gdn_w_out": 0, "gdn_conv_w": 1}
REPLICATED = ["ffn1_norm", "mix_norm", "ffn2_norm", "attn_b_in", "attn_sinks", "attn_b_out", "gdn_A_log",
              "gdn_dt_bias", "gdn_norm_w", "final_norm"]
WEIGHTS = ["ffn1_norm", "ffn1_w_gate_up", "ffn1_w_down", "mix_norm", "ffn2_norm", "ffn2_w_gate_up", "ffn2_w_down",
           "attn_w_in", "attn_b_in", "attn_sinks", "attn_w_out", "attn_b_out", "gdn_w_in", "gdn_conv_w", "gdn_A_log",
           "gdn_dt_bias", "gdn_norm_w", "gdn_w_out", "final_norm"]


def _pack_rows(flats, dtype, width, row_multiple):
    flat = jnp.concatenate([f.astype(dtype).reshape(-1) for f in flats])
    per = width * row_multiple
    pad = (-flat.shape[0]) % per
    if pad:
        flat = jnp.concatenate([flat, jnp.zeros((pad,), dtype)])
    return flat.reshape(-1, width)


def _by_width(shapes):
    groups = {}
    for n, shape in shapes.items():
        groups.setdefault(shape[-1], []).append(n)
    return groups


def _pack_groups(shards, dtype):
    groups = _by_width({n: s.shape for n, s in shards.items()})
    packed = [jnp.concatenate([shards[n].astype(dtype).reshape(-1, width) for n in names], axis=0)
              for width, names in groups.items()]

    def unpack(gots):
        full = {}
        for names, got in zip(groups.values(), gots):
            off = 0
            for n in names:
                layers, a, _ = shards[n].shape
                full[n] = [jnp.concatenate([got[j, off + l * a:off + (l + 1) * a] for j in range(4)], axis=SHARDED[n])
                           for l in range(layers)]
                off += layers * a
        return full

    return packed, unpack


def _gather_weights(shards, dtype, split=True):
    packed, unpack = _pack_groups(shards, dtype)
    return unpack([allgather_chips(p, split) for p in packed])


def _small_pack(items):
    rows = []
    for a in items:
        f = a.astype(F32).reshape(-1)
        pad = (-f.shape[0]) % LANE
        rows.append(jnp.concatenate([f, jnp.zeros((pad,), F32)]) if pad else f)
    return _pack_rows(rows, F32, LANE, 8)


def _small_unpack(buf, shapes):
    flat = buf.reshape(-1)
    out, off = [], 0
    for shape in shapes:
        size = 1
        for s in shape:
            size *= s
        out.append(flat[off:off + size].reshape(shape))
        off += size + (-size) % LANE
    return out


def _rope_table(positions):
    t = positions.shape[0]
    inv_freq = ROPE_THETA ** (-jnp.arange(0, ROPE_DIM, 2, dtype=F32) / ROPE_DIM)
    ang = positions.astype(F32)[:, None] * inv_freq
    cos, sin = jnp.cos(ang), jnp.sin(ang)
    rest = ATTN_HEAD_DIM - ROPE_DIM
    zeros = lambda n: jnp.zeros((t, n), F32)
    c64 = jnp.concatenate([cos, cos, jnp.ones((t, rest), F32)], axis=1)
    s_up = jnp.concatenate([zeros(ROPE_DIM // 2), sin, zeros(rest)], axis=1)
    s_dn = jnp.concatenate([-sin, zeros(ROPE_DIM // 2 + rest)], axis=1)
    return jnp.concatenate([c64, c64, s_up, s_up, s_dn, s_dn], axis=1)


def _as2d(a):
    return a.reshape(-1, a.shape[-1]) if a.ndim > 1 else a.reshape(1, -1)


def kernel(x, positions, ffn1_norm, ffn1_w_gate_up, ffn1_w_down, mix_norm, ffn2_norm, ffn2_w_gate_up, ffn2_w_down, attn_w_in, attn_b_in, attn_sinks, attn_w_out, attn_b_out, gdn_w_in, gdn_conv_w, gdn_A_log, gdn_dt_bias, gdn_norm_w, gdn_w_out, final_norm, loss_target, m_ffn1_norm, m_ffn1_w_gate_up, m_ffn1_w_down, m_mix_norm, m_ffn2_norm, m_ffn2_w_gate_up, m_ffn2_w_down, m_attn_w_in, m_attn_b_in, m_attn_sinks, m_attn_w_out, m_attn_b_out, m_gdn_w_in, m_gdn_conv_w, m_gdn_A_log, m_gdn_dt_bias, m_gdn_norm_w, m_gdn_w_out, m_final_norm, v_ffn1_norm, v_ffn1_w_gate_up, v_ffn1_w_down, v_mix_norm, v_ffn2_norm, v_ffn2_w_gate_up, v_ffn2_w_down, v_attn_w_in, v_attn_b_in, v_attn_sinks, v_attn_w_out, v_attn_b_out, v_gdn_w_in, v_gdn_conv_w, v_gdn_A_log, v_gdn_dt_bias, v_gdn_norm_w, v_gdn_w_out, v_final_norm):
    given = dict(locals())
    w = {n: given[n] for n in WEIGHTS}
    d = D_MODEL
    h = x[0]
    target = loss_target[0]
    depth = ffn1_norm.shape[0]

    big = [n for n in SHARDED if n != "gdn_conv_w"]
    first = _gather_weights({n: w[n][0:1] for n in ("ffn1_w_gate_up", "ffn1_w_down")}, BF16)
    ffn_order = [(tag, l) for l in range(depth) for tag in ("ffn1", "ffn2")]
    ffn_w = {ffn_order[0]: (first["ffn1_w_gate_up"][0], first["ffn1_w_down"][0])}
    mixers_packed, mixers_unpack = _pack_groups({n: w[n] for n in big if not n.startswith("ffn")}, BF16)
    full = {}

    def run_ffn(tag, l, h):
        i = ffn_order.index((tag, l))
        wgu, wd = ffn_w[(tag, l)]
        nw = w[tag + "_norm"][l][None]
        if i + 1 == len(ffn_order):
            return ffn_fwd(h, nw, wgu, wd)
        ntag, nl = ffn_order[i + 1]
        shards = (w[ntag + "_w_gate_up"][nl].astype(BF16), w[ntag + "_w_down"][nl].astype(BF16))
        h, gu, a, ngu, nd, *gots = ffn_fwd(h, nw, wgu, wd, shards, also=mixers_packed if i == 0 else ())
        ffn_w[(ntag, nl)] = (ngu, nd)
        if i == 0:
            full.update(mixers_unpack([_fill_own_slot(got, x) for got, x in zip(gots, mixers_packed)]))
        return h, gu, a

    conv_full = _gather_weights({"gdn_conv_w": gdn_conv_w}, F32, split=False)["gdn_conv_w"]
    tab = _rope_table(positions[0])
    zero_d = jnp.zeros((1, d), F32)

    def gdn_params(j):
        w_in = full["gdn_w_in"][j]
        w_cat = jnp.concatenate([w_in, jnp.zeros((d, GDN_PROJ_W - w_in.shape[1]), BF16)], axis=1)
        conv = jnp.concatenate([conv_full[j], jnp.zeros((8 - GDN_CONV, GDN_QKV_W), F32)], axis=0)
        lanes = lambda vec: jnp.concatenate([jnp.zeros((GC_G,), F32), vec, jnp.zeros((LANE - GC_G - GDN_HEADS,), F32)])
        par = jnp.concatenate([lanes(gdn_A_log[j])[None], lanes(gdn_dt_bias[j])[None], jnp.zeros((6, LANE), F32)], axis=0)
        return w_cat, conv, par

    saved = []
    for l in range(depth):
        j = l // 2
        rec = {"h1": h}
        h, rec["ffn1_gu"], rec["ffn1_a"] = run_ffn("ffn1", l, h)
        rec["h2"] = h
        if l % 2 == 0:
            p = norm_proj(h, mix_norm[l][None], full["attn_w_in"][j], attn_b_in[j][None])
            o = attn_fwd(p, tab, attn_sinks[j][None])
            h = linear_residual(h, o, full["attn_w_out"][j], attn_b_out[j][None])
            rec.update(p=p, o=o)
        else:
            w_cat, conv, par = gdn_params(j)
            proj = norm_proj(h, mix_norm[l][None], w_cat, jnp.zeros((1, GDN_PROJ_W), F32))
            qkv, gc = gdn_pre_fwd(proj, conv, par)
            on, st, rec["m_offs"] = gdn_chunk_fwd(qkv, gc, proj, gdn_norm_w[j][None])
            h = linear_residual(h, on, full["gdn_w_out"][j], zero_d)
            rec.update(proj=proj, qkv=qkv, gc=gc, on=on, st=st, w_cat=w_cat, conv=conv, par=par)
        rec["h3"] = h
        h, rec["ffn2_gu"], rec["ffn2_a"] = run_ffn("ffn2", l, h)
        saved.append(rec)

    dh, loss_tile, d_final = loss_head(h, final_norm[None], target)

    g = {n: [None] * w[n].shape[0] for n in WEIGHTS if n != "final_norm"}
    reducers = []

    def layer_of(n, i):
        return i if n.startswith("ffn") else (2 * i if n.startswith("attn") else 2 * i + 1)

    def layer_reduce(layer):
        picks = {n: idx for n in big if (idx := [i for i in range(w[n].shape[0]) if layer_of(n, i) == layer])}
        return GradReduce({n: [g[n][i] for i in idx] for n, idx in picks.items()},
                          {n: (len(idx),) + w[n].shape[1:] for n, idx in picks.items()})

    def tn(x, y, scale=1.0):
        big_enough = x.shape[1] * y.shape[1] >= D_MODEL * D_FF
        red = next((r for r in reducers if r.next_job() is not None), None) if big_enough else None
        if red is None:
            return matmul_tn(x, y, scale)
        out, res = matmul_tn(x, y, scale, carry=red.next_job())
        red.deliver(res)
        return out
    for l in reversed(range(depth)):
        j = l // 2
        rec = saved[l]

        def ffn_back(tag, h_in, dh):
            wgu, wd = ffn_w[(tag, l)]
            dh_new, hn, dgu, dn = ffn_bwd(h_in, w[tag + "_norm"][l][None], wgu, wd, rec[tag + "_gu"], dh)
            g[tag + "_w_gate_up"][l] = tn(hn, dgu)
            g[tag + "_w_down"][l] = tn(rec[tag + "_a"], dh, 0.5)
            g[tag + "_norm"][l] = dn[0]
            return dh_new

        dh = ffn_back("ffn2", rec["h3"], dh)
        if l % 2 == 0:
            w_in, w_out = full["attn_w_in"][j], full["attn_w_out"][j]
            do, db_out = matmul_nt(dh, w_out)
            g["attn_w_out"][j] = matmul_tn(rec["o"], dh)
            g["attn_b_out"][j] = db_out[0]
            dq, dkvc, dkvp, dsink = attn_bwd(rec["p"], tab, attn_sinks[j][None], do)
            dkv = kv_combine(dkvc, dkvp)
            dh, hn, dn, cs_q, cs_kv = norm_proj_bwd(rec["h2"], mix_norm[l][None], dh, [dq, dkv],
                                                    [w_in[:, :ATTN_Q_W], w_in[:, ATTN_Q_W:]])
            g["attn_w_in"][j] = jnp.concatenate([matmul_tn(hn, dq), matmul_tn(hn, dkv)], axis=1)
            g["attn_b_in"][j] = jnp.concatenate([cs_q[0], cs_kv[0]])
            g["attn_sinks"][j] = dsink[0]
        else:
            w_cat, conv, par = rec["w_cat"], rec["conv"], rec["par"]
            d_on, _ = matmul_nt(dh, full["gdn_w_out"][j])
            g["gdn_w_out"][j] = matmul_tn(rec["on"], dh)
            dq, dk, dv, dz, dgc_heads, dnw = gdn_chunk_bwd(rec["qkv"], rec["gc"], rec["proj"], gdn_norm_w[j][None],
                                                           rec["st"], rec["m_offs"], d_on)
            dy, dba, dpar = gdn_pre_bwd(rec["proj"], conv, par, dq, dk, dv, dgc_heads)
            dx, dconv = gdn_conv_bwd(rec["proj"], conv, dy)
            nz = GDN_QKV_W + GDN_HEADS * GDN_DK
            dh, hn, dn, _, _, _ = norm_proj_bwd(rec["h2"], mix_norm[l][None], dh, [dx, dz, dba],
                                                [w_cat[:, :GDN_QKV_W], w_cat[:, GDN_QKV_W:nz], w_cat[:, nz:]])
            g["gdn_w_in"][j] = jnp.concatenate(
                [tn(hn, dx), matmul_tn(hn, dz), matmul_tn(hn, dba)[:, :2 * GDN_HEADS]], axis=1)
            g["gdn_conv_w"][j] = dconv[:GDN_CONV]
            g["gdn_A_log"][j] = dpar[0, GDN_HEADS:2 * GDN_HEADS]
            g["gdn_dt_bias"][j] = dpar[1, GDN_HEADS:2 * GDN_HEADS]
            g["gdn_norm_w"][j] = dnw[0]
        g["mix_norm"][l] = dn[0]
        dh = ffn_back("ffn1", rec["h1"], dh)
        if l > 0:
            reducers.append(layer_reduce(l))
    grad_x = dh[None]

    small_names = REPLICATED + ["gdn_conv_w"]
    local = {n: jnp.stack(g[n]) for n in small_names if n != "final_norm"}
    local["final_norm"] = d_final[0]
    small_shapes = [(1,)] + [local[n].shape for n in small_names]
    small = allreduce_small(_small_pack([loss_tile[0, 0:1]] + [local[n] for n in small_names]))
    small = _small_unpack(small, small_shapes)
    loss = small[0][0]
    grads = dict(zip(small_names, small[1:]))
    conv_cols = gdn_conv_w.shape[2]
    grads["gdn_conv_w"] = lax.dynamic_slice_in_dim(grads["gdn_conv_w"], _chip_index() * conv_cols, conv_cols, axis=2)
    by_layer = [layer_reduce(0).finish()] + [r.finish() for r in reversed(reducers)]
    grads.update({n: jnp.concatenate([part[n] for part in by_layer if n in part], axis=0) for n in big})

    delta, new_m, new_v = {}, {}, {}
    for n in SHARDED:
        dl, nm, nv = adamw(_as2d(w[n]), _as2d(grads[n]), _as2d(given["m_" + n]), _as2d(given["v_" + n]))
        delta[n], new_m[n], new_v[n] = dl.reshape(w[n].shape), nm.reshape(w[n].shape), nv.reshape(w[n].shape)
    shapes = [w[n].shape for n in REPLICATED]
    packed = [_small_pack([src[n] for n in REPLICATED]) for src in
              (w, grads, {n: given["m_" + n] for n in REPLICATED}, {n: given["v_" + n] for n in REPLICATED})]
    for dst, buf in zip((delta, new_m, new_v), adamw(*packed)):
        dst.update(zip(REPLICATED, _small_unpack(buf, shapes)))

    return (loss, grad_x, *[grads[n] for n in WEIGHTS], *[delta[n] for n in WEIGHTS],
            *[new_m[n] for n in WEIGHTS], *[new_v[n] for n in WEIGHTS])
```

```python
import functools

import jax
import jax.numpy as jnp
from jax import lax
from jax.experimental import pallas as pl
from jax.experimental.pallas import tpu as pltpu

F32 = jnp.float32
BF16 = jnp.bfloat16
HI = lax.Precision.HIGHEST
MESH = pl.DeviceIdType.MESH

D_MODEL = 1024
D_FF = 2816
NORM_EPS = 1e-6
LANE = 128

ATTN_Q_HEADS = 16
ATTN_KV_HEADS = 4
ATTN_HEAD_DIM = 64
ATTN_GROUP = 4
ATTN_BLOCK = 128
ROPE_DIM = 16
ROPE_THETA = 500000.0
ATTN_Q_W = 1024
ATTN_KV_W = 256

GDN_HEADS = 8
GDN_DK = 128
GDN_CONV = 4
GDN_CHUNK = 64
GDN_QKV_W = 3072

ADAM_LR = 0.001
ADAM_B1 = 0.9
ADAM_B2 = 0.999
ADAM_EPS = 1e-08
ADAM_WD = 0.01
ADAM_STEP = 10

NEG = -1e30


def _dot(a, b, prec=None):
    return lax.dot_general(a, b, (((1,), (0,)), ((), ())), precision=prec, preferred_element_type=F32)


def _dot_nt(a, b, prec=None):
    return lax.dot_general(a, b, (((1,), (1,)), ((), ())), precision=prec, preferred_element_type=F32)


def _dot_tn(a, b, prec=None):
    return lax.dot_general(a, b, (((0,), (0,)), ((), ())), precision=prec, preferred_element_type=F32)


def _bdot(a, b):
    return _dot(a.astype(BF16), b.astype(BF16))


def _bdot_nt(a, b):
    return _dot_nt(a.astype(BF16), b.astype(BF16))


def _bdot_tn(a, b):
    return _dot_tn(a.astype(BF16), b.astype(BF16))


def _sigmoid(x):
    return 1.0 / (1.0 + jnp.exp(-x))


def _silu(x):
    return x * _sigmoid(x)


def _silu_grad(x):
    s = _sigmoid(x)
    return s * (1.0 + x * (1.0 - s))


def _rms(x, w):
    r = lax.rsqrt(jnp.mean(x * x, axis=-1, keepdims=True) + NORM_EPS)
    xhat = x * r
    return xhat * w, xhat, r


def _rms_bwd(dy, w, xhat, r):
    dxhat = dy * w
    dx = r * (dxhat - xhat * jnp.mean(dxhat * xhat, axis=-1, keepdims=True))
    dw = jnp.sum(dy * xhat, axis=0, keepdims=True)
    return dx, dw


def _arb(n):
    return pltpu.CompilerParams(dimension_semantics=("arbitrary",) * n)


def _tile(n, want):
    t = min(n, want)
    assert n % t == 0, (n, want)
    return t


def _iota2(shape, dim):
    return lax.broadcasted_iota(jnp.int32, shape, dim)


_NN = (((1,), (0,)), ((), ()))
_NT = (((1,), (1,)), ((), ()))
_TN = (((0,), (0,)), ((), ()))


def _raw1(a, b, dn):
    return lax.dot_general(a.astype(BF16), b.astype(BF16), dn, preferred_element_type=F32)


def _raw3(a, b, dn):
    ah, bh = a.astype(BF16), b.astype(BF16)
    al, bl = (a - ah.astype(F32)).astype(BF16), (b - bh.astype(F32)).astype(BF16)
    f = lambda x, y: lax.dot_general(x, y, dn, preferred_element_type=F32)
    return f(ah, bh) + f(ah, bl) + f(al, bh)


def _make_mm(raw):
    @jax.custom_vjp
    def mm(a, b):
        return raw(a, b, _NN)

    mm.defvjp(lambda a, b: (raw(a, b, _NN), (a, b)),
              lambda res, ct: (raw(ct, res[1], _NT), raw(res[0], ct, _TN)))

    @jax.custom_vjp
    def mm_nt(a, b):
        return raw(a, b, _NT)

    mm_nt.defvjp(lambda a, b: (raw(a, b, _NT), (a, b)),
                 lambda res, ct: (raw(ct, res[1], _NN), raw(ct, res[0], _TN)))
    return mm, mm_nt


_mm1, _mm1_nt = _make_mm(_raw1)
_mm3, _mm3_nt = _make_mm(_raw3)


def _eye(n):
    return (_iota2((n, n), 0) == _iota2((n, n), 1)).astype(F32)


def _each(f, *lists):
    return [f(*t) for t in zip(*lists)]


def _inv_newton(mats):
    eye = _eye(mats[0].shape[0])
    ps = [-a for a in mats]
    ms = [eye + p for p in ps]
    k = 1
    while 2 * k < GDN_CHUNK:
        ps = [_raw1(p, p, _NN) for p in ps]
        ms = _each(lambda m, p: m + _raw1(m, p, _NN), ms, ps)
        k *= 2
    rs = _each(lambda a, m: eye - m - _raw3(a, m, _NN), mats, ms)
    return _each(lambda m, r: m + _raw1(m, r, _NN), ms, rs)


def _inv_cotangent(ms, dms):
    ts = _each(lambda m, dm: _raw3(m, dm, _TN), ms, dms)
    return _each(lambda t, m: -_raw3(t, m, _NT), ts, ms)


def _chunk_causal(n):
    r, s = _iota2((n, n), 0), _iota2((n, n), 1)
    same = (r // GDN_CHUNK) == (s // GDN_CHUNK)
    return same & (r >= s), same & (r > s)


def _gdn_a(k, dcb, bb128):
    causal, strict = _chunk_causal(k[0].shape[0])
    decay_l = [jnp.exp(jnp.where(causal, d - d.T, NEG)) for d in dcb]
    kb = _each(lambda a, b: a * b, k, bb128)
    return _each(lambda x, y, dl: jnp.where(strict, _mm1_nt(x, y) * dl, 0.0), kb, k, decay_l)


def _gdn_rest(q, k, v, dcb, dcb128, dlb128, bb128, m_off):
    causal, _ = _chunk_causal(q[0].shape[0])
    decay_l = [jnp.exp(jnp.where(causal, d - d.T, NEG)) for d in dcb]
    kb = _each(lambda a, b: a * b, k, bb128)
    edc = [jnp.exp(d) for d in dcb128]
    rhs = _each(lambda vv, bb, kk, e: jnp.concatenate([vv * bb, kk * e], axis=1), v, bb128, kb, edc)
    sol = _each(lambda x, m: x + _mm3(m, x), rhs, m_off)
    aqk = _each(lambda x, y, dl: jnp.where(causal, _mm1_nt(x, y) * dl, 0.0), q, k, decay_l)
    q_dec = _each(lambda x, e: x * e, q, edc)
    k_dec = _each(lambda x, dl, dc: x * jnp.exp(dl - dc), k, dlb128, dcb128)
    return sol, aqk, q_dec, k_dec


def _gdn_local(q, k, v, dcb, dcb128, dlb128, bb128):
    eye = _eye(q[0].shape[0])
    m_off = [m - eye for m in _inv_newton(_gdn_a(k, dcb, bb128))]
    return _gdn_rest(q, k, v, dcb, dcb128, dlb128, bb128, m_off) + (m_off,)


def _gated_norm(o, z, nw):
    r = lax.rsqrt(jnp.mean(o * o, axis=-1, keepdims=True) + NORM_EPS)
    return o * r * nw * _silu(z)


def _gated_norm_bwd(dy, o, z, nw):
    r = lax.rsqrt(jnp.mean(o * o, axis=-1, keepdims=True) + NORM_EPS)
    xhat = o * r
    sz = _silu(z)
    dxhat = dy * nw * sz
    do = r * (dxhat - xhat * jnp.mean(dxhat * xhat, axis=-1, keepdims=True))
    dz = dy * xhat * nw * _silu_grad(z)
    dnw = jnp.sum(dy * xhat * sz, axis=0, keepdims=True)
    return do, dz, dnw


GDN_HEAD_GROUP = 8
GC_BETA, GC_G, GC_DECAY, GC_LAST = 0, GDN_HEADS, 2 * GDN_HEADS, 3 * GDN_HEADS


def _gate_cols(gc, h, rows):
    lane = _iota2((rows, LANE), 1)
    col = lambda off: jnp.sum(jnp.where(lane == off + h, gc, 0.0), axis=-1, keepdims=True)
    return col(GC_BETA), col(GC_DECAY), col(GC_LAST)


def _gdn_local_args(q_ref, k_ref, v_ref, gc_ref, hp, n):
    gc = gc_ref[...]
    lanes = [slice(hh * LANE, (hh + 1) * LANE) for hh in range(hp)]
    cols = [_gate_cols(gc, pl.program_id(0) * hp + hh, n) for hh in range(hp)]
    bcast = lambda i, w: [jnp.broadcast_to(c[i], (n, w)) for c in cols]
    args = ([q_ref[:, ls] for ls in lanes], [k_ref[:, ls] for ls in lanes], [v_ref[:, ls] for ls in lanes],
            bcast(1, n), bcast(1, LANE), bcast(2, LANE), bcast(0, LANE))
    return args, [c[2] for c in cols], lanes


def gdn_chunk_fwd(qkv, gc, proj, norm_w):
    t = qkv.shape[0]
    c = GDN_CHUNK
    tc = _tile(t, 256)
    nsub = tc // c

    hp = GDN_HEAD_GROUP

    def body(q_ref, k_ref, v_ref, gc_ref, z_ref, nw_ref, on_ref, st_ref, m_ref, s_ref):
        @pl.when(pl.program_id(1) == 0)
        def _():
            s_ref[...] = jnp.zeros_like(s_ref)

        args, dl, lanes = _gdn_local_args(q_ref, k_ref, v_ref, gc_ref, hp, tc)
        sol, aqk, q_dec, k_dec, m_off = _gdn_local(*args)
        for hh in range(hp):
            m_ref[hh] = m_off[hh]
        u = [x[:, :GDN_DK] for x in sol]
        w = [x[:, GDN_DK:] for x in sol]
        s = [s_ref[hh] for hh in range(hp)]
        v_new, o_state = [[] for _ in range(hp)], [[] for _ in range(hp)]
        for j in range(nsub):
            sl = slice(j * c, (j + 1) * c)
            for hh in range(hp):
                st_ref[hh, j] = s[hh]
            vn = _each(lambda uu, ww, ss: uu[sl] - _bdot(ww[sl], ss), u, w, s)
            os_ = _each(lambda qq, ss: _bdot(qq[sl], ss), q_dec, s)
            s = _each(lambda ss, d, kk, vv: ss * jnp.exp(d[j * c:j * c + 1]) + _bdot_tn(kk[sl], vv), s, dl, k_dec, vn)
            for hh in range(hp):
                v_new[hh].append(vn[hh])
                o_state[hh].append(os_[hh])
        for hh in range(hp):
            s_ref[hh] = s[hh]
        cat = lambda xs: jnp.concatenate(xs, axis=0)
        o = _each(lambda os_, aa, vv: cat(os_) + _bdot(aa, cat(vv)), o_state, aqk, v_new)
        for hh, ls in enumerate(lanes):
            on_ref[:, ls] = _gated_norm(o[hh], z_ref[:, ls], nw_ref[...]).astype(BF16)

    col = lambda off: pl.BlockSpec((tc, hp * LANE), lambda g, i: (i, off // hp + g))
    return pl.pallas_call(
        body, name="gdn_chunk_fwd",
        grid=(GDN_HEADS // hp, t // tc),
        in_specs=[col(0), col(GDN_HEADS), col(2 * GDN_HEADS),
                  pl.BlockSpec((tc, LANE), lambda g, i: (i, 0)),
                  col(GDN_QKV_W // LANE),
                  pl.BlockSpec((1, LANE), lambda g, i: (0, 0))],
        out_specs=[col(0),
                   pl.BlockSpec((hp, nsub, GDN_DK, GDN_DK), lambda g, i: (g, i, 0, 0)),
                   pl.BlockSpec((hp, tc, tc), lambda g, i: (g, i, 0))],
        out_shape=[jax.ShapeDtypeStruct((t, GDN_HEADS * GDN_DK), BF16),
                   jax.ShapeDtypeStruct((GDN_HEADS, t // c, GDN_DK, GDN_DK), F32),
                   jax.ShapeDtypeStruct((GDN_HEADS, t, tc), F32)],
        scratch_shapes=[pltpu.VMEM((hp, GDN_DK, GDN_DK), F32)],
        compiler_params=_arb(2),
    )(qkv, qkv, qkv, gc, proj, norm_w)


def gdn_chunk_bwd(qkv, gc, proj, norm_w, states, m_offs, d_on):
    t = qkv.shape[0]
    c = GDN_CHUNK
    tc = _tile(t, 256)
    nsub = tc // c
    nblk = t // tc
    hp = GDN_HEAD_GROUP

    def body(q_ref, k_ref, v_ref, gc_ref, z_ref, nw_ref, st_ref, m_ref, don_ref,
             dq_ref, dk_ref, dv_ref, dz_ref, dgc_ref, dnw_ref, ds_ref):
        @pl.when(pl.program_id(1) == 0)
        def _():
            ds_ref[...] = jnp.zeros_like(ds_ref)

        @pl.when((pl.program_id(1) == 0) & (pl.program_id(0) == 0))
        def _():
            dnw_ref[...] = jnp.zeros_like(dnw_ref)

        rows = [slice(j * c, (j + 1) * c) for j in range(nsub)]
        cat = lambda xs: jnp.concatenate(xs, axis=0)
        lsum = lambda x: jnp.sum(x, axis=-1, keepdims=True)
        lane = _iota2((tc, LANE), 1)
        row = _iota2((tc, 1), 0)
        heads = range(hp)
        args, dl, lanes = _gdn_local_args(q_ref, k_ref, v_ref, gc_ref, hp, tc)
        m_off = [m_ref[hh] for hh in heads]
        (sol, aqk, q_dec, k_dec), vjp = jax.vjp(_gdn_rest, *args, m_off)
        u = [x[:, :GDN_DK] for x in sol]
        w = [x[:, GDN_DK:] for x in sol]
        states = [[st_ref[hh, j] for j in range(nsub)] for hh in heads]
        v_new = _each(lambda uu, ww, st: [uu[sl] - _bdot(ww[sl], s) for sl, s in zip(rows, st)], u, w, states)
        v_all = [cat(v) for v in v_new]
        o = _each(lambda qq, st, aa, vv: cat([_bdot(qq[sl], s) for sl, s in zip(rows, st)]) + _bdot(aa, vv),
                  q_dec, states, aqk, v_all)
        nw = nw_ref[...]
        gn = [_gated_norm_bwd(don_ref[:, ls], o[hh], z_ref[:, ls], nw) for hh, ls in enumerate(lanes)]
        do = [x[0] for x in gn]
        for hh, ls in enumerate(lanes):
            dnw_ref[...] += gn[hh][2]
            dz_ref[:, ls] = gn[hh][1].astype(BF16)
        d_aqk = _each(_bdot_nt, do, v_all)
        dv_o = _each(_bdot_tn, aqk, do)
        ds = [ds_ref[hh] for hh in heads]
        d_u, d_w, d_qdec, d_kdec, d_last = ([[None] * nsub for _ in heads] for _ in range(5))
        for j in reversed(range(nsub)):
            sl = rows[j]
            sj = [states[hh][j] for hh in heads]
            cd = [jnp.exp(d[j * c:j * c + 1]) for d in dl]
            du = _each(lambda dvo, kk, dd: dvo[sl] + _bdot(kk[sl], dd), dv_o, k_dec, ds)
            dqd = _each(lambda dd, s: _bdot_nt(dd[sl], s), do, sj)
            dkd = _each(lambda vv, dd: _bdot_nt(vv[j], dd), v_new, ds)
            dla = _each(lambda s, dd, cc: jnp.sum(lsum(s * dd), axis=0, keepdims=True) * cc, sj, ds, cd)
            dw = _each(lambda x, s: -_bdot_nt(x, s), du, sj)
            ds = _each(lambda qq, dd, cc, dsn, ww, x: _bdot_tn(qq[sl], dd[sl]) + cc * dsn - _bdot_tn(ww[sl], x),
                       q_dec, do, cd, ds, w, du)
            for hh in heads:
                d_u[hh][j], d_w[hh][j], d_qdec[hh][j], d_kdec[hh][j], d_last[hh][j] = du[hh], dw[hh], dqd[hh], dkd[hh], dla[hh]
        for hh in heads:
            ds_ref[hh] = ds[hh]
        d_sol = _each(lambda a, b: jnp.concatenate([cat(a), cat(b)], axis=1), d_u, d_w)
        dq, dk, dv, d_dcb, d_dcb128, d_dlb128, d_bb, d_m = vjp(
            (d_sol, d_aqk, [cat(x) for x in d_qdec], [cat(x) for x in d_kdec]))
        eye = _eye(tc)
        d_a = _inv_cotangent([m + eye for m in m_off], d_m)
        k_args = (args[1], args[3], args[6])
        dk_a, d_dcb_a, d_bb_a = jax.vjp(_gdn_a, *k_args)[1](d_a)
        add = lambda xs, ys: _each(lambda x, y: x + y, xs, ys)
        dk, d_dcb, d_bb = add(dk, dk_a), add(d_dcb, d_dcb_a), add(d_bb, d_bb_a)
        dgc = jnp.zeros((tc, LANE), F32)
        for hh, ls in enumerate(lanes):
            h = pl.program_id(0) * hp + hh
            dq_ref[:, ls] = dq[hh]
            dk_ref[:, ls] = dk[hh]
            dv_ref[:, ls] = dv[hh]
            d_dl = lsum(d_dlb128[hh])
            for j in range(nsub):
                d_dl = d_dl + jnp.where(row == j * c, d_last[hh][j], 0.0)
            dgc = dgc + jnp.where(lane == GC_BETA + h, lsum(d_bb[hh]),
                                  jnp.where(lane == GC_DECAY + h, lsum(d_dcb[hh]) + lsum(d_dcb128[hh]),
                                            jnp.where(lane == GC_LAST + h, d_dl, 0.0)))
        dgc_ref[0] = dgc

    rev = lambda i: nblk - 1 - i
    col = lambda off: pl.BlockSpec((tc, hp * LANE), lambda g, i: (rev(i), off // hp + g))
    return pl.pallas_call(
        body, name="gdn_chunk_bwd",
        grid=(GDN_HEADS // hp, nblk),
        in_specs=[col(0), col(GDN_HEADS), col(2 * GDN_HEADS),
                  pl.BlockSpec((tc, LANE), lambda g, i: (rev(i), 0)),
                  col(GDN_QKV_W // LANE),
                  pl.BlockSpec((1, LANE), lambda g, i: (0, 0)),
                  pl.BlockSpec((hp, nsub, GDN_DK, GDN_DK), lambda g, i: (g, rev(i), 0, 0)),
                  pl.BlockSpec((hp, tc, tc), lambda g, i: (g, rev(i), 0)),
                  col(0)],
        out_specs=[col(0), col(0), col(0), col(0),
                   pl.BlockSpec((1, tc, LANE), lambda g, i: (g, rev(i), 0)),
                   pl.BlockSpec((1, LANE), lambda g, i: (0, 0))],
        out_shape=[
            jax.ShapeDtypeStruct((t, GDN_HEADS * GDN_DK), F32),
            jax.ShapeDtypeStruct((t, GDN_HEADS * GDN_DK), F32),
            jax.ShapeDtypeStruct((t, GDN_HEADS * GDN_DK), F32),
            jax.ShapeDtypeStruct((t, GDN_HEADS * GDN_DK), BF16),
            jax.ShapeDtypeStruct((GDN_HEADS // hp, t, LANE), F32),
            jax.ShapeDtypeStruct((1, LANE), F32)],
        scratch_shapes=[pltpu.VMEM((hp, GDN_DK, GDN_DK), F32)],
        compiler_params=_arb(2),
    )(qkv, qkv, qkv, gc, proj, norm_w, states, m_offs, d_on)


GDN_PROJ_W = GDN_QKV_W + 1024 + LANE
GDN_Q_SCALE = GDN_DK ** -0.5


def _shift_rows(ext, shift, lo, n):
    if shift == 0:
        return ext[lo:lo + n]
    return pltpu.roll(ext, shift, 0)[lo:lo + n]


def _conv_fwd(x, halo, w):
    n = x.shape[0]
    ext = jnp.concatenate([halo, x], axis=0)
    y = w[GDN_CONV - 1:GDN_CONV] * x
    for j in range(GDN_CONV - 1):
        y = y + w[j:j + 1] * _shift_rows(ext, GDN_CONV - 1 - j, 8, n)
    return y


def _chunk_masks(n):
    r, s = _iota2((n, n), 0), _iota2((n, n), 1)
    same = (r // GDN_CHUNK) == (s // GDN_CHUNK)
    return (same & (r >= s)).astype(F32), (same & (r <= s)).astype(F32), same.astype(F32)


def _softplus(x):
    return jnp.maximum(x, 0.0) + jnp.log(1.0 + jnp.exp(-jnp.abs(x)))


def _l2n(t):
    rs = lax.rsqrt(jnp.sum(t * t, axis=-1, keepdims=True) + NORM_EPS)
    return t * rs, rs


def gdn_pre_fwd(proj, conv_w, gate_par):
    t = proj.shape[0]
    tm = _tile(t, 256)

    def body(x_ref, halo_ref, ba_ref, w_ref, gp_ref, qkv_ref, gc_ref):
        i = pl.program_id(0)
        halo = jnp.where(i > 0, halo_ref[...], 0.0)
        y = _silu(_conv_fwd(x_ref[...], halo, w_ref[...]))
        for hh in range(2 * GDN_HEADS):
            sl = slice(hh * LANE, (hh + 1) * LANE)
            tn, _ = _l2n(y[:, sl])
            qkv_ref[:, sl] = tn * GDN_Q_SCALE if hh < GDN_HEADS else tn
        qkv_ref[:, 2 * GDN_HEADS * LANE:] = y[:, 2 * GDN_HEADS * LANE:]
        ba = ba_ref[...]
        lane = _iota2(ba.shape, 1)
        gp = gp_ref[...]
        is_a = (lane >= GC_G) & (lane < GC_G + GDN_HEADS)
        g = jnp.where(is_a, -jnp.exp(gp[0:1]) * _softplus(ba + gp[1:2]), 0.0)
        tri, _, same = _chunk_masks(tm)
        decay = pltpu.roll(_dot(tri, g, HI), GC_DECAY - GC_G, 1)
        last = pltpu.roll(_dot(same, g, HI), GC_LAST - GC_G, 1)
        gc_ref[...] = jnp.where(lane < GDN_HEADS, _sigmoid(ba), g) + decay + last

    return pl.pallas_call(
        body, name="gdn_pre_fwd",
        grid=(t // tm,),
        in_specs=[pl.BlockSpec((tm, GDN_QKV_W), lambda i: (i, 0)),
                  pl.BlockSpec((8, GDN_QKV_W), lambda i: (jnp.maximum(i * (tm // 8) - 1, 0), 0)),
                  pl.BlockSpec((tm, LANE), lambda i: (i, (GDN_QKV_W + 1024) // LANE)),
                  pl.BlockSpec((8, GDN_QKV_W), lambda i: (0, 0)),
                  pl.BlockSpec((8, LANE), lambda i: (0, 0))],
        out_specs=[pl.BlockSpec((tm, GDN_QKV_W), lambda i: (i, 0)),
                   pl.BlockSpec((tm, LANE), lambda i: (i, 0))],
        out_shape=[jax.ShapeDtypeStruct((t, GDN_QKV_W), F32), jax.ShapeDtypeStruct((t, LANE), F32)],
        compiler_params=_arb(1),
    )(proj, proj, proj, conv_w, gate_par)


def gdn_pre_bwd(proj, conv_w, gate_par, dq, dk, dv, dgc_heads):
    t = proj.shape[0]
    tm = _tile(t, 256)

    def body(x_ref, halo_ref, ba_ref, w_ref, gp_ref, dq_ref, dk_ref, dv_ref, dgc_ref, dy_ref, dba_ref, dgp_ref):
        i = pl.program_id(0)

        @pl.when(i == 0)
        def _():
            dgp_ref[...] = jnp.zeros_like(dgp_ref)

        halo = jnp.where(i > 0, halo_ref[...], 0.0)
        y = _conv_fwd(x_ref[...], halo, w_ref[...])
        for hh in range(3 * GDN_HEADS):
            sl = slice(hh * LANE, (hh + 1) * LANE)
            hsl = slice((hh % GDN_HEADS) * LANE, (hh % GDN_HEADS + 1) * LANE)
            yy = y[:, sl]
            if hh < 2 * GDN_HEADS:
                tn, rs = _l2n(_silu(yy))
                dtn = dq_ref[:, hsl] * GDN_Q_SCALE if hh < GDN_HEADS else dk_ref[:, hsl]
                dsil = rs * (dtn - tn * jnp.sum(dtn * tn, axis=-1, keepdims=True))
            else:
                dsil = dv_ref[:, hsl]
            dy_ref[:, sl] = dsil * _silu_grad(yy)
        dgc = dgc_ref[0]
        for hh in range(1, dgc_heads.shape[0]):
            dgc = dgc + dgc_ref[hh]
        ba = ba_ref[...]
        lane = _iota2(ba.shape, 1)
        _, tri_t, same = _chunk_masks(tm)
        d_decay = jnp.where((lane >= GC_DECAY) & (lane < GC_DECAY + GDN_HEADS), dgc, 0.0)
        d_last = jnp.where((lane >= GC_LAST) & (lane < GC_LAST + GDN_HEADS), dgc, 0.0)
        dgc = (jnp.where(lane < GDN_HEADS, dgc, 0.0) + pltpu.roll(_dot(tri_t, d_decay, HI), LANE - (GC_DECAY - GC_G), 1)
               + pltpu.roll(_dot(same, d_last, HI), LANE - (GC_LAST - GC_G), 1))
        gp = gp_ref[...]
        xg = ba + gp[1:2]
        ea = jnp.exp(gp[0:1])
        sp = _softplus(xg)
        sb = _sigmoid(ba)
        is_b = lane < GDN_HEADS
        is_a = (lane >= GDN_HEADS) & (lane < 2 * GDN_HEADS)
        d_pre = jnp.where(is_a, dgc * (-ea) * _sigmoid(xg), 0.0)
        dba_ref[...] = jnp.where(is_b, dgc * sb * (1.0 - sb), d_pre).astype(BF16)
        d_alog = jnp.sum(jnp.where(is_a, dgc * (-ea) * sp, 0.0), axis=0, keepdims=True)
        d_dtb = jnp.sum(d_pre, axis=0, keepdims=True)
        row = _iota2((8, LANE), 0)
        dgp_ref[...] += jnp.where(row == 0, d_alog, jnp.where(row == 1, d_dtb, 0.0))

    hspec = pl.BlockSpec((tm, GDN_HEADS * LANE), lambda i: (i, 0))
    return pl.pallas_call(
        body, name="gdn_pre_bwd",
        grid=(t // tm,),
        in_specs=[pl.BlockSpec((tm, GDN_QKV_W), lambda i: (i, 0)),
                  pl.BlockSpec((8, GDN_QKV_W), lambda i: (jnp.maximum(i * (tm // 8) - 1, 0), 0)),
                  pl.BlockSpec((tm, LANE), lambda i: (i, (GDN_QKV_W + 1024) // LANE)),
                  pl.BlockSpec((8, GDN_QKV_W), lambda i: (0, 0)),
                  pl.BlockSpec((8, LANE), lambda i: (0, 0)),
                  hspec, hspec, hspec,
                  pl.BlockSpec((dgc_heads.shape[0], tm, LANE), lambda i: (0, i, 0))],
        out_specs=[pl.BlockSpec((tm, GDN_QKV_W), lambda i: (i, 0)),
                   pl.BlockSpec((tm, LANE), lambda i: (i, 0)),
                   pl.BlockSpec((8, LANE), lambda i: (0, 0))],
        out_shape=[jax.ShapeDtypeStruct((t, GDN_QKV_W), F32), jax.ShapeDtypeStruct((t, LANE), BF16),
                   jax.ShapeDtypeStruct((8, LANE), F32)],
        compiler_params=_arb(1),
    )(proj, proj, proj, conv_w, gate_par, dq, dk, dv, dgc_heads)


def gdn_conv_bwd(proj, conv_w, dy):
    t = proj.shape[0]
    tm = _tile(t, 256)
    nblk = t // tm

    def body(x_ref, halo_ref, w_ref, dy_ref, dyn_ref, dx_ref, dw_ref):
        i = pl.program_id(0)

        @pl.when(i == 0)
        def _():
            dw_ref[...] = jnp.zeros_like(dw_ref)

        w = w_ref[...]
        dy = dy_ref[...]
        ext_dy = jnp.concatenate([dy, jnp.where(i < nblk - 1, dyn_ref[...], 0.0)], axis=0)
        ext_x = jnp.concatenate([jnp.where(i > 0, halo_ref[...], 0.0), x_ref[...]], axis=0)
        dx = w[GDN_CONV - 1:GDN_CONV] * dy
        rows = [jnp.sum(dy * x_ref[...], axis=0, keepdims=True)]
        for j in range(GDN_CONV - 1):
            sh = GDN_CONV - 1 - j
            dx = dx + w[j:j + 1] * _shift_rows(ext_dy, tm + 8 - sh, 0, tm)
            rows.insert(j, jnp.sum(dy * _shift_rows(ext_x, sh, 8, tm), axis=0, keepdims=True))
        dx_ref[...] = dx.astype(BF16)
        row = _iota2((8, GDN_QKV_W), 0)
        acc = jnp.zeros((8, GDN_QKV_W), F32)
        for j in range(GDN_CONV):
            acc = acc + jnp.where(row == j, rows[j], 0.0)
        dw_ref[...] += acc

    return pl.pallas_call(
        body, name="gdn_conv_bwd",
        grid=(nblk,),
        in_specs=[pl.BlockSpec((tm, GDN_QKV_W), lambda i: (i, 0)),
                  pl.BlockSpec((8, GDN_QKV_W), lambda i: (jnp.maximum(i * (tm // 8) - 1, 0), 0)),
                  pl.BlockSpec((8, GDN_QKV_W), lambda i: (0, 0)),
                  pl.BlockSpec((tm, GDN_QKV_W), lambda i: (i, 0)),
                  pl.BlockSpec((8, GDN_QKV_W), lambda i: (jnp.minimum((i + 1) * (tm // 8), t // 8 - 1), 0))],
        out_specs=[pl.BlockSpec((tm, GDN_QKV_W), lambda i: (i, 0)),
                   pl.BlockSpec((8, GDN_QKV_W), lambda i: (0, 0))],
        out_shape=[jax.ShapeDtypeStruct((t, GDN_QKV_W), BF16), jax.ShapeDtypeStruct((8, GDN_QKV_W), F32)],
        compiler_params=_arb(1),
    )(proj, proj, conv_w, dy, dy)


def _resident(w_hbm, w_vmem, sem):
    @pl.when(pl.program_id(0) == 0)
    def _():
        cp = pltpu.make_async_copy(w_hbm, w_vmem, sem)
        cp.start()
        cp.wait()


ANY = pl.BlockSpec(memory_space=pl.ANY)


def norm_proj(h, nw, w, bias):
    t, d = h.shape
    n = w.shape[1]
    tm = _tile(t, 512)
    nc = _tile(n, 1536) if n % 1536 == 0 else _tile(n, 1408)

    def body(h_ref, nw_ref, w_hbm, b_ref, o_ref, w_ref, sem):
        _resident(w_hbm, w_ref, sem)
        hn = _rms(h_ref[...], nw_ref[...])[0].astype(BF16)
        for c0 in range(0, n, nc):
            o_ref[:, c0:c0 + nc] = _dot(hn, w_ref[:, c0:c0 + nc]) + b_ref[:, c0:c0 + nc]

    return pl.pallas_call(
        body, name="norm_proj",
        grid=(t // tm,),
        in_specs=[pl.BlockSpec((tm, d), lambda i: (i, 0)), pl.BlockSpec((1, d), lambda i: (0, 0)), ANY,
                  pl.BlockSpec((1, n), lambda i: (0, 0))],
        out_specs=pl.BlockSpec((tm, n), lambda i: (i, 0)),
        out_shape=jax.ShapeDtypeStruct((t, n), F32),
        scratch_shapes=[pltpu.VMEM((d, n), BF16), pltpu.SemaphoreType.DMA],
        compiler_params=_arb(1),
    )(h, nw, w, bias)


def linear_residual(h, x, w, bias):
    t, d = h.shape
    k = x.shape[1]
    tm = _tile(t, 512)

    def body(h_ref, x_ref, w_hbm, b_ref, o_ref, w_ref, sem):
        _resident(w_hbm, w_ref, sem)
        o_ref[...] = h_ref[...] + _dot(x_ref[...], w_ref[...]) + b_ref[...]

    return pl.pallas_call(
        body, name="linear_residual",
        grid=(t // tm,),
        in_specs=[pl.BlockSpec((tm, d), lambda i: (i, 0)), pl.BlockSpec((tm, k), lambda i: (i, 0)), ANY,
                  pl.BlockSpec((1, d), lambda i: (0, 0))],
        out_specs=pl.BlockSpec((tm, d), lambda i: (i, 0)),
        out_shape=jax.ShapeDtypeStruct((t, d), F32),
        scratch_shapes=[pltpu.VMEM((k, d), BF16), pltpu.SemaphoreType.DMA],
        compiler_params=_arb(1),
    )(h, x, w, bias)


def matmul_nt(dy, w):
    t, d = dy.shape
    k = w.shape[0]
    tm = _tile(t, 512)

    def body(dy_ref, w_hbm, o_ref, cs_ref, w_ref, sem):
        _resident(w_hbm, w_ref, sem)

        @pl.when(pl.program_id(0) == 0)
        def _():
            cs_ref[...] = jnp.zeros_like(cs_ref)

        dy = dy_ref[...]
        cs_ref[...] += jnp.sum(dy, axis=0, keepdims=True)
        o_ref[...] = _dot_nt(dy.astype(BF16), w_ref[...])

    return pl.pallas_call(
        body, name="matmul_nt",
        grid=(t // tm,),
        in_specs=[pl.BlockSpec((tm, d), lambda i: (i, 0)), ANY],
        out_specs=[pl.BlockSpec((tm, k), lambda i: (i, 0)), pl.BlockSpec((1, d), lambda i: (0, 0))],
        out_shape=[jax.ShapeDtypeStruct((t, k), F32), jax.ShapeDtypeStruct((1, d), F32)],
        scratch_shapes=[pltpu.VMEM((k, d), BF16), pltpu.SemaphoreType.DMA],
        compiler_params=_arb(1),
    )(dy, w)


def norm_proj_bwd(h, nw, dh, dps, ws):
    t, d = h.shape
    np_ = len(dps)
    ns = [w.shape[1] for w in ws]
    tm = _tile(t, 512)

    def body(*refs):
        h_ref, nw_ref, dh_ref = refs[:3]
        dp_refs = refs[3:3 + np_]
        w_hbms = refs[3 + np_:3 + 2 * np_]
        o_ref, hn_ref, dnw_ref = refs[3 + 2 * np_:6 + 2 * np_]
        cs_refs = refs[6 + 2 * np_:6 + 3 * np_]
        w_refs = refs[6 + 3 * np_:6 + 4 * np_]
        sem = refs[6 + 4 * np_]
        for a, b in zip(w_hbms, w_refs):
            _resident(a, b, sem)

        @pl.when(pl.program_id(0) == 0)
        def _():
            dnw_ref[...] = jnp.zeros_like(dnw_ref)
            for c in cs_refs:
                c[...] = jnp.zeros_like(c)

        nw = nw_ref[...]
        hn, xhat, r = _rms(h_ref[...], nw)
        hn_ref[...] = hn.astype(BF16)
        dhn = jnp.zeros((tm, d), F32)
        for dp_ref, w_ref, cs_ref in zip(dp_refs, w_refs, cs_refs):
            dp = dp_ref[...]
            cs_ref[...] += jnp.sum(dp.astype(F32), axis=0, keepdims=True)
            dhn = dhn + _dot_nt(dp, w_ref[...])
        dx, dnw = _rms_bwd(dhn, nw, xhat, r)
        dnw_ref[...] += dnw
        o_ref[...] = dh_ref[...] + dx

    row = pl.BlockSpec((tm, d), lambda i: (i, 0))
    vec = pl.BlockSpec((1, d), lambda i: (0, 0))
    return pl.pallas_call(
        body, name="norm_proj_bwd",
        grid=(t // tm,),
        in_specs=[row, vec, row] + [pl.BlockSpec((tm, n), lambda i: (i, 0)) for n in ns] + [ANY] * np_,
        out_specs=[row, row, vec] + [pl.BlockSpec((1, n), lambda i: (0, 0)) for n in ns],
        out_shape=[jax.ShapeDtypeStruct((t, d), F32), jax.ShapeDtypeStruct((t, d), BF16),
                   jax.ShapeDtypeStruct((1, d), F32)] + [jax.ShapeDtypeStruct((1, n), F32) for n in ns],
        scratch_shapes=[pltpu.VMEM((d, n), BF16) for n in ns] + [pltpu.SemaphoreType.DMA],
        compiler_params=_arb(1),
    )(h, nw, dh, *dps, *ws)


def _pair_copies(g_refs, got_refs, send_sems, recv_sems):
    mx, my, mc = _me()
    copies = []
    for g_ref, got_ref in zip(g_refs, got_refs):
        for j in range(4):
            i = len(copies)
            copies.append(pltpu.make_async_remote_copy(
                src_ref=g_ref.at[j, 1 - mc], dst_ref=got_ref.at[j], send_sem=send_sems.at[i], recv_sem=recv_sems.at[i],
                device_id=(mx, my, 1 - mc), device_id_type=MESH))
    return copies


def _scatter_copies(p_refs, q_refs, send_sems, recv_sems):
    mx, my, mc = _me()
    copies = []
    for p_ref, q_ref in zip(p_refs, q_refs):
        for k in (1, 2, 3):
            px, py = _flip(mx, k >> 1), _flip(my, k & 1)
            i = len(copies)
            copies.append(pltpu.make_async_remote_copy(
                src_ref=p_ref.at[2 * px + py], dst_ref=q_ref.at[2 * mx + my], send_sem=send_sems.at[i],
                recv_sem=recv_sems.at[i], device_id=(px, py, mc), device_id_type=MESH))
    return copies


def _share_copies(o_refs, out_refs, send_sems, recv_sems):
    mx, my, mc = _me()
    return [pltpu.make_async_remote_copy(src_ref=o_ref, dst_ref=out_ref, send_sem=send_sems.at[i], recv_sem=recv_sems.at[i],
                                         device_id=(mx, my, 1 - mc), device_id_type=MESH)
            for i, (o_ref, out_ref) in enumerate(zip(o_refs, out_refs))]


CARRIED = {"pair": (_pair_copies, 4, lambda a: a.shape[:1] + a.shape[2:]), "scatter": (_scatter_copies, 3, lambda a: a.shape),
           "share": (_share_copies, 1, lambda a: a.shape)}


def matmul_tn(x, y, scale=1.0, carry=None):
    t, k = x.shape
    n = y.shape[1]
    tk = _tile(k, 1024) if k % 1024 == 0 else _tile(k, 1408)
    tn = n if n <= 1536 else (1024 if n % 1024 == 0 else 1408)
    assert n % tn == 0
    tt = _tile(t, 2048)
    grid = (k // tk, n // tn, t // tt)
    arrays = [] if carry is None else list(carry[1])
    na = len(arrays)

    def body(x_ref, y_ref, *rest):
        o_ref = rest[na]
        if carry is not None:
            make, _, _ = CARRIED[carry[0]]
            copies = lambda: make(rest[:na], rest[na + 1:2 * na + 1], rest[2 * na + 1], rest[2 * na + 2])
            pid = [pl.program_id(a) for a in range(3)]

            @pl.when((pid[0] == 0) & (pid[1] == 0) & (pid[2] == 0))
            def _():
                for cp in copies():
                    cp.start()

        @pl.when(pl.program_id(2) == 0)
        def _():
            o_ref[...] = jnp.zeros_like(o_ref)

        yv = y_ref[...]
        if scale != 1.0:
            yv = yv * scale
        o_ref[...] += _dot_tn(x_ref[...].astype(BF16), yv.astype(BF16))

        if carry is not None:
            @pl.when((pid[0] == grid[0] - 1) & (pid[1] == grid[1] - 1) & (pid[2] == grid[2] - 1))
            def _():
                for cp in copies():
                    cp.wait()

    out_shape = [jax.ShapeDtypeStruct((k, n), F32)]
    scratch = []
    if carry is not None:
        _, per_array, out_of = CARRIED[carry[0]]
        out_shape += [jax.ShapeDtypeStruct(out_of(a), a.dtype) for a in arrays]
        scratch = [pltpu.SemaphoreType.DMA((per_array * na,))] * 2
    res = pl.pallas_call(
        body, name="matmul_tn" if carry is None else "matmul_tn_" + carry[0],
        grid=grid,
        in_specs=[pl.BlockSpec((tt, tk), lambda i, j, s: (s, i)), pl.BlockSpec((tt, tn), lambda i, j, s: (s, j))] + [ANY] * na,
        out_specs=[pl.BlockSpec((tk, tn), lambda i, j, s: (i, j))] + [ANY] * na,
        out_shape=out_shape, scratch_shapes=scratch,
        compiler_params=_arb(3),
    )(x, y, *arrays)
    return res[0] if carry is None else (res[0], list(res[1:]))


FFN_CHUNKS = 2


class _FfnGather:
    def __init__(self, gu_sh, d_sh, gu_full, d_full, send_sems, recv_sems, local_sems):
        self.sh, self.full = (gu_sh, d_sh), (gu_full, d_full)
        self.send_sems, self.recv_sems, self.local_sems = send_sems, recv_sems, local_sems
        self.mx, self.my, self.mc = _me()
        self.me = 2 * self.mx + self.my
        self.w4, self.f4 = gu_sh.shape[1], d_sh.shape[0]
        self.hg, self.hd = gu_sh.shape[0] // 2, d_sh.shape[0] // 2

    def _window(self, a, chip, core=None):
        gu_full, d_full = self.full
        if a == 0:
            rows = pl.ds(0, 2 * self.hg) if core is None else pl.ds(pl.multiple_of(core * self.hg, 16), self.hg)
            return gu_full.at[rows, pl.ds(pl.multiple_of(chip * self.w4, LANE), self.w4)]
        if core is None:
            return d_full.at[pl.ds(pl.multiple_of(chip * self.f4, 16), self.f4), :]
        return d_full.at[pl.ds(pl.multiple_of(chip * self.f4 + core * self.hd, 16), self.hd), :]

    def _half(self, a):
        n = (self.hg, self.hd)[a]
        return self.sh[a].at[pl.ds(pl.multiple_of(self.mc * n, 16), n), :]

    def _peer(self, k):
        return _flip(self.mx, k >> 1), _flip(self.my, k & 1)

    def _fetch(self, k, a):
        px, py = self._peer(k)
        i = 2 * (k - 1) + a
        return pltpu.make_async_remote_copy(src_ref=self._half(a), dst_ref=self._window(a, self.me, self.mc),
                                            send_sem=self.send_sems.at[i], recv_sem=self.recv_sems.at[i],
                                            device_id=(px, py, self.mc), device_id_type=MESH)

    def _relay(self, k, a):
        px, py = self._peer(k)
        got = self._window(a, 2 * px + py, self.mc)
        i = 6 + 2 * (k - 1) + a
        return pltpu.make_async_remote_copy(src_ref=got, dst_ref=got, send_sem=self.send_sems.at[i],
                                            recv_sem=self.recv_sems.at[i], device_id=(self.mx, self.my, 1 - self.mc),
                                            device_id_type=MESH)

    def _own(self, a):
        return pltpu.make_async_copy(self.sh[a], self._window(a, self.me), self.local_sems.at[a])

    def start(self):
        for a in (0, 1):
            self._own(a).start()
            for k in (1, 2, 3):
                self._fetch(k, a).start()

    def relay(self):
        for k in (1, 2, 3):
            for a in (0, 1):
                self._fetch(k, a).wait_recv()
                self._relay(k, a).start()

    def finish(self):
        for k in (1, 2, 3):
            for a in (0, 1):
                self._fetch(k, a).wait_send()
                self._relay(k, a).wait()
        for a in (0, 1):
            self._own(a).wait()


def ffn_fwd(h, nw, wgu, wd, nxt=None, also=()):
    t, d = h.shape
    f = wd.shape[0]
    fc = f // FFN_CHUNKS
    tm = _tile(t, 512)
    nsteps = t // tm
    also = list(also)
    na = len(also)

    def body(h_ref, nw_ref, wgu_hbm, wd_hbm, *rest):
        gathers = []
        if nxt is None:
            o_ref, gu_ref, a_ref, wgu_ref, wd_ref, sem = rest
        else:
            gu_sh, d_sh = rest[:2]
            o_ref, gu_ref, a_ref, gu_full, d_full = rest[2 + na:7 + na]
            wgu_ref, wd_ref, sem, send_sems, recv_sems, local_sems = rest[7 + 2 * na:13 + 2 * na]
            gathers.append(_FfnGather(gu_sh, d_sh, gu_full, d_full, send_sems, recv_sems, local_sems))
            if na:
                gathers.append(_SplitGather(rest[2:2 + na], rest[7 + na:7 + 2 * na], *rest[13 + 2 * na:]))

        @pl.when(pl.program_id(0) == 0)
        def _():
            for gather in gathers:
                gather.start()

        _resident(wgu_hbm, wgu_ref, sem)
        _resident(wd_hbm, wd_ref, sem)

        @pl.when(pl.program_id(0) == (3 * nsteps) // 4)
        def _():
            for gather in gathers:
                gather.relay()

        x = h_ref[...]
        hn = _rms(x, nw_ref[...])[0].astype(BF16)
        acc = jnp.zeros((tm, d), F32)
        for c in range(FFN_CHUNKS):
            gs, us = slice(c * fc, (c + 1) * fc), slice(f + c * fc, f + (c + 1) * fc)
            g = _dot(hn, wgu_ref[:, gs])
            u = _dot(hn, wgu_ref[:, us])
            a = (_silu(g) * u).astype(BF16)
            gu_ref[:, gs] = g.astype(BF16)
            gu_ref[:, us] = u.astype(BF16)
            a_ref[:, gs] = a
            acc = acc + _dot(a, wd_ref[gs, :])
        o_ref[...] = x + 0.5 * acc

        @pl.when(pl.program_id(0) == nsteps - 1)
        def _():
            for gather in gathers:
                gather.finish()

    row = lambda w: pl.BlockSpec((tm, w), lambda i: (i, 0))
    in_specs = [row(d), pl.BlockSpec((1, d), lambda i: (0, 0)), ANY, ANY]
    out_specs = [row(d), row(2 * f), row(f)]
    out_shape = [jax.ShapeDtypeStruct((t, d), F32), jax.ShapeDtypeStruct((t, 2 * f), BF16),
                 jax.ShapeDtypeStruct((t, f), BF16)]
    scratch = [pltpu.VMEM((d, 2 * f), BF16), pltpu.VMEM((f, d), BF16), pltpu.SemaphoreType.DMA]
    args = (h, nw, wgu, wd)
    if nxt is not None:
        in_specs += [ANY] * (2 + na)
        out_specs += [ANY] * (2 + na)
        out_shape += [jax.ShapeDtypeStruct((d, 2 * f), BF16), jax.ShapeDtypeStruct((f, d), BF16)]
        out_shape += [jax.ShapeDtypeStruct((4,) + a.shape, a.dtype) for a in also]
        scratch += [pltpu.SemaphoreType.DMA((12,)), pltpu.SemaphoreType.DMA((12,)), pltpu.SemaphoreType.DMA((2,))]
        if na:
            scratch += [pltpu.SemaphoreType.DMA((6 * na,)), pltpu.SemaphoreType.DMA((6 * na,))]
        args += tuple(nxt) + tuple(also)
    return pl.pallas_call(
        body, name="ffn_fwd_gather" if nxt is not None else "ffn_fwd",
        grid=(nsteps,),
        in_specs=in_specs, out_specs=out_specs, out_shape=out_shape, scratch_shapes=scratch,
        compiler_params=_arb(1),
    )(*args)


def ffn_bwd(h, nw, wgu, wd, gu, dh):
    t, d = h.shape
    f = wd.shape[0]
    fc = f // FFN_CHUNKS
    tm = _tile(t, 256)

    def body(h_ref, nw_ref, wgu_hbm, wd_hbm, gu_ref, dh_ref, o_ref, hn_ref, dgu_ref, dout_ref, dnw_ref,
             wgu_ref, wd_ref, sem):
        _resident(wgu_hbm, wgu_ref, sem)
        _resident(wd_hbm, wd_ref, sem)

        @pl.when(pl.program_id(0) == 0)
        def _():
            dnw_ref[...] = jnp.zeros_like(dnw_ref)

        nw = nw_ref[...]
        hn32, xhat, r = _rms(h_ref[...], nw)
        hn_ref[...] = hn32.astype(BF16)
        dh = dh_ref[...]
        dout = (0.5 * dh).astype(BF16)
        dout_ref[...] = dout
        dhn = jnp.zeros((tm, d), F32)
        for c in range(FFN_CHUNKS):
            gs, us = slice(c * fc, (c + 1) * fc), slice(f + c * fc, f + (c + 1) * fc)
            g = gu_ref[:, gs].astype(F32)
            u = gu_ref[:, us].astype(F32)
            sg = _sigmoid(g)
            sil = g * sg
            da = _dot_nt(dout, wd_ref[gs, :])
            dg = (da * u * (sg * (1.0 + g * (1.0 - sg)))).astype(BF16)
            du = (da * sil).astype(BF16)
            dgu_ref[:, gs] = dg
            dgu_ref[:, us] = du
            dhn = dhn + _dot_nt(dg, wgu_ref[:, gs]) + _dot_nt(du, wgu_ref[:, us])
        dx, dnw = _rms_bwd(dhn, nw, xhat, r)
        dnw_ref[...] += dnw
        o_ref[...] = dh + dx

    row = pl.BlockSpec((tm, d), lambda i: (i, 0))
    vec = pl.BlockSpec((1, d), lambda i: (0, 0))
    return pl.pallas_call(
        body, name="ffn_bwd",
        grid=(t // tm,),
        in_specs=[row, vec, ANY, ANY, pl.BlockSpec((tm, 2 * f), lambda i: (i, 0)), row],
        out_specs=[row, row, pl.BlockSpec((tm, 2 * f), lambda i: (i, 0)), row, vec],
        out_shape=[jax.ShapeDtypeStruct((t, d), F32), jax.ShapeDtypeStruct((t, d), BF16),
                   jax.ShapeDtypeStruct((t, 2 * f), BF16), jax.ShapeDtypeStruct((t, d), BF16),
                   jax.ShapeDtypeStruct((1, d), F32)],
        scratch_shapes=[pltpu.VMEM((d, 2 * f), BF16), pltpu.VMEM((f, d), BF16), pltpu.SemaphoreType.DMA],
        compiler_params=_arb(1),
    )(h, nw, wgu, wd, gu, dh)


def loss_head(h, nw, target):
    t, d = h.shape
    tm = _tile(t, 512)

    def body(h_ref, nw_ref, tg_ref, dh_ref, loss_ref, dnw_ref):
        @pl.when(pl.program_id(0) == 0)
        def _():
            loss_ref[...] = jnp.zeros_like(loss_ref)
            dnw_ref[...] = jnp.zeros_like(dnw_ref)

        nw = nw_ref[...]
        y, xhat, r = _rms(h_ref[...], nw)
        e = y - tg_ref[...]
        loss_ref[...] += 0.5 * jnp.sum(jnp.mean(e * e, axis=-1, keepdims=True), axis=0, keepdims=True)
        dx, dnw = _rms_bwd(e * (1.0 / d), nw, xhat, r)
        dnw_ref[...] += dnw
        dh_ref[...] = dx

    row = pl.BlockSpec((tm, d), lambda i: (i, 0))
    vec = pl.BlockSpec((1, d), lambda i: (0, 0))
    return pl.pallas_call(
        body, name="loss_head",
        grid=(t // tm,),
        in_specs=[row, vec, row],
        out_specs=[row, pl.BlockSpec((8, LANE), lambda i: (0, 0)), vec],
        out_shape=[jax.ShapeDtypeStruct((t, d), F32), jax.ShapeDtypeStruct((8, LANE), F32),
                   jax.ShapeDtypeStruct((1, d), F32)],
        compiler_params=_arb(1),
    )(h, nw, target)


def adamw(w, g, m, v):
    r, c = w.shape
    tr = r
    while tr * c * 4 > (1 << 20) and tr % 16 == 0:
        tr //= 2

    def body(w_ref, g_ref, m_ref, v_ref, d_ref, nm_ref, nv_ref):
        g = g_ref[...]
        m = ADAM_B1 * m_ref[...] + (1.0 - ADAM_B1) * g
        v = ADAM_B2 * v_ref[...] + (1.0 - ADAM_B2) * (g * g)
        m_hat = m / (1.0 - ADAM_B1 ** ADAM_STEP)
        v_hat = v / (1.0 - ADAM_B2 ** ADAM_STEP)
        d_ref[...] = -ADAM_LR * (m_hat / (jnp.sqrt(v_hat) + ADAM_EPS) + ADAM_WD * w_ref[...])
        nm_ref[...] = m
        nv_ref[...] = v

    blk = pl.BlockSpec((tr, c), lambda i: (i, 0))
    return pl.pallas_call(
        body, name="adamw",
        grid=(r // tr,),
        in_specs=[blk] * 4, out_specs=[blk] * 3,
        out_shape=[jax.ShapeDtypeStruct((r, c), F32)] * 3,
        compiler_params=_arb(1),
    )(w, g, m, v)


ATTN_P_W = ATTN_Q_W + 2 * ATTN_KV_W
ATTN_SCALE = ATTN_HEAD_DIM ** -0.5


def _rope_group(t, tab, sign):
    return (t * tab[:, 0:LANE] + sign * pltpu.roll(t, 8, 1) * tab[:, LANE:2 * LANE]
            + sign * pltpu.roll(t, LANE - 8, 1) * tab[:, 2 * LANE:3 * LANE])


def _rope(t, tab, sign=1.0):
    return jnp.concatenate([_rope_group(t[:, s:s + LANE], tab, sign) for s in range(0, t.shape[1], LANE)], axis=1)


def _attn_heads(qs, ks, vs, sinks, first):
    b = ATTN_BLOCK
    rows = ATTN_GROUP * b
    qi = _iota2((rows, 2 * b), 0) % b
    kj = _iota2((rows, 2 * b), 1)
    rel = qi + b - kj
    valid = (rel >= 0) & (rel < b) & ((kj >= b) | jnp.logical_not(first))
    ss = _each(lambda q, k: jnp.where(valid, _mm1_nt(q, k) * ATTN_SCALE, NEG), qs, ks)
    ms = _each(lambda s, sink: lax.stop_gradient(jnp.maximum(jnp.max(s, axis=-1, keepdims=True), sink)), ss, sinks)
    ps = _each(lambda s, m: jnp.exp(s - m), ss, ms)
    dens = _each(lambda p, sink, m: jnp.sum(p, axis=-1, keepdims=True) + jnp.exp(sink - m), ps, sinks, ms)
    return _each(lambda p, den, v: _mm1(p / den, v), ps, dens, vs)


def _attn_prepare(p_ref, kvp_ref, tab_ref, tabp_ref, sink_ref):
    b = ATTN_BLOCK
    hd = ATTN_HEAD_DIM
    tab, tabp = tab_ref[...], tabp_ref[...]
    q = _rope(p_ref[:, 0:ATTN_Q_W], tab)
    kc = _rope(p_ref[:, ATTN_Q_W:ATTN_Q_W + ATTN_KV_W], tab)
    kp = _rope(kvp_ref[:, 0:ATTN_KV_W], tabp)
    vc = p_ref[:, ATTN_Q_W + ATTN_KV_W:ATTN_P_W]
    vp = kvp_ref[:, ATTN_KV_W:2 * ATTN_KV_W]
    sk = sink_ref[...]
    qs, ks, vs, sinks = [], [], [], []
    for h in range(ATTN_KV_HEADS):
        heads = [ATTN_GROUP * h + g for g in range(ATTN_GROUP)]
        qs.append(jnp.concatenate([q[:, i * hd:(i + 1) * hd] for i in heads], axis=0))
        ks.append(jnp.concatenate([kp[:, h * hd:(h + 1) * hd], kc[:, h * hd:(h + 1) * hd]], axis=0))
        vs.append(jnp.concatenate([vp[:, h * hd:(h + 1) * hd], vc[:, h * hd:(h + 1) * hd]], axis=0))
        sinks.append(jnp.concatenate([jnp.broadcast_to(sk[:, i:i + 1], (b, 1)) for i in heads], axis=0))
    return qs, ks, vs, sinks


def _unstack_heads(xs):
    b = ATTN_BLOCK
    return jnp.concatenate([x[g * b:(g + 1) * b] for x in xs for g in range(ATTN_GROUP)], axis=1)


def _attn_specs(nb):
    b = ATTN_BLOCK
    prev = lambda n: jnp.maximum(n - 1, 0)
    return [pl.BlockSpec((b, ATTN_P_W), lambda n: (n, 0)),
            pl.BlockSpec((b, 2 * ATTN_KV_W), lambda n: (prev(n), ATTN_Q_W // (2 * ATTN_KV_W))),
            pl.BlockSpec((b, 3 * LANE), lambda n: (n, 0)),
            pl.BlockSpec((b, 3 * LANE), lambda n: (prev(n), 0)),
            pl.BlockSpec((1, ATTN_Q_HEADS), lambda n: (0, 0))]


def attn_fwd(p, tab, sinks):
    t = p.shape[0]
    nb = t // ATTN_BLOCK

    def body(p_ref, kvp_ref, tab_ref, tabp_ref, sink_ref, o_ref):
        qs, ks, vs, sk = _attn_prepare(p_ref, kvp_ref, tab_ref, tabp_ref, sink_ref)
        os_ = _attn_heads(qs, ks, vs, sk, pl.program_id(0) == 0)
        o_ref[...] = _unstack_heads(os_).astype(BF16)

    return pl.pallas_call(
        body, name="attn_fwd",
        grid=(nb,),
        in_specs=_attn_specs(nb),
        out_specs=pl.BlockSpec((ATTN_BLOCK, ATTN_Q_W), lambda n: (n, 0)),
        out_shape=jax.ShapeDtypeStruct((t, ATTN_Q_W), BF16),
        compiler_params=_arb(1),
    )(p, p, tab, tab, sinks)


def attn_bwd(p, tab, sinks, do):
    t = p.shape[0]
    b = ATTN_BLOCK
    hd = ATTN_HEAD_DIM
    nb = t // b

    def body(p_ref, kvp_ref, tab_ref, tabp_ref, sink_ref, do_ref, dq_ref, dkvc_ref, dkvp_ref, dsink_ref):
        n = pl.program_id(0)

        @pl.when(n == 0)
        def _():
            dsink_ref[...] = jnp.zeros_like(dsink_ref)

        qs, ks, vs, sk = _attn_prepare(p_ref, kvp_ref, tab_ref, tabp_ref, sink_ref)
        first = n == 0
        _, vjp = jax.vjp(lambda a, bb, c, d: _attn_heads(a, bb, c, d, first), qs, ks, vs, sk)
        do = do_ref[...]
        dos = [jnp.concatenate([do[:, i * hd:(i + 1) * hd] for i in range(ATTN_GROUP * h, ATTN_GROUP * (h + 1))], axis=0)
               for h in range(ATTN_KV_HEADS)]
        dqs, dks, dvs, dsk = vjp(dos)
        tab, tabp = tab_ref[...], tabp_ref[...]
        dq_ref[...] = _rope(_unstack_heads([x.astype(F32) for x in dqs]), tab, -1.0).astype(BF16)
        dkc = jnp.concatenate([x.astype(F32)[b:] for x in dks], axis=1)
        dkp = jnp.concatenate([x.astype(F32)[:b] for x in dks], axis=1)
        dkvc_ref[:, 0:ATTN_KV_W] = _rope(dkc, tab, -1.0)
        dkvp_ref[:, 0:ATTN_KV_W] = _rope(dkp, tabp, -1.0)
        dkvc_ref[:, ATTN_KV_W:] = jnp.concatenate([x.astype(F32)[b:] for x in dvs], axis=1)
        dkvp_ref[:, ATTN_KV_W:] = jnp.concatenate([x.astype(F32)[:b] for x in dvs], axis=1)
        parts = [jnp.sum(d[g * b:(g + 1) * b], axis=0, keepdims=True) for d in dsk for g in range(ATTN_GROUP)]
        dsink_ref[...] += jnp.concatenate(parts, axis=1)

    blk = lambda w: pl.BlockSpec((b, w), lambda n: (n, 0))
    return pl.pallas_call(
        body, name="attn_bwd",
        grid=(nb,),
        in_specs=_attn_specs(nb) + [blk(ATTN_Q_W)],
        out_specs=[blk(ATTN_Q_W), blk(2 * ATTN_KV_W), blk(2 * ATTN_KV_W),
                   pl.BlockSpec((1, ATTN_Q_HEADS), lambda n: (0, 0))],
        out_shape=[jax.ShapeDtypeStruct((t, ATTN_Q_W), BF16), jax.ShapeDtypeStruct((t, 2 * ATTN_KV_W), F32),
                   jax.ShapeDtypeStruct((t, 2 * ATTN_KV_W), F32), jax.ShapeDtypeStruct((1, ATTN_Q_HEADS), F32)],
        compiler_params=_arb(1),
    )(p, p, tab, tab, sinks, do)


def kv_combine(dkvc, dkvp):
    t, w = dkvc.shape
    b = ATTN_BLOCK
    tm = _tile(t, 4 * b)
    nt = t // tm

    def body(c_ref, p_ref, pn_ref, o_ref):
        nxt = jnp.where(pl.program_id(0) < nt - 1, pn_ref[...], 0.0)
        shifted = jnp.concatenate([p_ref[b:, :], nxt], axis=0) if tm > b else nxt
        o_ref[...] = (c_ref[...] + shifted).astype(BF16)

    row = pl.BlockSpec((tm, w), lambda n: (n, 0))
    return pl.pallas_call(
        body, name="kv_combine",
        grid=(nt,),
        in_specs=[row, row, pl.BlockSpec((b, w), lambda n: (jnp.minimum((n + 1) * (tm // b), t // b - 1), 0))],
        out_specs=row,
        out_shape=jax.ShapeDtypeStruct((t, w), BF16),
        compiler_params=_arb(1),
    )(dkvc, dkvp, dkvp)


def _me():
    return lax.axis_index("x"), lax.axis_index("y"), lax.axis_index("c")


def _flip(v, bit):
    return 1 - v if bit else v


def _chip_index():
    return 2 * lax.axis_index("x") + lax.axis_index("y")


class _SplitGather:
    def __init__(self, x_refs, o_refs, send_sems, recv_sems):
        self.x_refs, self.o_refs, self.send_sems, self.recv_sems = x_refs, o_refs, send_sems, recv_sems
        self.mx, self.my, self.mc = _me()
        self.me = 2 * self.mx + self.my

    def _rows(self, a):
        rh = self.x_refs[a].shape[0] // 2
        return pl.ds(pl.multiple_of(self.mc * rh, 16), rh)

    def _fetch(self, a, k):
        rows, i = self._rows(a), 6 * a + k - 1
        return pltpu.make_async_remote_copy(
            src_ref=self.x_refs[a].at[rows], dst_ref=self.o_refs[a].at[self.me, rows], send_sem=self.send_sems.at[i],
            recv_sem=self.recv_sems.at[i], device_id=(_flip(self.mx, k >> 1), _flip(self.my, k & 1), self.mc),
            device_id_type=MESH)

    def _relay(self, a, k):
        theirs = self.o_refs[a].at[2 * _flip(self.mx, k >> 1) + _flip(self.my, k & 1), self._rows(a)]
        i = 6 * a + 2 + k
        return pltpu.make_async_remote_copy(src_ref=theirs, dst_ref=theirs, send_sem=self.send_sems.at[i],
                                            recv_sem=self.recv_sems.at[i], device_id=(self.mx, self.my, 1 - self.mc),
                                            device_id_type=MESH)

    def _each_copy(self):
        return [(a, k) for a in range(len(self.x_refs)) for k in (1, 2, 3)]

    def start(self):
        for a, k in self._each_copy():
            self._fetch(a, k).start()

    def relay(self):
        for a, k in self._each_copy():
            self._fetch(a, k).wait_recv()
            self._relay(a, k).start()

    def finish(self):
        for a, k in self._each_copy():
            self._fetch(a, k).wait_send()
            self._relay(a, k).wait()


def _fill_own_slot(got, x):
    return lax.dynamic_update_slice(got, x[None], (_chip_index(), 0, 0))


def allgather_chips(x, split):
    r, c = x.shape

    def body(x_ref, o_ref, send_sems, recv_sems):
        if split:
            gather = _SplitGather([x_ref], [o_ref], send_sems, recv_sems)
            gather.start()
            gather.relay()
            gather.finish()
            return
        mx, my, mc = _me()
        fetched = []
        for k in (1, 2, 3):
            cp = pltpu.make_async_remote_copy(src_ref=x_ref, dst_ref=o_ref.at[2 * mx + my],
                                              send_sem=send_sems.at[k - 1], recv_sem=recv_sems.at[k - 1],
                                              device_id=(_flip(mx, k >> 1), _flip(my, k & 1), mc), device_id_type=MESH)
            cp.start()
            fetched.append(cp)
        for cp in fetched:
            cp.wait()

    got = pl.pallas_call(
        body, name="allgather_chips",
        in_specs=[ANY], out_specs=ANY,
        out_shape=jax.ShapeDtypeStruct((4, r, c), x.dtype),
        scratch_shapes=[pltpu.SemaphoreType.DMA((6,)), pltpu.SemaphoreType.DMA((6,))],
    )(x)
    return _fill_own_slot(got, x)


def exchange(kind, arrays):
    make, per_array, out_of = CARRIED[kind]
    na = len(arrays)

    def body(*refs):
        copies = make(refs[:na], refs[na:2 * na], refs[2 * na], refs[2 * na + 1])
        for cp in copies:
            cp.start()
        for cp in copies:
            cp.wait()

    return pl.pallas_call(
        body, name=kind + "_exchange",
        in_specs=[ANY] * na, out_specs=[ANY] * na,
        out_shape=[jax.ShapeDtypeStruct(out_of(a), a.dtype) for a in arrays],
        scratch_shapes=[pltpu.SemaphoreType.DMA((per_array * na,))] * 2,
    )(*arrays)


def _row_tile(r, c):
    return max(d for d in range(16, r + 1, 16) if r % d == 0 and d * c * 4 <= (2 << 20))


def add_pair(g, got, core):
    _, _, r, c = g.shape
    tr = _row_tile(r, c)

    def body(core_ref, g_ref, got_ref, p32_ref, p16_ref):
        s = g_ref[...] + got_ref[...]
        p32_ref[...] = s
        p16_ref[...] = s.astype(BF16)

    blk = pl.BlockSpec((None, tr, c), lambda j, i, core_ref: (j, i, 0))
    return pl.pallas_call(
        body, name="add_pair",
        grid_spec=pltpu.PrefetchScalarGridSpec(
            num_scalar_prefetch=1, grid=(4, r // tr),
            in_specs=[pl.BlockSpec((None, None, tr, c), lambda j, i, core_ref: (j, core_ref[0], i, 0)), blk],
            out_specs=[blk, blk]),
        out_shape=[jax.ShapeDtypeStruct((4, r, c), F32), jax.ShapeDtypeStruct((4, r, c), BF16)],
        compiler_params=_arb(2),
    )(core, g, got)


def sum_slots(p32, q16, order):
    _, r, c = p32.shape
    tr = _row_tile(r, c)

    def body(order_ref, own_ref, a_ref, b_ref, c_ref, o_ref):
        o_ref[...] = ((own_ref[...] + a_ref[...].astype(F32)) + b_ref[...].astype(F32)) + c_ref[...].astype(F32)

    slot = lambda k: pl.BlockSpec((None, tr, c), functools.partial(lambda k, i, order_ref: (order_ref[k], i, 0), k))
    return pl.pallas_call(
        body, name="sum_slots",
        grid_spec=pltpu.PrefetchScalarGridSpec(
            num_scalar_prefetch=1, grid=(r // tr,),
            in_specs=[slot(0), slot(1), slot(2), slot(3)],
            out_specs=pl.BlockSpec((tr, c), lambda i, order_ref: (i, 0))),
        out_shape=jax.ShapeDtypeStruct((r, c), F32),
        compiler_params=_arb(1),
    )(order, p32, q16, q16, q16)


def allreduce_small(x):
    r, c = x.shape

    def body(x_ref, o_ref, buf, send_sems, recv_sems):
        mx, my, mc = _me()
        me = 4 * mx + 2 * my + mc
        buf[pl.ds(me, 1)] = x_ref[...][None]
        copies = []
        for k in range(1, 8):
            peer = (_flip(mx, k >> 2), _flip(my, (k >> 1) & 1), _flip(mc, k & 1))
            cp = pltpu.make_async_remote_copy(src_ref=x_ref, dst_ref=buf.at[me], send_sem=send_sems.at[k - 1],
                                              recv_sem=recv_sems.at[k - 1], device_id=peer, device_id_type=MESH)
            cp.start()
            copies.append(cp)
        for cp in copies:
            cp.wait()
        acc = buf[0]
        for d in range(1, 8):
            acc = acc + buf[d]
        o_ref[...] = acc

    return pl.pallas_call(
        body, name="allreduce_small",
        out_shape=jax.ShapeDtypeStruct((r, c), F32),
        scratch_shapes=[pltpu.VMEM((8, r, c), F32), pltpu.SemaphoreType.DMA((7,)), pltpu.SemaphoreType.DMA((7,))],
    )(x)


class GradReduce:
    def __init__(self, grads, shard_shapes):
        self.shapes = shard_shapes
        self.groups = _by_width(shard_shapes)
        self.core = lax.axis_index("c")
        me = _chip_index()
        self.order = jnp.stack([me, me ^ 1, me ^ 2, me ^ 3]).astype(jnp.int32)
        self.g = []
        for names in self.groups.values():
            slots = []
            for j in range(4):
                parts = []
                for n in names:
                    w = shard_shapes[n][1 + SHARDED[n]]
                    parts += [lax.slice_in_dim(gl, j * w, (j + 1) * w, axis=SHARDED[n]) for gl in grads[n]]
                slots.append(jnp.concatenate(parts, axis=0))
            g = jnp.stack(slots)
            self.g.append(g.reshape(4, 2, g.shape[1] // 2, g.shape[2]))
        self.p32 = self.p16 = None
        self.q16 = [None] * len(self.g)
        n = len(self.g)
        self.pieces = [list(range(2, n)), [1], [0]] if n > 2 else [[i] for i in reversed(range(n))]
        self.pieces = [p for p in self.pieces if p]
        self.stage = 0

    def next_job(self):
        if self.stage == 0:
            return "pair", self.g
        if self.stage <= len(self.pieces):
            return "scatter", [self.p16[i] for i in self.pieces[self.stage - 1]]
        return None

    def deliver(self, res):
        if self.stage == 0:
            core = self.core.astype(jnp.int32)[None]
            pairs = [add_pair(g, got, core) for g, got in zip(self.g, res)]
            self.p32, self.p16 = [p[0] for p in pairs], [p[1] for p in pairs]
        else:
            for i, q in zip(self.pieces[self.stage - 1], res):
                self.q16[i] = q
        self.stage += 1

    def finish(self):
        while (job := self.next_job()) is not None:
            self.deliver(exchange(*job))
        mine = [sum_slots(p, q, self.order) for p, q in zip(self.p32, self.q16)]
        theirs = exchange("share", mine)
        out = {}
        for names, m, t in zip(self.groups.values(), mine, theirs):
            red = jnp.where(self.core == 0, jnp.concatenate([m, t]), jnp.concatenate([t, m]))
            off = 0
            for n in names:
                layers, a, _ = self.shapes[n]
                out[n] = red[off:off + layers * a].reshape(self.shapes[n])
                off += layers * a
        return out


SHARDED = {"ffn1_w_gate_up": 1, "ffn1_w_down": 0, "ffn2_w_gate_up": 1, "ffn2_w_down": 0, "attn_w_in": 1,
           "attn_w_out": 0, "gdn_w_in": 1, "gdn_w_out": 0, "gdn_conv_w": 1}
REPLICATED = ["ffn1_norm", "mix_norm", "ffn2_norm", "attn_b_in", "attn_sinks", "attn_b_out", "gdn_A_log",
              "gdn_dt_bias", "gdn_norm_w", "final_norm"]
WEIGHTS = ["ffn1_norm", "ffn1_w_gate_up", "ffn1_w_down", "mix_norm", "ffn2_norm", "ffn2_w_gate_up", "ffn2_w_down",
           "attn_w_in", "attn_b_in", "attn_sinks", "attn_w_out", "attn_b_out", "gdn_w_in", "gdn_conv_w", "gdn_A_log",
           "gdn_dt_bias", "gdn_norm_w", "gdn_w_out", "final_norm"]


def _pack_rows(flats, dtype, width, row_multiple):
    flat = jnp.concatenate([f.astype(dtype).reshape(-1) for f in flats])
    per = width * row_multiple
    pad = (-flat.shape[0]) % per
    if pad:
        flat = jnp.concatenate([flat, jnp.zeros((pad,), dtype)])
    return flat.reshape(-1, width)


def _by_width(shapes):
    groups = {}
    for n, shape in shapes.items():
        groups.setdefault(shape[-1], []).append(n)
    return groups


def _pack_groups(shards, dtype):
    groups = _by_width({n: s.shape for n, s in shards.items()})
    packed = [jnp.concatenate([shards[n].astype(dtype).reshape(-1, width) for n in names], axis=0)
              for width, names in groups.items()]

    def unpack(gots):
        full = {}
        for names, got in zip(groups.values(), gots):
            off = 0
            for n in names:
                layers, a, _ = shards[n].shape
                full[n] = [jnp.concatenate([got[j, off + l * a:off + (l + 1) * a] for j in range(4)], axis=SHARDED[n])
                           for l in range(layers)]
                off += layers * a
        return full

    return packed, unpack


def _gather_weights(shards, dtype, split=True):
    packed, unpack = _pack_groups(shards, dtype)
    return unpack([allgather_chips(p, split) for p in packed])


def _small_pack(items):
    rows = []
    for a in items:
        f = a.astype(F32).reshape(-1)
        pad = (-f.shape[0]) % LANE
        rows.append(jnp.concatenate([f, jnp.zeros((pad,), F32)]) if pad else f)
    return _pack_rows(rows, F32, LANE, 8)


def _small_unpack(buf, shapes):
    flat = buf.reshape(-1)
    out, off = [], 0
    for shape in shapes:
        size = 1
        for s in shape:
            size *= s
        out.append(flat[off:off + size].reshape(shape))
        off += size + (-size) % LANE
    return out


def _rope_table(positions):
    t = positions.shape[0]
    inv_freq = ROPE_THETA ** (-jnp.arange(0, ROPE_DIM, 2, dtype=F32) / ROPE_DIM)
    ang = positions.astype(F32)[:, None] * inv_freq
    cos, sin = jnp.cos(ang), jnp.sin(ang)
    rest = ATTN_HEAD_DIM - ROPE_DIM
    zeros = lambda n: jnp.zeros((t, n), F32)
    c64 = jnp.concatenate([cos, cos, jnp.ones((t, rest), F32)], axis=1)
    s_up = jnp.concatenate([zeros(ROPE_DIM // 2), sin, zeros(rest)], axis=1)
    s_dn = jnp.concatenate([-sin, zeros(ROPE_DIM // 2 + rest)], axis=1)
    return jnp.concatenate([c64, c64, s_up, s_up, s_dn, s_dn], axis=1)


def _as2d(a):
    return a.reshape(-1, a.shape[-1]) if a.ndim > 1 else a.reshape(1, -1)


def kernel(x, positions, ffn1_norm, ffn1_w_gate_up, ffn1_w_down, mix_norm, ffn2_norm, ffn2_w_gate_up, ffn2_w_down, attn_w_in, attn_b_in, attn_sinks, attn_w_out, attn_b_out, gdn_w_in, gdn_conv_w, gdn_A_log, gdn_dt_bias, gdn_norm_w, gdn_w_out, final_norm, loss_target, m_ffn1_norm, m_ffn1_w_gate_up, m_ffn1_w_down, m_mix_norm, m_ffn2_norm, m_ffn2_w_gate_up, m_ffn2_w_down, m_attn_w_in, m_attn_b_in, m_attn_sinks, m_attn_w_out, m_attn_b_out, m_gdn_w_in, m_gdn_conv_w, m_gdn_A_log, m_gdn_dt_bias, m_gdn_norm_w, m_gdn_w_out, m_final_norm, v_ffn1_norm, v_ffn1_w_gate_up, v_ffn1_w_down, v_mix_norm, v_ffn2_norm, v_ffn2_w_gate_up, v_ffn2_w_down, v_attn_w_in, v_attn_b_in, v_attn_sinks, v_attn_w_out, v_attn_b_out, v_gdn_w_in, v_gdn_conv_w, v_gdn_A_log, v_gdn_dt_bias, v_gdn_norm_w, v_gdn_w_out, v_final_norm):
    given = dict(locals())
    w = {n: given[n] for n in WEIGHTS}
    d = D_MODEL
    h = x[0]
    target = loss_target[0]
    depth = ffn1_norm.shape[0]

    big = [n for n in SHARDED if n != "gdn_conv_w"]
    first = _gather_weights({n: w[n][0:1] for n in ("ffn1_w_gate_up", "ffn1_w_down")}, BF16)
    ffn_order = [(tag, l) for l in range(depth) for tag in ("ffn1", "ffn2")]
    ffn_w = {ffn_order[0]: (first["ffn1_w_gate_up"][0], first["ffn1_w_down"][0])}
    mixers_packed, mixers_unpack = _pack_groups({n: w[n] for n in big if not n.startswith("ffn")}, BF16)
    full = {}

    def run_ffn(tag, l, h):
        i = ffn_order.index((tag, l))
        wgu, wd = ffn_w[(tag, l)]
        nw = w[tag + "_norm"][l][None]
        if i + 1 == len(ffn_order):
            return ffn_fwd(h, nw, wgu, wd)
        ntag, nl = ffn_order[i + 1]
        shards = (w[ntag + "_w_gate_up"][nl].astype(BF16), w[ntag + "_w_down"][nl].astype(BF16))
        h, gu, a, ngu, nd, *gots = ffn_fwd(h, nw, wgu, wd, shards, also=mixers_packed if i == 0 else ())
        ffn_w[(ntag, nl)] = (ngu, nd)
        if i == 0:
            full.update(mixers_unpack([_fill_own_slot(got, x) for got, x in zip(gots, mixers_packed)]))
        return h, gu, a

    conv_full = _gather_weights({"gdn_conv_w": gdn_conv_w}, F32, split=False)["gdn_conv_w"]
    tab = _rope_table(positions[0])
    zero_d = jnp.zeros((1, d), F32)

    def gdn_params(j):
        w_in = full["gdn_w_in"][j]
        w_cat = jnp.concatenate([w_in, jnp.zeros((d, GDN_PROJ_W - w_in.shape[1]), BF16)], axis=1)
        conv = jnp.concatenate([conv_full[j], jnp.zeros((8 - GDN_CONV, GDN_QKV_W), F32)], axis=0)
        lanes = lambda vec: jnp.concatenate([jnp.zeros((GC_G,), F32), vec, jnp.zeros((LANE - GC_G - GDN_HEADS,), F32)])
        par = jnp.concatenate([lanes(gdn_A_log[j])[None], lanes(gdn_dt_bias[j])[None], jnp.zeros((6, LANE), F32)], axis=0)
        return w_cat, conv, par

    saved = []
    for l in range(depth):
        j = l // 2
        rec = {"h1": h}
        h, rec["ffn1_gu"], rec["ffn1_a"] = run_ffn("ffn1", l, h)
        rec["h2"] = h
        if l % 2 == 0:
            p = norm_proj(h, mix_norm[l][None], full["attn_w_in"][j], attn_b_in[j][None])
            o = attn_fwd(p, tab, attn_sinks[j][None])
            h = linear_residual(h, o, full["attn_w_out"][j], attn_b_out[j][None])
            rec.update(p=p, o=o)
        else:
            w_cat, conv, par = gdn_params(j)
            proj = norm_proj(h, mix_norm[l][None], w_cat, jnp.zeros((1, GDN_PROJ_W), F32))
            qkv, gc = gdn_pre_fwd(proj, conv, par)
            on, st, rec["m_offs"] = gdn_chunk_fwd(qkv, gc, proj, gdn_norm_w[j][None])
            h = linear_residual(h, on, full["gdn_w_out"][j], zero_d)
            rec.update(proj=proj, qkv=qkv, gc=gc, on=on, st=st, w_cat=w_cat, conv=conv, par=par)
        rec["h3"] = h
        h, rec["ffn2_gu"], rec["ffn2_a"] = run_ffn("ffn2", l, h)
        saved.append(rec)

    dh, loss_tile, d_final = loss_head(h, final_norm[None], target)

    g = {n: [None] * w[n].shape[0] for n in WEIGHTS if n != "final_norm"}
    reducers = []

    def layer_of(n, i):
        return i if n.startswith("ffn") else (2 * i if n.startswith("attn") else 2 * i + 1)

    def layer_reduce(layer):
        picks = {n: idx for n in big if (idx := [i for i in range(w[n].shape[0]) if layer_of(n, i) == layer])}
        return GradReduce({n: [g[n][i] for i in idx] for n, idx in picks.items()},
                          {n: (len(idx),) + w[n].shape[1:] for n, idx in picks.items()})

    def tn(x, y, scale=1.0):
        big_enough = x.shape[1] * y.shape[1] >= D_MODEL * D_FF
        red = next((r for r in reducers if r.next_job() is not None), None) if big_enough else None
        if red is None:
            return matmul_tn(x, y, scale)
        out, res = matmul_tn(x, y, scale, carry=red.next_job())
        red.deliver(res)
        return out
    for l in reversed(range(depth)):
        j = l // 2
        rec = saved[l]

        def ffn_back(tag, h_in, dh):
            wgu, wd = ffn_w[(tag, l)]
            dh_new, hn, dgu, dout, dn = ffn_bwd(h_in, w[tag + "_norm"][l][None], wgu, wd, rec[tag + "_gu"], dh)
            g[tag + "_w_gate_up"][l] = tn(hn, dgu)
            g[tag + "_w_down"][l] = tn(rec[tag + "_a"], dout)
            g[tag + "_norm"][l] = dn[0]
            return dh_new

        dh = ffn_back("ffn2", rec["h3"], dh)
        if l % 2 == 0:
            w_in, w_out = full["attn_w_in"][j], full["attn_w_out"][j]
            do, db_out = matmul_nt(dh, w_out)
            g["attn_w_out"][j] = matmul_tn(rec["o"], dh)
            g["attn_b_out"][j] = db_out[0]
            dq, dkvc, dkvp, dsink = attn_bwd(rec["p"], tab, attn_sinks[j][None], do)
            dkv = kv_combine(dkvc, dkvp)
            dh, hn, dn, cs_q, cs_kv = norm_proj_bwd(rec["h2"], mix_norm[l][None], dh, [dq, dkv],
                                                    [w_in[:, :ATTN_Q_W], w_in[:, ATTN_Q_W:]])
            g["attn_w_in"][j] = jnp.concatenate([matmul_tn(hn, dq), matmul_tn(hn, dkv)], axis=1)
            g["attn_b_in"][j] = jnp.concatenate([cs_q[0], cs_kv[0]])
            g["attn_sinks"][j] = dsink[0]
        else:
            w_cat, conv, par = rec["w_cat"], rec["conv"], rec["par"]
            d_on, _ = matmul_nt(dh, full["gdn_w_out"][j])
            g["gdn_w_out"][j] = matmul_tn(rec["on"], dh)
            dq, dk, dv, dz, dgc_heads, dnw = gdn_chunk_bwd(rec["qkv"], rec["gc"], rec["proj"], gdn_norm_w[j][None],
                                                           rec["st"], rec["m_offs"], d_on)
            dy, dba, dpar = gdn_pre_bwd(rec["proj"], conv, par, dq, dk, dv, dgc_heads)
            dx, dconv = gdn_conv_bwd(rec["proj"], conv, dy)
            nz = GDN_QKV_W + GDN_HEADS * GDN_DK
            dh, hn, dn, _, _, _ = norm_proj_bwd(rec["h2"], mix_norm[l][None], dh, [dx, dz, dba],
                                                [w_cat[:, :GDN_QKV_W], w_cat[:, GDN_QKV_W:nz], w_cat[:, nz:]])
            g["gdn_w_in"][j] = jnp.concatenate(
                [tn(hn, dx), matmul_tn(hn, dz), matmul_tn(hn, dba)[:, :2 * GDN_HEADS]], axis=1)
            g["gdn_conv_w"][j] = dconv[:GDN_CONV]
            g["gdn_A_log"][j] = dpar[0, GDN_HEADS:2 * GDN_HEADS]
            g["gdn_dt_bias"][j] = dpar[1, GDN_HEADS:2 * GDN_HEADS]
            g["gdn_norm_w"][j] = dnw[0]
        g["mix_norm"][l] = dn[0]
        dh = ffn_back("ffn1", rec["h1"], dh)
        if l > 0:
            reducers.append(layer_reduce(l))
    grad_x = dh[None]

    small_names = REPLICATED + ["gdn_conv_w"]
    local = {n: jnp.stack(g[n]) for n in small_names if n != "final_norm"}
    local["final_norm"] = d_final[0]
    small_shapes = [(1,)] + [local[n].shape for n in small_names]
    small = allreduce_small(_small_pack([loss_tile[0, 0:1]] + [local[n] for n in small_names]))
    small = _small_unpack(small, small_shapes)
    loss = small[0][0]
    grads = dict(zip(small_names, small[1:]))
    conv_cols = gdn_conv_w.shape[2]
    grads["gdn_conv_w"] = lax.dynamic_slice_in_dim(grads["gdn_conv_w"], _chip_index() * conv_cols, conv_cols, axis=2)
    by_layer = [layer_reduce(0).finish()] + [r.finish() for r in reversed(reducers)]
    grads.update({n: jnp.concatenate([part[n] for part in by_layer if n in part], axis=0) for n in big})

    delta, new_m, new_v = {}, {}, {}
    for n in SHARDED:
        dl, nm, nv = adamw(_as2d(w[n]), _as2d(grads[n]), _as2d(given["m_" + n]), _as2d(given["v_" + n]))
        delta[n], new_m[n], new_v[n] = dl.reshape(w[n].shape), nm.reshape(w[n].shape), nv.reshape(w[n].shape)
    shapes = [w[n].shape for n in REPLICATED]
    packed = [_small_pack([src[n] for n in REPLICATED]) for src in
              (w, grads, {n: given["m_" + n] for n in REPLICATED}, {n: given["v_" + n] for n in REPLICATED})]
    for dst, buf in zip((delta, new_m, new_v), adamw(*packed)):
        dst.update(zip(REPLICATED, _small_unpack(buf, shapes)))

    return (loss, grad_x, *[grads[n] for n in WEIGHTS], *[delta[n] for n in WEIGHTS],
            *[new_m[n] for n in WEIGHTS], *[new_v[n] for n in WEIGHTS])
```
